```python
import jax, jax.numpy as jnp
from jax import lax
import numpy as np

D_MODEL = 2048
BATCH = 8
SEQ = 2048
DEPTH = 2

N_MIXERS = 2
N_META = 16
D_FF = 5632
EPS = 1e-6
GLA_HEADS = 4
GLA_DK = D_MODEL // 2
GLA_DV = D_MODEL
GLA_HEAD_K = GLA_DK // GLA_HEADS
GLA_HEAD_V = GLA_DV // GLA_HEADS
GLA_GATE_RANK = 16
GLA_GATE_NORM = 16.0
GLA_CHUNK = 64
GLA_IN_W = GLA_DK + GLA_DK + GLA_DV + GLA_GATE_RANK + GLA_DV
POOL_WINDOWS = (2, 4, 8, 16)
POOL_GROUPS = 4
POOL_GROUP_W = D_MODEL // POOL_GROUPS
N_GLA_LAYERS = (DEPTH + 1) // 2
N_POOL_LAYERS = DEPTH // 2

kernel_name = 'hybrid_gla_pool_macaron'


def rms_norm(x, g):
    xf = x.astype(jnp.float32)
    y = xf * lax.rsqrt(jnp.mean(xf * xf, axis=-1, keepdims=True) + EPS)
    return (y * g.astype(jnp.float32)).astype(x.dtype)


def ffn_half(x, g, w_gate, w_up, w_down):
    h = rms_norm(x, g)
    return x + 0.5 * ((jax.nn.silu(h @ w_gate) * (h @ w_up)) @ w_down)


def gla_chunked(q, k, v, lg):
    B, H, T, dk = q.shape
    dv = v.shape[-1]
    C = GLA_CHUNK
    n = T // C

    def to_chunks(a):
        return jnp.moveaxis(a.reshape(B, H, n, C, a.shape[-1]), 2, 0)

    causal = jnp.tril(jnp.ones((C, C), dtype=bool))

    def step(S, inp):
        qc, kc, vc, gc = inp
        b = jnp.cumsum(gc, axis=2)
        b_last = b[:, :, -1:, :]
        o_inter = jnp.einsum('bhik,bhkv->bhiv', qc * jnp.exp(b), S)
        diff = b[:, :, :, None, :] - b[:, :, None, :, :]
        decay = jnp.exp(jnp.where(causal[:, :, None], diff, -jnp.inf))
        A = jnp.einsum('bhijk,bhjk->bhij', qc[:, :, :, None, :] * decay, kc)
        o = o_inter + jnp.einsum('bhij,bhjv->bhiv', A, vc)
        S = jnp.exp(b_last[:, :, 0, :])[..., None] * S + jnp.einsum('bhjk,bhjv->bhkv', kc * jnp.exp(b_last - b), vc)
        return S, o

    S0 = jnp.zeros((B, H, dk, dv), jnp.float32)
    _, o = lax.scan(step, S0, (to_chunks(q), to_chunks(k), to_chunks(v), to_chunks(lg)))
    return jnp.moveaxis(o, 0, 2).reshape(B, H, T, dv)


def gla_mixer(h, w_in, w_lr, b_lr, head_norm, w_out):
    B, L, _ = h.shape
    proj = h @ w_in
    q, k, v, lr, r = jnp.split(proj, [GLA_DK, 2 * GLA_DK, 2 * GLA_DK + GLA_DV, 2 * GLA_DK + GLA_DV + GLA_GATE_RANK], axis=-1)
    lg = jax.nn.log_sigmoid((lr @ w_lr + b_lr).astype(jnp.float32)) / GLA_GATE_NORM

    def heads(a, d):
        return a.reshape(B, L, GLA_HEADS, d).transpose(0, 2, 1, 3).astype(jnp.float32)

    q = heads(q, GLA_HEAD_K) * (GLA_HEAD_K ** -0.5)
    k = heads(k, GLA_HEAD_K)
    v = heads(v, GLA_HEAD_V)
    lg = heads(lg, GLA_HEAD_K)
    pad = (-N_META) % GLA_CHUNK
    padf = lambda a: jnp.pad(a, ((0, 0), (0, 0), (pad, 0), (0, 0)))
    o = gla_chunked(padf(q), padf(k), padf(v), padf(lg))[:, :, pad:, :]
    o = o * lax.rsqrt(jnp.mean(o * o, axis=-1, keepdims=True) + EPS) * head_norm.astype(jnp.float32)
    o = o.transpose(0, 2, 1, 3).reshape(B, L, GLA_DV).astype(h.dtype)
    return (o * jax.nn.silu(r)) @ w_out


def pool_mixer(h, w, b, scale):
    B, L, D = h.shape
    hf = h.astype(jnp.float32).reshape(B, L, POOL_GROUPS, POOL_GROUP_W)
    cs = jnp.cumsum(hf, axis=1)
    t = jnp.arange(L)
    outs = []
    for g, win in enumerate(POOL_WINDOWS):
        csg = cs[:, :, g]
        prev = jnp.pad(csg, ((0, 0), (win, 0), (0, 0)))[:, :L]
        cnt = jnp.minimum(t + 1, win).astype(jnp.float32)[:, None]
        outs.append((csg - prev) / cnt - hf[:, :, g])
    pooled = jnp.stack(outs, axis=2).astype(h.dtype)
    y = jnp.einsum('blgc,gcd->blgd', pooled, w) + b
    return y.reshape(B, L, D) * scale


def _fwd_setup_inputs(seed: int = 0) -> dict:
    key = jax.random.key(seed)
    ks = jax.random.split(key, 20)
    f32 = jnp.float32
    nrm = lambda k, s, sc: jax.random.normal(k, s, f32) * sc
    return {
        'x': nrm(ks[0], (BATCH, SEQ, D_MODEL), 1.0),
        'meta': nrm(ks[1], (N_META, D_MODEL), 1.0),
        'ffn_norm': 1.0 + nrm(ks[2], (DEPTH, 2, D_MODEL), 0.02),
        'ffn_w_gate': nrm(ks[3], (DEPTH, 2, D_MODEL, D_FF), D_MODEL ** -0.5),
        'ffn_w_up': nrm(ks[4], (DEPTH, 2, D_MODEL, D_FF), D_MODEL ** -0.5),
        'ffn_w_down': nrm(ks[5], (DEPTH, 2, D_FF, D_MODEL), D_FF ** -0.5),
        'gla_norm': 1.0 + nrm(ks[6], (N_GLA_LAYERS, D_MODEL), 0.02),
        'gla_w_in': nrm(ks[7], (N_GLA_LAYERS, D_MODEL, GLA_IN_W), D_MODEL ** -0.5),
        'gla_w_lr': nrm(ks[8], (N_GLA_LAYERS, GLA_GATE_RANK, GLA_DK), GLA_GATE_RANK ** -0.5),
        'gla_b_lr': nrm(ks[9], (N_GLA_LAYERS, GLA_DK), 0.01),
        'gla_head_norm': 1.0 + nrm(ks[10], (N_GLA_LAYERS, GLA_HEAD_V), 0.02),
        'gla_w_out': nrm(ks[11], (N_GLA_LAYERS, GLA_DV, D_MODEL), GLA_DV ** -0.5),
        'pool_norm': 1.0 + nrm(ks[12], (N_POOL_LAYERS, D_MODEL), 0.02),
        'pool_w': nrm(ks[13], (N_POOL_LAYERS, POOL_GROUPS, POOL_GROUP_W, POOL_GROUP_W), POOL_GROUP_W ** -0.5),
        'pool_b': nrm(ks[14], (N_POOL_LAYERS, POOL_GROUPS, POOL_GROUP_W), 0.01),
        'pool_scale': 1.0 + nrm(ks[15], (N_POOL_LAYERS, D_MODEL), 0.02),
        'final_norm': 1.0 + nrm(ks[16], (D_MODEL,), 0.02),
    }


def _fwd_reference(x, meta, ffn_norm, ffn_w_gate, ffn_w_up, ffn_w_down, gla_norm, gla_w_in, gla_w_lr, gla_b_lr,
              gla_head_norm, gla_w_out, pool_norm, pool_w, pool_b, pool_scale, final_norm):
    B = x.shape[0]
    m = jnp.broadcast_to(meta.astype(x.dtype)[None], (B, N_META, D_MODEL))
    x = jnp.concatenate([m, x], axis=1)
    for i in range(DEPTH):
        x = ffn_half(x, ffn_norm[i, 0], ffn_w_gate[i, 0], ffn_w_up[i, 0], ffn_w_down[i, 0])
        j = i // N_MIXERS
        if i % N_MIXERS == 0:
            x = x + gla_mixer(rms_norm(x, gla_norm[j]), gla_w_in[j], gla_w_lr[j], gla_b_lr[j], gla_head_norm[j], gla_w_out[j])
        else:
            x = x + pool_mixer(rms_norm(x, pool_norm[j]), pool_w[j], pool_b[j], pool_scale[j])
        x = ffn_half(x, ffn_norm[i, 1], ffn_w_gate[i, 1], ffn_w_up[i, 1], ffn_w_down[i, 1])
    return rms_norm(x, final_norm)[:, N_META:]


import jax as _jax
import jax.numpy as _jnp

TWIN_FORMAT = 'train_step'
FWD_PARAMS = ['x', 'meta', 'ffn_norm', 'ffn_w_gate', 'ffn_w_up', 'ffn_w_down', 'gla_norm', 'gla_w_in', 'gla_w_lr', 'gla_b_lr', 'gla_head_norm', 'gla_w_out', 'pool_norm', 'pool_w', 'pool_b', 'pool_scale', 'final_norm']
TWIN_WEIGHTS = ['meta', 'ffn_norm', 'ffn_w_gate', 'ffn_w_up', 'ffn_w_down', 'gla_norm', 'gla_w_in', 'gla_w_lr', 'gla_b_lr', 'gla_head_norm', 'gla_w_out', 'pool_norm', 'pool_w', 'pool_b', 'pool_scale', 'final_norm']
TWIN_DIFF_INPUT = 'x'
TWIN_INPUTS = ['x', 'meta', 'ffn_norm', 'ffn_w_gate', 'ffn_w_up', 'ffn_w_down', 'gla_norm', 'gla_w_in', 'gla_w_lr', 'gla_b_lr', 'gla_head_norm', 'gla_w_out', 'pool_norm', 'pool_w', 'pool_b', 'pool_scale', 'final_norm', 'loss_target', 'm_meta', 'm_ffn_norm', 'm_ffn_w_gate', 'm_ffn_w_up', 'm_ffn_w_down', 'm_gla_norm', 'm_gla_w_in', 'm_gla_w_lr', 'm_gla_b_lr', 'm_gla_head_norm', 'm_gla_w_out', 'm_pool_norm', 'm_pool_w', 'm_pool_b', 'm_pool_scale', 'm_final_norm', 'v_meta', 'v_ffn_norm', 'v_ffn_w_gate', 'v_ffn_w_up', 'v_ffn_w_down', 'v_gla_norm', 'v_gla_w_in', 'v_gla_w_lr', 'v_gla_b_lr', 'v_gla_head_norm', 'v_gla_w_out', 'v_pool_norm', 'v_pool_w', 'v_pool_b', 'v_pool_scale', 'v_final_norm']
TWIN_OUTPUTS = ['loss', 'grad_x', 'grad_meta', 'grad_ffn_norm', 'grad_ffn_w_gate', 'grad_ffn_w_up', 'grad_ffn_w_down', 'grad_gla_norm', 'grad_gla_w_in', 'grad_gla_w_lr', 'grad_gla_b_lr', 'grad_gla_head_norm', 'grad_gla_w_out', 'grad_pool_norm', 'grad_pool_w', 'grad_pool_b', 'grad_pool_scale', 'grad_final_norm', 'delta_meta', 'delta_ffn_norm', 'delta_ffn_w_gate', 'delta_ffn_w_up', 'delta_ffn_w_down', 'delta_gla_norm', 'delta_gla_w_in', 'delta_gla_w_lr', 'delta_gla_b_lr', 'delta_gla_head_norm', 'delta_gla_w_out', 'delta_pool_norm', 'delta_pool_w', 'delta_pool_b', 'delta_pool_scale', 'delta_final_norm', 'new_m_meta', 'new_m_ffn_norm', 'new_m_ffn_w_gate', 'new_m_ffn_w_up', 'new_m_ffn_w_down', 'new_m_gla_norm', 'new_m_gla_w_in', 'new_m_gla_w_lr', 'new_m_gla_b_lr', 'new_m_gla_head_norm', 'new_m_gla_w_out', 'new_m_pool_norm', 'new_m_pool_w', 'new_m_pool_b', 'new_m_pool_scale', 'new_m_final_norm', 'new_v_meta', 'new_v_ffn_norm', 'new_v_ffn_w_gate', 'new_v_ffn_w_up', 'new_v_ffn_w_down', 'new_v_gla_norm', 'new_v_gla_w_in', 'new_v_gla_w_lr', 'new_v_gla_b_lr', 'new_v_gla_head_norm', 'new_v_gla_w_out', 'new_v_pool_norm', 'new_v_pool_w', 'new_v_pool_b', 'new_v_pool_scale', 'new_v_final_norm']
TWIN_LEAF_KINDS = {'loss': 'loss', 'grad_x': 'grad_x', 'grad_meta': 'grad_w', 'grad_ffn_norm': 'grad_w', 'grad_ffn_w_gate': 'grad_w', 'grad_ffn_w_up': 'grad_w', 'grad_ffn_w_down': 'grad_w', 'grad_gla_norm': 'grad_w', 'grad_gla_w_in': 'grad_w', 'grad_gla_w_lr': 'grad_w', 'grad_gla_b_lr': 'grad_w', 'grad_gla_head_norm': 'grad_w', 'grad_gla_w_out': 'grad_w', 'grad_pool_norm': 'grad_w', 'grad_pool_w': 'grad_w', 'grad_pool_b': 'grad_w', 'grad_pool_scale': 'grad_w', 'grad_final_norm': 'grad_w', 'delta_meta': 'delta_w', 'delta_ffn_norm': 'delta_w', 'delta_ffn_w_gate': 'delta_w', 'delta_ffn_w_up': 'delta_w', 'delta_ffn_w_down': 'delta_w', 'delta_gla_norm': 'delta_w', 'delta_gla_w_in': 'delta_w', 'delta_gla_w_lr': 'delta_w', 'delta_gla_b_lr': 'delta_w', 'delta_gla_head_norm': 'delta_w', 'delta_gla_w_out': 'delta_w', 'delta_pool_norm': 'delta_w', 'delta_pool_w': 'delta_w', 'delta_pool_b': 'delta_w', 'delta_pool_scale': 'delta_w', 'delta_final_norm': 'delta_w', 'new_m_meta': 'new_m', 'new_m_ffn_norm': 'new_m', 'new_m_ffn_w_gate': 'new_m', 'new_m_ffn_w_up': 'new_m', 'new_m_ffn_w_down': 'new_m', 'new_m_gla_norm': 'new_m', 'new_m_gla_w_in': 'new_m', 'new_m_gla_w_lr': 'new_m', 'new_m_gla_b_lr': 'new_m', 'new_m_gla_head_norm': 'new_m', 'new_m_gla_w_out': 'new_m', 'new_m_pool_norm': 'new_m', 'new_m_pool_w': 'new_m', 'new_m_pool_b': 'new_m', 'new_m_pool_scale': 'new_m', 'new_m_final_norm': 'new_m', 'new_v_meta': 'new_v', 'new_v_ffn_norm': 'new_v', 'new_v_ffn_w_gate': 'new_v', 'new_v_ffn_w_up': 'new_v', 'new_v_ffn_w_down': 'new_v', 'new_v_gla_norm': 'new_v', 'new_v_gla_w_in': 'new_v', 'new_v_gla_w_lr': 'new_v', 'new_v_gla_b_lr': 'new_v', 'new_v_gla_head_norm': 'new_v', 'new_v_gla_w_out': 'new_v', 'new_v_pool_norm': 'new_v', 'new_v_pool_w': 'new_v', 'new_v_pool_b': 'new_v', 'new_v_pool_scale': 'new_v', 'new_v_final_norm': 'new_v'}


def _forward(args):
    return _fwd_reference(*[args[k] for k in FWD_PARAMS])


def _output_shape():
    out = _jax.eval_shape(lambda: _forward(_fwd_setup_inputs(0)))
    return out.shape, out.dtype

N_MICROBATCH = 1
ADAM_LR = 0.001
ADAM_B1 = 0.9
ADAM_B2 = 0.999
ADAM_EPS = 1e-08
ADAM_WD = 0.01
ADAM_STEP = 10
PER_EXAMPLE_BATCH_AXIS = {'x': 0, 'loss_target': 0}
SHARED_INPUTS = []
_WEIGHT_DTYPES = {'meta': _jnp.float32, 'ffn_norm': _jnp.float32, 'ffn_w_gate': _jnp.float32, 'ffn_w_up': _jnp.float32, 'ffn_w_down': _jnp.float32, 'gla_norm': _jnp.float32, 'gla_w_in': _jnp.float32, 'gla_w_lr': _jnp.float32, 'gla_b_lr': _jnp.float32, 'gla_head_norm': _jnp.float32, 'gla_w_out': _jnp.float32, 'pool_norm': _jnp.float32, 'pool_w': _jnp.float32, 'pool_b': _jnp.float32, 'pool_scale': _jnp.float32, 'final_norm': _jnp.float32}
MOMENT_SCALE = {'meta': 2.396920e-03, 'ffn_norm': 2.605465e-02, 'ffn_w_gate': 1.112145e-02, 'ffn_w_up': 1.077077e-02, 'ffn_w_down': 1.787306e-02, 'gla_norm': 7.063244e-02, 'gla_w_in': 3.950644e-02, 'gla_w_lr': 5.442068e-03, 'gla_b_lr': 2.173030e-02, 'gla_head_norm': 6.808603e-02, 'gla_w_out': 3.357267e-02, 'pool_norm': 3.653983e-02, 'pool_w': 3.655167e-02, 'pool_b': 4.186935e-02, 'pool_scale': 8.447334e-02, 'final_norm': 8.022499e+00}


def _to_microbatches(a, axis):
    t = _jnp.moveaxis(a, axis, 0)
    t = t.reshape((N_MICROBATCH, t.shape[0] // N_MICROBATCH) + t.shape[1:])
    return _jnp.moveaxis(t, 1, axis + 1)


def setup_inputs(seed: int = 0) -> dict:
    inp = _fwd_setup_inputs(seed)
    key = _jax.random.fold_in(_jax.random.key(seed), 7919)
    shape, _ = _output_shape()
    out = dict(inp)
    out["loss_target"] = _jax.random.normal(_jax.random.fold_in(key, 0), shape, _jnp.float32)
    for i, name in enumerate(TWIN_WEIGHTS):
        w = inp[name].astype(_jnp.float32)
        if MOMENT_SCALE is None:
            s = _jnp.sqrt(_jnp.mean(_jnp.square(w)) + 1e-30)
        else:
            s = MOMENT_SCALE[name]
        km, kv = _jax.random.split(_jax.random.fold_in(key, i + 1))
        out[name] = w
        out["m_" + name] = s * _jax.random.normal(km, w.shape, _jnp.float32)
        out["v_" + name] = (s * s) * _jax.random.uniform(kv, w.shape, _jnp.float32, 0.5, 1.5)
    if N_MICROBATCH > 1:
        for name, axis in PER_EXAMPLE_BATCH_AXIS.items():
            out[name] = _to_microbatches(out[name], axis)
    return {'x': out['x'], 'meta': out['meta'], 'ffn_norm': out['ffn_norm'], 'ffn_w_gate': out['ffn_w_gate'], 'ffn_w_up': out['ffn_w_up'], 'ffn_w_down': out['ffn_w_down'], 'gla_norm': out['gla_norm'], 'gla_w_in': out['gla_w_in'], 'gla_w_lr': out['gla_w_lr'], 'gla_b_lr': out['gla_b_lr'], 'gla_head_norm': out['gla_head_norm'], 'gla_w_out': out['gla_w_out'], 'pool_norm': out['pool_norm'], 'pool_w': out['pool_w'], 'pool_b': out['pool_b'], 'pool_scale': out['pool_scale'], 'final_norm': out['final_norm'], 'loss_target': out['loss_target'], 'm_meta': out['m_meta'], 'm_ffn_norm': out['m_ffn_norm'], 'm_ffn_w_gate': out['m_ffn_w_gate'], 'm_ffn_w_up': out['m_ffn_w_up'], 'm_ffn_w_down': out['m_ffn_w_down'], 'm_gla_norm': out['m_gla_norm'], 'm_gla_w_in': out['m_gla_w_in'], 'm_gla_w_lr': out['m_gla_w_lr'], 'm_gla_b_lr': out['m_gla_b_lr'], 'm_gla_head_norm': out['m_gla_head_norm'], 'm_gla_w_out': out['m_gla_w_out'], 'm_pool_norm': out['m_pool_norm'], 'm_pool_w': out['m_pool_w'], 'm_pool_b': out['m_pool_b'], 'm_pool_scale': out['m_pool_scale'], 'm_final_norm': out['m_final_norm'], 'v_meta': out['v_meta'], 'v_ffn_norm': out['v_ffn_norm'], 'v_ffn_w_gate': out['v_ffn_w_gate'], 'v_ffn_w_up': out['v_ffn_w_up'], 'v_ffn_w_down': out['v_ffn_w_down'], 'v_gla_norm': out['v_gla_norm'], 'v_gla_w_in': out['v_gla_w_in'], 'v_gla_w_lr': out['v_gla_w_lr'], 'v_gla_b_lr': out['v_gla_b_lr'], 'v_gla_head_norm': out['v_gla_head_norm'], 'v_gla_w_out': out['v_gla_w_out'], 'v_pool_norm': out['v_pool_norm'], 'v_pool_w': out['v_pool_w'], 'v_pool_b': out['v_pool_b'], 'v_pool_scale': out['v_pool_scale'], 'v_final_norm': out['v_final_norm']}


def _loss(weights, diff, rest, loss_target):
    with _jax.named_scope("forward"):
        args = {**rest, TWIN_DIFF_INPUT: diff, **{k: w.astype(_WEIGHT_DTYPES[k]) for k, w in weights.items()}}
        y = _forward(args)
    with _jax.named_scope("loss_head"):
        err = _jnp.square(y.astype(_jnp.float32) - loss_target)
        return 0.5 * _jnp.sum(_jnp.mean(err, axis=-1)) if err.ndim else 0.5 * err


def _adamw(w, g, m, v):
    m = ADAM_B1 * m + (1.0 - ADAM_B1) * g
    v = ADAM_B2 * v + (1.0 - ADAM_B2) * _jnp.square(g)
    m_hat = m / (1.0 - ADAM_B1 ** ADAM_STEP)
    v_hat = v / (1.0 - ADAM_B2 ** ADAM_STEP)
    delta = -ADAM_LR * (m_hat / (_jnp.sqrt(v_hat) + ADAM_EPS) + ADAM_WD * w)
    return delta, m, v


def reference(x, meta, ffn_norm, ffn_w_gate, ffn_w_up, ffn_w_down, gla_norm, gla_w_in, gla_w_lr, gla_b_lr, gla_head_norm, gla_w_out, pool_norm, pool_w, pool_b, pool_scale, final_norm, loss_target, m_meta, m_ffn_norm, m_ffn_w_gate, m_ffn_w_up, m_ffn_w_down, m_gla_norm, m_gla_w_in, m_gla_w_lr, m_gla_b_lr, m_gla_head_norm, m_gla_w_out, m_pool_norm, m_pool_w, m_pool_b, m_pool_scale, m_final_norm, v_meta, v_ffn_norm, v_ffn_w_gate, v_ffn_w_up, v_ffn_w_down, v_gla_norm, v_gla_w_in, v_gla_w_lr, v_gla_b_lr, v_gla_head_norm, v_gla_w_out, v_pool_norm, v_pool_w, v_pool_b, v_pool_scale, v_final_norm):
    given = dict(x=x, meta=meta, ffn_norm=ffn_norm, ffn_w_gate=ffn_w_gate, ffn_w_up=ffn_w_up, ffn_w_down=ffn_w_down, gla_norm=gla_norm, gla_w_in=gla_w_in, gla_w_lr=gla_w_lr, gla_b_lr=gla_b_lr, gla_head_norm=gla_head_norm, gla_w_out=gla_w_out, pool_norm=pool_norm, pool_w=pool_w, pool_b=pool_b, pool_scale=pool_scale, final_norm=final_norm, loss_target=loss_target, m_meta=m_meta, m_ffn_norm=m_ffn_norm, m_ffn_w_gate=m_ffn_w_gate, m_ffn_w_up=m_ffn_w_up, m_ffn_w_down=m_ffn_w_down, m_gla_norm=m_gla_norm, m_gla_w_in=m_gla_w_in, m_gla_w_lr=m_gla_w_lr, m_gla_b_lr=m_gla_b_lr, m_gla_head_norm=m_gla_head_norm, m_gla_w_out=m_gla_w_out, m_pool_norm=m_pool_norm, m_pool_w=m_pool_w, m_pool_b=m_pool_b, m_pool_scale=m_pool_scale, m_final_norm=m_final_norm, v_meta=v_meta, v_ffn_norm=v_ffn_norm, v_ffn_w_gate=v_ffn_w_gate, v_ffn_w_up=v_ffn_w_up, v_ffn_w_down=v_ffn_w_down, v_gla_norm=v_gla_norm, v_gla_w_in=v_gla_w_in, v_gla_w_lr=v_gla_w_lr, v_gla_b_lr=v_gla_b_lr, v_gla_head_norm=v_gla_head_norm, v_gla_w_out=v_gla_w_out, v_pool_norm=v_pool_norm, v_pool_w=v_pool_w, v_pool_b=v_pool_b, v_pool_scale=v_pool_scale, v_final_norm=v_final_norm)
    weights = {n: given[n] for n in TWIN_WEIGHTS}
    shared = {n: given[n] for n in SHARED_INPUTS}
    per_example = {n: given[n] for n in ['x']}
    grad_fn = _jax.value_and_grad(_loss, argnums=(0, 1))

    def one_microbatch(ex, loss_target):
        ex = dict(ex)
        diff = ex.pop(TWIN_DIFF_INPUT)
        return grad_fn(weights, diff, {**shared, **ex}, loss_target)

    if N_MICROBATCH == 1:
        loss, (grad_w, grad_x) = one_microbatch(per_example, given["loss_target"])
    else:
        def body(carry, xs):
            loss_sum, grad_sum = carry
            l_k, (gw_k, gx_k) = one_microbatch(xs[0], xs[1])
            with _jax.named_scope("update"):
                return (loss_sum + l_k, _jax.tree.map(_jnp.add, grad_sum, gw_k)), gx_k

        init = (_jnp.zeros((), _jnp.float32), _jax.tree.map(_jnp.zeros_like, weights))
        (loss, grad_w), grad_x = _jax.lax.scan(body, init, (per_example, given["loss_target"]))
    with _jax.named_scope("update"):
        delta_w, new_m, new_v = {}, {}, {}
        for n in TWIN_WEIGHTS:
            delta_w[n], new_m[n], new_v[n] = _adamw(weights[n], grad_w[n], given["m_" + n], given["v_" + n])
    return (loss, grad_x, *[grad_w[n] for n in TWIN_WEIGHTS], *[delta_w[n] for n in TWIN_WEIGHTS],
            *[new_m[n] for n in TWIN_WEIGHTS], *[new_v[n] for n in TWIN_WEIGHTS])
```

```python
import functools

import jax
import jax.numpy as jnp
from jax import lax
from jax.experimental import pallas as pl
from jax.experimental.pallas import tpu as pltpu

f32 = jnp.float32
bf16 = jnp.bfloat16

N_DEV = 8
N_META = 16
GLA_HEADS = 4
GLA_CHUNK = 64
GLA_SUB = 16
GATE_RANK = 16
GATE_PAD = 128
GATE_NORM = 16.0
EPS = 1e-6
POOL_GROUPS = 4
ADAM_LR = 0.001
ADAM_B1 = 0.9
ADAM_B2 = 0.999
ADAM_EPS = 1e-08
ADAM_WD = 0.01
ADAM_STEP = 10
LANES = 128
VMEM_LIMIT_MB = 56

NN = (((1,), (0,)), ((), ()))
NT = (((1,), (1,)), ((), ()))
TN = (((0,), (0,)), ((), ()))
HI = lax.Precision.HIGHEST
MESH = pl.DeviceIdType.MESH
ANY = pl.BlockSpec(memory_space=pl.ANY)


def _cparams(sem=None, vmem_mb=None):
    kw = {}
    if sem is not None:
        kw["dimension_semantics"] = sem
    if vmem_mb is not None:
        kw["vmem_limit_bytes"] = vmem_mb * 2 ** 20
    return pltpu.CompilerParams(**kw)


def _tile(n, target, mult=16):
    best = None
    for t in range(mult, min(n, target) + 1, mult):
        if n % t == 0:
            best = t
    assert best is not None, (n, target, mult)
    return best


def _dot(a, b, dims=NN, precision=None):
    return lax.dot_general(a, b, dims, preferred_element_type=f32, precision=precision)


def _sigmoid(x):
    return 1.0 / (1.0 + jnp.exp(-x))


def _row_ids(tile_index, tm):
    return tile_index * tm + lax.broadcasted_iota(jnp.int32, (tm, 1), 0)


def _rms_fwd(xs, g, out_dtype, name):
    Lp, D = xs.shape
    tm = _tile(Lp, 528)

    def body(x_ref, g_ref, h_ref):
        x = x_ref[...]
        rstd = lax.rsqrt(jnp.mean(x * x, axis=-1, keepdims=True) + EPS)
        h_ref[...] = (x * rstd * g_ref[...]).astype(out_dtype)

    return pl.pallas_call(
        body, name=name, grid=(Lp // tm,),
        in_specs=[pl.BlockSpec((tm, D), lambda i: (i, 0)), pl.BlockSpec((1, D), lambda i: (0, 0))],
        out_specs=pl.BlockSpec((tm, D), lambda i: (i, 0)),
        out_shape=jax.ShapeDtypeStruct((Lp, D), out_dtype),
        compiler_params=_cparams(("parallel",)),
    )(xs, g)


def _rms_bwd(dY, dh, xs, g, pad, name):
    Lp, D = xs.shape
    tm = _tile(Lp, 352)

    def body(dY_ref, dh_ref, x_ref, g_ref, dxs_ref, dg_ref):
        i = pl.program_id(0)

        @pl.when(i == 0)
        def _():
            dg_ref[...] = jnp.zeros_like(dg_ref)

        x = x_ref[...]
        rstd = lax.rsqrt(jnp.mean(x * x, axis=-1, keepdims=True) + EPS)
        xhat = x * rstd
        dh_ = dh_ref[...]
        dg_ref[...] += jnp.sum(dh_ * xhat, axis=0, keepdims=True)
        dxh = dh_ * g_ref[...]
        dx = rstd * (dxh - xhat * jnp.mean(dxh * xhat, axis=-1, keepdims=True))
        dxs_ref[...] = jnp.where(_row_ids(i, tm) >= pad, dY_ref[...] + dx, 0.0)

    row = pl.BlockSpec((tm, D), lambda i: (i, 0))
    vec = pl.BlockSpec((1, D), lambda i: (0, 0))
    return pl.pallas_call(
        body, name=name, grid=(Lp // tm,),
        in_specs=[row, row, row, vec], out_specs=[row, vec],
        out_shape=[jax.ShapeDtypeStruct((Lp, D), f32), jax.ShapeDtypeStruct((1, D), f32)],
        compiler_params=_cparams(("arbitrary",)),
    )(dY, dh, xs, g)


def _mm(a, b, mode, out_dtype, name, tm=512, tn=512, tk=512, residual=None):
    if mode == "nn":
        (M, K), N = a.shape, b.shape[1]
    elif mode == "nt":
        (M, K), N = a.shape, b.shape[0]
    else:
        (K, M), N = a.shape, b.shape[1]
    tm = _tile(M, tm, 16 if mode != "tn" else LANES) if M > tm else M
    tn = _tile(N, tn, LANES) if N > tn else N
    tk = _tile(K, tk, LANES if mode != "tn" else 16) if K > tk else K
    nk = K // tk
    dims = {"nn": NN, "nt": NT, "tn": TN}[mode]

    def body(*refs):
        if residual is None:
            a_ref, b_ref, o_ref, acc = refs
            r_ref = None
        else:
            a_ref, b_ref, r_ref, o_ref, acc = refs
        k = pl.program_id(2)

        @pl.when(k == 0)
        def _():
            acc[...] = jnp.zeros_like(acc)

        acc[...] += _dot(a_ref[...], b_ref[...], dims)

        @pl.when(k == nk - 1)
        def _():
            r = acc[...]
            if r_ref is not None:
                r = r + r_ref[...]
            o_ref[...] = r.astype(out_dtype)

    a_spec = pl.BlockSpec((tk, tm), lambda i, j, k: (k, i)) if mode == "tn" else pl.BlockSpec((tm, tk), lambda i, j, k: (i, k))
    b_spec = pl.BlockSpec((tn, tk), lambda i, j, k: (j, k)) if mode == "nt" else pl.BlockSpec((tk, tn), lambda i, j, k: (k, j))
    o_spec = pl.BlockSpec((tm, tn), lambda i, j, k: (i, j))
    in_specs = [a_spec, b_spec] + ([o_spec] if residual is not None else [])
    args = (a, b) + ((residual,) if residual is not None else ())
    return pl.pallas_call(
        body, name=name, grid=(M // tm, N // tn, nk),
        in_specs=in_specs, out_specs=o_spec,
        out_shape=jax.ShapeDtypeStruct((M, N), out_dtype),
        scratch_shapes=[pltpu.VMEM((tm, tn), f32)],
        compiler_params=_cparams(("parallel", "parallel", "arbitrary"), VMEM_LIMIT_MB),
    )(*args)


def _ffn_fwd(xs, g, wg, wu, wd, name):
    Lp, D = xs.shape
    nd, _, Fs = wg.shape
    tm = _tile(Lp, 704)
    once = pl.Buffered(1)

    def body(x_ref, g_ref, wg_ref, wu_ref, wd_ref, out_ref, h_ref, G_ref, U_ref, hs, acc):
        j = pl.program_id(1)

        @pl.when(j == 0)
        def _():
            x = x_ref[...]
            rstd = lax.rsqrt(jnp.mean(x * x, axis=-1, keepdims=True) + EPS)
            h = (x * rstd * g_ref[...]).astype(bf16)
            hs[...] = h
            h_ref[...] = h
            acc[...] = jnp.zeros_like(acc)

        h = hs[...]
        G = _dot(h, wg_ref[0])
        U = _dot(h, wu_ref[0])
        G_ref[0] = G.astype(bf16)
        U_ref[0] = U.astype(bf16)
        A = (G * _sigmoid(G) * U).astype(bf16)
        acc[...] += _dot(A, wd_ref[0])

        @pl.when(j == nd - 1)
        def _():
            out_ref[...] = x_ref[...] + 0.5 * acc[...]

    row_f = pl.BlockSpec((tm, D), lambda i, j: (i, 0), pipeline_mode=once)
    act = pl.BlockSpec((1, tm, Fs), lambda i, j: (j, i, 0))
    return pl.pallas_call(
        body, name=name, grid=(Lp // tm, nd),
        in_specs=[row_f, pl.BlockSpec((1, D), lambda i, j: (0, 0)),
                  pl.BlockSpec((1, D, Fs), lambda i, j: (j, 0, 0)),
                  pl.BlockSpec((1, D, Fs), lambda i, j: (j, 0, 0)),
                  pl.BlockSpec((1, Fs, D), lambda i, j: (j, 0, 0))],
        out_specs=[row_f, pl.BlockSpec((tm, D), lambda i, j: (i, 0), pipeline_mode=once), act, act],
        out_shape=[jax.ShapeDtypeStruct((Lp, D), f32), jax.ShapeDtypeStruct((Lp, D), bf16),
                   jax.ShapeDtypeStruct((nd, Lp, Fs), bf16), jax.ShapeDtypeStruct((nd, Lp, Fs), bf16)],
        scratch_shapes=[pltpu.VMEM((tm, D), bf16), pltpu.VMEM((tm, D), f32)],
        compiler_params=_cparams(("parallel", "arbitrary"), VMEM_LIMIT_MB),
    )(xs, g, wg, wu, wd)


def _ffn_bwd_dgrad(dY, xs, g, wg, wu, wd, G, U, pad, name):
    Lp, D = xs.shape
    nd, _, Fs = wg.shape
    tm = _tile(Lp, 352)
    once = pl.Buffered(1)

    def body(dY_ref, x_ref, g_ref, wg_ref, wu_ref, wd_ref, G_ref, U_ref,
             dxs_ref, dyh_ref, dG_ref, dU_ref, A_ref, dg_ref, dyh_s, acc):
        i = pl.program_id(0)
        j = pl.program_id(1)

        @pl.when(j == 0)
        def _():
            d = (0.5 * dY_ref[...]).astype(bf16)
            dyh_s[...] = d
            dyh_ref[...] = d
            acc[...] = jnp.zeros_like(acc)

        @pl.when((i == 0) & (j == 0))
        def _():
            dg_ref[...] = jnp.zeros_like(dg_ref)

        dA = _dot(dyh_s[...], wd_ref[0], NT)
        Gf = G_ref[0].astype(f32)
        Uf = U_ref[0].astype(f32)
        s = _sigmoid(Gf)
        silu = Gf * s
        dGb = (dA * Uf * (s * (1.0 + Gf * (1.0 - s)))).astype(bf16)
        dUb = (dA * silu).astype(bf16)
        dG_ref[0] = dGb
        dU_ref[0] = dUb
        A_ref[0] = (silu * Uf).astype(bf16)
        acc[...] += _dot(dGb, wg_ref[0], NT) + _dot(dUb, wu_ref[0], NT)

        @pl.when(j == nd - 1)
        def _():
            x = x_ref[...]
            rstd = lax.rsqrt(jnp.mean(x * x, axis=-1, keepdims=True) + EPS)
            xhat = x * rstd
            dh = acc[...]
            dg_ref[...] += jnp.sum(dh * xhat, axis=0, keepdims=True)
            dxh = dh * g_ref[...]
            dx = rstd * (dxh - xhat * jnp.mean(dxh * xhat, axis=-1, keepdims=True))
            dxs_ref[...] = jnp.where(_row_ids(i, tm) >= pad, dY_ref[...] + dx, 0.0)

    row_f = pl.BlockSpec((tm, D), lambda i, j: (i, 0), pipeline_mode=once)
    vec = pl.BlockSpec((1, D), lambda i, j: (0, 0))
    wcol = pl.BlockSpec((1, D, Fs), lambda i, j: (j, 0, 0))
    act = pl.BlockSpec((1, tm, Fs), lambda i, j: (j, i, 0))
    act_s = jax.ShapeDtypeStruct((nd, Lp, Fs), bf16)
    return pl.pallas_call(
        body, name=name, grid=(Lp // tm, nd),
        in_specs=[row_f, row_f, vec, wcol, wcol, pl.BlockSpec((1, Fs, D), lambda i, j: (j, 0, 0)), act, act],
        out_specs=[row_f, pl.BlockSpec((tm, D), lambda i, j: (i, 0), pipeline_mode=once), act, act, act, vec],
        out_shape=[jax.ShapeDtypeStruct((Lp, D), f32), jax.ShapeDtypeStruct((Lp, D), bf16), act_s, act_s, act_s,
                   jax.ShapeDtypeStruct((1, D), f32)],
        scratch_shapes=[pltpu.VMEM((tm, D), bf16), pltpu.VMEM((tm, D), f32)],
        compiler_params=_cparams(("arbitrary", "arbitrary"), VMEM_LIMIT_MB),
    )(dY, xs, g, wg, wu, wd, G, U)


def _ffn_bwd_wgrad(h, dyh, A, dG, dU, name):
    Lp, D = h.shape
    nd, _, Fs = A.shape
    tk = _tile(Lp, 528)
    nk = Lp // tk

    def body(h_ref, dyh_ref, A_ref, dG_ref, dU_ref, dwg_ref, dwu_ref, dwd_ref, ag, au, ad):
        i = pl.program_id(1)

        @pl.when(i == 0)
        def _():
            ag[...] = jnp.zeros_like(ag)
            au[...] = jnp.zeros_like(au)
            ad[...] = jnp.zeros_like(ad)

        hh = h_ref[...]
        ag[...] += _dot(hh, dG_ref[0], TN)
        au[...] += _dot(hh, dU_ref[0], TN)
        ad[...] += _dot(A_ref[0], dyh_ref[...], TN)

        @pl.when(i == nk - 1)
        def _():
            dwg_ref[0] = ag[...].astype(bf16)
            dwu_ref[0] = au[...].astype(bf16)
            dwd_ref[0] = ad[...].astype(bf16)

    row = pl.BlockSpec((tk, D), lambda j, i: (i, 0))
    act = pl.BlockSpec((1, tk, Fs), lambda j, i: (j, i, 0))
    wcol = pl.BlockSpec((1, D, Fs), lambda j, i: (j, 0, 0))
    wrow = pl.BlockSpec((1, Fs, D), lambda j, i: (j, 0, 0))
    return pl.pallas_call(
        body, name=name, grid=(nd, nk),
        in_specs=[row, row, act, act, act], out_specs=[wcol, wcol, wrow],
        out_shape=[jax.ShapeDtypeStruct((nd, D, Fs), bf16), jax.ShapeDtypeStruct((nd, D, Fs), bf16),
                   jax.ShapeDtypeStruct((nd, Fs, D), bf16)],
        scratch_shapes=[pltpu.VMEM((D, Fs), f32), pltpu.VMEM((D, Fs), f32), pltpu.VMEM((Fs, D), f32)],
        compiler_params=_cparams(("parallel", "arbitrary"), VMEM_LIMIT_MB),
    )(h, dyh, A, dG, dU)


def _gate_fwd(proj, wlr, blr, pad, gate_blk, name):
    Lp = proj.shape[0]
    DK = wlr.shape[1]
    tm = _tile(Lp, 528)

    def body(lr_ref, w_ref, b_ref, lg_ref):
        z = _dot(lr_ref[...].astype(bf16), w_ref[...].astype(bf16)) + b_ref[...]
        ls = jnp.minimum(z, 0.0) - jnp.log(1.0 + jnp.exp(-jnp.abs(z)))
        lg_ref[...] = jnp.where(_row_ids(pl.program_id(0), tm) >= pad, ls * (1.0 / GATE_NORM), 0.0)

    return pl.pallas_call(
        body, name=name, grid=(Lp // tm,),
        in_specs=[pl.BlockSpec((tm, GATE_PAD), lambda i: (i, gate_blk)),
                  pl.BlockSpec((GATE_PAD, DK), lambda i: (0, 0)), pl.BlockSpec((1, DK), lambda i: (0, 0))],
        out_specs=pl.BlockSpec((tm, DK), lambda i: (i, 0)),
        out_shape=jax.ShapeDtypeStruct((Lp, DK), f32),
        compiler_params=_cparams(("parallel",)),
    )(proj, wlr, blr)


def _gate_bwd(dlg, proj, wlr, blr, pad, gate_blk, name):
    Lp = proj.shape[0]
    DK = wlr.shape[1]
    tm = _tile(Lp, 528)

    def body(dlg_ref, lr_ref, w_ref, b_ref, dlr_ref, dw_ref, db_ref):
        i = pl.program_id(0)

        @pl.when(i == 0)
        def _():
            dw_ref[...] = jnp.zeros_like(dw_ref)
            db_ref[...] = jnp.zeros_like(db_ref)

        lr = lr_ref[...].astype(bf16)
        w = w_ref[...].astype(bf16)
        z = _dot(lr, w) + b_ref[...]
        dz = jnp.where(_row_ids(i, tm) >= pad, dlg_ref[...] * _sigmoid(-z) * (1.0 / GATE_NORM), 0.0)
        dzb = dz.astype(bf16)
        dlr_ref[...] = _dot(dzb, w, NT).astype(bf16)
        dw_ref[...] += _dot(lr, dzb, TN)
        db_ref[...] += jnp.sum(dz, axis=0, keepdims=True)

    return pl.pallas_call(
        body, name=name, grid=(Lp // tm,),
        in_specs=[pl.BlockSpec((tm, DK), lambda i: (i, 0)), pl.BlockSpec((tm, GATE_PAD), lambda i: (i, gate_blk)),
                  pl.BlockSpec((GATE_PAD, DK), lambda i: (0, 0)), pl.BlockSpec((1, DK), lambda i: (0, 0))],
        out_specs=[pl.BlockSpec((tm, GATE_PAD), lambda i: (i, 0)), pl.BlockSpec((GATE_PAD, DK), lambda i: (0, 0)),
                   pl.BlockSpec((1, DK), lambda i: (0, 0))],
        out_shape=[jax.ShapeDtypeStruct((Lp, GATE_PAD), bf16), jax.ShapeDtypeStruct((GATE_PAD, DK), f32),
                   jax.ShapeDtypeStruct((1, DK), f32)],
        compiler_params=_cparams(("arbitrary",)),
    )(dlg, proj, wlr, blr)


def _chunk_decay(lg):
    C = lg.shape[0]
    r = lax.broadcasted_iota(jnp.int32, (C, C), 0)
    c = lax.broadcasted_iota(jnp.int32, (C, C), 1)
    return _dot(jnp.where(r >= c, 1.0, 0.0).astype(f32), lg, NN, HI)


def _col(v):
    return jnp.transpose(jnp.broadcast_to(v, (8, v.shape[1])))[:, 0:1]


def _intra_scores(q, k, b, A_ref):
    C = q.shape[0]
    S = GLA_SUB
    A_ref[...] = jnp.zeros_like(A_ref)
    ri = lax.broadcasted_iota(jnp.int32, (S, 1), 0)
    for I in range(C // S):
        lo = S * I
        qI, bI = q[lo:lo + S], b[lo:lo + S]
        if I > 0:
            bref = b[lo - 1:lo]
            qs = qI * jnp.exp(bI - bref)
            ks = k[:lo] * jnp.exp(bref - b[:lo])
            A_ref[lo:lo + S, 0:lo] = _dot(qs, ks, NT, HI)
        for jj in range(S):
            j = lo + jj
            P = jnp.exp(jnp.minimum(bI - b[j:j + 1], 0.0))
            a = jnp.sum(qI * P * k[j:j + 1], axis=1, keepdims=True)
            A_ref[lo:lo + S, j:j + 1] = jnp.where(ri >= jj, a, 0.0)


def _intra_grads(q, k, b, dA, dq_ref, dk_ref):
    C = q.shape[0]
    S = GLA_SUB
    ri = lax.broadcasted_iota(jnp.int32, (S, 1), 0)
    for I in range(C // S):
        lo = S * I
        qI, bI = q[lo:lo + S], b[lo:lo + S]
        dqI = jnp.zeros_like(qI)
        if I > 0:
            bref = b[lo - 1:lo]
            eq = jnp.exp(bI - bref)
            ek = jnp.exp(bref - b[:lo])
            qs = qI * eq
            ks = k[:lo] * ek
            dAI = dA[lo:lo + S, 0:lo]
            dqI = dqI + _dot(dAI, ks, NN, HI) * eq
            dk_ref[0:lo, :] += _dot(dAI, qs, TN, HI) * ek
        for jj in range(S):
            j = lo + jj
            P = jnp.exp(jnp.minimum(bI - b[j:j + 1], 0.0))
            t = jnp.where(ri >= jj, dA[lo:lo + S, j:j + 1], 0.0) * P
            dqI = dqI + t * k[j:j + 1]
            dk_ref[j:j + 1, :] += jnp.sum(t * qI, axis=0, keepdims=True)
        dq_ref[lo:lo + S, :] += dqI


def _gla_fwd(proj, lg, hnw, H, name):
    Lp = proj.shape[0]
    DK = lg.shape[1]
    hk = DK // H
    hv = hnw.shape[1]
    DV = hv * H
    C = GLA_CHUNK
    NC = Lp // C
    scale = float(hk) ** -0.5
    kq, kv = DK // hk, (2 * DK) // hv
    kr = kv + H

    def body(q_ref, k_ref, v_ref, r_ref, lg_ref, w_ref, o_ref, y_ref, s_ref, S_scr, A_scr):
        c = pl.program_id(1)

        @pl.when(c == 0)
        def _():
            S_scr[...] = jnp.zeros_like(S_scr)

        q = q_ref[...] * scale
        k = k_ref[...]
        v = v_ref[...]
        b = _chunk_decay(lg_ref[...])
        bl = b[C - 1:C]
        S = S_scr[...]
        s_ref[0, 0] = S
        _intra_scores(q, k, b, A_scr)
        vb = v.astype(bf16)
        o = _dot((q * jnp.exp(b)).astype(bf16), S.astype(bf16)) + _dot(A_scr[...].astype(bf16), vb)
        kb = (k * jnp.exp(bl - b)).astype(bf16)
        S_scr[...] = jnp.exp(_col(bl)) * S + _dot(kb, vb, TN)
        o_ref[...] = o
        on = o * lax.rsqrt(jnp.mean(o * o, axis=-1, keepdims=True) + EPS) * w_ref[...]
        r = r_ref[...]
        y_ref[...] = (on * (r * _sigmoid(r))).astype(bf16)

    return pl.pallas_call(
        body, name=name, grid=(H, NC),
        in_specs=[pl.BlockSpec((C, hk), lambda h, c: (c, h)),
                  pl.BlockSpec((C, hk), lambda h, c: (c, kq + h)),
                  pl.BlockSpec((C, hv), lambda h, c: (c, kv + h)),
                  pl.BlockSpec((C, hv), lambda h, c: (c, kr + h)),
                  pl.BlockSpec((C, hk), lambda h, c: (c, h)),
                  pl.BlockSpec((1, hv), lambda h, c: (0, 0))],
        out_specs=[pl.BlockSpec((C, hv), lambda h, c: (c, h)), pl.BlockSpec((C, hv), lambda h, c: (c, h)),
                   pl.BlockSpec((1, 1, hk, hv), lambda h, c: (h, c, 0, 0))],
        out_shape=[jax.ShapeDtypeStruct((Lp, DV), f32), jax.ShapeDtypeStruct((Lp, DV), bf16),
                   jax.ShapeDtypeStruct((H, NC, hk, hv), f32)],
        scratch_shapes=[pltpu.VMEM((hk, hv), f32), pltpu.VMEM((C, C), f32)],
        compiler_params=_cparams(("parallel", "arbitrary")),
    )(proj, proj, proj, proj, lg, hnw)


def _gla_bwd(dy, proj, lg, o, states, hnw, H, pad, name):
    Lp = proj.shape[0]
    DK = lg.shape[1]
    hk = DK // H
    hv = hnw.shape[1]
    DV = hv * H
    C = GLA_CHUNK
    NC = Lp // C
    scale = float(hk) ** -0.5
    kq, kv = DK // hk, (2 * DK) // hv
    kr = kv + H

    def body(dy_ref, q_ref, k_ref, v_ref, r_ref, lg_ref, o_ref, s_ref, sn_ref, w_ref,
             dq_ref, dk_ref, dv_ref, dr_ref, dlg_ref, dw_ref, dS_scr, A_scr, dq_s, dk_s):
        h = pl.program_id(0)
        cc = pl.program_id(1)
        c = NC - 1 - cc

        @pl.when(cc == 0)
        def _():
            dS_scr[...] = jnp.zeros_like(dS_scr)

        @pl.when((cc == 0) & (h == 0))
        def _():
            dw_ref[...] = jnp.zeros_like(dw_ref)

        keep = (c * C + lax.broadcasted_iota(jnp.int32, (C, 1), 0)) >= pad
        w = w_ref[...]
        o_ = o_ref[...]
        rs = lax.rsqrt(jnp.mean(o_ * o_, axis=-1, keepdims=True) + EPS)
        ohat = o_ * rs
        r = r_ref[...]
        sg = _sigmoid(r)
        dy_ = dy_ref[...]
        d_on = dy_ * (r * sg)
        dr_ref[...] = jnp.where(keep, dy_ * (ohat * w) * (sg * (1.0 + r * (1.0 - sg))), 0.0).astype(bf16)
        dw_ref[...] += jnp.sum(d_on * ohat, axis=0, keepdims=True)
        d_oh = d_on * w
        do = rs * (d_oh - ohat * jnp.mean(d_oh * ohat, axis=-1, keepdims=True))
        dob = do.astype(bf16)
        q = q_ref[...] * scale
        k = k_ref[...]
        v = v_ref[...]
        vb = v.astype(bf16)
        b = _chunk_decay(lg_ref[...])
        bl = b[C - 1:C]
        eb = jnp.exp(b)
        ekb = jnp.exp(bl - b)
        S = s_ref[0, 0]
        dS = dS_scr[...]
        dSb = dS.astype(bf16)
        _intra_scores(q, k, b, A_scr)
        ri = lax.broadcasted_iota(jnp.int32, (C, C), 0)
        ci = lax.broadcasted_iota(jnp.int32, (C, C), 1)
        dA = jnp.where(ri >= ci, _dot(dob, vb, NT), 0.0)
        kb = (k * ekb).astype(bf16)
        qb = (q * eb).astype(bf16)
        dv = _dot(A_scr[...].astype(bf16), dob, TN) + _dot(kb, dSb)
        dq_s[...] = _dot(dob, S.astype(bf16), NT) * eb
        dk_s[...] = _dot(vb, dSb, NT) * ekb
        dS_scr[...] = _dot(qb, dob, TN) + jnp.exp(_col(bl)) * dS
        _intra_grads(q, k, b, dA, dq_s, dk_s)
        dq = dq_s[...]
        dk = dk_s[...]
        Dm = q * dq - k * dk
        after = _dot(jnp.ones((8, hv), f32), sn_ref[0, 0] * dS, NT, HI)[0:1]
        dlg = _dot(jnp.where(ri <= ci, 1.0, 0.0).astype(f32), Dm, NN, HI) + after
        dlg_ref[...] = jnp.where(keep, dlg, 0.0)
        dq_ref[...] = jnp.where(keep, dq * scale, 0.0).astype(bf16)
        dk_ref[...] = jnp.where(keep, dk, 0.0).astype(bf16)
        dv_ref[...] = jnp.where(keep, dv, 0.0).astype(bf16)

    rev = lambda h, cc: NC - 1 - cc
    return pl.pallas_call(
        body, name=name, grid=(H, NC),
        in_specs=[pl.BlockSpec((C, hv), lambda h, cc: (rev(h, cc), h)),
                  pl.BlockSpec((C, hk), lambda h, cc: (rev(h, cc), h)),
                  pl.BlockSpec((C, hk), lambda h, cc: (rev(h, cc), kq + h)),
                  pl.BlockSpec((C, hv), lambda h, cc: (rev(h, cc), kv + h)),
                  pl.BlockSpec((C, hv), lambda h, cc: (rev(h, cc), kr + h)),
                  pl.BlockSpec((C, hk), lambda h, cc: (rev(h, cc), h)),
                  pl.BlockSpec((C, hv), lambda h, cc: (rev(h, cc), h)),
                  pl.BlockSpec((1, 1, hk, hv), lambda h, cc: (h, rev(h, cc), 0, 0)),
                  pl.BlockSpec((1, 1, hk, hv), lambda h, cc: (h, jnp.minimum(rev(h, cc) + 1, NC - 1), 0, 0)),
                  pl.BlockSpec((1, hv), lambda h, cc: (0, 0))],
        out_specs=[pl.BlockSpec((C, hk), lambda h, cc: (rev(h, cc), h)),
                   pl.BlockSpec((C, hk), lambda h, cc: (rev(h, cc), h)),
                   pl.BlockSpec((C, hv), lambda h, cc: (rev(h, cc), h)),
                   pl.BlockSpec((C, hv), lambda h, cc: (rev(h, cc), h)),
                   pl.BlockSpec((C, hk), lambda h, cc: (rev(h, cc), h)),
                   pl.BlockSpec((1, hv), lambda h, cc: (0, 0))],
        out_shape=[jax.ShapeDtypeStruct((Lp, DK), bf16), jax.ShapeDtypeStruct((Lp, DK), bf16),
                   jax.ShapeDtypeStruct((Lp, DV), bf16), jax.ShapeDtypeStruct((Lp, DV), bf16),
                   jax.ShapeDtypeStruct((Lp, DK), f32), jax.ShapeDtypeStruct((1, hv), f32)],
        scratch_shapes=[pltpu.VMEM((hk, hv), f32), pltpu.VMEM((C, C), f32),
                        pltpu.VMEM((C, hk), f32), pltpu.VMEM((C, hk), f32)],
        compiler_params=_cparams(("arbitrary", "arbitrary")),
    )(dy, proj, proj, proj, proj, lg, o, states, states, hnw)


def _window_sums(x, back):
    n = x.shape[0]
    out = []
    s = x
    for w in (1, 2, 4, 8):
        s = s + pltpu.roll(s, w if back else n - w, 0)
        out.append(s)
    return out


def _pool_windows(hn, pad, n_real, name):
    Lp, D = hn.shape
    GW = D // POOL_GROUPS
    cb = min(GW, 256)
    per = GW // cb

    def body(h_ref, p_ref):
        g = pl.program_id(0) // per
        x = h_ref[...]
        s2, s4, s8, s16 = _window_sums(x, True)
        sel = jnp.where(g == 0, s2, jnp.where(g == 1, s4, jnp.where(g == 2, s8, s16)))
        win = jnp.left_shift(2, g).astype(f32)
        rows = lax.broadcasted_iota(jnp.int32, (Lp, 1), 0)
        t = (rows - pad).astype(f32)
        cnt = jnp.minimum(jnp.maximum(t, 0.0) + 1.0, win)
        p_ref[...] = jnp.where(rows >= pad, sel / cnt - x, 0.0).astype(bf16)

    return pl.pallas_call(
        body, name=name, grid=(D // cb,),
        in_specs=[pl.BlockSpec((Lp, cb), lambda i: (0, i))],
        out_specs=pl.BlockSpec((Lp, cb), lambda i: (0, i)),
        out_shape=jax.ShapeDtypeStruct((Lp, D), bf16),
        compiler_params=_cparams(("parallel",)),
    )(hn)


def _pool_windows_bwd(dp, pad, name):
    Lp, D = dp.shape
    GW = D // POOL_GROUPS
    cb = min(GW, 256)
    per = GW // cb

    def body(dp_ref, dh_ref):
        g = pl.program_id(0) // per
        rows = lax.broadcasted_iota(jnp.int32, (Lp, 1), 0)
        d = jnp.where(rows >= pad, dp_ref[...], 0.0)
        win = jnp.left_shift(2, g).astype(f32)
        t = (rows - pad).astype(f32)
        cnt = jnp.minimum(jnp.maximum(t, 0.0) + 1.0, win)
        s2, s4, s8, s16 = _window_sums(d / cnt, False)
        sel = jnp.where(g == 0, s2, jnp.where(g == 1, s4, jnp.where(g == 2, s8, s16)))
        dh_ref[...] = jnp.where(rows >= pad, sel - d, 0.0)

    return pl.pallas_call(
        body, name=name, grid=(D // cb,),
        in_specs=[pl.BlockSpec((Lp, cb), lambda i: (0, i))],
        out_specs=pl.BlockSpec((Lp, cb), lambda i: (0, i)),
        out_shape=jax.ShapeDtypeStruct((Lp, D), f32),
        compiler_params=_cparams(("parallel",)),
    )(dp)


def _pool_mix_fwd(xs, pooled, w, bias, scale, pad, name):
    Lp, D = xs.shape
    GW = D // POOL_GROUPS
    tm = _tile(Lp, 1056)

    def body(x_ref, p_ref, w_ref, b_ref, s_ref, o_ref):
        z = _dot(p_ref[...], w_ref[0]) + b_ref[...]
        keep = _row_ids(pl.program_id(1), tm) >= pad
        o_ref[...] = x_ref[...] + jnp.where(keep, z * s_ref[...], 0.0)

    blk = pl.BlockSpec((tm, GW), lambda g, i: (i, g))
    vec = pl.BlockSpec((1, GW), lambda g, i: (0, g))
    return pl.pallas_call(
        body, name=name, grid=(POOL_GROUPS, Lp // tm),
        in_specs=[blk, blk, pl.BlockSpec((1, GW, GW), lambda g, i: (g, 0, 0)), vec, vec],
        out_specs=blk, out_shape=jax.ShapeDtypeStruct((Lp, D), f32),
        compiler_params=_cparams(("parallel", "parallel")),
    )(xs, pooled, w, bias, scale)


def _pool_mix_bwd(dY, pooled, w, bias, scale, pad, name):
    Lp, D = dY.shape
    GW = D // POOL_GROUPS
    tm = _tile(Lp, 1056)
    nm = Lp // tm

    def body(dY_ref, p_ref, w_ref, b_ref, s_ref, dp_ref, dw_ref, db_ref, ds_ref, acc):
        i = pl.program_id(1)

        @pl.when(i == 0)
        def _():
            acc[...] = jnp.zeros_like(acc)
            db_ref[...] = jnp.zeros_like(db_ref)
            ds_ref[...] = jnp.zeros_like(ds_ref)

        keep = _row_ids(i, tm) >= pad
        dY_ = jnp.where(keep, dY_ref[...], 0.0)
        p = p_ref[...]
        z = _dot(p, w_ref[0]) + b_ref[...]
        ds_ref[...] += jnp.sum(dY_ * z, axis=0, keepdims=True)
        dz = dY_ * s_ref[...]
        db_ref[...] += jnp.sum(dz, axis=0, keepdims=True)
        dzb = dz.astype(bf16)
        acc[...] += _dot(p, dzb, TN)
        dp_ref[...] = _dot(dzb, w_ref[0], NT)

        @pl.when(i == nm - 1)
        def _():
            dw_ref[0] = acc[...].astype(bf16)

    blk = pl.BlockSpec((tm, GW), lambda g, i: (i, g))
    vec = pl.BlockSpec((1, GW), lambda g, i: (0, g))
    wsp = pl.BlockSpec((1, GW, GW), lambda g, i: (g, 0, 0))
    return pl.pallas_call(
        body, name=name, grid=(POOL_GROUPS, nm),
        in_specs=[blk, blk, wsp, vec, vec], out_specs=[blk, wsp, vec, vec],
        out_shape=[jax.ShapeDtypeStruct((Lp, D), f32), jax.ShapeDtypeStruct((POOL_GROUPS, GW, GW), bf16),
                   jax.ShapeDtypeStruct((1, D), f32), jax.ShapeDtypeStruct((1, D), f32)],
        scratch_shapes=[pltpu.VMEM((GW, GW), f32)],
        compiler_params=_cparams(("parallel", "arbitrary")),
    )(dY, pooled, w, bias, scale)


def _loss_head(xs, target, g, first, name):
    Lp, D = xs.shape
    tm = GLA_CHUNK
    off = first // tm

    def body(x_ref, t_ref, g_ref, loss_ref, dxs_ref, dg_ref):
        i = pl.program_id(0)

        @pl.when(i == 0)
        def _():
            loss_ref[...] = jnp.zeros_like(loss_ref)
            dg_ref[...] = jnp.zeros_like(dg_ref)

        @pl.when(i < off)
        def _():
            dxs_ref[...] = jnp.zeros_like(dxs_ref)

        @pl.when(i >= off)
        def _():
            x = x_ref[...]
            rstd = lax.rsqrt(jnp.mean(x * x, axis=-1, keepdims=True) + EPS)
            xhat = x * rstd
            gg = g_ref[...]
            err = xhat * gg - t_ref[...]
            loss_ref[...] += 0.5 * jnp.sum(jnp.mean(err * err, axis=-1, keepdims=True))
            dy = err * (1.0 / D)
            dg_ref[...] += jnp.sum(dy * xhat, axis=0, keepdims=True)
            dxh = dy * gg
            dxs_ref[...] = rstd * (dxh - xhat * jnp.mean(dxh * xhat, axis=-1, keepdims=True))

    row = pl.BlockSpec((tm, D), lambda i: (i, 0))
    return pl.pallas_call(
        body, name=name, grid=(Lp // tm,),
        in_specs=[row, pl.BlockSpec((tm, D), lambda i: (jnp.maximum(i - off, 0), 0)), pl.BlockSpec((1, D), lambda i: (0, 0))],
        out_specs=[pl.BlockSpec((8, LANES), lambda i: (0, 0)), row, pl.BlockSpec((1, D), lambda i: (0, 0))],
        out_shape=[jax.ShapeDtypeStruct((8, LANES), f32), jax.ShapeDtypeStruct((Lp, D), f32),
                   jax.ShapeDtypeStruct((1, D), f32)],
        compiler_params=_cparams(("arbitrary",)),
    )(xs, target, g)


def _adam_math(w, g, m, v):
    m2 = ADAM_B1 * m + (1.0 - ADAM_B1) * g
    v2 = ADAM_B2 * v + (1.0 - ADAM_B2) * (g * g)
    m_hat = m2 / (1.0 - ADAM_B1 ** ADAM_STEP)
    v_hat = v2 / (1.0 - ADAM_B2 ** ADAM_STEP)
    delta = -ADAM_LR * (m_hat / (jnp.sqrt(v_hat) + ADAM_EPS) + ADAM_WD * w)
    return delta, m2, v2


def _adamw(w, m, v, unit, own, own_idx, recv, prev, name):
    U, R, C = w.shape
    tr = _tile(R, 256, 8)
    n_recv = 0 if recv is None else recv.shape[0]

    def body(idx_ref, w_ref, m_ref, v_ref, own_ref, *rest):
        rest = list(rest)
        recv_refs = [rest.pop(0) for _ in range(n_recv)]
        if prev is not None:
            rest = rest[4:]
        g_ref, d_ref, m2_ref, v2_ref = rest
        g = own_ref[0].astype(f32)
        for r_ref in recv_refs:
            g = g + r_ref[0].astype(f32)
        delta, m2, v2 = _adam_math(w_ref[0], g, m_ref[0], v_ref[0])
        g_ref[0] = g
        d_ref[0] = delta
        m2_ref[0] = m2
        v2_ref[0] = v2

    blk = pl.BlockSpec((1, tr, C), lambda i, idx: (unit, i, 0))
    in_specs = [blk, blk, blk, pl.BlockSpec((1, tr, C), lambda i, idx: (idx[0], i, 0))]
    args = [w, m, v, own]
    for p in range(n_recv):
        in_specs.append(pl.BlockSpec((1, tr, C), lambda i, idx, p=p: (p, i, 0)))
        args.append(recv)
    aliases = {}
    if prev is not None:
        for t in range(4):
            aliases[1 + len(args) + t] = t
        in_specs += [ANY] * 4
        args += list(prev)
    out = jax.ShapeDtypeStruct((U, R, C), f32)
    return pl.pallas_call(
        body, name=name,
        grid_spec=pltpu.PrefetchScalarGridSpec(
            num_scalar_prefetch=1, grid=(R // tr,), in_specs=in_specs, out_specs=[blk] * 4),
        out_shape=[out] * 4, input_output_aliases=aliases,
        compiler_params=_cparams(("parallel",)),
    )(own_idx, *args)


def _place():
    return lax.axis_index("x"), lax.axis_index("y"), lax.axis_index("c")


def _all_gather(shards, name):
    n = len(shards)

    def body(*refs):
        ins, outs = refs[:n], refs[n:2 * n]
        send_sems, recv_sems, local_sems = refs[2 * n:]
        x, y, c = _place()
        sibling = (x, y, 1 - c)
        chips = [(1 - x, y), (x, 1 - y), (1 - x, 1 - y)]

        def slot(px, py, pc):
            return 4 * px + 2 * py + pc

        def copy(a, k, block, to, src=None):
            dst = outs[a].at[slot(*block)]
            return pltpu.make_async_remote_copy(
                src_ref=dst if src is None else src, dst_ref=dst,
                send_sem=send_sems.at[a, k], recv_sem=recv_sems.at[a, k],
                device_id=to, device_id_type=MESH)

        me = (x, y, c)
        mine = [pltpu.make_async_copy(ins[a], outs[a].at[slot(*me)], local_sems.at[a]) for a in range(n)]
        for cp in mine:
            cp.start()
        first = []
        for a in range(n):
            first.append(copy(a, 0, me, sibling, src=ins[a]))
            first += [copy(a, 1 + j, me, (*chip, c), src=ins[a]) for j, chip in enumerate(chips)]
        for cp in first:
            cp.start()
        passed = []
        for j, chip in enumerate(chips):
            for a in range(n):
                copy(a, 1 + j, (*chip, c), me).wait_recv()
                fwd = copy(a, 4 + j, (*chip, c), sibling)
                fwd.start()
                passed.append(fwd)
        for a in range(n):
            copy(a, 0, sibling, me).wait_recv()
            for j, chip in enumerate(chips):
                copy(a, 4 + j, (*chip, 1 - c), me).wait_recv()
        for cp in first + passed:
            cp.wait_send()
        for cp in mine:
            cp.wait()

    return pl.pallas_call(
        body, name=name,
        in_specs=[ANY] * n, out_specs=[ANY] * n,
        out_shape=[jax.ShapeDtypeStruct((N_DEV,) + s.shape, s.dtype) for s in shards],
        scratch_shapes=[pltpu.SemaphoreType.DMA((n, 7)), pltpu.SemaphoreType.DMA((n, 7)), pltpu.SemaphoreType.DMA((n,))],
    )(*shards)


def _pair_exchange(grads, name):
    n = len(grads)

    def body(*refs):
        ins, outs = refs[:n], refs[n:2 * n]
        send_sems, recv_sems = refs[2 * n:]
        x, y, c = _place()
        cps = []
        for a in range(n):
            for q in range(4):
                cps.append(pltpu.make_async_remote_copy(
                    src_ref=ins[a].at[2 * q + (1 - c)], dst_ref=outs[a].at[q],
                    send_sem=send_sems.at[a, q], recv_sem=recv_sems.at[a, q],
                    device_id=(x, y, 1 - c), device_id_type=MESH))
        for cp in cps:
            cp.start()
        for cp in cps:
            cp.wait()

    return pl.pallas_call(
        body, name=name,
        in_specs=[ANY] * n, out_specs=[ANY] * n,
        out_shape=[jax.ShapeDtypeStruct((4,) + g.shape[1:], g.dtype) for g in grads],
        scratch_shapes=[pltpu.SemaphoreType.DMA((n, 4)), pltpu.SemaphoreType.DMA((n, 4))],
    )(*grads)


def _pair_add(g, got, c_idx, name):
    _, R, C = g.shape
    tr = _tile(R, 512, 16)

    def body(c_ref, a_ref, b_ref, o_ref):
        o_ref[0] = (a_ref[0].astype(f32) + b_ref[0].astype(f32)).astype(o_ref.dtype)

    return pl.pallas_call(
        body, name=name,
        grid_spec=pltpu.PrefetchScalarGridSpec(
            num_scalar_prefetch=1, grid=(4, R // tr),
            in_specs=[pl.BlockSpec((1, tr, C), lambda q, i, c: (2 * q + c[0], i, 0)),
                      pl.BlockSpec((1, tr, C), lambda q, i, c: (q, i, 0))],
            out_specs=pl.BlockSpec((1, tr, C), lambda q, i, c: (q, i, 0))),
        out_shape=jax.ShapeDtypeStruct((4, R, C), g.dtype),
        compiler_params=_cparams(("parallel", "parallel")),
    )(c_idx, g, got)


def _chip_exchange(sums, name):
    n = len(sums)

    def body(*refs):
        ins, outs = refs[:n], refs[n:2 * n]
        send_sems, recv_sems = refs[2 * n:]
        x, y, c = _place()
        chips = [(1 - x, y), (x, 1 - y), (1 - x, 1 - y)]
        cps = []
        for a in range(n):
            for k, (px, py) in enumerate(chips):
                cps.append(pltpu.make_async_remote_copy(
                    src_ref=ins[a].at[2 * px + py], dst_ref=outs[a].at[k],
                    send_sem=send_sems.at[a, k], recv_sem=recv_sems.at[a, k],
                    device_id=(px, py, c), device_id_type=MESH))
        for cp in cps:
            cp.start()
        for cp in cps:
            cp.wait()

    return pl.pallas_call(
        body, name=name,
        in_specs=[ANY] * n, out_specs=[ANY] * n,
        out_shape=[jax.ShapeDtypeStruct((3,) + s.shape[1:], s.dtype) for s in sums],
        scratch_shapes=[pltpu.SemaphoreType.DMA((n, 3)), pltpu.SemaphoreType.DMA((n, 3))],
    )(*sums)


def _small_exchange(send, gather, name):
    R = send.shape[-2]

    def body(in_ref, out_ref, send_sems, recv_sems):
        x, y, c = _place()
        me = 4 * x + 2 * y + c
        out_ref[me] = in_ref[...] if gather else in_ref[me]
        cps = []
        for k in range(1, N_DEV):
            px, py, pc = x ^ ((k >> 2) & 1), y ^ ((k >> 1) & 1), c ^ (k & 1)
            src = in_ref if gather else in_ref.at[4 * px + 2 * py + pc]
            cps.append(pltpu.make_async_remote_copy(
                src_ref=src, dst_ref=out_ref.at[me],
                send_sem=send_sems.at[k - 1], recv_sem=recv_sems.at[k - 1],
                device_id=(px, py, pc), device_id_type=MESH))
        for cp in cps:
            cp.start()
        for cp in cps:
            cp.wait()

    return pl.pallas_call(
        body, name=name,
        in_specs=[pl.BlockSpec(memory_space=pltpu.VMEM)], out_specs=pl.BlockSpec(memory_space=pltpu.VMEM),
        out_shape=jax.ShapeDtypeStruct((N_DEV, R, LANES), f32),
        scratch_shapes=[pltpu.SemaphoreType.DMA((N_DEV - 1,)), pltpu.SemaphoreType.DMA((N_DEV - 1,))],
    )(send)


def _sum_blocks(blocks, name):
    def body(in_ref, o_ref):
        s = in_ref[0]
        for d in range(1, N_DEV):
            s = s + in_ref[d]
        o_ref[0] = s

    return pl.pallas_call(body, name=name, out_shape=jax.ShapeDtypeStruct((1,) + blocks.shape[1:], f32))(blocks)


def _rows(n):
    return -(-n // LANES)


def _pack(arrs, total_rows):
    parts = []
    for a in arrs:
        flat = a.reshape(-1).astype(f32)
        parts.append(jnp.pad(flat, (0, _rows(flat.size) * LANES - flat.size)))
    flat = jnp.concatenate(parts)
    return jnp.pad(flat, (0, total_rows * LANES - flat.size)).reshape(total_rows, LANES)


def _unpack(packed, shapes):
    lead = packed.shape[:-2]
    flat = packed.reshape(lead + (-1,))
    out, pos = [], 0
    for s in shapes:
        n = 1
        for d in s:
            n *= d
        out.append(flat[..., pos:pos + n].reshape(lead + tuple(s)))
        pos += _rows(n) * LANES
    return out


def _to_shards(full, axis):
    s = full.shape
    return jnp.moveaxis(full.reshape(s[:axis] + (N_DEV, s[axis] // N_DEV) + s[axis + 1:]), axis, 0)


def _from_shards(sh, axis):
    m = jnp.moveaxis(sh, 0, axis)
    s = m.shape
    return m.reshape(s[:axis] + (s[axis] * s[axis + 1],) + s[axis + 2:])


def kernel(x, meta, ffn_norm, ffn_w_gate, ffn_w_up, ffn_w_down, gla_norm, gla_w_in, gla_w_lr, gla_b_lr, gla_head_norm, gla_w_out, pool_norm, pool_w, pool_b, pool_scale, final_norm, loss_target, m_meta, m_ffn_norm, m_ffn_w_gate, m_ffn_w_up, m_ffn_w_down, m_gla_norm, m_gla_w_in, m_gla_w_lr, m_gla_b_lr, m_gla_head_norm, m_gla_w_out, m_pool_norm, m_pool_w, m_pool_b, m_pool_scale, m_final_norm, v_meta, v_ffn_norm, v_ffn_w_gate, v_ffn_w_up, v_ffn_w_down, v_gla_norm, v_gla_w_in, v_gla_w_lr, v_gla_b_lr, v_gla_head_norm, v_gla_w_out, v_pool_norm, v_pool_w, v_pool_b, v_pool_scale, v_final_norm):
    H = GLA_HEADS
    _, SEQ, D = x.shape
    Fs = ffn_w_gate.shape[-1]
    DK, DV = D // 2, D
    hv = DV // H
    GW = D // POOL_GROUPS
    INW = 2 * DK + 2 * DV + GATE_RANK
    NPK = 2 * DK + 2 * DV + GATE_PAD
    pad = (-N_META) % GLA_CHUNK
    first = pad + N_META
    Lp = first + SEQ
    n_units = ffn_w_gate.shape[0] * ffn_w_gate.shape[1]
    assert first % GLA_CHUNK == 0 and Lp % GLA_CHUNK == 0 and pad >= POOL_GROUPS * 4

    px, py, pc = _place()
    c_idx = jnp.reshape(pc, (1,)).astype(jnp.int32)
    q_idx = jnp.reshape(2 * px + py, (1,)).astype(jnp.int32)
    zero_idx = jnp.zeros((1,), jnp.int32)

    small_sh = [meta, ffn_norm, gla_w_lr, pool_norm, pool_b, pool_scale]
    small_axis = [1, 2, 2, 1, 2, 1]
    sh_shapes = [a.shape for a in small_sh]
    sh_rows = -(-sum(_rows(a.size) for a in small_sh) // 8) * 8
    gathered = _small_exchange(_pack(small_sh, sh_rows), True, "small_gather")
    meta_f, ffn_norm_f, wlr_f, pool_norm_f, pool_b_f, pool_scale_f = [
        _from_shards(a, ax) for a, ax in zip(_unpack(gathered, sh_shapes), small_axis)]
    ffn_norm_f = ffn_norm_f.reshape(n_units, 1, D)
    wlr128 = jnp.pad(wlr_f[0], ((0, GATE_PAD - GATE_RANK), (0, 0)))

    wg_l = ffn_w_gate.reshape(n_units, D, Fs).astype(bf16)
    wu_l = ffn_w_up.reshape(n_units, D, Fs).astype(bf16)
    wd_l = ffn_w_down.reshape(n_units, Fs, D).astype(bf16)
    ffn_w = [_all_gather([wg_l[u], wu_l[u], wd_l[u]], f"gather_ffn{u}") for u in range(n_units)]
    win_g, wout_g, wpool_g = _all_gather(
        [gla_w_in[0].astype(bf16), gla_w_out[0].astype(bf16), pool_w[0].astype(bf16)], "gather_mixers")
    win_full = _from_shards(win_g, 1)
    c_lr = 2 * DK + DV
    win_p = jnp.concatenate([win_full[:, :c_lr], win_full[:, c_lr + GATE_RANK:], win_full[:, c_lr:c_lr + GATE_RANK],
                             jnp.zeros((D, GATE_PAD - GATE_RANK), bf16)], axis=1)
    wout_full = wout_g.reshape(DV, D)
    wpool_full = _from_shards(wpool_g, 1)
    gate_blk = (2 * DK + 2 * DV) // GATE_PAD

    xs = jnp.concatenate([jnp.zeros((pad, D), f32), meta_f, x[0]], axis=0)
    saved = {}

    def ffn_f(u, xs):
        out, h, G, U = _ffn_fwd(xs, ffn_norm_f[u], *ffn_w[u], name=f"ffn_fwd{u}")
        saved[("ffn", u)] = (xs, h, G, U)
        return out

    def gla_f(xs):
        hn = _rms_fwd(xs, gla_norm, bf16, "gla_norm_fwd")
        proj = _mm(hn, win_p, "nn", f32, "gla_proj", tm=1056, tn=896, tk=2048)
        lg = _gate_fwd(proj, wlr128, gla_b_lr, pad, gate_blk, "gla_gate_fwd")
        o, y, states = _gla_fwd(proj, lg, gla_head_norm, H, "gla_core_fwd")
        out = _mm(y, wout_full, "nn", f32, "gla_out", tm=1056, tn=512, tk=2048, residual=xs)
        saved["gla"] = (xs, hn, proj, lg, o, y, states)
        return out

    def pool_f(xs):
        hn = _rms_fwd(xs, pool_norm_f, f32, "pool_norm_fwd")
        pooled = _pool_windows(hn, pad, Lp - pad, "pool_windows_fwd")
        out = _pool_mix_fwd(xs, pooled, wpool_full, pool_b_f.reshape(1, D), pool_scale_f, pad, "pool_mix_fwd")
        saved["pool"] = (xs, pooled)
        return out

    depth = ffn_w_gate.shape[0]
    for i in range(depth):
        xs = ffn_f(2 * i, xs)
        xs = gla_f(xs) if i % 2 == 0 else pool_f(xs)
        xs = ffn_f(2 * i + 1, xs)
    loss_part, dxs, d_final = _loss_head(xs, loss_target[0], final_norm.reshape(1, D), first, "loss_head")

    big = {}

    def reduce_big(tag, grads):
        got = _pair_exchange(grads, f"pair_{tag}")
        sums = [_pair_add(g, r, c_idx, f"pair_add_{tag}{a}") for a, (g, r) in enumerate(zip(grads, got))]
        recv = _chip_exchange(sums, f"chips_{tag}")
        return list(zip(sums, recv))

    d_ffn_norm = [None] * n_units

    def ffn_b(u, dY):
        xs_in, h, G, U = saved[("ffn", u)]
        dxs, dyh, dG, dU, A, dg = _ffn_bwd_dgrad(dY, xs_in, ffn_norm_f[u], *ffn_w[u], G, U, pad, f"ffn_dgrad{u}")
        dwg, dwu, dwd = _ffn_bwd_wgrad(h, dyh, A, dG, dU, f"ffn_wgrad{u}")
        big[("ffn", u)] = reduce_big(f"ffn{u}", [dwg, dwu, dwd])
        d_ffn_norm[u] = dg
        return dxs

    small_grads = {}

    def gla_b(dY):
        xs_in, hn, proj, lg, o, y, states = saved["gla"]
        dyb = dY.astype(bf16)
        dy = _mm(dyb, wout_full, "nt", f32, "gla_out_dgrad", tm=1056, tn=512, tk=2048)
        dwout = _mm(y, dyb, "tn", bf16, "gla_out_wgrad", tm=1024, tn=1024, tk=528)
        dq, dk, dv, dr, dlg, dhw = _gla_bwd(dy, proj, lg, o, states, gla_head_norm, H, pad, "gla_core_bwd")
        dlr, dwlr, dblr = _gate_bwd(dlg, proj, wlr128, gla_b_lr, pad, gate_blk, "gla_gate_bwd")
        dproj = jnp.concatenate([dq, dk, dv, dr, dlr], axis=1)
        dwin_p = _mm(hn, dproj, "tn", bf16, "gla_proj_wgrad", tm=1024, tn=896, tk=528)
        dhn = _mm(dproj, win_p, "nt", f32, "gla_proj_dgrad", tm=1056, tn=512, tk=896)
        dxs, dgn = _rms_bwd(dY, dhn, xs_in, gla_norm, pad, "gla_norm_bwd")
        c_r = 2 * DK + 2 * DV
        dwin = jnp.concatenate([dwin_p[:, :c_lr], dwin_p[:, c_r:c_r + GATE_RANK], dwin_p[:, c_lr:c_r]], axis=1)
        big["gla"] = reduce_big("gla", [_to_shards(dwin, 1), dwout.reshape(N_DEV, DV // N_DEV, D)])
        small_grads.update(gla_w_lr=dwlr[:GATE_RANK][None], gla_b_lr=dblr, gla_head_norm=dhw, gla_norm=dgn)
        return dxs

    def pool_b_(dY):
        xs_in, pooled = saved["pool"]
        dp, dw, db, ds = _pool_mix_bwd(dY, pooled, wpool_full, pool_b_f.reshape(1, D), pool_scale_f, pad, "pool_mix_bwd")
        dhn = _pool_windows_bwd(dp, pad, "pool_windows_bwd")
        dxs, dgn = _rms_bwd(dY, dhn, xs_in, pool_norm_f, pad, "pool_norm_bwd")
        dws = _to_shards(dw, 1)
        big["pool"] = reduce_big("pool", [dws.reshape(N_DEV, POOL_GROUPS * GW // N_DEV, GW)])
        small_grads.update(pool_b=db.reshape(1, POOL_GROUPS, GW), pool_scale=ds, pool_norm=dgn)
        return dxs

    for i in reversed(range(depth)):
        dxs = ffn_b(2 * i + 1, dxs)
        dxs = gla_b(dxs) if i % 2 == 0 else pool_b_(dxs)
        dxs = ffn_b(2 * i, dxs)
    grad_x = dxs[first:].reshape(x.shape)
    small_grads.update(meta=dxs[pad:first], ffn_norm=jnp.concatenate(d_ffn_norm, axis=0).reshape(ffn_norm_f.shape[0] // 2, 2, D),
                       final_norm=d_final.reshape(D))

    sh_names = ["meta", "ffn_norm", "gla_w_lr", "pool_norm", "pool_b", "pool_scale"]
    rep_names = ["gla_norm", "gla_b_lr", "gla_head_norm", "final_norm"]
    rep_w = [gla_norm, gla_b_lr, gla_head_norm, final_norm]
    rep_shapes = [a.shape for a in rep_w]
    rep_rows = -(-sum(_rows(a.size) for a in rep_w) // 8) * 8
    by_owner = [_to_shards(small_grads[nm].reshape(full_shape), ax) for nm, full_shape, ax in zip(
        sh_names, [meta_f.shape, (ffn_norm.shape[0], 2, D), wlr_f.shape, pool_norm_f.shape, pool_b_f.shape, pool_scale_f.shape],
        small_axis)]
    rep_pack = _pack([small_grads[nm].reshape(s) for nm, s in zip(rep_names, rep_shapes)], rep_rows)
    send = jnp.stack([
        jnp.concatenate([_pack([g[d] for g in by_owner], sh_rows), rep_pack, loss_part], axis=0) for d in range(N_DEV)])
    total = _sum_blocks(_small_exchange(send, False, "small_reduce"), "small_sum")
    loss = total[0, sh_rows + rep_rows, 0]
    n_small = sh_rows + rep_rows
    g_small = total[:, :n_small]

    def pack_small(sh_list, rep_list):
        return jnp.concatenate([_pack(sh_list, sh_rows), _pack(rep_list, rep_rows)], axis=0)[None]

    w_small = pack_small(small_sh, rep_w)
    m_small = pack_small([m_meta, m_ffn_norm, m_gla_w_lr, m_pool_norm, m_pool_b, m_pool_scale],
                         [m_gla_norm, m_gla_b_lr, m_gla_head_norm, m_final_norm])
    v_small = pack_small([v_meta, v_ffn_norm, v_gla_w_lr, v_pool_norm, v_pool_b, v_pool_scale],
                         [v_gla_norm, v_gla_b_lr, v_gla_head_norm, v_final_norm])
    small_out = _adamw(w_small, m_small, v_small, 0, g_small, zero_idx, None, None, "adamw_small")
    small_res = {}
    for kind, packed in zip(("grad", "delta", "new_m", "new_v"), small_out):
        sh_vals = _unpack(packed[0, :sh_rows], sh_shapes)
        rep_vals = _unpack(packed[0, sh_rows:], rep_shapes)
        for nm, val in zip(sh_names + rep_names, sh_vals + rep_vals):
            small_res[(kind, nm)] = val

    big_res = {}

    def adam_units(nm, w, m, v, entries):
        U = len(entries)
        shape = w.shape
        R, C = entries[0][0].shape[1:]
        w3, m3, v3 = (a.reshape(U, R, C) for a in (w, m, v))
        prev = None
        for u, (sums, recv) in enumerate(entries):
            if U > 1 and prev is None:
                prev = [lax.empty((U, R, C), f32) for _ in range(4)]
            prev = _adamw(w3, m3, v3, u, sums, q_idx, recv, prev if U > 1 else None, f"adamw_{nm}{u}")
        for kind, val in zip(("grad", "delta", "new_m", "new_v"), prev):
            big_res[(kind, nm)] = val.reshape(shape)

    for a, nm in enumerate(["ffn_w_gate", "ffn_w_up", "ffn_w_down"]):
        w, m, v = {"ffn_w_gate": (ffn_w_gate, m_ffn_w_gate, v_ffn_w_gate), "ffn_w_up": (ffn_w_up, m_ffn_w_up, v_ffn_w_up),
                   "ffn_w_down": (ffn_w_down, m_ffn_w_down, v_ffn_w_down)}[nm]
        adam_units(nm, w, m, v, [big[("ffn", u)][a] for u in range(n_units)])
    adam_units("gla_w_in", gla_w_in, m_gla_w_in, v_gla_w_in, [big["gla"][0]])
    adam_units("gla_w_out", gla_w_out, m_gla_w_out, v_gla_w_out, [big["gla"][1]])
    adam_units("pool_w", pool_w, m_pool_w, v_pool_w, [big["pool"][0]])

    order = ["meta", "ffn_norm", "ffn_w_gate", "ffn_w_up", "ffn_w_down", "gla_norm", "gla_w_in", "gla_w_lr", "gla_b_lr",
             "gla_head_norm", "gla_w_out", "pool_norm", "pool_w", "pool_b", "pool_scale", "final_norm"]
    res = {**small_res, **big_res}
    outs = [loss, grad_x]
    for kind in ("grad", "delta", "new_m", "new_v"):
        outs += [res[(kind, nm)] for nm in order]
    return tuple(outs)
```

```python
import functools

import jax
import jax.numpy as jnp
from jax import lax
from jax.experimental import pallas as pl
from jax.experimental.pallas import tpu as pltpu

f32 = jnp.float32
bf16 = jnp.bfloat16

N_DEV = 8
N_META = 16
GLA_HEADS = 4
GLA_CHUNK = 64
GLA_SUB = 16
GATE_RANK = 16
GATE_PAD = 128
GATE_NORM = 16.0
EPS = 1e-6
POOL_GROUPS = 4
ADAM_LR = 0.001
ADAM_B1 = 0.9
ADAM_B2 = 0.999
ADAM_EPS = 1e-08
ADAM_WD = 0.01
ADAM_STEP = 10
LANES = 128
VMEM_LIMIT_MB = 56

NN = (((1,), (0,)), ((), ()))
NT = (((1,), (1,)), ((), ()))
TN = (((0,), (0,)), ((), ()))
HI = lax.Precision.HIGHEST
MESH = pl.DeviceIdType.MESH
ANY = pl.BlockSpec(memory_space=pl.ANY)


def _cparams(sem=None, vmem_mb=None):
    kw = {}
    if sem is not None:
        kw["dimension_semantics"] = sem
    if vmem_mb is not None:
        kw["vmem_limit_bytes"] = vmem_mb * 2 ** 20
    return pltpu.CompilerParams(**kw)


def _tile(n, target, mult=16):
    best = None
    for t in range(mult, min(n, target) + 1, mult):
        if n % t == 0:
            best = t
    assert best is not None, (n, target, mult)
    return best


def _dot(a, b, dims=NN, precision=None):
    return lax.dot_general(a, b, dims, preferred_element_type=f32, precision=precision)


def _sigmoid(x):
    return 1.0 / (1.0 + jnp.exp(-x))


def _row_ids(tile_index, tm):
    return tile_index * tm + lax.broadcasted_iota(jnp.int32, (tm, 1), 0)


def _rms_fwd(xs, g, out_dtype, name):
    Lp, D = xs.shape
    tm = _tile(Lp, 528)

    def body(x_ref, g_ref, h_ref):
        x = x_ref[...]
        rstd = lax.rsqrt(jnp.mean(x * x, axis=-1, keepdims=True) + EPS)
        h_ref[...] = (x * rstd * g_ref[...]).astype(out_dtype)

    return pl.pallas_call(
        body, name=name, grid=(Lp // tm,),
        in_specs=[pl.BlockSpec((tm, D), lambda i: (i, 0)), pl.BlockSpec((1, D), lambda i: (0, 0))],
        out_specs=pl.BlockSpec((tm, D), lambda i: (i, 0)),
        out_shape=jax.ShapeDtypeStruct((Lp, D), out_dtype),
        compiler_params=_cparams(("parallel",)),
    )(xs, g)


def _rms_bwd(dY, dh, xs, g, pad, name):
    Lp, D = xs.shape
    tm = _tile(Lp, 352)

    def body(dY_ref, dh_ref, x_ref, g_ref, dxs_ref, dg_ref):
        i = pl.program_id(0)

        @pl.when(i == 0)
        def _():
            dg_ref[...] = jnp.zeros_like(dg_ref)

        x = x_ref[...]
        rstd = lax.rsqrt(jnp.mean(x * x, axis=-1, keepdims=True) + EPS)
        xhat = x * rstd
        dh_ = dh_ref[...]
        dg_ref[...] += jnp.sum(dh_ * xhat, axis=0, keepdims=True)
        dxh = dh_ * g_ref[...]
        dx = rstd * (dxh - xhat * jnp.mean(dxh * xhat, axis=-1, keepdims=True))
        dxs_ref[...] = jnp.where(_row_ids(i, tm) >= pad, dY_ref[...] + dx, 0.0)

    row = pl.BlockSpec((tm, D), lambda i: (i, 0))
    vec = pl.BlockSpec((1, D), lambda i: (0, 0))
    return pl.pallas_call(
        body, name=name, grid=(Lp // tm,),
        in_specs=[row, row, row, vec], out_specs=[row, vec],
        out_shape=[jax.ShapeDtypeStruct((Lp, D), f32), jax.ShapeDtypeStruct((1, D), f32)],
        compiler_params=_cparams(("arbitrary",)),
    )(dY, dh, xs, g)


def _mm(a, b, mode, out_dtype, name, tm=512, tn=512, tk=512, residual=None):
    if mode == "nn":
        (M, K), N = a.shape, b.shape[1]
    elif mode == "nt":
        (M, K), N = a.shape, b.shape[0]
    else:
        (K, M), N = a.shape, b.shape[1]
    tm = _tile(M, tm, 16 if mode != "tn" else LANES) if M > tm else M
    tn = _tile(N, tn, LANES) if N > tn else N
    tk = _tile(K, tk, LANES if mode != "tn" else 16) if K > tk else K
    nk = K // tk
    dims = {"nn": NN, "nt": NT, "tn": TN}[mode]

    def body(*refs):
        if residual is None:
            a_ref, b_ref, o_ref, acc = refs
            r_ref = None
        else:
            a_ref, b_ref, r_ref, o_ref, acc = refs
        k = pl.program_id(2)

        @pl.when(k == 0)
        def _():
            acc[...] = jnp.zeros_like(acc)

        acc[...] += _dot(a_ref[...], b_ref[...], dims)

        @pl.when(k == nk - 1)
        def _():
            r = acc[...]
            if r_ref is not None:
                r = r + r_ref[...]
            o_ref[...] = r.astype(out_dtype)

    a_spec = pl.BlockSpec((tk, tm), lambda i, j, k: (k, i)) if mode == "tn" else pl.BlockSpec((tm, tk), lambda i, j, k: (i, k))
    b_spec = pl.BlockSpec((tn, tk), lambda i, j, k: (j, k)) if mode == "nt" else pl.BlockSpec((tk, tn), lambda i, j, k: (k, j))
    o_spec = pl.BlockSpec((tm, tn), lambda i, j, k: (i, j))
    in_specs = [a_spec, b_spec] + ([o_spec] if residual is not None else [])
    args = (a, b) + ((residual,) if residual is not None else ())
    return pl.pallas_call(
        body, name=name, grid=(M // tm, N // tn, nk),
        in_specs=in_specs, out_specs=o_spec,
        out_shape=jax.ShapeDtypeStruct((M, N), out_dtype),
        scratch_shapes=[pltpu.VMEM((tm, tn), f32)],
        compiler_params=_cparams(("parallel", "parallel", "arbitrary"), VMEM_LIMIT_MB),
    )(*args)


def _ffn_fwd(xs, g, wg, wu, wd, name):
    Lp, D = xs.shape
    nd, _, Fs = wg.shape
    tm = _tile(Lp, 704)
    once = pl.Buffered(1)

    def body(x_ref, g_ref, wg_ref, wu_ref, wd_ref, out_ref, h_ref, G_ref, U_ref, hs, acc):
        j = pl.program_id(1)

        @pl.when(j == 0)
        def _():
            x = x_ref[...]
            rstd = lax.rsqrt(jnp.mean(x * x, axis=-1, keepdims=True) + EPS)
            h = (x * rstd * g_ref[...]).astype(bf16)
            hs[...] = h
            h_ref[...] = h
            acc[...] = jnp.zeros_like(acc)

        h = hs[...]
        G = _dot(h, wg_ref[0])
        U = _dot(h, wu_ref[0])
        G_ref[0] = G.astype(bf16)
        U_ref[0] = U.astype(bf16)
        A = (G * _sigmoid(G) * U).astype(bf16)
        acc[...] += _dot(A, wd_ref[0])

        @pl.when(j == nd - 1)
        def _():
            out_ref[...] = x_ref[...] + 0.5 * acc[...]

    row_f = pl.BlockSpec((tm, D), lambda i, j: (i, 0), pipeline_mode=once)
    act = pl.BlockSpec((1, tm, Fs), lambda i, j: (j, i, 0))
    return pl.pallas_call(
        body, name=name, grid=(Lp // tm, nd),
        in_specs=[row_f, pl.BlockSpec((1, D), lambda i, j: (0, 0)),
                  pl.BlockSpec((1, D, Fs), lambda i, j: (j, 0, 0)),
                  pl.BlockSpec((1, D, Fs), lambda i, j: (j, 0, 0)),
                  pl.BlockSpec((1, Fs, D), lambda i, j: (j, 0, 0))],
        out_specs=[row_f, pl.BlockSpec((tm, D), lambda i, j: (i, 0), pipeline_mode=once), act, act],
        out_shape=[jax.ShapeDtypeStruct((Lp, D), f32), jax.ShapeDtypeStruct((Lp, D), bf16),
                   jax.ShapeDtypeStruct((nd, Lp, Fs), bf16), jax.ShapeDtypeStruct((nd, Lp, Fs), bf16)],
        scratch_shapes=[pltpu.VMEM((tm, D), bf16), pltpu.VMEM((tm, D), f32)],
        compiler_params=_cparams(("parallel", "arbitrary"), VMEM_LIMIT_MB),
    )(xs, g, wg, wu, wd)


def _ffn_bwd_dgrad(dY, xs, g, wg, wu, wd, G, U, pad, name):
    Lp, D = xs.shape
    nd, _, Fs = wg.shape
    tm = _tile(Lp, 352)
    once = pl.Buffered(1)

    def body(dY_ref, x_ref, g_ref, wg_ref, wu_ref, wd_ref, G_ref, U_ref,
             dxs_ref, dyh_ref, dG_ref, dU_ref, A_ref, dg_ref, dyh_s, acc):
        i = pl.program_id(0)
        j = pl.program_id(1)

        @pl.when(j == 0)
        def _():
            d = (0.5 * dY_ref[...]).astype(bf16)
            dyh_s[...] = d
            dyh_ref[...] = d
            acc[...] = jnp.zeros_like(acc)

        @pl.when((i == 0) & (j == 0))
        def _():
            dg_ref[...] = jnp.zeros_like(dg_ref)

        dA = _dot(dyh_s[...], wd_ref[0], NT)
        Gf = G_ref[0].astype(f32)
        Uf = U_ref[0].astype(f32)
        s = _sigmoid(Gf)
        silu = Gf * s
        dGb = (dA * Uf * (s * (1.0 + Gf * (1.0 - s)))).astype(bf16)
        dUb = (dA * silu).astype(bf16)
        dG_ref[0] = dGb
        dU_ref[0] = dUb
        A_ref[0] = (silu * Uf).astype(bf16)
        acc[...] += _dot(dGb, wg_ref[0], NT) + _dot(dUb, wu_ref[0], NT)

        @pl.when(j == nd - 1)
        def _():
            x = x_ref[...]
            rstd = lax.rsqrt(jnp.mean(x * x, axis=-1, keepdims=True) + EPS)
            xhat = x * rstd
            dh = acc[...]
            dg_ref[...] += jnp.sum(dh * xhat, axis=0, keepdims=True)
            dxh = dh * g_ref[...]
            dx = rstd * (dxh - xhat * jnp.mean(dxh * xhat, axis=-1, keepdims=True))
            dxs_ref[...] = jnp.where(_row_ids(i, tm) >= pad, dY_ref[...] + dx, 0.0)

    row_f = pl.BlockSpec((tm, D), lambda i, j: (i, 0), pipeline_mode=once)
    vec = pl.BlockSpec((1, D), lambda i, j: (0, 0))
    wcol = pl.BlockSpec((1, D, Fs), lambda i, j: (j, 0, 0))
    act = pl.BlockSpec((1, tm, Fs), lambda i, j: (j, i, 0))
    act_s = jax.ShapeDtypeStruct((nd, Lp, Fs), bf16)
    return pl.pallas_call(
        body, name=name, grid=(Lp // tm, nd),
        in_specs=[row_f, row_f, vec, wcol, wcol, pl.BlockSpec((1, Fs, D), lambda i, j: (j, 0, 0)), act, act],
        out_specs=[row_f, pl.BlockSpec((tm, D), lambda i, j: (i, 0), pipeline_mode=once), act, act, act, vec],
        out_shape=[jax.ShapeDtypeStruct((Lp, D), f32), jax.ShapeDtypeStruct((Lp, D), bf16), act_s, act_s, act_s,
                   jax.ShapeDtypeStruct((1, D), f32)],
        scratch_shapes=[pltpu.VMEM((tm, D), bf16), pltpu.VMEM((tm, D), f32)],
        compiler_params=_cparams(("arbitrary", "arbitrary"), VMEM_LIMIT_MB),
    )(dY, xs, g, wg, wu, wd, G, U)


def _ffn_bwd_wgrad(h, dyh, A, dG, dU, name):
    Lp, D = h.shape
    nd, _, Fs = A.shape
    tk = _tile(Lp, 528)
    nk = Lp // tk

    def body(h_ref, dyh_ref, A_ref, dG_ref, dU_ref, dwg_ref, dwu_ref, dwd_ref, ag, au, ad):
        i = pl.program_id(1)

        @pl.when(i == 0)
        def _():
            ag[...] = jnp.zeros_like(ag)
            au[...] = jnp.zeros_like(au)
            ad[...] = jnp.zeros_like(ad)

        hh = h_ref[...]
        ag[...] += _dot(hh, dG_ref[0], TN)
        au[...] += _dot(hh, dU_ref[0], TN)
        ad[...] += _dot(A_ref[0], dyh_ref[...], TN)

        @pl.when(i == nk - 1)
        def _():
            dwg_ref[0] = ag[...].astype(bf16)
            dwu_ref[0] = au[...].astype(bf16)
            dwd_ref[0] = ad[...].astype(bf16)

    row = pl.BlockSpec((tk, D), lambda j, i: (i, 0))
    act = pl.BlockSpec((1, tk, Fs), lambda j, i: (j, i, 0))
    wcol = pl.BlockSpec((1, D, Fs), lambda j, i: (j, 0, 0))
    wrow = pl.BlockSpec((1, Fs, D), lambda j, i: (j, 0, 0))
    return pl.pallas_call(
        body, name=name, grid=(nd, nk),
        in_specs=[row, row, act, act, act], out_specs=[wcol, wcol, wrow],
        out_shape=[jax.ShapeDtypeStruct((nd, D, Fs), bf16), jax.ShapeDtypeStruct((nd, D, Fs), bf16),
                   jax.ShapeDtypeStruct((nd, Fs, D), bf16)],
        scratch_shapes=[pltpu.VMEM((D, Fs), f32), pltpu.VMEM((D, Fs), f32), pltpu.VMEM((Fs, D), f32)],
        compiler_params=_cparams(("parallel", "arbitrary"), VMEM_LIMIT_MB),
    )(h, dyh, A, dG, dU)


def _gate_fwd(proj, wlr, blr, pad, gate_blk, name):
    Lp = proj.shape[0]
    DK = wlr.shape[1]
    tm = _tile(Lp, 528)

    def body(lr_ref, w_ref, b_ref, lg_ref):
        z = _dot(lr_ref[...].astype(bf16), w_ref[...].astype(bf16)) + b_ref[...]
        ls = jnp.minimum(z, 0.0) - jnp.log(1.0 + jnp.exp(-jnp.abs(z)))
        lg_ref[...] = jnp.where(_row_ids(pl.program_id(0), tm) >= pad, ls * (1.0 / GATE_NORM), 0.0)

    return pl.pallas_call(
        body, name=name, grid=(Lp // tm,),
        in_specs=[pl.BlockSpec((tm, GATE_PAD), lambda i: (i, gate_blk)),
                  pl.BlockSpec((GATE_PAD, DK), lambda i: (0, 0)), pl.BlockSpec((1, DK), lambda i: (0, 0))],
        out_specs=pl.BlockSpec((tm, DK), lambda i: (i, 0)),
        out_shape=jax.ShapeDtypeStruct((Lp, DK), f32),
        compiler_params=_cparams(("parallel",)),
    )(proj, wlr, blr)


def _gate_bwd(dlg, proj, wlr, blr, pad, gate_blk, name):
    Lp = proj.shape[0]
    DK = wlr.shape[1]
    tm = _tile(Lp, 528)

    def body(dlg_ref, lr_ref, w_ref, b_ref, dlr_ref, dw_ref, db_ref):
        i = pl.program_id(0)

        @pl.when(i == 0)
        def _():
            dw_ref[...] = jnp.zeros_like(dw_ref)
            db_ref[...] = jnp.zeros_like(db_ref)

        lr = lr_ref[...].astype(bf16)
        w = w_ref[...].astype(bf16)
        z = _dot(lr, w) + b_ref[...]
        dz = jnp.where(_row_ids(i, tm) >= pad, dlg_ref[...] * _sigmoid(-z) * (1.0 / GATE_NORM), 0.0)
        dzb = dz.astype(bf16)
        dlr_ref[...] = _dot(dzb, w, NT).astype(bf16)
        dw_ref[...] += _dot(lr, dzb, TN)
        db_ref[...] += jnp.sum(dz, axis=0, keepdims=True)

    return pl.pallas_call(
        body, name=name, grid=(Lp // tm,),
        in_specs=[pl.BlockSpec((tm, DK), lambda i: (i, 0)), pl.BlockSpec((tm, GATE_PAD), lambda i: (i, gate_blk)),
                  pl.BlockSpec((GATE_PAD, DK), lambda i: (0, 0)), pl.BlockSpec((1, DK), lambda i: (0, 0))],
        out_specs=[pl.BlockSpec((tm, GATE_PAD), lambda i: (i, 0)), pl.BlockSpec((GATE_PAD, DK), lambda i: (0, 0)),
                   pl.BlockSpec((1, DK), lambda i: (0, 0))],
        out_shape=[jax.ShapeDtypeStruct((Lp, GATE_PAD), bf16), jax.ShapeDtypeStruct((GATE_PAD, DK), f32),
                   jax.ShapeDtypeStruct((1, DK), f32)],
        compiler_params=_cparams(("arbitrary",)),
    )(dlg, proj, wlr, blr)


def _chunk_decay(lg):
    C = lg.shape[0]
    r = lax.broadcasted_iota(jnp.int32, (C, C), 0)
    c = lax.broadcasted_iota(jnp.int32, (C, C), 1)
    return _dot(jnp.where(r >= c, 1.0, 0.0).astype(f32), lg, NN, HI)


def _col(v):
    return jnp.transpose(jnp.broadcast_to(v, (8, v.shape[1])))[:, 0:1]


def _intra_scores(q, k, b, A_ref):
    C = q.shape[0]
    S = GLA_SUB
    A_ref[...] = jnp.zeros_like(A_ref)
    ri = lax.broadcasted_iota(jnp.int32, (S, 1), 0)
    for I in range(C // S):
        lo = S * I
        qI, bI = q[lo:lo + S], b[lo:lo + S]
        if I > 0:
            bref = b[lo - 1:lo]
            qs = qI * jnp.exp(bI - bref)
            ks = k[:lo] * jnp.exp(bref - b[:lo])
            A_ref[lo:lo + S, 0:lo] = _dot(qs, ks, NT, HI)
        for jj in range(S):
            j = lo + jj
            P = jnp.exp(jnp.minimum(bI - b[j:j + 1], 0.0))
            a = jnp.sum(qI * P * k[j:j + 1], axis=1, keepdims=True)
            A_ref[lo:lo + S, j:j + 1] = jnp.where(ri >= jj, a, 0.0)


def _intra_grads(q, k, b, dA, dq_ref, dk_ref):
    C = q.shape[0]
    S = GLA_SUB
    ri = lax.broadcasted_iota(jnp.int32, (S, 1), 0)
    for I in range(C // S):
        lo = S * I
        qI, bI = q[lo:lo + S], b[lo:lo + S]
        dqI = jnp.zeros_like(qI)
        if I > 0:
            bref = b[lo - 1:lo]
            eq = jnp.exp(bI - bref)
            ek = jnp.exp(bref - b[:lo])
            qs = qI * eq
            ks = k[:lo] * ek
            dAI = dA[lo:lo + S, 0:lo]
            dqI = dqI + _dot(dAI, ks, NN, HI) * eq
            dk_ref[0:lo, :] += _dot(dAI, qs, TN, HI) * ek
        for jj in range(S):
            j = lo + jj
            P = jnp.exp(jnp.minimum(bI - b[j:j + 1], 0.0))
            t = jnp.where(ri >= jj, dA[lo:lo + S, j:j + 1], 0.0) * P
            dqI = dqI + t * k[j:j + 1]
            dk_ref[j:j + 1, :] += jnp.sum(t * qI, axis=0, keepdims=True)
        dq_ref[lo:lo + S, :] += dqI


def _gla_fwd(proj, lg, hnw, H, name):
    Lp = proj.shape[0]
    DK = lg.shape[1]
    hk = DK // H
    hv = hnw.shape[1]
    DV = hv * H
    C = GLA_CHUNK
    NC = Lp // C
    scale = float(hk) ** -0.5
    kq, kv = DK // hk, (2 * DK) // hv
    kr = kv + H

    def body(q_ref, k_ref, v_ref, r_ref, lg_ref, w_ref, o_ref, y_ref, s_ref, S_scr, A_scr):
        c = pl.program_id(1)

        @pl.when(c == 0)
        def _():
            S_scr[...] = jnp.zeros_like(S_scr)

        q = q_ref[...] * scale
        k = k_ref[...]
        v = v_ref[...]
        b = _chunk_decay(lg_ref[...])
        bl = b[C - 1:C]
        S = S_scr[...]
        s_ref[0, 0] = S
        _intra_scores(q, k, b, A_scr)
        vb = v.astype(bf16)
        o = _dot((q * jnp.exp(b)).astype(bf16), S.astype(bf16)) + _dot(A_scr[...].astype(bf16), vb)
        kb = (k * jnp.exp(bl - b)).astype(bf16)
        S_scr[...] = jnp.exp(_col(bl)) * S + _dot(kb, vb, TN)
        o_ref[...] = o
        on = o * lax.rsqrt(jnp.mean(o * o, axis=-1, keepdims=True) + EPS) * w_ref[...]
        r = r_ref[...]
        y_ref[...] = (on * (r * _sigmoid(r))).astype(bf16)

    return pl.pallas_call(
        body, name=name, grid=(H, NC),
        in_specs=[pl.BlockSpec((C, hk), lambda h, c: (c, h)),
                  pl.BlockSpec((C, hk), lambda h, c: (c, kq + h)),
                  pl.BlockSpec((C, hv), lambda h, c: (c, kv + h)),
                  pl.BlockSpec((C, hv), lambda h, c: (c, kr + h)),
                  pl.BlockSpec((C, hk), lambda h, c: (c, h)),
                  pl.BlockSpec((1, hv), lambda h, c: (0, 0))],
        out_specs=[pl.BlockSpec((C, hv), lambda h, c: (c, h)), pl.BlockSpec((C, hv), lambda h, c: (c, h)),
                   pl.BlockSpec((1, 1, hk, hv), lambda h, c: (h, c, 0, 0))],
        out_shape=[jax.ShapeDtypeStruct((Lp, DV), f32), jax.ShapeDtypeStruct((Lp, DV), bf16),
                   jax.ShapeDtypeStruct((H, NC, hk, hv), f32)],
        scratch_shapes=[pltpu.VMEM((hk, hv), f32), pltpu.VMEM((C, C), f32)],
        compiler_params=_cparams(("parallel", "arbitrary")),
    )(proj, proj, proj, proj, lg, hnw)


def _gla_bwd(dy, proj, lg, o, states, hnw, H, pad, name):
    Lp = proj.shape[0]
    DK = lg.shape[1]
    hk = DK // H
    hv = hnw.shape[1]
    DV = hv * H
    C = GLA_CHUNK
    NC = Lp // C
    scale = float(hk) ** -0.5
    kq, kv = DK // hk, (2 * DK) // hv
    kr = kv + H

    def body(dy_ref, q_ref, k_ref, v_ref, r_ref, lg_ref, o_ref, s_ref, sn_ref, w_ref,
             dq_ref, dk_ref, dv_ref, dr_ref, dlg_ref, dw_ref, dS_scr, A_scr, dq_s, dk_s):
        h = pl.program_id(0)
        cc = pl.program_id(1)
        c = NC - 1 - cc

        @pl.when(cc == 0)
        def _():
            dS_scr[...] = jnp.zeros_like(dS_scr)

        @pl.when((cc == 0) & (h == 0))
        def _():
            dw_ref[...] = jnp.zeros_like(dw_ref)

        keep = (c * C + lax.broadcasted_iota(jnp.int32, (C, 1), 0)) >= pad
        w = w_ref[...]
        o_ = o_ref[...]
        rs = lax.rsqrt(jnp.mean(o_ * o_, axis=-1, keepdims=True) + EPS)
        ohat = o_ * rs
        r = r_ref[...]
        sg = _sigmoid(r)
        dy_ = dy_ref[...]
        d_on = dy_ * (r * sg)
        dr_ref[...] = jnp.where(keep, dy_ * (ohat * w) * (sg * (1.0 + r * (1.0 - sg))), 0.0).astype(bf16)
        dw_ref[...] += jnp.sum(d_on * ohat, axis=0, keepdims=True)
        d_oh = d_on * w
        do = rs * (d_oh - ohat * jnp.mean(d_oh * ohat, axis=-1, keepdims=True))
        dob = do.astype(bf16)
        q = q_ref[...] * scale
        k = k_ref[...]
        v = v_ref[...]
        vb = v.astype(bf16)
        b = _chunk_decay(lg_ref[...])
        bl = b[C - 1:C]
        eb = jnp.exp(b)
        ekb = jnp.exp(bl - b)
        S = s_ref[0, 0]
        dS = dS_scr[...]
        dSb = dS.astype(bf16)
        _intra_scores(q, k, b, A_scr)
        ri = lax.broadcasted_iota(jnp.int32, (C, C), 0)
        ci = lax.broadcasted_iota(jnp.int32, (C, C), 1)
        dA = jnp.where(ri >= ci, _dot(dob, vb, NT), 0.0)
        kb = (k * ekb).astype(bf16)
        qb = (q * eb).astype(bf16)
        dv = _dot(A_scr[...].astype(bf16), dob, TN) + _dot(kb, dSb)
        dq_s[...] = _dot(dob, S.astype(bf16), NT) * eb
        dk_s[...] = _dot(vb, dSb, NT) * ekb
        dS_scr[...] = _dot(qb, dob, TN) + jnp.exp(_col(bl)) * dS
        _intra_grads(q, k, b, dA, dq_s, dk_s)
        dq = dq_s[...]
        dk = dk_s[...]
        Dm = q * dq - k * dk
        after = _dot(jnp.ones((8, hv), f32), sn_ref[0, 0] * dS, NT, HI)[0:1]
        dlg = _dot(jnp.where(ri <= ci, 1.0, 0.0).astype(f32), Dm, NN, HI) + after
        dlg_ref[...] = jnp.where(keep, dlg, 0.0)
        dq_ref[...] = jnp.where(keep, dq * scale, 0.0).astype(bf16)
        dk_ref[...] = jnp.where(keep, dk, 0.0).astype(bf16)
        dv_ref[...] = jnp.where(keep, dv, 0.0).astype(bf16)

    rev = lambda h, cc: NC - 1 - cc
    return pl.pallas_call(
        body, name=name, grid=(H, NC),
        in_specs=[pl.BlockSpec((C, hv), lambda h, cc: (rev(h, cc), h)),
                  pl.BlockSpec((C, hk), lambda h, cc: (rev(h, cc), h)),
                  pl.BlockSpec((C, hk), lambda h, cc: (rev(h, cc), kq + h)),
                  pl.BlockSpec((C, hv), lambda h, cc: (rev(h, cc), kv + h)),
                  pl.BlockSpec((C, hv), lambda h, cc: (rev(h, cc), kr + h)),
                  pl.BlockSpec((C, hk), lambda h, cc: (rev(h, cc), h)),
                  pl.BlockSpec((C, hv), lambda h, cc: (rev(h, cc), h)),
                  pl.BlockSpec((1, 1, hk, hv), lambda h, cc: (h, rev(h, cc), 0, 0)),
                  pl.BlockSpec((1, 1, hk, hv), lambda h, cc: (h, jnp.minimum(rev(h, cc) + 1, NC - 1), 0, 0)),
                  pl.BlockSpec((1, hv), lambda h, cc: (0, 0))],
        out_specs=[pl.BlockSpec((C, hk), lambda h, cc: (rev(h, cc), h)),
                   pl.BlockSpec((C, hk), lambda h, cc: (rev(h, cc), h)),
                   pl.BlockSpec((C, hv), lambda h, cc: (rev(h, cc), h)),
                   pl.BlockSpec((C, hv), lambda h, cc: (rev(h, cc), h)),
                   pl.BlockSpec((C, hk), lambda h, cc: (rev(h, cc), h)),
                   pl.BlockSpec((1, hv), lambda h, cc: (0, 0))],
        out_shape=[jax.ShapeDtypeStruct((Lp, DK), bf16), jax.ShapeDtypeStruct((Lp, DK), bf16),
                   jax.ShapeDtypeStruct((Lp, DV), bf16), jax.ShapeDtypeStruct((Lp, DV), bf16),
                   jax.ShapeDtypeStruct((Lp, DK), f32), jax.ShapeDtypeStruct((1, hv), f32)],
        scratch_shapes=[pltpu.VMEM((hk, hv), f32), pltpu.VMEM((C, C), f32),
                        pltpu.VMEM((C, hk), f32), pltpu.VMEM((C, hk), f32)],
        compiler_params=_cparams(("arbitrary", "arbitrary")),
    )(dy, proj, proj, proj, proj, lg, o, states, states, hnw)


def _window_sums(x, back):
    n = x.shape[0]
    out = []
    s = x
    for w in (1, 2, 4, 8):
        s = s + pltpu.roll(s, w if back else n - w, 0)
        out.append(s)
    return out


def _pool_windows(hn, pad, n_real, name):
    Lp, D = hn.shape
    GW = D // POOL_GROUPS
    cb = min(GW, 256)
    per = GW // cb

    def body(h_ref, p_ref):
        g = pl.program_id(0) // per
        x = h_ref[...]
        s2, s4, s8, s16 = _window_sums(x, True)
        sel = jnp.where(g == 0, s2, jnp.where(g == 1, s4, jnp.where(g == 2, s8, s16)))
        win = jnp.left_shift(2, g).astype(f32)
        rows = lax.broadcasted_iota(jnp.int32, (Lp, 1), 0)
        t = (rows - pad).astype(f32)
        cnt = jnp.minimum(jnp.maximum(t, 0.0) + 1.0, win)
        p_ref[...] = jnp.where(rows >= pad, sel / cnt - x, 0.0).astype(bf16)

    return pl.pallas_call(
        body, name=name, grid=(D // cb,),
        in_specs=[pl.BlockSpec((Lp, cb), lambda i: (0, i))],
        out_specs=pl.BlockSpec((Lp, cb), lambda i: (0, i)),
        out_shape=jax.ShapeDtypeStruct((Lp, D), bf16),
        compiler_params=_cparams(("parallel",)),
    )(hn)


def _pool_windows_bwd(dp, pad, name):
    Lp, D = dp.shape
    GW = D // POOL_GROUPS
    cb = min(GW, 256)
    per = GW // cb

    def body(dp_ref, dh_ref):
        g = pl.program_id(0) // per
        rows = lax.broadcasted_iota(jnp.int32, (Lp, 1), 0)
        d = jnp.where(rows >= pad, dp_ref[...], 0.0)
        win = jnp.left_shift(2, g).astype(f32)
        t = (rows - pad).astype(f32)
        cnt = jnp.minimum(jnp.maximum(t, 0.0) + 1.0, win)
        s2, s4, s8, s16 = _window_sums(d / cnt, False)
        sel = jnp.where(g == 0, s2, jnp.where(g == 1, s4, jnp.where(g == 2, s8, s16)))
        dh_ref[...] = jnp.where(rows >= pad, sel - d, 0.0)

    return pl.pallas_call(
        body, name=name, grid=(D // cb,),
        in_specs=[pl.BlockSpec((Lp, cb), lambda i: (0, i))],
        out_specs=pl.BlockSpec((Lp, cb), lambda i: (0, i)),
        out_shape=jax.ShapeDtypeStruct((Lp, D), f32),
        compiler_params=_cparams(("parallel",)),
    )(dp)


def _pool_mix_fwd(xs, pooled, w, bias, scale, pad, name):
    Lp, D = xs.shape
    GW = D // POOL_GROUPS
    tm = _tile(Lp, 1056)

    def body(x_ref, p_ref, w_ref, b_ref, s_ref, o_ref):
        z = _dot(p_ref[...], w_ref[0]) + b_ref[...]
        keep = _row_ids(pl.program_id(1), tm) >= pad
        o_ref[...] = x_ref[...] + jnp.where(keep, z * s_ref[...], 0.0)

    blk = pl.BlockSpec((tm, GW), lambda g, i: (i, g))
    vec = pl.BlockSpec((1, GW), lambda g, i: (0, g))
    return pl.pallas_call(
        body, name=name, grid=(POOL_GROUPS, Lp // tm),
        in_specs=[blk, blk, pl.BlockSpec((1, GW, GW), lambda g, i: (g, 0, 0)), vec, vec],
        out_specs=blk, out_shape=jax.ShapeDtypeStruct((Lp, D), f32),
        compiler_params=_cparams(("parallel", "parallel")),
    )(xs, pooled, w, bias, scale)


def _pool_mix_bwd(dY, pooled, w, bias, scale, pad, name):
    Lp, D = dY.shape
    GW = D // POOL_GROUPS
    tm = _tile(Lp, 1056)
    nm = Lp // tm

    def body(dY_ref, p_ref, w_ref, b_ref, s_ref, dp_ref, dw_ref, db_ref, ds_ref, acc):
        i = pl.program_id(1)

        @pl.when(i == 0)
        def _():
            acc[...] = jnp.zeros_like(acc)
            db_ref[...] = jnp.zeros_like(db_ref)
            ds_ref[...] = jnp.zeros_like(ds_ref)

        keep = _row_ids(i, tm) >= pad
        dY_ = jnp.where(keep, dY_ref[...], 0.0)
        p = p_ref[...]
        z = _dot(p, w_ref[0]) + b_ref[...]
        ds_ref[...] += jnp.sum(dY_ * z, axis=0, keepdims=True)
        dz = dY_ * s_ref[...]
        db_ref[...] += jnp.sum(dz, axis=0, keepdims=True)
        dzb = dz.astype(bf16)
        acc[...] += _dot(p, dzb, TN)
        dp_ref[...] = _dot(dzb, w_ref[0], NT)

        @pl.when(i == nm - 1)
        def _():
            dw_ref[0] = acc[...].astype(bf16)

    blk = pl.BlockSpec((tm, GW), lambda g, i: (i, g))
    vec = pl.BlockSpec((1, GW), lambda g, i: (0, g))
    wsp = pl.BlockSpec((1, GW, GW), lambda g, i: (g, 0, 0))
    return pl.pallas_call(
        body, name=name, grid=(POOL_GROUPS, nm),
        in_specs=[blk, blk, wsp, vec, vec], out_specs=[blk, wsp, vec, vec],
        out_shape=[jax.ShapeDtypeStruct((Lp, D), f32), jax.ShapeDtypeStruct((POOL_GROUPS, GW, GW), bf16),
                   jax.ShapeDtypeStruct((1, D), f32), jax.ShapeDtypeStruct((1, D), f32)],
        scratch_shapes=[pltpu.VMEM((GW, GW), f32)],
        compiler_params=_cparams(("parallel", "arbitrary")),
    )(dY, pooled, w, bias, scale)


def _loss_head(xs, target, g, first, name):
    Lp, D = xs.shape
    tm = GLA_CHUNK
    off = first // tm

    def body(x_ref, t_ref, g_ref, loss_ref, dxs_ref, dg_ref):
        i = pl.program_id(0)

        @pl.when(i == 0)
        def _():
            loss_ref[...] = jnp.zeros_like(loss_ref)
            dg_ref[...] = jnp.zeros_like(dg_ref)

        @pl.when(i < off)
        def _():
            dxs_ref[...] = jnp.zeros_like(dxs_ref)

        @pl.when(i >= off)
        def _():
            x = x_ref[...]
            rstd = lax.rsqrt(jnp.mean(x * x, axis=-1, keepdims=True) + EPS)
            xhat = x * rstd
            gg = g_ref[...]
            err = xhat * gg - t_ref[...]
            loss_ref[...] += 0.5 * jnp.sum(jnp.mean(err * err, axis=-1, keepdims=True))
            dy = err * (1.0 / D)
            dg_ref[...] += jnp.sum(dy * xhat, axis=0, keepdims=True)
            dxh = dy * gg
            dxs_ref[...] = rstd * (dxh - xhat * jnp.mean(dxh * xhat, axis=-1, keepdims=True))

    row = pl.BlockSpec((tm, D), lambda i: (i, 0))
    return pl.pallas_call(
        body, name=name, grid=(Lp // tm,),
        in_specs=[row, pl.BlockSpec((tm, D), lambda i: (jnp.maximum(i - off, 0), 0)), pl.BlockSpec((1, D), lambda i: (0, 0))],
        out_specs=[pl.BlockSpec((8, LANES), lambda i: (0, 0)), row, pl.BlockSpec((1, D), lambda i: (0, 0))],
        out_shape=[jax.ShapeDtypeStruct((8, LANES), f32), jax.ShapeDtypeStruct((Lp, D), f32),
                   jax.ShapeDtypeStruct((1, D), f32)],
        compiler_params=_cparams(("arbitrary",)),
    )(xs, target, g)


def _adam_math(w, g, m, v):
    m2 = ADAM_B1 * m + (1.0 - ADAM_B1) * g
    v2 = ADAM_B2 * v + (1.0 - ADAM_B2) * (g * g)
    m_hat = m2 / (1.0 - ADAM_B1 ** ADAM_STEP)
    v_hat = v2 / (1.0 - ADAM_B2 ** ADAM_STEP)
    delta = -ADAM_LR * (m_hat / (jnp.sqrt(v_hat) + ADAM_EPS) + ADAM_WD * w)
    return delta, m2, v2


def _adamw(w, m, v, unit, own, own_idx, recv, prev, name):
    U, R, C = w.shape
    tr = _tile(R, 256, 8)
    n_recv = 0 if recv is None else recv.shape[0]

    def body(idx_ref, w_ref, m_ref, v_ref, own_ref, *rest):
        rest = list(rest)
        recv_refs = [rest.pop(0) for _ in range(n_recv)]
        if prev is not None:
            rest = rest[4:]
        g_ref, d_ref, m2_ref, v2_ref = rest
        g = own_ref[0].astype(f32)
        for r_ref in recv_refs:
            g = g + r_ref[0].astype(f32)
        delta, m2, v2 = _adam_math(w_ref[0], g, m_ref[0], v_ref[0])
        g_ref[0] = g
        d_ref[0] = delta
        m2_ref[0] = m2
        v2_ref[0] = v2

    blk = pl.BlockSpec((1, tr, C), lambda i, idx: (unit, i, 0))
    in_specs = [blk, blk, blk, pl.BlockSpec((1, tr, C), lambda i, idx: (idx[0], i, 0))]
    args = [w, m, v, own]
    for p in range(n_recv):
        in_specs.append(pl.BlockSpec((1, tr, C), lambda i, idx, p=p: (p, i, 0)))
        args.append(recv)
    aliases = {}
    if prev is not None:
        for t in range(4):
            aliases[1 + len(args) + t] = t
        in_specs += [ANY] * 4
        args += list(prev)
    out = jax.ShapeDtypeStruct((U, R, C), f32)
    return pl.pallas_call(
        body, name=name,
        grid_spec=pltpu.PrefetchScalarGridSpec(
            num_scalar_prefetch=1, grid=(R // tr,), in_specs=in_specs, out_specs=[blk] * 4),
        out_shape=[out] * 4, input_output_aliases=aliases,
        compiler_params=_cparams(("parallel",)),
    )(own_idx, *args)


def _place():
    return lax.axis_index("x"), lax.axis_index("y"), lax.axis_index("c")


HBM = pl.BlockSpec(memory_space=pltpu.HBM)
SEM = pl.BlockSpec(memory_space=pltpu.SEMAPHORE)
VMEM_SPEC = pl.BlockSpec(memory_space=pltpu.VMEM)
EFFECT = pltpu.SideEffectType.DATAFLOW_SIDE_EFFECTING
TOKEN = jax.ShapeDtypeStruct((8, LANES), f32)


def _hbm(x):
    return pltpu.with_memory_space_constraint(x, pltpu.HBM)


def _hbm_like(xs):
    return [pltpu.HBM(x.shape, x.dtype) for x in xs]


def _tie(x, token):
    return lax.optimization_barrier((x, token))[0]


def _slot(px, py, pc):
    return 4 * px + 2 * py + pc


def _place_own(shards, name):
    n = len(shards)

    def body(*refs):
        ins, outs, sems = refs[:n], refs[n:2 * n], refs[2 * n]
        me = _slot(*_place())
        cps = [pltpu.make_async_copy(ins[a], outs[a].at[me], sems.at[a]) for a in range(n)]
        for cp in cps:
            cp.start()
        for cp in cps:
            cp.wait()

    return pl.pallas_call(
        body, name=name, in_specs=[ANY] * n, out_specs=[ANY] * n,
        out_shape=[jax.ShapeDtypeStruct((N_DEV,) + s.shape, s.dtype) for s in shards],
        scratch_shapes=[pltpu.SemaphoreType.DMA((n,))],
    )(*shards)


def _gather_start(shards, bufs, after, name):
    n = len(shards)

    def body(*refs):
        ins, land = refs[:n], refs[n:2 * n]
        send, recv = refs[2 * n + 1], refs[2 * n + 2]
        token = refs[-1]
        x, y, c = _place()
        to = [(x, y, 1 - c), (1 - x, y, c), (x, 1 - y, c), (1 - x, 1 - y, c)]
        for a in range(n):
            for k, dev in enumerate(to):
                pltpu.make_async_remote_copy(
                    src_ref=ins[a], dst_ref=land[a].at[_slot(x, y, c)], send_sem=send.at[4 * a + k], recv_sem=recv.at[4 * a + k],
                    device_id=dev, device_id_type=MESH).start()
        token[...] = jnp.zeros_like(token)

    out = pl.pallas_call(
        body, name=name,
        in_specs=[HBM] * (2 * n) + [ANY],
        out_specs=[SEM, SEM] + [HBM] * (2 * n) + [VMEM_SPEC],
        out_shape=[pltpu.SemaphoreType.DMA((4 * n,)), pltpu.SemaphoreType.DMA((4 * n,))] + _hbm_like(shards) + _hbm_like(bufs) + [TOKEN],
        input_output_aliases={i: 2 + i for i in range(2 * n)},
        compiler_params=pltpu.CompilerParams(has_side_effects=EFFECT),
    )(*[_hbm(s) for s in shards], *[_hbm(b) for b in bufs], after)
    return dict(send1=out[0], recv1=out[1], shards=list(out[2:2 + n]), bufs=list(out[2 + n:2 + 2 * n]), token=out[-1])


def _gather_mid(h, after, name):
    n = len(h["bufs"])

    def body(*refs):
        land, recv1 = refs[:n], refs[n]
        send2, recv2 = refs[n + 2], refs[n + 3]
        token = refs[-1]
        x, y, c = _place()
        chips = [(1 - x, y), (x, 1 - y), (1 - x, 1 - y)]
        for j, (px, py) in enumerate(chips):
            for a in range(n):
                blk = land[a].at[_slot(px, py, c)]
                pltpu.make_async_remote_copy(
                    src_ref=blk, dst_ref=blk, send_sem=send2.at[3 * a + j], recv_sem=recv1.at[4 * a + 1 + j],
                    device_id=(px, py, c), device_id_type=MESH).wait_recv()
                pltpu.make_async_remote_copy(
                    src_ref=blk, dst_ref=blk, send_sem=send2.at[3 * a + j], recv_sem=recv2.at[3 * a + j],
                    device_id=(x, y, 1 - c), device_id_type=MESH).start()
        token[...] = jnp.zeros_like(token)

    out = pl.pallas_call(
        body, name=name,
        in_specs=[HBM] * n + [SEM, ANY],
        out_specs=[SEM, SEM] + [HBM] * n + [VMEM_SPEC],
        out_shape=[pltpu.SemaphoreType.DMA((3 * n,)), pltpu.SemaphoreType.DMA((3 * n,))] + _hbm_like(h["bufs"]) + [TOKEN],
        input_output_aliases={i: 2 + i for i in range(n)},
        compiler_params=pltpu.CompilerParams(has_side_effects=EFFECT),
    )(*h["bufs"], h["recv1"], after)
    h.update(send2=out[0], recv2=out[1], bufs=list(out[2:2 + n]), token=out[-1])
    return h


def _gather_end(h, after, name):
    n = len(h["bufs"])

    def body(*refs):
        ins, land = refs[:n], refs[n:2 * n]
        send1, recv1, send2, recv2 = refs[2 * n:2 * n + 4]
        x, y, c = _place()
        chips = [(1 - x, y), (x, 1 - y), (1 - x, 1 - y)]
        sib = (x, y, 1 - c)
        for a in range(n):
            mine = land[a].at[_slot(x, y, c)]
            for k in range(4):
                pltpu.make_async_remote_copy(
                    src_ref=ins[a], dst_ref=mine, send_sem=send1.at[4 * a + k], recv_sem=recv1.at[4 * a + k],
                    device_id=sib, device_id_type=MESH).wait_send()
            theirs = land[a].at[_slot(x, y, 1 - c)]
            pltpu.make_async_remote_copy(
                src_ref=ins[a], dst_ref=theirs, send_sem=send1.at[4 * a], recv_sem=recv1.at[4 * a],
                device_id=sib, device_id_type=MESH).wait_recv()
            for j, (px, py) in enumerate(chips):
                sent = land[a].at[_slot(px, py, c)]
                got = land[a].at[_slot(px, py, 1 - c)]
                pltpu.make_async_remote_copy(
                    src_ref=sent, dst_ref=sent, send_sem=send2.at[3 * a + j], recv_sem=recv2.at[3 * a + j],
                    device_id=sib, device_id_type=MESH).wait_send()
                pltpu.make_async_remote_copy(
                    src_ref=sent, dst_ref=got, send_sem=send2.at[3 * a + j], recv_sem=recv2.at[3 * a + j],
                    device_id=sib, device_id_type=MESH).wait_recv()

    out = pl.pallas_call(
        body, name=name,
        in_specs=[HBM] * (2 * n) + [SEM] * 4 + [ANY],
        out_specs=[HBM] * n,
        out_shape=_hbm_like(h["bufs"]),
        input_output_aliases={n + i: i for i in range(n)},
        compiler_params=pltpu.CompilerParams(has_side_effects=EFFECT),
    )(*h["shards"], *h["bufs"], h["send1"], h["recv1"], h["send2"], h["recv2"], after)
    return list(out)


def _peer_plan(kind, x, y, c):
    if kind == "pair":
        return [(2 * q + (1 - c), q, (x, y, 1 - c)) for q in range(4)]
    chips = [(1 - x, y), (x, 1 - y), (1 - x, 1 - y)]
    return [(2 * px + py, k, (px, py, c)) for k, (px, py) in enumerate(chips)]


def _exchange_start(kind, srcs, after, name):
    n = len(srcs)
    K = 4 if kind == "pair" else 3
    lands = [_hbm(lax.empty((K,) + s.shape[1:], s.dtype)) for s in srcs]

    def body(*refs):
        ins, land = refs[:n], refs[n:2 * n]
        send, recv = refs[2 * n + 1], refs[2 * n + 2]
        token = refs[-1]
        for a in range(n):
            for k, (si, di, dev) in enumerate(_peer_plan(kind, *_place())):
                pltpu.make_async_remote_copy(
                    src_ref=ins[a].at[si], dst_ref=land[a].at[di], send_sem=send.at[K * a + k], recv_sem=recv.at[K * a + k],
                    device_id=dev, device_id_type=MESH).start()
        token[...] = jnp.zeros_like(token)

    out = pl.pallas_call(
        body, name=name,
        in_specs=[HBM] * (2 * n) + [ANY],
        out_specs=[SEM, SEM] + [HBM] * (2 * n) + [VMEM_SPEC],
        out_shape=[pltpu.SemaphoreType.DMA((K * n,)), pltpu.SemaphoreType.DMA((K * n,))] + _hbm_like(srcs) + _hbm_like(lands) + [TOKEN],
        input_output_aliases={i: 2 + i for i in range(2 * n)},
        compiler_params=pltpu.CompilerParams(has_side_effects=EFFECT),
    )(*[_hbm(s) for s in srcs], *lands, after)
    return dict(kind=kind, send=out[0], recv=out[1], srcs=list(out[2:2 + n]), lands=list(out[2 + n:2 + 2 * n]), token=out[-1])


def _exchange_wait(h, after, name):
    n = len(h["srcs"])
    kind = h["kind"]
    K = 4 if kind == "pair" else 3

    def body(*refs):
        ins, land = refs[:n], refs[n:2 * n]
        send, recv = refs[2 * n], refs[2 * n + 1]
        for a in range(n):
            for k, (si, di, dev) in enumerate(_peer_plan(kind, *_place())):
                cp = pltpu.make_async_remote_copy(
                    src_ref=ins[a].at[si], dst_ref=land[a].at[di], send_sem=send.at[K * a + k], recv_sem=recv.at[K * a + k],
                    device_id=dev, device_id_type=MESH)
                cp.wait_send()
                cp.wait_recv()

    out = pl.pallas_call(
        body, name=name,
        in_specs=[HBM] * (2 * n) + [SEM, SEM, ANY],
        out_specs=[HBM] * (2 * n),
        out_shape=_hbm_like(h["srcs"]) + _hbm_like(h["lands"]),
        input_output_aliases={i: i for i in range(2 * n)},
        compiler_params=pltpu.CompilerParams(has_side_effects=EFFECT),
    )(*h["srcs"], *h["lands"], h["send"], h["recv"], after)
    return list(out[:n]), list(out[n:])


def _pair_add(g, got, c_idx, name):
    _, R, C = g.shape
    tr = _tile(R, 512, 16)

    def body(c_ref, a_ref, b_ref, o_ref):
        o_ref[0] = (a_ref[0].astype(f32) + b_ref[0].astype(f32)).astype(o_ref.dtype)

    return pl.pallas_call(
        body, name=name,
        grid_spec=pltpu.PrefetchScalarGridSpec(
            num_scalar_prefetch=1, grid=(4, R // tr),
            in_specs=[pl.BlockSpec((1, tr, C), lambda q, i, c: (2 * q + c[0], i, 0)),
                      pl.BlockSpec((1, tr, C), lambda q, i, c: (q, i, 0))],
            out_specs=pl.BlockSpec((1, tr, C), lambda q, i, c: (q, i, 0))),
        out_shape=jax.ShapeDtypeStruct((4, R, C), g.dtype),
        compiler_params=_cparams(("parallel", "parallel")),
    )(c_idx, g, got)


def _small_exchange(send, gather, name):
    R = send.shape[-2]

    def body(in_ref, out_ref, send_sems, recv_sems):
        x, y, c = _place()
        me = 4 * x + 2 * y + c
        out_ref[me] = in_ref[...] if gather else in_ref[me]
        cps = []
        for k in range(1, N_DEV):
            px, py, pc = x ^ ((k >> 2) & 1), y ^ ((k >> 1) & 1), c ^ (k & 1)
            src = in_ref if gather else in_ref.at[4 * px + 2 * py + pc]
            cps.append(pltpu.make_async_remote_copy(
                src_ref=src, dst_ref=out_ref.at[me],
                send_sem=send_sems.at[k - 1], recv_sem=recv_sems.at[k - 1],
                device_id=(px, py, pc), device_id_type=MESH))
        for cp in cps:
            cp.start()
        for cp in cps:
            cp.wait()

    return pl.pallas_call(
        body, name=name,
        in_specs=[pl.BlockSpec(memory_space=pltpu.VMEM)], out_specs=pl.BlockSpec(memory_space=pltpu.VMEM),
        out_shape=jax.ShapeDtypeStruct((N_DEV, R, LANES), f32),
        scratch_shapes=[pltpu.SemaphoreType.DMA((N_DEV - 1,)), pltpu.SemaphoreType.DMA((N_DEV - 1,))],
    )(send)


def _sum_blocks(blocks, name):
    def body(in_ref, o_ref):
        s = in_ref[0]
        for d in range(1, N_DEV):
            s = s + in_ref[d]
        o_ref[0] = s

    return pl.pallas_call(body, name=name, out_shape=jax.ShapeDtypeStruct((1,) + blocks.shape[1:], f32))(blocks)


def _rows(n):
    return -(-n // LANES)


def _pack(arrs, total_rows):
    parts = []
    for a in arrs:
        flat = a.reshape(-1).astype(f32)
        parts.append(jnp.pad(flat, (0, _rows(flat.size) * LANES - flat.size)))
    flat = jnp.concatenate(parts)
    return jnp.pad(flat, (0, total_rows * LANES - flat.size)).reshape(total_rows, LANES)


def _unpack(packed, shapes):
    lead = packed.shape[:-2]
    flat = packed.reshape(lead + (-1,))
    out, pos = [], 0
    for s in shapes:
        n = 1
        for d in s:
            n *= d
        out.append(flat[..., pos:pos + n].reshape(lead + tuple(s)))
        pos += _rows(n) * LANES
    return out


def _to_shards(full, axis):
    s = full.shape
    return jnp.moveaxis(full.reshape(s[:axis] + (N_DEV, s[axis] // N_DEV) + s[axis + 1:]), axis, 0)


def _from_shards(sh, axis):
    m = jnp.moveaxis(sh, 0, axis)
    s = m.shape
    return m.reshape(s[:axis] + (s[axis] * s[axis + 1],) + s[axis + 2:])


def kernel(x, meta, ffn_norm, ffn_w_gate, ffn_w_up, ffn_w_down, gla_norm, gla_w_in, gla_w_lr, gla_b_lr, gla_head_norm, gla_w_out, pool_norm, pool_w, pool_b, pool_scale, final_norm, loss_target, m_meta, m_ffn_norm, m_ffn_w_gate, m_ffn_w_up, m_ffn_w_down, m_gla_norm, m_gla_w_in, m_gla_w_lr, m_gla_b_lr, m_gla_head_norm, m_gla_w_out, m_pool_norm, m_pool_w, m_pool_b, m_pool_scale, m_final_norm, v_meta, v_ffn_norm, v_ffn_w_gate, v_ffn_w_up, v_ffn_w_down, v_gla_norm, v_gla_w_in, v_gla_w_lr, v_gla_b_lr, v_gla_head_norm, v_gla_w_out, v_pool_norm, v_pool_w, v_pool_b, v_pool_scale, v_final_norm):
    H = GLA_HEADS
    _, SEQ, D = x.shape
    Fs = ffn_w_gate.shape[-1]
    DK, DV = D // 2, D
    hv = DV // H
    GW = D // POOL_GROUPS
    INW = 2 * DK + 2 * DV + GATE_RANK
    NPK = 2 * DK + 2 * DV + GATE_PAD
    pad = (-N_META) % GLA_CHUNK
    first = pad + N_META
    Lp = first + SEQ
    n_units = ffn_w_gate.shape[0] * ffn_w_gate.shape[1]
    assert first % GLA_CHUNK == 0 and Lp % GLA_CHUNK == 0 and pad >= POOL_GROUPS * 4

    px, py, pc = _place()
    c_idx = jnp.reshape(pc, (1,)).astype(jnp.int32)
    q_idx = jnp.reshape(2 * px + py, (1,)).astype(jnp.int32)
    zero_idx = jnp.zeros((1,), jnp.int32)

    small_sh = [meta, ffn_norm, gla_w_lr, pool_norm, pool_b, pool_scale]
    small_axis = [1, 2, 2, 1, 2, 1]
    sh_shapes = [a.shape for a in small_sh]
    sh_rows = -(-sum(_rows(a.size) for a in small_sh) // 8) * 8
    gathered = _small_exchange(_pack(small_sh, sh_rows), True, "small_gather")
    meta_f, ffn_norm_f, wlr_f, pool_norm_f, pool_b_f, pool_scale_f = [
        _from_shards(a, ax) for a, ax in zip(_unpack(gathered, sh_shapes), small_axis)]
    ffn_norm_f = ffn_norm_f.reshape(n_units, 1, D)
    wlr128 = jnp.pad(wlr_f[0], ((0, GATE_PAD - GATE_RANK), (0, 0)))

    wg_l = ffn_w_gate.reshape(n_units, D, Fs).astype(bf16)
    wu_l = ffn_w_up.reshape(n_units, D, Fs).astype(bf16)
    wd_l = ffn_w_down.reshape(n_units, Fs, D).astype(bf16)
    ffn_shards = [[wg_l[u], wu_l[u], wd_l[u]] for u in range(n_units)]
    mixer_shards = [gla_w_in[0].astype(bf16), gla_w_out[0].astype(bf16), pool_w[0].astype(bf16)]
    gather_order = [("ffn0", ffn_shards[0]), ("mixers", mixer_shards)] + [(f"ffn{u}", ffn_shards[u]) for u in range(1, n_units)]
    c_lr = 2 * DK + DV
    gate_blk = (2 * DK + 2 * DV) // GATE_PAD

    def gather_begin(i, after):
        tag, shards = gather_order[i]
        return _gather_start(shards, _place_own(shards, f"own_{tag}"), after, f"gather_start_{tag}")

    def gather_next(i, h, after):
        tag = gather_order[i][0]
        h = _gather_mid(h, after, f"gather_mid_{tag}")
        nxt = gather_begin(i + 1, h["token"]) if i + 1 < len(gather_order) else None
        done = _gather_end(h, h["token"] if nxt is None else nxt["token"], f"gather_end_{tag}")
        return done, nxt

    xs = jnp.concatenate([jnp.zeros((pad, D), f32), meta_f, x[0]], axis=0)
    saved = {}
    ffn_w = [None] * n_units

    def ffn_f(u, xs):
        out, h, G, U = _ffn_fwd(xs, ffn_norm_f[u], *ffn_w[u], name=f"ffn_fwd{u}")
        saved[("ffn", u)] = (xs, h, G, U)
        return out

    def gla_f(xs, win_p, wout_full):
        hn = _rms_fwd(xs, gla_norm, bf16, "gla_norm_fwd")
        proj = _mm(hn, win_p, "nn", f32, "gla_proj", tm=1056, tn=896, tk=2048)
        lg = _gate_fwd(proj, wlr128, gla_b_lr, pad, gate_blk, "gla_gate_fwd")
        o, y, states = _gla_fwd(proj, lg, gla_head_norm, H, "gla_core_fwd")
        out = _mm(y, wout_full, "nn", f32, "gla_out", tm=1056, tn=512, tk=2048, residual=xs)
        saved["gla"] = (xs, hn, proj, lg, o, y, states)
        return out

    def pool_f(xs, wpool_full):
        hn = _rms_fwd(xs, pool_norm_f, f32, "pool_norm_fwd")
        pooled = _pool_windows(hn, pad, Lp - pad, "pool_windows_fwd")
        out = _pool_mix_fwd(xs, pooled, wpool_full, pool_b_f.reshape(1, D), pool_scale_f, pad, "pool_mix_fwd")
        saved["pool"] = (xs, pooled)
        return out

    depth = ffn_w_gate.shape[0]
    assert depth == 2 and n_units == 4
    h = gather_begin(0, gathered)
    ffn_w[0], h = gather_next(0, h, h["token"])
    xs = ffn_f(0, xs)
    (win_g, wout_g, wpool_g), h = gather_next(1, h, xs)
    win_full = _from_shards(win_g, 1)
    win_p = jnp.concatenate([win_full[:, :c_lr], win_full[:, c_lr + GATE_RANK:], win_full[:, c_lr:c_lr + GATE_RANK],
                             jnp.zeros((D, GATE_PAD - GATE_RANK), bf16)], axis=1)
    wout_full = wout_g.reshape(DV, D)
    wpool_full = _from_shards(wpool_g, 1)
    xs = gla_f(xs, win_p, wout_full)
    ffn_w[1], h = gather_next(2, h, xs)
    xs = ffn_f(1, xs)
    ffn_w[2], h = gather_next(3, h, xs)
    xs = ffn_f(2, xs)
    xs = pool_f(xs, wpool_full)
    ffn_w[3], h = gather_next(4, h, xs)
    xs = ffn_f(3, xs)
    loss_part, dxs, d_final = _loss_head(xs, loss_target[0], final_norm.reshape(1, D), first, "loss_head")

    class Reduce:
        def __init__(self, tag, grads):
            self.tag = tag
            self.h = _exchange_start("pair", grads, grads[0], f"pair_start_{tag}")
            self.token = self.h["token"]

        def mid(self, after):
            grads, got = _exchange_wait(self.h, after, f"pair_wait_{self.tag}")
            self.sums = [_pair_add(g, r, c_idx, f"pair_add_{self.tag}{a}") for a, (g, r) in enumerate(zip(grads, got))]
            self.h = _exchange_start("chips", self.sums, self.sums[-1], f"chips_start_{self.tag}")
            self.token = self.h["token"]

        def end(self, after):
            sums, recv = _exchange_wait(self.h, after, f"chips_wait_{self.tag}")
            return list(zip(sums, recv))

    d_ffn_norm = [None] * n_units
    small_grads = {}

    def ffn_b(u, dY, prev):
        xs_in, h_, G, U = saved[("ffn", u)]
        if prev is not None:
            dY = _tie(dY, prev.token)
        dxs, dyh, dG, dU, A, dg = _ffn_bwd_dgrad(dY, xs_in, ffn_norm_f[u], *ffn_w[u], G, U, pad, f"ffn_dgrad{u}")
        if prev is not None:
            prev.mid(dxs)
            dyh = _tie(dyh, prev.token)
        dwg, dwu, dwd = _ffn_bwd_wgrad(h_, dyh, A, dG, dU, f"ffn_wgrad{u}")
        d_ffn_norm[u] = dg
        return dxs, Reduce(f"ffn{u}", [dwg, dwu, dwd])

    def gla_b(dY, prev):
        xs_in, hn, proj, lg, o, y, states = saved["gla"]
        dyb = _tie(dY, prev.token).astype(bf16)
        dy = _mm(dyb, wout_full, "nt", f32, "gla_out_dgrad", tm=1056, tn=512, tk=2048)
        dwout = _mm(y, dyb, "tn", bf16, "gla_out_wgrad", tm=1024, tn=1024, tk=528)
        prev.mid(dwout)
        dq, dk, dv, dr, dlg, dhw = _gla_bwd(_tie(dy, prev.token), proj, lg, o, states, gla_head_norm, H, pad, "gla_core_bwd")
        dlr, dwlr, dblr = _gate_bwd(dlg, proj, wlr128, gla_b_lr, pad, gate_blk, "gla_gate_bwd")
        dproj = jnp.concatenate([dq, dk, dv, dr, dlr], axis=1)
        dwin_p = _mm(hn, dproj, "tn", bf16, "gla_proj_wgrad", tm=1024, tn=896, tk=528)
        dhn = _mm(dproj, win_p, "nt", f32, "gla_proj_dgrad", tm=1056, tn=512, tk=896)
        dxs, dgn = _rms_bwd(dY, dhn, xs_in, gla_norm, pad, "gla_norm_bwd")
        c_r = 2 * DK + 2 * DV
        dwin = jnp.concatenate([dwin_p[:, :c_lr], dwin_p[:, c_r:c_r + GATE_RANK], dwin_p[:, c_lr:c_r]], axis=1)
        small_grads.update(gla_w_lr=dwlr[:GATE_RANK][None], gla_b_lr=dblr, gla_head_norm=dhw, gla_norm=dgn)
        return dxs, Reduce("gla", [_to_shards(dwin, 1), dwout.reshape(N_DEV, DV // N_DEV, D)])

    def pool_b_(dY, prev):
        xs_in, pooled = saved["pool"]
        dY_t = _tie(dY, prev.token)
        dp, dw, db, ds = _pool_mix_bwd(dY_t, pooled, wpool_full, pool_b_f.reshape(1, D), pool_scale_f, pad, "pool_mix_bwd")
        dhn = _pool_windows_bwd(dp, pad, "pool_windows_bwd")
        dxs, dgn = _rms_bwd(dY, dhn, xs_in, pool_norm_f, pad, "pool_norm_bwd")
        prev.mid(dxs)
        dws = _to_shards(dw, 1)
        small_grads.update(pool_b=db.reshape(1, POOL_GROUPS, GW), pool_scale=ds, pool_norm=dgn)
        red = Reduce("pool", [dws.reshape(N_DEV, POOL_GROUPS * GW // N_DEV, GW)])
        red.after_prev = prev.token
        return dxs, red

    dxs, r3 = ffn_b(3, dxs, None)
    dxs, rp = pool_b_(dxs, r3)
    dxs = _tie(dxs, rp.after_prev)
    dxs, r2 = ffn_b(2, dxs, rp)
    dxs, r1 = ffn_b(1, dxs, r2)
    dxs, rg = gla_b(dxs, r1)
    dxs, r0 = ffn_b(0, dxs, rg)
    grad_x = _tie(dxs, r0.token)[first:].reshape(x.shape)
    small_grads.update(meta=dxs[pad:first], ffn_norm=jnp.concatenate(d_ffn_norm, axis=0).reshape(ffn_norm_f.shape[0] // 2, 2, D),
                       final_norm=d_final.reshape(D))

    sh_names = ["meta", "ffn_norm", "gla_w_lr", "pool_norm", "pool_b", "pool_scale"]
    rep_names = ["gla_norm", "gla_b_lr", "gla_head_norm", "final_norm"]
    rep_w = [gla_norm, gla_b_lr, gla_head_norm, final_norm]
    rep_shapes = [a.shape for a in rep_w]
    rep_rows = -(-sum(_rows(a.size) for a in rep_w) // 8) * 8
    by_owner = [_to_shards(small_grads[nm].reshape(full_shape), ax) for nm, full_shape, ax in zip(
        sh_names, [meta_f.shape, (ffn_norm.shape[0], 2, D), wlr_f.shape, pool_norm_f.shape, pool_b_f.shape, pool_scale_f.shape],
        small_axis)]
    rep_pack = _pack([small_grads[nm].reshape(s) for nm, s in zip(rep_names, rep_shapes)], rep_rows)
    send = jnp.stack([
        jnp.concatenate([_pack([g[d] for g in by_owner], sh_rows), rep_pack, loss_part], axis=0) for d in range(N_DEV)])
    total = _sum_blocks(_small_exchange(send, False, "small_reduce"), "small_sum")
    loss = total[0, sh_rows + rep_rows, 0]
    n_small = sh_rows + rep_rows
    g_small = total[:, :n_small]

    def pack_small(sh_list, rep_list):
        return jnp.concatenate([_pack(sh_list, sh_rows), _pack(rep_list, rep_rows)], axis=0)[None]

    w_small = pack_small(small_sh, rep_w)
    m_small = pack_small([m_meta, m_ffn_norm, m_gla_w_lr, m_pool_norm, m_pool_b, m_pool_scale],
                         [m_gla_norm, m_gla_b_lr, m_gla_head_norm, m_final_norm])
    v_small = pack_small([v_meta, v_ffn_norm, v_gla_w_lr, v_pool_norm, v_pool_b, v_pool_scale],
                         [v_gla_norm, v_gla_b_lr, v_gla_head_norm, v_final_norm])
    small_out = _adamw(w_small, m_small, v_small, 0, g_small, zero_idx, None, None, "adamw_small")
    small_res = {}
    for kind, packed in zip(("grad", "delta", "new_m", "new_v"), small_out):
        sh_vals = _unpack(packed[0, :sh_rows], sh_shapes)
        rep_vals = _unpack(packed[0, sh_rows:], rep_shapes)
        for nm, val in zip(sh_names + rep_names, sh_vals + rep_vals):
            small_res[(kind, nm)] = val

    big_res = {}

    def adam_one(nm, w, m, v, entry):
        sums, recv = entry
        R, C = sums.shape[1:]
        out = _adamw(w.reshape(1, R, C), m.reshape(1, R, C), v.reshape(1, R, C), 0, sums, q_idx, recv, None, f"adamw_{nm}")
        for kind, val in zip(("grad", "delta", "new_m", "new_v"), out):
            big_res[(kind, nm)] = val.reshape(w.shape)
        return out[0]

    done = small_out[0]
    e_gla = rg.end(done)
    done = adam_one("gla_w_in", gla_w_in, m_gla_w_in, v_gla_w_in, e_gla[0])
    done = adam_one("gla_w_out", gla_w_out, m_gla_w_out, v_gla_w_out, e_gla[1])
    done = adam_one("pool_w", pool_w, m_pool_w, v_pool_w, rp.end(done)[0])
    r0.mid(done)

    ffn_names = ["ffn_w_gate", "ffn_w_up", "ffn_w_down"]
    ffn_wmv = [(ffn_w_gate, m_ffn_w_gate, v_ffn_w_gate), (ffn_w_up, m_ffn_w_up, v_ffn_w_up), (ffn_w_down, m_ffn_w_down, v_ffn_w_down)]
    ffn_prev = [None] * 3
    token = r0.token
    for u, red in ((3, r3), (2, r2), (1, r1), (0, r0)):
        entries = red.end(done)
        for a in range(3):
            sums, recv = entries[a]
            R, C = sums.shape[1:]
            w3, m3, v3 = (t.reshape(n_units, R, C) for t in ffn_wmv[a])
            if token is not None:
                w3 = _tie(w3, token)
                token = None
            if ffn_prev[a] is None:
                ffn_prev[a] = [lax.empty((n_units, R, C), f32) for _ in range(4)]
            ffn_prev[a] = _adamw(w3, m3, v3, u, sums, q_idx, recv, ffn_prev[a], f"adamw_{ffn_names[a]}{u}")
            done = ffn_prev[a][0]
    for a in range(3):
        for kind, val in zip(("grad", "delta", "new_m", "new_v"), ffn_prev[a]):
            big_res[(kind, ffn_names[a])] = val.reshape(ffn_wmv[a][0].shape)

    order = ["meta", "ffn_norm", "ffn_w_gate", "ffn_w_up", "ffn_w_down", "gla_norm", "gla_w_in", "gla_w_lr", "gla_b_lr",
             "gla_head_norm", "gla_w_out", "pool_norm", "pool_w", "pool_b", "pool_scale", "final_norm"]
    res = {**small_res, **big_res}
    outs = [loss, grad_x]
    for kind in ("grad", "delta", "new_m", "new_v"):
        outs += [res[(kind, nm)] for nm in order]
    return tuple(outs)
```

```python
import functools

import jax
import jax.numpy as jnp
from jax import lax
from jax.experimental import pallas as pl
from jax.experimental.pallas import tpu as pltpu

f32 = jnp.float32
bf16 = jnp.bfloat16

N_DEV = 8
N_META = 16
GLA_HEADS = 4
GLA_CHUNK = 64
GLA_SUB = 16
GATE_RANK = 16
GATE_PAD = 128
GATE_NORM = 16.0
EPS = 1e-6
POOL_GROUPS = 4
ADAM_LR = 0.001
ADAM_B1 = 0.9
ADAM_B2 = 0.999
ADAM_EPS = 1e-08
ADAM_WD = 0.01
ADAM_STEP = 10
LANES = 128
VMEM_LIMIT_MB = 56

NN = (((1,), (0,)), ((), ()))
NT = (((1,), (1,)), ((), ()))
TN = (((0,), (0,)), ((), ()))
HI = lax.Precision.HIGHEST
MESH = pl.DeviceIdType.MESH
ANY = pl.BlockSpec(memory_space=pl.ANY)


def _cparams(sem=None, vmem_mb=None):
    kw = {}
    if sem is not None:
        kw["dimension_semantics"] = sem
    if vmem_mb is not None:
        kw["vmem_limit_bytes"] = vmem_mb * 2 ** 20
    return pltpu.CompilerParams(**kw)


def _tile(n, target, mult=16):
    best = None
    for t in range(mult, min(n, target) + 1, mult):
        if n % t == 0:
            best = t
    assert best is not None, (n, target, mult)
    return best


def _tile2(R, C, rows, mult):
    if R % mult == 0:
        return _tile(R, rows, mult), C
    return R, _tile(C, 256, LANES)


def _dot(a, b, dims=NN, precision=None):
    return lax.dot_general(a, b, dims, preferred_element_type=f32, precision=precision)


def _sigmoid(x):
    return 1.0 / (1.0 + jnp.exp(-x))


def _row_ids(tile_index, tm):
    return tile_index * tm + lax.broadcasted_iota(jnp.int32, (tm, 1), 0)


def _ordered(body, in_specs, args, after, lead=0):
    if after is None:
        return body, in_specs, args
    pos = lead + len(args)

    def body_without(*refs):
        return body(*refs[:pos], *refs[pos + 1:])

    return body_without, list(in_specs) + [ANY], list(args) + [after]


def _rms_fwd(xs, g, out_dtype, name):
    Lp, D = xs.shape
    tm = _tile(Lp, 528)

    def body(x_ref, g_ref, h_ref):
        x = x_ref[...]
        rstd = lax.rsqrt(jnp.mean(x * x, axis=-1, keepdims=True) + EPS)
        h_ref[...] = (x * rstd * g_ref[...]).astype(out_dtype)

    return pl.pallas_call(
        body, name=name, grid=(Lp // tm,),
        in_specs=[pl.BlockSpec((tm, D), lambda i: (i, 0)), pl.BlockSpec((1, D), lambda i: (0, 0))],
        out_specs=pl.BlockSpec((tm, D), lambda i: (i, 0)),
        out_shape=jax.ShapeDtypeStruct((Lp, D), out_dtype),
        compiler_params=_cparams(("parallel",)),
    )(xs, g)


def _rms_bwd(dY, dh, xs, g, pad, name):
    Lp, D = xs.shape
    tm = _tile(Lp, 352)

    def body(dY_ref, dh_ref, x_ref, g_ref, dxs_ref, dg_ref):
        i = pl.program_id(0)

        @pl.when(i == 0)
        def _():
            dg_ref[...] = jnp.zeros_like(dg_ref)

        x = x_ref[...]
        rstd = lax.rsqrt(jnp.mean(x * x, axis=-1, keepdims=True) + EPS)
        xhat = x * rstd
        dh_ = dh_ref[...]
        dg_ref[...] += jnp.sum(dh_ * xhat, axis=0, keepdims=True)
        dxh = dh_ * g_ref[...]
        dx = rstd * (dxh - xhat * jnp.mean(dxh * xhat, axis=-1, keepdims=True))
        dxs_ref[...] = jnp.where(_row_ids(i, tm) >= pad, dY_ref[...] + dx, 0.0)

    row = pl.BlockSpec((tm, D), lambda i: (i, 0))
    vec = pl.BlockSpec((1, D), lambda i: (0, 0))
    return pl.pallas_call(
        body, name=name, grid=(Lp // tm,),
        in_specs=[row, row, row, vec], out_specs=[row, vec],
        out_shape=[jax.ShapeDtypeStruct((Lp, D), f32), jax.ShapeDtypeStruct((1, D), f32)],
        compiler_params=_cparams(("arbitrary",)),
    )(dY, dh, xs, g)


def _mm(a, b, mode, out_dtype, name, tm=512, tn=512, tk=512, residual=None, after=None):
    if mode == "nn":
        (M, K), N = a.shape, b.shape[1]
    elif mode == "nt":
        (M, K), N = a.shape, b.shape[0]
    else:
        (K, M), N = a.shape, b.shape[1]
    tm = _tile(M, tm, 16 if mode != "tn" else LANES) if M > tm else M
    tn = _tile(N, tn, LANES) if N > tn else N
    tk = _tile(K, tk, LANES if mode != "tn" else 16) if K > tk else K
    nk = K // tk
    dims = {"nn": NN, "nt": NT, "tn": TN}[mode]

    def body(*refs):
        if residual is None:
            a_ref, b_ref, o_ref, acc = refs
            r_ref = None
        else:
            a_ref, b_ref, r_ref, o_ref, acc = refs
        k = pl.program_id(2)

        @pl.when(k == 0)
        def _():
            acc[...] = jnp.zeros_like(acc)

        acc[...] += _dot(a_ref[...], b_ref[...], dims)

        @pl.when(k == nk - 1)
        def _():
            r = acc[...]
            if r_ref is not None:
                r = r + r_ref[...]
            o_ref[...] = r.astype(out_dtype)

    a_spec = pl.BlockSpec((tk, tm), lambda i, j, k: (k, i)) if mode == "tn" else pl.BlockSpec((tm, tk), lambda i, j, k: (i, k))
    b_spec = pl.BlockSpec((tn, tk), lambda i, j, k: (j, k)) if mode == "nt" else pl.BlockSpec((tk, tn), lambda i, j, k: (k, j))
    o_spec = pl.BlockSpec((tm, tn), lambda i, j, k: (i, j))
    in_specs = [a_spec, b_spec] + ([o_spec] if residual is not None else [])
    args = [a, b] + ([residual] if residual is not None else [])
    body, in_specs, args = _ordered(body, in_specs, args, after)
    return pl.pallas_call(
        body, name=name, grid=(M // tm, N // tn, nk),
        in_specs=in_specs, out_specs=o_spec,
        out_shape=jax.ShapeDtypeStruct((M, N), out_dtype),
        scratch_shapes=[pltpu.VMEM((tm, tn), f32)],
        compiler_params=_cparams(("parallel", "parallel", "arbitrary"), VMEM_LIMIT_MB),
    )(*args)


def _ffn_fwd(xs, g, wg, wu, wd, name):
    Lp, D = xs.shape
    nd, Fs, _ = wg.shape
    tm = _tile(Lp, 704)
    once = pl.Buffered(1)

    def body(x_ref, g_ref, wg_ref, wu_ref, wd_ref, out_ref, h_ref, G_ref, U_ref, hs, acc):
        j = pl.program_id(1)

        @pl.when(j == 0)
        def _():
            x = x_ref[...]
            rstd = lax.rsqrt(jnp.mean(x * x, axis=-1, keepdims=True) + EPS)
            h = (x * rstd * g_ref[...]).astype(bf16)
            hs[...] = h
            h_ref[...] = h
            acc[...] = jnp.zeros_like(acc)

        h = hs[...]
        G = _dot(h, wg_ref[0], NT)
        U = _dot(h, wu_ref[0], NT)
        G_ref[0] = G.astype(bf16)
        U_ref[0] = U.astype(bf16)
        A = (G * _sigmoid(G) * U).astype(bf16)
        acc[...] += _dot(A, wd_ref[0])

        @pl.when(j == nd - 1)
        def _():
            out_ref[...] = x_ref[...] + 0.5 * acc[...]

    row_f = pl.BlockSpec((tm, D), lambda i, j: (i, 0), pipeline_mode=once)
    act = pl.BlockSpec((1, tm, Fs), lambda i, j: (j, i, 0))
    return pl.pallas_call(
        body, name=name, grid=(Lp // tm, nd),
        in_specs=[row_f, pl.BlockSpec((1, D), lambda i, j: (0, 0)),
                  pl.BlockSpec((1, Fs, D), lambda i, j: (j, 0, 0)),
                  pl.BlockSpec((1, Fs, D), lambda i, j: (j, 0, 0)),
                  pl.BlockSpec((1, Fs, D), lambda i, j: (j, 0, 0))],
        out_specs=[row_f, pl.BlockSpec((tm, D), lambda i, j: (i, 0), pipeline_mode=once), act, act],
        out_shape=[jax.ShapeDtypeStruct((Lp, D), f32), jax.ShapeDtypeStruct((Lp, D), bf16),
                   jax.ShapeDtypeStruct((nd, Lp, Fs), bf16), jax.ShapeDtypeStruct((nd, Lp, Fs), bf16)],
        scratch_shapes=[pltpu.VMEM((tm, D), bf16), pltpu.VMEM((tm, D), f32)],
        compiler_params=_cparams(("parallel", "arbitrary"), VMEM_LIMIT_MB),
    )(xs, g, wg, wu, wd)


def _ffn_bwd_dgrad(dY, xs, g, wg, wu, wd, G, U, pad, name, after=None):
    Lp, D = xs.shape
    nd, Fs, _ = wg.shape
    tm = _tile(Lp, 352)
    once = pl.Buffered(1)

    def body(dY_ref, x_ref, g_ref, wg_ref, wu_ref, wd_ref, G_ref, U_ref,
             dxs_ref, dyh_ref, dG_ref, dU_ref, A_ref, dg_ref, dyh_s, acc):
        i = pl.program_id(0)
        j = pl.program_id(1)

        @pl.when(j == 0)
        def _():
            d = (0.5 * dY_ref[...]).astype(bf16)
            dyh_s[...] = d
            dyh_ref[...] = d
            acc[...] = jnp.zeros_like(acc)

        @pl.when((i == 0) & (j == 0))
        def _():
            dg_ref[...] = jnp.zeros_like(dg_ref)

        dA = _dot(dyh_s[...], wd_ref[0], NT)
        Gf = G_ref[0].astype(f32)
        Uf = U_ref[0].astype(f32)
        s = _sigmoid(Gf)
        silu = Gf * s
        dGb = (dA * Uf * (s * (1.0 + Gf * (1.0 - s)))).astype(bf16)
        dUb = (dA * silu).astype(bf16)
        dG_ref[0] = dGb
        dU_ref[0] = dUb
        A_ref[0] = (silu * Uf).astype(bf16)
        acc[...] += _dot(dGb, wg_ref[0]) + _dot(dUb, wu_ref[0])

        @pl.when(j == nd - 1)
        def _():
            x = x_ref[...]
            rstd = lax.rsqrt(jnp.mean(x * x, axis=-1, keepdims=True) + EPS)
            xhat = x * rstd
            dh = acc[...]
            dg_ref[...] += jnp.sum(dh * xhat, axis=0, keepdims=True)
            dxh = dh * g_ref[...]
            dx = rstd * (dxh - xhat * jnp.mean(dxh * xhat, axis=-1, keepdims=True))
            dxs_ref[...] = jnp.where(_row_ids(i, tm) >= pad, dY_ref[...] + dx, 0.0)

    row_f = pl.BlockSpec((tm, D), lambda i, j: (i, 0), pipeline_mode=once)
    vec = pl.BlockSpec((1, D), lambda i, j: (0, 0))
    wrow = pl.BlockSpec((1, Fs, D), lambda i, j: (j, 0, 0))
    act = pl.BlockSpec((1, tm, Fs), lambda i, j: (j, i, 0))
    act_s = jax.ShapeDtypeStruct((nd, Lp, Fs), bf16)
    body, in_specs, args = _ordered(body, [row_f, row_f, vec, wrow, wrow, wrow, act, act], [dY, xs, g, wg, wu, wd, G, U], after)
    return pl.pallas_call(
        body, name=name, grid=(Lp // tm, nd),
        in_specs=in_specs,
        out_specs=[row_f, pl.BlockSpec((tm, D), lambda i, j: (i, 0), pipeline_mode=once), act, act, act, vec],
        out_shape=[jax.ShapeDtypeStruct((Lp, D), f32), jax.ShapeDtypeStruct((Lp, D), bf16), act_s, act_s, act_s,
                   jax.ShapeDtypeStruct((1, D), f32)],
        scratch_shapes=[pltpu.VMEM((tm, D), bf16), pltpu.VMEM((tm, D), f32)],
        compiler_params=_cparams(("arbitrary", "arbitrary"), VMEM_LIMIT_MB),
    )(*args)


def _ffn_bwd_wgrad(h, dyh, A, dG, dU, name, after=None):
    Lp, D = h.shape
    nd, _, Fs = A.shape
    tk = _tile(Lp, 528)
    nk = Lp // tk

    def body(h_ref, dyh_ref, A_ref, dG_ref, dU_ref, dwg_ref, dwu_ref, dwd_ref, ag, au, ad):
        i = pl.program_id(1)

        @pl.when(i == 0)
        def _():
            ag[...] = jnp.zeros_like(ag)
            au[...] = jnp.zeros_like(au)
            ad[...] = jnp.zeros_like(ad)

        hh = h_ref[...]
        ag[...] += _dot(dG_ref[0], hh, TN)
        au[...] += _dot(dU_ref[0], hh, TN)
        ad[...] += _dot(A_ref[0], dyh_ref[...], TN)

        @pl.when(i == nk - 1)
        def _():
            dwg_ref[0] = ag[...].astype(bf16)
            dwu_ref[0] = au[...].astype(bf16)
            dwd_ref[0] = ad[...].astype(bf16)

    row = pl.BlockSpec((tk, D), lambda j, i: (i, 0))
    act = pl.BlockSpec((1, tk, Fs), lambda j, i: (j, i, 0))
    wrow = pl.BlockSpec((1, Fs, D), lambda j, i: (j, 0, 0))
    w_s = jax.ShapeDtypeStruct((nd, Fs, D), bf16)
    body, in_specs, args = _ordered(body, [row, row, act, act, act], [h, dyh, A, dG, dU], after)
    return pl.pallas_call(
        body, name=name, grid=(nd, nk),
        in_specs=in_specs, out_specs=[wrow, wrow, wrow],
        out_shape=[w_s, w_s, w_s],
        scratch_shapes=[pltpu.VMEM((Fs, D), f32), pltpu.VMEM((Fs, D), f32), pltpu.VMEM((Fs, D), f32)],
        compiler_params=_cparams(("parallel", "arbitrary"), VMEM_LIMIT_MB),
    )(*args)


def _gate_fwd(proj, wlr, blr, pad, gate_blk, name):
    Lp = proj.shape[0]
    DK = wlr.shape[1]
    tm = _tile(Lp, 528)

    def body(lr_ref, w_ref, b_ref, lg_ref):
        z = _dot(lr_ref[...].astype(bf16), w_ref[...].astype(bf16)) + b_ref[...]
        ls = jnp.minimum(z, 0.0) - jnp.log(1.0 + jnp.exp(-jnp.abs(z)))
        lg_ref[...] = jnp.where(_row_ids(pl.program_id(0), tm) >= pad, ls * (1.0 / GATE_NORM), 0.0)

    return pl.pallas_call(
        body, name=name, grid=(Lp // tm,),
        in_specs=[pl.BlockSpec((tm, GATE_PAD), lambda i: (i, gate_blk)),
                  pl.BlockSpec((GATE_PAD, DK), lambda i: (0, 0)), pl.BlockSpec((1, DK), lambda i: (0, 0))],
        out_specs=pl.BlockSpec((tm, DK), lambda i: (i, 0)),
        out_shape=jax.ShapeDtypeStruct((Lp, DK), f32),
        compiler_params=_cparams(("parallel",)),
    )(proj, wlr, blr)


def _gate_bwd(dlg, proj, wlr, blr, pad, gate_blk, name):
    Lp = proj.shape[0]
    DK = wlr.shape[1]
    tm = _tile(Lp, 528)

    def body(dlg_ref, lr_ref, w_ref, b_ref, dlr_ref, dw_ref, db_ref):
        i = pl.program_id(0)

        @pl.when(i == 0)
        def _():
            dw_ref[...] = jnp.zeros_like(dw_ref)
            db_ref[...] = jnp.zeros_like(db_ref)

        lr = lr_ref[...].astype(bf16)
        w = w_ref[...].astype(bf16)
        z = _dot(lr, w) + b_ref[...]
        dz = jnp.where(_row_ids(i, tm) >= pad, dlg_ref[...] * _sigmoid(-z) * (1.0 / GATE_NORM), 0.0)
        dzb = dz.astype(bf16)
        dlr_ref[...] = _dot(dzb, w, NT).astype(bf16)
        dw_ref[...] += _dot(lr, dzb, TN)
        db_ref[...] += jnp.sum(dz, axis=0, keepdims=True)

    return pl.pallas_call(
        body, name=name, grid=(Lp // tm,),
        in_specs=[pl.BlockSpec((tm, DK), lambda i: (i, 0)), pl.BlockSpec((tm, GATE_PAD), lambda i: (i, gate_blk)),
                  pl.BlockSpec((GATE_PAD, DK), lambda i: (0, 0)), pl.BlockSpec((1, DK), lambda i: (0, 0))],
        out_specs=[pl.BlockSpec((tm, GATE_PAD), lambda i: (i, 0)), pl.BlockSpec((GATE_PAD, DK), lambda i: (0, 0)),
                   pl.BlockSpec((1, DK), lambda i: (0, 0))],
        out_shape=[jax.ShapeDtypeStruct((Lp, GATE_PAD), bf16), jax.ShapeDtypeStruct((GATE_PAD, DK), f32),
                   jax.ShapeDtypeStruct((1, DK), f32)],
        compiler_params=_cparams(("arbitrary",)),
    )(dlg, proj, wlr, blr)


def _chunk_decay(lg):
    C = lg.shape[0]
    r = lax.broadcasted_iota(jnp.int32, (C, C), 0)
    c = lax.broadcasted_iota(jnp.int32, (C, C), 1)
    return _dot(jnp.where(r >= c, 1.0, 0.0).astype(f32), lg, NN, HI)


def _col(v):
    return jnp.transpose(jnp.broadcast_to(v, (8, v.shape[1])))[:, 0:1]


def _intra_scores(q, k, b, A_ref):
    C = q.shape[0]
    S = GLA_SUB
    A_ref[...] = jnp.zeros_like(A_ref)
    ri = lax.broadcasted_iota(jnp.int32, (S, 1), 0)
    for I in range(C // S):
        lo = S * I
        qI, bI = q[lo:lo + S], b[lo:lo + S]
        if I > 0:
            bref = b[lo - 1:lo]
            qs = qI * jnp.exp(bI - bref)
            ks = k[:lo] * jnp.exp(bref - b[:lo])
            A_ref[lo:lo + S, 0:lo] = _dot(qs, ks, NT, HI)
        for jj in range(S):
            j = lo + jj
            P = jnp.exp(jnp.minimum(bI - b[j:j + 1], 0.0))
            a = jnp.sum(qI * P * k[j:j + 1], axis=1, keepdims=True)
            A_ref[lo:lo + S, j:j + 1] = jnp.where(ri >= jj, a, 0.0)


def _intra_grads(q, k, b, dA, dq_ref, dk_ref):
    C = q.shape[0]
    S = GLA_SUB
    ri = lax.broadcasted_iota(jnp.int32, (S, 1), 0)
    for I in range(C // S):
        lo = S * I
        qI, bI = q[lo:lo + S], b[lo:lo + S]
        dqI = jnp.zeros_like(qI)
        if I > 0:
            bref = b[lo - 1:lo]
            eq = jnp.exp(bI - bref)
            ek = jnp.exp(bref - b[:lo])
            qs = qI * eq
            ks = k[:lo] * ek
            dAI = dA[lo:lo + S, 0:lo]
            dqI = dqI + _dot(dAI, ks, NN, HI) * eq
            dk_ref[0:lo, :] += _dot(dAI, qs, TN, HI) * ek
        for jj in range(S):
            j = lo + jj
            P = jnp.exp(jnp.minimum(bI - b[j:j + 1], 0.0))
            t = jnp.where(ri >= jj, dA[lo:lo + S, j:j + 1], 0.0) * P
            dqI = dqI + t * k[j:j + 1]
            dk_ref[j:j + 1, :] += jnp.sum(t * qI, axis=0, keepdims=True)
        dq_ref[lo:lo + S, :] += dqI


def _gla_fwd(proj, lg, hnw, H, name):
    Lp = proj.shape[0]
    DK = lg.shape[1]
    hk = DK // H
    hv = hnw.shape[1]
    DV = hv * H
    C = GLA_CHUNK
    NC = Lp // C
    scale = float(hk) ** -0.5
    kq, kv = DK // hk, (2 * DK) // hv
    kr = kv + H

    def body(q_ref, k_ref, v_ref, r_ref, lg_ref, w_ref, o_ref, y_ref, s_ref, S_scr, A_scr):
        c = pl.program_id(1)

        @pl.when(c == 0)
        def _():
            S_scr[...] = jnp.zeros_like(S_scr)

        q = q_ref[...] * scale
        k = k_ref[...]
        v = v_ref[...]
        b = _chunk_decay(lg_ref[...])
        bl = b[C - 1:C]
        S = S_scr[...]
        s_ref[0, 0] = S
        _intra_scores(q, k, b, A_scr)
        vb = v.astype(bf16)
        o = _dot((q * jnp.exp(b)).astype(bf16), S.astype(bf16)) + _dot(A_scr[...].astype(bf16), vb)
        kb = (k * jnp.exp(bl - b)).astype(bf16)
        S_scr[...] = jnp.exp(_col(bl)) * S + _dot(kb, vb, TN)
        o_ref[...] = o
        on = o * lax.rsqrt(jnp.mean(o * o, axis=-1, keepdims=True) + EPS) * w_ref[...]
        r = r_ref[...]
        y_ref[...] = (on * (r * _sigmoid(r))).astype(bf16)

    return pl.pallas_call(
        body, name=name, grid=(H, NC),
        in_specs=[pl.BlockSpec((C, hk), lambda h, c: (c, h)),
                  pl.BlockSpec((C, hk), lambda h, c: (c, kq + h)),
                  pl.BlockSpec((C, hv), lambda h, c: (c, kv + h)),
                  pl.BlockSpec((C, hv), lambda h, c: (c, kr + h)),
                  pl.BlockSpec((C, hk), lambda h, c: (c, h)),
                  pl.BlockSpec((1, hv), lambda h, c: (0, 0))],
        out_specs=[pl.BlockSpec((C, hv), lambda h, c: (c, h)), pl.BlockSpec((C, hv), lambda h, c: (c, h)),
                   pl.BlockSpec((1, 1, hk, hv), lambda h, c: (h, c, 0, 0))],
        out_shape=[jax.ShapeDtypeStruct((Lp, DV), f32), jax.ShapeDtypeStruct((Lp, DV), bf16),
                   jax.ShapeDtypeStruct((H, NC, hk, hv), f32)],
        scratch_shapes=[pltpu.VMEM((hk, hv), f32), pltpu.VMEM((C, C), f32)],
        compiler_params=_cparams(("parallel", "arbitrary")),
    )(proj, proj, proj, proj, lg, hnw)


def _gla_bwd(dy, proj, lg, o, states, hnw, H, pad, name, after=None):
    Lp = proj.shape[0]
    DK = lg.shape[1]
    hk = DK // H
    hv = hnw.shape[1]
    DV = hv * H
    C = GLA_CHUNK
    NC = Lp // C
    scale = float(hk) ** -0.5
    kq, kv = DK // hk, (2 * DK) // hv
    kr = kv + H

    def body(dy_ref, q_ref, k_ref, v_ref, r_ref, lg_ref, o_ref, s_ref, sn_ref, w_ref,
             dq_ref, dk_ref, dv_ref, dr_ref, dlg_ref, dw_ref, dS_scr, A_scr, dq_s, dk_s):
        h = pl.program_id(0)
        cc = pl.program_id(1)
        c = NC - 1 - cc

        @pl.when(cc == 0)
        def _():
            dS_scr[...] = jnp.zeros_like(dS_scr)

        @pl.when((cc == 0) & (h == 0))
        def _():
            dw_ref[...] = jnp.zeros_like(dw_ref)

        keep = (c * C + lax.broadcasted_iota(jnp.int32, (C, 1), 0)) >= pad
        w = w_ref[...]
        o_ = o_ref[...]
        rs = lax.rsqrt(jnp.mean(o_ * o_, axis=-1, keepdims=True) + EPS)
        ohat = o_ * rs
        r = r_ref[...]
        sg = _sigmoid(r)
        dy_ = dy_ref[...]
        d_on = dy_ * (r * sg)
        dr_ref[...] = jnp.where(keep, dy_ * (ohat * w) * (sg * (1.0 + r * (1.0 - sg))), 0.0).astype(bf16)
        dw_ref[...] += jnp.sum(d_on * ohat, axis=0, keepdims=True)
        d_oh = d_on * w
        do = rs * (d_oh - ohat * jnp.mean(d_oh * ohat, axis=-1, keepdims=True))
        dob = do.astype(bf16)
        q = q_ref[...] * scale
        k = k_ref[...]
        v = v_ref[...]
        vb = v.astype(bf16)
        b = _chunk_decay(lg_ref[...])
        bl = b[C - 1:C]
        eb = jnp.exp(b)
        ekb = jnp.exp(bl - b)
        S = s_ref[0, 0]
        dS = dS_scr[...]
        dSb = dS.astype(bf16)
        _intra_scores(q, k, b, A_scr)
        ri = lax.broadcasted_iota(jnp.int32, (C, C), 0)
        ci = lax.broadcasted_iota(jnp.int32, (C, C), 1)
        dA = jnp.where(ri >= ci, _dot(dob, vb, NT), 0.0)
        kb = (k * ekb).astype(bf16)
        qb = (q * eb).astype(bf16)
        dv = _dot(A_scr[...].astype(bf16), dob, TN) + _dot(kb, dSb)
        dq_s[...] = _dot(dob, S.astype(bf16), NT) * eb
        dk_s[...] = _dot(vb, dSb, NT) * ekb
        dS_scr[...] = _dot(qb, dob, TN) + jnp.exp(_col(bl)) * dS
        _intra_grads(q, k, b, dA, dq_s, dk_s)
        dq = dq_s[...]
        dk = dk_s[...]
        Dm = q * dq - k * dk
        after = _dot(jnp.ones((8, hv), f32), sn_ref[0, 0] * dS, NT, HI)[0:1]
        dlg = _dot(jnp.where(ri <= ci, 1.0, 0.0).astype(f32), Dm, NN, HI) + after
        dlg_ref[...] = jnp.where(keep, dlg, 0.0)
        dq_ref[...] = jnp.where(keep, dq * scale, 0.0).astype(bf16)
        dk_ref[...] = jnp.where(keep, dk, 0.0).astype(bf16)
        dv_ref[...] = jnp.where(keep, dv, 0.0).astype(bf16)

    rev = lambda h, cc: NC - 1 - cc
    in_specs = [pl.BlockSpec((C, hv), lambda h, cc: (rev(h, cc), h)),
                pl.BlockSpec((C, hk), lambda h, cc: (rev(h, cc), h)),
                pl.BlockSpec((C, hk), lambda h, cc: (rev(h, cc), kq + h)),
                pl.BlockSpec((C, hv), lambda h, cc: (rev(h, cc), kv + h)),
                pl.BlockSpec((C, hv), lambda h, cc: (rev(h, cc), kr + h)),
                pl.BlockSpec((C, hk), lambda h, cc: (rev(h, cc), h)),
                pl.BlockSpec((C, hv), lambda h, cc: (rev(h, cc), h)),
                pl.BlockSpec((1, 1, hk, hv), lambda h, cc: (h, rev(h, cc), 0, 0)),
                pl.BlockSpec((1, 1, hk, hv), lambda h, cc: (h, jnp.minimum(rev(h, cc) + 1, NC - 1), 0, 0)),
                pl.BlockSpec((1, hv), lambda h, cc: (0, 0))]
    body, in_specs, args = _ordered(body, in_specs, [dy, proj, proj, proj, proj, lg, o, states, states, hnw], after)
    return pl.pallas_call(
        body, name=name, grid=(H, NC),
        in_specs=in_specs,
        out_specs=[pl.BlockSpec((C, hk), lambda h, cc: (rev(h, cc), h)),
                   pl.BlockSpec((C, hk), lambda h, cc: (rev(h, cc), h)),
                   pl.BlockSpec((C, hv), lambda h, cc: (rev(h, cc), h)),
                   pl.BlockSpec((C, hv), lambda h, cc: (rev(h, cc), h)),
                   pl.BlockSpec((C, hk), lambda h, cc: (rev(h, cc), h)),
                   pl.BlockSpec((1, hv), lambda h, cc: (0, 0))],
        out_shape=[jax.ShapeDtypeStruct((Lp, DK), bf16), jax.ShapeDtypeStruct((Lp, DK), bf16),
                   jax.ShapeDtypeStruct((Lp, DV), bf16), jax.ShapeDtypeStruct((Lp, DV), bf16),
                   jax.ShapeDtypeStruct((Lp, DK), f32), jax.ShapeDtypeStruct((1, hv), f32)],
        scratch_shapes=[pltpu.VMEM((hk, hv), f32), pltpu.VMEM((C, C), f32),
                        pltpu.VMEM((C, hk), f32), pltpu.VMEM((C, hk), f32)],
        compiler_params=_cparams(("arbitrary", "arbitrary")),
    )(*args)


def _window_sums(x, back):
    n = x.shape[0]
    out = []
    s = x
    for w in (1, 2, 4, 8):
        s = s + pltpu.roll(s, w if back else n - w, 0)
        out.append(s)
    return out


def _pool_windows(hn, pad, n_real, name):
    Lp, D = hn.shape
    GW = D // POOL_GROUPS
    cb = min(GW, 256)
    per = GW // cb

    def body(h_ref, p_ref):
        g = pl.program_id(0) // per
        x = h_ref[...]
        s2, s4, s8, s16 = _window_sums(x, True)
        sel = jnp.where(g == 0, s2, jnp.where(g == 1, s4, jnp.where(g == 2, s8, s16)))
        win = jnp.left_shift(2, g).astype(f32)
        rows = lax.broadcasted_iota(jnp.int32, (Lp, 1), 0)
        t = (rows - pad).astype(f32)
        cnt = jnp.minimum(jnp.maximum(t, 0.0) + 1.0, win)
        p_ref[...] = jnp.where(rows >= pad, sel / cnt - x, 0.0).astype(bf16)

    return pl.pallas_call(
        body, name=name, grid=(D // cb,),
        in_specs=[pl.BlockSpec((Lp, cb), lambda i: (0, i))],
        out_specs=pl.BlockSpec((Lp, cb), lambda i: (0, i)),
        out_shape=jax.ShapeDtypeStruct((Lp, D), bf16),
        compiler_params=_cparams(("parallel",)),
    )(hn)


def _pool_windows_bwd(dp, pad, name):
    Lp, D = dp.shape
    GW = D // POOL_GROUPS
    cb = min(GW, 256)
    per = GW // cb

    def body(dp_ref, dh_ref):
        g = pl.program_id(0) // per
        rows = lax.broadcasted_iota(jnp.int32, (Lp, 1), 0)
        d = jnp.where(rows >= pad, dp_ref[...], 0.0)
        win = jnp.left_shift(2, g).astype(f32)
        t = (rows - pad).astype(f32)
        cnt = jnp.minimum(jnp.maximum(t, 0.0) + 1.0, win)
        s2, s4, s8, s16 = _window_sums(d / cnt, False)
        sel = jnp.where(g == 0, s2, jnp.where(g == 1, s4, jnp.where(g == 2, s8, s16)))
        dh_ref[...] = jnp.where(rows >= pad, sel - d, 0.0)

    return pl.pallas_call(
        body, name=name, grid=(D // cb,),
        in_specs=[pl.BlockSpec((Lp, cb), lambda i: (0, i))],
        out_specs=pl.BlockSpec((Lp, cb), lambda i: (0, i)),
        out_shape=jax.ShapeDtypeStruct((Lp, D), f32),
        compiler_params=_cparams(("parallel",)),
    )(dp)


def _pool_mix_fwd(xs, pooled, w, bias, scale, pad, name):
    Lp, D = xs.shape
    GW = D // POOL_GROUPS
    tm = _tile(Lp, 1056)

    def body(x_ref, p_ref, w_ref, b_ref, s_ref, o_ref):
        z = _dot(p_ref[...], w_ref[0]) + b_ref[...]
        keep = _row_ids(pl.program_id(1), tm) >= pad
        o_ref[...] = x_ref[...] + jnp.where(keep, z * s_ref[...], 0.0)

    blk = pl.BlockSpec((tm, GW), lambda g, i: (i, g))
    vec = pl.BlockSpec((1, GW), lambda g, i: (0, g))
    return pl.pallas_call(
        body, name=name, grid=(POOL_GROUPS, Lp // tm),
        in_specs=[blk, blk, pl.BlockSpec((1, GW, GW), lambda g, i: (g, 0, 0)), vec, vec],
        out_specs=blk, out_shape=jax.ShapeDtypeStruct((Lp, D), f32),
        compiler_params=_cparams(("parallel", "parallel")),
    )(xs, pooled, w, bias, scale)


def _pool_mix_bwd(dY, pooled, w, bias, scale, pad, name, after=None):
    Lp, D = dY.shape
    GW = D // POOL_GROUPS
    tm = _tile(Lp, 1056)
    nm = Lp // tm

    def body(dY_ref, p_ref, w_ref, b_ref, s_ref, dp_ref, dw_ref, db_ref, ds_ref, acc):
        i = pl.program_id(1)

        @pl.when(i == 0)
        def _():
            acc[...] = jnp.zeros_like(acc)
            db_ref[...] = jnp.zeros_like(db_ref)
            ds_ref[...] = jnp.zeros_like(ds_ref)

        keep = _row_ids(i, tm) >= pad
        dY_ = jnp.where(keep, dY_ref[...], 0.0)
        p = p_ref[...]
        z = _dot(p, w_ref[0]) + b_ref[...]
        ds_ref[...] += jnp.sum(dY_ * z, axis=0, keepdims=True)
        dz = dY_ * s_ref[...]
        db_ref[...] += jnp.sum(dz, axis=0, keepdims=True)
        dzb = dz.astype(bf16)
        acc[...] += _dot(p, dzb, TN)
        dp_ref[...] = _dot(dzb, w_ref[0], NT)

        @pl.when(i == nm - 1)
        def _():
            dw_ref[0] = acc[...].astype(bf16)

    blk = pl.BlockSpec((tm, GW), lambda g, i: (i, g))
    vec = pl.BlockSpec((1, GW), lambda g, i: (0, g))
    wsp = pl.BlockSpec((1, GW, GW), lambda g, i: (g, 0, 0))
    body, in_specs, args = _ordered(body, [blk, blk, wsp, vec, vec], [dY, pooled, w, bias, scale], after)
    return pl.pallas_call(
        body, name=name, grid=(POOL_GROUPS, nm),
        in_specs=in_specs, out_specs=[blk, wsp, vec, vec],
        out_shape=[jax.ShapeDtypeStruct((Lp, D), f32), jax.ShapeDtypeStruct((POOL_GROUPS, GW, GW), bf16),
                   jax.ShapeDtypeStruct((1, D), f32), jax.ShapeDtypeStruct((1, D), f32)],
        scratch_shapes=[pltpu.VMEM((GW, GW), f32)],
        compiler_params=_cparams(("parallel", "arbitrary")),
    )(*args)


def _loss_head(xs, target, g, first, name):
    Lp, D = xs.shape
    tm = GLA_CHUNK
    off = first // tm

    def body(x_ref, t_ref, g_ref, loss_ref, dxs_ref, dg_ref):
        i = pl.program_id(0)

        @pl.when(i == 0)
        def _():
            loss_ref[...] = jnp.zeros_like(loss_ref)
            dg_ref[...] = jnp.zeros_like(dg_ref)

        @pl.when(i < off)
        def _():
            dxs_ref[...] = jnp.zeros_like(dxs_ref)

        @pl.when(i >= off)
        def _():
            x = x_ref[...]
            rstd = lax.rsqrt(jnp.mean(x * x, axis=-1, keepdims=True) + EPS)
            xhat = x * rstd
            gg = g_ref[...]
            err = xhat * gg - t_ref[...]
            loss_ref[...] += 0.5 * jnp.sum(jnp.mean(err * err, axis=-1, keepdims=True))
            dy = err * (1.0 / D)
            dg_ref[...] += jnp.sum(dy * xhat, axis=0, keepdims=True)
            dxh = dy * gg
            dxs_ref[...] = rstd * (dxh - xhat * jnp.mean(dxh * xhat, axis=-1, keepdims=True))

    row = pl.BlockSpec((tm, D), lambda i: (i, 0))
    return pl.pallas_call(
        body, name=name, grid=(Lp // tm,),
        in_specs=[row, pl.BlockSpec((tm, D), lambda i: (jnp.maximum(i - off, 0), 0)), pl.BlockSpec((1, D), lambda i: (0, 0))],
        out_specs=[pl.BlockSpec((8, LANES), lambda i: (0, 0)), row, pl.BlockSpec((1, D), lambda i: (0, 0))],
        out_shape=[jax.ShapeDtypeStruct((8, LANES), f32), jax.ShapeDtypeStruct((Lp, D), f32),
                   jax.ShapeDtypeStruct((1, D), f32)],
        compiler_params=_cparams(("arbitrary",)),
    )(xs, target, g)


def _adam_math(w, g, m, v):
    m2 = ADAM_B1 * m + (1.0 - ADAM_B1) * g
    v2 = ADAM_B2 * v + (1.0 - ADAM_B2) * (g * g)
    m_hat = m2 / (1.0 - ADAM_B1 ** ADAM_STEP)
    v_hat = v2 / (1.0 - ADAM_B2 ** ADAM_STEP)
    delta = -ADAM_LR * (m_hat / (jnp.sqrt(v_hat) + ADAM_EPS) + ADAM_WD * w)
    return delta, m2, v2


def _adamw(w, m, v, unit, own, own_idx, recv, prev, name, after=None):
    U, R, C = w.shape
    tr, tc = _tile2(R, C, 256, 8 if own.dtype == f32 and recv is None else 16)
    n_recv = 0 if recv is None else recv.shape[0]

    def body(idx_ref, w_ref, m_ref, v_ref, own_ref, *rest):
        rest = list(rest)
        recv_refs = [rest.pop(0) for _ in range(n_recv)]
        if prev is not None:
            rest = rest[4:]
        g_ref, d_ref, m2_ref, v2_ref = rest
        g = own_ref[0].astype(f32)
        for r_ref in recv_refs:
            g = g + r_ref[0].astype(f32)
        delta, m2, v2 = _adam_math(w_ref[0], g, m_ref[0], v_ref[0])
        g_ref[0] = g
        d_ref[0] = delta
        m2_ref[0] = m2
        v2_ref[0] = v2

    blk = pl.BlockSpec((1, tr, tc), lambda i, j, idx: (unit, i, j))
    in_specs = [blk, blk, blk, pl.BlockSpec((1, tr, tc), lambda i, j, idx: (idx[0], i, j))]
    args = [w, m, v, own]
    for p in range(n_recv):
        in_specs.append(pl.BlockSpec((1, tr, tc), lambda i, j, idx, p=p: (p, i, j)))
        args.append(recv)
    aliases = {}
    if prev is not None:
        for t in range(4):
            aliases[1 + len(args) + t] = t
        in_specs += [ANY] * 4
        args += list(prev)
    body, in_specs, args = _ordered(body, in_specs, args, after, lead=1)
    out = jax.ShapeDtypeStruct((U, R, C), f32)
    return pl.pallas_call(
        body, name=name,
        grid_spec=pltpu.PrefetchScalarGridSpec(
            num_scalar_prefetch=1, grid=(R // tr, C // tc), in_specs=in_specs, out_specs=[blk] * 4),
        out_shape=[out] * 4, input_output_aliases=aliases,
        compiler_params=_cparams(("parallel", "parallel")),
    )(own_idx, *args)


def _place():
    return lax.axis_index("x"), lax.axis_index("y"), lax.axis_index("c")


HBM = pl.BlockSpec(memory_space=pltpu.HBM)
SEM = pl.BlockSpec(memory_space=pltpu.SEMAPHORE)
VMEM_SPEC = pl.BlockSpec(memory_space=pltpu.VMEM)
EFFECT = pltpu.SideEffectType.DATAFLOW_SIDE_EFFECTING
TOKEN = jax.ShapeDtypeStruct((8, LANES), f32)


def _hbm(x):
    return pltpu.with_memory_space_constraint(x, pltpu.HBM)


def _hbm_like(xs):
    return [pltpu.HBM(x.shape, x.dtype) for x in xs]


def _slot(px, py, pc):
    return 4 * px + 2 * py + pc


def _gather_start(shards, after, name):
    n = len(shards)
    me = _slot(*_place())
    bufs = [lax.dynamic_update_slice(lax.empty((N_DEV,) + s.shape, s.dtype), s[None], (me,) + (0,) * s.ndim) for s in shards]

    def body(*refs):
        ins, land = refs[:n], refs[n:2 * n]
        send, recv = refs[2 * n + 1], refs[2 * n + 2]
        token = refs[-1]
        x, y, c = _place()
        to = [(x, y, 1 - c), (1 - x, y, c), (x, 1 - y, c), (1 - x, 1 - y, c)]
        for a in range(n):
            for k, dev in enumerate(to):
                pltpu.make_async_remote_copy(
                    src_ref=ins[a], dst_ref=land[a].at[_slot(x, y, c)], send_sem=send.at[4 * a + k], recv_sem=recv.at[4 * a + k],
                    device_id=dev, device_id_type=MESH).start()
        token[...] = jnp.zeros_like(token)

    out = pl.pallas_call(
        body, name=name,
        in_specs=[HBM] * (2 * n) + [ANY],
        out_specs=[SEM, SEM] + [HBM] * (2 * n) + [VMEM_SPEC],
        out_shape=[pltpu.SemaphoreType.DMA((4 * n,)), pltpu.SemaphoreType.DMA((4 * n,))] + _hbm_like(shards) + _hbm_like(bufs) + [TOKEN],
        input_output_aliases={i: 2 + i for i in range(2 * n)},
        compiler_params=pltpu.CompilerParams(has_side_effects=EFFECT),
    )(*[_hbm(s) for s in shards], *[_hbm(b) for b in bufs], after)
    return dict(send1=out[0], recv1=out[1], shards=list(out[2:2 + n]), bufs=list(out[2 + n:2 + 2 * n]), token=out[-1])


def _gather_mid(h, after, name):
    n = len(h["bufs"])

    def body(*refs):
        land, recv1 = refs[:n], refs[n]
        send2, recv2 = refs[n + 2], refs[n + 3]
        token = refs[-1]
        x, y, c = _place()
        chips = [(1 - x, y), (x, 1 - y), (1 - x, 1 - y)]
        for j, (px, py) in enumerate(chips):
            for a in range(n):
                blk = land[a].at[_slot(px, py, c)]
                pltpu.make_async_remote_copy(
                    src_ref=blk, dst_ref=blk, send_sem=send2.at[3 * a + j], recv_sem=recv1.at[4 * a + 1 + j],
                    device_id=(px, py, c), device_id_type=MESH).wait_recv()
                pltpu.make_async_remote_copy(
                    src_ref=blk, dst_ref=blk, send_sem=send2.at[3 * a + j], recv_sem=recv2.at[3 * a + j],
                    device_id=(x, y, 1 - c), device_id_type=MESH).start()
        token[...] = jnp.zeros_like(token)

    out = pl.pallas_call(
        body, name=name,
        in_specs=[HBM] * n + [SEM, ANY],
        out_specs=[SEM, SEM] + [HBM] * n + [VMEM_SPEC],
        out_shape=[pltpu.SemaphoreType.DMA((3 * n,)), pltpu.SemaphoreType.DMA((3 * n,))] + _hbm_like(h["bufs"]) + [TOKEN],
        input_output_aliases={i: 2 + i for i in range(n)},
        compiler_params=pltpu.CompilerParams(has_side_effects=EFFECT),
    )(*h["bufs"], h["recv1"], after)
    h.update(send2=out[0], recv2=out[1], bufs=list(out[2:2 + n]), token=out[-1])
    return h


def _gather_end(h, after, name):
    n = len(h["bufs"])

    def body(*refs):
        ins, land = refs[:n], refs[n:2 * n]
        send1, recv1, send2, recv2 = refs[2 * n:2 * n + 4]
        x, y, c = _place()
        chips = [(1 - x, y), (x, 1 - y), (1 - x, 1 - y)]
        sib = (x, y, 1 - c)
        for a in range(n):
            mine = land[a].at[_slot(x, y, c)]
            for k in range(4):
                pltpu.make_async_remote_copy(
                    src_ref=ins[a], dst_ref=mine, send_sem=send1.at[4 * a + k], recv_sem=recv1.at[4 * a + k],
                    device_id=sib, device_id_type=MESH).wait_send()
            theirs = land[a].at[_slot(x, y, 1 - c)]
            pltpu.make_async_remote_copy(
                src_ref=ins[a], dst_ref=theirs, send_sem=send1.at[4 * a], recv_sem=recv1.at[4 * a],
                device_id=sib, device_id_type=MESH).wait_recv()
            for j, (px, py) in enumerate(chips):
                sent = land[a].at[_slot(px, py, c)]
                got = land[a].at[_slot(px, py, 1 - c)]
                pltpu.make_async_remote_copy(
                    src_ref=sent, dst_ref=sent, send_sem=send2.at[3 * a + j], recv_sem=recv2.at[3 * a + j],
                    device_id=sib, device_id_type=MESH).wait_send()
                pltpu.make_async_remote_copy(
                    src_ref=sent, dst_ref=got, send_sem=send2.at[3 * a + j], recv_sem=recv2.at[3 * a + j],
                    device_id=sib, device_id_type=MESH).wait_recv()

    out = pl.pallas_call(
        body, name=name,
        in_specs=[HBM] * (2 * n) + [SEM] * 4 + [ANY],
        out_specs=[HBM] * n,
        out_shape=_hbm_like(h["bufs"]),
        input_output_aliases={n + i: i for i in range(n)},
        compiler_params=pltpu.CompilerParams(has_side_effects=EFFECT),
    )(*h["shards"], *h["bufs"], h["send1"], h["recv1"], h["send2"], h["recv2"], after)
    return list(out)


def _peer_plan(kind, x, y, c):
    if kind == "pair":
        return [(2 * q + (1 - c), q, (x, y, 1 - c)) for q in range(4)]
    chips = [(1 - x, y), (x, 1 - y), (1 - x, 1 - y)]
    return [(2 * px + py, k, (px, py, c)) for k, (px, py) in enumerate(chips)]


def _exchange_start(kind, srcs, after, name):
    n = len(srcs)
    K = 4 if kind == "pair" else 3
    lands = [_hbm(lax.empty((K,) + s.shape[1:], s.dtype)) for s in srcs]

    def body(*refs):
        ins, land = refs[:n], refs[n:2 * n]
        send, recv = refs[2 * n + 1], refs[2 * n + 2]
        token = refs[-1]
        for a in range(n):
            for k, (si, di, dev) in enumerate(_peer_plan(kind, *_place())):
                pltpu.make_async_remote_copy(
                    src_ref=ins[a].at[si], dst_ref=land[a].at[di], send_sem=send.at[K * a + k], recv_sem=recv.at[K * a + k],
                    device_id=dev, device_id_type=MESH).start()
        token[...] = jnp.zeros_like(token)

    out = pl.pallas_call(
        body, name=name,
        in_specs=[HBM] * (2 * n) + [ANY],
        out_specs=[SEM, SEM] + [HBM] * (2 * n) + [VMEM_SPEC],
        out_shape=[pltpu.SemaphoreType.DMA((K * n,)), pltpu.SemaphoreType.DMA((K * n,))] + _hbm_like(srcs) + _hbm_like(lands) + [TOKEN],
        input_output_aliases={i: 2 + i for i in range(2 * n)},
        compiler_params=pltpu.CompilerParams(has_side_effects=EFFECT),
    )(*[_hbm(s) for s in srcs], *lands, after)
    return dict(kind=kind, send=out[0], recv=out[1], srcs=list(out[2:2 + n]), lands=list(out[2 + n:2 + 2 * n]), token=out[-1])


def _exchange_wait(h, after, name):
    n = len(h["srcs"])
    kind = h["kind"]
    K = 4 if kind == "pair" else 3

    def body(*refs):
        ins, land = refs[:n], refs[n:2 * n]
        send, recv = refs[2 * n], refs[2 * n + 1]
        for a in range(n):
            for k, (si, di, dev) in enumerate(_peer_plan(kind, *_place())):
                cp = pltpu.make_async_remote_copy(
                    src_ref=ins[a].at[si], dst_ref=land[a].at[di], send_sem=send.at[K * a + k], recv_sem=recv.at[K * a + k],
                    device_id=dev, device_id_type=MESH)
                cp.wait_send()
                cp.wait_recv()

    out = pl.pallas_call(
        body, name=name,
        in_specs=[HBM] * (2 * n) + [SEM, SEM, ANY],
        out_specs=[HBM] * (2 * n),
        out_shape=_hbm_like(h["srcs"]) + _hbm_like(h["lands"]),
        input_output_aliases={i: i for i in range(2 * n)},
        compiler_params=pltpu.CompilerParams(has_side_effects=EFFECT),
    )(*h["srcs"], *h["lands"], h["send"], h["recv"], after)
    return list(out[:n]), list(out[n:])


def _pair_add(g, got, c_idx, name):
    _, R, C = g.shape
    tr, tc = _tile2(R, C, 512, 16)

    def body(c_ref, a_ref, b_ref, o_ref):
        o_ref[0] = (a_ref[0].astype(f32) + b_ref[0].astype(f32)).astype(o_ref.dtype)

    return pl.pallas_call(
        body, name=name,
        grid_spec=pltpu.PrefetchScalarGridSpec(
            num_scalar_prefetch=1, grid=(4, R // tr, C // tc),
            in_specs=[pl.BlockSpec((1, tr, tc), lambda q, i, j, c: (2 * q + c[0], i, j)),
                      pl.BlockSpec((1, tr, tc), lambda q, i, j, c: (q, i, j))],
            out_specs=pl.BlockSpec((1, tr, tc), lambda q, i, j, c: (q, i, j))),
        out_shape=jax.ShapeDtypeStruct((4, R, C), g.dtype),
        compiler_params=_cparams(("parallel", "parallel", "parallel")),
    )(c_idx, g, got)


def _small_exchange(send, gather, name):
    R = send.shape[-2]

    def body(in_ref, out_ref, send_sems, recv_sems):
        x, y, c = _place()
        me = 4 * x + 2 * y + c
        out_ref[me] = in_ref[...] if gather else in_ref[me]
        cps = []
        for k in range(1, N_DEV):
            px, py, pc = x ^ ((k >> 2) & 1), y ^ ((k >> 1) & 1), c ^ (k & 1)
            src = in_ref if gather else in_ref.at[4 * px + 2 * py + pc]
            cps.append(pltpu.make_async_remote_copy(
                src_ref=src, dst_ref=out_ref.at[me],
                send_sem=send_sems.at[k - 1], recv_sem=recv_sems.at[k - 1],
                device_id=(px, py, pc), device_id_type=MESH))
        for cp in cps:
            cp.start()
        for cp in cps:
            cp.wait()

    return pl.pallas_call(
        body, name=name,
        in_specs=[pl.BlockSpec(memory_space=pltpu.VMEM)], out_specs=pl.BlockSpec(memory_space=pltpu.VMEM),
        out_shape=jax.ShapeDtypeStruct((N_DEV, R, LANES), f32),
        scratch_shapes=[pltpu.SemaphoreType.DMA((N_DEV - 1,)), pltpu.SemaphoreType.DMA((N_DEV - 1,))],
    )(send)


def _sum_blocks(blocks, name):
    def body(in_ref, o_ref):
        s = in_ref[0]
        for d in range(1, N_DEV):
            s = s + in_ref[d]
        o_ref[0] = s

    return pl.pallas_call(body, name=name, out_shape=jax.ShapeDtypeStruct((1,) + blocks.shape[1:], f32))(blocks)


def _rows(n):
    return -(-n // LANES)


def _pack(arrs, total_rows):
    parts = []
    for a in arrs:
        flat = a.reshape(-1).astype(f32)
        parts.append(jnp.pad(flat, (0, _rows(flat.size) * LANES - flat.size)))
    flat = jnp.concatenate(parts)
    return jnp.pad(flat, (0, total_rows * LANES - flat.size)).reshape(total_rows, LANES)


def _unpack(packed, shapes):
    lead = packed.shape[:-2]
    flat = packed.reshape(lead + (-1,))
    out, pos = [], 0
    for s in shapes:
        n = 1
        for d in s:
            n *= d
        out.append(flat[..., pos:pos + n].reshape(lead + tuple(s)))
        pos += _rows(n) * LANES
    return out


def _to_shards(full, axis):
    s = full.shape
    return jnp.moveaxis(full.reshape(s[:axis] + (N_DEV, s[axis] // N_DEV) + s[axis + 1:]), axis, 0)


def _from_shards(sh, axis):
    m = jnp.moveaxis(sh, 0, axis)
    s = m.shape
    return m.reshape(s[:axis] + (s[axis] * s[axis + 1],) + s[axis + 2:])


def kernel(x, meta, ffn_norm, ffn_w_gate, ffn_w_up, ffn_w_down, gla_norm, gla_w_in, gla_w_lr, gla_b_lr, gla_head_norm, gla_w_out, pool_norm, pool_w, pool_b, pool_scale, final_norm, loss_target, m_meta, m_ffn_norm, m_ffn_w_gate, m_ffn_w_up, m_ffn_w_down, m_gla_norm, m_gla_w_in, m_gla_w_lr, m_gla_b_lr, m_gla_head_norm, m_gla_w_out, m_pool_norm, m_pool_w, m_pool_b, m_pool_scale, m_final_norm, v_meta, v_ffn_norm, v_ffn_w_gate, v_ffn_w_up, v_ffn_w_down, v_gla_norm, v_gla_w_in, v_gla_w_lr, v_gla_b_lr, v_gla_head_norm, v_gla_w_out, v_pool_norm, v_pool_w, v_pool_b, v_pool_scale, v_final_norm):
    H = GLA_HEADS
    _, SEQ, D = x.shape
    Fs = ffn_w_gate.shape[-1]
    DK, DV = D // 2, D
    hv = DV // H
    GW = D // POOL_GROUPS
    INW = 2 * DK + 2 * DV + GATE_RANK
    NPK = 2 * DK + 2 * DV + GATE_PAD
    pad = (-N_META) % GLA_CHUNK
    first = pad + N_META
    Lp = first + SEQ
    n_units = ffn_w_gate.shape[0] * ffn_w_gate.shape[1]
    assert first % GLA_CHUNK == 0 and Lp % GLA_CHUNK == 0 and pad >= POOL_GROUPS * 4

    px, py, pc = _place()
    c_idx = jnp.reshape(pc, (1,)).astype(jnp.int32)
    q_idx = jnp.reshape(2 * px + py, (1,)).astype(jnp.int32)
    zero_idx = jnp.zeros((1,), jnp.int32)

    small_sh = [meta, ffn_norm, gla_w_lr, pool_norm, pool_b, pool_scale]
    small_axis = [1, 2, 2, 1, 2, 1]
    sh_shapes = [a.shape for a in small_sh]
    sh_rows = -(-sum(_rows(a.size) for a in small_sh) // 8) * 8
    gathered = _small_exchange(_pack(small_sh, sh_rows), True, "small_gather")
    meta_f, ffn_norm_f, wlr_f, pool_norm_f, pool_b_f, pool_scale_f = [
        _from_shards(a, ax) for a, ax in zip(_unpack(gathered, sh_shapes), small_axis)]
    ffn_norm_f = ffn_norm_f.reshape(n_units, 1, D)
    wlr128 = jnp.pad(wlr_f[0], ((0, GATE_PAD - GATE_RANK), (0, 0)))

    def t_units(w):
        return jnp.swapaxes(w, -1, -2).reshape(n_units, Fs, D)

    wg_l = t_units(ffn_w_gate).astype(bf16)
    wu_l = t_units(ffn_w_up).astype(bf16)
    wd_l = ffn_w_down.reshape(n_units, Fs, D).astype(bf16)
    ffn_shards = [[wg_l[u], wu_l[u], wd_l[u]] for u in range(n_units)]
    mixer_shards = [gla_w_in[0].T.astype(bf16), gla_w_out[0].astype(bf16), pool_w[0].astype(bf16)]
    gather_order = [("ffn0", ffn_shards[0]), ("mixers", mixer_shards)] + [(f"ffn{u}", ffn_shards[u]) for u in range(1, n_units)]
    c_lr = 2 * DK + DV
    c_r = 2 * DK + 2 * DV
    gate_blk = c_r // GATE_PAD

    def gather_begin(i, after):
        tag, shards = gather_order[i]
        return _gather_start(shards, after, f"gather_start_{tag}")

    def gather_next(i, h, after):
        tag = gather_order[i][0]
        h = _gather_mid(h, after, f"gather_mid_{tag}")
        nxt = gather_begin(i + 1, h["token"]) if i + 1 < len(gather_order) else None
        done = _gather_end(h, h["token"] if nxt is None else nxt["token"], f"gather_end_{tag}")
        return done, nxt

    xs = jnp.concatenate([jnp.zeros((pad, D), f32), meta_f, x[0]], axis=0)
    saved = {}
    ffn_w = [None] * n_units

    def ffn_f(u, xs):
        out, h, G, U = _ffn_fwd(xs, ffn_norm_f[u], *ffn_w[u], name=f"ffn_fwd{u}")
        saved[("ffn", u)] = (xs, h, G, U)
        return out

    def gla_f(xs, win_p, wout_full):
        hn = _rms_fwd(xs, gla_norm, bf16, "gla_norm_fwd")
        proj = _mm(hn, win_p, "nt", f32, "gla_proj", tm=1056, tn=896, tk=2048)
        lg = _gate_fwd(proj, wlr128, gla_b_lr, pad, gate_blk, "gla_gate_fwd")
        o, y, states = _gla_fwd(proj, lg, gla_head_norm, H, "gla_core_fwd")
        out = _mm(y, wout_full, "nn", f32, "gla_out", tm=1056, tn=512, tk=2048, residual=xs)
        saved["gla"] = (xs, hn, proj, lg, o, y, states)
        return out

    def pool_f(xs, wpool_full):
        hn = _rms_fwd(xs, pool_norm_f, f32, "pool_norm_fwd")
        pooled = _pool_windows(hn, pad, Lp - pad, "pool_windows_fwd")
        out = _pool_mix_fwd(xs, pooled, wpool_full, pool_b_f.reshape(1, D), pool_scale_f, pad, "pool_mix_fwd")
        saved["pool"] = (xs, pooled)
        return out

    depth = ffn_w_gate.shape[0]
    assert depth == 2 and n_units == 4
    h = gather_begin(0, gathered)
    ffn_w[0], h = gather_next(0, h, h["token"])
    xs = ffn_f(0, xs)
    (win_g, wout_g, wpool_g), h = gather_next(1, h, xs)
    win_full = win_g.reshape(INW, D)
    win_p = jnp.concatenate([win_full[:c_lr], win_full[c_lr + GATE_RANK:], win_full[c_lr:c_lr + GATE_RANK],
                             jnp.zeros((GATE_PAD - GATE_RANK, D), bf16)], axis=0)
    wout_full = wout_g.reshape(DV, D)
    wpool_full = _from_shards(wpool_g, 1)
    xs = gla_f(xs, win_p, wout_full)
    ffn_w[1], h = gather_next(2, h, xs)
    xs = ffn_f(1, xs)
    ffn_w[2], h = gather_next(3, h, xs)
    xs = ffn_f(2, xs)
    xs = pool_f(xs, wpool_full)
    ffn_w[3], h = gather_next(4, h, xs)
    xs = ffn_f(3, xs)
    loss_part, dxs, d_final = _loss_head(xs, loss_target[0], final_norm.reshape(1, D), first, "loss_head")

    class Reduce:
        def __init__(self, tag, grads, after=None):
            self.tag = tag
            self.h = _exchange_start("pair", grads, grads[0] if after is None else after, f"pair_start_{tag}")
            self.token = self.h["token"]

        def mid(self, after):
            grads, got = _exchange_wait(self.h, after, f"pair_wait_{self.tag}")
            self.sums = [_pair_add(g, r, c_idx, f"pair_add_{self.tag}{a}") for a, (g, r) in enumerate(zip(grads, got))]
            self.h = _exchange_start("chips", self.sums, self.sums[-1], f"chips_start_{self.tag}")
            self.token = self.h["token"]

        def end(self, after):
            sums, recv = _exchange_wait(self.h, after, f"chips_wait_{self.tag}")
            return list(zip(sums, recv))

    d_ffn_norm = [None] * n_units
    small_grads = {}

    def ffn_b(u, dY, prev):
        xs_in, h_, G, U = saved[("ffn", u)]
        tok = None if prev is None else prev.token
        dxs, dyh, dG, dU, A, dg = _ffn_bwd_dgrad(dY, xs_in, ffn_norm_f[u], *ffn_w[u], G, U, pad, f"ffn_dgrad{u}", after=tok)
        if prev is not None:
            prev.mid(dxs)
            tok = prev.token
        dwg, dwu, dwd = _ffn_bwd_wgrad(h_, dyh, A, dG, dU, f"ffn_wgrad{u}", after=tok)
        d_ffn_norm[u] = dg
        return dxs, Reduce(f"ffn{u}", [dwg, dwu, dwd])

    def gla_b(dY, prev):
        xs_in, hn, proj, lg, o, y, states = saved["gla"]
        dyb = dY.astype(bf16)
        dy = _mm(dyb, wout_full, "nt", f32, "gla_out_dgrad", tm=1056, tn=512, tk=2048, after=prev.token)
        dwout = _mm(y, dyb, "tn", bf16, "gla_out_wgrad", tm=1024, tn=1024, tk=528, after=prev.token)
        prev.mid(dwout)
        dq, dk, dv, dr, dlg, dhw = _gla_bwd(dy, proj, lg, o, states, gla_head_norm, H, pad, "gla_core_bwd", after=prev.token)
        dlr, dwlr, dblr = _gate_bwd(dlg, proj, wlr128, gla_b_lr, pad, gate_blk, "gla_gate_bwd")
        dproj = jnp.concatenate([dq, dk, dv, dr, dlr], axis=1)
        dwin_p = _mm(dproj, hn, "tn", bf16, "gla_proj_wgrad", tm=896, tn=1024, tk=528)
        dhn = _mm(dproj, win_p, "nn", f32, "gla_proj_dgrad", tm=1056, tn=512, tk=896)
        dxs, dgn = _rms_bwd(dY, dhn, xs_in, gla_norm, pad, "gla_norm_bwd")
        dwin = jnp.concatenate([dwin_p[:c_lr], dwin_p[c_r:c_r + GATE_RANK], dwin_p[c_lr:c_r]], axis=0)
        small_grads.update(gla_w_lr=dwlr[:GATE_RANK][None], gla_b_lr=dblr, gla_head_norm=dhw, gla_norm=dgn)
        return dxs, Reduce("gla", [dwin.reshape(N_DEV, INW // N_DEV, D), dwout.reshape(N_DEV, DV // N_DEV, D)])

    def pool_b_(dY, prev):
        xs_in, pooled = saved["pool"]
        dp, dw, db, ds = _pool_mix_bwd(dY, pooled, wpool_full, pool_b_f.reshape(1, D), pool_scale_f, pad, "pool_mix_bwd",
                                       after=prev.token)
        dhn = _pool_windows_bwd(dp, pad, "pool_windows_bwd")
        dxs, dgn = _rms_bwd(dY, dhn, xs_in, pool_norm_f, pad, "pool_norm_bwd")
        prev.mid(dxs)
        dws = _to_shards(dw, 1)
        small_grads.update(pool_b=db.reshape(1, POOL_GROUPS, GW), pool_scale=ds, pool_norm=dgn)
        return dxs, Reduce("pool", [dws.reshape(N_DEV, POOL_GROUPS * GW // N_DEV, GW)], after=prev.token)

    dxs, r3 = ffn_b(3, dxs, None)
    dxs, rp = pool_b_(dxs, r3)
    dxs, r2 = ffn_b(2, dxs, rp)
    dxs, r1 = ffn_b(1, dxs, r2)
    dxs, rg = gla_b(dxs, r1)
    dxs, r0 = ffn_b(0, dxs, rg)
    grad_x = dxs[first:].reshape(x.shape)
    small_grads.update(meta=dxs[pad:first], ffn_norm=jnp.concatenate(d_ffn_norm, axis=0).reshape(ffn_norm_f.shape[0] // 2, 2, D),
                       final_norm=d_final.reshape(D))

    sh_names = ["meta", "ffn_norm", "gla_w_lr", "pool_norm", "pool_b", "pool_scale"]
    rep_names = ["gla_norm", "gla_b_lr", "gla_head_norm", "final_norm"]
    rep_w = [gla_norm, gla_b_lr, gla_head_norm, final_norm]
    rep_shapes = [a.shape for a in rep_w]
    rep_rows = -(-sum(_rows(a.size) for a in rep_w) // 8) * 8
    by_owner = [_to_shards(small_grads[nm].reshape(full_shape), ax) for nm, full_shape, ax in zip(
        sh_names, [meta_f.shape, (ffn_norm.shape[0], 2, D), wlr_f.shape, pool_norm_f.shape, pool_b_f.shape, pool_scale_f.shape],
        small_axis)]
    rep_pack = _pack([small_grads[nm].reshape(s) for nm, s in zip(rep_names, rep_shapes)], rep_rows)
    send = jnp.stack([
        jnp.concatenate([_pack([g[d] for g in by_owner], sh_rows), rep_pack, loss_part], axis=0) for d in range(N_DEV)])
    total = _sum_blocks(_small_exchange(send, False, "small_reduce"), "small_sum")
    loss = total[0, sh_rows + rep_rows, 0]
    n_small = sh_rows + rep_rows
    g_small = total[:, :n_small]

    def pack_small(sh_list, rep_list):
        return jnp.concatenate([_pack(sh_list, sh_rows), _pack(rep_list, rep_rows)], axis=0)[None]

    w_small = pack_small(small_sh, rep_w)
    m_small = pack_small([m_meta, m_ffn_norm, m_gla_w_lr, m_pool_norm, m_pool_b, m_pool_scale],
                         [m_gla_norm, m_gla_b_lr, m_gla_head_norm, m_final_norm])
    v_small = pack_small([v_meta, v_ffn_norm, v_gla_w_lr, v_pool_norm, v_pool_b, v_pool_scale],
                         [v_gla_norm, v_gla_b_lr, v_gla_head_norm, v_final_norm])
    small_out = _adamw(w_small, m_small, v_small, 0, g_small, zero_idx, None, None, "adamw_small")
    small_res = {}
    for kind, packed in zip(("grad", "delta", "new_m", "new_v"), small_out):
        sh_vals = _unpack(packed[0, :sh_rows], sh_shapes)
        rep_vals = _unpack(packed[0, sh_rows:], rep_shapes)
        for nm, val in zip(sh_names + rep_names, sh_vals + rep_vals):
            small_res[(kind, nm)] = val

    big_res = {}

    def adam_one(nm, w, m, v, entry, transposed=False):
        sums, recv = entry
        R, C = sums.shape[1:]
        w1, m1, v1 = ((t[0].T if transposed else t).reshape(1, R, C) for t in (w, m, v))
        out = _adamw(w1, m1, v1, 0, sums, q_idx, recv, None, f"adamw_{nm}", after=r0.token)
        for kind, val in zip(("grad", "delta", "new_m", "new_v"), out):
            big_res[(kind, nm)] = val[0].T[None] if transposed else val.reshape(w.shape)
        return out[0]

    done = small_out[0]
    e_gla = rg.end(done)
    done = adam_one("gla_w_in", gla_w_in, m_gla_w_in, v_gla_w_in, e_gla[0], transposed=True)
    done = adam_one("gla_w_out", gla_w_out, m_gla_w_out, v_gla_w_out, e_gla[1])
    done = adam_one("pool_w", pool_w, m_pool_w, v_pool_w, rp.end(done)[0])
    r0.mid(done)

    ffn_names = ["ffn_w_gate", "ffn_w_up", "ffn_w_down"]
    ffn_wmv = [tuple(t_units(t) for t in (ffn_w_gate, m_ffn_w_gate, v_ffn_w_gate)),
               tuple(t_units(t) for t in (ffn_w_up, m_ffn_w_up, v_ffn_w_up)),
               tuple(t.reshape(n_units, Fs, D) for t in (ffn_w_down, m_ffn_w_down, v_ffn_w_down))]
    ffn_prev = [[lax.empty((n_units, Fs, D), f32) for _ in range(4)] for _ in range(3)]
    order_after = r0.token
    for u, red in ((3, r3), (2, r2), (1, r1), (0, r0)):
        entries = red.end(done)
        for a in range(3):
            sums, recv = entries[a]
            ffn_prev[a] = _adamw(*ffn_wmv[a], u, sums, q_idx, recv, ffn_prev[a], f"adamw_{ffn_names[a]}{u}", after=order_after)
            done = order_after = ffn_prev[a][0]
    for a in range(3):
        for kind, val in zip(("grad", "delta", "new_m", "new_v"), ffn_prev[a]):
            val = val.reshape(ffn_w_down.shape)
            big_res[(kind, ffn_names[a])] = val if a == 2 else jnp.swapaxes(val, -1, -2)

    order = ["meta", "ffn_norm", "ffn_w_gate", "ffn_w_up", "ffn_w_down", "gla_norm", "gla_w_in", "gla_w_lr", "gla_b_lr",
             "gla_head_norm", "gla_w_out", "pool_norm", "pool_w", "pool_b", "pool_scale", "final_norm"]
    res = {**small_res, **big_res}
    outs = [loss, grad_x]
    for kind in ("grad", "delta", "new_m", "new_v"):
        outs += [res[(kind, nm)] for nm in order]
    return tuple(outs)
```

```python
import functools

import jax
import jax.numpy as jnp
from jax import lax
from jax.experimental import pallas as pl
from jax.experimental.pallas import tpu as pltpu

f32 = jnp.float32
bf16 = jnp.bfloat16

N_DEV = 8
N_META = 16
GLA_HEADS = 4
GLA_CHUNK = 64
GLA_SUB = 16
GATE_RANK = 16
GATE_PAD = 128
GATE_NORM = 16.0
EPS = 1e-6
POOL_GROUPS = 4
ADAM_LR = 0.001
ADAM_B1 = 0.9
ADAM_B2 = 0.999
ADAM_EPS = 1e-08
ADAM_WD = 0.01
ADAM_STEP = 10
LANES = 128
VMEM_LIMIT_MB = 56

NN = (((1,), (0,)), ((), ()))
NT = (((1,), (1,)), ((), ()))
TN = (((0,), (0,)), ((), ()))
HI = lax.Precision.HIGHEST
MESH = pl.DeviceIdType.MESH
ANY = pl.BlockSpec(memory_space=pl.ANY)


def _cparams(sem=None, vmem_mb=None):
    kw = {}
    if sem is not None:
        kw["dimension_semantics"] = sem
    if vmem_mb is not None:
        kw["vmem_limit_bytes"] = vmem_mb * 2 ** 20
    return pltpu.CompilerParams(**kw)


def _tile(n, target, mult=16):
    best = None
    for t in range(mult, min(n, target) + 1, mult):
        if n % t == 0:
            best = t
    assert best is not None, (n, target, mult)
    return best


def _tile2(R, C, rows, mult):
    if R % mult == 0:
        return _tile(R, rows, mult), C
    return R, _tile(C, 256, LANES)


def _dot(a, b, dims=NN, precision=None):
    return lax.dot_general(a, b, dims, preferred_element_type=f32, precision=precision)


def _sigmoid(x):
    return 1.0 / (1.0 + jnp.exp(-x))


def _row_ids(tile_index, tm):
    return tile_index * tm + lax.broadcasted_iota(jnp.int32, (tm, 1), 0)


def _ordered(body, in_specs, args, after, lead=0):
    if after is None:
        return body, in_specs, args
    pos = lead + len(args)

    def body_without(*refs):
        return body(*refs[:pos], *refs[pos + 1:])

    return body_without, list(in_specs) + [ANY], list(args) + [after]


def _rms_fwd(xs, g, out_dtype, name):
    Lp, D = xs.shape
    tm = _tile(Lp, 528)

    def body(x_ref, g_ref, h_ref):
        x = x_ref[...]
        rstd = lax.rsqrt(jnp.mean(x * x, axis=-1, keepdims=True) + EPS)
        h_ref[...] = (x * rstd * g_ref[...]).astype(out_dtype)

    return pl.pallas_call(
        body, name=name, grid=(Lp // tm,),
        in_specs=[pl.BlockSpec((tm, D), lambda i: (i, 0)), pl.BlockSpec((1, D), lambda i: (0, 0))],
        out_specs=pl.BlockSpec((tm, D), lambda i: (i, 0)),
        out_shape=jax.ShapeDtypeStruct((Lp, D), out_dtype),
        compiler_params=_cparams(("parallel",)),
    )(xs, g)


def _rms_bwd(dY, dh, xs, g, pad, name):
    Lp, D = xs.shape
    tm = _tile(Lp, 352)

    def body(dY_ref, dh_ref, x_ref, g_ref, dxs_ref, dg_ref, half_ref):
        i = pl.program_id(0)

        @pl.when(i == 0)
        def _():
            dg_ref[...] = jnp.zeros_like(dg_ref)

        x = x_ref[...]
        rstd = lax.rsqrt(jnp.mean(x * x, axis=-1, keepdims=True) + EPS)
        xhat = x * rstd
        dh_ = dh_ref[...]
        dg_ref[...] += jnp.sum(dh_ * xhat, axis=0, keepdims=True)
        dxh = dh_ * g_ref[...]
        dx = rstd * (dxh - xhat * jnp.mean(dxh * xhat, axis=-1, keepdims=True))
        out = jnp.where(_row_ids(i, tm) >= pad, dY_ref[...] + dx, 0.0)
        dxs_ref[...] = out
        half_ref[...] = (0.5 * out).astype(bf16)

    row = pl.BlockSpec((tm, D), lambda i: (i, 0))
    vec = pl.BlockSpec((1, D), lambda i: (0, 0))
    return pl.pallas_call(
        body, name=name, grid=(Lp // tm,),
        in_specs=[row, row, row, vec], out_specs=[row, vec, row],
        out_shape=[jax.ShapeDtypeStruct((Lp, D), f32), jax.ShapeDtypeStruct((1, D), f32), jax.ShapeDtypeStruct((Lp, D), bf16)],
        compiler_params=_cparams(("arbitrary",)),
    )(dY, dh, xs, g)


def _mm(a, b, mode, out_dtype, name, tm=512, tn=512, tk=512, residual=None, after=None):
    if mode == "nn":
        (M, K), N = a.shape, b.shape[1]
    elif mode == "nt":
        (M, K), N = a.shape, b.shape[0]
    else:
        (K, M), N = a.shape, b.shape[1]
    tm = _tile(M, tm, 16 if mode != "tn" else LANES) if M > tm else M
    tn = _tile(N, tn, LANES) if N > tn else N
    tk = _tile(K, tk, LANES if mode != "tn" else 16) if K > tk else K
    nk = K // tk
    dims = {"nn": NN, "nt": NT, "tn": TN}[mode]

    def body(*refs):
        if residual is None:
            a_ref, b_ref, o_ref, acc = refs
            r_ref = None
        else:
            a_ref, b_ref, r_ref, o_ref, acc = refs
        k = pl.program_id(2)

        @pl.when(k == 0)
        def _():
            acc[...] = jnp.zeros_like(acc)

        acc[...] += _dot(a_ref[...], b_ref[...], dims)

        @pl.when(k == nk - 1)
        def _():
            r = acc[...]
            if r_ref is not None:
                r = r + r_ref[...]
            o_ref[...] = r.astype(out_dtype)

    a_spec = pl.BlockSpec((tk, tm), lambda i, j, k: (k, i)) if mode == "tn" else pl.BlockSpec((tm, tk), lambda i, j, k: (i, k))
    b_spec = pl.BlockSpec((tn, tk), lambda i, j, k: (j, k)) if mode == "nt" else pl.BlockSpec((tk, tn), lambda i, j, k: (k, j))
    o_spec = pl.BlockSpec((tm, tn), lambda i, j, k: (i, j))
    in_specs = [a_spec, b_spec] + ([o_spec] if residual is not None else [])
    args = [a, b] + ([residual] if residual is not None else [])
    body, in_specs, args = _ordered(body, in_specs, args, after)
    return pl.pallas_call(
        body, name=name, grid=(M // tm, N // tn, nk),
        in_specs=in_specs, out_specs=o_spec,
        out_shape=jax.ShapeDtypeStruct((M, N), out_dtype),
        scratch_shapes=[pltpu.VMEM((tm, tn), f32)],
        compiler_params=_cparams(("parallel", "parallel", "arbitrary"), VMEM_LIMIT_MB),
    )(*args)


def _ffn_fwd(xs, g, wg, wu, wd, name):
    Lp, D = xs.shape
    nd, Fs, _ = wg.shape
    tm = _tile(Lp, 704)
    once = pl.Buffered(1)

    def body(x_ref, g_ref, wg_ref, wu_ref, wd_ref, out_ref, h_ref, G_ref, U_ref, hs, acc):
        j = pl.program_id(1)

        @pl.when(j == 0)
        def _():
            x = x_ref[...]
            rstd = lax.rsqrt(jnp.mean(x * x, axis=-1, keepdims=True) + EPS)
            h = (x * rstd * g_ref[...]).astype(bf16)
            hs[...] = h
            h_ref[...] = h
            acc[...] = jnp.zeros_like(acc)

        h = hs[...]
        G = _dot(h, wg_ref[0], NT)
        U = _dot(h, wu_ref[0], NT)
        G_ref[0] = G.astype(bf16)
        U_ref[0] = U.astype(bf16)
        A = (G * _sigmoid(G) * U).astype(bf16)
        acc[...] += _dot(A, wd_ref[0])

        @pl.when(j == nd - 1)
        def _():
            out_ref[...] = x_ref[...] + 0.5 * acc[...]

    row_f = pl.BlockSpec((tm, D), lambda i, j: (i, 0), pipeline_mode=once)
    act = pl.BlockSpec((1, tm, Fs), lambda i, j: (j, i, 0))
    return pl.pallas_call(
        body, name=name, grid=(Lp // tm, nd),
        in_specs=[row_f, pl.BlockSpec((1, D), lambda i, j: (0, 0)),
                  pl.BlockSpec((1, Fs, D), lambda i, j: (j, 0, 0)),
                  pl.BlockSpec((1, Fs, D), lambda i, j: (j, 0, 0)),
                  pl.BlockSpec((1, Fs, D), lambda i, j: (j, 0, 0))],
        out_specs=[row_f, pl.BlockSpec((tm, D), lambda i, j: (i, 0), pipeline_mode=once), act, act],
        out_shape=[jax.ShapeDtypeStruct((Lp, D), f32), jax.ShapeDtypeStruct((Lp, D), bf16),
                   jax.ShapeDtypeStruct((nd, Lp, Fs), bf16), jax.ShapeDtypeStruct((nd, Lp, Fs), bf16)],
        scratch_shapes=[pltpu.VMEM((tm, D), bf16), pltpu.VMEM((tm, D), f32)],
        compiler_params=_cparams(("parallel", "arbitrary"), VMEM_LIMIT_MB),
    )(xs, g, wg, wu, wd)


def _ffn_bwd_act(dyh, wd, G, U, name, after=None):
    Lp, D = dyh.shape
    nd, Fs, _ = wd.shape
    tm = _tile(Lp, 704)

    def body(dyh_ref, wd_ref, G_ref, U_ref, dG_ref, dU_ref, A_ref):
        dA = _dot(dyh_ref[...], wd_ref[0], NT)
        Gf = G_ref[0].astype(f32)
        Uf = U_ref[0].astype(f32)
        s = _sigmoid(Gf)
        silu = Gf * s
        dG_ref[0] = (dA * Uf * (s * (1.0 + Gf * (1.0 - s)))).astype(bf16)
        dU_ref[0] = (dA * silu).astype(bf16)
        A_ref[0] = (silu * Uf).astype(bf16)

    act = pl.BlockSpec((1, tm, Fs), lambda j, i: (j, i, 0))
    act_s = jax.ShapeDtypeStruct((nd, Lp, Fs), bf16)
    in_specs = [pl.BlockSpec((tm, D), lambda j, i: (i, 0)), pl.BlockSpec((1, Fs, D), lambda j, i: (j, 0, 0)), act, act]
    body, in_specs, args = _ordered(body, in_specs, [dyh, wd, G, U], after)
    return pl.pallas_call(
        body, name=name, grid=(nd, Lp // tm),
        in_specs=in_specs, out_specs=[act, act, act], out_shape=[act_s, act_s, act_s],
        compiler_params=_cparams(("parallel", "parallel"), VMEM_LIMIT_MB),
    )(*args)


def _ffn_bwd_dh(dG, dU, wg, wu, name):
    nd, Lp, Fs = dG.shape
    D = wg.shape[2]
    tm = _tile(Lp, 1056)

    def body(dG_ref, dU_ref, wg_ref, wu_ref, dh_ref, acc):
        j = pl.program_id(1)

        @pl.when(j == 0)
        def _():
            acc[...] = jnp.zeros_like(acc)

        acc[...] += _dot(dG_ref[0], wg_ref[0]) + _dot(dU_ref[0], wu_ref[0])

        @pl.when(j == nd - 1)
        def _():
            dh_ref[...] = acc[...]

    act = pl.BlockSpec((1, tm, Fs), lambda i, j: (j, i, 0))
    wrow = pl.BlockSpec((1, Fs, D), lambda i, j: (j, 0, 0))
    return pl.pallas_call(
        body, name=name, grid=(Lp // tm, nd),
        in_specs=[act, act, wrow, wrow],
        out_specs=pl.BlockSpec((tm, D), lambda i, j: (i, 0), pipeline_mode=pl.Buffered(1)),
        out_shape=jax.ShapeDtypeStruct((Lp, D), f32),
        scratch_shapes=[pltpu.VMEM((tm, D), f32)],
        compiler_params=_cparams(("parallel", "arbitrary"), VMEM_LIMIT_MB),
    )(dG, dU, wg, wu)


def _ffn_bwd_wgrad(act, rows, name, after=None):
    nd, Lp, Fs = act.shape
    D = rows.shape[1]

    def body(a_ref, r_ref, o_ref):
        o_ref[0] = _dot(a_ref[0], r_ref[...], TN).astype(bf16)

    in_specs = [pl.BlockSpec((1, Lp, Fs), lambda j: (j, 0, 0)),
                pl.BlockSpec((Lp, D), lambda j: (0, 0), pipeline_mode=pl.Buffered(1))]
    body, in_specs, args = _ordered(body, in_specs, [act, rows], after)
    return pl.pallas_call(
        body, name=name, grid=(nd,),
        in_specs=in_specs, out_specs=pl.BlockSpec((1, Fs, D), lambda j: (j, 0, 0)),
        out_shape=jax.ShapeDtypeStruct((nd, Fs, D), bf16),
        compiler_params=_cparams(("parallel",), VMEM_LIMIT_MB),
    )(*args)


def _gate_fwd(proj, wlr, blr, pad, gate_blk, name):
    Lp = proj.shape[0]
    DK = wlr.shape[1]
    tm = _tile(Lp, 528)

    def body(lr_ref, w_ref, b_ref, lg_ref):
        z = _dot(lr_ref[...].astype(bf16), w_ref[...].astype(bf16)) + b_ref[...]
        ls = jnp.minimum(z, 0.0) - jnp.log(1.0 + jnp.exp(-jnp.abs(z)))
        lg_ref[...] = jnp.where(_row_ids(pl.program_id(0), tm) >= pad, ls * (1.0 / GATE_NORM), 0.0)

    return pl.pallas_call(
        body, name=name, grid=(Lp // tm,),
        in_specs=[pl.BlockSpec((tm, GATE_PAD), lambda i: (i, gate_blk)),
                  pl.BlockSpec((GATE_PAD, DK), lambda i: (0, 0)), pl.BlockSpec((1, DK), lambda i: (0, 0))],
        out_specs=pl.BlockSpec((tm, DK), lambda i: (i, 0)),
        out_shape=jax.ShapeDtypeStruct((Lp, DK), f32),
        compiler_params=_cparams(("parallel",)),
    )(proj, wlr, blr)


def _gate_bwd(dlg, proj, wlr, blr, pad, gate_blk, name):
    Lp = proj.shape[0]
    DK = wlr.shape[1]
    tm = _tile(Lp, 528)

    def body(dlg_ref, lr_ref, w_ref, b_ref, dlr_ref, dw_ref, db_ref):
        i = pl.program_id(0)

        @pl.when(i == 0)
        def _():
            dw_ref[...] = jnp.zeros_like(dw_ref)
            db_ref[...] = jnp.zeros_like(db_ref)

        lr = lr_ref[...].astype(bf16)
        w = w_ref[...].astype(bf16)
        z = _dot(lr, w) + b_ref[...]
        dz = jnp.where(_row_ids(i, tm) >= pad, dlg_ref[...] * _sigmoid(-z) * (1.0 / GATE_NORM), 0.0)
        dzb = dz.astype(bf16)
        dlr_ref[...] = _dot(dzb, w, NT).astype(bf16)
        dw_ref[...] += _dot(lr, dzb, TN)
        db_ref[...] += jnp.sum(dz, axis=0, keepdims=True)

    return pl.pallas_call(
        body, name=name, grid=(Lp // tm,),
        in_specs=[pl.BlockSpec((tm, DK), lambda i: (i, 0)), pl.BlockSpec((tm, GATE_PAD), lambda i: (i, gate_blk)),
                  pl.BlockSpec((GATE_PAD, DK), lambda i: (0, 0)), pl.BlockSpec((1, DK), lambda i: (0, 0))],
        out_specs=[pl.BlockSpec((tm, GATE_PAD), lambda i: (i, 0)), pl.BlockSpec((GATE_PAD, DK), lambda i: (0, 0)),
                   pl.BlockSpec((1, DK), lambda i: (0, 0))],
        out_shape=[jax.ShapeDtypeStruct((Lp, GATE_PAD), bf16), jax.ShapeDtypeStruct((GATE_PAD, DK), f32),
                   jax.ShapeDtypeStruct((1, DK), f32)],
        compiler_params=_cparams(("arbitrary",)),
    )(dlg, proj, wlr, blr)


def _chunk_decay(lg):
    C = lg.shape[0]
    r = lax.broadcasted_iota(jnp.int32, (C, C), 0)
    c = lax.broadcasted_iota(jnp.int32, (C, C), 1)
    return _dot(jnp.where(r >= c, 1.0, 0.0).astype(f32), lg, NN, HI)


def _col(v):
    return jnp.transpose(jnp.broadcast_to(v, (8, v.shape[1])))[:, 0:1]


def _intra_scores(q, k, b, A_ref):
    C = q.shape[0]
    S = GLA_SUB
    A_ref[...] = jnp.zeros_like(A_ref)
    ri = lax.broadcasted_iota(jnp.int32, (S, 1), 0)
    for I in range(C // S):
        lo = S * I
        qI, bI = q[lo:lo + S], b[lo:lo + S]
        if I > 0:
            bref = b[lo - 1:lo]
            qs = qI * jnp.exp(bI - bref)
            ks = k[:lo] * jnp.exp(bref - b[:lo])
            A_ref[lo:lo + S, 0:lo] = _dot(qs, ks, NT, HI)
        for jj in range(S):
            j = lo + jj
            P = jnp.exp(jnp.minimum(bI - b[j:j + 1], 0.0))
            a = jnp.sum(qI * P * k[j:j + 1], axis=1, keepdims=True)
            A_ref[lo:lo + S, j:j + 1] = jnp.where(ri >= jj, a, 0.0)


def _intra_grads(q, k, b, dA, dq_ref, dk_ref):
    C = q.shape[0]
    S = GLA_SUB
    ri = lax.broadcasted_iota(jnp.int32, (S, 1), 0)
    for I in range(C // S):
        lo = S * I
        qI, bI = q[lo:lo + S], b[lo:lo + S]
        dqI = jnp.zeros_like(qI)
        if I > 0:
            bref = b[lo - 1:lo]
            eq = jnp.exp(bI - bref)
            ek = jnp.exp(bref - b[:lo])
            qs = qI * eq
            ks = k[:lo] * ek
            dAI = dA[lo:lo + S, 0:lo]
            dqI = dqI + _dot(dAI, ks, NN, HI) * eq
            dk_ref[0:lo, :] += _dot(dAI, qs, TN, HI) * ek
        for jj in range(S):
            j = lo + jj
            P = jnp.exp(jnp.minimum(bI - b[j:j + 1], 0.0))
            t = jnp.where(ri >= jj, dA[lo:lo + S, j:j + 1], 0.0) * P
            dqI = dqI + t * k[j:j + 1]
            dk_ref[j:j + 1, :] += jnp.sum(t * qI, axis=0, keepdims=True)
        dq_ref[lo:lo + S, :] += dqI


def _gla_fwd(proj, lg, hnw, H, name):
    Lp = proj.shape[0]
    DK = lg.shape[1]
    hk = DK // H
    hv = hnw.shape[1]
    DV = hv * H
    C = GLA_CHUNK
    NC = Lp // C
    scale = float(hk) ** -0.5
    kq, kv = DK // hk, (2 * DK) // hv
    kr = kv + H

    def body(q_ref, k_ref, v_ref, r_ref, lg_ref, w_ref, o_ref, y_ref, s_ref, S_scr, A_scr):
        c = pl.program_id(1)

        @pl.when(c == 0)
        def _():
            S_scr[...] = jnp.zeros_like(S_scr)

        q = q_ref[...] * scale
        k = k_ref[...]
        v = v_ref[...]
        b = _chunk_decay(lg_ref[...])
        bl = b[C - 1:C]
        S = S_scr[...]
        s_ref[0, 0] = S
        _intra_scores(q, k, b, A_scr)
        vb = v.astype(bf16)
        o = _dot((q * jnp.exp(b)).astype(bf16), S.astype(bf16)) + _dot(A_scr[...].astype(bf16), vb)
        kb = (k * jnp.exp(bl - b)).astype(bf16)
        S_scr[...] = jnp.exp(_col(bl)) * S + _dot(kb, vb, TN)
        o_ref[...] = o
        on = o * lax.rsqrt(jnp.mean(o * o, axis=-1, keepdims=True) + EPS) * w_ref[...]
        r = r_ref[...]
        y_ref[...] = (on * (r * _sigmoid(r))).astype(bf16)

    return pl.pallas_call(
        body, name=name, grid=(H, NC),
        in_specs=[pl.BlockSpec((C, hk), lambda h, c: (c, h)),
                  pl.BlockSpec((C, hk), lambda h, c: (c, kq + h)),
                  pl.BlockSpec((C, hv), lambda h, c: (c, kv + h)),
                  pl.BlockSpec((C, hv), lambda h, c: (c, kr + h)),
                  pl.BlockSpec((C, hk), lambda h, c: (c, h)),
                  pl.BlockSpec((1, hv), lambda h, c: (0, 0))],
        out_specs=[pl.BlockSpec((C, hv), lambda h, c: (c, h)), pl.BlockSpec((C, hv), lambda h, c: (c, h)),
                   pl.BlockSpec((1, 1, hk, hv), lambda h, c: (h, c, 0, 0))],
        out_shape=[jax.ShapeDtypeStruct((Lp, DV), f32), jax.ShapeDtypeStruct((Lp, DV), bf16),
                   jax.ShapeDtypeStruct((H, NC, hk, hv), f32)],
        scratch_shapes=[pltpu.VMEM((hk, hv), f32), pltpu.VMEM((C, C), f32)],
        compiler_params=_cparams(("parallel", "arbitrary")),
    )(proj, proj, proj, proj, lg, hnw)


def _gla_bwd(dy, proj, lg, o, states, hnw, H, pad, name, after=None):
    Lp = proj.shape[0]
    DK = lg.shape[1]
    hk = DK // H
    hv = hnw.shape[1]
    DV = hv * H
    C = GLA_CHUNK
    NC = Lp // C
    scale = float(hk) ** -0.5
    kq, kv = DK // hk, (2 * DK) // hv
    kr = kv + H

    def body(dy_ref, q_ref, k_ref, v_ref, r_ref, lg_ref, o_ref, s_ref, sn_ref, w_ref,
             dq_ref, dk_ref, dv_ref, dr_ref, dlg_ref, dw_ref, dS_scr, A_scr, dq_s, dk_s):
        h = pl.program_id(0)
        cc = pl.program_id(1)
        c = NC - 1 - cc

        @pl.when(cc == 0)
        def _():
            dS_scr[...] = jnp.zeros_like(dS_scr)

        @pl.when((cc == 0) & (h == 0))
        def _():
            dw_ref[...] = jnp.zeros_like(dw_ref)

        keep = (c * C + lax.broadcasted_iota(jnp.int32, (C, 1), 0)) >= pad
        w = w_ref[...]
        o_ = o_ref[...]
        rs = lax.rsqrt(jnp.mean(o_ * o_, axis=-1, keepdims=True) + EPS)
        ohat = o_ * rs
        r = r_ref[...]
        sg = _sigmoid(r)
        dy_ = dy_ref[...]
        d_on = dy_ * (r * sg)
        dr_ref[...] = jnp.where(keep, dy_ * (ohat * w) * (sg * (1.0 + r * (1.0 - sg))), 0.0).astype(bf16)
        dw_ref[...] += jnp.sum(d_on * ohat, axis=0, keepdims=True)
        d_oh = d_on * w
        do = rs * (d_oh - ohat * jnp.mean(d_oh * ohat, axis=-1, keepdims=True))
        dob = do.astype(bf16)
        q = q_ref[...] * scale
        k = k_ref[...]
        v = v_ref[...]
        vb = v.astype(bf16)
        b = _chunk_decay(lg_ref[...])
        bl = b[C - 1:C]
        eb = jnp.exp(b)
        ekb = jnp.exp(bl - b)
        S = s_ref[0, 0]
        dS = dS_scr[...]
        dSb = dS.astype(bf16)
        _intra_scores(q, k, b, A_scr)
        ri = lax.broadcasted_iota(jnp.int32, (C, C), 0)
        ci = lax.broadcasted_iota(jnp.int32, (C, C), 1)
        dA = jnp.where(ri >= ci, _dot(dob, vb, NT), 0.0)
        kb = (k * ekb).astype(bf16)
        qb = (q * eb).astype(bf16)
        dv = _dot(A_scr[...].astype(bf16), dob, TN) + _dot(kb, dSb)
        dq_s[...] = _dot(dob, S.astype(bf16), NT) * eb
        dk_s[...] = _dot(vb, dSb, NT) * ekb
        dS_scr[...] = _dot(qb, dob, TN) + jnp.exp(_col(bl)) * dS
        _intra_grads(q, k, b, dA, dq_s, dk_s)
        dq = dq_s[...]
        dk = dk_s[...]
        Dm = q * dq - k * dk
        after = _dot(jnp.ones((8, hv), f32), sn_ref[0, 0] * dS, NT, HI)[0:1]
        dlg = _dot(jnp.where(ri <= ci, 1.0, 0.0).astype(f32), Dm, NN, HI) + after
        dlg_ref[...] = jnp.where(keep, dlg, 0.0)
        dq_ref[...] = jnp.where(keep, dq * scale, 0.0).astype(bf16)
        dk_ref[...] = jnp.where(keep, dk, 0.0).astype(bf16)
        dv_ref[...] = jnp.where(keep, dv, 0.0).astype(bf16)

    rev = lambda h, cc: NC - 1 - cc
    in_specs = [pl.BlockSpec((C, hv), lambda h, cc: (rev(h, cc), h)),
                pl.BlockSpec((C, hk), lambda h, cc: (rev(h, cc), h)),
                pl.BlockSpec((C, hk), lambda h, cc: (rev(h, cc), kq + h)),
                pl.BlockSpec((C, hv), lambda h, cc: (rev(h, cc), kv + h)),
                pl.BlockSpec((C, hv), lambda h, cc: (rev(h, cc), kr + h)),
                pl.BlockSpec((C, hk), lambda h, cc: (rev(h, cc), h)),
                pl.BlockSpec((C, hv), lambda h, cc: (rev(h, cc), h)),
                pl.BlockSpec((1, 1, hk, hv), lambda h, cc: (h, rev(h, cc), 0, 0)),
                pl.BlockSpec((1, 1, hk, hv), lambda h, cc: (h, jnp.minimum(rev(h, cc) + 1, NC - 1), 0, 0)),
                pl.BlockSpec((1, hv), lambda h, cc: (0, 0))]
    body, in_specs, args = _ordered(body, in_specs, [dy, proj, proj, proj, proj, lg, o, states, states, hnw], after)
    return pl.pallas_call(
        body, name=name, grid=(H, NC),
        in_specs=in_specs,
        out_specs=[pl.BlockSpec((C, hk), lambda h, cc: (rev(h, cc), h)),
                   pl.BlockSpec((C, hk), lambda h, cc: (rev(h, cc), h)),
                   pl.BlockSpec((C, hv), lambda h, cc: (rev(h, cc), h)),
                   pl.BlockSpec((C, hv), lambda h, cc: (rev(h, cc), h)),
                   pl.BlockSpec((C, hk), lambda h, cc: (rev(h, cc), h)),
                   pl.BlockSpec((1, hv), lambda h, cc: (0, 0))],
        out_shape=[jax.ShapeDtypeStruct((Lp, DK), bf16), jax.ShapeDtypeStruct((Lp, DK), bf16),
                   jax.ShapeDtypeStruct((Lp, DV), bf16), jax.ShapeDtypeStruct((Lp, DV), bf16),
                   jax.ShapeDtypeStruct((Lp, DK), f32), jax.ShapeDtypeStruct((1, hv), f32)],
        scratch_shapes=[pltpu.VMEM((hk, hv), f32), pltpu.VMEM((C, C), f32),
                        pltpu.VMEM((C, hk), f32), pltpu.VMEM((C, hk), f32)],
        compiler_params=_cparams(("arbitrary", "arbitrary")),
    )(*args)


def _window_sums(x, back):
    n = x.shape[0]
    out = []
    s = x
    for w in (1, 2, 4, 8):
        s = s + pltpu.roll(s, w if back else n - w, 0)
        out.append(s)
    return out


def _pool_windows(hn, pad, n_real, name):
    Lp, D = hn.shape
    GW = D // POOL_GROUPS
    cb = min(GW, 256)
    per = GW // cb

    def body(h_ref, p_ref):
        g = pl.program_id(0) // per
        x = h_ref[...]
        s2, s4, s8, s16 = _window_sums(x, True)
        sel = jnp.where(g == 0, s2, jnp.where(g == 1, s4, jnp.where(g == 2, s8, s16)))
        win = jnp.left_shift(2, g).astype(f32)
        rows = lax.broadcasted_iota(jnp.int32, (Lp, 1), 0)
        t = (rows - pad).astype(f32)
        cnt = jnp.minimum(jnp.maximum(t, 0.0) + 1.0, win)
        p_ref[...] = jnp.where(rows >= pad, sel / cnt - x, 0.0).astype(bf16)

    return pl.pallas_call(
        body, name=name, grid=(D // cb,),
        in_specs=[pl.BlockSpec((Lp, cb), lambda i: (0, i))],
        out_specs=pl.BlockSpec((Lp, cb), lambda i: (0, i)),
        out_shape=jax.ShapeDtypeStruct((Lp, D), bf16),
        compiler_params=_cparams(("parallel",)),
    )(hn)


def _pool_windows_bwd(dp, pad, name):
    Lp, D = dp.shape
    GW = D // POOL_GROUPS
    cb = min(GW, 256)
    per = GW // cb

    def body(dp_ref, dh_ref):
        g = pl.program_id(0) // per
        rows = lax.broadcasted_iota(jnp.int32, (Lp, 1), 0)
        d = jnp.where(rows >= pad, dp_ref[...], 0.0)
        win = jnp.left_shift(2, g).astype(f32)
        t = (rows - pad).astype(f32)
        cnt = jnp.minimum(jnp.maximum(t, 0.0) + 1.0, win)
        s2, s4, s8, s16 = _window_sums(d / cnt, False)
        sel = jnp.where(g == 0, s2, jnp.where(g == 1, s4, jnp.where(g == 2, s8, s16)))
        dh_ref[...] = jnp.where(rows >= pad, sel - d, 0.0)

    return pl.pallas_call(
        body, name=name, grid=(D // cb,),
        in_specs=[pl.BlockSpec((Lp, cb), lambda i: (0, i))],
        out_specs=pl.BlockSpec((Lp, cb), lambda i: (0, i)),
        out_shape=jax.ShapeDtypeStruct((Lp, D), f32),
        compiler_params=_cparams(("parallel",)),
    )(dp)


def _pool_mix_fwd(xs, pooled, w, bias, scale, pad, name):
    Lp, D = xs.shape
    GW = D // POOL_GROUPS
    tm = _tile(Lp, 1056)

    def body(x_ref, p_ref, w_ref, b_ref, s_ref, o_ref):
        z = _dot(p_ref[...], w_ref[0]) + b_ref[...]
        keep = _row_ids(pl.program_id(1), tm) >= pad
        o_ref[...] = x_ref[...] + jnp.where(keep, z * s_ref[...], 0.0)

    blk = pl.BlockSpec((tm, GW), lambda g, i: (i, g))
    vec = pl.BlockSpec((1, GW), lambda g, i: (0, g))
    return pl.pallas_call(
        body, name=name, grid=(POOL_GROUPS, Lp // tm),
        in_specs=[blk, blk, pl.BlockSpec((1, GW, GW), lambda g, i: (g, 0, 0)), vec, vec],
        out_specs=blk, out_shape=jax.ShapeDtypeStruct((Lp, D), f32),
        compiler_params=_cparams(("parallel", "parallel")),
    )(xs, pooled, w, bias, scale)


def _pool_mix_bwd(dY, pooled, w, bias, scale, pad, name, after=None):
    Lp, D = dY.shape
    GW = D // POOL_GROUPS
    tm = _tile(Lp, 1056)
    nm = Lp // tm

    def body(dY_ref, p_ref, w_ref, b_ref, s_ref, dp_ref, dw_ref, db_ref, ds_ref, acc):
        i = pl.program_id(1)

        @pl.when(i == 0)
        def _():
            acc[...] = jnp.zeros_like(acc)
            db_ref[...] = jnp.zeros_like(db_ref)
            ds_ref[...] = jnp.zeros_like(ds_ref)

        keep = _row_ids(i, tm) >= pad
        dY_ = jnp.where(keep, dY_ref[...], 0.0)
        p = p_ref[...]
        z = _dot(p, w_ref[0]) + b_ref[...]
        ds_ref[...] += jnp.sum(dY_ * z, axis=0, keepdims=True)
        dz = dY_ * s_ref[...]
        db_ref[...] += jnp.sum(dz, axis=0, keepdims=True)
        dzb = dz.astype(bf16)
        acc[...] += _dot(p, dzb, TN)
        dp_ref[...] = _dot(dzb, w_ref[0], NT)

        @pl.when(i == nm - 1)
        def _():
            dw_ref[0] = acc[...].astype(bf16)

    blk = pl.BlockSpec((tm, GW), lambda g, i: (i, g))
    vec = pl.BlockSpec((1, GW), lambda g, i: (0, g))
    wsp = pl.BlockSpec((1, GW, GW), lambda g, i: (g, 0, 0))
    body, in_specs, args = _ordered(body, [blk, blk, wsp, vec, vec], [dY, pooled, w, bias, scale], after)
    return pl.pallas_call(
        body, name=name, grid=(POOL_GROUPS, nm),
        in_specs=in_specs, out_specs=[blk, wsp, vec, vec],
        out_shape=[jax.ShapeDtypeStruct((Lp, D), f32), jax.ShapeDtypeStruct((POOL_GROUPS, GW, GW), bf16),
                   jax.ShapeDtypeStruct((1, D), f32), jax.ShapeDtypeStruct((1, D), f32)],
        scratch_shapes=[pltpu.VMEM((GW, GW), f32)],
        compiler_params=_cparams(("parallel", "arbitrary")),
    )(*args)


def _loss_head(xs, target, g, first, name):
    Lp, D = xs.shape
    tm = GLA_CHUNK
    off = first // tm

    def body(x_ref, t_ref, g_ref, loss_ref, dxs_ref, dg_ref, half_ref):
        i = pl.program_id(0)

        @pl.when(i == 0)
        def _():
            loss_ref[...] = jnp.zeros_like(loss_ref)
            dg_ref[...] = jnp.zeros_like(dg_ref)

        @pl.when(i < off)
        def _():
            dxs_ref[...] = jnp.zeros_like(dxs_ref)
            half_ref[...] = jnp.zeros_like(half_ref)

        @pl.when(i >= off)
        def _():
            x = x_ref[...]
            rstd = lax.rsqrt(jnp.mean(x * x, axis=-1, keepdims=True) + EPS)
            xhat = x * rstd
            gg = g_ref[...]
            err = xhat * gg - t_ref[...]
            loss_ref[...] += 0.5 * jnp.sum(jnp.mean(err * err, axis=-1, keepdims=True))
            dy = err * (1.0 / D)
            dg_ref[...] += jnp.sum(dy * xhat, axis=0, keepdims=True)
            dxh = dy * gg
            out = rstd * (dxh - xhat * jnp.mean(dxh * xhat, axis=-1, keepdims=True))
            dxs_ref[...] = out
            half_ref[...] = (0.5 * out).astype(bf16)

    row = pl.BlockSpec((tm, D), lambda i: (i, 0))
    return pl.pallas_call(
        body, name=name, grid=(Lp // tm,),
        in_specs=[row, pl.BlockSpec((tm, D), lambda i: (jnp.maximum(i - off, 0), 0)), pl.BlockSpec((1, D), lambda i: (0, 0))],
        out_specs=[pl.BlockSpec((8, LANES), lambda i: (0, 0)), row, pl.BlockSpec((1, D), lambda i: (0, 0)), row],
        out_shape=[jax.ShapeDtypeStruct((8, LANES), f32), jax.ShapeDtypeStruct((Lp, D), f32),
                   jax.ShapeDtypeStruct((1, D), f32), jax.ShapeDtypeStruct((Lp, D), bf16)],
        compiler_params=_cparams(("arbitrary",)),
    )(xs, target, g)


def _adam_math(w, g, m, v):
    m2 = ADAM_B1 * m + (1.0 - ADAM_B1) * g
    v2 = ADAM_B2 * v + (1.0 - ADAM_B2) * (g * g)
    m_hat = m2 / (1.0 - ADAM_B1 ** ADAM_STEP)
    v_hat = v2 / (1.0 - ADAM_B2 ** ADAM_STEP)
    delta = -ADAM_LR * (m_hat / (jnp.sqrt(v_hat) + ADAM_EPS) + ADAM_WD * w)
    return delta, m2, v2


def _adamw(w, m, v, unit, own, own_idx, recv, prev, name, after=None):
    U, R, C = w.shape
    tr, tc = _tile2(R, C, 256, 8 if own.dtype == f32 and recv is None else 16)
    n_recv = 0 if recv is None else recv.shape[0]

    def body(idx_ref, w_ref, m_ref, v_ref, own_ref, *rest):
        rest = list(rest)
        recv_refs = [rest.pop(0) for _ in range(n_recv)]
        if prev is not None:
            rest = rest[4:]
        g_ref, d_ref, m2_ref, v2_ref = rest
        g = own_ref[0].astype(f32)
        for r_ref in recv_refs:
            g = g + r_ref[0].astype(f32)
        delta, m2, v2 = _adam_math(w_ref[0], g, m_ref[0], v_ref[0])
        g_ref[0] = g
        d_ref[0] = delta
        m2_ref[0] = m2
        v2_ref[0] = v2

    blk = pl.BlockSpec((1, tr, tc), lambda i, j, idx: (unit, i, j))
    in_specs = [blk, blk, blk, pl.BlockSpec((1, tr, tc), lambda i, j, idx: (idx[0], i, j))]
    args = [w, m, v, own]
    for p in range(n_recv):
        in_specs.append(pl.BlockSpec((1, tr, tc), lambda i, j, idx, p=p: (p, i, j)))
        args.append(recv)
    aliases = {}
    if prev is not None:
        for t in range(4):
            aliases[1 + len(args) + t] = t
        in_specs += [ANY] * 4
        args += list(prev)
    body, in_specs, args = _ordered(body, in_specs, args, after, lead=1)
    out = jax.ShapeDtypeStruct((U, R, C), f32)
    return pl.pallas_call(
        body, name=name,
        grid_spec=pltpu.PrefetchScalarGridSpec(
            num_scalar_prefetch=1, grid=(R // tr, C // tc), in_specs=in_specs, out_specs=[blk] * 4),
        out_shape=[out] * 4, input_output_aliases=aliases,
        compiler_params=_cparams(("parallel", "parallel")),
    )(own_idx, *args)


def _place():
    return lax.axis_index("x"), lax.axis_index("y"), lax.axis_index("c")


HBM = pl.BlockSpec(memory_space=pltpu.HBM)
SEM = pl.BlockSpec(memory_space=pltpu.SEMAPHORE)
VMEM_SPEC = pl.BlockSpec(memory_space=pltpu.VMEM)
EFFECT = pltpu.SideEffectType.DATAFLOW_SIDE_EFFECTING
TOKEN = jax.ShapeDtypeStruct((8, LANES), f32)


def _hbm(x):
    return pltpu.with_memory_space_constraint(x, pltpu.HBM)


def _hbm_like(xs):
    return [pltpu.HBM(x.shape, x.dtype) for x in xs]


def _slot(px, py, pc):
    return 4 * px + 2 * py + pc


def _gather_start(shards, after, name):
    n = len(shards)
    me = _slot(*_place())
    bufs = [lax.dynamic_update_slice(lax.empty((N_DEV,) + s.shape, s.dtype), s[None], (me,) + (0,) * s.ndim) for s in shards]

    def body(*refs):
        ins, land = refs[:n], refs[n:2 * n]
        send, recv = refs[2 * n + 1], refs[2 * n + 2]
        token = refs[-1]
        x, y, c = _place()
        to = [(x, y, 1 - c), (1 - x, y, c), (x, 1 - y, c), (1 - x, 1 - y, c)]
        for a in range(n):
            for k, dev in enumerate(to):
                pltpu.make_async_remote_copy(
                    src_ref=ins[a], dst_ref=land[a].at[_slot(x, y, c)], send_sem=send.at[4 * a + k], recv_sem=recv.at[4 * a + k],
                    device_id=dev, device_id_type=MESH).start()
        token[...] = jnp.zeros_like(token)

    out = pl.pallas_call(
        body, name=name,
        in_specs=[HBM] * (2 * n) + [ANY],
        out_specs=[SEM, SEM] + [HBM] * (2 * n) + [VMEM_SPEC],
        out_shape=[pltpu.SemaphoreType.DMA((4 * n,)), pltpu.SemaphoreType.DMA((4 * n,))] + _hbm_like(shards) + _hbm_like(bufs) + [TOKEN],
        input_output_aliases={i: 2 + i for i in range(2 * n)},
        compiler_params=pltpu.CompilerParams(has_side_effects=EFFECT),
    )(*[_hbm(s) for s in shards], *[_hbm(b) for b in bufs], after)
    return dict(send1=out[0], recv1=out[1], shards=list(out[2:2 + n]), bufs=list(out[2 + n:2 + 2 * n]), token=out[-1])


def _gather_mid(h, after, name):
    n = len(h["bufs"])

    def body(*refs):
        land, recv1 = refs[:n], refs[n]
        send2, recv2 = refs[n + 2], refs[n + 3]
        token = refs[-1]
        x, y, c = _place()
        chips = [(1 - x, y), (x, 1 - y), (1 - x, 1 - y)]
        for j, (px, py) in enumerate(chips):
            for a in range(n):
                blk = land[a].at[_slot(px, py, c)]
                pltpu.make_async_remote_copy(
                    src_ref=blk, dst_ref=blk, send_sem=send2.at[3 * a + j], recv_sem=recv1.at[4 * a + 1 + j],
                    device_id=(px, py, c), device_id_type=MESH).wait_recv()
                pltpu.make_async_remote_copy(
                    src_ref=blk, dst_ref=blk, send_sem=send2.at[3 * a + j], recv_sem=recv2.at[3 * a + j],
                    device_id=(x, y, 1 - c), device_id_type=MESH).start()
        token[...] = jnp.zeros_like(token)

    out = pl.pallas_call(
        body, name=name,
        in_specs=[HBM] * n + [SEM, ANY],
        out_specs=[SEM, SEM] + [HBM] * n + [VMEM_SPEC],
        out_shape=[pltpu.SemaphoreType.DMA((3 * n,)), pltpu.SemaphoreType.DMA((3 * n,))] + _hbm_like(h["bufs"]) + [TOKEN],
        input_output_aliases={i: 2 + i for i in range(n)},
        compiler_params=pltpu.CompilerParams(has_side_effects=EFFECT),
    )(*h["bufs"], h["recv1"], after)
    h.update(send2=out[0], recv2=out[1], bufs=list(out[2:2 + n]), token=out[-1])
    return h


def _gather_end(h, after, name):
    n = len(h["bufs"])

    def body(*refs):
        ins, land = refs[:n], refs[n:2 * n]
        send1, recv1, send2, recv2 = refs[2 * n:2 * n + 4]
        x, y, c = _place()
        chips = [(1 - x, y), (x, 1 - y), (1 - x, 1 - y)]
        sib = (x, y, 1 - c)
        for a in range(n):
            mine = land[a].at[_slot(x, y, c)]
            for k in range(4):
                pltpu.make_async_remote_copy(
                    src_ref=ins[a], dst_ref=mine, send_sem=send1.at[4 * a + k], recv_sem=recv1.at[4 * a + k],
                    device_id=sib, device_id_type=MESH).wait_send()
            theirs = land[a].at[_slot(x, y, 1 - c)]
            pltpu.make_async_remote_copy(
                src_ref=ins[a], dst_ref=theirs, send_sem=send1.at[4 * a], recv_sem=recv1.at[4 * a],
                device_id=sib, device_id_type=MESH).wait_recv()
            for j, (px, py) in enumerate(chips):
                sent = land[a].at[_slot(px, py, c)]
                got = land[a].at[_slot(px, py, 1 - c)]
                pltpu.make_async_remote_copy(
                    src_ref=sent, dst_ref=sent, send_sem=send2.at[3 * a + j], recv_sem=recv2.at[3 * a + j],
                    device_id=sib, device_id_type=MESH).wait_send()
                pltpu.make_async_remote_copy(
                    src_ref=sent, dst_ref=got, send_sem=send2.at[3 * a + j], recv_sem=recv2.at[3 * a + j],
                    device_id=sib, device_id_type=MESH).wait_recv()

    out = pl.pallas_call(
        body, name=name,
        in_specs=[HBM] * (2 * n) + [SEM] * 4 + [ANY],
        out_specs=[HBM] * n,
        out_shape=_hbm_like(h["bufs"]),
        input_output_aliases={n + i: i for i in range(n)},
        compiler_params=pltpu.CompilerParams(has_side_effects=EFFECT),
    )(*h["shards"], *h["bufs"], h["send1"], h["recv1"], h["send2"], h["recv2"], after)
    return list(out)


def _peer_plan(kind, x, y, c):
    if kind == "pair":
        return [(2 * q + (1 - c), q, (x, y, 1 - c)) for q in range(4)]
    chips = [(1 - x, y), (x, 1 - y), (1 - x, 1 - y)]
    return [(2 * px + py, k, (px, py, c)) for k, (px, py) in enumerate(chips)]


def _exchange_start(kind, srcs, after, name):
    n = len(srcs)
    K = 4 if kind == "pair" else 3
    lands = [_hbm(lax.empty((K,) + s.shape[1:], s.dtype)) for s in srcs]

    def body(*refs):
        ins, land = refs[:n], refs[n:2 * n]
        send, recv = refs[2 * n + 1], refs[2 * n + 2]
        token = refs[-1]
        for a in range(n):
            for k, (si, di, dev) in enumerate(_peer_plan(kind, *_place())):
                pltpu.make_async_remote_copy(
                    src_ref=ins[a].at[si], dst_ref=land[a].at[di], send_sem=send.at[K * a + k], recv_sem=recv.at[K * a + k],
                    device_id=dev, device_id_type=MESH).start()
        token[...] = jnp.zeros_like(token)

    out = pl.pallas_call(
        body, name=name,
        in_specs=[HBM] * (2 * n) + [ANY],
        out_specs=[SEM, SEM] + [HBM] * (2 * n) + [VMEM_SPEC],
        out_shape=[pltpu.SemaphoreType.DMA((K * n,)), pltpu.SemaphoreType.DMA((K * n,))] + _hbm_like(srcs) + _hbm_like(lands) + [TOKEN],
        input_output_aliases={i: 2 + i for i in range(2 * n)},
        compiler_params=pltpu.CompilerParams(has_side_effects=EFFECT),
    )(*[_hbm(s) for s in srcs], *lands, after)
    return dict(kind=kind, send=out[0], recv=out[1], srcs=list(out[2:2 + n]), lands=list(out[2 + n:2 + 2 * n]), token=out[-1])


def _exchange_wait(h, after, name):
    n = len(h["srcs"])
    kind = h["kind"]
    K = 4 if kind == "pair" else 3

    def body(*refs):
        ins, land = refs[:n], refs[n:2 * n]
        send, recv = refs[2 * n], refs[2 * n + 1]
        for a in range(n):
            for k, (si, di, dev) in enumerate(_peer_plan(kind, *_place())):
                cp = pltpu.make_async_remote_copy(
                    src_ref=ins[a].at[si], dst_ref=land[a].at[di], send_sem=send.at[K * a + k], recv_sem=recv.at[K * a + k],
                    device_id=dev, device_id_type=MESH)
                cp.wait_send()
                cp.wait_recv()

    out = pl.pallas_call(
        body, name=name,
        in_specs=[HBM] * (2 * n) + [SEM, SEM, ANY],
        out_specs=[HBM] * (2 * n),
        out_shape=_hbm_like(h["srcs"]) + _hbm_like(h["lands"]),
        input_output_aliases={i: i for i in range(2 * n)},
        compiler_params=pltpu.CompilerParams(has_side_effects=EFFECT),
    )(*h["srcs"], *h["lands"], h["send"], h["recv"], after)
    return list(out[:n]), list(out[n:])


def _pair_add(g, got, c_idx, name):
    _, R, C = g.shape
    tr, tc = _tile2(R, C, 512, 16)

    def body(c_ref, a_ref, b_ref, o_ref):
        o_ref[0] = (a_ref[0].astype(f32) + b_ref[0].astype(f32)).astype(o_ref.dtype)

    return pl.pallas_call(
        body, name=name,
        grid_spec=pltpu.PrefetchScalarGridSpec(
            num_scalar_prefetch=1, grid=(4, R // tr, C // tc),
            in_specs=[pl.BlockSpec((1, tr, tc), lambda q, i, j, c: (2 * q + c[0], i, j)),
                      pl.BlockSpec((1, tr, tc), lambda q, i, j, c: (q, i, j))],
            out_specs=pl.BlockSpec((1, tr, tc), lambda q, i, j, c: (q, i, j))),
        out_shape=jax.ShapeDtypeStruct((4, R, C), g.dtype),
        compiler_params=_cparams(("parallel", "parallel", "parallel")),
    )(c_idx, g, got)


def _small_exchange(send, gather, name, after=None):
    R = send.shape[-2]

    def body(in_ref, out_ref, send_sems, recv_sems):
        x, y, c = _place()
        me = 4 * x + 2 * y + c
        out_ref[me] = in_ref[...] if gather else in_ref[me]
        cps = []
        for k in range(1, N_DEV):
            px, py, pc = x ^ ((k >> 2) & 1), y ^ ((k >> 1) & 1), c ^ (k & 1)
            src = in_ref if gather else in_ref.at[4 * px + 2 * py + pc]
            cps.append(pltpu.make_async_remote_copy(
                src_ref=src, dst_ref=out_ref.at[me],
                send_sem=send_sems.at[k - 1], recv_sem=recv_sems.at[k - 1],
                device_id=(px, py, pc), device_id_type=MESH))
        for cp in cps:
            cp.start()
        for cp in cps:
            cp.wait()

    body, in_specs, args = _ordered(body, [pl.BlockSpec(memory_space=pltpu.VMEM)], [send], after)
    return pl.pallas_call(
        body, name=name,
        in_specs=in_specs, out_specs=pl.BlockSpec(memory_space=pltpu.VMEM),
        out_shape=jax.ShapeDtypeStruct((N_DEV, R, LANES), f32),
        scratch_shapes=[pltpu.SemaphoreType.DMA((N_DEV - 1,)), pltpu.SemaphoreType.DMA((N_DEV - 1,))],
    )(*args)


def _sum_blocks(blocks, name):
    def body(in_ref, o_ref):
        s = in_ref[0]
        for d in range(1, N_DEV):
            s = s + in_ref[d]
        o_ref[0] = s

    return pl.pallas_call(body, name=name, out_shape=jax.ShapeDtypeStruct((1,) + blocks.shape[1:], f32))(blocks)


def _rows(n):
    return -(-n // LANES)


def _pack(arrs, total_rows):
    parts = []
    for a in arrs:
        flat = a.reshape(-1).astype(f32)
        parts.append(jnp.pad(flat, (0, _rows(flat.size) * LANES - flat.size)))
    flat = jnp.concatenate(parts)
    return jnp.pad(flat, (0, total_rows * LANES - flat.size)).reshape(total_rows, LANES)


def _unpack(packed, shapes):
    lead = packed.shape[:-2]
    flat = packed.reshape(lead + (-1,))
    out, pos = [], 0
    for s in shapes:
        n = 1
        for d in s:
            n *= d
        out.append(flat[..., pos:pos + n].reshape(lead + tuple(s)))
        pos += _rows(n) * LANES
    return out


def _to_shards(full, axis):
    s = full.shape
    return jnp.moveaxis(full.reshape(s[:axis] + (N_DEV, s[axis] // N_DEV) + s[axis + 1:]), axis, 0)


def _from_shards(sh, axis):
    m = jnp.moveaxis(sh, 0, axis)
    s = m.shape
    return m.reshape(s[:axis] + (s[axis] * s[axis + 1],) + s[axis + 2:])


def kernel(x, meta, ffn_norm, ffn_w_gate, ffn_w_up, ffn_w_down, gla_norm, gla_w_in, gla_w_lr, gla_b_lr, gla_head_norm, gla_w_out, pool_norm, pool_w, pool_b, pool_scale, final_norm, loss_target, m_meta, m_ffn_norm, m_ffn_w_gate, m_ffn_w_up, m_ffn_w_down, m_gla_norm, m_gla_w_in, m_gla_w_lr, m_gla_b_lr, m_gla_head_norm, m_gla_w_out, m_pool_norm, m_pool_w, m_pool_b, m_pool_scale, m_final_norm, v_meta, v_ffn_norm, v_ffn_w_gate, v_ffn_w_up, v_ffn_w_down, v_gla_norm, v_gla_w_in, v_gla_w_lr, v_gla_b_lr, v_gla_head_norm, v_gla_w_out, v_pool_norm, v_pool_w, v_pool_b, v_pool_scale, v_final_norm):
    H = GLA_HEADS
    _, SEQ, D = x.shape
    Fs = ffn_w_gate.shape[-1]
    DK, DV = D // 2, D
    hv = DV // H
    GW = D // POOL_GROUPS
    INW = 2 * DK + 2 * DV + GATE_RANK
    NPK = 2 * DK + 2 * DV + GATE_PAD
    pad = (-N_META) % GLA_CHUNK
    first = pad + N_META
    Lp = first + SEQ
    n_units = ffn_w_gate.shape[0] * ffn_w_gate.shape[1]
    assert first % GLA_CHUNK == 0 and Lp % GLA_CHUNK == 0 and pad >= POOL_GROUPS * 4

    px, py, pc = _place()
    c_idx = jnp.reshape(pc, (1,)).astype(jnp.int32)
    q_idx = jnp.reshape(2 * px + py, (1,)).astype(jnp.int32)
    zero_idx = jnp.zeros((1,), jnp.int32)

    small_sh = [meta, ffn_norm, gla_w_lr, pool_norm, pool_b, pool_scale]
    small_axis = [1, 2, 2, 1, 2, 1]
    sh_shapes = [a.shape for a in small_sh]
    sh_rows = -(-sum(_rows(a.size) for a in small_sh) // 8) * 8
    gathered = _small_exchange(_pack(small_sh, sh_rows), True, "small_gather")
    meta_f, ffn_norm_f, wlr_f, pool_norm_f, pool_b_f, pool_scale_f = [
        _from_shards(a, ax) for a, ax in zip(_unpack(gathered, sh_shapes), small_axis)]
    ffn_norm_f = ffn_norm_f.reshape(n_units, 1, D)
    wlr128 = jnp.pad(wlr_f[0], ((0, GATE_PAD - GATE_RANK), (0, 0)))

    def t_units(w):
        return jnp.swapaxes(w, -1, -2).reshape(n_units, Fs, D)

    wg_l = t_units(ffn_w_gate).astype(bf16)
    wu_l = t_units(ffn_w_up).astype(bf16)
    wd_l = ffn_w_down.reshape(n_units, Fs, D).astype(bf16)
    ffn_shards = [[wg_l[u], wu_l[u], wd_l[u]] for u in range(n_units)]
    mixer_shards = [gla_w_in[0].T.astype(bf16), gla_w_out[0].astype(bf16), pool_w[0].astype(bf16)]
    gather_order = [("ffn0", ffn_shards[0]), ("mixers", mixer_shards)] + [(f"ffn{u}", ffn_shards[u]) for u in range(1, n_units)]
    c_lr = 2 * DK + DV
    c_r = 2 * DK + 2 * DV
    gate_blk = c_r // GATE_PAD

    def gather_begin(i, after):
        tag, shards = gather_order[i]
        return _gather_start(shards, after, f"gather_start_{tag}")

    def gather_next(i, h, after):
        tag = gather_order[i][0]
        h = _gather_mid(h, after, f"gather_mid_{tag}")
        nxt = gather_begin(i + 1, h["token"]) if i + 1 < len(gather_order) else None
        done = _gather_end(h, h["token"] if nxt is None else nxt["token"], f"gather_end_{tag}")
        return done, nxt

    xs = jnp.concatenate([jnp.zeros((pad, D), f32), meta_f, x[0]], axis=0)
    saved = {}
    ffn_w = [None] * n_units

    def ffn_f(u, xs):
        out, h, G, U = _ffn_fwd(xs, ffn_norm_f[u], *ffn_w[u], name=f"ffn_fwd{u}")
        saved[("ffn", u)] = (xs, h, G, U)
        return out

    def gla_f(xs, win_p, wout_full):
        hn = _rms_fwd(xs, gla_norm, bf16, "gla_norm_fwd")
        proj = _mm(hn, win_p, "nt", f32, "gla_proj", tm=1056, tn=896, tk=2048)
        lg = _gate_fwd(proj, wlr128, gla_b_lr, pad, gate_blk, "gla_gate_fwd")
        o, y, states = _gla_fwd(proj, lg, gla_head_norm, H, "gla_core_fwd")
        out = _mm(y, wout_full, "nn", f32, "gla_out", tm=1056, tn=512, tk=2048, residual=xs)
        saved["gla"] = (xs, hn, proj, lg, o, y, states)
        return out

    def pool_f(xs, wpool_full):
        hn = _rms_fwd(xs, pool_norm_f, f32, "pool_norm_fwd")
        pooled = _pool_windows(hn, pad, Lp - pad, "pool_windows_fwd")
        out = _pool_mix_fwd(xs, pooled, wpool_full, pool_b_f.reshape(1, D), pool_scale_f, pad, "pool_mix_fwd")
        saved["pool"] = (xs, pooled)
        return out

    depth = ffn_w_gate.shape[0]
    assert depth == 2 and n_units == 4
    h = gather_begin(0, gathered)
    ffn_w[0], h = gather_next(0, h, h["token"])
    xs = ffn_f(0, xs)
    (win_g, wout_g, wpool_g), h = gather_next(1, h, xs)
    win_full = win_g.reshape(INW, D)
    win_p = jnp.concatenate([win_full[:c_lr], win_full[c_lr + GATE_RANK:], win_full[c_lr:c_lr + GATE_RANK],
                             jnp.zeros((GATE_PAD - GATE_RANK, D), bf16)], axis=0)
    wout_full = wout_g.reshape(DV, D)
    wpool_full = _from_shards(wpool_g, 1)
    xs = gla_f(xs, win_p, wout_full)
    ffn_w[1], h = gather_next(2, h, xs)
    xs = ffn_f(1, xs)
    ffn_w[2], h = gather_next(3, h, xs)
    xs = ffn_f(2, xs)
    xs = pool_f(xs, wpool_full)
    ffn_w[3], h = gather_next(4, h, xs)
    xs = ffn_f(3, xs)
    loss_part, dxs, d_final, dyh = _loss_head(xs, loss_target[0], final_norm.reshape(1, D), first, "loss_head")

    class Reduce:
        def __init__(self, tag, grads, after=None):
            self.tag = tag
            self.h = _exchange_start("pair", grads, grads[0] if after is None else after, f"pair_start_{tag}")
            self.token = self.h["token"]

        def mid(self, after):
            grads, got = _exchange_wait(self.h, after, f"pair_wait_{self.tag}")
            self.sums = [_pair_add(g, r, c_idx, f"pair_add_{self.tag}{a}") for a, (g, r) in enumerate(zip(grads, got))]
            self.h = _exchange_start("chips", self.sums, self.sums[-1], f"chips_start_{self.tag}")
            self.token = self.h["token"]

        def end(self, after):
            sums, recv = _exchange_wait(self.h, after, f"chips_wait_{self.tag}")
            return list(zip(sums, recv))

    d_ffn_norm = [None] * n_units
    small_grads = {}

    def ffn_b(u, dY, dyh, prev):
        xs_in, h_, G, U = saved[("ffn", u)]
        wg, wu, wd = ffn_w[u]
        tok = None if prev is None else prev.token
        dG, dU, A = _ffn_bwd_act(dyh, wd, G, U, f"ffn_act{u}", after=tok)
        dh = _ffn_bwd_dh(dG, dU, wg, wu, f"ffn_dh{u}")
        dxs, dg, dyh_next = _rms_bwd(dY, dh, xs_in, ffn_norm_f[u], pad, f"ffn_norm_bwd{u}")
        if prev is not None:
            prev.mid(dxs)
            tok = prev.token
        dwg = _ffn_bwd_wgrad(dG, h_, f"ffn_wgrad_gate{u}", after=tok)
        dwu = _ffn_bwd_wgrad(dU, h_, f"ffn_wgrad_up{u}", after=tok)
        dwd = _ffn_bwd_wgrad(A, dyh, f"ffn_wgrad_down{u}", after=tok)
        d_ffn_norm[u] = dg
        return dxs, dyh_next, Reduce(f"ffn{u}", [dwg, dwu, dwd])

    def gla_b(dY, prev):
        xs_in, hn, proj, lg, o, y, states = saved["gla"]
        dyb = dY.astype(bf16)
        dy = _mm(dyb, wout_full, "nt", f32, "gla_out_dgrad", tm=1056, tn=512, tk=2048, after=prev.token)
        dwout = _mm(y, dyb, "tn", bf16, "gla_out_wgrad", tm=1024, tn=1024, tk=528, after=prev.token)
        prev.mid(dwout)
        dq, dk, dv, dr, dlg, dhw = _gla_bwd(dy, proj, lg, o, states, gla_head_norm, H, pad, "gla_core_bwd", after=prev.token)
        dlr, dwlr, dblr = _gate_bwd(dlg, proj, wlr128, gla_b_lr, pad, gate_blk, "gla_gate_bwd")
        dproj = jnp.concatenate([dq, dk, dv, dr, dlr], axis=1)
        dwin_p = _mm(dproj, hn, "tn", bf16, "gla_proj_wgrad", tm=896, tn=1024, tk=528)
        dhn = _mm(dproj, win_p, "nn", f32, "gla_proj_dgrad", tm=1056, tn=512, tk=896)
        dxs, dgn, dyh_next = _rms_bwd(dY, dhn, xs_in, gla_norm, pad, "gla_norm_bwd")
        dwin = jnp.concatenate([dwin_p[:c_lr], dwin_p[c_r:c_r + GATE_RANK], dwin_p[c_lr:c_r]], axis=0)
        small_grads.update(gla_w_lr=dwlr[:GATE_RANK][None], gla_b_lr=dblr, gla_head_norm=dhw, gla_norm=dgn)
        return dxs, dyh_next, Reduce("gla", [dwin.reshape(N_DEV, INW // N_DEV, D), dwout.reshape(N_DEV, DV // N_DEV, D)])

    def pool_b_(dY, prev):
        xs_in, pooled = saved["pool"]
        dp, dw, db, ds = _pool_mix_bwd(dY, pooled, wpool_full, pool_b_f.reshape(1, D), pool_scale_f, pad, "pool_mix_bwd",
                                       after=prev.token)
        dhn = _pool_windows_bwd(dp, pad, "pool_windows_bwd")
        dxs, dgn, dyh_next = _rms_bwd(dY, dhn, xs_in, pool_norm_f, pad, "pool_norm_bwd")
        prev.mid(dxs)
        dws = _to_shards(dw, 1)
        small_grads.update(pool_b=db.reshape(1, POOL_GROUPS, GW), pool_scale=ds, pool_norm=dgn)
        return dxs, dyh_next, Reduce("pool", [dws.reshape(N_DEV, POOL_GROUPS * GW // N_DEV, GW)], after=prev.token)

    dxs, dyh, r3 = ffn_b(3, dxs, dyh, None)
    dxs, dyh, rp = pool_b_(dxs, r3)
    dxs, dyh, r2 = ffn_b(2, dxs, dyh, rp)
    dxs, dyh, r1 = ffn_b(1, dxs, dyh, r2)
    dxs, dyh, rg = gla_b(dxs, r1)
    dxs, dyh, r0 = ffn_b(0, dxs, dyh, rg)
    grad_x = dxs[first:].reshape(x.shape)
    small_grads.update(meta=dxs[pad:first], ffn_norm=jnp.concatenate(d_ffn_norm, axis=0).reshape(ffn_norm_f.shape[0] // 2, 2, D),
                       final_norm=d_final.reshape(D))

    big_res = {}

    def adam_one(nm, w, m, v, entry, transposed=False):
        sums, recv = entry
        R, C = sums.shape[1:]
        w1, m1, v1 = ((t[0].T if transposed else t).reshape(1, R, C) for t in (w, m, v))
        out = _adamw(w1, m1, v1, 0, sums, q_idx, recv, None, f"adamw_{nm}", after=r0.token)
        for kind, val in zip(("grad", "delta", "new_m", "new_v"), out):
            big_res[(kind, nm)] = val[0].T[None] if transposed else val.reshape(w.shape)
        return out[0]

    e_gla = rg.end(dxs)
    done = adam_one("gla_w_in", gla_w_in, m_gla_w_in, v_gla_w_in, e_gla[0], transposed=True)
    done = adam_one("gla_w_out", gla_w_out, m_gla_w_out, v_gla_w_out, e_gla[1])
    done = adam_one("pool_w", pool_w, m_pool_w, v_pool_w, rp.end(done)[0])
    r0.mid(done)

    sh_names = ["meta", "ffn_norm", "gla_w_lr", "pool_norm", "pool_b", "pool_scale"]
    rep_names = ["gla_norm", "gla_b_lr", "gla_head_norm", "final_norm"]
    rep_w = [gla_norm, gla_b_lr, gla_head_norm, final_norm]
    rep_shapes = [a.shape for a in rep_w]
    rep_rows = -(-sum(_rows(a.size) for a in rep_w) // 8) * 8
    by_owner = [_to_shards(small_grads[nm].reshape(full_shape), ax) for nm, full_shape, ax in zip(
        sh_names, [meta_f.shape, (ffn_norm.shape[0], 2, D), wlr_f.shape, pool_norm_f.shape, pool_b_f.shape, pool_scale_f.shape],
        small_axis)]
    rep_pack = _pack([small_grads[nm].reshape(s) for nm, s in zip(rep_names, rep_shapes)], rep_rows)
    send = jnp.stack([
        jnp.concatenate([_pack([g[d] for g in by_owner], sh_rows), rep_pack, loss_part], axis=0) for d in range(N_DEV)])
    total = _sum_blocks(_small_exchange(send, False, "small_reduce", after=r0.token), "small_sum")
    loss = total[0, sh_rows + rep_rows, 0]
    n_small = sh_rows + rep_rows
    g_small = total[:, :n_small]

    def pack_small(sh_list, rep_list):
        return jnp.concatenate([_pack(sh_list, sh_rows), _pack(rep_list, rep_rows)], axis=0)[None]

    w_small = pack_small(small_sh, rep_w)
    m_small = pack_small([m_meta, m_ffn_norm, m_gla_w_lr, m_pool_norm, m_pool_b, m_pool_scale],
                         [m_gla_norm, m_gla_b_lr, m_gla_head_norm, m_final_norm])
    v_small = pack_small([v_meta, v_ffn_norm, v_gla_w_lr, v_pool_norm, v_pool_b, v_pool_scale],
                         [v_gla_norm, v_gla_b_lr, v_gla_head_norm, v_final_norm])
    small_out = _adamw(w_small, m_small, v_small, 0, g_small, zero_idx, None, None, "adamw_small")
    small_res = {}
    for kind, packed in zip(("grad", "delta", "new_m", "new_v"), small_out):
        sh_vals = _unpack(packed[0, :sh_rows], sh_shapes)
        rep_vals = _unpack(packed[0, sh_rows:], rep_shapes)
        for nm, val in zip(sh_names + rep_names, sh_vals + rep_vals):
            small_res[(kind, nm)] = val

    done = small_out[0]

    ffn_names = ["ffn_w_gate", "ffn_w_up", "ffn_w_down"]
    ffn_wmv = [tuple(t_units(t) for t in (ffn_w_gate, m_ffn_w_gate, v_ffn_w_gate)),
               tuple(t_units(t) for t in (ffn_w_up, m_ffn_w_up, v_ffn_w_up)),
               tuple(t.reshape(n_units, Fs, D) for t in (ffn_w_down, m_ffn_w_down, v_ffn_w_down))]
    ffn_prev = [[lax.empty((n_units, Fs, D), f32) for _ in range(4)] for _ in range(3)]
    order_after = r0.token
    for u, red in ((3, r3), (2, r2), (1, r1), (0, r0)):
        entries = red.end(done)
        for a in range(3):
            sums, recv = entries[a]
            ffn_prev[a] = _adamw(*ffn_wmv[a], u, sums, q_idx, recv, ffn_prev[a], f"adamw_{ffn_names[a]}{u}", after=order_after)
            done = order_after = ffn_prev[a][0]
    for a in range(3):
        for kind, val in zip(("grad", "delta", "new_m", "new_v"), ffn_prev[a]):
            val = val.reshape(ffn_w_down.shape)
            big_res[(kind, ffn_names[a])] = val if a == 2 else jnp.swapaxes(val, -1, -2)

    order = ["meta", "ffn_norm", "ffn_w_gate", "ffn_w_up", "ffn_w_down", "gla_norm", "gla_w_in", "gla_w_lr", "gla_b_lr",
             "gla_head_norm", "gla_w_out", "pool_norm", "pool_w", "pool_b", "pool_scale", "final_norm"]
    res = {**small_res, **big_res}
    outs = [loss, grad_x]
    for kind in ("grad", "delta", "new_m", "new_v"):
        outs += [res[(kind, nm)] for nm in order]
    return tuple(outs)
```

```python
import functools

import jax
import jax.numpy as jnp
from jax import lax
from jax.experimental import pallas as pl
from jax.experimental.pallas import tpu as pltpu

f32 = jnp.float32
bf16 = jnp.bfloat16

N_DEV = 8
N_META = 16
GLA_HEADS = 4
GLA_CHUNK = 64
GLA_SUB = 16
GATE_RANK = 16
GATE_PAD = 128
GATE_NORM = 16.0
EPS = 1e-6
POOL_GROUPS = 4
ADAM_LR = 0.001
ADAM_B1 = 0.9
ADAM_B2 = 0.999
ADAM_EPS = 1e-08
ADAM_WD = 0.01
ADAM_STEP = 10
LANES = 128
VMEM_LIMIT_MB = 56

NN = (((1,), (0,)), ((), ()))
NT = (((1,), (1,)), ((), ()))
TN = (((0,), (0,)), ((), ()))
HI = lax.Precision.HIGHEST
MESH = pl.DeviceIdType.MESH
ANY = pl.BlockSpec(memory_space=pl.ANY)


def _cparams(sem=None, vmem_mb=None):
    kw = {}
    if sem is not None:
        kw["dimension_semantics"] = sem
    if vmem_mb is not None:
        kw["vmem_limit_bytes"] = vmem_mb * 2 ** 20
    return pltpu.CompilerParams(**kw)


def _tile(n, target, mult=16):
    best = None
    for t in range(mult, min(n, target) + 1, mult):
        if n % t == 0:
            best = t
    assert best is not None, (n, target, mult)
    return best


def _tile2(R, C, rows, mult):
    if R % mult == 0:
        return _tile(R, rows, mult), C
    return R, _tile(C, 256, LANES)


def _dot(a, b, dims=NN, precision=None):
    return lax.dot_general(a, b, dims, preferred_element_type=f32, precision=precision)


def _sigmoid(x):
    return 1.0 / (1.0 + jnp.exp(-x))


def _row_ids(tile_index, tm):
    return tile_index * tm + lax.broadcasted_iota(jnp.int32, (tm, 1), 0)


def _ordered(body, in_specs, args, after, lead=0):
    if after is None:
        return body, in_specs, args
    pos = lead + len(args)

    def body_without(*refs):
        return body(*refs[:pos], *refs[pos + 1:])

    return body_without, list(in_specs) + [ANY], list(args) + [after]


def _rms_fwd(xs, g, out_dtype, name):
    Lp, D = xs.shape
    tm = _tile(Lp, 528)

    def body(x_ref, g_ref, h_ref):
        x = x_ref[...]
        rstd = lax.rsqrt(jnp.mean(x * x, axis=-1, keepdims=True) + EPS)
        h_ref[...] = (x * rstd * g_ref[...]).astype(out_dtype)

    return pl.pallas_call(
        body, name=name, grid=(Lp // tm,),
        in_specs=[pl.BlockSpec((tm, D), lambda i: (i, 0)), pl.BlockSpec((1, D), lambda i: (0, 0))],
        out_specs=pl.BlockSpec((tm, D), lambda i: (i, 0)),
        out_shape=jax.ShapeDtypeStruct((Lp, D), out_dtype),
        compiler_params=_cparams(("parallel",)),
    )(xs, g)


def _rms_bwd(dY, dh, xs, g, pad, name):
    Lp, D = xs.shape
    tm = _tile(Lp, 352)

    def body(dY_ref, dh_ref, x_ref, g_ref, dxs_ref, dg_ref, half_ref):
        i = pl.program_id(0)

        @pl.when(i == 0)
        def _():
            dg_ref[...] = jnp.zeros_like(dg_ref)

        x = x_ref[...]
        rstd = lax.rsqrt(jnp.mean(x * x, axis=-1, keepdims=True) + EPS)
        xhat = x * rstd
        dh_ = dh_ref[...]
        dg_ref[...] += jnp.sum(dh_ * xhat, axis=0, keepdims=True)
        dxh = dh_ * g_ref[...]
        dx = rstd * (dxh - xhat * jnp.mean(dxh * xhat, axis=-1, keepdims=True))
        out = jnp.where(_row_ids(i, tm) >= pad, dY_ref[...] + dx, 0.0)
        dxs_ref[...] = out
        half_ref[...] = (0.5 * out).astype(bf16)

    row = pl.BlockSpec((tm, D), lambda i: (i, 0))
    vec = pl.BlockSpec((1, D), lambda i: (0, 0))
    return pl.pallas_call(
        body, name=name, grid=(Lp // tm,),
        in_specs=[row, row, row, vec], out_specs=[row, vec, row],
        out_shape=[jax.ShapeDtypeStruct((Lp, D), f32), jax.ShapeDtypeStruct((1, D), f32), jax.ShapeDtypeStruct((Lp, D), bf16)],
        compiler_params=_cparams(("arbitrary",)),
    )(dY, dh, xs, g)


def _mm(a, b, mode, out_dtype, name, tm=512, tn=512, tk=512, residual=None, after=None):
    if mode == "nn":
        (M, K), N = a.shape, b.shape[1]
    elif mode == "nt":
        (M, K), N = a.shape, b.shape[0]
    else:
        (K, M), N = a.shape, b.shape[1]
    tm = _tile(M, tm, 16 if mode != "tn" else LANES) if M > tm else M
    tn = _tile(N, tn, LANES) if N > tn else N
    tk = _tile(K, tk, LANES if mode != "tn" else 16) if K > tk else K
    nk = K // tk
    dims = {"nn": NN, "nt": NT, "tn": TN}[mode]

    def body(*refs):
        if residual is None:
            a_ref, b_ref, o_ref, acc = refs
            r_ref = None
        else:
            a_ref, b_ref, r_ref, o_ref, acc = refs
        k = pl.program_id(2)

        @pl.when(k == 0)
        def _():
            acc[...] = jnp.zeros_like(acc)

        acc[...] += _dot(a_ref[...], b_ref[...], dims)

        @pl.when(k == nk - 1)
        def _():
            r = acc[...]
            if r_ref is not None:
                r = r + r_ref[...]
            o_ref[...] = r.astype(out_dtype)

    a_spec = pl.BlockSpec((tk, tm), lambda i, j, k: (k, i)) if mode == "tn" else pl.BlockSpec((tm, tk), lambda i, j, k: (i, k))
    b_spec = pl.BlockSpec((tn, tk), lambda i, j, k: (j, k)) if mode == "nt" else pl.BlockSpec((tk, tn), lambda i, j, k: (k, j))
    o_spec = pl.BlockSpec((tm, tn), lambda i, j, k: (i, j))
    in_specs = [a_spec, b_spec] + ([o_spec] if residual is not None else [])
    args = [a, b] + ([residual] if residual is not None else [])
    body, in_specs, args = _ordered(body, in_specs, args, after)
    return pl.pallas_call(
        body, name=name, grid=(M // tm, N // tn, nk),
        in_specs=in_specs, out_specs=o_spec,
        out_shape=jax.ShapeDtypeStruct((M, N), out_dtype),
        scratch_shapes=[pltpu.VMEM((tm, tn), f32)],
        compiler_params=_cparams(("parallel", "parallel", "arbitrary"), VMEM_LIMIT_MB),
    )(*args)


def _ffn_fwd(xs, g, wg, wu, wd, name):
    Lp, D = xs.shape
    nd, Fs, _ = wg.shape
    tm = _tile(Lp, 704)
    once = pl.Buffered(1)

    def body(x_ref, g_ref, wg_ref, wu_ref, wd_ref, out_ref, h_ref, G_ref, U_ref, hs, acc):
        j = pl.program_id(1)

        @pl.when(j == 0)
        def _():
            x = x_ref[...]
            rstd = lax.rsqrt(jnp.mean(x * x, axis=-1, keepdims=True) + EPS)
            h = (x * rstd * g_ref[...]).astype(bf16)
            hs[...] = h
            h_ref[...] = h
            acc[...] = jnp.zeros_like(acc)

        h = hs[...]
        G = _dot(h, wg_ref[0], NT)
        U = _dot(h, wu_ref[0], NT)
        G_ref[0] = G.astype(bf16)
        U_ref[0] = U.astype(bf16)
        A = (G * _sigmoid(G) * U).astype(bf16)
        acc[...] += _dot(A, wd_ref[0])

        @pl.when(j == nd - 1)
        def _():
            out_ref[...] = x_ref[...] + 0.5 * acc[...]

    row_f = pl.BlockSpec((tm, D), lambda i, j: (i, 0), pipeline_mode=once)
    act = pl.BlockSpec((1, tm, Fs), lambda i, j: (j, i, 0))
    return pl.pallas_call(
        body, name=name, grid=(Lp // tm, nd),
        in_specs=[row_f, pl.BlockSpec((1, D), lambda i, j: (0, 0)),
                  pl.BlockSpec((1, Fs, D), lambda i, j: (j, 0, 0)),
                  pl.BlockSpec((1, Fs, D), lambda i, j: (j, 0, 0)),
                  pl.BlockSpec((1, Fs, D), lambda i, j: (j, 0, 0))],
        out_specs=[row_f, pl.BlockSpec((tm, D), lambda i, j: (i, 0), pipeline_mode=once), act, act],
        out_shape=[jax.ShapeDtypeStruct((Lp, D), f32), jax.ShapeDtypeStruct((Lp, D), bf16),
                   jax.ShapeDtypeStruct((nd, Lp, Fs), bf16), jax.ShapeDtypeStruct((nd, Lp, Fs), bf16)],
        scratch_shapes=[pltpu.VMEM((tm, D), bf16), pltpu.VMEM((tm, D), f32)],
        compiler_params=_cparams(("parallel", "arbitrary"), VMEM_LIMIT_MB),
    )(xs, g, wg, wu, wd)


def _ffn_bwd_act(dyh, wd, G, U, name, after=None):
    Lp, D = dyh.shape
    nd, Fs, _ = wd.shape
    tm = _tile(Lp, 704)

    def body(dyh_ref, wd_ref, G_ref, U_ref, dG_ref, dU_ref, A_ref):
        dA = _dot(dyh_ref[...], wd_ref[0], NT)
        Gf = G_ref[0].astype(f32)
        Uf = U_ref[0].astype(f32)
        s = _sigmoid(Gf)
        silu = Gf * s
        dG_ref[0] = (dA * Uf * (s * (1.0 + Gf * (1.0 - s)))).astype(bf16)
        dU_ref[0] = (dA * silu).astype(bf16)
        A_ref[0] = (silu * Uf).astype(bf16)

    act = pl.BlockSpec((1, tm, Fs), lambda j, i: (j, i, 0))
    act_s = jax.ShapeDtypeStruct((nd, Lp, Fs), bf16)
    in_specs = [pl.BlockSpec((tm, D), lambda j, i: (i, 0)), pl.BlockSpec((1, Fs, D), lambda j, i: (j, 0, 0)), act, act]
    body, in_specs, args = _ordered(body, in_specs, [dyh, wd, G, U], after)
    return pl.pallas_call(
        body, name=name, grid=(nd, Lp // tm),
        in_specs=in_specs, out_specs=[act, act, act], out_shape=[act_s, act_s, act_s],
        compiler_params=_cparams(("parallel", "parallel"), VMEM_LIMIT_MB),
    )(*args)


def _ffn_bwd_dh(dG, dU, wg, wu, name, after=None):
    nd, Lp, Fs = dG.shape
    D = wg.shape[2]
    tm = _tile(Lp, 1056)

    def body(dG_ref, dU_ref, wg_ref, wu_ref, dh_ref, acc):
        j = pl.program_id(1)

        @pl.when(j == 0)
        def _():
            acc[...] = jnp.zeros_like(acc)

        acc[...] += _dot(dG_ref[0], wg_ref[0]) + _dot(dU_ref[0], wu_ref[0])

        @pl.when(j == nd - 1)
        def _():
            dh_ref[...] = acc[...]

    act = pl.BlockSpec((1, tm, Fs), lambda i, j: (j, i, 0))
    wrow = pl.BlockSpec((1, Fs, D), lambda i, j: (j, 0, 0))
    body, in_specs, args = _ordered(body, [act, act, wrow, wrow], [dG, dU, wg, wu], after)
    return pl.pallas_call(
        body, name=name, grid=(Lp // tm, nd),
        in_specs=in_specs,
        out_specs=pl.BlockSpec((tm, D), lambda i, j: (i, 0), pipeline_mode=pl.Buffered(1)),
        out_shape=jax.ShapeDtypeStruct((Lp, D), f32),
        scratch_shapes=[pltpu.VMEM((tm, D), f32)],
        compiler_params=_cparams(("parallel", "arbitrary"), VMEM_LIMIT_MB),
    )(*args)


def _ffn_bwd_wgrad(act, rows, name, after=None):
    nd, Lp, Fs = act.shape
    D = rows.shape[1]

    def body(a_ref, r_ref, o_ref):
        o_ref[0] = _dot(a_ref[0], r_ref[...], TN).astype(bf16)

    in_specs = [pl.BlockSpec((1, Lp, Fs), lambda j: (j, 0, 0)),
                pl.BlockSpec((Lp, D), lambda j: (0, 0), pipeline_mode=pl.Buffered(1))]
    body, in_specs, args = _ordered(body, in_specs, [act, rows], after)
    return pl.pallas_call(
        body, name=name, grid=(nd,),
        in_specs=in_specs, out_specs=pl.BlockSpec((1, Fs, D), lambda j: (j, 0, 0)),
        out_shape=jax.ShapeDtypeStruct((nd, Fs, D), bf16),
        compiler_params=_cparams(("parallel",), VMEM_LIMIT_MB),
    )(*args)


def _gate_fwd(proj, wlr, blr, pad, gate_blk, name):
    Lp = proj.shape[0]
    DK = wlr.shape[1]
    tm = _tile(Lp, 528)

    def body(lr_ref, w_ref, b_ref, lg_ref):
        z = _dot(lr_ref[...].astype(bf16), w_ref[...].astype(bf16)) + b_ref[...]
        ls = jnp.minimum(z, 0.0) - jnp.log(1.0 + jnp.exp(-jnp.abs(z)))
        lg_ref[...] = jnp.where(_row_ids(pl.program_id(0), tm) >= pad, ls * (1.0 / GATE_NORM), 0.0)

    return pl.pallas_call(
        body, name=name, grid=(Lp // tm,),
        in_specs=[pl.BlockSpec((tm, GATE_PAD), lambda i: (i, gate_blk)),
                  pl.BlockSpec((GATE_PAD, DK), lambda i: (0, 0)), pl.BlockSpec((1, DK), lambda i: (0, 0))],
        out_specs=pl.BlockSpec((tm, DK), lambda i: (i, 0)),
        out_shape=jax.ShapeDtypeStruct((Lp, DK), f32),
        compiler_params=_cparams(("parallel",)),
    )(proj, wlr, blr)


def _gate_bwd(dlg, proj, wlr, blr, pad, gate_blk, name):
    Lp = proj.shape[0]
    DK = wlr.shape[1]
    tm = _tile(Lp, 528)

    def body(dlg_ref, lr_ref, w_ref, b_ref, dlr_ref, dw_ref, db_ref):
        i = pl.program_id(0)

        @pl.when(i == 0)
        def _():
            dw_ref[...] = jnp.zeros_like(dw_ref)
            db_ref[...] = jnp.zeros_like(db_ref)

        lr = lr_ref[...].astype(bf16)
        w = w_ref[...].astype(bf16)
        z = _dot(lr, w) + b_ref[...]
        dz = jnp.where(_row_ids(i, tm) >= pad, dlg_ref[...] * _sigmoid(-z) * (1.0 / GATE_NORM), 0.0)
        dzb = dz.astype(bf16)
        dlr_ref[...] = _dot(dzb, w, NT).astype(bf16)
        dw_ref[...] += _dot(lr, dzb, TN)
        db_ref[...] += jnp.sum(dz, axis=0, keepdims=True)

    return pl.pallas_call(
        body, name=name, grid=(Lp // tm,),
        in_specs=[pl.BlockSpec((tm, DK), lambda i: (i, 0)), pl.BlockSpec((tm, GATE_PAD), lambda i: (i, gate_blk)),
                  pl.BlockSpec((GATE_PAD, DK), lambda i: (0, 0)), pl.BlockSpec((1, DK), lambda i: (0, 0))],
        out_specs=[pl.BlockSpec((tm, GATE_PAD), lambda i: (i, 0)), pl.BlockSpec((GATE_PAD, DK), lambda i: (0, 0)),
                   pl.BlockSpec((1, DK), lambda i: (0, 0))],
        out_shape=[jax.ShapeDtypeStruct((Lp, GATE_PAD), bf16), jax.ShapeDtypeStruct((GATE_PAD, DK), f32),
                   jax.ShapeDtypeStruct((1, DK), f32)],
        compiler_params=_cparams(("arbitrary",)),
    )(dlg, proj, wlr, blr)


def _chunk_decay(lg):
    C = lg.shape[0]
    r = lax.broadcasted_iota(jnp.int32, (C, C), 0)
    c = lax.broadcasted_iota(jnp.int32, (C, C), 1)
    return _dot(jnp.where(r >= c, 1.0, 0.0).astype(f32), lg, NN, HI)


def _col(v):
    return jnp.transpose(jnp.broadcast_to(v, (8, v.shape[1])))[:, 0:1]


def _intra_scores(q, k, b, A_ref):
    C = q.shape[0]
    S = GLA_SUB
    A_ref[...] = jnp.zeros_like(A_ref)
    ri = lax.broadcasted_iota(jnp.int32, (S, 1), 0)
    for I in range(C // S):
        lo = S * I
        qI, bI = q[lo:lo + S], b[lo:lo + S]
        if I > 0:
            bref = b[lo - 1:lo]
            qs = qI * jnp.exp(bI - bref)
            ks = k[:lo] * jnp.exp(bref - b[:lo])
            A_ref[lo:lo + S, 0:lo] = _dot(qs, ks, NT, HI)
        for jj in range(S):
            j = lo + jj
            P = jnp.exp(jnp.minimum(bI - b[j:j + 1], 0.0))
            a = jnp.sum(qI * P * k[j:j + 1], axis=1, keepdims=True)
            A_ref[lo:lo + S, j:j + 1] = jnp.where(ri >= jj, a, 0.0)


def _intra_grads(q, k, b, dA, dq_ref, dk_ref):
    C = q.shape[0]
    S = GLA_SUB
    ri = lax.broadcasted_iota(jnp.int32, (S, 1), 0)
    for I in range(C // S):
        lo = S * I
        qI, bI = q[lo:lo + S], b[lo:lo + S]
        dqI = jnp.zeros_like(qI)
        if I > 0:
            bref = b[lo - 1:lo]
            eq = jnp.exp(bI - bref)
            ek = jnp.exp(bref - b[:lo])
            qs = qI * eq
            ks = k[:lo] * ek
            dAI = dA[lo:lo + S, 0:lo]
            dqI = dqI + _dot(dAI, ks, NN, HI) * eq
            dk_ref[0:lo, :] += _dot(dAI, qs, TN, HI) * ek
        for jj in range(S):
            j = lo + jj
            P = jnp.exp(jnp.minimum(bI - b[j:j + 1], 0.0))
            t = jnp.where(ri >= jj, dA[lo:lo + S, j:j + 1], 0.0) * P
            dqI = dqI + t * k[j:j + 1]
            dk_ref[j:j + 1, :] += jnp.sum(t * qI, axis=0, keepdims=True)
        dq_ref[lo:lo + S, :] += dqI


def _gla_fwd(proj, lg, hnw, H, name):
    Lp = proj.shape[0]
    DK = lg.shape[1]
    hk = DK // H
    hv = hnw.shape[1]
    DV = hv * H
    C = GLA_CHUNK
    NC = Lp // C
    scale = float(hk) ** -0.5
    kq, kv = DK // hk, (2 * DK) // hv
    kr = kv + H

    def body(q_ref, k_ref, v_ref, r_ref, lg_ref, w_ref, o_ref, y_ref, s_ref, S_scr, A_scr):
        c = pl.program_id(1)

        @pl.when(c == 0)
        def _():
            S_scr[...] = jnp.zeros_like(S_scr)

        q = q_ref[...] * scale
        k = k_ref[...]
        v = v_ref[...]
        b = _chunk_decay(lg_ref[...])
        bl = b[C - 1:C]
        S = S_scr[...]
        s_ref[0, 0] = S
        _intra_scores(q, k, b, A_scr)
        vb = v.astype(bf16)
        o = _dot((q * jnp.exp(b)).astype(bf16), S.astype(bf16)) + _dot(A_scr[...].astype(bf16), vb)
        kb = (k * jnp.exp(bl - b)).astype(bf16)
        S_scr[...] = jnp.exp(_col(bl)) * S + _dot(kb, vb, TN)
        o_ref[...] = o
        on = o * lax.rsqrt(jnp.mean(o * o, axis=-1, keepdims=True) + EPS) * w_ref[...]
        r = r_ref[...]
        y_ref[...] = (on * (r * _sigmoid(r))).astype(bf16)

    return pl.pallas_call(
        body, name=name, grid=(H, NC),
        in_specs=[pl.BlockSpec((C, hk), lambda h, c: (c, h)),
                  pl.BlockSpec((C, hk), lambda h, c: (c, kq + h)),
                  pl.BlockSpec((C, hv), lambda h, c: (c, kv + h)),
                  pl.BlockSpec((C, hv), lambda h, c: (c, kr + h)),
                  pl.BlockSpec((C, hk), lambda h, c: (c, h)),
                  pl.BlockSpec((1, hv), lambda h, c: (0, 0))],
        out_specs=[pl.BlockSpec((C, hv), lambda h, c: (c, h)), pl.BlockSpec((C, hv), lambda h, c: (c, h)),
                   pl.BlockSpec((1, 1, hk, hv), lambda h, c: (h, c, 0, 0))],
        out_shape=[jax.ShapeDtypeStruct((Lp, DV), f32), jax.ShapeDtypeStruct((Lp, DV), bf16),
                   jax.ShapeDtypeStruct((H, NC, hk, hv), f32)],
        scratch_shapes=[pltpu.VMEM((hk, hv), f32), pltpu.VMEM((C, C), f32)],
        compiler_params=_cparams(("parallel", "arbitrary")),
    )(proj, proj, proj, proj, lg, hnw)


def _gla_bwd(dy, proj, lg, o, states, hnw, H, pad, name, after=None):
    Lp = proj.shape[0]
    DK = lg.shape[1]
    hk = DK // H
    hv = hnw.shape[1]
    DV = hv * H
    C = GLA_CHUNK
    NC = Lp // C
    scale = float(hk) ** -0.5
    kq, kv = DK // hk, (2 * DK) // hv
    kr = kv + H

    def body(dy_ref, q_ref, k_ref, v_ref, r_ref, lg_ref, o_ref, s_ref, sn_ref, w_ref,
             dq_ref, dk_ref, dv_ref, dr_ref, dlg_ref, dw_ref, dS_scr, A_scr, dq_s, dk_s):
        h = pl.program_id(0)
        cc = pl.program_id(1)
        c = NC - 1 - cc

        @pl.when(cc == 0)
        def _():
            dS_scr[...] = jnp.zeros_like(dS_scr)

        @pl.when((cc == 0) & (h == 0))
        def _():
            dw_ref[...] = jnp.zeros_like(dw_ref)

        keep = (c * C + lax.broadcasted_iota(jnp.int32, (C, 1), 0)) >= pad
        w = w_ref[...]
        o_ = o_ref[...]
        rs = lax.rsqrt(jnp.mean(o_ * o_, axis=-1, keepdims=True) + EPS)
        ohat = o_ * rs
        r = r_ref[...]
        sg = _sigmoid(r)
        dy_ = dy_ref[...]
        d_on = dy_ * (r * sg)
        dr_ref[...] = jnp.where(keep, dy_ * (ohat * w) * (sg * (1.0 + r * (1.0 - sg))), 0.0).astype(bf16)
        dw_ref[...] += jnp.sum(d_on * ohat, axis=0, keepdims=True)
        d_oh = d_on * w
        do = rs * (d_oh - ohat * jnp.mean(d_oh * ohat, axis=-1, keepdims=True))
        dob = do.astype(bf16)
        q = q_ref[...] * scale
        k = k_ref[...]
        v = v_ref[...]
        vb = v.astype(bf16)
        b = _chunk_decay(lg_ref[...])
        bl = b[C - 1:C]
        eb = jnp.exp(b)
        ekb = jnp.exp(bl - b)
        S = s_ref[0, 0]
        dS = dS_scr[...]
        dSb = dS.astype(bf16)
        _intra_scores(q, k, b, A_scr)
        ri = lax.broadcasted_iota(jnp.int32, (C, C), 0)
        ci = lax.broadcasted_iota(jnp.int32, (C, C), 1)
        dA = jnp.where(ri >= ci, _dot(dob, vb, NT), 0.0)
        kb = (k * ekb).astype(bf16)
        qb = (q * eb).astype(bf16)
        dv = _dot(A_scr[...].astype(bf16), dob, TN) + _dot(kb, dSb)
        dq_s[...] = _dot(dob, S.astype(bf16), NT) * eb
        dk_s[...] = _dot(vb, dSb, NT) * ekb
        dS_scr[...] = _dot(qb, dob, TN) + jnp.exp(_col(bl)) * dS
        _intra_grads(q, k, b, dA, dq_s, dk_s)
        dq = dq_s[...]
        dk = dk_s[...]
        Dm = q * dq - k * dk
        after = _dot(jnp.ones((8, hv), f32), sn_ref[0, 0] * dS, NT, HI)[0:1]
        dlg = _dot(jnp.where(ri <= ci, 1.0, 0.0).astype(f32), Dm, NN, HI) + after
        dlg_ref[...] = jnp.where(keep, dlg, 0.0)
        dq_ref[...] = jnp.where(keep, dq * scale, 0.0).astype(bf16)
        dk_ref[...] = jnp.where(keep, dk, 0.0).astype(bf16)
        dv_ref[...] = jnp.where(keep, dv, 0.0).astype(bf16)

    rev = lambda h, cc: NC - 1 - cc
    in_specs = [pl.BlockSpec((C, hv), lambda h, cc: (rev(h, cc), h)),
                pl.BlockSpec((C, hk), lambda h, cc: (rev(h, cc), h)),
                pl.BlockSpec((C, hk), lambda h, cc: (rev(h, cc), kq + h)),
                pl.BlockSpec((C, hv), lambda h, cc: (rev(h, cc), kv + h)),
                pl.BlockSpec((C, hv), lambda h, cc: (rev(h, cc), kr + h)),
                pl.BlockSpec((C, hk), lambda h, cc: (rev(h, cc), h)),
                pl.BlockSpec((C, hv), lambda h, cc: (rev(h, cc), h)),
                pl.BlockSpec((1, 1, hk, hv), lambda h, cc: (h, rev(h, cc), 0, 0)),
                pl.BlockSpec((1, 1, hk, hv), lambda h, cc: (h, jnp.minimum(rev(h, cc) + 1, NC - 1), 0, 0)),
                pl.BlockSpec((1, hv), lambda h, cc: (0, 0))]
    body, in_specs, args = _ordered(body, in_specs, [dy, proj, proj, proj, proj, lg, o, states, states, hnw], after)
    return pl.pallas_call(
        body, name=name, grid=(H, NC),
        in_specs=in_specs,
        out_specs=[pl.BlockSpec((C, hk), lambda h, cc: (rev(h, cc), h)),
                   pl.BlockSpec((C, hk), lambda h, cc: (rev(h, cc), h)),
                   pl.BlockSpec((C, hv), lambda h, cc: (rev(h, cc), h)),
                   pl.BlockSpec((C, hv), lambda h, cc: (rev(h, cc), h)),
                   pl.BlockSpec((C, hk), lambda h, cc: (rev(h, cc), h)),
                   pl.BlockSpec((1, hv), lambda h, cc: (0, 0))],
        out_shape=[jax.ShapeDtypeStruct((Lp, DK), bf16), jax.ShapeDtypeStruct((Lp, DK), bf16),
                   jax.ShapeDtypeStruct((Lp, DV), bf16), jax.ShapeDtypeStruct((Lp, DV), bf16),
                   jax.ShapeDtypeStruct((Lp, DK), f32), jax.ShapeDtypeStruct((1, hv), f32)],
        scratch_shapes=[pltpu.VMEM((hk, hv), f32), pltpu.VMEM((C, C), f32),
                        pltpu.VMEM((C, hk), f32), pltpu.VMEM((C, hk), f32)],
        compiler_params=_cparams(("arbitrary", "arbitrary")),
    )(*args)


def _window_sums(x, back):
    n = x.shape[0]
    out = []
    s = x
    for w in (1, 2, 4, 8):
        s = s + pltpu.roll(s, w if back else n - w, 0)
        out.append(s)
    return out


def _pool_windows(hn, pad, n_real, name):
    Lp, D = hn.shape
    GW = D // POOL_GROUPS
    cb = min(GW, 256)
    per = GW // cb

    def body(h_ref, p_ref):
        g = pl.program_id(0) // per
        x = h_ref[...]
        s2, s4, s8, s16 = _window_sums(x, True)
        sel = jnp.where(g == 0, s2, jnp.where(g == 1, s4, jnp.where(g == 2, s8, s16)))
        win = jnp.left_shift(2, g).astype(f32)
        rows = lax.broadcasted_iota(jnp.int32, (Lp, 1), 0)
        t = (rows - pad).astype(f32)
        cnt = jnp.minimum(jnp.maximum(t, 0.0) + 1.0, win)
        p_ref[...] = jnp.where(rows >= pad, sel / cnt - x, 0.0).astype(bf16)

    return pl.pallas_call(
        body, name=name, grid=(D // cb,),
        in_specs=[pl.BlockSpec((Lp, cb), lambda i: (0, i))],
        out_specs=pl.BlockSpec((Lp, cb), lambda i: (0, i)),
        out_shape=jax.ShapeDtypeStruct((Lp, D), bf16),
        compiler_params=_cparams(("parallel",)),
    )(hn)


def _pool_windows_bwd(dp, pad, name):
    Lp, D = dp.shape
    GW = D // POOL_GROUPS
    cb = min(GW, 256)
    per = GW // cb

    def body(dp_ref, dh_ref):
        g = pl.program_id(0) // per
        rows = lax.broadcasted_iota(jnp.int32, (Lp, 1), 0)
        d = jnp.where(rows >= pad, dp_ref[...], 0.0)
        win = jnp.left_shift(2, g).astype(f32)
        t = (rows - pad).astype(f32)
        cnt = jnp.minimum(jnp.maximum(t, 0.0) + 1.0, win)
        s2, s4, s8, s16 = _window_sums(d / cnt, False)
        sel = jnp.where(g == 0, s2, jnp.where(g == 1, s4, jnp.where(g == 2, s8, s16)))
        dh_ref[...] = jnp.where(rows >= pad, sel - d, 0.0)

    return pl.pallas_call(
        body, name=name, grid=(D // cb,),
        in_specs=[pl.BlockSpec((Lp, cb), lambda i: (0, i))],
        out_specs=pl.BlockSpec((Lp, cb), lambda i: (0, i)),
        out_shape=jax.ShapeDtypeStruct((Lp, D), f32),
        compiler_params=_cparams(("parallel",)),
    )(dp)


def _pool_mix_fwd(xs, pooled, w, bias, scale, pad, name):
    Lp, D = xs.shape
    GW = D // POOL_GROUPS
    tm = _tile(Lp, 1056)

    def body(x_ref, p_ref, w_ref, b_ref, s_ref, o_ref):
        z = _dot(p_ref[...], w_ref[0]) + b_ref[...]
        keep = _row_ids(pl.program_id(1), tm) >= pad
        o_ref[...] = x_ref[...] + jnp.where(keep, z * s_ref[...], 0.0)

    blk = pl.BlockSpec((tm, GW), lambda g, i: (i, g))
    vec = pl.BlockSpec((1, GW), lambda g, i: (0, g))
    return pl.pallas_call(
        body, name=name, grid=(POOL_GROUPS, Lp // tm),
        in_specs=[blk, blk, pl.BlockSpec((1, GW, GW), lambda g, i: (g, 0, 0)), vec, vec],
        out_specs=blk, out_shape=jax.ShapeDtypeStruct((Lp, D), f32),
        compiler_params=_cparams(("parallel", "parallel")),
    )(xs, pooled, w, bias, scale)


def _pool_mix_bwd(dY, pooled, w, bias, scale, pad, name, after=None):
    Lp, D = dY.shape
    GW = D // POOL_GROUPS
    tm = _tile(Lp, 1056)
    nm = Lp // tm

    def body(dY_ref, p_ref, w_ref, b_ref, s_ref, dp_ref, dw_ref, db_ref, ds_ref, acc):
        i = pl.program_id(1)

        @pl.when(i == 0)
        def _():
            acc[...] = jnp.zeros_like(acc)
            db_ref[...] = jnp.zeros_like(db_ref)
            ds_ref[...] = jnp.zeros_like(ds_ref)

        keep = _row_ids(i, tm) >= pad
        dY_ = jnp.where(keep, dY_ref[...], 0.0)
        p = p_ref[...]
        z = _dot(p, w_ref[0]) + b_ref[...]
        ds_ref[...] += jnp.sum(dY_ * z, axis=0, keepdims=True)
        dz = dY_ * s_ref[...]
        db_ref[...] += jnp.sum(dz, axis=0, keepdims=True)
        dzb = dz.astype(bf16)
        acc[...] += _dot(p, dzb, TN)
        dp_ref[...] = _dot(dzb, w_ref[0], NT)

        @pl.when(i == nm - 1)
        def _():
            dw_ref[0] = acc[...].astype(bf16)

    blk = pl.BlockSpec((tm, GW), lambda g, i: (i, g))
    vec = pl.BlockSpec((1, GW), lambda g, i: (0, g))
    wsp = pl.BlockSpec((1, GW, GW), lambda g, i: (g, 0, 0))
    body, in_specs, args = _ordered(body, [blk, blk, wsp, vec, vec], [dY, pooled, w, bias, scale], after)
    return pl.pallas_call(
        body, name=name, grid=(POOL_GROUPS, nm),
        in_specs=in_specs, out_specs=[blk, wsp, vec, vec],
        out_shape=[jax.ShapeDtypeStruct((Lp, D), f32), jax.ShapeDtypeStruct((POOL_GROUPS, GW, GW), bf16),
                   jax.ShapeDtypeStruct((1, D), f32), jax.ShapeDtypeStruct((1, D), f32)],
        scratch_shapes=[pltpu.VMEM((GW, GW), f32)],
        compiler_params=_cparams(("parallel", "arbitrary")),
    )(*args)


def _loss_head(xs, target, g, first, name):
    Lp, D = xs.shape
    tm = GLA_CHUNK
    off = first // tm

    def body(x_ref, t_ref, g_ref, loss_ref, dxs_ref, dg_ref, half_ref):
        i = pl.program_id(0)

        @pl.when(i == 0)
        def _():
            loss_ref[...] = jnp.zeros_like(loss_ref)
            dg_ref[...] = jnp.zeros_like(dg_ref)

        @pl.when(i < off)
        def _():
            dxs_ref[...] = jnp.zeros_like(dxs_ref)
            half_ref[...] = jnp.zeros_like(half_ref)

        @pl.when(i >= off)
        def _():
            x = x_ref[...]
            rstd = lax.rsqrt(jnp.mean(x * x, axis=-1, keepdims=True) + EPS)
            xhat = x * rstd
            gg = g_ref[...]
            err = xhat * gg - t_ref[...]
            loss_ref[...] += 0.5 * jnp.sum(jnp.mean(err * err, axis=-1, keepdims=True))
            dy = err * (1.0 / D)
            dg_ref[...] += jnp.sum(dy * xhat, axis=0, keepdims=True)
            dxh = dy * gg
            out = rstd * (dxh - xhat * jnp.mean(dxh * xhat, axis=-1, keepdims=True))
            dxs_ref[...] = out
            half_ref[...] = (0.5 * out).astype(bf16)

    row = pl.BlockSpec((tm, D), lambda i: (i, 0))
    return pl.pallas_call(
        body, name=name, grid=(Lp // tm,),
        in_specs=[row, pl.BlockSpec((tm, D), lambda i: (jnp.maximum(i - off, 0), 0)), pl.BlockSpec((1, D), lambda i: (0, 0))],
        out_specs=[pl.BlockSpec((8, LANES), lambda i: (0, 0)), row, pl.BlockSpec((1, D), lambda i: (0, 0)), row],
        out_shape=[jax.ShapeDtypeStruct((8, LANES), f32), jax.ShapeDtypeStruct((Lp, D), f32),
                   jax.ShapeDtypeStruct((1, D), f32), jax.ShapeDtypeStruct((Lp, D), bf16)],
        compiler_params=_cparams(("arbitrary",)),
    )(xs, target, g)


def _adam_math(w, g, m, v):
    m2 = ADAM_B1 * m + (1.0 - ADAM_B1) * g
    v2 = ADAM_B2 * v + (1.0 - ADAM_B2) * (g * g)
    m_hat = m2 / (1.0 - ADAM_B1 ** ADAM_STEP)
    v_hat = v2 / (1.0 - ADAM_B2 ** ADAM_STEP)
    delta = -ADAM_LR * (m_hat / (jnp.sqrt(v_hat) + ADAM_EPS) + ADAM_WD * w)
    return delta, m2, v2


def _adamw(w, m, v, unit, own, own_idx, recv, prev, name, after=None):
    U, R, C = w.shape
    tr, tc = _tile2(R, C, 256, 8 if own.dtype == f32 and recv is None else 16)
    n_recv = 0 if recv is None else recv.shape[0]

    def body(idx_ref, w_ref, m_ref, v_ref, own_ref, *rest):
        rest = list(rest)
        recv_refs = [rest.pop(0) for _ in range(n_recv)]
        if prev is not None:
            rest = rest[4:]
        g_ref, d_ref, m2_ref, v2_ref = rest
        g = own_ref[0].astype(f32)
        for r_ref in recv_refs:
            g = g + r_ref[0].astype(f32)
        delta, m2, v2 = _adam_math(w_ref[0], g, m_ref[0], v_ref[0])
        g_ref[0] = g
        d_ref[0] = delta
        m2_ref[0] = m2
        v2_ref[0] = v2

    blk = pl.BlockSpec((1, tr, tc), lambda i, j, idx: (unit, i, j))
    in_specs = [blk, blk, blk, pl.BlockSpec((1, tr, tc), lambda i, j, idx: (idx[0], i, j))]
    args = [w, m, v, own]
    for p in range(n_recv):
        in_specs.append(pl.BlockSpec((1, tr, tc), lambda i, j, idx, p=p: (p, i, j)))
        args.append(recv)
    aliases = {}
    if prev is not None:
        for t in range(4):
            aliases[1 + len(args) + t] = t
        in_specs += [ANY] * 4
        args += list(prev)
    body, in_specs, args = _ordered(body, in_specs, args, after, lead=1)
    out = jax.ShapeDtypeStruct((U, R, C), f32)
    return pl.pallas_call(
        body, name=name,
        grid_spec=pltpu.PrefetchScalarGridSpec(
            num_scalar_prefetch=1, grid=(R // tr, C // tc), in_specs=in_specs, out_specs=[blk] * 4),
        out_shape=[out] * 4, input_output_aliases=aliases,
        compiler_params=_cparams(("parallel", "parallel")),
    )(own_idx, *args)


def _place():
    return lax.axis_index("x"), lax.axis_index("y"), lax.axis_index("c")


HBM = pl.BlockSpec(memory_space=pltpu.HBM)
SEM = pl.BlockSpec(memory_space=pltpu.SEMAPHORE)
VMEM_SPEC = pl.BlockSpec(memory_space=pltpu.VMEM)
EFFECT = pltpu.SideEffectType.DATAFLOW_SIDE_EFFECTING
TOKEN = jax.ShapeDtypeStruct((8, LANES), f32)


def _hbm(x):
    return pltpu.with_memory_space_constraint(x, pltpu.HBM)


def _hbm_like(xs):
    return [pltpu.HBM(x.shape, x.dtype) for x in xs]


def _slot(px, py, pc):
    return 4 * px + 2 * py + pc


def _gather_start(shards, after, name):
    n = len(shards)
    me = _slot(*_place())
    bufs = [lax.dynamic_update_slice(lax.empty((N_DEV,) + s.shape, s.dtype), s[None], (me,) + (0,) * s.ndim) for s in shards]

    def body(*refs):
        ins, land = refs[:n], refs[n:2 * n]
        send, recv = refs[2 * n + 1], refs[2 * n + 2]
        token = refs[-1]
        x, y, c = _place()
        to = [(x, y, 1 - c), (1 - x, y, c), (x, 1 - y, c), (1 - x, 1 - y, c)]
        for a in range(n):
            for k, dev in enumerate(to):
                pltpu.make_async_remote_copy(
                    src_ref=ins[a], dst_ref=land[a].at[_slot(x, y, c)], send_sem=send.at[4 * a + k], recv_sem=recv.at[4 * a + k],
                    device_id=dev, device_id_type=MESH).start()
        token[...] = jnp.zeros_like(token)

    out = pl.pallas_call(
        body, name=name,
        in_specs=[HBM] * (2 * n) + [ANY],
        out_specs=[SEM, SEM] + [HBM] * (2 * n) + [VMEM_SPEC],
        out_shape=[pltpu.SemaphoreType.DMA((4 * n,)), pltpu.SemaphoreType.DMA((4 * n,))] + _hbm_like(shards) + _hbm_like(bufs) + [TOKEN],
        input_output_aliases={i: 2 + i for i in range(2 * n)},
        compiler_params=pltpu.CompilerParams(has_side_effects=EFFECT),
    )(*[_hbm(s) for s in shards], *[_hbm(b) for b in bufs], after)
    return dict(send1=out[0], recv1=out[1], shards=list(out[2:2 + n]), bufs=list(out[2 + n:2 + 2 * n]), token=out[-1])


def _gather_mid(h, after, name):
    n = len(h["bufs"])

    def body(*refs):
        land, recv1 = refs[:n], refs[n]
        send2, recv2 = refs[n + 2], refs[n + 3]
        token = refs[-1]
        x, y, c = _place()
        chips = [(1 - x, y), (x, 1 - y), (1 - x, 1 - y)]
        for j, (px, py) in enumerate(chips):
            for a in range(n):
                blk = land[a].at[_slot(px, py, c)]
                pltpu.make_async_remote_copy(
                    src_ref=blk, dst_ref=blk, send_sem=send2.at[3 * a + j], recv_sem=recv1.at[4 * a + 1 + j],
                    device_id=(px, py, c), device_id_type=MESH).wait_recv()
                pltpu.make_async_remote_copy(
                    src_ref=blk, dst_ref=blk, send_sem=send2.at[3 * a + j], recv_sem=recv2.at[3 * a + j],
                    device_id=(x, y, 1 - c), device_id_type=MESH).start()
        token[...] = jnp.zeros_like(token)

    out = pl.pallas_call(
        body, name=name,
        in_specs=[HBM] * n + [SEM, ANY],
        out_specs=[SEM, SEM] + [HBM] * n + [VMEM_SPEC],
        out_shape=[pltpu.SemaphoreType.DMA((3 * n,)), pltpu.SemaphoreType.DMA((3 * n,))] + _hbm_like(h["bufs"]) + [TOKEN],
        input_output_aliases={i: 2 + i for i in range(n)},
        compiler_params=pltpu.CompilerParams(has_side_effects=EFFECT),
    )(*h["bufs"], h["recv1"], after)
    h.update(send2=out[0], recv2=out[1], bufs=list(out[2:2 + n]), token=out[-1])
    return h


def _gather_end(h, after, name):
    n = len(h["bufs"])

    def body(*refs):
        ins, land = refs[:n], refs[n:2 * n]
        send1, recv1, send2, recv2 = refs[2 * n:2 * n + 4]
        x, y, c = _place()
        chips = [(1 - x, y), (x, 1 - y), (1 - x, 1 - y)]
        sib = (x, y, 1 - c)
        for a in range(n):
            mine = land[a].at[_slot(x, y, c)]
            for k in range(4):
                pltpu.make_async_remote_copy(
                    src_ref=ins[a], dst_ref=mine, send_sem=send1.at[4 * a + k], recv_sem=recv1.at[4 * a + k],
                    device_id=sib, device_id_type=MESH).wait_send()
            theirs = land[a].at[_slot(x, y, 1 - c)]
            pltpu.make_async_remote_copy(
                src_ref=ins[a], dst_ref=theirs, send_sem=send1.at[4 * a], recv_sem=recv1.at[4 * a],
                device_id=sib, device_id_type=MESH).wait_recv()
            for j, (px, py) in enumerate(chips):
                sent = land[a].at[_slot(px, py, c)]
                got = land[a].at[_slot(px, py, 1 - c)]
                pltpu.make_async_remote_copy(
                    src_ref=sent, dst_ref=sent, send_sem=send2.at[3 * a + j], recv_sem=recv2.at[3 * a + j],
                    device_id=sib, device_id_type=MESH).wait_send()
                pltpu.make_async_remote_copy(
                    src_ref=sent, dst_ref=got, send_sem=send2.at[3 * a + j], recv_sem=recv2.at[3 * a + j],
                    device_id=sib, device_id_type=MESH).wait_recv()

    out = pl.pallas_call(
        body, name=name,
        in_specs=[HBM] * (2 * n) + [SEM] * 4 + [ANY],
        out_specs=[HBM] * n,
        out_shape=_hbm_like(h["bufs"]),
        input_output_aliases={n + i: i for i in range(n)},
        compiler_params=pltpu.CompilerParams(has_side_effects=EFFECT),
    )(*h["shards"], *h["bufs"], h["send1"], h["recv1"], h["send2"], h["recv2"], after)
    return list(out)


def _peer_plan(kind, x, y, c):
    if kind == "pair":
        return [(2 * q + (1 - c), q, (x, y, 1 - c)) for q in range(4)]
    chips = [(1 - x, y), (x, 1 - y), (1 - x, 1 - y)]
    return [(2 * px + py, k, (px, py, c)) for k, (px, py) in enumerate(chips)]


def _exchange_start(kind, srcs, after, name):
    n = len(srcs)
    K = 4 if kind == "pair" else 3
    lands = [_hbm(lax.empty((K,) + s.shape[1:], s.dtype)) for s in srcs]

    def body(*refs):
        ins, land = refs[:n], refs[n:2 * n]
        send, recv = refs[2 * n + 1], refs[2 * n + 2]
        token = refs[-1]
        for a in range(n):
            for k, (si, di, dev) in enumerate(_peer_plan(kind, *_place())):
                pltpu.make_async_remote_copy(
                    src_ref=ins[a].at[si], dst_ref=land[a].at[di], send_sem=send.at[K * a + k], recv_sem=recv.at[K * a + k],
                    device_id=dev, device_id_type=MESH).start()
        token[...] = jnp.zeros_like(token)

    out = pl.pallas_call(
        body, name=name,
        in_specs=[HBM] * (2 * n) + [ANY],
        out_specs=[SEM, SEM] + [HBM] * (2 * n) + [VMEM_SPEC],
        out_shape=[pltpu.SemaphoreType.DMA((K * n,)), pltpu.SemaphoreType.DMA((K * n,))] + _hbm_like(srcs) + _hbm_like(lands) + [TOKEN],
        input_output_aliases={i: 2 + i for i in range(2 * n)},
        compiler_params=pltpu.CompilerParams(has_side_effects=EFFECT),
    )(*[_hbm(s) for s in srcs], *lands, after)
    return dict(kind=kind, send=out[0], recv=out[1], srcs=list(out[2:2 + n]), lands=list(out[2 + n:2 + 2 * n]), token=out[-1])


def _exchange_wait(h, after, name):
    n = len(h["srcs"])
    kind = h["kind"]
    K = 4 if kind == "pair" else 3

    def body(*refs):
        ins, land = refs[:n], refs[n:2 * n]
        send, recv = refs[2 * n], refs[2 * n + 1]
        for a in range(n):
            for k, (si, di, dev) in enumerate(_peer_plan(kind, *_place())):
                cp = pltpu.make_async_remote_copy(
                    src_ref=ins[a].at[si], dst_ref=land[a].at[di], send_sem=send.at[K * a + k], recv_sem=recv.at[K * a + k],
                    device_id=dev, device_id_type=MESH)
                cp.wait_send()
                cp.wait_recv()

    out = pl.pallas_call(
        body, name=name,
        in_specs=[HBM] * (2 * n) + [SEM, SEM, ANY],
        out_specs=[HBM] * (2 * n),
        out_shape=_hbm_like(h["srcs"]) + _hbm_like(h["lands"]),
        input_output_aliases={i: i for i in range(2 * n)},
        compiler_params=pltpu.CompilerParams(has_side_effects=EFFECT),
    )(*h["srcs"], *h["lands"], h["send"], h["recv"], after)
    return list(out[:n]), list(out[n:])


def _pair_add(g, got, c_idx, name):
    _, R, C = g.shape
    tr, tc = _tile2(R, C, 512, 16)

    def body(c_ref, a_ref, b_ref, o_ref):
        o_ref[0] = (a_ref[0].astype(f32) + b_ref[0].astype(f32)).astype(o_ref.dtype)

    return pl.pallas_call(
        body, name=name,
        grid_spec=pltpu.PrefetchScalarGridSpec(
            num_scalar_prefetch=1, grid=(4, R // tr, C // tc),
            in_specs=[pl.BlockSpec((1, tr, tc), lambda q, i, j, c: (2 * q + c[0], i, j)),
                      pl.BlockSpec((1, tr, tc), lambda q, i, j, c: (q, i, j))],
            out_specs=pl.BlockSpec((1, tr, tc), lambda q, i, j, c: (q, i, j))),
        out_shape=jax.ShapeDtypeStruct((4, R, C), g.dtype),
        compiler_params=_cparams(("parallel", "parallel", "parallel")),
    )(c_idx, g, got)


def _small_exchange(send, gather, name, after=None):
    R = send.shape[-2]

    def body(in_ref, out_ref, send_sems, recv_sems):
        x, y, c = _place()
        me = 4 * x + 2 * y + c
        out_ref[me] = in_ref[...] if gather else in_ref[me]
        cps = []
        for k in range(1, N_DEV):
            px, py, pc = x ^ ((k >> 2) & 1), y ^ ((k >> 1) & 1), c ^ (k & 1)
            src = in_ref if gather else in_ref.at[4 * px + 2 * py + pc]
            cps.append(pltpu.make_async_remote_copy(
                src_ref=src, dst_ref=out_ref.at[me],
                send_sem=send_sems.at[k - 1], recv_sem=recv_sems.at[k - 1],
                device_id=(px, py, pc), device_id_type=MESH))
        for cp in cps:
            cp.start()
        for cp in cps:
            cp.wait()

    body, in_specs, args = _ordered(body, [pl.BlockSpec(memory_space=pltpu.VMEM)], [send], after)
    return pl.pallas_call(
        body, name=name,
        in_specs=in_specs, out_specs=pl.BlockSpec(memory_space=pltpu.VMEM),
        out_shape=jax.ShapeDtypeStruct((N_DEV, R, LANES), f32),
        scratch_shapes=[pltpu.SemaphoreType.DMA((N_DEV - 1,)), pltpu.SemaphoreType.DMA((N_DEV - 1,))],
    )(*args)


def _sum_blocks(blocks, name):
    def body(in_ref, o_ref):
        s = in_ref[0]
        for d in range(1, N_DEV):
            s = s + in_ref[d]
        o_ref[0] = s

    return pl.pallas_call(body, name=name, out_shape=jax.ShapeDtypeStruct((1,) + blocks.shape[1:], f32))(blocks)


def _rows(n):
    return -(-n // LANES)


def _pack(arrs, total_rows):
    parts = []
    for a in arrs:
        flat = a.reshape(-1).astype(f32)
        parts.append(jnp.pad(flat, (0, _rows(flat.size) * LANES - flat.size)))
    flat = jnp.concatenate(parts)
    return jnp.pad(flat, (0, total_rows * LANES - flat.size)).reshape(total_rows, LANES)


def _unpack(packed, shapes):
    lead = packed.shape[:-2]
    flat = packed.reshape(lead + (-1,))
    out, pos = [], 0
    for s in shapes:
        n = 1
        for d in s:
            n *= d
        out.append(flat[..., pos:pos + n].reshape(lead + tuple(s)))
        pos += _rows(n) * LANES
    return out


def _to_shards(full, axis):
    s = full.shape
    return jnp.moveaxis(full.reshape(s[:axis] + (N_DEV, s[axis] // N_DEV) + s[axis + 1:]), axis, 0)


def _from_shards(sh, axis):
    m = jnp.moveaxis(sh, 0, axis)
    s = m.shape
    return m.reshape(s[:axis] + (s[axis] * s[axis + 1],) + s[axis + 2:])


def kernel(x, meta, ffn_norm, ffn_w_gate, ffn_w_up, ffn_w_down, gla_norm, gla_w_in, gla_w_lr, gla_b_lr, gla_head_norm, gla_w_out, pool_norm, pool_w, pool_b, pool_scale, final_norm, loss_target, m_meta, m_ffn_norm, m_ffn_w_gate, m_ffn_w_up, m_ffn_w_down, m_gla_norm, m_gla_w_in, m_gla_w_lr, m_gla_b_lr, m_gla_head_norm, m_gla_w_out, m_pool_norm, m_pool_w, m_pool_b, m_pool_scale, m_final_norm, v_meta, v_ffn_norm, v_ffn_w_gate, v_ffn_w_up, v_ffn_w_down, v_gla_norm, v_gla_w_in, v_gla_w_lr, v_gla_b_lr, v_gla_head_norm, v_gla_w_out, v_pool_norm, v_pool_w, v_pool_b, v_pool_scale, v_final_norm):
    H = GLA_HEADS
    _, SEQ, D = x.shape
    Fs = ffn_w_gate.shape[-1]
    DK, DV = D // 2, D
    hv = DV // H
    GW = D // POOL_GROUPS
    INW = 2 * DK + 2 * DV + GATE_RANK
    NPK = 2 * DK + 2 * DV + GATE_PAD
    pad = (-N_META) % GLA_CHUNK
    first = pad + N_META
    Lp = first + SEQ
    n_units = ffn_w_gate.shape[0] * ffn_w_gate.shape[1]
    assert first % GLA_CHUNK == 0 and Lp % GLA_CHUNK == 0 and pad >= POOL_GROUPS * 4

    px, py, pc = _place()
    c_idx = jnp.reshape(pc, (1,)).astype(jnp.int32)
    q_idx = jnp.reshape(2 * px + py, (1,)).astype(jnp.int32)
    zero_idx = jnp.zeros((1,), jnp.int32)

    small_sh = [meta, ffn_norm, gla_w_lr, pool_norm, pool_b, pool_scale]
    small_axis = [1, 2, 2, 1, 2, 1]
    sh_shapes = [a.shape for a in small_sh]
    sh_rows = -(-sum(_rows(a.size) for a in small_sh) // 8) * 8
    gathered = _small_exchange(_pack(small_sh, sh_rows), True, "small_gather")
    meta_f, ffn_norm_f, wlr_f, pool_norm_f, pool_b_f, pool_scale_f = [
        _from_shards(a, ax) for a, ax in zip(_unpack(gathered, sh_shapes), small_axis)]
    ffn_norm_f = ffn_norm_f.reshape(n_units, 1, D)
    wlr128 = jnp.pad(wlr_f[0], ((0, GATE_PAD - GATE_RANK), (0, 0)))

    def t_units(w):
        return jnp.swapaxes(w, -1, -2).reshape(n_units, Fs, D)

    wd_units = ffn_w_down.reshape(n_units, Fs, D)
    ffn_shards = [[t_units(ffn_w_gate)[u].astype(bf16), t_units(ffn_w_up)[u].astype(bf16), wd_units[u].astype(bf16)]
                  for u in range(n_units)]
    mixer_shards = [gla_w_in[0].T.astype(bf16), gla_w_out[0].astype(bf16), pool_w[0].astype(bf16)]
    gather_order = [("ffn0", ffn_shards[0]), ("mixers", mixer_shards)] + [(f"ffn{u}", ffn_shards[u]) for u in range(1, n_units)]
    c_lr = 2 * DK + DV
    c_r = 2 * DK + 2 * DV
    gate_blk = c_r // GATE_PAD

    def gather_begin(i, after):
        tag, shards = gather_order[i]
        return _gather_start(shards, after, f"gather_start_{tag}")

    def gather_next(i, h, after):
        tag = gather_order[i][0]
        h = _gather_mid(h, after, f"gather_mid_{tag}")
        nxt = gather_begin(i + 1, h["token"]) if i + 1 < len(gather_order) else None
        done = _gather_end(h, h["token"] if nxt is None else nxt["token"], f"gather_end_{tag}")
        return done, nxt

    xs = jnp.concatenate([jnp.zeros((pad, D), f32), meta_f, x[0]], axis=0)
    saved = {}
    ffn_w = [None] * n_units

    def ffn_f(u, xs):
        out, h, G, U = _ffn_fwd(xs, ffn_norm_f[u], *ffn_w[u], name=f"ffn_fwd{u}")
        saved[("ffn", u)] = (xs, h, G, U)
        return out

    def gla_f(xs, win_p, wout_full):
        hn = _rms_fwd(xs, gla_norm, bf16, "gla_norm_fwd")
        proj = _mm(hn, win_p, "nt", f32, "gla_proj", tm=1056, tn=896, tk=2048)
        lg = _gate_fwd(proj, wlr128, gla_b_lr, pad, gate_blk, "gla_gate_fwd")
        o, y, states = _gla_fwd(proj, lg, gla_head_norm, H, "gla_core_fwd")
        out = _mm(y, wout_full, "nn", f32, "gla_out", tm=1056, tn=512, tk=2048, residual=xs)
        saved["gla"] = (xs, hn, proj, lg, o, y, states)
        return out

    def pool_f(xs, wpool_full):
        hn = _rms_fwd(xs, pool_norm_f, f32, "pool_norm_fwd")
        pooled = _pool_windows(hn, pad, Lp - pad, "pool_windows_fwd")
        out = _pool_mix_fwd(xs, pooled, wpool_full, pool_b_f.reshape(1, D), pool_scale_f, pad, "pool_mix_fwd")
        saved["pool"] = (xs, pooled)
        return out

    depth = ffn_w_gate.shape[0]
    assert depth == 2 and n_units == 4
    h = gather_begin(0, gathered)
    ffn_w[0], h = gather_next(0, h, h["token"])
    xs = ffn_f(0, xs)
    (win_g, wout_g, wpool_g), h = gather_next(1, h, xs)
    win_full = win_g.reshape(INW, D)
    win_p = jnp.concatenate([win_full[:c_lr], win_full[c_lr + GATE_RANK:], win_full[c_lr:c_lr + GATE_RANK],
                             jnp.zeros((GATE_PAD - GATE_RANK, D), bf16)], axis=0)
    wout_full = wout_g.reshape(DV, D)
    wpool_full = _from_shards(wpool_g, 1)
    xs = gla_f(xs, win_p, wout_full)
    ffn_w[1], h = gather_next(2, h, xs)
    xs = ffn_f(1, xs)
    ffn_w[2], h = gather_next(3, h, xs)
    xs = ffn_f(2, xs)
    xs = pool_f(xs, wpool_full)
    ffn_w[3], h = gather_next(4, h, xs)
    xs = ffn_f(3, xs)
    loss_part, dxs, d_final, dyh = _loss_head(xs, loss_target[0], final_norm.reshape(1, D), first, "loss_head")

    class Reduce:
        def __init__(self, tag, grads, after=None):
            self.tag = tag
            self.h = _exchange_start("pair", grads, grads[0] if after is None else after, f"pair_start_{tag}")
            self.token = self.h["token"]

        def mid(self, after):
            grads, got = _exchange_wait(self.h, after, f"pair_wait_{self.tag}")
            self.sums = [_pair_add(g, r, c_idx, f"pair_add_{self.tag}{a}") for a, (g, r) in enumerate(zip(grads, got))]
            self.h = _exchange_start("chips", self.sums, self.sums[-1], f"chips_start_{self.tag}")
            self.token = self.h["token"]

        def end(self, after):
            sums, recv = _exchange_wait(self.h, after, f"chips_wait_{self.tag}")
            return list(zip(sums, recv))

    d_ffn_norm = [None] * n_units
    small_grads = {}

    def ffn_b(u, dY, dyh, prev):
        xs_in, h_, G, U = saved[("ffn", u)]
        wg, wu, wd = ffn_w[u]
        tok = None if prev is None else prev.token
        dG, dU, A = _ffn_bwd_act(dyh, wd, G, U, f"ffn_act{u}", after=tok)
        dh = _ffn_bwd_dh(dG, dU, wg, wu, f"ffn_dh{u}")
        dxs, dg, dyh_next = _rms_bwd(dY, dh, xs_in, ffn_norm_f[u], pad, f"ffn_norm_bwd{u}")
        if prev is not None:
            prev.mid(dxs)
            tok = prev.token
        dwg = _ffn_bwd_wgrad(dG, h_, f"ffn_wgrad_gate{u}", after=tok)
        dwu = _ffn_bwd_wgrad(dU, h_, f"ffn_wgrad_up{u}", after=tok)
        dwd = _ffn_bwd_wgrad(A, dyh, f"ffn_wgrad_down{u}", after=tok)
        d_ffn_norm[u] = dg
        return dxs, dyh_next, Reduce(f"ffn{u}", [dwg, dwu, dwd])

    def gla_b(dY, prev):
        xs_in, hn, proj, lg, o, y, states = saved["gla"]
        dyb = dY.astype(bf16)
        dy = _mm(dyb, wout_full, "nt", f32, "gla_out_dgrad", tm=1056, tn=512, tk=2048, after=prev.token)
        dwout = _mm(y, dyb, "tn", bf16, "gla_out_wgrad", tm=1024, tn=1024, tk=528, after=prev.token)
        prev.mid(dwout)
        dq, dk, dv, dr, dlg, dhw = _gla_bwd(dy, proj, lg, o, states, gla_head_norm, H, pad, "gla_core_bwd", after=prev.token)
        dlr, dwlr, dblr = _gate_bwd(dlg, proj, wlr128, gla_b_lr, pad, gate_blk, "gla_gate_bwd")
        dproj = jnp.concatenate([dq, dk, dv, dr, dlr], axis=1)
        dwin_p = _mm(dproj, hn, "tn", bf16, "gla_proj_wgrad", tm=896, tn=1024, tk=528)
        dhn = _mm(dproj, win_p, "nn", f32, "gla_proj_dgrad", tm=1056, tn=512, tk=896)
        dxs, dgn, dyh_next = _rms_bwd(dY, dhn, xs_in, gla_norm, pad, "gla_norm_bwd")
        dwin = jnp.concatenate([dwin_p[:c_lr], dwin_p[c_r:c_r + GATE_RANK], dwin_p[c_lr:c_r]], axis=0)
        small_grads.update(gla_w_lr=dwlr[:GATE_RANK][None], gla_b_lr=dblr, gla_head_norm=dhw, gla_norm=dgn)
        return dxs, dyh_next, Reduce("gla", [dwin.reshape(N_DEV, INW // N_DEV, D), dwout.reshape(N_DEV, DV // N_DEV, D)])

    def pool_b_(dY, prev):
        xs_in, pooled = saved["pool"]
        dp, dw, db, ds = _pool_mix_bwd(dY, pooled, wpool_full, pool_b_f.reshape(1, D), pool_scale_f, pad, "pool_mix_bwd",
                                       after=prev.token)
        dhn = _pool_windows_bwd(dp, pad, "pool_windows_bwd")
        dxs, dgn, dyh_next = _rms_bwd(dY, dhn, xs_in, pool_norm_f, pad, "pool_norm_bwd")
        prev.mid(dxs)
        dws = _to_shards(dw, 1)
        small_grads.update(pool_b=db.reshape(1, POOL_GROUPS, GW), pool_scale=ds, pool_norm=dgn)
        return dxs, dyh_next, Reduce("pool", [dws.reshape(N_DEV, POOL_GROUPS * GW // N_DEV, GW)], after=prev.token)

    sh_names = ["meta", "ffn_norm", "gla_w_lr", "pool_norm", "pool_b", "pool_scale"]
    rep_names = ["gla_norm", "gla_b_lr", "gla_head_norm", "final_norm"]
    rep_w = [gla_norm, gla_b_lr, gla_head_norm, final_norm]
    rep_shapes = [a.shape for a in rep_w]
    rep_rows = -(-sum(_rows(a.size) for a in rep_w) // 8) * 8

    def small_path(dxs0):
        small_grads.update(meta=dxs0[pad:first], ffn_norm=jnp.concatenate(d_ffn_norm, axis=0).reshape(n_units // 2, 2, D),
                           final_norm=d_final.reshape(D))
        by_owner = [_to_shards(small_grads[nm].reshape(full_shape), ax) for nm, full_shape, ax in zip(
            sh_names, [meta_f.shape, (ffn_norm.shape[0], 2, D), wlr_f.shape, pool_norm_f.shape, pool_b_f.shape, pool_scale_f.shape],
            small_axis)]
        rep_pack = _pack([small_grads[nm].reshape(s) for nm, s in zip(rep_names, rep_shapes)], rep_rows)
        send = jnp.stack([
            jnp.concatenate([_pack([g[d] for g in by_owner], sh_rows), rep_pack, loss_part], axis=0) for d in range(N_DEV)])
        total = _sum_blocks(_small_exchange(send, False, "small_reduce"), "small_sum")
        n_small = sh_rows + rep_rows

        def pack_small(sh_list, rep_list):
            return jnp.concatenate([_pack(sh_list, sh_rows), _pack(rep_list, rep_rows)], axis=0)[None]

        w_small = pack_small(small_sh, rep_w)
        m_small = pack_small([m_meta, m_ffn_norm, m_gla_w_lr, m_pool_norm, m_pool_b, m_pool_scale],
                             [m_gla_norm, m_gla_b_lr, m_gla_head_norm, m_final_norm])
        v_small = pack_small([v_meta, v_ffn_norm, v_gla_w_lr, v_pool_norm, v_pool_b, v_pool_scale],
                             [v_gla_norm, v_gla_b_lr, v_gla_head_norm, v_final_norm])
        small_out = _adamw(w_small, m_small, v_small, 0, total[:, :n_small], zero_idx, None, None, "adamw_small")
        small_res = {}
        for kind, packed in zip(("grad", "delta", "new_m", "new_v"), small_out):
            sh_vals = _unpack(packed[0, :sh_rows], sh_shapes)
            rep_vals = _unpack(packed[0, sh_rows:], rep_shapes)
            for nm, val in zip(sh_names + rep_names, sh_vals + rep_vals):
                small_res[(kind, nm)] = val
        return total[0, n_small, 0], small_res, small_out[0]

    def ffn_b_last(dY, dyh, prev):
        xs_in, h_, G, U = saved[("ffn", 0)]
        wg, wu, wd = ffn_w[0]
        dG, dU, A = _ffn_bwd_act(dyh, wd, G, U, "ffn_act0", after=prev.token)
        dwd = _ffn_bwd_wgrad(A, dyh, "ffn_wgrad_down0", after=prev.token)
        r_d = Reduce("ffn0_down", [dwd])
        dh = _ffn_bwd_dh(dG, dU, wg, wu, "ffn_dh0", after=r_d.token)
        dxs, dg, _ = _rms_bwd(dY, dh, xs_in, ffn_norm_f[0], pad, "ffn_norm_bwd0")
        d_ffn_norm[0] = dg
        small = small_path(dxs)
        prev.mid(small[2])
        r_d.mid(prev.token)
        dwg = _ffn_bwd_wgrad(dG, h_, "ffn_wgrad_gate0", after=r_d.token)
        r_g = Reduce("ffn0_gate", [dwg])
        dwu = _ffn_bwd_wgrad(dU, h_, "ffn_wgrad_up0", after=r_g.token)
        r_g.mid(dwu)
        r_u = Reduce("ffn0_up", [dwu], after=r_g.token)
        return dxs, small, (r_g, r_u, r_d)

    dxs, dyh, r3 = ffn_b(3, dxs, dyh, None)
    dxs, dyh, rp = pool_b_(dxs, r3)
    dxs, dyh, r2 = ffn_b(2, dxs, dyh, rp)
    dxs, dyh, r1 = ffn_b(1, dxs, dyh, r2)
    dxs, dyh, rg = gla_b(dxs, r1)
    dxs, (loss, small_res, _), r0 = ffn_b_last(dxs, dyh, rg)
    grad_x = dxs[first:].reshape(x.shape)
    r_last = r0[1]

    big_res = {}

    def adam_one(nm, w, m, v, entry, transposed=False):
        sums, recv = entry
        R, C = sums.shape[1:]
        w1, m1, v1 = ((t[0].T if transposed else t).reshape(1, R, C) for t in (w, m, v))
        out = _adamw(w1, m1, v1, 0, sums, q_idx, recv, None, f"adamw_{nm}", after=r_last.token)
        for kind, val in zip(("grad", "delta", "new_m", "new_v"), out):
            big_res[(kind, nm)] = val[0].T[None] if transposed else val.reshape(w.shape)
        return out[0]

    e_gla = rg.end(dxs)
    done = adam_one("gla_w_in", gla_w_in, m_gla_w_in, v_gla_w_in, e_gla[0], transposed=True)
    done = adam_one("gla_w_out", gla_w_out, m_gla_w_out, v_gla_w_out, e_gla[1])
    done = adam_one("pool_w", pool_w, m_pool_w, v_pool_w, rp.end(done)[0])
    r_last.mid(done)

    ffn_names = ["ffn_w_gate", "ffn_w_up", "ffn_w_down"]
    ffn_wmv = [tuple(t_units(t) for t in (ffn_w_gate, m_ffn_w_gate, v_ffn_w_gate)),
               tuple(t_units(t) for t in (ffn_w_up, m_ffn_w_up, v_ffn_w_up)),
               tuple(t.reshape(n_units, Fs, D) for t in (ffn_w_down, m_ffn_w_down, v_ffn_w_down))]
    ffn_prev = [[lax.empty((n_units, Fs, D), f32) for _ in range(4)] for _ in range(3)]
    order_after = r_last.token
    for u, red in ((3, r3), (2, r2), (1, r1), (0, r0)):
        entries = [r.end(done)[0] for r in red] if u == 0 else red.end(done)
        for a in range(3):
            sums, recv = entries[a]
            ffn_prev[a] = _adamw(*ffn_wmv[a], u, sums, q_idx, recv, ffn_prev[a], f"adamw_{ffn_names[a]}{u}", after=order_after)
            done = order_after = ffn_prev[a][0]
    for a in range(3):
        for kind, val in zip(("grad", "delta", "new_m", "new_v"), ffn_prev[a]):
            val = val.reshape(ffn_w_down.shape)
            big_res[(kind, ffn_names[a])] = val if a == 2 else jnp.swapaxes(val, -1, -2)

    order = ["meta", "ffn_norm", "ffn_w_gate", "ffn_w_up", "ffn_w_down", "gla_norm", "gla_w_in", "gla_w_lr", "gla_b_lr",
             "gla_head_norm", "gla_w_out", "pool_norm", "pool_w", "pool_b", "pool_scale", "final_norm"]
    res = {**small_res, **big_res}
    outs = [loss, grad_x]
    for kind in ("grad", "delta", "new_m", "new_v"):
        outs += [res[(kind, nm)] for nm in order]
    return tuple(outs)
```

```python
import functools

import jax
import jax.numpy as jnp
from jax import lax
from jax.experimental import pallas as pl
from jax.experimental.pallas import tpu as pltpu

f32 = jnp.float32
bf16 = jnp.bfloat16

N_DEV = 8
N_META = 16
GLA_HEADS = 4
GLA_CHUNK = 64
GLA_SUB = 16
GATE_RANK = 16
GATE_PAD = 128
GATE_NORM = 16.0
EPS = 1e-6
POOL_GROUPS = 4
ADAM_LR = 0.001
ADAM_B1 = 0.9
ADAM_B2 = 0.999
ADAM_EPS = 1e-08
ADAM_WD = 0.01
ADAM_STEP = 10
LANES = 128
VMEM_LIMIT_MB = 56

NN = (((1,), (0,)), ((), ()))
NT = (((1,), (1,)), ((), ()))
TN = (((0,), (0,)), ((), ()))
HI = lax.Precision.HIGHEST
MESH = pl.DeviceIdType.MESH
ANY = pl.BlockSpec(memory_space=pl.ANY)


def _cparams(sem=None, vmem_mb=None):
    kw = {}
    if sem is not None:
        kw["dimension_semantics"] = sem
    if vmem_mb is not None:
        kw["vmem_limit_bytes"] = vmem_mb * 2 ** 20
    return pltpu.CompilerParams(**kw)


def _tile(n, target, mult=16):
    best = None
    for t in range(mult, min(n, target) + 1, mult):
        if n % t == 0:
            best = t
    assert best is not None, (n, target, mult)
    return best


def _tile2(R, C, rows, mult):
    if R % mult == 0:
        return _tile(R, rows, mult), C
    return R, _tile(C, 256, LANES)


def _dot(a, b, dims=NN, precision=None):
    return lax.dot_general(a, b, dims, preferred_element_type=f32, precision=precision)


def _sigmoid(x):
    return 1.0 / (1.0 + jnp.exp(-x))


def _row_ids(tile_index, tm):
    return tile_index * tm + lax.broadcasted_iota(jnp.int32, (tm, 1), 0)


def _ordered(body, in_specs, args, after, lead=0):
    if after is None:
        return body, in_specs, args
    pos = lead + len(args)

    def body_without(*refs):
        return body(*refs[:pos], *refs[pos + 1:])

    return body_without, list(in_specs) + [ANY], list(args) + [after]


def _rms_fwd(xs, g, out_dtype, name):
    Lp, D = xs.shape
    tm = _tile(Lp, 528)

    def body(x_ref, g_ref, h_ref):
        x = x_ref[...]
        rstd = lax.rsqrt(jnp.mean(x * x, axis=-1, keepdims=True) + EPS)
        h_ref[...] = (x * rstd * g_ref[...]).astype(out_dtype)

    return pl.pallas_call(
        body, name=name, grid=(Lp // tm,),
        in_specs=[pl.BlockSpec((tm, D), lambda i: (i, 0)), pl.BlockSpec((1, D), lambda i: (0, 0))],
        out_specs=pl.BlockSpec((tm, D), lambda i: (i, 0)),
        out_shape=jax.ShapeDtypeStruct((Lp, D), out_dtype),
        compiler_params=_cparams(("parallel",)),
    )(xs, g)


def _rms_bwd(dY, dh, xs, g, pad, name):
    Lp, D = xs.shape
    tm = _tile(Lp, 352)

    def body(dY_ref, dh_ref, x_ref, g_ref, dxs_ref, dg_ref, half_ref):
        i = pl.program_id(0)

        @pl.when(i == 0)
        def _():
            dg_ref[...] = jnp.zeros_like(dg_ref)

        x = x_ref[...]
        rstd = lax.rsqrt(jnp.mean(x * x, axis=-1, keepdims=True) + EPS)
        xhat = x * rstd
        dh_ = dh_ref[...]
        dg_ref[...] += jnp.sum(dh_ * xhat, axis=0, keepdims=True)
        dxh = dh_ * g_ref[...]
        dx = rstd * (dxh - xhat * jnp.mean(dxh * xhat, axis=-1, keepdims=True))
        out = jnp.where(_row_ids(i, tm) >= pad, dY_ref[...] + dx, 0.0)
        dxs_ref[...] = out
        half_ref[...] = (0.5 * out).astype(bf16)

    row = pl.BlockSpec((tm, D), lambda i: (i, 0))
    vec = pl.BlockSpec((1, D), lambda i: (0, 0))
    return pl.pallas_call(
        body, name=name, grid=(Lp // tm,),
        in_specs=[row, row, row, vec], out_specs=[row, vec, row],
        out_shape=[jax.ShapeDtypeStruct((Lp, D), f32), jax.ShapeDtypeStruct((1, D), f32), jax.ShapeDtypeStruct((Lp, D), bf16)],
        compiler_params=_cparams(("arbitrary",)),
    )(dY, dh, xs, g)


def _mm(a, b, mode, out_dtype, name, tm=512, tn=512, tk=512, residual=None, after=None):
    if mode == "nn":
        (M, K), N = a.shape, b.shape[1]
    elif mode == "nt":
        (M, K), N = a.shape, b.shape[0]
    else:
        (K, M), N = a.shape, b.shape[1]
    tm = _tile(M, tm, 16 if mode != "tn" else LANES) if M > tm else M
    tn = _tile(N, tn, LANES) if N > tn else N
    tk = _tile(K, tk, LANES if mode != "tn" else 16) if K > tk else K
    nk = K // tk
    dims = {"nn": NN, "nt": NT, "tn": TN}[mode]

    def body(*refs):
        if residual is None:
            a_ref, b_ref, o_ref, acc = refs
            r_ref = None
        else:
            a_ref, b_ref, r_ref, o_ref, acc = refs
        k = pl.program_id(2)

        @pl.when(k == 0)
        def _():
            acc[...] = jnp.zeros_like(acc)

        acc[...] += _dot(a_ref[...], b_ref[...], dims)

        @pl.when(k == nk - 1)
        def _():
            r = acc[...]
            if r_ref is not None:
                r = r + r_ref[...]
            o_ref[...] = r.astype(out_dtype)

    a_spec = pl.BlockSpec((tk, tm), lambda i, j, k: (k, i)) if mode == "tn" else pl.BlockSpec((tm, tk), lambda i, j, k: (i, k))
    b_spec = pl.BlockSpec((tn, tk), lambda i, j, k: (j, k)) if mode == "nt" else pl.BlockSpec((tk, tn), lambda i, j, k: (k, j))
    o_spec = pl.BlockSpec((tm, tn), lambda i, j, k: (i, j))
    in_specs = [a_spec, b_spec] + ([o_spec] if residual is not None else [])
    args = [a, b] + ([residual] if residual is not None else [])
    body, in_specs, args = _ordered(body, in_specs, args, after)
    return pl.pallas_call(
        body, name=name, grid=(M // tm, N // tn, nk),
        in_specs=in_specs, out_specs=o_spec,
        out_shape=jax.ShapeDtypeStruct((M, N), out_dtype),
        scratch_shapes=[pltpu.VMEM((tm, tn), f32)],
        compiler_params=_cparams(("parallel", "parallel", "arbitrary"), VMEM_LIMIT_MB),
    )(*args)


def _ffn_fwd(xs, g, wg, wu, wd, name):
    Lp, D = xs.shape
    nd, Fs, _ = wg.shape
    tm = _tile(Lp, 704)
    once = pl.Buffered(1)

    def body(x_ref, g_ref, wg_ref, wu_ref, wd_ref, out_ref, h_ref, G_ref, U_ref, hs, acc):
        j = pl.program_id(1)

        @pl.when(j == 0)
        def _():
            x = x_ref[...]
            rstd = lax.rsqrt(jnp.mean(x * x, axis=-1, keepdims=True) + EPS)
            h = (x * rstd * g_ref[...]).astype(bf16)
            hs[...] = h
            h_ref[...] = h
            acc[...] = jnp.zeros_like(acc)

        h = hs[...]
        G = _dot(h, wg_ref[0], NT)
        U = _dot(h, wu_ref[0], NT)
        G_ref[0] = G.astype(bf16)
        U_ref[0] = U.astype(bf16)
        A = (G * _sigmoid(G) * U).astype(bf16)
        acc[...] += _dot(A, wd_ref[0])

        @pl.when(j == nd - 1)
        def _():
            out_ref[...] = x_ref[...] + 0.5 * acc[...]

    row_f = pl.BlockSpec((tm, D), lambda i, j: (i, 0), pipeline_mode=once)
    act = pl.BlockSpec((1, tm, Fs), lambda i, j: (j, i, 0))
    return pl.pallas_call(
        body, name=name, grid=(Lp // tm, nd),
        in_specs=[row_f, pl.BlockSpec((1, D), lambda i, j: (0, 0)),
                  pl.BlockSpec((1, Fs, D), lambda i, j: (j, 0, 0)),
                  pl.BlockSpec((1, Fs, D), lambda i, j: (j, 0, 0)),
                  pl.BlockSpec((1, Fs, D), lambda i, j: (j, 0, 0))],
        out_specs=[row_f, pl.BlockSpec((tm, D), lambda i, j: (i, 0), pipeline_mode=once), act, act],
        out_shape=[jax.ShapeDtypeStruct((Lp, D), f32), jax.ShapeDtypeStruct((Lp, D), bf16),
                   jax.ShapeDtypeStruct((nd, Lp, Fs), bf16), jax.ShapeDtypeStruct((nd, Lp, Fs), bf16)],
        scratch_shapes=[pltpu.VMEM((tm, D), bf16), pltpu.VMEM((tm, D), f32)],
        compiler_params=_cparams(("parallel", "arbitrary"), VMEM_LIMIT_MB),
    )(xs, g, wg, wu, wd)


def _ffn_bwd_act(dyh, wd, G, U, name, after=None):
    Lp, D = dyh.shape
    nd, Fs, _ = wd.shape
    tm = _tile(Lp, 704)

    def body(dyh_ref, wd_ref, G_ref, U_ref, dG_ref, dU_ref, A_ref):
        dA = _dot(dyh_ref[...], wd_ref[0], NT)
        Gf = G_ref[0].astype(f32)
        Uf = U_ref[0].astype(f32)
        s = _sigmoid(Gf)
        silu = Gf * s
        dG_ref[0] = (dA * Uf * (s * (1.0 + Gf * (1.0 - s)))).astype(bf16)
        dU_ref[0] = (dA * silu).astype(bf16)
        A_ref[0] = (silu * Uf).astype(bf16)

    act = pl.BlockSpec((1, tm, Fs), lambda j, i: (j, i, 0))
    act_s = jax.ShapeDtypeStruct((nd, Lp, Fs), bf16)
    in_specs = [pl.BlockSpec((tm, D), lambda j, i: (i, 0)), pl.BlockSpec((1, Fs, D), lambda j, i: (j, 0, 0)), act, act]
    body, in_specs, args = _ordered(body, in_specs, [dyh, wd, G, U], after)
    return pl.pallas_call(
        body, name=name, grid=(nd, Lp // tm),
        in_specs=in_specs, out_specs=[act, act, act], out_shape=[act_s, act_s, act_s],
        compiler_params=_cparams(("parallel", "parallel"), VMEM_LIMIT_MB),
    )(*args)


def _ffn_bwd_dh(dG, dU, wg, wu, name, after=None):
    nd, Lp, Fs = dG.shape
    D = wg.shape[2]
    tm = _tile(Lp, 1056)

    def body(dG_ref, dU_ref, wg_ref, wu_ref, dh_ref, acc):
        j = pl.program_id(1)

        @pl.when(j == 0)
        def _():
            acc[...] = jnp.zeros_like(acc)

        acc[...] += _dot(dG_ref[0], wg_ref[0]) + _dot(dU_ref[0], wu_ref[0])

        @pl.when(j == nd - 1)
        def _():
            dh_ref[...] = acc[...]

    act = pl.BlockSpec((1, tm, Fs), lambda i, j: (j, i, 0))
    wrow = pl.BlockSpec((1, Fs, D), lambda i, j: (j, 0, 0))
    body, in_specs, args = _ordered(body, [act, act, wrow, wrow], [dG, dU, wg, wu], after)
    return pl.pallas_call(
        body, name=name, grid=(Lp // tm, nd),
        in_specs=in_specs,
        out_specs=pl.BlockSpec((tm, D), lambda i, j: (i, 0), pipeline_mode=pl.Buffered(1)),
        out_shape=jax.ShapeDtypeStruct((Lp, D), f32),
        scratch_shapes=[pltpu.VMEM((tm, D), f32)],
        compiler_params=_cparams(("parallel", "arbitrary"), VMEM_LIMIT_MB),
    )(*args)


def _ffn_bwd_wgrad(act, rows, name, after=None):
    nd, Lp, Fs = act.shape
    D = rows.shape[1]

    def body(a_ref, r_ref, o_ref):
        o_ref[0] = _dot(a_ref[0], r_ref[...], TN).astype(bf16)

    in_specs = [pl.BlockSpec((1, Lp, Fs), lambda j: (j, 0, 0)),
                pl.BlockSpec((Lp, D), lambda j: (0, 0), pipeline_mode=pl.Buffered(1))]
    body, in_specs, args = _ordered(body, in_specs, [act, rows], after)
    return pl.pallas_call(
        body, name=name, grid=(nd,),
        in_specs=in_specs, out_specs=pl.BlockSpec((1, Fs, D), lambda j: (j, 0, 0)),
        out_shape=jax.ShapeDtypeStruct((nd, Fs, D), bf16),
        compiler_params=_cparams(("parallel",), VMEM_LIMIT_MB),
    )(*args)


def _gate_fwd(proj, wlr, blr, pad, gate_blk, name):
    Lp = proj.shape[0]
    DK = wlr.shape[1]
    tm = _tile(Lp, 528)

    def body(lr_ref, w_ref, b_ref, lg_ref):
        z = _dot(lr_ref[...].astype(bf16), w_ref[...].astype(bf16)) + b_ref[...]
        ls = jnp.minimum(z, 0.0) - jnp.log(1.0 + jnp.exp(-jnp.abs(z)))
        lg_ref[...] = jnp.where(_row_ids(pl.program_id(0), tm) >= pad, ls * (1.0 / GATE_NORM), 0.0)

    return pl.pallas_call(
        body, name=name, grid=(Lp // tm,),
        in_specs=[pl.BlockSpec((tm, GATE_PAD), lambda i: (i, gate_blk)),
                  pl.BlockSpec((GATE_PAD, DK), lambda i: (0, 0)), pl.BlockSpec((1, DK), lambda i: (0, 0))],
        out_specs=pl.BlockSpec((tm, DK), lambda i: (i, 0)),
        out_shape=jax.ShapeDtypeStruct((Lp, DK), f32),
        compiler_params=_cparams(("parallel",)),
    )(proj, wlr, blr)


def _gate_bwd(dlg, proj, wlr, blr, pad, gate_blk, name):
    Lp = proj.shape[0]
    DK = wlr.shape[1]
    tm = _tile(Lp, 528)

    def body(dlg_ref, lr_ref, w_ref, b_ref, dlr_ref, dw_ref, db_ref):
        i = pl.program_id(0)

        @pl.when(i == 0)
        def _():
            dw_ref[...] = jnp.zeros_like(dw_ref)
            db_ref[...] = jnp.zeros_like(db_ref)

        lr = lr_ref[...].astype(bf16)
        w = w_ref[...].astype(bf16)
        z = _dot(lr, w) + b_ref[...]
        dz = jnp.where(_row_ids(i, tm) >= pad, dlg_ref[...] * _sigmoid(-z) * (1.0 / GATE_NORM), 0.0)
        dzb = dz.astype(bf16)
        dlr_ref[...] = _dot(dzb, w, NT).astype(bf16)
        dw_ref[...] += _dot(lr, dzb, TN)
        db_ref[...] += jnp.sum(dz, axis=0, keepdims=True)

    return pl.pallas_call(
        body, name=name, grid=(Lp // tm,),
        in_specs=[pl.BlockSpec((tm, DK), lambda i: (i, 0)), pl.BlockSpec((tm, GATE_PAD), lambda i: (i, gate_blk)),
                  pl.BlockSpec((GATE_PAD, DK), lambda i: (0, 0)), pl.BlockSpec((1, DK), lambda i: (0, 0))],
        out_specs=[pl.BlockSpec((tm, GATE_PAD), lambda i: (i, 0)), pl.BlockSpec((GATE_PAD, DK), lambda i: (0, 0)),
                   pl.BlockSpec((1, DK), lambda i: (0, 0))],
        out_shape=[jax.ShapeDtypeStruct((Lp, GATE_PAD), bf16), jax.ShapeDtypeStruct((GATE_PAD, DK), f32),
                   jax.ShapeDtypeStruct((1, DK), f32)],
        compiler_params=_cparams(("arbitrary",)),
    )(dlg, proj, wlr, blr)


def _chunk_decay(lg):
    C = lg.shape[0]
    r = lax.broadcasted_iota(jnp.int32, (C, C), 0)
    c = lax.broadcasted_iota(jnp.int32, (C, C), 1)
    return _dot(jnp.where(r >= c, 1.0, 0.0).astype(f32), lg, NN, HI)


def _col(v):
    return jnp.transpose(jnp.broadcast_to(v, (8, v.shape[1])))[:, 0:1]


def _intra_scores(q, k, b, A_ref):
    C = q.shape[0]
    S = GLA_SUB
    A_ref[...] = jnp.zeros_like(A_ref)
    ri = lax.broadcasted_iota(jnp.int32, (S, 1), 0)
    for I in range(C // S):
        lo = S * I
        qI, bI = q[lo:lo + S], b[lo:lo + S]
        if I > 0:
            bref = b[lo - 1:lo]
            qs = qI * jnp.exp(bI - bref)
            ks = k[:lo] * jnp.exp(bref - b[:lo])
            A_ref[lo:lo + S, 0:lo] = _dot(qs, ks, NT, HI)
        for jj in range(S):
            j = lo + jj
            P = jnp.exp(jnp.minimum(bI - b[j:j + 1], 0.0))
            a = jnp.sum(qI * P * k[j:j + 1], axis=1, keepdims=True)
            A_ref[lo:lo + S, j:j + 1] = jnp.where(ri >= jj, a, 0.0)


def _intra_grads(q, k, b, dA, dq_ref, dk_ref):
    C = q.shape[0]
    S = GLA_SUB
    ri = lax.broadcasted_iota(jnp.int32, (S, 1), 0)
    for I in range(C // S):
        lo = S * I
        qI, bI = q[lo:lo + S], b[lo:lo + S]
        dqI = jnp.zeros_like(qI)
        if I > 0:
            bref = b[lo - 1:lo]
            eq = jnp.exp(bI - bref)
            ek = jnp.exp(bref - b[:lo])
            qs = qI * eq
            ks = k[:lo] * ek
            dAI = dA[lo:lo + S, 0:lo]
            dqI = dqI + _dot(dAI, ks, NN, HI) * eq
            dk_ref[0:lo, :] += _dot(dAI, qs, TN, HI) * ek
        for jj in range(S):
            j = lo + jj
            P = jnp.exp(jnp.minimum(bI - b[j:j + 1], 0.0))
            t = jnp.where(ri >= jj, dA[lo:lo + S, j:j + 1], 0.0) * P
            dqI = dqI + t * k[j:j + 1]
            dk_ref[j:j + 1, :] += jnp.sum(t * qI, axis=0, keepdims=True)
        dq_ref[lo:lo + S, :] += dqI


def _gla_fwd(proj, lg, hnw, H, name):
    Lp = proj.shape[0]
    DK = lg.shape[1]
    hk = DK // H
    hv = hnw.shape[1]
    DV = hv * H
    C = GLA_CHUNK
    NC = Lp // C
    scale = float(hk) ** -0.5
    kq, kv = DK // hk, (2 * DK) // hv
    kr = kv + H

    def body(q_ref, k_ref, v_ref, r_ref, lg_ref, w_ref, o_ref, y_ref, s_ref, S_scr, A_scr):
        c = pl.program_id(1)

        @pl.when(c == 0)
        def _():
            S_scr[...] = jnp.zeros_like(S_scr)

        q = q_ref[...] * scale
        k = k_ref[...]
        v = v_ref[...]
        b = _chunk_decay(lg_ref[...])
        bl = b[C - 1:C]
        S = S_scr[...]
        s_ref[0, 0] = S
        _intra_scores(q, k, b, A_scr)
        vb = v.astype(bf16)
        o = _dot((q * jnp.exp(b)).astype(bf16), S.astype(bf16)) + _dot(A_scr[...].astype(bf16), vb)
        kb = (k * jnp.exp(bl - b)).astype(bf16)
        S_scr[...] = jnp.exp(_col(bl)) * S + _dot(kb, vb, TN)
        o_ref[...] = o
        on = o * lax.rsqrt(jnp.mean(o * o, axis=-1, keepdims=True) + EPS) * w_ref[...]
        r = r_ref[...]
        y_ref[...] = (on * (r * _sigmoid(r))).astype(bf16)

    return pl.pallas_call(
        body, name=name, grid=(H, NC),
        in_specs=[pl.BlockSpec((C, hk), lambda h, c: (c, h)),
                  pl.BlockSpec((C, hk), lambda h, c: (c, kq + h)),
                  pl.BlockSpec((C, hv), lambda h, c: (c, kv + h)),
                  pl.BlockSpec((C, hv), lambda h, c: (c, kr + h)),
                  pl.BlockSpec((C, hk), lambda h, c: (c, h)),
                  pl.BlockSpec((1, hv), lambda h, c: (0, 0))],
        out_specs=[pl.BlockSpec((C, hv), lambda h, c: (c, h)), pl.BlockSpec((C, hv), lambda h, c: (c, h)),
                   pl.BlockSpec((1, 1, hk, hv), lambda h, c: (h, c, 0, 0))],
        out_shape=[jax.ShapeDtypeStruct((Lp, DV), f32), jax.ShapeDtypeStruct((Lp, DV), bf16),
                   jax.ShapeDtypeStruct((H, NC, hk, hv), f32)],
        scratch_shapes=[pltpu.VMEM((hk, hv), f32), pltpu.VMEM((C, C), f32)],
        compiler_params=_cparams(("parallel", "arbitrary")),
    )(proj, proj, proj, proj, lg, hnw)


def _gla_bwd(dy, proj, lg, o, states, hnw, H, pad, name, after=None):
    Lp = proj.shape[0]
    DK = lg.shape[1]
    hk = DK // H
    hv = hnw.shape[1]
    DV = hv * H
    C = GLA_CHUNK
    NC = Lp // C
    scale = float(hk) ** -0.5
    kq, kv = DK // hk, (2 * DK) // hv
    kr = kv + H

    def body(dy_ref, q_ref, k_ref, v_ref, r_ref, lg_ref, o_ref, s_ref, sn_ref, w_ref,
             dq_ref, dk_ref, dv_ref, dr_ref, dlg_ref, dw_ref, dS_scr, A_scr, dq_s, dk_s):
        h = pl.program_id(0)
        cc = pl.program_id(1)
        c = NC - 1 - cc

        @pl.when(cc == 0)
        def _():
            dS_scr[...] = jnp.zeros_like(dS_scr)

        @pl.when((cc == 0) & (h == 0))
        def _():
            dw_ref[...] = jnp.zeros_like(dw_ref)

        keep = (c * C + lax.broadcasted_iota(jnp.int32, (C, 1), 0)) >= pad
        w = w_ref[...]
        o_ = o_ref[...]
        rs = lax.rsqrt(jnp.mean(o_ * o_, axis=-1, keepdims=True) + EPS)
        ohat = o_ * rs
        r = r_ref[...]
        sg = _sigmoid(r)
        dy_ = dy_ref[...]
        d_on = dy_ * (r * sg)
        dr_ref[...] = jnp.where(keep, dy_ * (ohat * w) * (sg * (1.0 + r * (1.0 - sg))), 0.0).astype(bf16)
        dw_ref[...] += jnp.sum(d_on * ohat, axis=0, keepdims=True)
        d_oh = d_on * w
        do = rs * (d_oh - ohat * jnp.mean(d_oh * ohat, axis=-1, keepdims=True))
        dob = do.astype(bf16)
        q = q_ref[...] * scale
        k = k_ref[...]
        v = v_ref[...]
        vb = v.astype(bf16)
        b = _chunk_decay(lg_ref[...])
        bl = b[C - 1:C]
        eb = jnp.exp(b)
        ekb = jnp.exp(bl - b)
        S = s_ref[0, 0]
        dS = dS_scr[...]
        dSb = dS.astype(bf16)
        _intra_scores(q, k, b, A_scr)
        ri = lax.broadcasted_iota(jnp.int32, (C, C), 0)
        ci = lax.broadcasted_iota(jnp.int32, (C, C), 1)
        dA = jnp.where(ri >= ci, _dot(dob, vb, NT), 0.0)
        kb = (k * ekb).astype(bf16)
        qb = (q * eb).astype(bf16)
        dv = _dot(A_scr[...].astype(bf16), dob, TN) + _dot(kb, dSb)
        dq_s[...] = _dot(dob, S.astype(bf16), NT) * eb
        dk_s[...] = _dot(vb, dSb, NT) * ekb
        dS_scr[...] = _dot(qb, dob, TN) + jnp.exp(_col(bl)) * dS
        _intra_grads(q, k, b, dA, dq_s, dk_s)
        dq = dq_s[...]
        dk = dk_s[...]
        Dm = q * dq - k * dk
        after = _dot(jnp.ones((8, hv), f32), sn_ref[0, 0] * dS, NT, HI)[0:1]
        dlg = _dot(jnp.where(ri <= ci, 1.0, 0.0).astype(f32), Dm, NN, HI) + after
        dlg_ref[...] = jnp.where(keep, dlg, 0.0)
        dq_ref[...] = jnp.where(keep, dq * scale, 0.0).astype(bf16)
        dk_ref[...] = jnp.where(keep, dk, 0.0).astype(bf16)
        dv_ref[...] = jnp.where(keep, dv, 0.0).astype(bf16)

    rev = lambda h, cc: NC - 1 - cc
    in_specs = [pl.BlockSpec((C, hv), lambda h, cc: (rev(h, cc), h)),
                pl.BlockSpec((C, hk), lambda h, cc: (rev(h, cc), h)),
                pl.BlockSpec((C, hk), lambda h, cc: (rev(h, cc), kq + h)),
                pl.BlockSpec((C, hv), lambda h, cc: (rev(h, cc), kv + h)),
                pl.BlockSpec((C, hv), lambda h, cc: (rev(h, cc), kr + h)),
                pl.BlockSpec((C, hk), lambda h, cc: (rev(h, cc), h)),
                pl.BlockSpec((C, hv), lambda h, cc: (rev(h, cc), h)),
                pl.BlockSpec((1, 1, hk, hv), lambda h, cc: (h, rev(h, cc), 0, 0)),
                pl.BlockSpec((1, 1, hk, hv), lambda h, cc: (h, jnp.minimum(rev(h, cc) + 1, NC - 1), 0, 0)),
                pl.BlockSpec((1, hv), lambda h, cc: (0, 0))]
    body, in_specs, args = _ordered(body, in_specs, [dy, proj, proj, proj, proj, lg, o, states, states, hnw], after)
    return pl.pallas_call(
        body, name=name, grid=(H, NC),
        in_specs=in_specs,
        out_specs=[pl.BlockSpec((C, hk), lambda h, cc: (rev(h, cc), h)),
                   pl.BlockSpec((C, hk), lambda h, cc: (rev(h, cc), h)),
                   pl.BlockSpec((C, hv), lambda h, cc: (rev(h, cc), h)),
                   pl.BlockSpec((C, hv), lambda h, cc: (rev(h, cc), h)),
                   pl.BlockSpec((C, hk), lambda h, cc: (rev(h, cc), h)),
                   pl.BlockSpec((1, hv), lambda h, cc: (0, 0))],
        out_shape=[jax.ShapeDtypeStruct((Lp, DK), bf16), jax.ShapeDtypeStruct((Lp, DK), bf16),
                   jax.ShapeDtypeStruct((Lp, DV), bf16), jax.ShapeDtypeStruct((Lp, DV), bf16),
                   jax.ShapeDtypeStruct((Lp, DK), f32), jax.ShapeDtypeStruct((1, hv), f32)],
        scratch_shapes=[pltpu.VMEM((hk, hv), f32), pltpu.VMEM((C, C), f32),
                        pltpu.VMEM((C, hk), f32), pltpu.VMEM((C, hk), f32)],
        compiler_params=_cparams(("arbitrary", "arbitrary")),
    )(*args)


def _window_sums(x, back):
    n = x.shape[0]
    out = []
    s = x
    for w in (1, 2, 4, 8):
        s = s + pltpu.roll(s, w if back else n - w, 0)
        out.append(s)
    return out


def _pool_windows(hn, pad, n_real, name):
    Lp, D = hn.shape
    GW = D // POOL_GROUPS
    cb = min(GW, 256)
    per = GW // cb

    def body(h_ref, p_ref):
        g = pl.program_id(0) // per
        x = h_ref[...]
        s2, s4, s8, s16 = _window_sums(x, True)
        sel = jnp.where(g == 0, s2, jnp.where(g == 1, s4, jnp.where(g == 2, s8, s16)))
        win = jnp.left_shift(2, g).astype(f32)
        rows = lax.broadcasted_iota(jnp.int32, (Lp, 1), 0)
        t = (rows - pad).astype(f32)
        cnt = jnp.minimum(jnp.maximum(t, 0.0) + 1.0, win)
        p_ref[...] = jnp.where(rows >= pad, sel / cnt - x, 0.0).astype(bf16)

    return pl.pallas_call(
        body, name=name, grid=(D // cb,),
        in_specs=[pl.BlockSpec((Lp, cb), lambda i: (0, i))],
        out_specs=pl.BlockSpec((Lp, cb), lambda i: (0, i)),
        out_shape=jax.ShapeDtypeStruct((Lp, D), bf16),
        compiler_params=_cparams(("parallel",)),
    )(hn)


def _pool_windows_bwd(dp, pad, name):
    Lp, D = dp.shape
    GW = D // POOL_GROUPS
    cb = min(GW, 256)
    per = GW // cb

    def body(dp_ref, dh_ref):
        g = pl.program_id(0) // per
        rows = lax.broadcasted_iota(jnp.int32, (Lp, 1), 0)
        d = jnp.where(rows >= pad, dp_ref[...], 0.0)
        win = jnp.left_shift(2, g).astype(f32)
        t = (rows - pad).astype(f32)
        cnt = jnp.minimum(jnp.maximum(t, 0.0) + 1.0, win)
        s2, s4, s8, s16 = _window_sums(d / cnt, False)
        sel = jnp.where(g == 0, s2, jnp.where(g == 1, s4, jnp.where(g == 2, s8, s16)))
        dh_ref[...] = jnp.where(rows >= pad, sel - d, 0.0)

    return pl.pallas_call(
        body, name=name, grid=(D // cb,),
        in_specs=[pl.BlockSpec((Lp, cb), lambda i: (0, i))],
        out_specs=pl.BlockSpec((Lp, cb), lambda i: (0, i)),
        out_shape=jax.ShapeDtypeStruct((Lp, D), f32),
        compiler_params=_cparams(("parallel",)),
    )(dp)


def _pool_mix_fwd(xs, pooled, w, bias, scale, pad, name):
    Lp, D = xs.shape
    GW = D // POOL_GROUPS
    tm = _tile(Lp, 1056)

    def body(x_ref, p_ref, w_ref, b_ref, s_ref, o_ref):
        z = _dot(p_ref[...], w_ref[0]) + b_ref[...]
        keep = _row_ids(pl.program_id(1), tm) >= pad
        o_ref[...] = x_ref[...] + jnp.where(keep, z * s_ref[...], 0.0)

    blk = pl.BlockSpec((tm, GW), lambda g, i: (i, g))
    vec = pl.BlockSpec((1, GW), lambda g, i: (0, g))
    return pl.pallas_call(
        body, name=name, grid=(POOL_GROUPS, Lp // tm),
        in_specs=[blk, blk, pl.BlockSpec((1, GW, GW), lambda g, i: (g, 0, 0)), vec, vec],
        out_specs=blk, out_shape=jax.ShapeDtypeStruct((Lp, D), f32),
        compiler_params=_cparams(("parallel", "parallel")),
    )(xs, pooled, w, bias, scale)


def _pool_mix_bwd(dY, pooled, w, bias, scale, pad, name, after=None):
    Lp, D = dY.shape
    GW = D // POOL_GROUPS
    tm = _tile(Lp, 1056)
    nm = Lp // tm

    def body(dY_ref, p_ref, w_ref, b_ref, s_ref, dp_ref, dw_ref, db_ref, ds_ref, acc):
        i = pl.program_id(1)

        @pl.when(i == 0)
        def _():
            acc[...] = jnp.zeros_like(acc)
            db_ref[...] = jnp.zeros_like(db_ref)
            ds_ref[...] = jnp.zeros_like(ds_ref)

        keep = _row_ids(i, tm) >= pad
        dY_ = jnp.where(keep, dY_ref[...], 0.0)
        p = p_ref[...]
        z = _dot(p, w_ref[0]) + b_ref[...]
        ds_ref[...] += jnp.sum(dY_ * z, axis=0, keepdims=True)
        dz = dY_ * s_ref[...]
        db_ref[...] += jnp.sum(dz, axis=0, keepdims=True)
        dzb = dz.astype(bf16)
        acc[...] += _dot(p, dzb, TN)
        dp_ref[...] = _dot(dzb, w_ref[0], NT)

        @pl.when(i == nm - 1)
        def _():
            dw_ref[0] = acc[...].astype(bf16)

    blk = pl.BlockSpec((tm, GW), lambda g, i: (i, g))
    vec = pl.BlockSpec((1, GW), lambda g, i: (0, g))
    wsp = pl.BlockSpec((1, GW, GW), lambda g, i: (g, 0, 0))
    body, in_specs, args = _ordered(body, [blk, blk, wsp, vec, vec], [dY, pooled, w, bias, scale], after)
    return pl.pallas_call(
        body, name=name, grid=(POOL_GROUPS, nm),
        in_specs=in_specs, out_specs=[blk, wsp, vec, vec],
        out_shape=[jax.ShapeDtypeStruct((Lp, D), f32), jax.ShapeDtypeStruct((POOL_GROUPS, GW, GW), bf16),
                   jax.ShapeDtypeStruct((1, D), f32), jax.ShapeDtypeStruct((1, D), f32)],
        scratch_shapes=[pltpu.VMEM((GW, GW), f32)],
        compiler_params=_cparams(("parallel", "arbitrary")),
    )(*args)


def _loss_head(xs, target, g, first, name):
    Lp, D = xs.shape
    tm = GLA_CHUNK
    off = first // tm

    def body(x_ref, t_ref, g_ref, loss_ref, dxs_ref, dg_ref, half_ref):
        i = pl.program_id(0)

        @pl.when(i == 0)
        def _():
            loss_ref[...] = jnp.zeros_like(loss_ref)
            dg_ref[...] = jnp.zeros_like(dg_ref)

        @pl.when(i < off)
        def _():
            dxs_ref[...] = jnp.zeros_like(dxs_ref)
            half_ref[...] = jnp.zeros_like(half_ref)

        @pl.when(i >= off)
        def _():
            x = x_ref[...]
            rstd = lax.rsqrt(jnp.mean(x * x, axis=-1, keepdims=True) + EPS)
            xhat = x * rstd
            gg = g_ref[...]
            err = xhat * gg - t_ref[...]
            loss_ref[...] += 0.5 * jnp.sum(jnp.mean(err * err, axis=-1, keepdims=True))
            dy = err * (1.0 / D)
            dg_ref[...] += jnp.sum(dy * xhat, axis=0, keepdims=True)
            dxh = dy * gg
            out = rstd * (dxh - xhat * jnp.mean(dxh * xhat, axis=-1, keepdims=True))
            dxs_ref[...] = out
            half_ref[...] = (0.5 * out).astype(bf16)

    row = pl.BlockSpec((tm, D), lambda i: (i, 0))
    return pl.pallas_call(
        body, name=name, grid=(Lp // tm,),
        in_specs=[row, pl.BlockSpec((tm, D), lambda i: (jnp.maximum(i - off, 0), 0)), pl.BlockSpec((1, D), lambda i: (0, 0))],
        out_specs=[pl.BlockSpec((8, LANES), lambda i: (0, 0)), row, pl.BlockSpec((1, D), lambda i: (0, 0)), row],
        out_shape=[jax.ShapeDtypeStruct((8, LANES), f32), jax.ShapeDtypeStruct((Lp, D), f32),
                   jax.ShapeDtypeStruct((1, D), f32), jax.ShapeDtypeStruct((Lp, D), bf16)],
        compiler_params=_cparams(("arbitrary",)),
    )(xs, target, g)


def _adam_math(w, g, m, v):
    m2 = ADAM_B1 * m + (1.0 - ADAM_B1) * g
    v2 = ADAM_B2 * v + (1.0 - ADAM_B2) * (g * g)
    m_hat = m2 / (1.0 - ADAM_B1 ** ADAM_STEP)
    v_hat = v2 / (1.0 - ADAM_B2 ** ADAM_STEP)
    delta = -ADAM_LR * (m_hat / (jnp.sqrt(v_hat) + ADAM_EPS) + ADAM_WD * w)
    return delta, m2, v2


def _adamw(w, m, v, unit, own, own_idx, recv, prev, name, after=None):
    U, R, C = w.shape
    tr, tc = _tile2(R, C, 256, 8 if own.dtype == f32 and recv is None else 16)
    n_recv = 0 if recv is None else recv.shape[0]

    def body(idx_ref, w_ref, m_ref, v_ref, own_ref, *rest):
        rest = list(rest)
        recv_refs = [rest.pop(0) for _ in range(n_recv)]
        if prev is not None:
            rest = rest[4:]
        g_ref, d_ref, m2_ref, v2_ref = rest
        g = own_ref[0].astype(f32)
        for r_ref in recv_refs:
            g = g + r_ref[0].astype(f32)
        delta, m2, v2 = _adam_math(w_ref[0], g, m_ref[0], v_ref[0])
        g_ref[0] = g
        d_ref[0] = delta
        m2_ref[0] = m2
        v2_ref[0] = v2

    blk = pl.BlockSpec((1, tr, tc), lambda i, j, idx: (unit, i, j))
    in_specs = [blk, blk, blk, pl.BlockSpec((1, tr, tc), lambda i, j, idx: (idx[0], i, j))]
    args = [w, m, v, own]
    for p in range(n_recv):
        in_specs.append(pl.BlockSpec((1, tr, tc), lambda i, j, idx, p=p: (p, i, j)))
        args.append(recv)
    aliases = {}
    if prev is not None:
        for t in range(4):
            aliases[1 + len(args) + t] = t
        in_specs += [ANY] * 4
        args += list(prev)
    body, in_specs, args = _ordered(body, in_specs, args, after, lead=1)
    out = jax.ShapeDtypeStruct((U, R, C), f32)
    return pl.pallas_call(
        body, name=name,
        grid_spec=pltpu.PrefetchScalarGridSpec(
            num_scalar_prefetch=1, grid=(R // tr, C // tc), in_specs=in_specs, out_specs=[blk] * 4),
        out_shape=[out] * 4, input_output_aliases=aliases,
        compiler_params=_cparams(("parallel", "parallel")),
    )(own_idx, *args)


def _place():
    return lax.axis_index("x"), lax.axis_index("y"), lax.axis_index("c")


HBM = pl.BlockSpec(memory_space=pltpu.HBM)
SEM = pl.BlockSpec(memory_space=pltpu.SEMAPHORE)
VMEM_SPEC = pl.BlockSpec(memory_space=pltpu.VMEM)
EFFECT = pltpu.SideEffectType.DATAFLOW_SIDE_EFFECTING
TOKEN = jax.ShapeDtypeStruct((8, LANES), f32)


def _hbm(x):
    return pltpu.with_memory_space_constraint(x, pltpu.HBM)


def _hbm_like(xs):
    return [pltpu.HBM(x.shape, x.dtype) for x in xs]


def _slot(px, py, pc):
    return 4 * px + 2 * py + pc


def _halves(ref):
    n = ref.shape[0]
    cut = n // 2 if n < 32 else (n // 2) // 16 * 16
    return ref.at[pl.ds(0, cut)], ref.at[pl.ds(cut, n - cut)]


def _gather_start(shards, after, name):
    n = len(shards)
    me = _slot(*_place())
    bufs = [lax.dynamic_update_slice(lax.empty((N_DEV,) + s.shape, s.dtype), s[None], (me,) + (0,) * s.ndim) for s in shards]

    def body(*refs):
        ins, land = refs[:n], refs[n:2 * n]
        send, recv = refs[2 * n + 1], refs[2 * n + 2]
        token = refs[-1]
        x, y, c = _place()
        to = [(x, y, 1 - c), (1 - x, y, c), (x, 1 - y, c)]
        for a in range(n):
            for k, dev in enumerate(to):
                pltpu.make_async_remote_copy(
                    src_ref=ins[a], dst_ref=land[a].at[_slot(x, y, c)], send_sem=send.at[3 * a + k], recv_sem=recv.at[3 * a + k],
                    device_id=dev, device_id_type=MESH).start()
        token[...] = jnp.zeros_like(token)

    out = pl.pallas_call(
        body, name=name,
        in_specs=[HBM] * (2 * n) + [ANY],
        out_specs=[SEM, SEM] + [HBM] * (2 * n) + [VMEM_SPEC],
        out_shape=[pltpu.SemaphoreType.DMA((3 * n,)), pltpu.SemaphoreType.DMA((3 * n,))] + _hbm_like(shards) + _hbm_like(bufs) + [TOKEN],
        input_output_aliases={i: 2 + i for i in range(2 * n)},
        compiler_params=pltpu.CompilerParams(has_side_effects=EFFECT),
    )(*[_hbm(s) for s in shards], *[_hbm(b) for b in bufs], after)
    return dict(send1=out[0], recv1=out[1], shards=list(out[2:2 + n]), bufs=list(out[2 + n:2 + 2 * n]), token=out[-1])


def _gather_mid(h, after, name):
    n = len(h["bufs"])

    def body(*refs):
        land, recv1 = refs[:n], refs[n]
        send2, recv2 = refs[n + 2], refs[n + 3]
        token = refs[-1]
        x, y, c = _place()
        nbr = [(1 - x, y, c), (x, 1 - y, c)]
        for j, dev in enumerate(nbr):
            for a in range(n):
                blk = land[a].at[_slot(*dev)]
                pltpu.make_async_remote_copy(
                    src_ref=blk, dst_ref=blk, send_sem=send2.at[4 * a + j], recv_sem=recv1.at[3 * a + 1 + j],
                    device_id=dev, device_id_type=MESH).wait_recv()
                pltpu.make_async_remote_copy(
                    src_ref=blk, dst_ref=blk, send_sem=send2.at[4 * a + j], recv_sem=recv2.at[4 * a + j],
                    device_id=(x, y, 1 - c), device_id_type=MESH).start()
        for a in range(n):
            from_x, from_y = land[a].at[_slot(*nbr[0])], land[a].at[_slot(*nbr[1])]
            for k, (half, dev) in enumerate([(_halves(from_y)[0], nbr[0]), (_halves(from_x)[1], nbr[1])]):
                pltpu.make_async_remote_copy(
                    src_ref=half, dst_ref=half, send_sem=send2.at[4 * a + 2 + k], recv_sem=recv2.at[4 * a + 2 + k],
                    device_id=dev, device_id_type=MESH).start()
        token[...] = jnp.zeros_like(token)

    out = pl.pallas_call(
        body, name=name,
        in_specs=[HBM] * n + [SEM, ANY],
        out_specs=[SEM, SEM] + [HBM] * n + [VMEM_SPEC],
        out_shape=[pltpu.SemaphoreType.DMA((4 * n,)), pltpu.SemaphoreType.DMA((4 * n,))] + _hbm_like(h["bufs"]) + [TOKEN],
        input_output_aliases={i: 2 + i for i in range(n)},
        compiler_params=pltpu.CompilerParams(has_side_effects=EFFECT),
    )(*h["bufs"], h["recv1"], after)
    h.update(send2=out[0], recv2=out[1], bufs=list(out[2:2 + n]), token=out[-1])
    return h


def _gather_mid2(h, after, name):
    n = len(h["bufs"])

    def body(*refs):
        land, recv2 = refs[:n], refs[n]
        send3, recv3 = refs[n + 2], refs[n + 3]
        token = refs[-1]
        x, y, c = _place()
        for a in range(n):
            blk = land[a].at[_slot(1 - x, 1 - y, c)]
            for k, half in enumerate(_halves(blk)):
                pltpu.make_async_remote_copy(
                    src_ref=half, dst_ref=half, send_sem=send3.at[a], recv_sem=recv2.at[4 * a + 2 + k],
                    device_id=(x, y, 1 - c), device_id_type=MESH).wait_recv()
            pltpu.make_async_remote_copy(
                src_ref=blk, dst_ref=blk, send_sem=send3.at[a], recv_sem=recv3.at[a],
                device_id=(x, y, 1 - c), device_id_type=MESH).start()
        token[...] = jnp.zeros_like(token)

    out = pl.pallas_call(
        body, name=name,
        in_specs=[HBM] * n + [SEM, ANY],
        out_specs=[SEM, SEM] + [HBM] * n + [VMEM_SPEC],
        out_shape=[pltpu.SemaphoreType.DMA((n,)), pltpu.SemaphoreType.DMA((n,))] + _hbm_like(h["bufs"]) + [TOKEN],
        input_output_aliases={i: 2 + i for i in range(n)},
        compiler_params=pltpu.CompilerParams(has_side_effects=EFFECT),
    )(*h["bufs"], h["recv2"], after)
    h.update(send3=out[0], recv3=out[1], bufs=list(out[2:2 + n]), token=out[-1])
    return h


def _gather_end(h, after, name):
    n = len(h["bufs"])

    def body(*refs):
        ins, land = refs[:n], refs[n:2 * n]
        send1, recv1, send2, recv2, send3, recv3 = refs[2 * n:2 * n + 6]
        x, y, c = _place()
        sib = (x, y, 1 - c)
        nbr = [(1 - x, y), (x, 1 - y)]

        def wait(src, dst, ssem, rsem, send):
            cp = pltpu.make_async_remote_copy(src_ref=src, dst_ref=dst, send_sem=ssem, recv_sem=rsem, device_id=sib, device_id_type=MESH)
            cp.wait_send() if send else cp.wait_recv()

        for a in range(n):
            mine = land[a].at[_slot(x, y, c)]
            for k in range(3):
                wait(ins[a], mine, send1.at[3 * a + k], recv1.at[3 * a + k], True)
            wait(ins[a], land[a].at[_slot(x, y, 1 - c)], send1.at[3 * a], recv1.at[3 * a], False)
            for j, (px, py) in enumerate(nbr):
                sent = land[a].at[_slot(px, py, c)]
                wait(sent, sent, send2.at[4 * a + j], recv2.at[4 * a + j], True)
                wait(sent, land[a].at[_slot(px, py, 1 - c)], send2.at[4 * a + j], recv2.at[4 * a + j], False)
            halves = [_halves(land[a].at[_slot(*nbr[1], c)])[0], _halves(land[a].at[_slot(*nbr[0], c)])[1]]
            for k, half in enumerate(halves):
                wait(half, half, send2.at[4 * a + 2 + k], recv2.at[4 * a + 2 + k], True)
            diag = land[a].at[_slot(1 - x, 1 - y, c)]
            wait(diag, diag, send3.at[a], recv3.at[a], True)
            wait(diag, land[a].at[_slot(1 - x, 1 - y, 1 - c)], send3.at[a], recv3.at[a], False)

    out = pl.pallas_call(
        body, name=name,
        in_specs=[HBM] * (2 * n) + [SEM] * 6 + [ANY],
        out_specs=[HBM] * n,
        out_shape=_hbm_like(h["bufs"]),
        input_output_aliases={n + i: i for i in range(n)},
        compiler_params=pltpu.CompilerParams(has_side_effects=EFFECT),
    )(*h["shards"], *h["bufs"], h["send1"], h["recv1"], h["send2"], h["recv2"], h["send3"], h["recv3"], after)
    return list(out)


def _peer_plan(kind, x, y, c):
    if kind == "pair":
        return [(2 * q + (1 - c), q, (x, y, 1 - c)) for q in range(4)]
    chips = [(1 - x, y), (x, 1 - y), (1 - x, 1 - y)]
    return [(2 * px + py, k, (px, py, c)) for k, (px, py) in enumerate(chips)]


def _exchange_start(kind, srcs, after, name):
    n = len(srcs)
    K = 4 if kind == "pair" else 3
    lands = [_hbm(lax.empty((K,) + s.shape[1:], s.dtype)) for s in srcs]

    def body(*refs):
        ins, land = refs[:n], refs[n:2 * n]
        send, recv = refs[2 * n + 1], refs[2 * n + 2]
        token = refs[-1]
        for a in range(n):
            for k, (si, di, dev) in enumerate(_peer_plan(kind, *_place())):
                pltpu.make_async_remote_copy(
                    src_ref=ins[a].at[si], dst_ref=land[a].at[di], send_sem=send.at[K * a + k], recv_sem=recv.at[K * a + k],
                    device_id=dev, device_id_type=MESH).start()
        token[...] = jnp.zeros_like(token)

    out = pl.pallas_call(
        body, name=name,
        in_specs=[HBM] * (2 * n) + [ANY],
        out_specs=[SEM, SEM] + [HBM] * (2 * n) + [VMEM_SPEC],
        out_shape=[pltpu.SemaphoreType.DMA((K * n,)), pltpu.SemaphoreType.DMA((K * n,))] + _hbm_like(srcs) + _hbm_like(lands) + [TOKEN],
        input_output_aliases={i: 2 + i for i in range(2 * n)},
        compiler_params=pltpu.CompilerParams(has_side_effects=EFFECT),
    )(*[_hbm(s) for s in srcs], *lands, after)
    return dict(kind=kind, send=out[0], recv=out[1], srcs=list(out[2:2 + n]), lands=list(out[2 + n:2 + 2 * n]), token=out[-1])


def _exchange_wait(h, after, name):
    n = len(h["srcs"])
    kind = h["kind"]
    K = 4 if kind == "pair" else 3

    def body(*refs):
        ins, land = refs[:n], refs[n:2 * n]
        send, recv = refs[2 * n], refs[2 * n + 1]
        for a in range(n):
            for k, (si, di, dev) in enumerate(_peer_plan(kind, *_place())):
                cp = pltpu.make_async_remote_copy(
                    src_ref=ins[a].at[si], dst_ref=land[a].at[di], send_sem=send.at[K * a + k], recv_sem=recv.at[K * a + k],
                    device_id=dev, device_id_type=MESH)
                cp.wait_send()
                cp.wait_recv()

    out = pl.pallas_call(
        body, name=name,
        in_specs=[HBM] * (2 * n) + [SEM, SEM, ANY],
        out_specs=[HBM] * (2 * n),
        out_shape=_hbm_like(h["srcs"]) + _hbm_like(h["lands"]),
        input_output_aliases={i: i for i in range(2 * n)},
        compiler_params=pltpu.CompilerParams(has_side_effects=EFFECT),
    )(*h["srcs"], *h["lands"], h["send"], h["recv"], after)
    return list(out[:n]), list(out[n:])


def _pair_add(g, got, c_idx, name):
    _, R, C = g.shape
    tr, tc = _tile2(R, C, 512, 16)

    def body(c_ref, a_ref, b_ref, o_ref):
        o_ref[0] = (a_ref[0].astype(f32) + b_ref[0].astype(f32)).astype(o_ref.dtype)

    return pl.pallas_call(
        body, name=name,
        grid_spec=pltpu.PrefetchScalarGridSpec(
            num_scalar_prefetch=1, grid=(4, R // tr, C // tc),
            in_specs=[pl.BlockSpec((1, tr, tc), lambda q, i, j, c: (2 * q + c[0], i, j)),
                      pl.BlockSpec((1, tr, tc), lambda q, i, j, c: (q, i, j))],
            out_specs=pl.BlockSpec((1, tr, tc), lambda q, i, j, c: (q, i, j))),
        out_shape=jax.ShapeDtypeStruct((4, R, C), g.dtype),
        compiler_params=_cparams(("parallel", "parallel", "parallel")),
    )(c_idx, g, got)


def _small_exchange(send, gather, name, after=None):
    R = send.shape[-2]

    def body(in_ref, out_ref, send_sems, recv_sems):
        x, y, c = _place()
        me = 4 * x + 2 * y + c
        out_ref[me] = in_ref[...] if gather else in_ref[me]
        cps = []
        for k in range(1, N_DEV):
            px, py, pc = x ^ ((k >> 2) & 1), y ^ ((k >> 1) & 1), c ^ (k & 1)
            src = in_ref if gather else in_ref.at[4 * px + 2 * py + pc]
            cps.append(pltpu.make_async_remote_copy(
                src_ref=src, dst_ref=out_ref.at[me],
                send_sem=send_sems.at[k - 1], recv_sem=recv_sems.at[k - 1],
                device_id=(px, py, pc), device_id_type=MESH))
        for cp in cps:
            cp.start()
        for cp in cps:
            cp.wait()

    body, in_specs, args = _ordered(body, [pl.BlockSpec(memory_space=pltpu.VMEM)], [send], after)
    return pl.pallas_call(
        body, name=name,
        in_specs=in_specs, out_specs=pl.BlockSpec(memory_space=pltpu.VMEM),
        out_shape=jax.ShapeDtypeStruct((N_DEV, R, LANES), f32),
        scratch_shapes=[pltpu.SemaphoreType.DMA((N_DEV - 1,)), pltpu.SemaphoreType.DMA((N_DEV - 1,))],
    )(*args)


def _sum_blocks(blocks, name):
    def body(in_ref, o_ref):
        s = in_ref[0]
        for d in range(1, N_DEV):
            s = s + in_ref[d]
        o_ref[0] = s

    return pl.pallas_call(body, name=name, out_shape=jax.ShapeDtypeStruct((1,) + blocks.shape[1:], f32))(blocks)


def _rows(n):
    return -(-n // LANES)


def _pack(arrs, total_rows):
    parts = []
    for a in arrs:
        flat = a.reshape(-1).astype(f32)
        parts.append(jnp.pad(flat, (0, _rows(flat.size) * LANES - flat.size)))
    flat = jnp.concatenate(parts)
    return jnp.pad(flat, (0, total_rows * LANES - flat.size)).reshape(total_rows, LANES)


def _unpack(packed, shapes):
    lead = packed.shape[:-2]
    flat = packed.reshape(lead + (-1,))
    out, pos = [], 0
    for s in shapes:
        n = 1
        for d in s:
            n *= d
        out.append(flat[..., pos:pos + n].reshape(lead + tuple(s)))
        pos += _rows(n) * LANES
    return out


def _to_shards(full, axis):
    s = full.shape
    return jnp.moveaxis(full.reshape(s[:axis] + (N_DEV, s[axis] // N_DEV) + s[axis + 1:]), axis, 0)


def _from_shards(sh, axis):
    m = jnp.moveaxis(sh, 0, axis)
    s = m.shape
    return m.reshape(s[:axis] + (s[axis] * s[axis + 1],) + s[axis + 2:])


def kernel(x, meta, ffn_norm, ffn_w_gate, ffn_w_up, ffn_w_down, gla_norm, gla_w_in, gla_w_lr, gla_b_lr, gla_head_norm, gla_w_out, pool_norm, pool_w, pool_b, pool_scale, final_norm, loss_target, m_meta, m_ffn_norm, m_ffn_w_gate, m_ffn_w_up, m_ffn_w_down, m_gla_norm, m_gla_w_in, m_gla_w_lr, m_gla_b_lr, m_gla_head_norm, m_gla_w_out, m_pool_norm, m_pool_w, m_pool_b, m_pool_scale, m_final_norm, v_meta, v_ffn_norm, v_ffn_w_gate, v_ffn_w_up, v_ffn_w_down, v_gla_norm, v_gla_w_in, v_gla_w_lr, v_gla_b_lr, v_gla_head_norm, v_gla_w_out, v_pool_norm, v_pool_w, v_pool_b, v_pool_scale, v_final_norm):
    H = GLA_HEADS
    _, SEQ, D = x.shape
    Fs = ffn_w_gate.shape[-1]
    DK, DV = D // 2, D
    hv = DV // H
    GW = D // POOL_GROUPS
    INW = 2 * DK + 2 * DV + GATE_RANK
    NPK = 2 * DK + 2 * DV + GATE_PAD
    pad = (-N_META) % GLA_CHUNK
    first = pad + N_META
    Lp = first + SEQ
    n_units = ffn_w_gate.shape[0] * ffn_w_gate.shape[1]
    assert first % GLA_CHUNK == 0 and Lp % GLA_CHUNK == 0 and pad >= POOL_GROUPS * 4

    px, py, pc = _place()
    c_idx = jnp.reshape(pc, (1,)).astype(jnp.int32)
    q_idx = jnp.reshape(2 * px + py, (1,)).astype(jnp.int32)
    zero_idx = jnp.zeros((1,), jnp.int32)

    small_sh = [meta, ffn_norm, gla_w_lr, pool_norm, pool_b, pool_scale]
    small_axis = [1, 2, 2, 1, 2, 1]
    sh_shapes = [a.shape for a in small_sh]
    sh_rows = -(-sum(_rows(a.size) for a in small_sh) // 8) * 8
    gathered = _small_exchange(_pack(small_sh, sh_rows), True, "small_gather")
    meta_f, ffn_norm_f, wlr_f, pool_norm_f, pool_b_f, pool_scale_f = [
        _from_shards(a, ax) for a, ax in zip(_unpack(gathered, sh_shapes), small_axis)]
    ffn_norm_f = ffn_norm_f.reshape(n_units, 1, D)
    wlr128 = jnp.pad(wlr_f[0], ((0, GATE_PAD - GATE_RANK), (0, 0)))

    def t_units(w):
        return jnp.swapaxes(w, -1, -2).reshape(n_units, Fs, D)

    wd_units = ffn_w_down.reshape(n_units, Fs, D)
    ffn_shards = [[t_units(ffn_w_gate)[u].astype(bf16), t_units(ffn_w_up)[u].astype(bf16), wd_units[u].astype(bf16)]
                  for u in range(n_units)]
    mixer_shards = [gla_w_in[0].T.astype(bf16), gla_w_out[0].astype(bf16), pool_w[0].astype(bf16)]
    gather_order = [("ffn0", ffn_shards[0]), ("mixers", mixer_shards)] + [(f"ffn{u}", ffn_shards[u]) for u in range(1, n_units)]
    c_lr = 2 * DK + DV
    c_r = 2 * DK + 2 * DV
    gate_blk = c_r // GATE_PAD

    def gather_begin(i, after):
        tag, shards = gather_order[i]
        return _gather_start(shards, after, f"gather_start_{tag}")

    def gather_next(i, h, after):
        tag = gather_order[i][0]
        h = _gather_mid(h, after, f"gather_mid_{tag}")
        nxt = gather_begin(i + 1, h["token"]) if i + 1 < len(gather_order) else None
        h = _gather_mid2(h, h["token"] if nxt is None else nxt["token"], f"gather_mid2_{tag}")
        done = _gather_end(h, h["token"], f"gather_end_{tag}")
        return done, nxt

    xs = jnp.concatenate([jnp.zeros((pad, D), f32), meta_f, x[0]], axis=0)
    saved = {}
    ffn_w = [None] * n_units

    def ffn_f(u, xs):
        out, h, G, U = _ffn_fwd(xs, ffn_norm_f[u], *ffn_w[u], name=f"ffn_fwd{u}")
        saved[("ffn", u)] = (xs, h, G, U)
        return out

    def gla_f(xs, win_p, wout_full):
        hn = _rms_fwd(xs, gla_norm, bf16, "gla_norm_fwd")
        proj = _mm(hn, win_p, "nt", f32, "gla_proj", tm=1056, tn=896, tk=2048)
        lg = _gate_fwd(proj, wlr128, gla_b_lr, pad, gate_blk, "gla_gate_fwd")
        o, y, states = _gla_fwd(proj, lg, gla_head_norm, H, "gla_core_fwd")
        out = _mm(y, wout_full, "nn", f32, "gla_out", tm=1056, tn=512, tk=2048, residual=xs)
        saved["gla"] = (xs, hn, proj, lg, o, y, states)
        return out

    def pool_f(xs, wpool_full):
        hn = _rms_fwd(xs, pool_norm_f, f32, "pool_norm_fwd")
        pooled = _pool_windows(hn, pad, Lp - pad, "pool_windows_fwd")
        out = _pool_mix_fwd(xs, pooled, wpool_full, pool_b_f.reshape(1, D), pool_scale_f, pad, "pool_mix_fwd")
        saved["pool"] = (xs, pooled)
        return out

    depth = ffn_w_gate.shape[0]
    assert depth == 2 and n_units == 4
    h = gather_begin(0, gathered)
    ffn_w[0], h = gather_next(0, h, h["token"])
    xs = ffn_f(0, xs)
    (win_g, wout_g, wpool_g), h = gather_next(1, h, xs)
    win_full = win_g.reshape(INW, D)
    win_p = jnp.concatenate([win_full[:c_lr], win_full[c_lr + GATE_RANK:], win_full[c_lr:c_lr + GATE_RANK],
                             jnp.zeros((GATE_PAD - GATE_RANK, D), bf16)], axis=0)
    wout_full = wout_g.reshape(DV, D)
    wpool_full = _from_shards(wpool_g, 1)
    xs = gla_f(xs, win_p, wout_full)
    ffn_w[1], h = gather_next(2, h, xs)
    xs = ffn_f(1, xs)
    ffn_w[2], h = gather_next(3, h, xs)
    xs = ffn_f(2, xs)
    xs = pool_f(xs, wpool_full)
    ffn_w[3], h = gather_next(4, h, xs)
    xs = ffn_f(3, xs)
    loss_part, dxs, d_final, dyh = _loss_head(xs, loss_target[0], final_norm.reshape(1, D), first, "loss_head")

    class Reduce:
        def __init__(self, tag, grads, after=None):
            self.tag = tag
            self.h = _exchange_start("pair", grads, grads[0] if after is None else after, f"pair_start_{tag}")
            self.token = self.h["token"]

        def mid(self, after):
            grads, got = _exchange_wait(self.h, after, f"pair_wait_{self.tag}")
            self.sums = [_pair_add(g, r, c_idx, f"pair_add_{self.tag}{a}") for a, (g, r) in enumerate(zip(grads, got))]
            self.h = _exchange_start("chips", self.sums, self.sums[-1], f"chips_start_{self.tag}")
            self.token = self.h["token"]

        def end(self, after):
            sums, recv = _exchange_wait(self.h, after, f"chips_wait_{self.tag}")
            return list(zip(sums, recv))

    d_ffn_norm = [None] * n_units
    small_grads = {}

    def ffn_b(u, dY, dyh, prev):
        xs_in, h_, G, U = saved[("ffn", u)]
        wg, wu, wd = ffn_w[u]
        tok = None if prev is None else prev.token
        dG, dU, A = _ffn_bwd_act(dyh, wd, G, U, f"ffn_act{u}", after=tok)
        dh = _ffn_bwd_dh(dG, dU, wg, wu, f"ffn_dh{u}")
        dxs, dg, dyh_next = _rms_bwd(dY, dh, xs_in, ffn_norm_f[u], pad, f"ffn_norm_bwd{u}")
        if prev is not None:
            prev.mid(dxs)
            tok = prev.token
        dwg = _ffn_bwd_wgrad(dG, h_, f"ffn_wgrad_gate{u}", after=tok)
        dwu = _ffn_bwd_wgrad(dU, h_, f"ffn_wgrad_up{u}", after=tok)
        dwd = _ffn_bwd_wgrad(A, dyh, f"ffn_wgrad_down{u}", after=tok)
        d_ffn_norm[u] = dg
        return dxs, dyh_next, Reduce(f"ffn{u}", [dwg, dwu, dwd])

    def gla_b(dY, prev):
        xs_in, hn, proj, lg, o, y, states = saved["gla"]
        dyb = dY.astype(bf16)
        dy = _mm(dyb, wout_full, "nt", f32, "gla_out_dgrad", tm=1056, tn=512, tk=2048, after=prev.token)
        dwout = _mm(y, dyb, "tn", bf16, "gla_out_wgrad", tm=1024, tn=1024, tk=528, after=prev.token)
        prev.mid(dwout)
        dq, dk, dv, dr, dlg, dhw = _gla_bwd(dy, proj, lg, o, states, gla_head_norm, H, pad, "gla_core_bwd", after=prev.token)
        dlr, dwlr, dblr = _gate_bwd(dlg, proj, wlr128, gla_b_lr, pad, gate_blk, "gla_gate_bwd")
        dproj = jnp.concatenate([dq, dk, dv, dr, dlr], axis=1)
        dwin_p = _mm(dproj, hn, "tn", bf16, "gla_proj_wgrad", tm=896, tn=1024, tk=528)
        dhn = _mm(dproj, win_p, "nn", f32, "gla_proj_dgrad", tm=1056, tn=512, tk=896)
        dxs, dgn, dyh_next = _rms_bwd(dY, dhn, xs_in, gla_norm, pad, "gla_norm_bwd")
        dwin = jnp.concatenate([dwin_p[:c_lr], dwin_p[c_r:c_r + GATE_RANK], dwin_p[c_lr:c_r]], axis=0)
        small_grads.update(gla_w_lr=dwlr[:GATE_RANK][None], gla_b_lr=dblr, gla_head_norm=dhw, gla_norm=dgn)
        return dxs, dyh_next, Reduce("gla", [dwin.reshape(N_DEV, INW // N_DEV, D), dwout.reshape(N_DEV, DV // N_DEV, D)])

    def pool_b_(dY, prev):
        xs_in, pooled = saved["pool"]
        dp, dw, db, ds = _pool_mix_bwd(dY, pooled, wpool_full, pool_b_f.reshape(1, D), pool_scale_f, pad, "pool_mix_bwd",
                                       after=prev.token)
        dhn = _pool_windows_bwd(dp, pad, "pool_windows_bwd")
        dxs, dgn, dyh_next = _rms_bwd(dY, dhn, xs_in, pool_norm_f, pad, "pool_norm_bwd")
        prev.mid(dxs)
        dws = _to_shards(dw, 1)
        small_grads.update(pool_b=db.reshape(1, POOL_GROUPS, GW), pool_scale=ds, pool_norm=dgn)
        return dxs, dyh_next, Reduce("pool", [dws.reshape(N_DEV, POOL_GROUPS * GW // N_DEV, GW)], after=prev.token)

    sh_names = ["meta", "ffn_norm", "gla_w_lr", "pool_norm", "pool_b", "pool_scale"]
    rep_names = ["gla_norm", "gla_b_lr", "gla_head_norm", "final_norm"]
    rep_w = [gla_norm, gla_b_lr, gla_head_norm, final_norm]
    rep_shapes = [a.shape for a in rep_w]
    rep_rows = -(-sum(_rows(a.size) for a in rep_w) // 8) * 8

    def small_path(dxs0):
        small_grads.update(meta=dxs0[pad:first], ffn_norm=jnp.concatenate(d_ffn_norm, axis=0).reshape(n_units // 2, 2, D),
                           final_norm=d_final.reshape(D))
        by_owner = [_to_shards(small_grads[nm].reshape(full_shape), ax) for nm, full_shape, ax in zip(
            sh_names, [meta_f.shape, (ffn_norm.shape[0], 2, D), wlr_f.shape, pool_norm_f.shape, pool_b_f.shape, pool_scale_f.shape],
            small_axis)]
        rep_pack = _pack([small_grads[nm].reshape(s) for nm, s in zip(rep_names, rep_shapes)], rep_rows)
        send = jnp.stack([
            jnp.concatenate([_pack([g[d] for g in by_owner], sh_rows), rep_pack, loss_part], axis=0) for d in range(N_DEV)])
        total = _sum_blocks(_small_exchange(send, False, "small_reduce"), "small_sum")
        n_small = sh_rows + rep_rows

        def pack_small(sh_list, rep_list):
            return jnp.concatenate([_pack(sh_list, sh_rows), _pack(rep_list, rep_rows)], axis=0)[None]

        w_small = pack_small(small_sh, rep_w)
        m_small = pack_small([m_meta, m_ffn_norm, m_gla_w_lr, m_pool_norm, m_pool_b, m_pool_scale],
                             [m_gla_norm, m_gla_b_lr, m_gla_head_norm, m_final_norm])
        v_small = pack_small([v_meta, v_ffn_norm, v_gla_w_lr, v_pool_norm, v_pool_b, v_pool_scale],
                             [v_gla_norm, v_gla_b_lr, v_gla_head_norm, v_final_norm])
        small_out = _adamw(w_small, m_small, v_small, 0, total[:, :n_small], zero_idx, None, None, "adamw_small")
        small_res = {}
        for kind, packed in zip(("grad", "delta", "new_m", "new_v"), small_out):
            sh_vals = _unpack(packed[0, :sh_rows], sh_shapes)
            rep_vals = _unpack(packed[0, sh_rows:], rep_shapes)
            for nm, val in zip(sh_names + rep_names, sh_vals + rep_vals):
                small_res[(kind, nm)] = val
        return total[0, n_small, 0], small_res, small_out[0]

    def ffn_b_last(dY, dyh, prev):
        xs_in, h_, G, U = saved[("ffn", 0)]
        wg, wu, wd = ffn_w[0]
        dG, dU, A = _ffn_bwd_act(dyh, wd, G, U, "ffn_act0", after=prev.token)
        dwd = _ffn_bwd_wgrad(A, dyh, "ffn_wgrad_down0", after=prev.token)
        r_d = Reduce("ffn0_down", [dwd])
        dh = _ffn_bwd_dh(dG, dU, wg, wu, "ffn_dh0", after=r_d.token)
        dxs, dg, _ = _rms_bwd(dY, dh, xs_in, ffn_norm_f[0], pad, "ffn_norm_bwd0")
        d_ffn_norm[0] = dg
        small = small_path(dxs)
        prev.mid(small[2])
        r_d.mid(prev.token)
        dwg = _ffn_bwd_wgrad(dG, h_, "ffn_wgrad_gate0", after=r_d.token)
        r_g = Reduce("ffn0_gate", [dwg])
        dwu = _ffn_bwd_wgrad(dU, h_, "ffn_wgrad_up0", after=r_g.token)
        r_g.mid(dwu)
        r_u = Reduce("ffn0_up", [dwu], after=r_g.token)
        return dxs, small, (r_g, r_u, r_d)

    dxs, dyh, r3 = ffn_b(3, dxs, dyh, None)
    dxs, dyh, rp = pool_b_(dxs, r3)
    dxs, dyh, r2 = ffn_b(2, dxs, dyh, rp)
    dxs, dyh, r1 = ffn_b(1, dxs, dyh, r2)
    dxs, dyh, rg = gla_b(dxs, r1)
    dxs, (loss, small_res, _), r0 = ffn_b_last(dxs, dyh, rg)
    grad_x = dxs[first:].reshape(x.shape)
    r_last = r0[1]

    big_res = {}

    def adam_one(nm, w, m, v, entry, transposed=False):
        sums, recv = entry
        R, C = sums.shape[1:]
        w1, m1, v1 = ((t[0].T if transposed else t).reshape(1, R, C) for t in (w, m, v))
        out = _adamw(w1, m1, v1, 0, sums, q_idx, recv, None, f"adamw_{nm}", after=r_last.token)
        for kind, val in zip(("grad", "delta", "new_m", "new_v"), out):
            big_res[(kind, nm)] = val[0].T[None] if transposed else val.reshape(w.shape)
        return out[0]

    e_gla = rg.end(dxs)
    done = adam_one("gla_w_in", gla_w_in, m_gla_w_in, v_gla_w_in, e_gla[0], transposed=True)
    done = adam_one("gla_w_out", gla_w_out, m_gla_w_out, v_gla_w_out, e_gla[1])
    done = adam_one("pool_w", pool_w, m_pool_w, v_pool_w, rp.end(done)[0])
    r_last.mid(done)

    ffn_names = ["ffn_w_gate", "ffn_w_up", "ffn_w_down"]
    ffn_wmv = [tuple(t_units(t) for t in (ffn_w_gate, m_ffn_w_gate, v_ffn_w_gate)),
               tuple(t_units(t) for t in (ffn_w_up, m_ffn_w_up, v_ffn_w_up)),
               tuple(t.reshape(n_units, Fs, D) for t in (ffn_w_down, m_ffn_w_down, v_ffn_w_down))]
    ffn_prev = [[lax.empty((n_units, Fs, D), f32) for _ in range(4)] for _ in range(3)]
    order_after = r_last.token
    for u, red in ((3, r3), (2, r2), (1, r1), (0, r0)):
        entries = [r.end(done)[0] for r in red] if u == 0 else red.end(done)
        for a in range(3):
            sums, recv = entries[a]
            ffn_prev[a] = _adamw(*ffn_wmv[a], u, sums, q_idx, recv, ffn_prev[a], f"adamw_{ffn_names[a]}{u}", after=order_after)
            done = order_after = ffn_prev[a][0]
    for a in range(3):
        for kind, val in zip(("grad", "delta", "new_m", "new_v"), ffn_prev[a]):
            val = val.reshape(ffn_w_down.shape)
            big_res[(kind, ffn_names[a])] = val if a == 2 else jnp.swapaxes(val, -1, -2)

    order = ["meta", "ffn_norm", "ffn_w_gate", "ffn_w_up", "ffn_w_down", "gla_norm", "gla_w_in", "gla_w_lr", "gla_b_lr",
             "gla_head_norm", "gla_w_out", "pool_norm", "pool_w", "pool_b", "pool_scale", "final_norm"]
    res = {**small_res, **big_res}
    outs = [loss, grad_x]
    for kind in ("grad", "delta", "new_m", "new_v"):
        outs += [res[(kind, nm)] for nm in order]
    return tuple(outs)
```

```python
import functools

import jax
import jax.numpy as jnp
from jax import lax
from jax.experimental import pallas as pl
from jax.experimental.pallas import tpu as pltpu

f32 = jnp.float32
bf16 = jnp.bfloat16

N_DEV = 8
N_META = 16
GLA_HEADS = 4
GLA_CHUNK = 64
GLA_SUB = 16
GATE_RANK = 16
GATE_PAD = 128
GATE_NORM = 16.0
EPS = 1e-6
POOL_GROUPS = 4
ADAM_LR = 0.001
ADAM_B1 = 0.9
ADAM_B2 = 0.999
ADAM_EPS = 1e-08
ADAM_WD = 0.01
ADAM_STEP = 10
LANES = 128
VMEM_LIMIT_MB = 56

NN = (((1,), (0,)), ((), ()))
NT = (((1,), (1,)), ((), ()))
TN = (((0,), (0,)), ((), ()))
HI = lax.Precision.HIGHEST
MESH = pl.DeviceIdType.MESH
ANY = pl.BlockSpec(memory_space=pl.ANY)


def _cparams(sem=None, vmem_mb=None):
    kw = {}
    if sem is not None:
        kw["dimension_semantics"] = sem
    if vmem_mb is not None:
        kw["vmem_limit_bytes"] = vmem_mb * 2 ** 20
    return pltpu.CompilerParams(**kw)


def _tile(n, target, mult=16):
    best = None
    for t in range(mult, min(n, target) + 1, mult):
        if n % t == 0:
            best = t
    assert best is not None, (n, target, mult)
    return best


def _tile2(R, C, rows, mult):
    if R % mult == 0:
        return _tile(R, rows, mult), C
    return R, _tile(C, 256, LANES)


def _dot(a, b, dims=NN, precision=None):
    return lax.dot_general(a, b, dims, preferred_element_type=f32, precision=precision)


def _sigmoid(x):
    return 1.0 / (1.0 + jnp.exp(-x))


def _row_ids(tile_index, tm):
    return tile_index * tm + lax.broadcasted_iota(jnp.int32, (tm, 1), 0)


def _ordered(body, in_specs, args, after, lead=0):
    if after is None:
        return body, in_specs, args
    pos = lead + len(args)

    def body_without(*refs):
        return body(*refs[:pos], *refs[pos + 1:])

    return body_without, list(in_specs) + [ANY], list(args) + [after]


def _cast_unit(w, unit, name, after=None):
    _, R, C = w.shape
    tr, tc = _tile2(R, C, 256, 16)

    def body(w_ref, o_ref):
        o_ref[...] = w_ref[0].astype(bf16)

    body, in_specs, args = _ordered(body, [pl.BlockSpec((1, tr, tc), lambda i, j: (unit, i, j))], [w], after)
    return pl.pallas_call(
        body, name=name, grid=(R // tr, C // tc),
        in_specs=in_specs, out_specs=pl.BlockSpec((tr, tc), lambda i, j: (i, j)),
        out_shape=jax.ShapeDtypeStruct((R, C), bf16),
        compiler_params=_cparams(("parallel", "parallel")),
    )(*args)


def _rms_fwd(xs, g, out_dtype, name):
    Lp, D = xs.shape
    tm = _tile(Lp, 528)

    def body(x_ref, g_ref, h_ref):
        x = x_ref[...]
        rstd = lax.rsqrt(jnp.mean(x * x, axis=-1, keepdims=True) + EPS)
        h_ref[...] = (x * rstd * g_ref[...]).astype(out_dtype)

    return pl.pallas_call(
        body, name=name, grid=(Lp // tm,),
        in_specs=[pl.BlockSpec((tm, D), lambda i: (i, 0)), pl.BlockSpec((1, D), lambda i: (0, 0))],
        out_specs=pl.BlockSpec((tm, D), lambda i: (i, 0)),
        out_shape=jax.ShapeDtypeStruct((Lp, D), out_dtype),
        compiler_params=_cparams(("parallel",)),
    )(xs, g)


def _rms_bwd(dY, dh, xs, g, pad, name):
    Lp, D = xs.shape
    tm = _tile(Lp, 352)

    def body(dY_ref, dh_ref, x_ref, g_ref, dxs_ref, dg_ref, half_ref):
        i = pl.program_id(0)

        @pl.when(i == 0)
        def _():
            dg_ref[...] = jnp.zeros_like(dg_ref)

        x = x_ref[...]
        rstd = lax.rsqrt(jnp.mean(x * x, axis=-1, keepdims=True) + EPS)
        xhat = x * rstd
        dh_ = dh_ref[...]
        dg_ref[...] += jnp.sum(dh_ * xhat, axis=0, keepdims=True)
        dxh = dh_ * g_ref[...]
        dx = rstd * (dxh - xhat * jnp.mean(dxh * xhat, axis=-1, keepdims=True))
        out = jnp.where(_row_ids(i, tm) >= pad, dY_ref[...] + dx, 0.0)
        dxs_ref[...] = out
        half_ref[...] = (0.5 * out).astype(bf16)

    row = pl.BlockSpec((tm, D), lambda i: (i, 0))
    vec = pl.BlockSpec((1, D), lambda i: (0, 0))
    return pl.pallas_call(
        body, name=name, grid=(Lp // tm,),
        in_specs=[row, row, row, vec], out_specs=[row, vec, row],
        out_shape=[jax.ShapeDtypeStruct((Lp, D), f32), jax.ShapeDtypeStruct((1, D), f32), jax.ShapeDtypeStruct((Lp, D), bf16)],
        compiler_params=_cparams(("arbitrary",)),
    )(dY, dh, xs, g)


def _mm(a, b, mode, out_dtype, name, tm=512, tn=512, tk=512, residual=None, after=None):
    if mode == "nn":
        (M, K), N = a.shape, b.shape[1]
    elif mode == "nt":
        (M, K), N = a.shape, b.shape[0]
    else:
        (K, M), N = a.shape, b.shape[1]
    tm = _tile(M, tm, 16 if mode != "tn" else LANES) if M > tm else M
    tn = _tile(N, tn, LANES) if N > tn else N
    tk = _tile(K, tk, LANES if mode != "tn" else 16) if K > tk else K
    nk = K // tk
    dims = {"nn": NN, "nt": NT, "tn": TN}[mode]

    def body(*refs):
        if residual is None:
            a_ref, b_ref, o_ref, acc = refs
            r_ref = None
        else:
            a_ref, b_ref, r_ref, o_ref, acc = refs
        k = pl.program_id(2)

        @pl.when(k == 0)
        def _():
            acc[...] = jnp.zeros_like(acc)

        acc[...] += _dot(a_ref[...], b_ref[...], dims)

        @pl.when(k == nk - 1)
        def _():
            r = acc[...]
            if r_ref is not None:
                r = r + r_ref[...]
            o_ref[...] = r.astype(out_dtype)

    a_spec = pl.BlockSpec((tk, tm), lambda i, j, k: (k, i)) if mode == "tn" else pl.BlockSpec((tm, tk), lambda i, j, k: (i, k))
    b_spec = pl.BlockSpec((tn, tk), lambda i, j, k: (j, k)) if mode == "nt" else pl.BlockSpec((tk, tn), lambda i, j, k: (k, j))
    o_spec = pl.BlockSpec((tm, tn), lambda i, j, k: (i, j))
    in_specs = [a_spec, b_spec] + ([o_spec] if residual is not None else [])
    args = [a, b] + ([residual] if residual is not None else [])
    body, in_specs, args = _ordered(body, in_specs, args, after)
    return pl.pallas_call(
        body, name=name, grid=(M // tm, N // tn, nk),
        in_specs=in_specs, out_specs=o_spec,
        out_shape=jax.ShapeDtypeStruct((M, N), out_dtype),
        scratch_shapes=[pltpu.VMEM((tm, tn), f32)],
        compiler_params=_cparams(("parallel", "parallel", "arbitrary"), VMEM_LIMIT_MB),
    )(*args)


def _mm_tn_full(a, b, name, tm, after=None):
    K, M = a.shape
    N = b.shape[1]
    tm = _tile(M, tm, LANES)

    def body(a_ref, b_ref, o_ref):
        o_ref[...] = _dot(a_ref[...], b_ref[...], TN).astype(bf16)

    in_specs = [pl.BlockSpec((K, tm), lambda i: (0, i)), pl.BlockSpec((K, N), lambda i: (0, 0), pipeline_mode=pl.Buffered(1))]
    body, in_specs, args = _ordered(body, in_specs, [a, b], after)
    return pl.pallas_call(
        body, name=name, grid=(M // tm,),
        in_specs=in_specs, out_specs=pl.BlockSpec((tm, N), lambda i: (i, 0)),
        out_shape=jax.ShapeDtypeStruct((M, N), bf16),
        compiler_params=_cparams(("parallel",), VMEM_LIMIT_MB),
    )(*args)


def _ffn_fwd(xs, g, wg, wu, wd, name):
    Lp, D = xs.shape
    nd, Fs, _ = wg.shape
    tm = _tile(Lp, 704)
    once = pl.Buffered(1)

    def body(x_ref, g_ref, wg_ref, wu_ref, wd_ref, out_ref, h_ref, G_ref, U_ref, hs, acc):
        j = pl.program_id(1)

        @pl.when(j == 0)
        def _():
            x = x_ref[...]
            rstd = lax.rsqrt(jnp.mean(x * x, axis=-1, keepdims=True) + EPS)
            h = (x * rstd * g_ref[...]).astype(bf16)
            hs[...] = h
            h_ref[...] = h
            acc[...] = jnp.zeros_like(acc)

        h = hs[...]
        G = _dot(h, wg_ref[0], NT)
        U = _dot(h, wu_ref[0], NT)
        G_ref[0] = G.astype(bf16)
        U_ref[0] = U.astype(bf16)
        A = (G * _sigmoid(G) * U).astype(bf16)
        acc[...] += _dot(A, wd_ref[0])

        @pl.when(j == nd - 1)
        def _():
            out_ref[...] = x_ref[...] + 0.5 * acc[...]

    row_f = pl.BlockSpec((tm, D), lambda i, j: (i, 0), pipeline_mode=once)
    act = pl.BlockSpec((1, tm, Fs), lambda i, j: (j, i, 0))
    return pl.pallas_call(
        body, name=name, grid=(Lp // tm, nd),
        in_specs=[row_f, pl.BlockSpec((1, D), lambda i, j: (0, 0)),
                  pl.BlockSpec((1, Fs, D), lambda i, j: (j, 0, 0)),
                  pl.BlockSpec((1, Fs, D), lambda i, j: (j, 0, 0)),
                  pl.BlockSpec((1, Fs, D), lambda i, j: (j, 0, 0))],
        out_specs=[row_f, pl.BlockSpec((tm, D), lambda i, j: (i, 0), pipeline_mode=once), act, act],
        out_shape=[jax.ShapeDtypeStruct((Lp, D), f32), jax.ShapeDtypeStruct((Lp, D), bf16),
                   jax.ShapeDtypeStruct((nd, Lp, Fs), bf16), jax.ShapeDtypeStruct((nd, Lp, Fs), bf16)],
        scratch_shapes=[pltpu.VMEM((tm, D), bf16), pltpu.VMEM((tm, D), f32)],
        compiler_params=_cparams(("parallel", "arbitrary"), VMEM_LIMIT_MB),
    )(xs, g, wg, wu, wd)


def _ffn_bwd_act(dyh, wd, G, U, name, after=None):
    Lp, D = dyh.shape
    nd, Fs, _ = wd.shape
    tm = _tile(Lp, 704)

    def body(dyh_ref, wd_ref, G_ref, U_ref, dG_ref, dU_ref, A_ref):
        dA = _dot(dyh_ref[...], wd_ref[0], NT)
        Gf = G_ref[0].astype(f32)
        Uf = U_ref[0].astype(f32)
        s = _sigmoid(Gf)
        silu = Gf * s
        dG_ref[0] = (dA * Uf * (s * (1.0 + Gf * (1.0 - s)))).astype(bf16)
        dU_ref[0] = (dA * silu).astype(bf16)
        A_ref[0] = (silu * Uf).astype(bf16)

    act = pl.BlockSpec((1, tm, Fs), lambda j, i: (j, i, 0))
    act_s = jax.ShapeDtypeStruct((nd, Lp, Fs), bf16)
    in_specs = [pl.BlockSpec((tm, D), lambda j, i: (i, 0)), pl.BlockSpec((1, Fs, D), lambda j, i: (j, 0, 0)), act, act]
    body, in_specs, args = _ordered(body, in_specs, [dyh, wd, G, U], after)
    return pl.pallas_call(
        body, name=name, grid=(nd, Lp // tm),
        in_specs=in_specs, out_specs=[act, act, act], out_shape=[act_s, act_s, act_s],
        compiler_params=_cparams(("parallel", "parallel"), VMEM_LIMIT_MB),
    )(*args)


def _ffn_bwd_dh(dG, dU, wg, wu, name, after=None):
    nd, Lp, Fs = dG.shape
    D = wg.shape[2]
    tm = _tile(Lp, 1056)

    def body(dG_ref, dU_ref, wg_ref, wu_ref, dh_ref, acc):
        j = pl.program_id(1)

        @pl.when(j == 0)
        def _():
            acc[...] = jnp.zeros_like(acc)

        acc[...] += _dot(dG_ref[0], wg_ref[0]) + _dot(dU_ref[0], wu_ref[0])

        @pl.when(j == nd - 1)
        def _():
            dh_ref[...] = acc[...]

    act = pl.BlockSpec((1, tm, Fs), lambda i, j: (j, i, 0))
    wrow = pl.BlockSpec((1, Fs, D), lambda i, j: (j, 0, 0))
    body, in_specs, args = _ordered(body, [act, act, wrow, wrow], [dG, dU, wg, wu], after)
    return pl.pallas_call(
        body, name=name, grid=(Lp // tm, nd),
        in_specs=in_specs,
        out_specs=pl.BlockSpec((tm, D), lambda i, j: (i, 0), pipeline_mode=pl.Buffered(1)),
        out_shape=jax.ShapeDtypeStruct((Lp, D), f32),
        scratch_shapes=[pltpu.VMEM((tm, D), f32)],
        compiler_params=_cparams(("parallel", "arbitrary"), VMEM_LIMIT_MB),
    )(*args)


def _ffn_bwd_wgrad(act, rows, name, after=None):
    nd, Lp, Fs = act.shape
    D = rows.shape[1]

    def body(a_ref, r_ref, o_ref):
        o_ref[0] = _dot(a_ref[0], r_ref[...], TN).astype(bf16)

    in_specs = [pl.BlockSpec((1, Lp, Fs), lambda j: (j, 0, 0)),
                pl.BlockSpec((Lp, D), lambda j: (0, 0), pipeline_mode=pl.Buffered(1))]
    body, in_specs, args = _ordered(body, in_specs, [act, rows], after)
    return pl.pallas_call(
        body, name=name, grid=(nd,),
        in_specs=in_specs, out_specs=pl.BlockSpec((1, Fs, D), lambda j: (j, 0, 0)),
        out_shape=jax.ShapeDtypeStruct((nd, Fs, D), bf16),
        compiler_params=_cparams(("parallel",), VMEM_LIMIT_MB),
    )(*args)


def _gate_fwd(proj, wlr, blr, pad, gate_blk, name):
    Lp = proj.shape[0]
    DK = wlr.shape[1]
    tm = _tile(Lp, 528)

    def body(lr_ref, w_ref, b_ref, lg_ref):
        z = _dot(lr_ref[...].astype(bf16), w_ref[...].astype(bf16)) + b_ref[...]
        ls = jnp.minimum(z, 0.0) - jnp.log(1.0 + jnp.exp(-jnp.abs(z)))
        lg_ref[...] = jnp.where(_row_ids(pl.program_id(0), tm) >= pad, ls * (1.0 / GATE_NORM), 0.0)

    return pl.pallas_call(
        body, name=name, grid=(Lp // tm,),
        in_specs=[pl.BlockSpec((tm, GATE_PAD), lambda i: (i, gate_blk)),
                  pl.BlockSpec((GATE_PAD, DK), lambda i: (0, 0)), pl.BlockSpec((1, DK), lambda i: (0, 0))],
        out_specs=pl.BlockSpec((tm, DK), lambda i: (i, 0)),
        out_shape=jax.ShapeDtypeStruct((Lp, DK), f32),
        compiler_params=_cparams(("parallel",)),
    )(proj, wlr, blr)


def _gate_bwd(dlg, proj, wlr, blr, pad, gate_blk, name):
    Lp = proj.shape[0]
    DK = wlr.shape[1]
    tm = _tile(Lp, 528)

    def body(dlg_ref, lr_ref, w_ref, b_ref, dlr_ref, dw_ref, db_ref):
        i = pl.program_id(0)

        @pl.when(i == 0)
        def _():
            dw_ref[...] = jnp.zeros_like(dw_ref)
            db_ref[...] = jnp.zeros_like(db_ref)

        lr = lr_ref[...].astype(bf16)
        w = w_ref[...].astype(bf16)
        z = _dot(lr, w) + b_ref[...]
        dz = jnp.where(_row_ids(i, tm) >= pad, dlg_ref[...] * _sigmoid(-z) * (1.0 / GATE_NORM), 0.0)
        dzb = dz.astype(bf16)
        dlr_ref[...] = _dot(dzb, w, NT).astype(bf16)
        dw_ref[...] += _dot(lr, dzb, TN)
        db_ref[...] += jnp.sum(dz, axis=0, keepdims=True)

    return pl.pallas_call(
        body, name=name, grid=(Lp // tm,),
        in_specs=[pl.BlockSpec((tm, DK), lambda i: (i, 0)), pl.BlockSpec((tm, GATE_PAD), lambda i: (i, gate_blk)),
                  pl.BlockSpec((GATE_PAD, DK), lambda i: (0, 0)), pl.BlockSpec((1, DK), lambda i: (0, 0))],
        out_specs=[pl.BlockSpec((tm, GATE_PAD), lambda i: (i, 0)), pl.BlockSpec((GATE_PAD, DK), lambda i: (0, 0)),
                   pl.BlockSpec((1, DK), lambda i: (0, 0))],
        out_shape=[jax.ShapeDtypeStruct((Lp, GATE_PAD), bf16), jax.ShapeDtypeStruct((GATE_PAD, DK), f32),
                   jax.ShapeDtypeStruct((1, DK), f32)],
        compiler_params=_cparams(("arbitrary",)),
    )(dlg, proj, wlr, blr)


def _chunk_decay(lg):
    C = lg.shape[0]
    r = lax.broadcasted_iota(jnp.int32, (C, C), 0)
    c = lax.broadcasted_iota(jnp.int32, (C, C), 1)
    return _dot(jnp.where(r >= c, 1.0, 0.0).astype(f32), lg, NN, HI)


def _col(v):
    return jnp.transpose(jnp.broadcast_to(v, (8, v.shape[1])))[:, 0:1]


def _intra_scores(q, k, b, A_ref):
    C = q.shape[0]
    S = GLA_SUB
    A_ref[...] = jnp.zeros_like(A_ref)
    ri = lax.broadcasted_iota(jnp.int32, (S, 1), 0)
    for I in range(C // S):
        lo = S * I
        qI, bI = q[lo:lo + S], b[lo:lo + S]
        if I > 0:
            bref = b[lo - 1:lo]
            qs = qI * jnp.exp(bI - bref)
            ks = k[:lo] * jnp.exp(bref - b[:lo])
            A_ref[lo:lo + S, 0:lo] = _dot(qs, ks, NT, HI)
        for jj in range(S):
            j = lo + jj
            P = jnp.exp(jnp.minimum(bI - b[j:j + 1], 0.0))
            a = jnp.sum(qI * P * k[j:j + 1], axis=1, keepdims=True)
            A_ref[lo:lo + S, j:j + 1] = jnp.where(ri >= jj, a, 0.0)


def _intra_grads(q, k, b, dA, dq_ref, dk_ref):
    C = q.shape[0]
    S = GLA_SUB
    ri = lax.broadcasted_iota(jnp.int32, (S, 1), 0)
    for I in range(C // S):
        lo = S * I
        qI, bI = q[lo:lo + S], b[lo:lo + S]
        dqI = jnp.zeros_like(qI)
        if I > 0:
            bref = b[lo - 1:lo]
            eq = jnp.exp(bI - bref)
            ek = jnp.exp(bref - b[:lo])
            qs = qI * eq
            ks = k[:lo] * ek
            dAI = dA[lo:lo + S, 0:lo]
            dqI = dqI + _dot(dAI, ks, NN, HI) * eq
            dk_ref[0:lo, :] += _dot(dAI, qs, TN, HI) * ek
        for jj in range(S):
            j = lo + jj
            P = jnp.exp(jnp.minimum(bI - b[j:j + 1], 0.0))
            t = jnp.where(ri >= jj, dA[lo:lo + S, j:j + 1], 0.0) * P
            dqI = dqI + t * k[j:j + 1]
            dk_ref[j:j + 1, :] += jnp.sum(t * qI, axis=0, keepdims=True)
        dq_ref[lo:lo + S, :] += dqI


def _gla_fwd(proj, lg, hnw, H, name):
    Lp = proj.shape[0]
    DK = lg.shape[1]
    hk = DK // H
    hv = hnw.shape[1]
    DV = hv * H
    C = GLA_CHUNK
    NC = Lp // C
    scale = float(hk) ** -0.5
    kq, kv = DK // hk, (2 * DK) // hv
    kr = kv + H

    def body(q_ref, k_ref, v_ref, r_ref, lg_ref, w_ref, o_ref, y_ref, s_ref, S_scr, A_scr):
        c = pl.program_id(1)

        @pl.when(c == 0)
        def _():
            S_scr[...] = jnp.zeros_like(S_scr)

        q = q_ref[...] * scale
        k = k_ref[...]
        v = v_ref[...]
        b = _chunk_decay(lg_ref[...])
        bl = b[C - 1:C]
        S = S_scr[...]
        s_ref[0, 0] = S
        _intra_scores(q, k, b, A_scr)
        vb = v.astype(bf16)
        o = _dot((q * jnp.exp(b)).astype(bf16), S.astype(bf16)) + _dot(A_scr[...].astype(bf16), vb)
        kb = (k * jnp.exp(bl - b)).astype(bf16)
        S_scr[...] = jnp.exp(_col(bl)) * S + _dot(kb, vb, TN)
        o_ref[...] = o
        on = o * lax.rsqrt(jnp.mean(o * o, axis=-1, keepdims=True) + EPS) * w_ref[...]
        r = r_ref[...]
        y_ref[...] = (on * (r * _sigmoid(r))).astype(bf16)

    return pl.pallas_call(
        body, name=name, grid=(H, NC),
        in_specs=[pl.BlockSpec((C, hk), lambda h, c: (c, h)),
                  pl.BlockSpec((C, hk), lambda h, c: (c, kq + h)),
                  pl.BlockSpec((C, hv), lambda h, c: (c, kv + h)),
                  pl.BlockSpec((C, hv), lambda h, c: (c, kr + h)),
                  pl.BlockSpec((C, hk), lambda h, c: (c, h)),
                  pl.BlockSpec((1, hv), lambda h, c: (0, 0))],
        out_specs=[pl.BlockSpec((C, hv), lambda h, c: (c, h)), pl.BlockSpec((C, hv), lambda h, c: (c, h)),
                   pl.BlockSpec((1, 1, hk, hv), lambda h, c: (h, c, 0, 0))],
        out_shape=[jax.ShapeDtypeStruct((Lp, DV), f32), jax.ShapeDtypeStruct((Lp, DV), bf16),
                   jax.ShapeDtypeStruct((H, NC, hk, hv), f32)],
        scratch_shapes=[pltpu.VMEM((hk, hv), f32), pltpu.VMEM((C, C), f32)],
        compiler_params=_cparams(("parallel", "arbitrary")),
    )(proj, proj, proj, proj, lg, hnw)


def _gla_bwd(dy, proj, lg, o, states, hnw, H, pad, name, after=None):
    Lp = proj.shape[0]
    DK = lg.shape[1]
    hk = DK // H
    hv = hnw.shape[1]
    DV = hv * H
    C = GLA_CHUNK
    NC = Lp // C
    scale = float(hk) ** -0.5
    kq, kv = DK // hk, (2 * DK) // hv
    kr = kv + H

    def body(dy_ref, q_ref, k_ref, v_ref, r_ref, lg_ref, o_ref, s_ref, sn_ref, w_ref,
             dq_ref, dk_ref, dv_ref, dr_ref, dlg_ref, dw_ref, dS_scr, A_scr, dq_s, dk_s):
        h = pl.program_id(0)
        cc = pl.program_id(1)
        c = NC - 1 - cc

        @pl.when(cc == 0)
        def _():
            dS_scr[...] = jnp.zeros_like(dS_scr)

        @pl.when((cc == 0) & (h == 0))
        def _():
            dw_ref[...] = jnp.zeros_like(dw_ref)

        keep = (c * C + lax.broadcasted_iota(jnp.int32, (C, 1), 0)) >= pad
        w = w_ref[...]
        o_ = o_ref[...]
        rs = lax.rsqrt(jnp.mean(o_ * o_, axis=-1, keepdims=True) + EPS)
        ohat = o_ * rs
        r = r_ref[...]
        sg = _sigmoid(r)
        dy_ = dy_ref[...]
        d_on = dy_ * (r * sg)
        dr_ref[...] = jnp.where(keep, dy_ * (ohat * w) * (sg * (1.0 + r * (1.0 - sg))), 0.0).astype(bf16)
        dw_ref[...] += jnp.sum(d_on * ohat, axis=0, keepdims=True)
        d_oh = d_on * w
        do = rs * (d_oh - ohat * jnp.mean(d_oh * ohat, axis=-1, keepdims=True))
        dob = do.astype(bf16)
        q = q_ref[...] * scale
        k = k_ref[...]
        v = v_ref[...]
        vb = v.astype(bf16)
        b = _chunk_decay(lg_ref[...])
        bl = b[C - 1:C]
        eb = jnp.exp(b)
        ekb = jnp.exp(bl - b)
        S = s_ref[0, 0]
        dS = dS_scr[...]
        dSb = dS.astype(bf16)
        _intra_scores(q, k, b, A_scr)
        ri = lax.broadcasted_iota(jnp.int32, (C, C), 0)
        ci = lax.broadcasted_iota(jnp.int32, (C, C), 1)
        dA = jnp.where(ri >= ci, _dot(dob, vb, NT), 0.0)
        kb = (k * ekb).astype(bf16)
        qb = (q * eb).astype(bf16)
        dv = _dot(A_scr[...].astype(bf16), dob, TN) + _dot(kb, dSb)
        dq_s[...] = _dot(dob, S.astype(bf16), NT) * eb
        dk_s[...] = _dot(vb, dSb, NT) * ekb
        dS_scr[...] = _dot(qb, dob, TN) + jnp.exp(_col(bl)) * dS
        _intra_grads(q, k, b, dA, dq_s, dk_s)
        dq = dq_s[...]
        dk = dk_s[...]
        Dm = q * dq - k * dk
        after = _dot(jnp.ones((8, hv), f32), sn_ref[0, 0] * dS, NT, HI)[0:1]
        dlg = _dot(jnp.where(ri <= ci, 1.0, 0.0).astype(f32), Dm, NN, HI) + after
        dlg_ref[...] = jnp.where(keep, dlg, 0.0)
        dq_ref[...] = jnp.where(keep, dq * scale, 0.0).astype(bf16)
        dk_ref[...] = jnp.where(keep, dk, 0.0).astype(bf16)
        dv_ref[...] = jnp.where(keep, dv, 0.0).astype(bf16)

    rev = lambda h, cc: NC - 1 - cc
    in_specs = [pl.BlockSpec((C, hv), lambda h, cc: (rev(h, cc), h)),
                pl.BlockSpec((C, hk), lambda h, cc: (rev(h, cc), h)),
                pl.BlockSpec((C, hk), lambda h, cc: (rev(h, cc), kq + h)),
                pl.BlockSpec((C, hv), lambda h, cc: (rev(h, cc), kv + h)),
                pl.BlockSpec((C, hv), lambda h, cc: (rev(h, cc), kr + h)),
                pl.BlockSpec((C, hk), lambda h, cc: (rev(h, cc), h)),
                pl.BlockSpec((C, hv), lambda h, cc: (rev(h, cc), h)),
                pl.BlockSpec((1, 1, hk, hv), lambda h, cc: (h, rev(h, cc), 0, 0)),
                pl.BlockSpec((1, 1, hk, hv), lambda h, cc: (h, jnp.minimum(rev(h, cc) + 1, NC - 1), 0, 0)),
                pl.BlockSpec((1, hv), lambda h, cc: (0, 0))]
    body, in_specs, args = _ordered(body, in_specs, [dy, proj, proj, proj, proj, lg, o, states, states, hnw], after)
    return pl.pallas_call(
        body, name=name, grid=(H, NC),
        in_specs=in_specs,
        out_specs=[pl.BlockSpec((C, hk), lambda h, cc: (rev(h, cc), h)),
                   pl.BlockSpec((C, hk), lambda h, cc: (rev(h, cc), h)),
                   pl.BlockSpec((C, hv), lambda h, cc: (rev(h, cc), h)),
                   pl.BlockSpec((C, hv), lambda h, cc: (rev(h, cc), h)),
                   pl.BlockSpec((C, hk), lambda h, cc: (rev(h, cc), h)),
                   pl.BlockSpec((1, hv), lambda h, cc: (0, 0))],
        out_shape=[jax.ShapeDtypeStruct((Lp, DK), bf16), jax.ShapeDtypeStruct((Lp, DK), bf16),
                   jax.ShapeDtypeStruct((Lp, DV), bf16), jax.ShapeDtypeStruct((Lp, DV), bf16),
                   jax.ShapeDtypeStruct((Lp, DK), f32), jax.ShapeDtypeStruct((1, hv), f32)],
        scratch_shapes=[pltpu.VMEM((hk, hv), f32), pltpu.VMEM((C, C), f32),
                        pltpu.VMEM((C, hk), f32), pltpu.VMEM((C, hk), f32)],
        compiler_params=_cparams(("arbitrary", "arbitrary")),
    )(*args)


def _window_sums(x, back):
    n = x.shape[0]
    out = []
    s = x
    for w in (1, 2, 4, 8):
        s = s + pltpu.roll(s, w if back else n - w, 0)
        out.append(s)
    return out


def _pool_windows(hn, pad, n_real, name):
    Lp, D = hn.shape
    GW = D // POOL_GROUPS
    cb = min(GW, 256)
    per = GW // cb

    def body(h_ref, p_ref):
        g = pl.program_id(0) // per
        x = h_ref[...]
        s2, s4, s8, s16 = _window_sums(x, True)
        sel = jnp.where(g == 0, s2, jnp.where(g == 1, s4, jnp.where(g == 2, s8, s16)))
        win = jnp.left_shift(2, g).astype(f32)
        rows = lax.broadcasted_iota(jnp.int32, (Lp, 1), 0)
        t = (rows - pad).astype(f32)
        cnt = jnp.minimum(jnp.maximum(t, 0.0) + 1.0, win)
        p_ref[...] = jnp.where(rows >= pad, sel / cnt - x, 0.0).astype(bf16)

    return pl.pallas_call(
        body, name=name, grid=(D // cb,),
        in_specs=[pl.BlockSpec((Lp, cb), lambda i: (0, i))],
        out_specs=pl.BlockSpec((Lp, cb), lambda i: (0, i)),
        out_shape=jax.ShapeDtypeStruct((Lp, D), bf16),
        compiler_params=_cparams(("parallel",)),
    )(hn)


def _pool_windows_bwd(dp, pad, name):
    Lp, D = dp.shape
    GW = D // POOL_GROUPS
    cb = min(GW, 256)
    per = GW // cb

    def body(dp_ref, dh_ref):
        g = pl.program_id(0) // per
        rows = lax.broadcasted_iota(jnp.int32, (Lp, 1), 0)
        d = jnp.where(rows >= pad, dp_ref[...], 0.0)
        win = jnp.left_shift(2, g).astype(f32)
        t = (rows - pad).astype(f32)
        cnt = jnp.minimum(jnp.maximum(t, 0.0) + 1.0, win)
        s2, s4, s8, s16 = _window_sums(d / cnt, False)
        sel = jnp.where(g == 0, s2, jnp.where(g == 1, s4, jnp.where(g == 2, s8, s16)))
        dh_ref[...] = jnp.where(rows >= pad, sel - d, 0.0)

    return pl.pallas_call(
        body, name=name, grid=(D // cb,),
        in_specs=[pl.BlockSpec((Lp, cb), lambda i: (0, i))],
        out_specs=pl.BlockSpec((Lp, cb), lambda i: (0, i)),
        out_shape=jax.ShapeDtypeStruct((Lp, D), f32),
        compiler_params=_cparams(("parallel",)),
    )(dp)


def _pool_mix_fwd(xs, pooled, w, bias, scale, pad, name):
    Lp, D = xs.shape
    GW = D // POOL_GROUPS
    tm = _tile(Lp, 1056)

    def body(x_ref, p_ref, w_ref, b_ref, s_ref, o_ref):
        z = _dot(p_ref[...], w_ref[0]) + b_ref[...]
        keep = _row_ids(pl.program_id(1), tm) >= pad
        o_ref[...] = x_ref[...] + jnp.where(keep, z * s_ref[...], 0.0)

    blk = pl.BlockSpec((tm, GW), lambda g, i: (i, g))
    vec = pl.BlockSpec((1, GW), lambda g, i: (0, g))
    return pl.pallas_call(
        body, name=name, grid=(POOL_GROUPS, Lp // tm),
        in_specs=[blk, blk, pl.BlockSpec((1, GW, GW), lambda g, i: (g, 0, 0)), vec, vec],
        out_specs=blk, out_shape=jax.ShapeDtypeStruct((Lp, D), f32),
        compiler_params=_cparams(("parallel", "parallel")),
    )(xs, pooled, w, bias, scale)


def _pool_mix_bwd(dY, pooled, w, bias, scale, pad, name, after=None):
    Lp, D = dY.shape
    GW = D // POOL_GROUPS
    tm = _tile(Lp, 1056)
    nm = Lp // tm

    def body(dY_ref, p_ref, w_ref, b_ref, s_ref, dp_ref, dw_ref, db_ref, ds_ref, acc):
        i = pl.program_id(1)

        @pl.when(i == 0)
        def _():
            acc[...] = jnp.zeros_like(acc)
            db_ref[...] = jnp.zeros_like(db_ref)
            ds_ref[...] = jnp.zeros_like(ds_ref)

        keep = _row_ids(i, tm) >= pad
        dY_ = jnp.where(keep, dY_ref[...], 0.0)
        p = p_ref[...]
        z = _dot(p, w_ref[0]) + b_ref[...]
        ds_ref[...] += jnp.sum(dY_ * z, axis=0, keepdims=True)
        dz = dY_ * s_ref[...]
        db_ref[...] += jnp.sum(dz, axis=0, keepdims=True)
        dzb = dz.astype(bf16)
        acc[...] += _dot(p, dzb, TN)
        dp_ref[...] = _dot(dzb, w_ref[0], NT)

        @pl.when(i == nm - 1)
        def _():
            dw_ref[0] = acc[...].astype(bf16)

    blk = pl.BlockSpec((tm, GW), lambda g, i: (i, g))
    vec = pl.BlockSpec((1, GW), lambda g, i: (0, g))
    wsp = pl.BlockSpec((1, GW, GW), lambda g, i: (g, 0, 0))
    body, in_specs, args = _ordered(body, [blk, blk, wsp, vec, vec], [dY, pooled, w, bias, scale], after)
    return pl.pallas_call(
        body, name=name, grid=(POOL_GROUPS, nm),
        in_specs=in_specs, out_specs=[blk, wsp, vec, vec],
        out_shape=[jax.ShapeDtypeStruct((Lp, D), f32), jax.ShapeDtypeStruct((POOL_GROUPS, GW, GW), bf16),
                   jax.ShapeDtypeStruct((1, D), f32), jax.ShapeDtypeStruct((1, D), f32)],
        scratch_shapes=[pltpu.VMEM((GW, GW), f32)],
        compiler_params=_cparams(("parallel", "arbitrary")),
    )(*args)


def _loss_head(xs, target, g, first, name):
    Lp, D = xs.shape
    tm = GLA_CHUNK
    off = first // tm

    def body(x_ref, t_ref, g_ref, loss_ref, dxs_ref, dg_ref, half_ref):
        i = pl.program_id(0)

        @pl.when(i == 0)
        def _():
            loss_ref[...] = jnp.zeros_like(loss_ref)
            dg_ref[...] = jnp.zeros_like(dg_ref)

        @pl.when(i < off)
        def _():
            dxs_ref[...] = jnp.zeros_like(dxs_ref)
            half_ref[...] = jnp.zeros_like(half_ref)

        @pl.when(i >= off)
        def _():
            x = x_ref[...]
            rstd = lax.rsqrt(jnp.mean(x * x, axis=-1, keepdims=True) + EPS)
            xhat = x * rstd
            gg = g_ref[...]
            err = xhat * gg - t_ref[...]
            loss_ref[...] += 0.5 * jnp.sum(jnp.mean(err * err, axis=-1, keepdims=True))
            dy = err * (1.0 / D)
            dg_ref[...] += jnp.sum(dy * xhat, axis=0, keepdims=True)
            dxh = dy * gg
            out = rstd * (dxh - xhat * jnp.mean(dxh * xhat, axis=-1, keepdims=True))
            dxs_ref[...] = out
            half_ref[...] = (0.5 * out).astype(bf16)

    row = pl.BlockSpec((tm, D), lambda i: (i, 0))
    return pl.pallas_call(
        body, name=name, grid=(Lp // tm,),
        in_specs=[row, pl.BlockSpec((tm, D), lambda i: (jnp.maximum(i - off, 0), 0)), pl.BlockSpec((1, D), lambda i: (0, 0))],
        out_specs=[pl.BlockSpec((8, LANES), lambda i: (0, 0)), row, pl.BlockSpec((1, D), lambda i: (0, 0)), row],
        out_shape=[jax.ShapeDtypeStruct((8, LANES), f32), jax.ShapeDtypeStruct((Lp, D), f32),
                   jax.ShapeDtypeStruct((1, D), f32), jax.ShapeDtypeStruct((Lp, D), bf16)],
        compiler_params=_cparams(("arbitrary",)),
    )(xs, target, g)


def _adam_math(w, g, m, v):
    m2 = ADAM_B1 * m + (1.0 - ADAM_B1) * g
    v2 = ADAM_B2 * v + (1.0 - ADAM_B2) * (g * g)
    m_hat = m2 / (1.0 - ADAM_B1 ** ADAM_STEP)
    v_hat = v2 / (1.0 - ADAM_B2 ** ADAM_STEP)
    delta = -ADAM_LR * (m_hat / (jnp.sqrt(v_hat) + ADAM_EPS) + ADAM_WD * w)
    return delta, m2, v2


def _adamw(w, m, v, unit, own, own_idx, recv, prev, name, after=None):
    U, R, C = w.shape
    tr, tc = _tile2(R, C, 256, 8 if own.dtype == f32 and recv is None else 16)
    n_recv = 0 if recv is None else recv.shape[0]

    def body(idx_ref, w_ref, m_ref, v_ref, own_ref, *rest):
        rest = list(rest)
        recv_refs = [rest.pop(0) for _ in range(n_recv)]
        if prev is not None:
            rest = rest[4:]
        g_ref, d_ref, m2_ref, v2_ref = rest
        g = own_ref[0].astype(f32)
        for r_ref in recv_refs:
            g = g + r_ref[0].astype(f32)
        delta, m2, v2 = _adam_math(w_ref[0], g, m_ref[0], v_ref[0])
        g_ref[0] = g
        d_ref[0] = delta
        m2_ref[0] = m2
        v2_ref[0] = v2

    blk = pl.BlockSpec((1, tr, tc), lambda i, j, idx: (unit, i, j))
    in_specs = [blk, blk, blk, pl.BlockSpec((1, tr, tc), lambda i, j, idx: (idx[0], i, j))]
    args = [w, m, v, own]
    for p in range(n_recv):
        in_specs.append(pl.BlockSpec((1, tr, tc), lambda i, j, idx, p=p: (p, i, j)))
        args.append(recv)
    aliases = {}
    if prev is not None:
        for t in range(4):
            aliases[1 + len(args) + t] = t
        in_specs += [ANY] * 4
        args += list(prev)
    body, in_specs, args = _ordered(body, in_specs, args, after, lead=1)
    out = jax.ShapeDtypeStruct((U, R, C), f32)
    return pl.pallas_call(
        body, name=name,
        grid_spec=pltpu.PrefetchScalarGridSpec(
            num_scalar_prefetch=1, grid=(R // tr, C // tc), in_specs=in_specs, out_specs=[blk] * 4),
        out_shape=[out] * 4, input_output_aliases=aliases,
        compiler_params=_cparams(("parallel", "parallel")),
    )(own_idx, *args)


def _place():
    return lax.axis_index("x"), lax.axis_index("y"), lax.axis_index("c")


HBM = pl.BlockSpec(memory_space=pltpu.HBM)
SEM = pl.BlockSpec(memory_space=pltpu.SEMAPHORE)
VMEM_SPEC = pl.BlockSpec(memory_space=pltpu.VMEM)
EFFECT = pltpu.SideEffectType.DATAFLOW_SIDE_EFFECTING
TOKEN = jax.ShapeDtypeStruct((8, LANES), f32)


def _hbm(x):
    return pltpu.with_memory_space_constraint(x, pltpu.HBM)


def _hbm_like(xs):
    return [pltpu.HBM(x.shape, x.dtype) for x in xs]


def _slot(px, py, pc):
    return 4 * px + 2 * py + pc


def _halves(ref):
    n = ref.shape[0]
    cut = n // 2 if n < 32 else (n // 2) // 16 * 16
    return ref.at[pl.ds(0, cut)], ref.at[pl.ds(cut, n - cut)]


def _gather_start(shards, after, name):
    n = len(shards)
    me = _slot(*_place())
    bufs = [lax.dynamic_update_slice(lax.empty((N_DEV,) + s.shape, s.dtype), s[None], (me,) + (0,) * s.ndim) for s in shards]

    def body(*refs):
        ins, land = refs[:n], refs[n:2 * n]
        send, recv = refs[2 * n + 1], refs[2 * n + 2]
        token = refs[-1]
        x, y, c = _place()
        to = [(x, y, 1 - c), (1 - x, y, c), (x, 1 - y, c)]
        for a in range(n):
            for k, dev in enumerate(to):
                pltpu.make_async_remote_copy(
                    src_ref=ins[a], dst_ref=land[a].at[_slot(x, y, c)], send_sem=send.at[3 * a + k], recv_sem=recv.at[3 * a + k],
                    device_id=dev, device_id_type=MESH).start()
        token[...] = jnp.zeros_like(token)

    out = pl.pallas_call(
        body, name=name,
        in_specs=[HBM] * (2 * n) + [ANY],
        out_specs=[SEM, SEM] + [HBM] * (2 * n) + [VMEM_SPEC],
        out_shape=[pltpu.SemaphoreType.DMA((3 * n,)), pltpu.SemaphoreType.DMA((3 * n,))] + _hbm_like(shards) + _hbm_like(bufs) + [TOKEN],
        input_output_aliases={i: 2 + i for i in range(2 * n)},
        compiler_params=pltpu.CompilerParams(has_side_effects=EFFECT),
    )(*[_hbm(s) for s in shards], *[_hbm(b) for b in bufs], after)
    return dict(send1=out[0], recv1=out[1], shards=list(out[2:2 + n]), bufs=list(out[2 + n:2 + 2 * n]), token=out[-1])


def _gather_mid(h, after, name):
    n = len(h["bufs"])

    def body(*refs):
        land, recv1 = refs[:n], refs[n]
        send2, recv2 = refs[n + 2], refs[n + 3]
        token = refs[-1]
        x, y, c = _place()
        nbr = [(1 - x, y, c), (x, 1 - y, c)]
        for j, dev in enumerate(nbr):
            for a in range(n):
                blk = land[a].at[_slot(*dev)]
                pltpu.make_async_remote_copy(
                    src_ref=blk, dst_ref=blk, send_sem=send2.at[4 * a + j], recv_sem=recv1.at[3 * a + 1 + j],
                    device_id=dev, device_id_type=MESH).wait_recv()
                pltpu.make_async_remote_copy(
                    src_ref=blk, dst_ref=blk, send_sem=send2.at[4 * a + j], recv_sem=recv2.at[4 * a + j],
                    device_id=(x, y, 1 - c), device_id_type=MESH).start()
        for a in range(n):
            from_x, from_y = land[a].at[_slot(*nbr[0])], land[a].at[_slot(*nbr[1])]
            for k, (half, dev) in enumerate([(_halves(from_y)[0], nbr[0]), (_halves(from_x)[1], nbr[1])]):
                pltpu.make_async_remote_copy(
                    src_ref=half, dst_ref=half, send_sem=send2.at[4 * a + 2 + k], recv_sem=recv2.at[4 * a + 2 + k],
                    device_id=dev, device_id_type=MESH).start()
        token[...] = jnp.zeros_like(token)

    out = pl.pallas_call(
        body, name=name,
        in_specs=[HBM] * n + [SEM, ANY],
        out_specs=[SEM, SEM] + [HBM] * n + [VMEM_SPEC],
        out_shape=[pltpu.SemaphoreType.DMA((4 * n,)), pltpu.SemaphoreType.DMA((4 * n,))] + _hbm_like(h["bufs"]) + [TOKEN],
        input_output_aliases={i: 2 + i for i in range(n)},
        compiler_params=pltpu.CompilerParams(has_side_effects=EFFECT),
    )(*h["bufs"], h["recv1"], after)
    h.update(send2=out[0], recv2=out[1], bufs=list(out[2:2 + n]), token=out[-1])
    return h


def _gather_mid2(h, after, name):
    n = len(h["bufs"])

    def body(*refs):
        land, recv2 = refs[:n], refs[n]
        send3, recv3 = refs[n + 2], refs[n + 3]
        token = refs[-1]
        x, y, c = _place()
        for a in range(n):
            blk = land[a].at[_slot(1 - x, 1 - y, c)]
            for k, half in enumerate(_halves(blk)):
                pltpu.make_async_remote_copy(
                    src_ref=half, dst_ref=half, send_sem=send3.at[a], recv_sem=recv2.at[4 * a + 2 + k],
                    device_id=(x, y, 1 - c), device_id_type=MESH).wait_recv()
            pltpu.make_async_remote_copy(
                src_ref=blk, dst_ref=blk, send_sem=send3.at[a], recv_sem=recv3.at[a],
                device_id=(x, y, 1 - c), device_id_type=MESH).start()
        token[...] = jnp.zeros_like(token)

    out = pl.pallas_call(
        body, name=name,
        in_specs=[HBM] * n + [SEM, ANY],
        out_specs=[SEM, SEM] + [HBM] * n + [VMEM_SPEC],
        out_shape=[pltpu.SemaphoreType.DMA((n,)), pltpu.SemaphoreType.DMA((n,))] + _hbm_like(h["bufs"]) + [TOKEN],
        input_output_aliases={i: 2 + i for i in range(n)},
        compiler_params=pltpu.CompilerParams(has_side_effects=EFFECT),
    )(*h["bufs"], h["recv2"], after)
    h.update(send3=out[0], recv3=out[1], bufs=list(out[2:2 + n]), token=out[-1])
    return h


def _gather_end(h, after, name):
    n = len(h["bufs"])

    def body(*refs):
        ins, land = refs[:n], refs[n:2 * n]
        send1, recv1, send2, recv2, send3, recv3 = refs[2 * n:2 * n + 6]
        x, y, c = _place()
        sib = (x, y, 1 - c)
        nbr = [(1 - x, y), (x, 1 - y)]

        def wait(src, dst, ssem, rsem, send):
            cp = pltpu.make_async_remote_copy(src_ref=src, dst_ref=dst, send_sem=ssem, recv_sem=rsem, device_id=sib, device_id_type=MESH)
            cp.wait_send() if send else cp.wait_recv()

        for a in range(n):
            mine = land[a].at[_slot(x, y, c)]
            for k in range(3):
                wait(ins[a], mine, send1.at[3 * a + k], recv1.at[3 * a + k], True)
            wait(ins[a], land[a].at[_slot(x, y, 1 - c)], send1.at[3 * a], recv1.at[3 * a], False)
            for j, (px, py) in enumerate(nbr):
                sent = land[a].at[_slot(px, py, c)]
                wait(sent, sent, send2.at[4 * a + j], recv2.at[4 * a + j], True)
                wait(sent, land[a].at[_slot(px, py, 1 - c)], send2.at[4 * a + j], recv2.at[4 * a + j], False)
            halves = [_halves(land[a].at[_slot(*nbr[1], c)])[0], _halves(land[a].at[_slot(*nbr[0], c)])[1]]
            for k, half in enumerate(halves):
                wait(half, half, send2.at[4 * a + 2 + k], recv2.at[4 * a + 2 + k], True)
            diag = land[a].at[_slot(1 - x, 1 - y, c)]
            wait(diag, diag, send3.at[a], recv3.at[a], True)
            wait(diag, land[a].at[_slot(1 - x, 1 - y, 1 - c)], send3.at[a], recv3.at[a], False)

    out = pl.pallas_call(
        body, name=name,
        in_specs=[HBM] * (2 * n) + [SEM] * 6 + [ANY],
        out_specs=[HBM] * n,
        out_shape=_hbm_like(h["bufs"]),
        input_output_aliases={n + i: i for i in range(n)},
        compiler_params=pltpu.CompilerParams(has_side_effects=EFFECT),
    )(*h["shards"], *h["bufs"], h["send1"], h["recv1"], h["send2"], h["recv2"], h["send3"], h["recv3"], after)
    return list(out)


def _peer_plan(kind, x, y, c):
    if kind == "pair":
        return [(2 * q + (1 - c), q, (x, y, 1 - c)) for q in range(4)]
    chips = [(1 - x, y), (x, 1 - y), (1 - x, 1 - y)]
    return [(2 * px + py, k, (px, py, c)) for k, (px, py) in enumerate(chips)]


def _exchange_start(kind, srcs, after, name):
    n = len(srcs)
    K = 4 if kind == "pair" else 3
    lands = [_hbm(lax.empty((K,) + s.shape[1:], s.dtype)) for s in srcs]

    def body(*refs):
        ins, land = refs[:n], refs[n:2 * n]
        send, recv = refs[2 * n + 1], refs[2 * n + 2]
        token = refs[-1]
        for a in range(n):
            for k, (si, di, dev) in enumerate(_peer_plan(kind, *_place())):
                pltpu.make_async_remote_copy(
                    src_ref=ins[a].at[si], dst_ref=land[a].at[di], send_sem=send.at[K * a + k], recv_sem=recv.at[K * a + k],
                    device_id=dev, device_id_type=MESH).start()
        token[...] = jnp.zeros_like(token)

    out = pl.pallas_call(
        body, name=name,
        in_specs=[HBM] * (2 * n) + [ANY],
        out_specs=[SEM, SEM] + [HBM] * (2 * n) + [VMEM_SPEC],
        out_shape=[pltpu.SemaphoreType.DMA((K * n,)), pltpu.SemaphoreType.DMA((K * n,))] + _hbm_like(srcs) + _hbm_like(lands) + [TOKEN],
        input_output_aliases={i: 2 + i for i in range(2 * n)},
        compiler_params=pltpu.CompilerParams(has_side_effects=EFFECT),
    )(*[_hbm(s) for s in srcs], *lands, after)
    return dict(kind=kind, send=out[0], recv=out[1], srcs=list(out[2:2 + n]), lands=list(out[2 + n:2 + 2 * n]), token=out[-1])


def _exchange_wait(h, after, name):
    n = len(h["srcs"])
    kind = h["kind"]
    K = 4 if kind == "pair" else 3

    def body(*refs):
        ins, land = refs[:n], refs[n:2 * n]
        send, recv = refs[2 * n], refs[2 * n + 1]
        for a in range(n):
            for k, (si, di, dev) in enumerate(_peer_plan(kind, *_place())):
                cp = pltpu.make_async_remote_copy(
                    src_ref=ins[a].at[si], dst_ref=land[a].at[di], send_sem=send.at[K * a + k], recv_sem=recv.at[K * a + k],
                    device_id=dev, device_id_type=MESH)
                cp.wait_send()
                cp.wait_recv()

    out = pl.pallas_call(
        body, name=name,
        in_specs=[HBM] * (2 * n) + [SEM, SEM, ANY],
        out_specs=[HBM] * (2 * n),
        out_shape=_hbm_like(h["srcs"]) + _hbm_like(h["lands"]),
        input_output_aliases={i: i for i in range(2 * n)},
        compiler_params=pltpu.CompilerParams(has_side_effects=EFFECT),
    )(*h["srcs"], *h["lands"], h["send"], h["recv"], after)
    return list(out[:n]), list(out[n:])


def _pair_add(g, got, c_idx, name):
    _, R, C = g.shape
    tr, tc = _tile2(R, C, 512, 16)

    def body(c_ref, a_ref, b_ref, o_ref):
        o_ref[0] = (a_ref[0].astype(f32) + b_ref[0].astype(f32)).astype(o_ref.dtype)

    return pl.pallas_call(
        body, name=name,
        grid_spec=pltpu.PrefetchScalarGridSpec(
            num_scalar_prefetch=1, grid=(4, R // tr, C // tc),
            in_specs=[pl.BlockSpec((1, tr, tc), lambda q, i, j, c: (2 * q + c[0], i, j)),
                      pl.BlockSpec((1, tr, tc), lambda q, i, j, c: (q, i, j))],
            out_specs=pl.BlockSpec((1, tr, tc), lambda q, i, j, c: (q, i, j))),
        out_shape=jax.ShapeDtypeStruct((4, R, C), g.dtype),
        compiler_params=_cparams(("parallel", "parallel", "parallel")),
    )(c_idx, g, got)


def _small_exchange(send, gather, name, after=None):
    R = send.shape[-2]

    def body(in_ref, out_ref, send_sems, recv_sems):
        x, y, c = _place()
        me = 4 * x + 2 * y + c
        out_ref[me] = in_ref[...] if gather else in_ref[me]
        cps = []
        for k in range(1, N_DEV):
            px, py, pc = x ^ ((k >> 2) & 1), y ^ ((k >> 1) & 1), c ^ (k & 1)
            src = in_ref if gather else in_ref.at[4 * px + 2 * py + pc]
            cps.append(pltpu.make_async_remote_copy(
                src_ref=src, dst_ref=out_ref.at[me],
                send_sem=send_sems.at[k - 1], recv_sem=recv_sems.at[k - 1],
                device_id=(px, py, pc), device_id_type=MESH))
        for cp in cps:
            cp.start()
        for cp in cps:
            cp.wait()

    body, in_specs, args = _ordered(body, [pl.BlockSpec(memory_space=pltpu.VMEM)], [send], after)
    return pl.pallas_call(
        body, name=name,
        in_specs=in_specs, out_specs=pl.BlockSpec(memory_space=pltpu.VMEM),
        out_shape=jax.ShapeDtypeStruct((N_DEV, R, LANES), f32),
        scratch_shapes=[pltpu.SemaphoreType.DMA((N_DEV - 1,)), pltpu.SemaphoreType.DMA((N_DEV - 1,))],
    )(*args)


def _sum_blocks(blocks, name):
    def body(in_ref, o_ref):
        s = in_ref[0]
        for d in range(1, N_DEV):
            s = s + in_ref[d]
        o_ref[0] = s

    return pl.pallas_call(body, name=name, out_shape=jax.ShapeDtypeStruct((1,) + blocks.shape[1:], f32))(blocks)


def _rows(n):
    return -(-n // LANES)


def _pack(arrs, total_rows):
    parts = []
    for a in arrs:
        flat = a.reshape(-1).astype(f32)
        parts.append(jnp.pad(flat, (0, _rows(flat.size) * LANES - flat.size)))
    flat = jnp.concatenate(parts)
    return jnp.pad(flat, (0, total_rows * LANES - flat.size)).reshape(total_rows, LANES)


def _unpack(packed, shapes):
    lead = packed.shape[:-2]
    flat = packed.reshape(lead + (-1,))
    out, pos = [], 0
    for s in shapes:
        n = 1
        for d in s:
            n *= d
        out.append(flat[..., pos:pos + n].reshape(lead + tuple(s)))
        pos += _rows(n) * LANES
    return out


def _to_shards(full, axis):
    s = full.shape
    return jnp.moveaxis(full.reshape(s[:axis] + (N_DEV, s[axis] // N_DEV) + s[axis + 1:]), axis, 0)


def _from_shards(sh, axis):
    m = jnp.moveaxis(sh, 0, axis)
    s = m.shape
    return m.reshape(s[:axis] + (s[axis] * s[axis + 1],) + s[axis + 2:])


def kernel(x, meta, ffn_norm, ffn_w_gate, ffn_w_up, ffn_w_down, gla_norm, gla_w_in, gla_w_lr, gla_b_lr, gla_head_norm, gla_w_out, pool_norm, pool_w, pool_b, pool_scale, final_norm, loss_target, m_meta, m_ffn_norm, m_ffn_w_gate, m_ffn_w_up, m_ffn_w_down, m_gla_norm, m_gla_w_in, m_gla_w_lr, m_gla_b_lr, m_gla_head_norm, m_gla_w_out, m_pool_norm, m_pool_w, m_pool_b, m_pool_scale, m_final_norm, v_meta, v_ffn_norm, v_ffn_w_gate, v_ffn_w_up, v_ffn_w_down, v_gla_norm, v_gla_w_in, v_gla_w_lr, v_gla_b_lr, v_gla_head_norm, v_gla_w_out, v_pool_norm, v_pool_w, v_pool_b, v_pool_scale, v_final_norm):
    H = GLA_HEADS
    _, SEQ, D = x.shape
    Fs = ffn_w_gate.shape[-1]
    DK, DV = D // 2, D
    hv = DV // H
    GW = D // POOL_GROUPS
    INW = 2 * DK + 2 * DV + GATE_RANK
    NPK = 2 * DK + 2 * DV + GATE_PAD
    pad = (-N_META) % GLA_CHUNK
    first = pad + N_META
    Lp = first + SEQ
    n_units = ffn_w_gate.shape[0] * ffn_w_gate.shape[1]
    assert first % GLA_CHUNK == 0 and Lp % GLA_CHUNK == 0 and pad >= POOL_GROUPS * 4

    px, py, pc = _place()
    c_idx = jnp.reshape(pc, (1,)).astype(jnp.int32)
    q_idx = jnp.reshape(2 * px + py, (1,)).astype(jnp.int32)
    zero_idx = jnp.zeros((1,), jnp.int32)

    small_sh = [meta, ffn_norm, gla_w_lr, pool_norm, pool_b, pool_scale]
    small_axis = [1, 2, 2, 1, 2, 1]
    sh_shapes = [a.shape for a in small_sh]
    sh_rows = -(-sum(_rows(a.size) for a in small_sh) // 8) * 8
    gathered = _small_exchange(_pack(small_sh, sh_rows), True, "small_gather")
    meta_f, ffn_norm_f, wlr_f, pool_norm_f, pool_b_f, pool_scale_f = [
        _from_shards(a, ax) for a, ax in zip(_unpack(gathered, sh_shapes), small_axis)]
    ffn_norm_f = ffn_norm_f.reshape(n_units, 1, D)
    wlr128 = jnp.pad(wlr_f[0], ((0, GATE_PAD - GATE_RANK), (0, 0)))

    def t_units(w):
        return jnp.swapaxes(w, -1, -2).reshape(n_units, Fs, D)

    ffn_f32 = [t_units(ffn_w_gate), t_units(ffn_w_up), ffn_w_down.reshape(n_units, Fs, D)]
    mixer_f32 = [gla_w_in[0].T[None], gla_w_out, pool_w[0].reshape(1, -1, GW)]
    gather_order = [("ffn0", ffn_f32, 0), ("mixers", mixer_f32, 0)] + [(f"ffn{u}", ffn_f32, u) for u in range(1, n_units)]
    c_lr = 2 * DK + DV
    c_r = 2 * DK + 2 * DV
    gate_blk = c_r // GATE_PAD

    def gather_begin(i, after):
        tag, arrays, u = gather_order[i]
        shards = [_cast_unit(w, u, f"cast_{tag}_{a}", after if i > 0 else None) for a, w in enumerate(arrays)]
        if tag == "mixers":
            shards[2] = shards[2].reshape(pool_w.shape[1:])
        return _gather_start(shards, after, f"gather_start_{tag}")

    def gather_next(i, h, after):
        tag = gather_order[i][0]
        h = _gather_mid(h, after, f"gather_mid_{tag}")
        nxt = gather_begin(i + 1, h["token"]) if i + 1 < len(gather_order) else None
        h = _gather_mid2(h, h["token"] if nxt is None else nxt["token"], f"gather_mid2_{tag}")
        done = _gather_end(h, h["token"], f"gather_end_{tag}")
        return done, nxt

    xs = jnp.concatenate([jnp.zeros((pad, D), f32), meta_f, x[0]], axis=0)
    saved = {}
    ffn_w = [None] * n_units

    def ffn_f(u, xs):
        out, h, G, U = _ffn_fwd(xs, ffn_norm_f[u], *ffn_w[u], name=f"ffn_fwd{u}")
        saved[("ffn", u)] = (xs, h, G, U)
        return out

    def gla_f(xs, win_p, wout_full):
        hn = _rms_fwd(xs, gla_norm, bf16, "gla_norm_fwd")
        proj = _mm(hn, win_p, "nt", f32, "gla_proj", tm=1056, tn=896, tk=2048)
        lg = _gate_fwd(proj, wlr128, gla_b_lr, pad, gate_blk, "gla_gate_fwd")
        o, y, states = _gla_fwd(proj, lg, gla_head_norm, H, "gla_core_fwd")
        out = _mm(y, wout_full, "nn", f32, "gla_out", tm=1056, tn=512, tk=2048, residual=xs)
        saved["gla"] = (xs, hn, proj, lg, o, y, states)
        return out

    def pool_f(xs, wpool_full):
        hn = _rms_fwd(xs, pool_norm_f, f32, "pool_norm_fwd")
        pooled = _pool_windows(hn, pad, Lp - pad, "pool_windows_fwd")
        out = _pool_mix_fwd(xs, pooled, wpool_full, pool_b_f.reshape(1, D), pool_scale_f, pad, "pool_mix_fwd")
        saved["pool"] = (xs, pooled)
        return out

    depth = ffn_w_gate.shape[0]
    assert depth == 2 and n_units == 4
    h = gather_begin(0, gathered)
    ffn_w[0], h = gather_next(0, h, h["token"])
    xs = ffn_f(0, xs)
    (win_g, wout_g, wpool_g), h = gather_next(1, h, xs)
    win_full = win_g.reshape(INW, D)
    win_p = jnp.concatenate([win_full[:c_lr], win_full[c_lr + GATE_RANK:], win_full[c_lr:c_lr + GATE_RANK],
                             jnp.zeros((GATE_PAD - GATE_RANK, D), bf16)], axis=0)
    wout_full = wout_g.reshape(DV, D)
    wpool_full = _from_shards(wpool_g, 1)
    xs = gla_f(xs, win_p, wout_full)
    ffn_w[1], h = gather_next(2, h, xs)
    xs = ffn_f(1, xs)
    ffn_w[2], h = gather_next(3, h, xs)
    xs = ffn_f(2, xs)
    xs = pool_f(xs, wpool_full)
    ffn_w[3], h = gather_next(4, h, xs)
    xs = ffn_f(3, xs)
    loss_part, dxs, d_final, dyh = _loss_head(xs, loss_target[0], final_norm.reshape(1, D), first, "loss_head")

    class Reduce:
        def __init__(self, tag, grads, after=None):
            self.tag = tag
            self.h = _exchange_start("pair", grads, loss_part if after is None else after, f"pair_start_{tag}")
            self.token = self.h["token"]

        def mid(self, after):
            grads, got = _exchange_wait(self.h, after, f"pair_wait_{self.tag}")
            self.sums = [_pair_add(g, r, c_idx, f"pair_add_{self.tag}{a}") for a, (g, r) in enumerate(zip(grads, got))]
            self.h = _exchange_start("chips", self.sums, loss_part, f"chips_start_{self.tag}")
            self.token = self.h["token"]

        def end(self, after):
            sums, recv = _exchange_wait(self.h, after, f"chips_wait_{self.tag}")
            return list(zip(sums, recv))

    d_ffn_norm = [None] * n_units
    small_grads = {}

    def ffn_b(u, dY, dyh, prev):
        xs_in, h_, G, U = saved[("ffn", u)]
        wg, wu, wd = ffn_w[u]
        tok = None if prev is None else prev.token
        dG, dU, A = _ffn_bwd_act(dyh, wd, G, U, f"ffn_act{u}", after=tok)
        dh = _ffn_bwd_dh(dG, dU, wg, wu, f"ffn_dh{u}")
        dxs, dg, dyh_next = _rms_bwd(dY, dh, xs_in, ffn_norm_f[u], pad, f"ffn_norm_bwd{u}")
        if prev is not None:
            prev.mid(dxs)
            tok = prev.token
        dwg = _ffn_bwd_wgrad(dG, h_, f"ffn_wgrad_gate{u}", after=tok)
        dwu = _ffn_bwd_wgrad(dU, h_, f"ffn_wgrad_up{u}", after=tok)
        dwd = _ffn_bwd_wgrad(A, dyh, f"ffn_wgrad_down{u}", after=tok)
        d_ffn_norm[u] = dg
        return dxs, dyh_next, Reduce(f"ffn{u}", [dwg, dwu, dwd])

    def gla_b(dY, prev):
        xs_in, hn, proj, lg, o, y, states = saved["gla"]
        dyb = dY.astype(bf16)
        dy = _mm(dyb, wout_full, "nt", f32, "gla_out_dgrad", tm=1056, tn=512, tk=2048, after=prev.token)
        dwout = _mm_tn_full(y, dyb, "gla_out_wgrad", 1024, after=prev.token)
        prev.mid(dwout)
        dq, dk, dv, dr, dlg, dhw = _gla_bwd(dy, proj, lg, o, states, gla_head_norm, H, pad, "gla_core_bwd", after=prev.token)
        dlr, dwlr, dblr = _gate_bwd(dlg, proj, wlr128, gla_b_lr, pad, gate_blk, "gla_gate_bwd")
        dproj = jnp.concatenate([dq, dk, dv, dr, dlr], axis=1)
        dwin_p = _mm_tn_full(dproj, hn, "gla_proj_wgrad", 896)
        dhn = _mm(dproj, win_p, "nn", f32, "gla_proj_dgrad", tm=1056, tn=1024, tk=896)
        dxs, dgn, dyh_next = _rms_bwd(dY, dhn, xs_in, gla_norm, pad, "gla_norm_bwd")
        dwin = jnp.concatenate([dwin_p[:c_lr], dwin_p[c_r:c_r + GATE_RANK], dwin_p[c_lr:c_r]], axis=0)
        small_grads.update(gla_w_lr=dwlr[:GATE_RANK][None], gla_b_lr=dblr, gla_head_norm=dhw, gla_norm=dgn)
        return dxs, dyh_next, Reduce("gla", [dwin.reshape(N_DEV, INW // N_DEV, D), dwout.reshape(N_DEV, DV // N_DEV, D)])

    def pool_b_(dY, prev):
        xs_in, pooled = saved["pool"]
        dp, dw, db, ds = _pool_mix_bwd(dY, pooled, wpool_full, pool_b_f.reshape(1, D), pool_scale_f, pad, "pool_mix_bwd",
                                       after=prev.token)
        dhn = _pool_windows_bwd(dp, pad, "pool_windows_bwd")
        dxs, dgn, dyh_next = _rms_bwd(dY, dhn, xs_in, pool_norm_f, pad, "pool_norm_bwd")
        prev.mid(dxs)
        dws = _to_shards(dw, 1)
        small_grads.update(pool_b=db.reshape(1, POOL_GROUPS, GW), pool_scale=ds, pool_norm=dgn)
        return dxs, dyh_next, Reduce("pool", [dws.reshape(N_DEV, POOL_GROUPS * GW // N_DEV, GW)], after=prev.token)

    sh_names = ["meta", "ffn_norm", "gla_w_lr", "pool_norm", "pool_b", "pool_scale"]
    rep_names = ["gla_norm", "gla_b_lr", "gla_head_norm", "final_norm"]
    rep_w = [gla_norm, gla_b_lr, gla_head_norm, final_norm]
    rep_shapes = [a.shape for a in rep_w]
    rep_rows = -(-sum(_rows(a.size) for a in rep_w) // 8) * 8

    def small_path(dxs0):
        small_grads.update(meta=dxs0[pad:first], ffn_norm=jnp.concatenate(d_ffn_norm, axis=0).reshape(n_units // 2, 2, D),
                           final_norm=d_final.reshape(D))
        by_owner = [_to_shards(small_grads[nm].reshape(full_shape), ax) for nm, full_shape, ax in zip(
            sh_names, [meta_f.shape, (ffn_norm.shape[0], 2, D), wlr_f.shape, pool_norm_f.shape, pool_b_f.shape, pool_scale_f.shape],
            small_axis)]
        rep_pack = _pack([small_grads[nm].reshape(s) for nm, s in zip(rep_names, rep_shapes)], rep_rows)
        send = jnp.stack([
            jnp.concatenate([_pack([g[d] for g in by_owner], sh_rows), rep_pack, loss_part], axis=0) for d in range(N_DEV)])
        total = _sum_blocks(_small_exchange(send, False, "small_reduce"), "small_sum")
        n_small = sh_rows + rep_rows

        def pack_small(sh_list, rep_list):
            return jnp.concatenate([_pack(sh_list, sh_rows), _pack(rep_list, rep_rows)], axis=0)[None]

        w_small = pack_small(small_sh, rep_w)
        m_small = pack_small([m_meta, m_ffn_norm, m_gla_w_lr, m_pool_norm, m_pool_b, m_pool_scale],
                             [m_gla_norm, m_gla_b_lr, m_gla_head_norm, m_final_norm])
        v_small = pack_small([v_meta, v_ffn_norm, v_gla_w_lr, v_pool_norm, v_pool_b, v_pool_scale],
                             [v_gla_norm, v_gla_b_lr, v_gla_head_norm, v_final_norm])
        small_out = _adamw(w_small, m_small, v_small, 0, total[:, :n_small], zero_idx, None, None, "adamw_small")
        small_res = {}
        for kind, packed in zip(("grad", "delta", "new_m", "new_v"), small_out):
            sh_vals = _unpack(packed[0, :sh_rows], sh_shapes)
            rep_vals = _unpack(packed[0, sh_rows:], rep_shapes)
            for nm, val in zip(sh_names + rep_names, sh_vals + rep_vals):
                small_res[(kind, nm)] = val
        return total[0, n_small, 0], small_res, small_out[0]

    def ffn_b_last(dY, dyh, prev):
        xs_in, h_, G, U = saved[("ffn", 0)]
        wg, wu, wd = ffn_w[0]
        dG, dU, A = _ffn_bwd_act(dyh, wd, G, U, "ffn_act0", after=prev.token)
        dwd = _ffn_bwd_wgrad(A, dyh, "ffn_wgrad_down0", after=prev.token)
        r_d = Reduce("ffn0_down", [dwd])
        dh = _ffn_bwd_dh(dG, dU, wg, wu, "ffn_dh0", after=r_d.token)
        dxs, dg, _ = _rms_bwd(dY, dh, xs_in, ffn_norm_f[0], pad, "ffn_norm_bwd0")
        d_ffn_norm[0] = dg
        small = small_path(dxs)
        prev.mid(small[2])
        r_d.mid(prev.token)
        dwg = _ffn_bwd_wgrad(dG, h_, "ffn_wgrad_gate0", after=r_d.token)
        r_g = Reduce("ffn0_gate", [dwg])
        dwu = _ffn_bwd_wgrad(dU, h_, "ffn_wgrad_up0", after=r_g.token)
        r_g.mid(dwu)
        r_u = Reduce("ffn0_up", [dwu], after=r_g.token)
        return dxs, small, (r_g, r_u, r_d)

    dxs, dyh, r3 = ffn_b(3, dxs, dyh, None)
    dxs, dyh, rp = pool_b_(dxs, r3)
    dxs, dyh, r2 = ffn_b(2, dxs, dyh, rp)
    dxs, dyh, r1 = ffn_b(1, dxs, dyh, r2)
    dxs, dyh, rg = gla_b(dxs, r1)
    dxs, (loss, small_res, _), r0 = ffn_b_last(dxs, dyh, rg)
    grad_x = dxs[first:].reshape(x.shape)
    r_last = r0[1]

    big_res = {}

    def adam_one(nm, w, m, v, entry, transposed=False):
        sums, recv = entry
        R, C = sums.shape[1:]
        w1, m1, v1 = ((t[0].T if transposed else t).reshape(1, R, C) for t in (w, m, v))
        out = _adamw(w1, m1, v1, 0, sums, q_idx, recv, None, f"adamw_{nm}", after=r_last.token)
        for kind, val in zip(("grad", "delta", "new_m", "new_v"), out):
            big_res[(kind, nm)] = val[0].T[None] if transposed else val.reshape(w.shape)
        return out[0]

    e_gla = rg.end(dxs)
    done = adam_one("gla_w_in", gla_w_in, m_gla_w_in, v_gla_w_in, e_gla[0], transposed=True)
    done = adam_one("gla_w_out", gla_w_out, m_gla_w_out, v_gla_w_out, e_gla[1])
    done = adam_one("pool_w", pool_w, m_pool_w, v_pool_w, rp.end(done)[0])
    r_last.mid(done)

    ffn_names = ["ffn_w_gate", "ffn_w_up", "ffn_w_down"]
    ffn_wmv = [tuple(t_units(t) for t in (ffn_w_gate, m_ffn_w_gate, v_ffn_w_gate)),
               tuple(t_units(t) for t in (ffn_w_up, m_ffn_w_up, v_ffn_w_up)),
               tuple(t.reshape(n_units, Fs, D) for t in (ffn_w_down, m_ffn_w_down, v_ffn_w_down))]
    ffn_prev = [[lax.empty((n_units, Fs, D), f32) for _ in range(4)] for _ in range(3)]
    order_after = r_last.token
    for u, red in ((3, r3), (2, r2), (1, r1), (0, r0)):
        entries = [r.end(done)[0] for r in red] if u == 0 else red.end(done)
        for a in range(3):
            sums, recv = entries[a]
            ffn_prev[a] = _adamw(*ffn_wmv[a], u, sums, q_idx, recv, ffn_prev[a], f"adamw_{ffn_names[a]}{u}", after=order_after)
            done = order_after = ffn_prev[a][0]
    for a in range(3):
        for kind, val in zip(("grad", "delta", "new_m", "new_v"), ffn_prev[a]):
            val = val.reshape(ffn_w_down.shape)
            big_res[(kind, ffn_names[a])] = val if a == 2 else jnp.swapaxes(val, -1, -2)

    order = ["meta", "ffn_norm", "ffn_w_gate", "ffn_w_up", "ffn_w_down", "gla_norm", "gla_w_in", "gla_w_lr", "gla_b_lr",
             "gla_head_norm", "gla_w_out", "pool_norm", "pool_w", "pool_b", "pool_scale", "final_norm"]
    res = {**small_res, **big_res}
    outs = [loss, grad_x]
    for kind in ("grad", "delta", "new_m", "new_v"):
        outs += [res[(kind, nm)] for nm in order]
    return tuple(outs)
```

```python
import functools

import jax
import jax.numpy as jnp
from jax import lax
from jax.experimental import pallas as pl
from jax.experimental.pallas import tpu as pltpu

f32 = jnp.float32
bf16 = jnp.bfloat16

N_DEV = 8
N_META = 16
GLA_HEADS = 4
GLA_CHUNK = 64
GLA_SUB = 16
GATE_RANK = 16
GATE_PAD = 128
GATE_NORM = 16.0
EPS = 1e-6
POOL_GROUPS = 4
ADAM_LR = 0.001
ADAM_B1 = 0.9
ADAM_B2 = 0.999
ADAM_EPS = 1e-08
ADAM_WD = 0.01
ADAM_STEP = 10
LANES = 128
VMEM_LIMIT_MB = 56

NN = (((1,), (0,)), ((), ()))
NT = (((1,), (1,)), ((), ()))
TN = (((0,), (0,)), ((), ()))
HI = lax.Precision.HIGHEST
MESH = pl.DeviceIdType.MESH
ANY = pl.BlockSpec(memory_space=pl.ANY)


def _cparams(sem=None, vmem_mb=None):
    kw = {}
    if sem is not None:
        kw["dimension_semantics"] = sem
    if vmem_mb is not None:
        kw["vmem_limit_bytes"] = vmem_mb * 2 ** 20
    return pltpu.CompilerParams(**kw)


def _tile(n, target, mult=16):
    best = None
    for t in range(mult, min(n, target) + 1, mult):
        if n % t == 0:
            best = t
    assert best is not None, (n, target, mult)
    return best


def _tile2(R, C, rows, mult):
    if R % mult == 0:
        return _tile(R, rows, mult), C
    return R, _tile(C, 256, LANES)


def _dot(a, b, dims=NN, precision=None):
    return lax.dot_general(a, b, dims, preferred_element_type=f32, precision=precision)


def _sigmoid(x):
    return 1.0 / (1.0 + jnp.exp(-x))


def _row_ids(tile_index, tm):
    return tile_index * tm + lax.broadcasted_iota(jnp.int32, (tm, 1), 0)


def _ordered(body, in_specs, args, after, lead=0):
    if after is None:
        return body, in_specs, args
    pos = lead + len(args)

    def body_without(*refs):
        return body(*refs[:pos], *refs[pos + 1:])

    return body_without, list(in_specs) + [ANY], list(args) + [after]


def _cast_unit(w, unit, name, after=None):
    _, R, C = w.shape
    tr, tc = _tile2(R, C, 256, 16)

    def body(w_ref, o_ref):
        o_ref[...] = w_ref[0].astype(bf16)

    body, in_specs, args = _ordered(body, [pl.BlockSpec((1, tr, tc), lambda i, j: (unit, i, j))], [w], after)
    return pl.pallas_call(
        body, name=name, grid=(R // tr, C // tc),
        in_specs=in_specs, out_specs=pl.BlockSpec((tr, tc), lambda i, j: (i, j)),
        out_shape=jax.ShapeDtypeStruct((R, C), bf16),
        compiler_params=_cparams(("parallel", "parallel")),
    )(*args)


def _rms_fwd(xs, g, out_dtype, name):
    Lp, D = xs.shape
    tm = _tile(Lp, 528)

    def body(x_ref, g_ref, h_ref):
        x = x_ref[...]
        rstd = lax.rsqrt(jnp.mean(x * x, axis=-1, keepdims=True) + EPS)
        h_ref[...] = (x * rstd * g_ref[...]).astype(out_dtype)

    return pl.pallas_call(
        body, name=name, grid=(Lp // tm,),
        in_specs=[pl.BlockSpec((tm, D), lambda i: (i, 0)), pl.BlockSpec((1, D), lambda i: (0, 0))],
        out_specs=pl.BlockSpec((tm, D), lambda i: (i, 0)),
        out_shape=jax.ShapeDtypeStruct((Lp, D), out_dtype),
        compiler_params=_cparams(("parallel",)),
    )(xs, g)


def _rms_bwd(dY, dh, xs, g, pad, name):
    Lp, D = xs.shape
    tm = _tile(Lp, 352)

    def body(dY_ref, dh_ref, x_ref, g_ref, dxs_ref, dg_ref, half_ref):
        i = pl.program_id(0)

        @pl.when(i == 0)
        def _():
            dg_ref[...] = jnp.zeros_like(dg_ref)

        x = x_ref[...]
        rstd = lax.rsqrt(jnp.mean(x * x, axis=-1, keepdims=True) + EPS)
        xhat = x * rstd
        dh_ = dh_ref[...]
        dg_ref[...] += jnp.sum(dh_ * xhat, axis=0, keepdims=True)
        dxh = dh_ * g_ref[...]
        dx = rstd * (dxh - xhat * jnp.mean(dxh * xhat, axis=-1, keepdims=True))
        out = jnp.where(_row_ids(i, tm) >= pad, dY_ref[...] + dx, 0.0)
        dxs_ref[...] = out
        half_ref[...] = (0.5 * out).astype(bf16)

    row = pl.BlockSpec((tm, D), lambda i: (i, 0))
    vec = pl.BlockSpec((1, D), lambda i: (0, 0))
    return pl.pallas_call(
        body, name=name, grid=(Lp // tm,),
        in_specs=[row, row, row, vec], out_specs=[row, vec, row],
        out_shape=[jax.ShapeDtypeStruct((Lp, D), f32), jax.ShapeDtypeStruct((1, D), f32), jax.ShapeDtypeStruct((Lp, D), bf16)],
        compiler_params=_cparams(("arbitrary",)),
    )(dY, dh, xs, g)


def _mm(a, b, mode, out_dtype, name, tm=512, tn=512, tk=512, residual=None, after=None):
    if mode == "nn":
        (M, K), N = a.shape, b.shape[1]
    elif mode == "nt":
        (M, K), N = a.shape, b.shape[0]
    else:
        (K, M), N = a.shape, b.shape[1]
    tm = _tile(M, tm, 16 if mode != "tn" else LANES) if M > tm else M
    tn = _tile(N, tn, LANES) if N > tn else N
    tk = _tile(K, tk, LANES if mode != "tn" else 16) if K > tk else K
    nk = K // tk
    dims = {"nn": NN, "nt": NT, "tn": TN}[mode]

    def body(*refs):
        if residual is None:
            a_ref, b_ref, o_ref, acc = refs
            r_ref = None
        else:
            a_ref, b_ref, r_ref, o_ref, acc = refs
        k = pl.program_id(2)

        @pl.when(k == 0)
        def _():
            acc[...] = jnp.zeros_like(acc)

        acc[...] += _dot(a_ref[...], b_ref[...], dims)

        @pl.when(k == nk - 1)
        def _():
            r = acc[...]
            if r_ref is not None:
                r = r + r_ref[...]
            o_ref[...] = r.astype(out_dtype)

    a_spec = pl.BlockSpec((tk, tm), lambda i, j, k: (k, i)) if mode == "tn" else pl.BlockSpec((tm, tk), lambda i, j, k: (i, k))
    b_spec = pl.BlockSpec((tn, tk), lambda i, j, k: (j, k)) if mode == "nt" else pl.BlockSpec((tk, tn), lambda i, j, k: (k, j))
    o_spec = pl.BlockSpec((tm, tn), lambda i, j, k: (i, j))
    in_specs = [a_spec, b_spec] + ([o_spec] if residual is not None else [])
    args = [a, b] + ([residual] if residual is not None else [])
    body, in_specs, args = _ordered(body, in_specs, args, after)
    return pl.pallas_call(
        body, name=name, grid=(M // tm, N // tn, nk),
        in_specs=in_specs, out_specs=o_spec,
        out_shape=jax.ShapeDtypeStruct((M, N), out_dtype),
        scratch_shapes=[pltpu.VMEM((tm, tn), f32)],
        compiler_params=_cparams(("parallel", "parallel", "arbitrary"), VMEM_LIMIT_MB),
    )(*args)


def _mm_tn_full(a, b, name, tm, after=None):
    K, M = a.shape
    N = b.shape[1]
    tm = _tile(M, tm, LANES)

    def body(a_ref, b_ref, o_ref):
        o_ref[...] = _dot(a_ref[...], b_ref[...], TN).astype(bf16)

    in_specs = [pl.BlockSpec((K, tm), lambda i: (0, i)), pl.BlockSpec((K, N), lambda i: (0, 0), pipeline_mode=pl.Buffered(1))]
    body, in_specs, args = _ordered(body, in_specs, [a, b], after)
    return pl.pallas_call(
        body, name=name, grid=(M // tm,),
        in_specs=in_specs, out_specs=pl.BlockSpec((tm, N), lambda i: (i, 0)),
        out_shape=jax.ShapeDtypeStruct((M, N), bf16),
        compiler_params=_cparams(("parallel",), VMEM_LIMIT_MB),
    )(*args)


def _ffn_fwd(xs, g, wg, wu, wd, name):
    Lp, D = xs.shape
    nd, Fs, _ = wg.shape
    tm = _tile(Lp, 704)
    once = pl.Buffered(1)

    def body(x_ref, g_ref, wg_ref, wu_ref, wd_ref, out_ref, h_ref, G_ref, U_ref, hs, acc):
        j = pl.program_id(1)

        @pl.when(j == 0)
        def _():
            x = x_ref[...]
            rstd = lax.rsqrt(jnp.mean(x * x, axis=-1, keepdims=True) + EPS)
            h = (x * rstd * g_ref[...]).astype(bf16)
            hs[...] = h
            h_ref[...] = h
            acc[...] = jnp.zeros_like(acc)

        h = hs[...]
        G = _dot(h, wg_ref[0], NT)
        U = _dot(h, wu_ref[0], NT)
        G_ref[0] = G.astype(bf16)
        U_ref[0] = U.astype(bf16)
        A = (G * _sigmoid(G) * U).astype(bf16)
        acc[...] += _dot(A, wd_ref[0])

        @pl.when(j == nd - 1)
        def _():
            out_ref[...] = x_ref[...] + 0.5 * acc[...]

    row_f = pl.BlockSpec((tm, D), lambda i, j: (i, 0), pipeline_mode=once)
    act = pl.BlockSpec((1, tm, Fs), lambda i, j: (j, i, 0))
    return pl.pallas_call(
        body, name=name, grid=(Lp // tm, nd),
        in_specs=[row_f, pl.BlockSpec((1, D), lambda i, j: (0, 0)),
                  pl.BlockSpec((1, Fs, D), lambda i, j: (j, 0, 0)),
                  pl.BlockSpec((1, Fs, D), lambda i, j: (j, 0, 0)),
                  pl.BlockSpec((1, Fs, D), lambda i, j: (j, 0, 0))],
        out_specs=[row_f, pl.BlockSpec((tm, D), lambda i, j: (i, 0), pipeline_mode=once), act, act],
        out_shape=[jax.ShapeDtypeStruct((Lp, D), f32), jax.ShapeDtypeStruct((Lp, D), bf16),
                   jax.ShapeDtypeStruct((nd, Lp, Fs), bf16), jax.ShapeDtypeStruct((nd, Lp, Fs), bf16)],
        scratch_shapes=[pltpu.VMEM((tm, D), bf16), pltpu.VMEM((tm, D), f32)],
        compiler_params=_cparams(("parallel", "arbitrary"), VMEM_LIMIT_MB),
    )(xs, g, wg, wu, wd)


def _ffn_bwd_act(dyh, wd, G, U, name, after=None):
    Lp, D = dyh.shape
    nd, Fs, _ = wd.shape
    tm = _tile(Lp, 704)

    def body(dyh_ref, wd_ref, G_ref, U_ref, dG_ref, dU_ref, A_ref):
        dA = _dot(dyh_ref[...], wd_ref[0], NT)
        Gf = G_ref[0].astype(f32)
        Uf = U_ref[0].astype(f32)
        s = _sigmoid(Gf)
        silu = Gf * s
        dG_ref[0] = (dA * Uf * (s * (1.0 + Gf * (1.0 - s)))).astype(bf16)
        dU_ref[0] = (dA * silu).astype(bf16)
        A_ref[0] = (silu * Uf).astype(bf16)

    act = pl.BlockSpec((1, tm, Fs), lambda j, i: (j, i, 0))
    act_s = jax.ShapeDtypeStruct((nd, Lp, Fs), bf16)
    in_specs = [pl.BlockSpec((tm, D), lambda j, i: (i, 0)), pl.BlockSpec((1, Fs, D), lambda j, i: (j, 0, 0)), act, act]
    body, in_specs, args = _ordered(body, in_specs, [dyh, wd, G, U], after)
    return pl.pallas_call(
        body, name=name, grid=(nd, Lp // tm),
        in_specs=in_specs, out_specs=[act, act, act], out_shape=[act_s, act_s, act_s],
        compiler_params=_cparams(("parallel", "parallel"), VMEM_LIMIT_MB),
    )(*args)


def _ffn_bwd_dh(dG, dU, wg, wu, name, after=None):
    nd, Lp, Fs = dG.shape
    D = wg.shape[2]
    tm = _tile(Lp, 1056)

    def body(dG_ref, dU_ref, wg_ref, wu_ref, dh_ref, acc):
        j = pl.program_id(1)

        @pl.when(j == 0)
        def _():
            acc[...] = jnp.zeros_like(acc)

        acc[...] += _dot(dG_ref[0], wg_ref[0]) + _dot(dU_ref[0], wu_ref[0])

        @pl.when(j == nd - 1)
        def _():
            dh_ref[...] = acc[...]

    act = pl.BlockSpec((1, tm, Fs), lambda i, j: (j, i, 0))
    wrow = pl.BlockSpec((1, Fs, D), lambda i, j: (j, 0, 0))
    body, in_specs, args = _ordered(body, [act, act, wrow, wrow], [dG, dU, wg, wu], after)
    return pl.pallas_call(
        body, name=name, grid=(Lp // tm, nd),
        in_specs=in_specs,
        out_specs=pl.BlockSpec((tm, D), lambda i, j: (i, 0), pipeline_mode=pl.Buffered(1)),
        out_shape=jax.ShapeDtypeStruct((Lp, D), f32),
        scratch_shapes=[pltpu.VMEM((tm, D), f32)],
        compiler_params=_cparams(("parallel", "arbitrary"), VMEM_LIMIT_MB),
    )(*args)


def _ffn_bwd_wgrad(act, rows, name, after=None):
    nd, Lp, Fs = act.shape
    D = rows.shape[1]

    def body(a_ref, r_ref, o_ref):
        o_ref[0] = _dot(a_ref[0], r_ref[...], TN).astype(bf16)

    in_specs = [pl.BlockSpec((1, Lp, Fs), lambda j: (j, 0, 0)),
                pl.BlockSpec((Lp, D), lambda j: (0, 0), pipeline_mode=pl.Buffered(1))]
    body, in_specs, args = _ordered(body, in_specs, [act, rows], after)
    return pl.pallas_call(
        body, name=name, grid=(nd,),
        in_specs=in_specs, out_specs=pl.BlockSpec((1, Fs, D), lambda j: (j, 0, 0)),
        out_shape=jax.ShapeDtypeStruct((nd, Fs, D), bf16),
        compiler_params=_cparams(("parallel",), VMEM_LIMIT_MB),
    )(*args)


def _gate_fwd(proj, wlr, blr, pad, gate_blk, name):
    Lp = proj.shape[0]
    DK = wlr.shape[1]
    tm = _tile(Lp, 528)

    def body(lr_ref, w_ref, b_ref, lg_ref):
        z = _dot(lr_ref[...].astype(bf16), w_ref[...].astype(bf16)) + b_ref[...]
        ls = jnp.minimum(z, 0.0) - jnp.log(1.0 + jnp.exp(-jnp.abs(z)))
        lg_ref[...] = jnp.where(_row_ids(pl.program_id(0), tm) >= pad, ls * (1.0 / GATE_NORM), 0.0)

    return pl.pallas_call(
        body, name=name, grid=(Lp // tm,),
        in_specs=[pl.BlockSpec((tm, GATE_PAD), lambda i: (i, gate_blk)),
                  pl.BlockSpec((GATE_PAD, DK), lambda i: (0, 0)), pl.BlockSpec((1, DK), lambda i: (0, 0))],
        out_specs=pl.BlockSpec((tm, DK), lambda i: (i, 0)),
        out_shape=jax.ShapeDtypeStruct((Lp, DK), f32),
        compiler_params=_cparams(("parallel",)),
    )(proj, wlr, blr)


def _gate_bwd(dlg, proj, wlr, blr, pad, gate_blk, name):
    Lp = proj.shape[0]
    DK = wlr.shape[1]
    tm = _tile(Lp, 528)

    def body(dlg_ref, lr_ref, w_ref, b_ref, dlr_ref, dw_ref, db_ref):
        i = pl.program_id(0)

        @pl.when(i == 0)
        def _():
            dw_ref[...] = jnp.zeros_like(dw_ref)
            db_ref[...] = jnp.zeros_like(db_ref)

        lr = lr_ref[...].astype(bf16)
        w = w_ref[...].astype(bf16)
        z = _dot(lr, w) + b_ref[...]
        dz = jnp.where(_row_ids(i, tm) >= pad, dlg_ref[...] * _sigmoid(-z) * (1.0 / GATE_NORM), 0.0)
        dzb = dz.astype(bf16)
        dlr_ref[...] = _dot(dzb, w, NT).astype(bf16)
        dw_ref[...] += _dot(lr, dzb, TN)
        db_ref[...] += jnp.sum(dz, axis=0, keepdims=True)

    return pl.pallas_call(
        body, name=name, grid=(Lp // tm,),
        in_specs=[pl.BlockSpec((tm, DK), lambda i: (i, 0)), pl.BlockSpec((tm, GATE_PAD), lambda i: (i, gate_blk)),
                  pl.BlockSpec((GATE_PAD, DK), lambda i: (0, 0)), pl.BlockSpec((1, DK), lambda i: (0, 0))],
        out_specs=[pl.BlockSpec((tm, GATE_PAD), lambda i: (i, 0)), pl.BlockSpec((GATE_PAD, DK), lambda i: (0, 0)),
                   pl.BlockSpec((1, DK), lambda i: (0, 0))],
        out_shape=[jax.ShapeDtypeStruct((Lp, GATE_PAD), bf16), jax.ShapeDtypeStruct((GATE_PAD, DK), f32),
                   jax.ShapeDtypeStruct((1, DK), f32)],
        compiler_params=_cparams(("arbitrary",)),
    )(dlg, proj, wlr, blr)


def _chunk_decay(lg):
    C = lg.shape[0]
    r = lax.broadcasted_iota(jnp.int32, (C, C), 0)
    c = lax.broadcasted_iota(jnp.int32, (C, C), 1)
    return _dot(jnp.where(r >= c, 1.0, 0.0).astype(f32), lg, NN, HI)


def _col(v):
    return jnp.transpose(jnp.broadcast_to(v, (8, v.shape[1])))[:, 0:1]


def _intra_scores(q, k, b, A_ref):
    C = q.shape[0]
    S = GLA_SUB
    A_ref[...] = jnp.zeros_like(A_ref)
    ri = lax.broadcasted_iota(jnp.int32, (S, 1), 0)
    for I in range(C // S):
        lo = S * I
        qI, bI = q[lo:lo + S], b[lo:lo + S]
        if I > 0:
            bref = b[lo - 1:lo]
            qs = qI * jnp.exp(bI - bref)
            ks = k[:lo] * jnp.exp(bref - b[:lo])
            A_ref[lo:lo + S, 0:lo] = _dot(qs, ks, NT, HI)
        for jj in range(S):
            j = lo + jj
            P = jnp.exp(jnp.minimum(bI - b[j:j + 1], 0.0))
            a = jnp.sum(qI * P * k[j:j + 1], axis=1, keepdims=True)
            A_ref[lo:lo + S, j:j + 1] = jnp.where(ri >= jj, a, 0.0)


def _intra_grads(q, k, b, dA, dq_ref, dk_ref):
    C = q.shape[0]
    S = GLA_SUB
    ri = lax.broadcasted_iota(jnp.int32, (S, 1), 0)
    for I in range(C // S):
        lo = S * I
        qI, bI = q[lo:lo + S], b[lo:lo + S]
        dqI = jnp.zeros_like(qI)
        if I > 0:
            bref = b[lo - 1:lo]
            eq = jnp.exp(bI - bref)
            ek = jnp.exp(bref - b[:lo])
            qs = qI * eq
            ks = k[:lo] * ek
            dAI = dA[lo:lo + S, 0:lo]
            dqI = dqI + _dot(dAI, ks, NN, HI) * eq
            dk_ref[0:lo, :] += _dot(dAI, qs, TN, HI) * ek
        for jj in range(S):
            j = lo + jj
            P = jnp.exp(jnp.minimum(bI - b[j:j + 1], 0.0))
            t = jnp.where(ri >= jj, dA[lo:lo + S, j:j + 1], 0.0) * P
            dqI = dqI + t * k[j:j + 1]
            dk_ref[j:j + 1, :] += jnp.sum(t * qI, axis=0, keepdims=True)
        dq_ref[lo:lo + S, :] += dqI


GLA_HEADS_PER_STEP = 2


def _gla_fwd(proj, lg, hnw, H, name):
    Lp = proj.shape[0]
    DK = lg.shape[1]
    hk = DK // H
    hv = hnw.shape[1]
    DV = hv * H
    C = GLA_CHUNK
    NC = Lp // C
    HS = GLA_HEADS_PER_STEP
    G = H // HS
    scale = float(hk) ** -0.5
    kq, kv, kr = G, (2 * DK) // (HS * hv), (2 * DK) // (HS * hv) + G

    def body(q_ref, k_ref, v_ref, r_ref, lg_ref, w_ref, o_ref, y_ref, s_ref, S_scr, A_scr):
        c = pl.program_id(1)

        @pl.when(c == 0)
        def _():
            S_scr[...] = jnp.zeros_like(S_scr)

        for hh in range(HS):
            ck, cv = slice(hh * hk, (hh + 1) * hk), slice(hh * hv, (hh + 1) * hv)
            q = q_ref[:, ck] * scale
            k = k_ref[:, ck]
            v = v_ref[:, cv]
            b = _chunk_decay(lg_ref[:, ck])
            bl = b[C - 1:C]
            S = S_scr[hh]
            s_ref[hh, 0] = S
            _intra_scores(q, k, b, A_scr.at[hh])
            vb = v.astype(bf16)
            o = _dot((q * jnp.exp(b)).astype(bf16), S.astype(bf16)) + _dot(A_scr[hh].astype(bf16), vb)
            kb = (k * jnp.exp(bl - b)).astype(bf16)
            S_scr[hh] = jnp.exp(_col(bl)) * S + _dot(kb, vb, TN)
            o_ref[:, cv] = o
            on = o * lax.rsqrt(jnp.mean(o * o, axis=-1, keepdims=True) + EPS) * w_ref[...]
            r = r_ref[:, cv]
            y_ref[:, cv] = (on * (r * _sigmoid(r))).astype(bf16)

    return pl.pallas_call(
        body, name=name, grid=(G, NC),
        in_specs=[pl.BlockSpec((C, HS * hk), lambda g, c: (c, g)),
                  pl.BlockSpec((C, HS * hk), lambda g, c: (c, kq + g)),
                  pl.BlockSpec((C, HS * hv), lambda g, c: (c, kv + g)),
                  pl.BlockSpec((C, HS * hv), lambda g, c: (c, kr + g)),
                  pl.BlockSpec((C, HS * hk), lambda g, c: (c, g)),
                  pl.BlockSpec((1, hv), lambda g, c: (0, 0))],
        out_specs=[pl.BlockSpec((C, HS * hv), lambda g, c: (c, g)), pl.BlockSpec((C, HS * hv), lambda g, c: (c, g)),
                   pl.BlockSpec((HS, 1, hk, hv), lambda g, c: (g, c, 0, 0))],
        out_shape=[jax.ShapeDtypeStruct((Lp, DV), f32), jax.ShapeDtypeStruct((Lp, DV), bf16),
                   jax.ShapeDtypeStruct((H, NC, hk, hv), f32)],
        scratch_shapes=[pltpu.VMEM((HS, hk, hv), f32), pltpu.VMEM((HS, C, C), f32)],
        compiler_params=_cparams(("parallel", "arbitrary")),
    )(proj, proj, proj, proj, lg, hnw)


def _gla_bwd(dy, proj, lg, o, states, hnw, H, pad, name, after=None):
    Lp = proj.shape[0]
    DK = lg.shape[1]
    hk = DK // H
    hv = hnw.shape[1]
    DV = hv * H
    C = GLA_CHUNK
    NC = Lp // C
    HS = GLA_HEADS_PER_STEP
    G = H // HS
    scale = float(hk) ** -0.5
    kq, kv, kr = G, (2 * DK) // (HS * hv), (2 * DK) // (HS * hv) + G

    def body(dy_ref, q_ref, k_ref, v_ref, r_ref, lg_ref, o_ref, s_ref, sn_ref, w_ref,
             dq_ref, dk_ref, dv_ref, dr_ref, dlg_ref, dw_ref, dS_scr, A_scr, dq_s, dk_s):
        g = pl.program_id(0)
        cc = pl.program_id(1)
        c = NC - 1 - cc

        @pl.when(cc == 0)
        def _():
            dS_scr[...] = jnp.zeros_like(dS_scr)

        @pl.when((cc == 0) & (g == 0))
        def _():
            dw_ref[...] = jnp.zeros_like(dw_ref)

        keep = (c * C + lax.broadcasted_iota(jnp.int32, (C, 1), 0)) >= pad
        ri = lax.broadcasted_iota(jnp.int32, (C, C), 0)
        ci = lax.broadcasted_iota(jnp.int32, (C, C), 1)
        w = w_ref[...]
        for hh in range(HS):
            ck, cv = slice(hh * hk, (hh + 1) * hk), slice(hh * hv, (hh + 1) * hv)
            o_ = o_ref[:, cv]
            rs = lax.rsqrt(jnp.mean(o_ * o_, axis=-1, keepdims=True) + EPS)
            ohat = o_ * rs
            r = r_ref[:, cv]
            sg = _sigmoid(r)
            dy_ = dy_ref[:, cv]
            d_on = dy_ * (r * sg)
            dr_ref[:, cv] = jnp.where(keep, dy_ * (ohat * w) * (sg * (1.0 + r * (1.0 - sg))), 0.0).astype(bf16)
            dw_ref[...] += jnp.sum(d_on * ohat, axis=0, keepdims=True)
            d_oh = d_on * w
            do = rs * (d_oh - ohat * jnp.mean(d_oh * ohat, axis=-1, keepdims=True))
            dob = do.astype(bf16)
            q = q_ref[:, ck] * scale
            k = k_ref[:, ck]
            vb = v_ref[:, cv].astype(bf16)
            b = _chunk_decay(lg_ref[:, ck])
            bl = b[C - 1:C]
            eb = jnp.exp(b)
            ekb = jnp.exp(bl - b)
            S = s_ref[hh, 0]
            dS = dS_scr[hh]
            dSb = dS.astype(bf16)
            _intra_scores(q, k, b, A_scr.at[hh])
            dA = jnp.where(ri >= ci, _dot(dob, vb, NT), 0.0)
            kb = (k * ekb).astype(bf16)
            qb = (q * eb).astype(bf16)
            dv = _dot(A_scr[hh].astype(bf16), dob, TN) + _dot(kb, dSb)
            dq_s[hh] = _dot(dob, S.astype(bf16), NT) * eb
            dk_s[hh] = _dot(vb, dSb, NT) * ekb
            dS_scr[hh] = _dot(qb, dob, TN) + jnp.exp(_col(bl)) * dS
            _intra_grads(q, k, b, dA, dq_s.at[hh], dk_s.at[hh])
            dq = dq_s[hh]
            dk = dk_s[hh]
            Dm = q * dq - k * dk
            after_rows = _dot(jnp.ones((8, hv), f32), sn_ref[hh, 0] * dS, NT, HI)[0:1]
            dlg = _dot(jnp.where(ri <= ci, 1.0, 0.0).astype(f32), Dm, NN, HI) + after_rows
            dlg_ref[:, ck] = jnp.where(keep, dlg, 0.0)
            dq_ref[:, ck] = jnp.where(keep, dq * scale, 0.0).astype(bf16)
            dk_ref[:, ck] = jnp.where(keep, dk, 0.0).astype(bf16)
            dv_ref[:, cv] = jnp.where(keep, dv, 0.0).astype(bf16)

    rev = lambda cc: NC - 1 - cc
    bk = lambda off: pl.BlockSpec((C, HS * hk), lambda g, cc: (rev(cc), off + g))
    bv = lambda off: pl.BlockSpec((C, HS * hv), lambda g, cc: (rev(cc), off + g))
    in_specs = [bv(0), bk(0), bk(kq), bv(kv), bv(kr), bk(0), bv(0),
                pl.BlockSpec((HS, 1, hk, hv), lambda g, cc: (g, rev(cc), 0, 0)),
                pl.BlockSpec((HS, 1, hk, hv), lambda g, cc: (g, jnp.minimum(rev(cc) + 1, NC - 1), 0, 0)),
                pl.BlockSpec((1, hv), lambda g, cc: (0, 0))]
    body, in_specs, args = _ordered(body, in_specs, [dy, proj, proj, proj, proj, lg, o, states, states, hnw], after)
    return pl.pallas_call(
        body, name=name, grid=(G, NC),
        in_specs=in_specs,
        out_specs=[bk(0), bk(0), bv(0), bv(0), bk(0), pl.BlockSpec((1, hv), lambda g, cc: (0, 0))],
        out_shape=[jax.ShapeDtypeStruct((Lp, DK), bf16), jax.ShapeDtypeStruct((Lp, DK), bf16),
                   jax.ShapeDtypeStruct((Lp, DV), bf16), jax.ShapeDtypeStruct((Lp, DV), bf16),
                   jax.ShapeDtypeStruct((Lp, DK), f32), jax.ShapeDtypeStruct((1, hv), f32)],
        scratch_shapes=[pltpu.VMEM((HS, hk, hv), f32), pltpu.VMEM((HS, C, C), f32),
                        pltpu.VMEM((HS, C, hk), f32), pltpu.VMEM((HS, C, hk), f32)],
        compiler_params=_cparams(("arbitrary", "arbitrary")),
    )(*args)


def _window_sums(x, back):
    n = x.shape[0]
    out = []
    s = x
    for w in (1, 2, 4, 8):
        s = s + pltpu.roll(s, w if back else n - w, 0)
        out.append(s)
    return out


def _pool_windows(hn, pad, n_real, name):
    Lp, D = hn.shape
    GW = D // POOL_GROUPS
    cb = min(GW, 256)
    per = GW // cb

    def body(h_ref, p_ref):
        g = pl.program_id(0) // per
        x = h_ref[...]
        s2, s4, s8, s16 = _window_sums(x, True)
        sel = jnp.where(g == 0, s2, jnp.where(g == 1, s4, jnp.where(g == 2, s8, s16)))
        win = jnp.left_shift(2, g).astype(f32)
        rows = lax.broadcasted_iota(jnp.int32, (Lp, 1), 0)
        t = (rows - pad).astype(f32)
        cnt = jnp.minimum(jnp.maximum(t, 0.0) + 1.0, win)
        p_ref[...] = jnp.where(rows >= pad, sel / cnt - x, 0.0).astype(bf16)

    return pl.pallas_call(
        body, name=name, grid=(D // cb,),
        in_specs=[pl.BlockSpec((Lp, cb), lambda i: (0, i))],
        out_specs=pl.BlockSpec((Lp, cb), lambda i: (0, i)),
        out_shape=jax.ShapeDtypeStruct((Lp, D), bf16),
        compiler_params=_cparams(("parallel",)),
    )(hn)


def _pool_windows_bwd(dp, pad, name):
    Lp, D = dp.shape
    GW = D // POOL_GROUPS
    cb = min(GW, 256)
    per = GW // cb

    def body(dp_ref, dh_ref):
        g = pl.program_id(0) // per
        rows = lax.broadcasted_iota(jnp.int32, (Lp, 1), 0)
        d = jnp.where(rows >= pad, dp_ref[...], 0.0)
        win = jnp.left_shift(2, g).astype(f32)
        t = (rows - pad).astype(f32)
        cnt = jnp.minimum(jnp.maximum(t, 0.0) + 1.0, win)
        s2, s4, s8, s16 = _window_sums(d / cnt, False)
        sel = jnp.where(g == 0, s2, jnp.where(g == 1, s4, jnp.where(g == 2, s8, s16)))
        dh_ref[...] = jnp.where(rows >= pad, sel - d, 0.0)

    return pl.pallas_call(
        body, name=name, grid=(D // cb,),
        in_specs=[pl.BlockSpec((Lp, cb), lambda i: (0, i))],
        out_specs=pl.BlockSpec((Lp, cb), lambda i: (0, i)),
        out_shape=jax.ShapeDtypeStruct((Lp, D), f32),
        compiler_params=_cparams(("parallel",)),
    )(dp)


def _pool_mix_fwd(xs, pooled, w, bias, scale, pad, name):
    Lp, D = xs.shape
    GW = D // POOL_GROUPS
    tm = _tile(Lp, 1056)

    def body(x_ref, p_ref, w_ref, b_ref, s_ref, o_ref):
        z = _dot(p_ref[...], w_ref[0]) + b_ref[...]
        keep = _row_ids(pl.program_id(1), tm) >= pad
        o_ref[...] = x_ref[...] + jnp.where(keep, z * s_ref[...], 0.0)

    blk = pl.BlockSpec((tm, GW), lambda g, i: (i, g))
    vec = pl.BlockSpec((1, GW), lambda g, i: (0, g))
    return pl.pallas_call(
        body, name=name, grid=(POOL_GROUPS, Lp // tm),
        in_specs=[blk, blk, pl.BlockSpec((1, GW, GW), lambda g, i: (g, 0, 0)), vec, vec],
        out_specs=blk, out_shape=jax.ShapeDtypeStruct((Lp, D), f32),
        compiler_params=_cparams(("parallel", "parallel")),
    )(xs, pooled, w, bias, scale)


def _pool_mix_bwd(dY, pooled, w, bias, scale, pad, name, after=None):
    Lp, D = dY.shape
    GW = D // POOL_GROUPS
    tm = _tile(Lp, 1056)
    nm = Lp // tm

    def body(dY_ref, p_ref, w_ref, b_ref, s_ref, dp_ref, dw_ref, db_ref, ds_ref, acc):
        i = pl.program_id(1)

        @pl.when(i == 0)
        def _():
            acc[...] = jnp.zeros_like(acc)
            db_ref[...] = jnp.zeros_like(db_ref)
            ds_ref[...] = jnp.zeros_like(ds_ref)

        keep = _row_ids(i, tm) >= pad
        dY_ = jnp.where(keep, dY_ref[...], 0.0)
        p = p_ref[...]
        z = _dot(p, w_ref[0]) + b_ref[...]
        ds_ref[...] += jnp.sum(dY_ * z, axis=0, keepdims=True)
        dz = dY_ * s_ref[...]
        db_ref[...] += jnp.sum(dz, axis=0, keepdims=True)
        dzb = dz.astype(bf16)
        acc[...] += _dot(p, dzb, TN)
        dp_ref[...] = _dot(dzb, w_ref[0], NT)

        @pl.when(i == nm - 1)
        def _():
            dw_ref[0] = acc[...].astype(bf16)

    blk = pl.BlockSpec((tm, GW), lambda g, i: (i, g))
    vec = pl.BlockSpec((1, GW), lambda g, i: (0, g))
    wsp = pl.BlockSpec((1, GW, GW), lambda g, i: (g, 0, 0))
    body, in_specs, args = _ordered(body, [blk, blk, wsp, vec, vec], [dY, pooled, w, bias, scale], after)
    return pl.pallas_call(
        body, name=name, grid=(POOL_GROUPS, nm),
        in_specs=in_specs, out_specs=[blk, wsp, vec, vec],
        out_shape=[jax.ShapeDtypeStruct((Lp, D), f32), jax.ShapeDtypeStruct((POOL_GROUPS, GW, GW), bf16),
                   jax.ShapeDtypeStruct((1, D), f32), jax.ShapeDtypeStruct((1, D), f32)],
        scratch_shapes=[pltpu.VMEM((GW, GW), f32)],
        compiler_params=_cparams(("parallel", "arbitrary")),
    )(*args)


def _loss_head(xs, target, g, first, name):
    Lp, D = xs.shape
    tm = GLA_CHUNK
    off = first // tm

    def body(x_ref, t_ref, g_ref, loss_ref, dxs_ref, dg_ref, half_ref):
        i = pl.program_id(0)

        @pl.when(i == 0)
        def _():
            loss_ref[...] = jnp.zeros_like(loss_ref)
            dg_ref[...] = jnp.zeros_like(dg_ref)

        @pl.when(i < off)
        def _():
            dxs_ref[...] = jnp.zeros_like(dxs_ref)
            half_ref[...] = jnp.zeros_like(half_ref)

        @pl.when(i >= off)
        def _():
            x = x_ref[...]
            rstd = lax.rsqrt(jnp.mean(x * x, axis=-1, keepdims=True) + EPS)
            xhat = x * rstd
            gg = g_ref[...]
            err = xhat * gg - t_ref[...]
            loss_ref[...] += 0.5 * jnp.sum(jnp.mean(err * err, axis=-1, keepdims=True))
            dy = err * (1.0 / D)
            dg_ref[...] += jnp.sum(dy * xhat, axis=0, keepdims=True)
            dxh = dy * gg
            out = rstd * (dxh - xhat * jnp.mean(dxh * xhat, axis=-1, keepdims=True))
            dxs_ref[...] = out
            half_ref[...] = (0.5 * out).astype(bf16)

    row = pl.BlockSpec((tm, D), lambda i: (i, 0))
    return pl.pallas_call(
        body, name=name, grid=(Lp // tm,),
        in_specs=[row, pl.BlockSpec((tm, D), lambda i: (jnp.maximum(i - off, 0), 0)), pl.BlockSpec((1, D), lambda i: (0, 0))],
        out_specs=[pl.BlockSpec((8, LANES), lambda i: (0, 0)), row, pl.BlockSpec((1, D), lambda i: (0, 0)), row],
        out_shape=[jax.ShapeDtypeStruct((8, LANES), f32), jax.ShapeDtypeStruct((Lp, D), f32),
                   jax.ShapeDtypeStruct((1, D), f32), jax.ShapeDtypeStruct((Lp, D), bf16)],
        compiler_params=_cparams(("arbitrary",)),
    )(xs, target, g)


def _adam_math(w, g, m, v):
    m2 = ADAM_B1 * m + (1.0 - ADAM_B1) * g
    v2 = ADAM_B2 * v + (1.0 - ADAM_B2) * (g * g)
    m_hat = m2 / (1.0 - ADAM_B1 ** ADAM_STEP)
    v_hat = v2 / (1.0 - ADAM_B2 ** ADAM_STEP)
    delta = -ADAM_LR * (m_hat / (jnp.sqrt(v_hat) + ADAM_EPS) + ADAM_WD * w)
    return delta, m2, v2


def _adamw(w, m, v, unit, own, own_idx, recv, prev, name, after=None):
    U, R, C = w.shape
    tr, tc = _tile2(R, C, 256, 8 if own.dtype == f32 and recv is None else 16)
    n_recv = 0 if recv is None else recv.shape[0]

    def body(idx_ref, w_ref, m_ref, v_ref, own_ref, *rest):
        rest = list(rest)
        recv_refs = [rest.pop(0) for _ in range(n_recv)]
        if prev is not None:
            rest = rest[4:]
        g_ref, d_ref, m2_ref, v2_ref = rest
        g = own_ref[0].astype(f32)
        for r_ref in recv_refs:
            g = g + r_ref[0].astype(f32)
        delta, m2, v2 = _adam_math(w_ref[0], g, m_ref[0], v_ref[0])
        g_ref[0] = g
        d_ref[0] = delta
        m2_ref[0] = m2
        v2_ref[0] = v2

    blk = pl.BlockSpec((1, tr, tc), lambda i, j, idx: (unit, i, j))
    in_specs = [blk, blk, blk, pl.BlockSpec((1, tr, tc), lambda i, j, idx: (idx[0], i, j))]
    args = [w, m, v, own]
    for p in range(n_recv):
        in_specs.append(pl.BlockSpec((1, tr, tc), lambda i, j, idx, p=p: (p, i, j)))
        args.append(recv)
    aliases = {}
    if prev is not None:
        for t in range(4):
            aliases[1 + len(args) + t] = t
        in_specs += [ANY] * 4
        args += list(prev)
    body, in_specs, args = _ordered(body, in_specs, args, after, lead=1)
    out = jax.ShapeDtypeStruct((U, R, C), f32)
    return pl.pallas_call(
        body, name=name,
        grid_spec=pltpu.PrefetchScalarGridSpec(
            num_scalar_prefetch=1, grid=(R // tr, C // tc), in_specs=in_specs, out_specs=[blk] * 4),
        out_shape=[out] * 4, input_output_aliases=aliases,
        compiler_params=_cparams(("parallel", "parallel")),
    )(own_idx, *args)


def _place():
    return lax.axis_index("x"), lax.axis_index("y"), lax.axis_index("c")


HBM = pl.BlockSpec(memory_space=pltpu.HBM)
SEM = pl.BlockSpec(memory_space=pltpu.SEMAPHORE)
VMEM_SPEC = pl.BlockSpec(memory_space=pltpu.VMEM)
EFFECT = pltpu.SideEffectType.DATAFLOW_SIDE_EFFECTING
TOKEN = jax.ShapeDtypeStruct((8, LANES), f32)


def _hbm(x):
    return pltpu.with_memory_space_constraint(x, pltpu.HBM)


def _hbm_like(xs):
    return [pltpu.HBM(x.shape, x.dtype) for x in xs]


def _slot(px, py, pc):
    return 4 * px + 2 * py + pc


def _halves(ref):
    n = ref.shape[0]
    cut = n // 2 if n < 32 else (n // 2) // 16 * 16
    return ref.at[pl.ds(0, cut)], ref.at[pl.ds(cut, n - cut)]


def _gather_start(shards, after, name):
    n = len(shards)
    me = _slot(*_place())
    bufs = [lax.dynamic_update_slice(lax.empty((N_DEV,) + s.shape, s.dtype), s[None], (me,) + (0,) * s.ndim) for s in shards]

    def body(*refs):
        ins, land = refs[:n], refs[n:2 * n]
        send, recv = refs[2 * n + 1], refs[2 * n + 2]
        token = refs[-1]
        x, y, c = _place()
        to = [(x, y, 1 - c), (1 - x, y, c), (x, 1 - y, c)]
        for a in range(n):
            for k, dev in enumerate(to):
                pltpu.make_async_remote_copy(
                    src_ref=ins[a], dst_ref=land[a].at[_slot(x, y, c)], send_sem=send.at[3 * a + k], recv_sem=recv.at[3 * a + k],
                    device_id=dev, device_id_type=MESH).start()
        token[...] = jnp.zeros_like(token)

    out = pl.pallas_call(
        body, name=name,
        in_specs=[HBM] * (2 * n) + [ANY],
        out_specs=[SEM, SEM] + [HBM] * (2 * n) + [VMEM_SPEC],
        out_shape=[pltpu.SemaphoreType.DMA((3 * n,)), pltpu.SemaphoreType.DMA((3 * n,))] + _hbm_like(shards) + _hbm_like(bufs) + [TOKEN],
        input_output_aliases={i: 2 + i for i in range(2 * n)},
        compiler_params=pltpu.CompilerParams(has_side_effects=EFFECT),
    )(*[_hbm(s) for s in shards], *[_hbm(b) for b in bufs], after)
    return dict(send1=out[0], recv1=out[1], shards=list(out[2:2 + n]), bufs=list(out[2 + n:2 + 2 * n]), token=out[-1])


def _gather_mid(h, after, name):
    n = len(h["bufs"])

    def body(*refs):
        land, recv1 = refs[:n], refs[n]
        send2, recv2 = refs[n + 2], refs[n + 3]
        token = refs[-1]
        x, y, c = _place()
        nbr = [(1 - x, y, c), (x, 1 - y, c)]
        for j, dev in enumerate(nbr):
            for a in range(n):
                blk = land[a].at[_slot(*dev)]
                pltpu.make_async_remote_copy(
                    src_ref=blk, dst_ref=blk, send_sem=send2.at[4 * a + j], recv_sem=recv1.at[3 * a + 1 + j],
                    device_id=dev, device_id_type=MESH).wait_recv()
                pltpu.make_async_remote_copy(
                    src_ref=blk, dst_ref=blk, send_sem=send2.at[4 * a + j], recv_sem=recv2.at[4 * a + j],
                    device_id=(x, y, 1 - c), device_id_type=MESH).start()
        for a in range(n):
            from_x, from_y = land[a].at[_slot(*nbr[0])], land[a].at[_slot(*nbr[1])]
            for k, (half, dev) in enumerate([(_halves(from_y)[0], nbr[0]), (_halves(from_x)[1], nbr[1])]):
                pltpu.make_async_remote_copy(
                    src_ref=half, dst_ref=half, send_sem=send2.at[4 * a + 2 + k], recv_sem=recv2.at[4 * a + 2 + k],
                    device_id=dev, device_id_type=MESH).start()
        token[...] = jnp.zeros_like(token)

    out = pl.pallas_call(
        body, name=name,
        in_specs=[HBM] * n + [SEM, ANY],
        out_specs=[SEM, SEM] + [HBM] * n + [VMEM_SPEC],
        out_shape=[pltpu.SemaphoreType.DMA((4 * n,)), pltpu.SemaphoreType.DMA((4 * n,))] + _hbm_like(h["bufs"]) + [TOKEN],
        input_output_aliases={i: 2 + i for i in range(n)},
        compiler_params=pltpu.CompilerParams(has_side_effects=EFFECT),
    )(*h["bufs"], h["recv1"], after)
    h.update(send2=out[0], recv2=out[1], bufs=list(out[2:2 + n]), token=out[-1])
    return h


def _gather_mid2(h, after, name):
    n = len(h["bufs"])

    def body(*refs):
        land, recv2 = refs[:n], refs[n]
        send3, recv3 = refs[n + 2], refs[n + 3]
        token = refs[-1]
        x, y, c = _place()
        for a in range(n):
            blk = land[a].at[_slot(1 - x, 1 - y, c)]
            for k, half in enumerate(_halves(blk)):
                pltpu.make_async_remote_copy(
                    src_ref=half, dst_ref=half, send_sem=send3.at[a], recv_sem=recv2.at[4 * a + 2 + k],
                    device_id=(x, y, 1 - c), device_id_type=MESH).wait_recv()
            pltpu.make_async_remote_copy(
                src_ref=blk, dst_ref=blk, send_sem=send3.at[a], recv_sem=recv3.at[a],
                device_id=(x, y, 1 - c), device_id_type=MESH).start()
        token[...] = jnp.zeros_like(token)

    out = pl.pallas_call(
        body, name=name,
        in_specs=[HBM] * n + [SEM, ANY],
        out_specs=[SEM, SEM] + [HBM] * n + [VMEM_SPEC],
        out_shape=[pltpu.SemaphoreType.DMA((n,)), pltpu.SemaphoreType.DMA((n,))] + _hbm_like(h["bufs"]) + [TOKEN],
        input_output_aliases={i: 2 + i for i in range(n)},
        compiler_params=pltpu.CompilerParams(has_side_effects=EFFECT),
    )(*h["bufs"], h["recv2"], after)
    h.update(send3=out[0], recv3=out[1], bufs=list(out[2:2 + n]), token=out[-1])
    return h


def _gather_end(h, after, name):
    n = len(h["bufs"])

    def body(*refs):
        ins, land = refs[:n], refs[n:2 * n]
        send1, recv1, send2, recv2, send3, recv3 = refs[2 * n:2 * n + 6]
        x, y, c = _place()
        sib = (x, y, 1 - c)
        nbr = [(1 - x, y), (x, 1 - y)]

        def wait(src, dst, ssem, rsem, send):
            cp = pltpu.make_async_remote_copy(src_ref=src, dst_ref=dst, send_sem=ssem, recv_sem=rsem, device_id=sib, device_id_type=MESH)
            cp.wait_send() if send else cp.wait_recv()

        for a in range(n):
            mine = land[a].at[_slot(x, y, c)]
            for k in range(3):
                wait(ins[a], mine, send1.at[3 * a + k], recv1.at[3 * a + k], True)
            wait(ins[a], land[a].at[_slot(x, y, 1 - c)], send1.at[3 * a], recv1.at[3 * a], False)
            for j, (px, py) in enumerate(nbr):
                sent = land[a].at[_slot(px, py, c)]
                wait(sent, sent, send2.at[4 * a + j], recv2.at[4 * a + j], True)
                wait(sent, land[a].at[_slot(px, py, 1 - c)], send2.at[4 * a + j], recv2.at[4 * a + j], False)
            halves = [_halves(land[a].at[_slot(*nbr[1], c)])[0], _halves(land[a].at[_slot(*nbr[0], c)])[1]]
            for k, half in enumerate(halves):
                wait(half, half, send2.at[4 * a + 2 + k], recv2.at[4 * a + 2 + k], True)
            diag = land[a].at[_slot(1 - x, 1 - y, c)]
            wait(diag, diag, send3.at[a], recv3.at[a], True)
            wait(diag, land[a].at[_slot(1 - x, 1 - y, 1 - c)], send3.at[a], recv3.at[a], False)

    out = pl.pallas_call(
        body, name=name,
        in_specs=[HBM] * (2 * n) + [SEM] * 6 + [ANY],
        out_specs=[HBM] * n,
        out_shape=_hbm_like(h["bufs"]),
        input_output_aliases={n + i: i for i in range(n)},
        compiler_params=pltpu.CompilerParams(has_side_effects=EFFECT),
    )(*h["shards"], *h["bufs"], h["send1"], h["recv1"], h["send2"], h["recv2"], h["send3"], h["recv3"], after)
    return list(out)


def _peer_plan(kind, x, y, c):
    if kind == "pair":
        return [(2 * q + (1 - c), q, (x, y, 1 - c)) for q in range(4)]
    chips = [(1 - x, y), (x, 1 - y), (1 - x, 1 - y)]
    return [(2 * px + py, k, (px, py, c)) for k, (px, py) in enumerate(chips)]


def _exchange_start(kind, srcs, after, name):
    n = len(srcs)
    K = 4 if kind == "pair" else 3
    lands = [_hbm(lax.empty((K,) + s.shape[1:], s.dtype)) for s in srcs]

    def body(*refs):
        ins, land = refs[:n], refs[n:2 * n]
        send, recv = refs[2 * n + 1], refs[2 * n + 2]
        token = refs[-1]
        for a in range(n):
            for k, (si, di, dev) in enumerate(_peer_plan(kind, *_place())):
                pltpu.make_async_remote_copy(
                    src_ref=ins[a].at[si], dst_ref=land[a].at[di], send_sem=send.at[K * a + k], recv_sem=recv.at[K * a + k],
                    device_id=dev, device_id_type=MESH).start()
        token[...] = jnp.zeros_like(token)

    out = pl.pallas_call(
        body, name=name,
        in_specs=[HBM] * (2 * n) + [ANY],
        out_specs=[SEM, SEM] + [HBM] * (2 * n) + [VMEM_SPEC],
        out_shape=[pltpu.SemaphoreType.DMA((K * n,)), pltpu.SemaphoreType.DMA((K * n,))] + _hbm_like(srcs) + _hbm_like(lands) + [TOKEN],
        input_output_aliases={i: 2 + i for i in range(2 * n)},
        compiler_params=pltpu.CompilerParams(has_side_effects=EFFECT),
    )(*[_hbm(s) for s in srcs], *lands, after)
    return dict(kind=kind, send=out[0], recv=out[1], srcs=list(out[2:2 + n]), lands=list(out[2 + n:2 + 2 * n]), token=out[-1])


def _exchange_wait(h, after, name):
    n = len(h["srcs"])
    kind = h["kind"]
    K = 4 if kind == "pair" else 3

    def body(*refs):
        ins, land = refs[:n], refs[n:2 * n]
        send, recv = refs[2 * n], refs[2 * n + 1]
        for a in range(n):
            for k, (si, di, dev) in enumerate(_peer_plan(kind, *_place())):
                cp = pltpu.make_async_remote_copy(
                    src_ref=ins[a].at[si], dst_ref=land[a].at[di], send_sem=send.at[K * a + k], recv_sem=recv.at[K * a + k],
                    device_id=dev, device_id_type=MESH)
                cp.wait_send()
                cp.wait_recv()

    out = pl.pallas_call(
        body, name=name,
        in_specs=[HBM] * (2 * n) + [SEM, SEM, ANY],
        out_specs=[HBM] * (2 * n),
        out_shape=_hbm_like(h["srcs"]) + _hbm_like(h["lands"]),
        input_output_aliases={i: i for i in range(2 * n)},
        compiler_params=pltpu.CompilerParams(has_side_effects=EFFECT),
    )(*h["srcs"], *h["lands"], h["send"], h["recv"], after)
    return list(out[:n]), list(out[n:])


def _pair_add(g, got, c_idx, name):
    _, R, C = g.shape
    tr, tc = _tile2(R, C, 512, 16)

    def body(c_ref, a_ref, b_ref, o_ref):
        o_ref[0] = (a_ref[0].astype(f32) + b_ref[0].astype(f32)).astype(o_ref.dtype)

    return pl.pallas_call(
        body, name=name,
        grid_spec=pltpu.PrefetchScalarGridSpec(
            num_scalar_prefetch=1, grid=(4, R // tr, C // tc),
            in_specs=[pl.BlockSpec((1, tr, tc), lambda q, i, j, c: (2 * q + c[0], i, j)),
                      pl.BlockSpec((1, tr, tc), lambda q, i, j, c: (q, i, j))],
            out_specs=pl.BlockSpec((1, tr, tc), lambda q, i, j, c: (q, i, j))),
        out_shape=jax.ShapeDtypeStruct((4, R, C), g.dtype),
        compiler_params=_cparams(("parallel", "parallel", "parallel")),
    )(c_idx, g, got)


def _small_exchange(send, gather, name, after=None):
    R = send.shape[-2]

    def body(in_ref, out_ref, send_sems, recv_sems):
        x, y, c = _place()
        me = 4 * x + 2 * y + c
        out_ref[me] = in_ref[...] if gather else in_ref[me]
        cps = []
        for k in range(1, N_DEV):
            px, py, pc = x ^ ((k >> 2) & 1), y ^ ((k >> 1) & 1), c ^ (k & 1)
            src = in_ref if gather else in_ref.at[4 * px + 2 * py + pc]
            cps.append(pltpu.make_async_remote_copy(
                src_ref=src, dst_ref=out_ref.at[me],
                send_sem=send_sems.at[k - 1], recv_sem=recv_sems.at[k - 1],
                device_id=(px, py, pc), device_id_type=MESH))
        for cp in cps:
            cp.start()
        for cp in cps:
            cp.wait()

    body, in_specs, args = _ordered(body, [pl.BlockSpec(memory_space=pltpu.VMEM)], [send], after)
    return pl.pallas_call(
        body, name=name,
        in_specs=in_specs, out_specs=pl.BlockSpec(memory_space=pltpu.VMEM),
        out_shape=jax.ShapeDtypeStruct((N_DEV, R, LANES), f32),
        scratch_shapes=[pltpu.SemaphoreType.DMA((N_DEV - 1,)), pltpu.SemaphoreType.DMA((N_DEV - 1,))],
    )(*args)


def _sum_blocks(blocks, name):
    def body(in_ref, o_ref):
        s = in_ref[0]
        for d in range(1, N_DEV):
            s = s + in_ref[d]
        o_ref[0] = s

    return pl.pallas_call(body, name=name, out_shape=jax.ShapeDtypeStruct((1,) + blocks.shape[1:], f32))(blocks)


def _rows(n):
    return -(-n // LANES)


def _pack(arrs, total_rows):
    parts = []
    for a in arrs:
        flat = a.reshape(-1).astype(f32)
        parts.append(jnp.pad(flat, (0, _rows(flat.size) * LANES - flat.size)))
    flat = jnp.concatenate(parts)
    return jnp.pad(flat, (0, total_rows * LANES - flat.size)).reshape(total_rows, LANES)


def _unpack(packed, shapes):
    lead = packed.shape[:-2]
    flat = packed.reshape(lead + (-1,))
    out, pos = [], 0
    for s in shapes:
        n = 1
        for d in s:
            n *= d
        out.append(flat[..., pos:pos + n].reshape(lead + tuple(s)))
        pos += _rows(n) * LANES
    return out


def _to_shards(full, axis):
    s = full.shape
    return jnp.moveaxis(full.reshape(s[:axis] + (N_DEV, s[axis] // N_DEV) + s[axis + 1:]), axis, 0)


def _from_shards(sh, axis):
    m = jnp.moveaxis(sh, 0, axis)
    s = m.shape
    return m.reshape(s[:axis] + (s[axis] * s[axis + 1],) + s[axis + 2:])


def kernel(x, meta, ffn_norm, ffn_w_gate, ffn_w_up, ffn_w_down, gla_norm, gla_w_in, gla_w_lr, gla_b_lr, gla_head_norm, gla_w_out, pool_norm, pool_w, pool_b, pool_scale, final_norm, loss_target, m_meta, m_ffn_norm, m_ffn_w_gate, m_ffn_w_up, m_ffn_w_down, m_gla_norm, m_gla_w_in, m_gla_w_lr, m_gla_b_lr, m_gla_head_norm, m_gla_w_out, m_pool_norm, m_pool_w, m_pool_b, m_pool_scale, m_final_norm, v_meta, v_ffn_norm, v_ffn_w_gate, v_ffn_w_up, v_ffn_w_down, v_gla_norm, v_gla_w_in, v_gla_w_lr, v_gla_b_lr, v_gla_head_norm, v_gla_w_out, v_pool_norm, v_pool_w, v_pool_b, v_pool_scale, v_final_norm):
    H = GLA_HEADS
    _, SEQ, D = x.shape
    Fs = ffn_w_gate.shape[-1]
    DK, DV = D // 2, D
    hv = DV // H
    GW = D // POOL_GROUPS
    INW = 2 * DK + 2 * DV + GATE_RANK
    NPK = 2 * DK + 2 * DV + GATE_PAD
    pad = (-N_META) % GLA_CHUNK
    first = pad + N_META
    Lp = first + SEQ
    n_units = ffn_w_gate.shape[0] * ffn_w_gate.shape[1]
    assert first % GLA_CHUNK == 0 and Lp % GLA_CHUNK == 0 and pad >= POOL_GROUPS * 4

    px, py, pc = _place()
    c_idx = jnp.reshape(pc, (1,)).astype(jnp.int32)
    q_idx = jnp.reshape(2 * px + py, (1,)).astype(jnp.int32)
    zero_idx = jnp.zeros((1,), jnp.int32)

    small_sh = [meta, ffn_norm, gla_w_lr, pool_norm, pool_b, pool_scale]
    small_axis = [1, 2, 2, 1, 2, 1]
    sh_shapes = [a.shape for a in small_sh]
    sh_rows = -(-sum(_rows(a.size) for a in small_sh) // 8) * 8
    gathered = _small_exchange(_pack(small_sh, sh_rows), True, "small_gather")
    meta_f, ffn_norm_f, wlr_f, pool_norm_f, pool_b_f, pool_scale_f = [
        _from_shards(a, ax) for a, ax in zip(_unpack(gathered, sh_shapes), small_axis)]
    ffn_norm_f = ffn_norm_f.reshape(n_units, 1, D)
    wlr128 = jnp.pad(wlr_f[0], ((0, GATE_PAD - GATE_RANK), (0, 0)))

    def t_units(w):
        return jnp.swapaxes(w, -1, -2).reshape(n_units, Fs, D)

    ffn_f32 = [t_units(ffn_w_gate), t_units(ffn_w_up), ffn_w_down.reshape(n_units, Fs, D)]
    mixer_f32 = [gla_w_in[0].T[None], gla_w_out, pool_w[0].reshape(1, -1, GW)]
    gather_order = [("ffn0", ffn_f32, 0), ("mixers", mixer_f32, 0)] + [(f"ffn{u}", ffn_f32, u) for u in range(1, n_units)]
    c_lr = 2 * DK + DV
    c_r = 2 * DK + 2 * DV
    gate_blk = c_r // GATE_PAD

    def cast_shards(i, after):
        tag, arrays, u = gather_order[i]
        shards = [_cast_unit(w, u, f"cast_{tag}_{a}", after) for a, w in enumerate(arrays)]
        if tag == "mixers":
            shards[2] = shards[2].reshape(pool_w.shape[1:])
        return shards

    def gather_next(i, h, after):
        tag = gather_order[i][0]
        nxt = cast_shards(i + 1, after) if i + 1 < len(gather_order) else None
        h = _gather_mid(h, after if nxt is None else nxt[0], f"gather_mid_{tag}")
        if nxt is not None:
            nxt = _gather_start(nxt, h["token"], f"gather_start_{gather_order[i + 1][0]}")
        h = _gather_mid2(h, h["token"] if nxt is None else nxt["token"], f"gather_mid2_{tag}")
        done = _gather_end(h, h["token"], f"gather_end_{tag}")
        return done, nxt

    xs = jnp.concatenate([jnp.zeros((pad, D), f32), meta_f, x[0]], axis=0)
    saved = {}
    ffn_w = [None] * n_units

    def ffn_f(u, xs):
        out, h, G, U = _ffn_fwd(xs, ffn_norm_f[u], *ffn_w[u], name=f"ffn_fwd{u}")
        saved[("ffn", u)] = (xs, h, G, U)
        return out

    def gla_f(xs, win_p, wout_full):
        hn = _rms_fwd(xs, gla_norm, bf16, "gla_norm_fwd")
        proj = _mm(hn, win_p, "nt", f32, "gla_proj", tm=1056, tn=896, tk=2048)
        lg = _gate_fwd(proj, wlr128, gla_b_lr, pad, gate_blk, "gla_gate_fwd")
        o, y, states = _gla_fwd(proj, lg, gla_head_norm, H, "gla_core_fwd")
        out = _mm(y, wout_full, "nn", f32, "gla_out", tm=1056, tn=512, tk=2048, residual=xs)
        saved["gla"] = (xs, hn, proj, lg, o, y, states)
        return out

    def pool_f(xs, wpool_full):
        hn = _rms_fwd(xs, pool_norm_f, f32, "pool_norm_fwd")
        pooled = _pool_windows(hn, pad, Lp - pad, "pool_windows_fwd")
        out = _pool_mix_fwd(xs, pooled, wpool_full, pool_b_f.reshape(1, D), pool_scale_f, pad, "pool_mix_fwd")
        saved["pool"] = (xs, pooled)
        return out

    depth = ffn_w_gate.shape[0]
    assert depth == 2 and n_units == 4
    h = _gather_start(cast_shards(0, None), gathered, "gather_start_ffn0")
    ffn_w[0], h = gather_next(0, h, h["token"])
    xs = ffn_f(0, xs)
    (win_g, wout_g, wpool_g), h = gather_next(1, h, xs)
    win_full = win_g.reshape(INW, D)
    win_p = jnp.concatenate([win_full[:c_lr], win_full[c_lr + GATE_RANK:], win_full[c_lr:c_lr + GATE_RANK],
                             jnp.zeros((GATE_PAD - GATE_RANK, D), bf16)], axis=0)
    wout_full = wout_g.reshape(DV, D)
    wpool_full = _from_shards(wpool_g, 1)
    xs = gla_f(xs, win_p, wout_full)
    ffn_w[1], h = gather_next(2, h, xs)
    xs = ffn_f(1, xs)
    ffn_w[2], h = gather_next(3, h, xs)
    xs = ffn_f(2, xs)
    xs = pool_f(xs, wpool_full)
    ffn_w[3], h = gather_next(4, h, xs)
    xs = ffn_f(3, xs)
    loss_part, dxs, d_final, dyh = _loss_head(xs, loss_target[0], final_norm.reshape(1, D), first, "loss_head")

    class Reduce:
        def __init__(self, tag, grads, after=None):
            self.tag = tag
            self.h = _exchange_start("pair", grads, loss_part if after is None else after, f"pair_start_{tag}")
            self.token = self.h["token"]

        def mid(self, after):
            grads, got = _exchange_wait(self.h, after, f"pair_wait_{self.tag}")
            self.sums = [_pair_add(g, r, c_idx, f"pair_add_{self.tag}{a}") for a, (g, r) in enumerate(zip(grads, got))]
            self.h = _exchange_start("chips", self.sums, loss_part, f"chips_start_{self.tag}")
            self.token = self.h["token"]

        def end(self, after):
            sums, recv = _exchange_wait(self.h, after, f"chips_wait_{self.tag}")
            return list(zip(sums, recv))

    d_ffn_norm = [None] * n_units
    small_grads = {}

    def ffn_b(u, dY, dyh, prev):
        xs_in, h_, G, U = saved[("ffn", u)]
        wg, wu, wd = ffn_w[u]
        tok = None if prev is None else prev.token
        dG, dU, A = _ffn_bwd_act(dyh, wd, G, U, f"ffn_act{u}", after=tok)
        dh = _ffn_bwd_dh(dG, dU, wg, wu, f"ffn_dh{u}")
        dxs, dg, dyh_next = _rms_bwd(dY, dh, xs_in, ffn_norm_f[u], pad, f"ffn_norm_bwd{u}")
        if prev is not None:
            prev.mid(dxs)
            tok = prev.token
        dwg = _ffn_bwd_wgrad(dG, h_, f"ffn_wgrad_gate{u}", after=tok)
        dwu = _ffn_bwd_wgrad(dU, h_, f"ffn_wgrad_up{u}", after=tok)
        dwd = _ffn_bwd_wgrad(A, dyh, f"ffn_wgrad_down{u}", after=tok)
        d_ffn_norm[u] = dg
        return dxs, dyh_next, Reduce(f"ffn{u}", [dwg, dwu, dwd])

    def gla_b(dY, prev):
        xs_in, hn, proj, lg, o, y, states = saved["gla"]
        dyb = dY.astype(bf16)
        dy = _mm(dyb, wout_full, "nt", f32, "gla_out_dgrad", tm=1056, tn=512, tk=2048, after=prev.token)
        dwout = _mm_tn_full(y, dyb, "gla_out_wgrad", 1024, after=prev.token)
        prev.mid(dwout)
        dq, dk, dv, dr, dlg, dhw = _gla_bwd(dy, proj, lg, o, states, gla_head_norm, H, pad, "gla_core_bwd", after=prev.token)
        dlr, dwlr, dblr = _gate_bwd(dlg, proj, wlr128, gla_b_lr, pad, gate_blk, "gla_gate_bwd")
        dproj = jnp.concatenate([dq, dk, dv, dr, dlr], axis=1)
        dwin_p = _mm_tn_full(dproj, hn, "gla_proj_wgrad", 896)
        dhn = _mm(dproj, win_p, "nn", f32, "gla_proj_dgrad", tm=1056, tn=1024, tk=896)
        dxs, dgn, dyh_next = _rms_bwd(dY, dhn, xs_in, gla_norm, pad, "gla_norm_bwd")
        dwin = jnp.concatenate([dwin_p[:c_lr], dwin_p[c_r:c_r + GATE_RANK], dwin_p[c_lr:c_r]], axis=0)
        small_grads.update(gla_w_lr=dwlr[:GATE_RANK][None], gla_b_lr=dblr, gla_head_norm=dhw, gla_norm=dgn)
        return dxs, dyh_next, Reduce("gla", [dwin.reshape(N_DEV, INW // N_DEV, D), dwout.reshape(N_DEV, DV // N_DEV, D)])

    def pool_b_(dY, prev):
        xs_in, pooled = saved["pool"]
        dp, dw, db, ds = _pool_mix_bwd(dY, pooled, wpool_full, pool_b_f.reshape(1, D), pool_scale_f, pad, "pool_mix_bwd",
                                       after=prev.token)
        dhn = _pool_windows_bwd(dp, pad, "pool_windows_bwd")
        dxs, dgn, dyh_next = _rms_bwd(dY, dhn, xs_in, pool_norm_f, pad, "pool_norm_bwd")
        prev.mid(dxs)
        dws = _to_shards(dw, 1)
        small_grads.update(pool_b=db.reshape(1, POOL_GROUPS, GW), pool_scale=ds, pool_norm=dgn)
        return dxs, dyh_next, Reduce("pool", [dws.reshape(N_DEV, POOL_GROUPS * GW // N_DEV, GW)], after=prev.token)

    sh_names = ["meta", "ffn_norm", "gla_w_lr", "pool_norm", "pool_b", "pool_scale"]
    rep_names = ["gla_norm", "gla_b_lr", "gla_head_norm", "final_norm"]
    rep_w = [gla_norm, gla_b_lr, gla_head_norm, final_norm]
    rep_shapes = [a.shape for a in rep_w]
    rep_rows = -(-sum(_rows(a.size) for a in rep_w) // 8) * 8

    def small_path(dxs0):
        small_grads.update(meta=dxs0[pad:first], ffn_norm=jnp.concatenate(d_ffn_norm, axis=0).reshape(n_units // 2, 2, D),
                           final_norm=d_final.reshape(D))
        by_owner = [_to_shards(small_grads[nm].reshape(full_shape), ax) for nm, full_shape, ax in zip(
            sh_names, [meta_f.shape, (ffn_norm.shape[0], 2, D), wlr_f.shape, pool_norm_f.shape, pool_b_f.shape, pool_scale_f.shape],
            small_axis)]
        rep_pack = _pack([small_grads[nm].reshape(s) for nm, s in zip(rep_names, rep_shapes)], rep_rows)
        send = jnp.stack([
            jnp.concatenate([_pack([g[d] for g in by_owner], sh_rows), rep_pack, loss_part], axis=0) for d in range(N_DEV)])
        total = _sum_blocks(_small_exchange(send, False, "small_reduce"), "small_sum")
        n_small = sh_rows + rep_rows

        def pack_small(sh_list, rep_list):
            return jnp.concatenate([_pack(sh_list, sh_rows), _pack(rep_list, rep_rows)], axis=0)[None]

        w_small = pack_small(small_sh, rep_w)
        m_small = pack_small([m_meta, m_ffn_norm, m_gla_w_lr, m_pool_norm, m_pool_b, m_pool_scale],
                             [m_gla_norm, m_gla_b_lr, m_gla_head_norm, m_final_norm])
        v_small = pack_small([v_meta, v_ffn_norm, v_gla_w_lr, v_pool_norm, v_pool_b, v_pool_scale],
                             [v_gla_norm, v_gla_b_lr, v_gla_head_norm, v_final_norm])
        small_out = _adamw(w_small, m_small, v_small, 0, total[:, :n_small], zero_idx, None, None, "adamw_small")
        small_res = {}
        for kind, packed in zip(("grad", "delta", "new_m", "new_v"), small_out):
            sh_vals = _unpack(packed[0, :sh_rows], sh_shapes)
            rep_vals = _unpack(packed[0, sh_rows:], rep_shapes)
            for nm, val in zip(sh_names + rep_names, sh_vals + rep_vals):
                small_res[(kind, nm)] = val
        return total[0, n_small, 0], small_res, small_out[0]

    def ffn_b_last(dY, dyh, prev):
        xs_in, h_, G, U = saved[("ffn", 0)]
        wg, wu, wd = ffn_w[0]
        dG, dU, A = _ffn_bwd_act(dyh, wd, G, U, "ffn_act0", after=prev.token)
        dwd = _ffn_bwd_wgrad(A, dyh, "ffn_wgrad_down0", after=prev.token)
        r_d = Reduce("ffn0_down", [dwd])
        dh = _ffn_bwd_dh(dG, dU, wg, wu, "ffn_dh0", after=r_d.token)
        dxs, dg, _ = _rms_bwd(dY, dh, xs_in, ffn_norm_f[0], pad, "ffn_norm_bwd0")
        d_ffn_norm[0] = dg
        small = small_path(dxs)
        prev.mid(small[2])
        r_d.mid(prev.token)
        dwg = _ffn_bwd_wgrad(dG, h_, "ffn_wgrad_gate0", after=r_d.token)
        r_g = Reduce("ffn0_gate", [dwg])
        dwu = _ffn_bwd_wgrad(dU, h_, "ffn_wgrad_up0", after=r_g.token)
        r_g.mid(dwu)
        r_u = Reduce("ffn0_up", [dwu], after=r_g.token)
        return dxs, small, (r_g, r_u, r_d)

    dxs, dyh, r3 = ffn_b(3, dxs, dyh, None)
    dxs, dyh, rp = pool_b_(dxs, r3)
    dxs, dyh, r2 = ffn_b(2, dxs, dyh, rp)
    dxs, dyh, r1 = ffn_b(1, dxs, dyh, r2)
    dxs, dyh, rg = gla_b(dxs, r1)
    dxs, (loss, small_res, _), r0 = ffn_b_last(dxs, dyh, rg)
    grad_x = dxs[first:].reshape(x.shape)
    r_last = r0[1]

    big_res = {}

    def adam_one(nm, w, m, v, entry, transposed=False):
        sums, recv = entry
        R, C = sums.shape[1:]
        w1, m1, v1 = ((t[0].T if transposed else t).reshape(1, R, C) for t in (w, m, v))
        out = _adamw(w1, m1, v1, 0, sums, q_idx, recv, None, f"adamw_{nm}", after=r_last.token)
        for kind, val in zip(("grad", "delta", "new_m", "new_v"), out):
            big_res[(kind, nm)] = val[0].T[None] if transposed else val.reshape(w.shape)
        return out[0]

    e_gla = rg.end(dxs)
    done = adam_one("gla_w_in", gla_w_in, m_gla_w_in, v_gla_w_in, e_gla[0], transposed=True)
    done = adam_one("gla_w_out", gla_w_out, m_gla_w_out, v_gla_w_out, e_gla[1])
    done = adam_one("pool_w", pool_w, m_pool_w, v_pool_w, rp.end(done)[0])
    r_last.mid(done)

    ffn_names = ["ffn_w_gate", "ffn_w_up", "ffn_w_down"]
    ffn_wmv = [tuple(t_units(t) for t in (ffn_w_gate, m_ffn_w_gate, v_ffn_w_gate)),
               tuple(t_units(t) for t in (ffn_w_up, m_ffn_w_up, v_ffn_w_up)),
               tuple(t.reshape(n_units, Fs, D) for t in (ffn_w_down, m_ffn_w_down, v_ffn_w_down))]
    ffn_prev = [[lax.empty((n_units, Fs, D), f32) for _ in range(4)] for _ in range(3)]
    order_after = r_last.token
    for u, red in ((3, r3), (2, r2), (1, r1), (0, r0)):
        entries = [r.end(done)[0] for r in red] if u == 0 else red.end(done)
        for a in range(3):
            sums, recv = entries[a]
            ffn_prev[a] = _adamw(*ffn_wmv[a], u, sums, q_idx, recv, ffn_prev[a], f"adamw_{ffn_names[a]}{u}", after=order_after)
            done = order_after = ffn_prev[a][0]
    for a in range(3):
        for kind, val in zip(("grad", "delta", "new_m", "new_v"), ffn_prev[a]):
            val = val.reshape(ffn_w_down.shape)
            big_res[(kind, ffn_names[a])] = val if a == 2 else jnp.swapaxes(val, -1, -2)

    order = ["meta", "ffn_norm", "ffn_w_gate", "ffn_w_up", "ffn_w_down", "gla_norm", "gla_w_in", "gla_w_lr", "gla_b_lr",
             "gla_head_norm", "gla_w_out", "pool_norm", "pool_w", "pool_b", "pool_scale", "final_norm"]
    res = {**small_res, **big_res}
    outs = [loss, grad_x]
    for kind in ("grad", "delta", "new_m", "new_v"):
        outs += [res[(kind, nm)] for nm in order]
    return tuple(outs)
```

```python
import functools

import jax
import jax.numpy as jnp
from jax import lax
from jax.experimental import pallas as pl
from jax.experimental.pallas import tpu as pltpu

f32 = jnp.float32
bf16 = jnp.bfloat16

N_DEV = 8
N_META = 16
GLA_HEADS = 4
GLA_CHUNK = 64
GLA_SUB = 16
GATE_RANK = 16
GATE_PAD = 128
GATE_NORM = 16.0
EPS = 1e-6
POOL_GROUPS = 4
ADAM_LR = 0.001
ADAM_B1 = 0.9
ADAM_B2 = 0.999
ADAM_EPS = 1e-08
ADAM_WD = 0.01
ADAM_STEP = 10
LANES = 128
VMEM_LIMIT_MB = 56

NN = (((1,), (0,)), ((), ()))
NT = (((1,), (1,)), ((), ()))
TN = (((0,), (0,)), ((), ()))
HI = lax.Precision.HIGHEST
MESH = pl.DeviceIdType.MESH
ANY = pl.BlockSpec(memory_space=pl.ANY)


def _cparams(sem=None, vmem_mb=None):
    kw = {}
    if sem is not None:
        kw["dimension_semantics"] = sem
    if vmem_mb is not None:
        kw["vmem_limit_bytes"] = vmem_mb * 2 ** 20
    return pltpu.CompilerParams(**kw)


def _tile(n, target, mult=16):
    best = None
    for t in range(mult, min(n, target) + 1, mult):
        if n % t == 0:
            best = t
    assert best is not None, (n, target, mult)
    return best


def _tile2(R, C, rows, mult):
    if R % mult == 0:
        return _tile(R, rows, mult), C
    return R, _tile(C, 256, LANES)


def _dot(a, b, dims=NN, precision=None):
    return lax.dot_general(a, b, dims, preferred_element_type=f32, precision=precision)


def _sigmoid(x):
    return 1.0 / (1.0 + jnp.exp(-x))


def _row_ids(tile_index, tm):
    return tile_index * tm + lax.broadcasted_iota(jnp.int32, (tm, 1), 0)


def _ordered(body, in_specs, args, after, lead=0):
    if after is None:
        return body, in_specs, args
    pos = lead + len(args)

    def body_without(*refs):
        return body(*refs[:pos], *refs[pos + 1:])

    return body_without, list(in_specs) + [ANY], list(args) + [after]


def _cast_unit(w, unit, name, after=None):
    _, R, C = w.shape
    tr, tc = _tile2(R, C, 256, 16)

    def body(w_ref, o_ref):
        o_ref[...] = w_ref[0].astype(bf16)

    body, in_specs, args = _ordered(body, [pl.BlockSpec((1, tr, tc), lambda i, j: (unit, i, j))], [w], after)
    return pl.pallas_call(
        body, name=name, grid=(R // tr, C // tc),
        in_specs=in_specs, out_specs=pl.BlockSpec((tr, tc), lambda i, j: (i, j)),
        out_shape=jax.ShapeDtypeStruct((R, C), bf16),
        compiler_params=_cparams(("parallel", "parallel")),
    )(*args)


def _rms_fwd(xs, g, out_dtype, name, after=None):
    Lp, D = xs.shape
    tm = _tile(Lp, 528)

    def body(x_ref, g_ref, h_ref):
        x = x_ref[...]
        rstd = lax.rsqrt(jnp.mean(x * x, axis=-1, keepdims=True) + EPS)
        h_ref[...] = (x * rstd * g_ref[...]).astype(out_dtype)

    in_specs = [pl.BlockSpec((tm, D), lambda i: (i, 0)), pl.BlockSpec((1, D), lambda i: (0, 0))]
    body, in_specs, args = _ordered(body, in_specs, [xs, g], after)
    return pl.pallas_call(
        body, name=name, grid=(Lp // tm,),
        in_specs=in_specs,
        out_specs=pl.BlockSpec((tm, D), lambda i: (i, 0)),
        out_shape=jax.ShapeDtypeStruct((Lp, D), out_dtype),
        compiler_params=_cparams(("parallel",)),
    )(*args)


def _rms_bwd(dY, dh, xs, g, pad, name):
    Lp, D = xs.shape
    tm = _tile(Lp, 352)

    def body(dY_ref, dh_ref, x_ref, g_ref, dxs_ref, dg_ref, half_ref):
        i = pl.program_id(0)

        @pl.when(i == 0)
        def _():
            dg_ref[...] = jnp.zeros_like(dg_ref)

        x = x_ref[...]
        rstd = lax.rsqrt(jnp.mean(x * x, axis=-1, keepdims=True) + EPS)
        xhat = x * rstd
        dh_ = dh_ref[...]
        dg_ref[...] += jnp.sum(dh_ * xhat, axis=0, keepdims=True)
        dxh = dh_ * g_ref[...]
        dx = rstd * (dxh - xhat * jnp.mean(dxh * xhat, axis=-1, keepdims=True))
        out = jnp.where(_row_ids(i, tm) >= pad, dY_ref[...] + dx, 0.0)
        dxs_ref[...] = out
        half_ref[...] = (0.5 * out).astype(bf16)

    row = pl.BlockSpec((tm, D), lambda i: (i, 0))
    vec = pl.BlockSpec((1, D), lambda i: (0, 0))
    return pl.pallas_call(
        body, name=name, grid=(Lp // tm,),
        in_specs=[row, row, row, vec], out_specs=[row, vec, row],
        out_shape=[jax.ShapeDtypeStruct((Lp, D), f32), jax.ShapeDtypeStruct((1, D), f32), jax.ShapeDtypeStruct((Lp, D), bf16)],
        compiler_params=_cparams(("arbitrary",)),
    )(dY, dh, xs, g)


def _mm(a, b, mode, out_dtype, name, tm=512, tn=512, tk=512, residual=None, after=None):
    if mode == "nn":
        (M, K), N = a.shape, b.shape[1]
    elif mode == "nt":
        (M, K), N = a.shape, b.shape[0]
    else:
        (K, M), N = a.shape, b.shape[1]
    tm = _tile(M, tm, 16 if mode != "tn" else LANES) if M > tm else M
    tn = _tile(N, tn, LANES) if N > tn else N
    tk = _tile(K, tk, LANES if mode != "tn" else 16) if K > tk else K
    nk = K // tk
    dims = {"nn": NN, "nt": NT, "tn": TN}[mode]

    def body(*refs):
        if residual is None:
            a_ref, b_ref, o_ref, acc = refs
            r_ref = None
        else:
            a_ref, b_ref, r_ref, o_ref, acc = refs
        k = pl.program_id(2)

        @pl.when(k == 0)
        def _():
            acc[...] = jnp.zeros_like(acc)

        acc[...] += _dot(a_ref[...], b_ref[...], dims)

        @pl.when(k == nk - 1)
        def _():
            r = acc[...]
            if r_ref is not None:
                r = r + r_ref[...]
            o_ref[...] = r.astype(out_dtype)

    a_spec = pl.BlockSpec((tk, tm), lambda i, j, k: (k, i)) if mode == "tn" else pl.BlockSpec((tm, tk), lambda i, j, k: (i, k))
    b_spec = pl.BlockSpec((tn, tk), lambda i, j, k: (j, k)) if mode == "nt" else pl.BlockSpec((tk, tn), lambda i, j, k: (k, j))
    o_spec = pl.BlockSpec((tm, tn), lambda i, j, k: (i, j))
    in_specs = [a_spec, b_spec] + ([o_spec] if residual is not None else [])
    args = [a, b] + ([residual] if residual is not None else [])
    body, in_specs, args = _ordered(body, in_specs, args, after)
    return pl.pallas_call(
        body, name=name, grid=(M // tm, N // tn, nk),
        in_specs=in_specs, out_specs=o_spec,
        out_shape=jax.ShapeDtypeStruct((M, N), out_dtype),
        scratch_shapes=[pltpu.VMEM((tm, tn), f32)],
        compiler_params=_cparams(("parallel", "parallel", "arbitrary"), VMEM_LIMIT_MB),
    )(*args)


def _mm_tn_full(a, b, name, tm, after=None):
    K, M = a.shape
    N = b.shape[1]
    tm = _tile(M, tm, LANES)

    def body(a_ref, b_ref, o_ref):
        o_ref[...] = _dot(a_ref[...], b_ref[...], TN).astype(bf16)

    in_specs = [pl.BlockSpec((K, tm), lambda i: (0, i)), pl.BlockSpec((K, N), lambda i: (0, 0), pipeline_mode=pl.Buffered(1))]
    body, in_specs, args = _ordered(body, in_specs, [a, b], after)
    return pl.pallas_call(
        body, name=name, grid=(M // tm,),
        in_specs=in_specs, out_specs=pl.BlockSpec((tm, N), lambda i: (i, 0)),
        out_shape=jax.ShapeDtypeStruct((M, N), bf16),
        compiler_params=_cparams(("parallel",), VMEM_LIMIT_MB),
    )(*args)


def _ffn_fwd(xs, g, wg, wu, wd, name, after=None):
    Lp, D = xs.shape
    nd, Fs, _ = wg.shape
    tm = _tile(Lp, 704)
    once = pl.Buffered(1)

    def body(x_ref, g_ref, wg_ref, wu_ref, wd_ref, out_ref, h_ref, G_ref, U_ref, hs, acc):
        j = pl.program_id(1)

        @pl.when(j == 0)
        def _():
            x = x_ref[...]
            rstd = lax.rsqrt(jnp.mean(x * x, axis=-1, keepdims=True) + EPS)
            h = (x * rstd * g_ref[...]).astype(bf16)
            hs[...] = h
            h_ref[...] = h
            acc[...] = jnp.zeros_like(acc)

        h = hs[...]
        G = _dot(h, wg_ref[0], NT)
        U = _dot(h, wu_ref[0], NT)
        G_ref[0] = G.astype(bf16)
        U_ref[0] = U.astype(bf16)
        A = (G * _sigmoid(G) * U).astype(bf16)
        acc[...] += _dot(A, wd_ref[0])

        @pl.when(j == nd - 1)
        def _():
            out_ref[...] = x_ref[...] + 0.5 * acc[...]

    row_f = pl.BlockSpec((tm, D), lambda i, j: (i, 0), pipeline_mode=once)
    act = pl.BlockSpec((1, tm, Fs), lambda i, j: (j, i, 0))
    wrow = pl.BlockSpec((1, Fs, D), lambda i, j: (j, 0, 0))
    in_specs = [row_f, pl.BlockSpec((1, D), lambda i, j: (0, 0)), wrow, wrow, wrow]
    body, in_specs, args = _ordered(body, in_specs, [xs, g, wg, wu, wd], after)
    return pl.pallas_call(
        body, name=name, grid=(Lp // tm, nd),
        in_specs=in_specs,
        out_specs=[row_f, pl.BlockSpec((tm, D), lambda i, j: (i, 0), pipeline_mode=once), act, act],
        out_shape=[jax.ShapeDtypeStruct((Lp, D), f32), jax.ShapeDtypeStruct((Lp, D), bf16),
                   jax.ShapeDtypeStruct((nd, Lp, Fs), bf16), jax.ShapeDtypeStruct((nd, Lp, Fs), bf16)],
        scratch_shapes=[pltpu.VMEM((tm, D), bf16), pltpu.VMEM((tm, D), f32)],
        compiler_params=_cparams(("parallel", "arbitrary"), VMEM_LIMIT_MB),
    )(*args)


def _ffn_bwd_act(dyh, wd, G, U, name, after=None):
    Lp, D = dyh.shape
    nd, Fs, _ = wd.shape
    tm = _tile(Lp, 704)

    def body(dyh_ref, wd_ref, G_ref, U_ref, dG_ref, dU_ref, A_ref):
        dA = _dot(dyh_ref[...], wd_ref[0], NT)
        Gf = G_ref[0].astype(f32)
        Uf = U_ref[0].astype(f32)
        s = _sigmoid(Gf)
        silu = Gf * s
        dG_ref[0] = (dA * Uf * (s * (1.0 + Gf * (1.0 - s)))).astype(bf16)
        dU_ref[0] = (dA * silu).astype(bf16)
        A_ref[0] = (silu * Uf).astype(bf16)

    act = pl.BlockSpec((1, tm, Fs), lambda j, i: (j, i, 0))
    act_s = jax.ShapeDtypeStruct((nd, Lp, Fs), bf16)
    in_specs = [pl.BlockSpec((tm, D), lambda j, i: (i, 0)), pl.BlockSpec((1, Fs, D), lambda j, i: (j, 0, 0)), act, act]
    body, in_specs, args = _ordered(body, in_specs, [dyh, wd, G, U], after)
    return pl.pallas_call(
        body, name=name, grid=(nd, Lp // tm),
        in_specs=in_specs, out_specs=[act, act, act], out_shape=[act_s, act_s, act_s],
        compiler_params=_cparams(("parallel", "parallel"), VMEM_LIMIT_MB),
    )(*args)


def _ffn_bwd_dh(dG, dU, wg, wu, name, after=None):
    nd, Lp, Fs = dG.shape
    D = wg.shape[2]
    tm = _tile(Lp, 1056)

    def body(dG_ref, dU_ref, wg_ref, wu_ref, dh_ref, acc):
        j = pl.program_id(1)

        @pl.when(j == 0)
        def _():
            acc[...] = jnp.zeros_like(acc)

        acc[...] += _dot(dG_ref[0], wg_ref[0]) + _dot(dU_ref[0], wu_ref[0])

        @pl.when(j == nd - 1)
        def _():
            dh_ref[...] = acc[...]

    act = pl.BlockSpec((1, tm, Fs), lambda i, j: (j, i, 0))
    wrow = pl.BlockSpec((1, Fs, D), lambda i, j: (j, 0, 0))
    body, in_specs, args = _ordered(body, [act, act, wrow, wrow], [dG, dU, wg, wu], after)
    return pl.pallas_call(
        body, name=name, grid=(Lp // tm, nd),
        in_specs=in_specs,
        out_specs=pl.BlockSpec((tm, D), lambda i, j: (i, 0), pipeline_mode=pl.Buffered(1)),
        out_shape=jax.ShapeDtypeStruct((Lp, D), f32),
        scratch_shapes=[pltpu.VMEM((tm, D), f32)],
        compiler_params=_cparams(("parallel", "arbitrary"), VMEM_LIMIT_MB),
    )(*args)


def _ffn_bwd_wgrad(act, rows, name, after=None):
    nd, Lp, Fs = act.shape
    D = rows.shape[1]

    def body(a_ref, r_ref, o_ref):
        o_ref[0] = _dot(a_ref[0], r_ref[...], TN).astype(bf16)

    in_specs = [pl.BlockSpec((1, Lp, Fs), lambda j: (j, 0, 0)),
                pl.BlockSpec((Lp, D), lambda j: (0, 0), pipeline_mode=pl.Buffered(1))]
    body, in_specs, args = _ordered(body, in_specs, [act, rows], after)
    return pl.pallas_call(
        body, name=name, grid=(nd,),
        in_specs=in_specs, out_specs=pl.BlockSpec((1, Fs, D), lambda j: (j, 0, 0)),
        out_shape=jax.ShapeDtypeStruct((nd, Fs, D), bf16),
        compiler_params=_cparams(("parallel",), VMEM_LIMIT_MB),
    )(*args)


def _gate_fwd(proj, wlr, blr, pad, gate_blk, name):
    Lp = proj.shape[0]
    DK = wlr.shape[1]
    tm = _tile(Lp, 528)

    def body(lr_ref, w_ref, b_ref, lg_ref):
        z = _dot(lr_ref[...].astype(bf16), w_ref[...].astype(bf16)) + b_ref[...]
        ls = jnp.minimum(z, 0.0) - jnp.log(1.0 + jnp.exp(-jnp.abs(z)))
        lg_ref[...] = jnp.where(_row_ids(pl.program_id(0), tm) >= pad, ls * (1.0 / GATE_NORM), 0.0)

    return pl.pallas_call(
        body, name=name, grid=(Lp // tm,),
        in_specs=[pl.BlockSpec((tm, GATE_PAD), lambda i: (i, gate_blk)),
                  pl.BlockSpec((GATE_PAD, DK), lambda i: (0, 0)), pl.BlockSpec((1, DK), lambda i: (0, 0))],
        out_specs=pl.BlockSpec((tm, DK), lambda i: (i, 0)),
        out_shape=jax.ShapeDtypeStruct((Lp, DK), f32),
        compiler_params=_cparams(("parallel",)),
    )(proj, wlr, blr)


def _gate_bwd(dlg, proj, wlr, blr, pad, gate_blk, name):
    Lp = proj.shape[0]
    DK = wlr.shape[1]
    tm = _tile(Lp, 528)

    def body(dlg_ref, lr_ref, w_ref, b_ref, dlr_ref, dw_ref, db_ref):
        i = pl.program_id(0)

        @pl.when(i == 0)
        def _():
            dw_ref[...] = jnp.zeros_like(dw_ref)
            db_ref[...] = jnp.zeros_like(db_ref)

        lr = lr_ref[...].astype(bf16)
        w = w_ref[...].astype(bf16)
        z = _dot(lr, w) + b_ref[...]
        dz = jnp.where(_row_ids(i, tm) >= pad, dlg_ref[...] * _sigmoid(-z) * (1.0 / GATE_NORM), 0.0)
        dzb = dz.astype(bf16)
        dlr_ref[...] = _dot(dzb, w, NT).astype(bf16)
        dw_ref[...] += _dot(lr, dzb, TN)
        db_ref[...] += jnp.sum(dz, axis=0, keepdims=True)

    return pl.pallas_call(
        body, name=name, grid=(Lp // tm,),
        in_specs=[pl.BlockSpec((tm, DK), lambda i: (i, 0)), pl.BlockSpec((tm, GATE_PAD), lambda i: (i, gate_blk)),
                  pl.BlockSpec((GATE_PAD, DK), lambda i: (0, 0)), pl.BlockSpec((1, DK), lambda i: (0, 0))],
        out_specs=[pl.BlockSpec((tm, GATE_PAD), lambda i: (i, 0)), pl.BlockSpec((GATE_PAD, DK), lambda i: (0, 0)),
                   pl.BlockSpec((1, DK), lambda i: (0, 0))],
        out_shape=[jax.ShapeDtypeStruct((Lp, GATE_PAD), bf16), jax.ShapeDtypeStruct((GATE_PAD, DK), f32),
                   jax.ShapeDtypeStruct((1, DK), f32)],
        compiler_params=_cparams(("arbitrary",)),
    )(dlg, proj, wlr, blr)


def _chunk_decay(lg):
    C = lg.shape[0]
    r = lax.broadcasted_iota(jnp.int32, (C, C), 0)
    c = lax.broadcasted_iota(jnp.int32, (C, C), 1)
    return _dot(jnp.where(r >= c, 1.0, 0.0).astype(f32), lg, NN, HI)


def _col(v):
    return jnp.transpose(jnp.broadcast_to(v, (8, v.shape[1])))[:, 0:1]


def _intra_scores(q, k, b, A_ref):
    C = q.shape[0]
    S = GLA_SUB
    A_ref[...] = jnp.zeros_like(A_ref)
    ri = lax.broadcasted_iota(jnp.int32, (S, 1), 0)
    for I in range(C // S):
        lo = S * I
        qI, bI = q[lo:lo + S], b[lo:lo + S]
        if I > 0:
            bref = b[lo - 1:lo]
            qs = qI * jnp.exp(bI - bref)
            ks = k[:lo] * jnp.exp(bref - b[:lo])
            A_ref[lo:lo + S, 0:lo] = _dot(qs, ks, NT, HI)
        for jj in range(S):
            j = lo + jj
            P = jnp.exp(jnp.minimum(bI - b[j:j + 1], 0.0))
            a = jnp.sum(qI * P * k[j:j + 1], axis=1, keepdims=True)
            A_ref[lo:lo + S, j:j + 1] = jnp.where(ri >= jj, a, 0.0)


def _intra_grads(q, k, b, dA, dq_ref, dk_ref):
    C = q.shape[0]
    S = GLA_SUB
    ri = lax.broadcasted_iota(jnp.int32, (S, 1), 0)
    for I in range(C // S):
        lo = S * I
        qI, bI = q[lo:lo + S], b[lo:lo + S]
        dqI = jnp.zeros_like(qI)
        if I > 0:
            bref = b[lo - 1:lo]
            eq = jnp.exp(bI - bref)
            ek = jnp.exp(bref - b[:lo])
            qs = qI * eq
            ks = k[:lo] * ek
            dAI = dA[lo:lo + S, 0:lo]
            dqI = dqI + _dot(dAI, ks, NN, HI) * eq
            dk_ref[0:lo, :] += _dot(dAI, qs, TN, HI) * ek
        for jj in range(S):
            j = lo + jj
            P = jnp.exp(jnp.minimum(bI - b[j:j + 1], 0.0))
            t = jnp.where(ri >= jj, dA[lo:lo + S, j:j + 1], 0.0) * P
            dqI = dqI + t * k[j:j + 1]
            dk_ref[j:j + 1, :] += jnp.sum(t * qI, axis=0, keepdims=True)
        dq_ref[lo:lo + S, :] += dqI


GLA_HEADS_PER_STEP = 2


def _gla_fwd(proj, lg, hnw, H, name):
    Lp = proj.shape[0]
    DK = lg.shape[1]
    hk = DK // H
    hv = hnw.shape[1]
    DV = hv * H
    C = GLA_CHUNK
    NC = Lp // C
    HS = GLA_HEADS_PER_STEP
    G = H // HS
    scale = float(hk) ** -0.5
    kq, kv, kr = G, (2 * DK) // (HS * hv), (2 * DK) // (HS * hv) + G

    def body(q_ref, k_ref, v_ref, r_ref, lg_ref, w_ref, o_ref, y_ref, s_ref, S_scr, A_scr):
        c = pl.program_id(1)

        @pl.when(c == 0)
        def _():
            S_scr[...] = jnp.zeros_like(S_scr)

        for hh in range(HS):
            ck, cv = slice(hh * hk, (hh + 1) * hk), slice(hh * hv, (hh + 1) * hv)
            q = q_ref[:, ck] * scale
            k = k_ref[:, ck]
            v = v_ref[:, cv]
            b = _chunk_decay(lg_ref[:, ck])
            bl = b[C - 1:C]
            S = S_scr[hh]
            s_ref[hh, 0] = S
            _intra_scores(q, k, b, A_scr.at[hh])
            vb = v.astype(bf16)
            o = _dot((q * jnp.exp(b)).astype(bf16), S.astype(bf16)) + _dot(A_scr[hh].astype(bf16), vb)
            kb = (k * jnp.exp(bl - b)).astype(bf16)
            S_scr[hh] = jnp.exp(_col(bl)) * S + _dot(kb, vb, TN)
            o_ref[:, cv] = o
            on = o * lax.rsqrt(jnp.mean(o * o, axis=-1, keepdims=True) + EPS) * w_ref[...]
            r = r_ref[:, cv]
            y_ref[:, cv] = (on * (r * _sigmoid(r))).astype(bf16)

    return pl.pallas_call(
        body, name=name, grid=(G, NC),
        in_specs=[pl.BlockSpec((C, HS * hk), lambda g, c: (c, g)),
                  pl.BlockSpec((C, HS * hk), lambda g, c: (c, kq + g)),
                  pl.BlockSpec((C, HS * hv), lambda g, c: (c, kv + g)),
                  pl.BlockSpec((C, HS * hv), lambda g, c: (c, kr + g)),
                  pl.BlockSpec((C, HS * hk), lambda g, c: (c, g)),
                  pl.BlockSpec((1, hv), lambda g, c: (0, 0))],
        out_specs=[pl.BlockSpec((C, HS * hv), lambda g, c: (c, g)), pl.BlockSpec((C, HS * hv), lambda g, c: (c, g)),
                   pl.BlockSpec((HS, 1, hk, hv), lambda g, c: (g, c, 0, 0))],
        out_shape=[jax.ShapeDtypeStruct((Lp, DV), f32), jax.ShapeDtypeStruct((Lp, DV), bf16),
                   jax.ShapeDtypeStruct((H, NC, hk, hv), f32)],
        scratch_shapes=[pltpu.VMEM((HS, hk, hv), f32), pltpu.VMEM((HS, C, C), f32)],
        compiler_params=_cparams(("parallel", "arbitrary")),
    )(proj, proj, proj, proj, lg, hnw)


def _gla_bwd(dy, proj, lg, o, states, hnw, H, pad, name, after=None):
    Lp = proj.shape[0]
    DK = lg.shape[1]
    hk = DK // H
    hv = hnw.shape[1]
    DV = hv * H
    C = GLA_CHUNK
    NC = Lp // C
    HS = GLA_HEADS_PER_STEP
    G = H // HS
    scale = float(hk) ** -0.5
    kq, kv, kr = G, (2 * DK) // (HS * hv), (2 * DK) // (HS * hv) + G

    def body(dy_ref, q_ref, k_ref, v_ref, r_ref, lg_ref, o_ref, s_ref, sn_ref, w_ref,
             dq_ref, dk_ref, dv_ref, dr_ref, dlg_ref, dw_ref, dS_scr, A_scr, dq_s, dk_s):
        g = pl.program_id(0)
        cc = pl.program_id(1)
        c = NC - 1 - cc

        @pl.when(cc == 0)
        def _():
            dS_scr[...] = jnp.zeros_like(dS_scr)

        @pl.when((cc == 0) & (g == 0))
        def _():
            dw_ref[...] = jnp.zeros_like(dw_ref)

        keep = (c * C + lax.broadcasted_iota(jnp.int32, (C, 1), 0)) >= pad
        ri = lax.broadcasted_iota(jnp.int32, (C, C), 0)
        ci = lax.broadcasted_iota(jnp.int32, (C, C), 1)
        w = w_ref[...]
        for hh in range(HS):
            ck, cv = slice(hh * hk, (hh + 1) * hk), slice(hh * hv, (hh + 1) * hv)
            o_ = o_ref[:, cv]
            rs = lax.rsqrt(jnp.mean(o_ * o_, axis=-1, keepdims=True) + EPS)
            ohat = o_ * rs
            r = r_ref[:, cv]
            sg = _sigmoid(r)
            dy_ = dy_ref[:, cv]
            d_on = dy_ * (r * sg)
            dr_ref[:, cv] = jnp.where(keep, dy_ * (ohat * w) * (sg * (1.0 + r * (1.0 - sg))), 0.0).astype(bf16)
            dw_ref[...] += jnp.sum(d_on * ohat, axis=0, keepdims=True)
            d_oh = d_on * w
            do = rs * (d_oh - ohat * jnp.mean(d_oh * ohat, axis=-1, keepdims=True))
            dob = do.astype(bf16)
            q = q_ref[:, ck] * scale
            k = k_ref[:, ck]
            vb = v_ref[:, cv].astype(bf16)
            b = _chunk_decay(lg_ref[:, ck])
            bl = b[C - 1:C]
            eb = jnp.exp(b)
            ekb = jnp.exp(bl - b)
            S = s_ref[hh, 0]
            dS = dS_scr[hh]
            dSb = dS.astype(bf16)
            _intra_scores(q, k, b, A_scr.at[hh])
            dA = jnp.where(ri >= ci, _dot(dob, vb, NT), 0.0)
            kb = (k * ekb).astype(bf16)
            qb = (q * eb).astype(bf16)
            dv = _dot(A_scr[hh].astype(bf16), dob, TN) + _dot(kb, dSb)
            dq_s[hh] = _dot(dob, S.astype(bf16), NT) * eb
            dk_s[hh] = _dot(vb, dSb, NT) * ekb
            dS_scr[hh] = _dot(qb, dob, TN) + jnp.exp(_col(bl)) * dS
            _intra_grads(q, k, b, dA, dq_s.at[hh], dk_s.at[hh])
            dq = dq_s[hh]
            dk = dk_s[hh]
            Dm = q * dq - k * dk
            after_rows = _dot(jnp.ones((8, hv), f32), sn_ref[hh, 0] * dS, NT, HI)[0:1]
            dlg = _dot(jnp.where(ri <= ci, 1.0, 0.0).astype(f32), Dm, NN, HI) + after_rows
            dlg_ref[:, ck] = jnp.where(keep, dlg, 0.0)
            dq_ref[:, ck] = jnp.where(keep, dq * scale, 0.0).astype(bf16)
            dk_ref[:, ck] = jnp.where(keep, dk, 0.0).astype(bf16)
            dv_ref[:, cv] = jnp.where(keep, dv, 0.0).astype(bf16)

    rev = lambda cc: NC - 1 - cc
    bk = lambda off: pl.BlockSpec((C, HS * hk), lambda g, cc: (rev(cc), off + g))
    bv = lambda off: pl.BlockSpec((C, HS * hv), lambda g, cc: (rev(cc), off + g))
    in_specs = [bv(0), bk(0), bk(kq), bv(kv), bv(kr), bk(0), bv(0),
                pl.BlockSpec((HS, 1, hk, hv), lambda g, cc: (g, rev(cc), 0, 0)),
                pl.BlockSpec((HS, 1, hk, hv), lambda g, cc: (g, jnp.minimum(rev(cc) + 1, NC - 1), 0, 0)),
                pl.BlockSpec((1, hv), lambda g, cc: (0, 0))]
    body, in_specs, args = _ordered(body, in_specs, [dy, proj, proj, proj, proj, lg, o, states, states, hnw], after)
    return pl.pallas_call(
        body, name=name, grid=(G, NC),
        in_specs=in_specs,
        out_specs=[bk(0), bk(0), bv(0), bv(0), bk(0), pl.BlockSpec((1, hv), lambda g, cc: (0, 0))],
        out_shape=[jax.ShapeDtypeStruct((Lp, DK), bf16), jax.ShapeDtypeStruct((Lp, DK), bf16),
                   jax.ShapeDtypeStruct((Lp, DV), bf16), jax.ShapeDtypeStruct((Lp, DV), bf16),
                   jax.ShapeDtypeStruct((Lp, DK), f32), jax.ShapeDtypeStruct((1, hv), f32)],
        scratch_shapes=[pltpu.VMEM((HS, hk, hv), f32), pltpu.VMEM((HS, C, C), f32),
                        pltpu.VMEM((HS, C, hk), f32), pltpu.VMEM((HS, C, hk), f32)],
        compiler_params=_cparams(("arbitrary", "arbitrary")),
    )(*args)


def _window_sums(x, back):
    n = x.shape[0]
    out = []
    s = x
    for w in (1, 2, 4, 8):
        s = s + pltpu.roll(s, w if back else n - w, 0)
        out.append(s)
    return out


def _pool_windows(hn, pad, n_real, name):
    Lp, D = hn.shape
    GW = D // POOL_GROUPS
    cb = min(GW, 256)
    per = GW // cb

    def body(h_ref, p_ref):
        g = pl.program_id(0) // per
        x = h_ref[...]
        s2, s4, s8, s16 = _window_sums(x, True)
        sel = jnp.where(g == 0, s2, jnp.where(g == 1, s4, jnp.where(g == 2, s8, s16)))
        win = jnp.left_shift(2, g).astype(f32)
        rows = lax.broadcasted_iota(jnp.int32, (Lp, 1), 0)
        t = (rows - pad).astype(f32)
        cnt = jnp.minimum(jnp.maximum(t, 0.0) + 1.0, win)
        p_ref[...] = jnp.where(rows >= pad, sel / cnt - x, 0.0).astype(bf16)

    return pl.pallas_call(
        body, name=name, grid=(D // cb,),
        in_specs=[pl.BlockSpec((Lp, cb), lambda i: (0, i))],
        out_specs=pl.BlockSpec((Lp, cb), lambda i: (0, i)),
        out_shape=jax.ShapeDtypeStruct((Lp, D), bf16),
        compiler_params=_cparams(("parallel",)),
    )(hn)


def _pool_windows_bwd(dp, pad, name):
    Lp, D = dp.shape
    GW = D // POOL_GROUPS
    cb = min(GW, 256)
    per = GW // cb

    def body(dp_ref, dh_ref):
        g = pl.program_id(0) // per
        rows = lax.broadcasted_iota(jnp.int32, (Lp, 1), 0)
        d = jnp.where(rows >= pad, dp_ref[...], 0.0)
        win = jnp.left_shift(2, g).astype(f32)
        t = (rows - pad).astype(f32)
        cnt = jnp.minimum(jnp.maximum(t, 0.0) + 1.0, win)
        s2, s4, s8, s16 = _window_sums(d / cnt, False)
        sel = jnp.where(g == 0, s2, jnp.where(g == 1, s4, jnp.where(g == 2, s8, s16)))
        dh_ref[...] = jnp.where(rows >= pad, sel - d, 0.0)

    return pl.pallas_call(
        body, name=name, grid=(D // cb,),
        in_specs=[pl.BlockSpec((Lp, cb), lambda i: (0, i))],
        out_specs=pl.BlockSpec((Lp, cb), lambda i: (0, i)),
        out_shape=jax.ShapeDtypeStruct((Lp, D), f32),
        compiler_params=_cparams(("parallel",)),
    )(dp)


def _pool_mix_fwd(xs, pooled, w, bias, scale, pad, name):
    Lp, D = xs.shape
    GW = D // POOL_GROUPS
    tm = _tile(Lp, 1056)

    def body(x_ref, p_ref, w_ref, b_ref, s_ref, o_ref):
        z = _dot(p_ref[...], w_ref[0]) + b_ref[...]
        keep = _row_ids(pl.program_id(1), tm) >= pad
        o_ref[...] = x_ref[...] + jnp.where(keep, z * s_ref[...], 0.0)

    blk = pl.BlockSpec((tm, GW), lambda g, i: (i, g))
    vec = pl.BlockSpec((1, GW), lambda g, i: (0, g))
    return pl.pallas_call(
        body, name=name, grid=(POOL_GROUPS, Lp // tm),
        in_specs=[blk, blk, pl.BlockSpec((1, GW, GW), lambda g, i: (g, 0, 0)), vec, vec],
        out_specs=blk, out_shape=jax.ShapeDtypeStruct((Lp, D), f32),
        compiler_params=_cparams(("parallel", "parallel")),
    )(xs, pooled, w, bias, scale)


def _pool_mix_bwd(dY, pooled, w, bias, scale, pad, name, after=None):
    Lp, D = dY.shape
    GW = D // POOL_GROUPS
    tm = _tile(Lp, 1056)
    nm = Lp // tm

    def body(dY_ref, p_ref, w_ref, b_ref, s_ref, dp_ref, dw_ref, db_ref, ds_ref, acc):
        i = pl.program_id(1)

        @pl.when(i == 0)
        def _():
            acc[...] = jnp.zeros_like(acc)
            db_ref[...] = jnp.zeros_like(db_ref)
            ds_ref[...] = jnp.zeros_like(ds_ref)

        keep = _row_ids(i, tm) >= pad
        dY_ = jnp.where(keep, dY_ref[...], 0.0)
        p = p_ref[...]
        z = _dot(p, w_ref[0]) + b_ref[...]
        ds_ref[...] += jnp.sum(dY_ * z, axis=0, keepdims=True)
        dz = dY_ * s_ref[...]
        db_ref[...] += jnp.sum(dz, axis=0, keepdims=True)
        dzb = dz.astype(bf16)
        acc[...] += _dot(p, dzb, TN)
        dp_ref[...] = _dot(dzb, w_ref[0], NT)

        @pl.when(i == nm - 1)
        def _():
            dw_ref[0] = acc[...].astype(bf16)

    blk = pl.BlockSpec((tm, GW), lambda g, i: (i, g))
    vec = pl.BlockSpec((1, GW), lambda g, i: (0, g))
    wsp = pl.BlockSpec((1, GW, GW), lambda g, i: (g, 0, 0))
    body, in_specs, args = _ordered(body, [blk, blk, wsp, vec, vec], [dY, pooled, w, bias, scale], after)
    return pl.pallas_call(
        body, name=name, grid=(POOL_GROUPS, nm),
        in_specs=in_specs, out_specs=[blk, wsp, vec, vec],
        out_shape=[jax.ShapeDtypeStruct((Lp, D), f32), jax.ShapeDtypeStruct((POOL_GROUPS, GW, GW), bf16),
                   jax.ShapeDtypeStruct((1, D), f32), jax.ShapeDtypeStruct((1, D), f32)],
        scratch_shapes=[pltpu.VMEM((GW, GW), f32)],
        compiler_params=_cparams(("parallel", "arbitrary")),
    )(*args)


def _loss_head(xs, target, g, first, name):
    Lp, D = xs.shape
    tm = GLA_CHUNK
    off = first // tm

    def body(x_ref, t_ref, g_ref, loss_ref, dxs_ref, dg_ref, half_ref):
        i = pl.program_id(0)

        @pl.when(i == 0)
        def _():
            loss_ref[...] = jnp.zeros_like(loss_ref)
            dg_ref[...] = jnp.zeros_like(dg_ref)

        @pl.when(i < off)
        def _():
            dxs_ref[...] = jnp.zeros_like(dxs_ref)
            half_ref[...] = jnp.zeros_like(half_ref)

        @pl.when(i >= off)
        def _():
            x = x_ref[...]
            rstd = lax.rsqrt(jnp.mean(x * x, axis=-1, keepdims=True) + EPS)
            xhat = x * rstd
            gg = g_ref[...]
            err = xhat * gg - t_ref[...]
            loss_ref[...] += 0.5 * jnp.sum(jnp.mean(err * err, axis=-1, keepdims=True))
            dy = err * (1.0 / D)
            dg_ref[...] += jnp.sum(dy * xhat, axis=0, keepdims=True)
            dxh = dy * gg
            out = rstd * (dxh - xhat * jnp.mean(dxh * xhat, axis=-1, keepdims=True))
            dxs_ref[...] = out
            half_ref[...] = (0.5 * out).astype(bf16)

    row = pl.BlockSpec((tm, D), lambda i: (i, 0))
    return pl.pallas_call(
        body, name=name, grid=(Lp // tm,),
        in_specs=[row, pl.BlockSpec((tm, D), lambda i: (jnp.maximum(i - off, 0), 0)), pl.BlockSpec((1, D), lambda i: (0, 0))],
        out_specs=[pl.BlockSpec((8, LANES), lambda i: (0, 0)), row, pl.BlockSpec((1, D), lambda i: (0, 0)), row],
        out_shape=[jax.ShapeDtypeStruct((8, LANES), f32), jax.ShapeDtypeStruct((Lp, D), f32),
                   jax.ShapeDtypeStruct((1, D), f32), jax.ShapeDtypeStruct((Lp, D), bf16)],
        compiler_params=_cparams(("arbitrary",)),
    )(xs, target, g)


def _adam_math(w, g, m, v):
    m2 = ADAM_B1 * m + (1.0 - ADAM_B1) * g
    v2 = ADAM_B2 * v + (1.0 - ADAM_B2) * (g * g)
    m_hat = m2 / (1.0 - ADAM_B1 ** ADAM_STEP)
    v_hat = v2 / (1.0 - ADAM_B2 ** ADAM_STEP)
    delta = -ADAM_LR * (m_hat / (jnp.sqrt(v_hat) + ADAM_EPS) + ADAM_WD * w)
    return delta, m2, v2


def _adamw(w, m, v, unit, own, own_idx, recv, prev, name, after=None):
    U, R, C = w.shape
    tr, tc = _tile2(R, C, 256, 8 if own.dtype == f32 and recv is None else 16)
    n_recv = 0 if recv is None else recv.shape[0]

    def body(idx_ref, w_ref, m_ref, v_ref, own_ref, *rest):
        rest = list(rest)
        recv_refs = [rest.pop(0) for _ in range(n_recv)]
        if prev is not None:
            rest = rest[4:]
        g_ref, d_ref, m2_ref, v2_ref = rest
        g = own_ref[0].astype(f32)
        for r_ref in recv_refs:
            g = g + r_ref[0].astype(f32)
        delta, m2, v2 = _adam_math(w_ref[0], g, m_ref[0], v_ref[0])
        g_ref[0] = g
        d_ref[0] = delta
        m2_ref[0] = m2
        v2_ref[0] = v2

    blk = pl.BlockSpec((1, tr, tc), lambda i, j, idx: (unit, i, j))
    in_specs = [blk, blk, blk, pl.BlockSpec((1, tr, tc), lambda i, j, idx: (idx[0], i, j))]
    args = [w, m, v, own]
    for p in range(n_recv):
        in_specs.append(pl.BlockSpec((1, tr, tc), lambda i, j, idx, p=p: (p, i, j)))
        args.append(recv)
    aliases = {}
    if prev is not None:
        for t in range(4):
            aliases[1 + len(args) + t] = t
        in_specs += [ANY] * 4
        args += list(prev)
    body, in_specs, args = _ordered(body, in_specs, args, after, lead=1)
    out = jax.ShapeDtypeStruct((U, R, C), f32)
    return pl.pallas_call(
        body, name=name,
        grid_spec=pltpu.PrefetchScalarGridSpec(
            num_scalar_prefetch=1, grid=(R // tr, C // tc), in_specs=in_specs, out_specs=[blk] * 4),
        out_shape=[out] * 4, input_output_aliases=aliases,
        compiler_params=_cparams(("parallel", "parallel")),
    )(own_idx, *args)


def _place():
    return lax.axis_index("x"), lax.axis_index("y"), lax.axis_index("c")


HBM = pl.BlockSpec(memory_space=pltpu.HBM)
SEM = pl.BlockSpec(memory_space=pltpu.SEMAPHORE)
VMEM_SPEC = pl.BlockSpec(memory_space=pltpu.VMEM)
EFFECT = pltpu.SideEffectType.DATAFLOW_SIDE_EFFECTING
TOKEN = jax.ShapeDtypeStruct((8, LANES), f32)


def _hbm(x):
    return pltpu.with_memory_space_constraint(x, pltpu.HBM)


def _hbm_like(xs):
    return [pltpu.HBM(x.shape, x.dtype) for x in xs]


def _slot(px, py, pc):
    return 4 * px + 2 * py + pc


def _halves(ref):
    n = ref.shape[0]
    cut = n // 2 if n < 32 else (n // 2) // 16 * 16
    return ref.at[pl.ds(0, cut)], ref.at[pl.ds(cut, n - cut)]


def _gather_start(shards, after, name):
    n = len(shards)
    me = _slot(*_place())
    bufs = [lax.dynamic_update_slice(lax.empty((N_DEV,) + s.shape, s.dtype), s[None], (me,) + (0,) * s.ndim) for s in shards]

    def body(*refs):
        ins, land = refs[:n], refs[n:2 * n]
        send, recv = refs[2 * n + 1], refs[2 * n + 2]
        token = refs[-1]
        x, y, c = _place()
        to = [(x, y, 1 - c), (1 - x, y, c), (x, 1 - y, c)]
        for a in range(n):
            for k, dev in enumerate(to):
                pltpu.make_async_remote_copy(
                    src_ref=ins[a], dst_ref=land[a].at[_slot(x, y, c)], send_sem=send.at[3 * a + k], recv_sem=recv.at[3 * a + k],
                    device_id=dev, device_id_type=MESH).start()
        token[...] = jnp.zeros_like(token)

    out = pl.pallas_call(
        body, name=name,
        in_specs=[HBM] * (2 * n) + [ANY],
        out_specs=[SEM, SEM] + [HBM] * (2 * n) + [VMEM_SPEC],
        out_shape=[pltpu.SemaphoreType.DMA((3 * n,)), pltpu.SemaphoreType.DMA((3 * n,))] + _hbm_like(shards) + _hbm_like(bufs) + [TOKEN],
        input_output_aliases={i: 2 + i for i in range(2 * n)},
        compiler_params=pltpu.CompilerParams(has_side_effects=EFFECT),
    )(*[_hbm(s) for s in shards], *[_hbm(b) for b in bufs], after)
    return dict(send1=out[0], recv1=out[1], shards=list(out[2:2 + n]), bufs=list(out[2 + n:2 + 2 * n]), token=out[-1])


def _gather_mid(h, after, name):
    n = len(h["bufs"])

    def body(*refs):
        land, recv1 = refs[:n], refs[n]
        send2, recv2 = refs[n + 2], refs[n + 3]
        token = refs[-1]
        x, y, c = _place()
        nbr = [(1 - x, y, c), (x, 1 - y, c)]
        for j, dev in enumerate(nbr):
            for a in range(n):
                blk = land[a].at[_slot(*dev)]
                pltpu.make_async_remote_copy(
                    src_ref=blk, dst_ref=blk, send_sem=send2.at[4 * a + j], recv_sem=recv1.at[3 * a + 1 + j],
                    device_id=dev, device_id_type=MESH).wait_recv()
                pltpu.make_async_remote_copy(
                    src_ref=blk, dst_ref=blk, send_sem=send2.at[4 * a + j], recv_sem=recv2.at[4 * a + j],
                    device_id=(x, y, 1 - c), device_id_type=MESH).start()
        for a in range(n):
            from_x, from_y = land[a].at[_slot(*nbr[0])], land[a].at[_slot(*nbr[1])]
            for k, (half, dev) in enumerate([(_halves(from_y)[0], nbr[0]), (_halves(from_x)[1], nbr[1])]):
                pltpu.make_async_remote_copy(
                    src_ref=half, dst_ref=half, send_sem=send2.at[4 * a + 2 + k], recv_sem=recv2.at[4 * a + 2 + k],
                    device_id=dev, device_id_type=MESH).start()
        token[...] = jnp.zeros_like(token)

    out = pl.pallas_call(
        body, name=name,
        in_specs=[HBM] * n + [SEM, ANY],
        out_specs=[SEM, SEM] + [HBM] * n + [VMEM_SPEC],
        out_shape=[pltpu.SemaphoreType.DMA((4 * n,)), pltpu.SemaphoreType.DMA((4 * n,))] + _hbm_like(h["bufs"]) + [TOKEN],
        input_output_aliases={i: 2 + i for i in range(n)},
        compiler_params=pltpu.CompilerParams(has_side_effects=EFFECT),
    )(*h["bufs"], h["recv1"], after)
    h.update(send2=out[0], recv2=out[1], bufs=list(out[2:2 + n]), token=out[-1])
    return h


def _gather_mid2(h, after, name):
    n = len(h["bufs"])

    def body(*refs):
        land, recv2 = refs[:n], refs[n]
        send3, recv3 = refs[n + 2], refs[n + 3]
        token = refs[-1]
        x, y, c = _place()
        for a in range(n):
            blk = land[a].at[_slot(1 - x, 1 - y, c)]
            for k, half in enumerate(_halves(blk)):
                pltpu.make_async_remote_copy(
                    src_ref=half, dst_ref=half, send_sem=send3.at[a], recv_sem=recv2.at[4 * a + 2 + k],
                    device_id=(x, y, 1 - c), device_id_type=MESH).wait_recv()
            pltpu.make_async_remote_copy(
                src_ref=blk, dst_ref=blk, send_sem=send3.at[a], recv_sem=recv3.at[a],
                device_id=(x, y, 1 - c), device_id_type=MESH).start()
        token[...] = jnp.zeros_like(token)

    out = pl.pallas_call(
        body, name=name,
        in_specs=[HBM] * n + [SEM, ANY],
        out_specs=[SEM, SEM] + [HBM] * n + [VMEM_SPEC],
        out_shape=[pltpu.SemaphoreType.DMA((n,)), pltpu.SemaphoreType.DMA((n,))] + _hbm_like(h["bufs"]) + [TOKEN],
        input_output_aliases={i: 2 + i for i in range(n)},
        compiler_params=pltpu.CompilerParams(has_side_effects=EFFECT),
    )(*h["bufs"], h["recv2"], after)
    h.update(send3=out[0], recv3=out[1], bufs=list(out[2:2 + n]), token=out[-1])
    return h


def _gather_end(h, after, name):
    n = len(h["bufs"])

    def body(*refs):
        ins, land = refs[:n], refs[n:2 * n]
        send1, recv1, send2, recv2, send3, recv3 = refs[2 * n:2 * n + 6]
        x, y, c = _place()
        sib = (x, y, 1 - c)
        nbr = [(1 - x, y), (x, 1 - y)]

        def wait(src, dst, ssem, rsem, send):
            cp = pltpu.make_async_remote_copy(src_ref=src, dst_ref=dst, send_sem=ssem, recv_sem=rsem, device_id=sib, device_id_type=MESH)
            cp.wait_send() if send else cp.wait_recv()

        for a in range(n):
            mine = land[a].at[_slot(x, y, c)]
            for k in range(3):
                wait(ins[a], mine, send1.at[3 * a + k], recv1.at[3 * a + k], True)
            wait(ins[a], land[a].at[_slot(x, y, 1 - c)], send1.at[3 * a], recv1.at[3 * a], False)
            for j, (px, py) in enumerate(nbr):
                sent = land[a].at[_slot(px, py, c)]
                wait(sent, sent, send2.at[4 * a + j], recv2.at[4 * a + j], True)
                wait(sent, land[a].at[_slot(px, py, 1 - c)], send2.at[4 * a + j], recv2.at[4 * a + j], False)
            halves = [_halves(land[a].at[_slot(*nbr[1], c)])[0], _halves(land[a].at[_slot(*nbr[0], c)])[1]]
            for k, half in enumerate(halves):
                wait(half, half, send2.at[4 * a + 2 + k], recv2.at[4 * a + 2 + k], True)
            diag = land[a].at[_slot(1 - x, 1 - y, c)]
            wait(diag, diag, send3.at[a], recv3.at[a], True)
            wait(diag, land[a].at[_slot(1 - x, 1 - y, 1 - c)], send3.at[a], recv3.at[a], False)

    out = pl.pallas_call(
        body, name=name,
        in_specs=[HBM] * (2 * n) + [SEM] * 6 + [ANY],
        out_specs=[HBM] * n,
        out_shape=_hbm_like(h["bufs"]),
        input_output_aliases={n + i: i for i in range(n)},
        compiler_params=pltpu.CompilerParams(has_side_effects=EFFECT),
    )(*h["shards"], *h["bufs"], h["send1"], h["recv1"], h["send2"], h["recv2"], h["send3"], h["recv3"], after)
    return list(out)


def _peer_plan(kind, x, y, c):
    if kind == "pair":
        return [(2 * q + (1 - c), q, (x, y, 1 - c)) for q in range(4)]
    chips = [(1 - x, y), (x, 1 - y), (1 - x, 1 - y)]
    return [(2 * px + py, k, (px, py, c)) for k, (px, py) in enumerate(chips)]


def _exchange_start(kind, srcs, after, name):
    n = len(srcs)
    K = 4 if kind == "pair" else 3
    lands = [_hbm(lax.empty((K,) + s.shape[1:], s.dtype)) for s in srcs]

    def body(*refs):
        ins, land = refs[:n], refs[n:2 * n]
        send, recv = refs[2 * n + 1], refs[2 * n + 2]
        token = refs[-1]
        for a in range(n):
            for k, (si, di, dev) in enumerate(_peer_plan(kind, *_place())):
                pltpu.make_async_remote_copy(
                    src_ref=ins[a].at[si], dst_ref=land[a].at[di], send_sem=send.at[K * a + k], recv_sem=recv.at[K * a + k],
                    device_id=dev, device_id_type=MESH).start()
        token[...] = jnp.zeros_like(token)

    out = pl.pallas_call(
        body, name=name,
        in_specs=[HBM] * (2 * n) + [ANY],
        out_specs=[SEM, SEM] + [HBM] * (2 * n) + [VMEM_SPEC],
        out_shape=[pltpu.SemaphoreType.DMA((K * n,)), pltpu.SemaphoreType.DMA((K * n,))] + _hbm_like(srcs) + _hbm_like(lands) + [TOKEN],
        input_output_aliases={i: 2 + i for i in range(2 * n)},
        compiler_params=pltpu.CompilerParams(has_side_effects=EFFECT),
    )(*[_hbm(s) for s in srcs], *lands, after)
    return dict(kind=kind, send=out[0], recv=out[1], srcs=list(out[2:2 + n]), lands=list(out[2 + n:2 + 2 * n]), token=out[-1])


def _exchange_wait(h, after, name):
    n = len(h["srcs"])
    kind = h["kind"]
    K = 4 if kind == "pair" else 3

    def body(*refs):
        ins, land = refs[:n], refs[n:2 * n]
        send, recv = refs[2 * n], refs[2 * n + 1]
        for a in range(n):
            for k, (si, di, dev) in enumerate(_peer_plan(kind, *_place())):
                cp = pltpu.make_async_remote_copy(
                    src_ref=ins[a].at[si], dst_ref=land[a].at[di], send_sem=send.at[K * a + k], recv_sem=recv.at[K * a + k],
                    device_id=dev, device_id_type=MESH)
                cp.wait_send()
                cp.wait_recv()

    out = pl.pallas_call(
        body, name=name,
        in_specs=[HBM] * (2 * n) + [SEM, SEM, ANY],
        out_specs=[HBM] * (2 * n),
        out_shape=_hbm_like(h["srcs"]) + _hbm_like(h["lands"]),
        input_output_aliases={i: i for i in range(2 * n)},
        compiler_params=pltpu.CompilerParams(has_side_effects=EFFECT),
    )(*h["srcs"], *h["lands"], h["send"], h["recv"], after)
    return list(out[:n]), list(out[n:])


def _pair_add(g, got, c_idx, name):
    _, R, C = g.shape
    tr, tc = _tile2(R, C, 512, 16)

    def body(c_ref, a_ref, b_ref, o_ref):
        o_ref[0] = (a_ref[0].astype(f32) + b_ref[0].astype(f32)).astype(o_ref.dtype)

    return pl.pallas_call(
        body, name=name,
        grid_spec=pltpu.PrefetchScalarGridSpec(
            num_scalar_prefetch=1, grid=(4, R // tr, C // tc),
            in_specs=[pl.BlockSpec((1, tr, tc), lambda q, i, j, c: (2 * q + c[0], i, j)),
                      pl.BlockSpec((1, tr, tc), lambda q, i, j, c: (q, i, j))],
            out_specs=pl.BlockSpec((1, tr, tc), lambda q, i, j, c: (q, i, j))),
        out_shape=jax.ShapeDtypeStruct((4, R, C), g.dtype),
        compiler_params=_cparams(("parallel", "parallel", "parallel")),
    )(c_idx, g, got)


def _small_exchange(send, gather, name, after=None):
    R = send.shape[-2]

    def body(in_ref, out_ref, send_sems, recv_sems):
        x, y, c = _place()
        me = 4 * x + 2 * y + c
        out_ref[me] = in_ref[...] if gather else in_ref[me]
        cps = []
        for k in range(1, N_DEV):
            px, py, pc = x ^ ((k >> 2) & 1), y ^ ((k >> 1) & 1), c ^ (k & 1)
            src = in_ref if gather else in_ref.at[4 * px + 2 * py + pc]
            cps.append(pltpu.make_async_remote_copy(
                src_ref=src, dst_ref=out_ref.at[me],
                send_sem=send_sems.at[k - 1], recv_sem=recv_sems.at[k - 1],
                device_id=(px, py, pc), device_id_type=MESH))
        for cp in cps:
            cp.start()
        for cp in cps:
            cp.wait()

    body, in_specs, args = _ordered(body, [pl.BlockSpec(memory_space=pltpu.VMEM)], [send], after)
    return pl.pallas_call(
        body, name=name,
        in_specs=in_specs, out_specs=pl.BlockSpec(memory_space=pltpu.VMEM),
        out_shape=jax.ShapeDtypeStruct((N_DEV, R, LANES), f32),
        scratch_shapes=[pltpu.SemaphoreType.DMA((N_DEV - 1,)), pltpu.SemaphoreType.DMA((N_DEV - 1,))],
    )(*args)


def _sum_blocks(blocks, name):
    def body(in_ref, o_ref):
        s = in_ref[0]
        for d in range(1, N_DEV):
            s = s + in_ref[d]
        o_ref[0] = s

    return pl.pallas_call(body, name=name, out_shape=jax.ShapeDtypeStruct((1,) + blocks.shape[1:], f32))(blocks)


def _rows(n):
    return -(-n // LANES)


def _pack(arrs, total_rows):
    parts = []
    for a in arrs:
        flat = a.reshape(-1).astype(f32)
        parts.append(jnp.pad(flat, (0, _rows(flat.size) * LANES - flat.size)))
    flat = jnp.concatenate(parts)
    return jnp.pad(flat, (0, total_rows * LANES - flat.size)).reshape(total_rows, LANES)


def _unpack(packed, shapes):
    lead = packed.shape[:-2]
    flat = packed.reshape(lead + (-1,))
    out, pos = [], 0
    for s in shapes:
        n = 1
        for d in s:
            n *= d
        out.append(flat[..., pos:pos + n].reshape(lead + tuple(s)))
        pos += _rows(n) * LANES
    return out


def _to_shards(full, axis):
    s = full.shape
    return jnp.moveaxis(full.reshape(s[:axis] + (N_DEV, s[axis] // N_DEV) + s[axis + 1:]), axis, 0)


def _from_shards(sh, axis):
    m = jnp.moveaxis(sh, 0, axis)
    s = m.shape
    return m.reshape(s[:axis] + (s[axis] * s[axis + 1],) + s[axis + 2:])


def kernel(x, meta, ffn_norm, ffn_w_gate, ffn_w_up, ffn_w_down, gla_norm, gla_w_in, gla_w_lr, gla_b_lr, gla_head_norm, gla_w_out, pool_norm, pool_w, pool_b, pool_scale, final_norm, loss_target, m_meta, m_ffn_norm, m_ffn_w_gate, m_ffn_w_up, m_ffn_w_down, m_gla_norm, m_gla_w_in, m_gla_w_lr, m_gla_b_lr, m_gla_head_norm, m_gla_w_out, m_pool_norm, m_pool_w, m_pool_b, m_pool_scale, m_final_norm, v_meta, v_ffn_norm, v_ffn_w_gate, v_ffn_w_up, v_ffn_w_down, v_gla_norm, v_gla_w_in, v_gla_w_lr, v_gla_b_lr, v_gla_head_norm, v_gla_w_out, v_pool_norm, v_pool_w, v_pool_b, v_pool_scale, v_final_norm):
    H = GLA_HEADS
    _, SEQ, D = x.shape
    Fs = ffn_w_gate.shape[-1]
    DK, DV = D // 2, D
    hv = DV // H
    GW = D // POOL_GROUPS
    INW = 2 * DK + 2 * DV + GATE_RANK
    NPK = 2 * DK + 2 * DV + GATE_PAD
    pad = (-N_META) % GLA_CHUNK
    first = pad + N_META
    Lp = first + SEQ
    n_units = ffn_w_gate.shape[0] * ffn_w_gate.shape[1]
    assert first % GLA_CHUNK == 0 and Lp % GLA_CHUNK == 0 and pad >= POOL_GROUPS * 4

    px, py, pc = _place()
    c_idx = jnp.reshape(pc, (1,)).astype(jnp.int32)
    q_idx = jnp.reshape(2 * px + py, (1,)).astype(jnp.int32)
    zero_idx = jnp.zeros((1,), jnp.int32)

    small_sh = [meta, ffn_norm, gla_w_lr, pool_norm, pool_b, pool_scale]
    small_axis = [1, 2, 2, 1, 2, 1]
    sh_shapes = [a.shape for a in small_sh]
    sh_rows = -(-sum(_rows(a.size) for a in small_sh) // 8) * 8
    gathered = _small_exchange(_pack(small_sh, sh_rows), True, "small_gather")
    meta_f, ffn_norm_f, wlr_f, pool_norm_f, pool_b_f, pool_scale_f = [
        _from_shards(a, ax) for a, ax in zip(_unpack(gathered, sh_shapes), small_axis)]
    ffn_norm_f = ffn_norm_f.reshape(n_units, 1, D)
    wlr128 = jnp.pad(wlr_f[0], ((0, GATE_PAD - GATE_RANK), (0, 0)))

    def t_units(w):
        return jnp.swapaxes(w, -1, -2).reshape(n_units, Fs, D)

    ffn_f32 = [t_units(ffn_w_gate), t_units(ffn_w_up), ffn_w_down.reshape(n_units, Fs, D)]
    mixer_f32 = [gla_w_in[0].T[None], gla_w_out, pool_w[0].reshape(1, -1, GW)]
    gather_order = [("ffn0", ffn_f32, 0), ("mixers", mixer_f32, 0)] + [(f"ffn{u}", ffn_f32, u) for u in range(1, n_units)]
    c_lr = 2 * DK + DV
    c_r = 2 * DK + 2 * DV
    gate_blk = c_r // GATE_PAD

    def cast_shards(i, after):
        tag, arrays, u = gather_order[i]
        shards = [_cast_unit(w, u, f"cast_{tag}_{a}", after) for a, w in enumerate(arrays)]
        if tag == "mixers":
            shards[2] = shards[2].reshape(pool_w.shape[1:])
        return shards

    def pass_on(i, h, after):
        tag = gather_order[i][0]
        nxt = cast_shards(i + 1, after) if i + 1 < len(gather_order) else None
        h = _gather_mid(h, after if nxt is None else nxt[0], f"gather_mid_{tag}")
        if nxt is not None:
            nxt = _gather_start(nxt, h["token"], f"gather_start_{gather_order[i + 1][0]}")
        return h, nxt

    def complete(i, h, after):
        tag = gather_order[i][0]
        h = _gather_mid2(h, after, f"gather_mid2_{tag}")
        return _gather_end(h, h["token"], f"gather_end_{tag}")

    xs = jnp.concatenate([jnp.zeros((pad, D), f32), meta_f, x[0]], axis=0)
    saved = {}
    ffn_w = [None] * n_units

    def ffn_f(u, xs, after=None):
        out, h, G, U = _ffn_fwd(xs, ffn_norm_f[u], *ffn_w[u], name=f"ffn_fwd{u}", after=after)
        saved[("ffn", u)] = (xs, h, G, U)
        return out

    def gla_f(xs, win_p, wout_full, after=None):
        hn = _rms_fwd(xs, gla_norm, bf16, "gla_norm_fwd", after=after)
        proj = _mm(hn, win_p, "nt", f32, "gla_proj", tm=1056, tn=896, tk=2048)
        lg = _gate_fwd(proj, wlr128, gla_b_lr, pad, gate_blk, "gla_gate_fwd")
        o, y, states = _gla_fwd(proj, lg, gla_head_norm, H, "gla_core_fwd")
        out = _mm(y, wout_full, "nn", f32, "gla_out", tm=1056, tn=512, tk=2048, residual=xs)
        saved["gla"] = (xs, hn, proj, lg, o, y, states)
        return out

    def pool_f(xs, wpool_full):
        hn = _rms_fwd(xs, pool_norm_f, f32, "pool_norm_fwd")
        pooled = _pool_windows(hn, pad, Lp - pad, "pool_windows_fwd")
        out = _pool_mix_fwd(xs, pooled, wpool_full, pool_b_f.reshape(1, D), pool_scale_f, pad, "pool_mix_fwd")
        saved["pool"] = (xs, pooled)
        return out

    depth = ffn_w_gate.shape[0]
    assert depth == 2 and n_units == 4
    h0 = _gather_start(cast_shards(0, None), gathered, "gather_start_ffn0")
    h0, h1 = pass_on(0, h0, h0["token"])
    ffn_w[0] = complete(0, h0, h1["token"])
    h1, h2 = pass_on(1, h1, ffn_w[0][0])
    xs = ffn_f(0, xs, after=h2["token"])
    win_g, wout_g, wpool_g = complete(1, h1, xs)
    h2, h3 = pass_on(2, h2, wout_g)
    win_full = win_g.reshape(INW, D)
    win_p = jnp.concatenate([win_full[:c_lr], win_full[c_lr + GATE_RANK:], win_full[c_lr:c_lr + GATE_RANK],
                             jnp.zeros((GATE_PAD - GATE_RANK, D), bf16)], axis=0)
    wout_full = wout_g.reshape(DV, D)
    wpool_full = _from_shards(wpool_g, 1)
    xs = gla_f(xs, win_p, wout_full, after=h3["token"])
    ffn_w[1] = complete(2, h2, xs)
    h3, h4 = pass_on(3, h3, ffn_w[1][0])
    xs = ffn_f(1, xs, after=h4["token"])
    ffn_w[2] = complete(3, h3, xs)
    h4, _ = pass_on(4, h4, ffn_w[2][0])
    xs = ffn_f(2, xs, after=h4["token"])
    xs = pool_f(xs, wpool_full)
    ffn_w[3] = complete(4, h4, xs)
    xs = ffn_f(3, xs)
    loss_part, dxs, d_final, dyh = _loss_head(xs, loss_target[0], final_norm.reshape(1, D), first, "loss_head")

    class Reduce:
        def __init__(self, tag, grads, after=None):
            self.tag = tag
            self.h = _exchange_start("pair", grads, loss_part if after is None else after, f"pair_start_{tag}")
            self.token = self.h["token"]

        def mid(self, after):
            grads, got = _exchange_wait(self.h, after, f"pair_wait_{self.tag}")
            self.sums = [_pair_add(g, r, c_idx, f"pair_add_{self.tag}{a}") for a, (g, r) in enumerate(zip(grads, got))]
            self.h = _exchange_start("chips", self.sums, loss_part, f"chips_start_{self.tag}")
            self.token = self.h["token"]

        def end(self, after):
            sums, recv = _exchange_wait(self.h, after, f"chips_wait_{self.tag}")
            return list(zip(sums, recv))

    d_ffn_norm = [None] * n_units
    small_grads = {}

    def ffn_b(u, dY, dyh, prev):
        xs_in, h_, G, U = saved[("ffn", u)]
        wg, wu, wd = ffn_w[u]
        tok = None if prev is None else prev.token
        dG, dU, A = _ffn_bwd_act(dyh, wd, G, U, f"ffn_act{u}", after=tok)
        dh = _ffn_bwd_dh(dG, dU, wg, wu, f"ffn_dh{u}")
        dxs, dg, dyh_next = _rms_bwd(dY, dh, xs_in, ffn_norm_f[u], pad, f"ffn_norm_bwd{u}")
        if prev is not None:
            prev.mid(dxs)
            tok = prev.token
        dwg = _ffn_bwd_wgrad(dG, h_, f"ffn_wgrad_gate{u}", after=tok)
        dwu = _ffn_bwd_wgrad(dU, h_, f"ffn_wgrad_up{u}", after=tok)
        dwd = _ffn_bwd_wgrad(A, dyh, f"ffn_wgrad_down{u}", after=tok)
        d_ffn_norm[u] = dg
        return dxs, dyh_next, Reduce(f"ffn{u}", [dwg, dwu, dwd])

    def gla_b(dY, prev):
        xs_in, hn, proj, lg, o, y, states = saved["gla"]
        dyb = dY.astype(bf16)
        dy = _mm(dyb, wout_full, "nt", f32, "gla_out_dgrad", tm=1056, tn=512, tk=2048, after=prev.token)
        dwout = _mm_tn_full(y, dyb, "gla_out_wgrad", 1024, after=prev.token)
        prev.mid(dwout)
        dq, dk, dv, dr, dlg, dhw = _gla_bwd(dy, proj, lg, o, states, gla_head_norm, H, pad, "gla_core_bwd", after=prev.token)
        dlr, dwlr, dblr = _gate_bwd(dlg, proj, wlr128, gla_b_lr, pad, gate_blk, "gla_gate_bwd")
        dproj = jnp.concatenate([dq, dk, dv, dr, dlr], axis=1)
        dwin_p = _mm_tn_full(dproj, hn, "gla_proj_wgrad", 896)
        dhn = _mm(dproj, win_p, "nn", f32, "gla_proj_dgrad", tm=1056, tn=1024, tk=896)
        dxs, dgn, dyh_next = _rms_bwd(dY, dhn, xs_in, gla_norm, pad, "gla_norm_bwd")
        dwin = jnp.concatenate([dwin_p[:c_lr], dwin_p[c_r:c_r + GATE_RANK], dwin_p[c_lr:c_r]], axis=0)
        small_grads.update(gla_w_lr=dwlr[:GATE_RANK][None], gla_b_lr=dblr, gla_head_norm=dhw, gla_norm=dgn)
        return dxs, dyh_next, Reduce("gla", [dwin.reshape(N_DEV, INW // N_DEV, D), dwout.reshape(N_DEV, DV // N_DEV, D)])

    def pool_b_(dY, prev):
        xs_in, pooled = saved["pool"]
        dp, dw, db, ds = _pool_mix_bwd(dY, pooled, wpool_full, pool_b_f.reshape(1, D), pool_scale_f, pad, "pool_mix_bwd",
                                       after=prev.token)
        dhn = _pool_windows_bwd(dp, pad, "pool_windows_bwd")
        dxs, dgn, dyh_next = _rms_bwd(dY, dhn, xs_in, pool_norm_f, pad, "pool_norm_bwd")
        prev.mid(dxs)
        dws = _to_shards(dw, 1)
        small_grads.update(pool_b=db.reshape(1, POOL_GROUPS, GW), pool_scale=ds, pool_norm=dgn)
        return dxs, dyh_next, Reduce("pool", [dws.reshape(N_DEV, POOL_GROUPS * GW // N_DEV, GW)], after=prev.token)

    sh_names = ["meta", "ffn_norm", "gla_w_lr", "pool_norm", "pool_b", "pool_scale"]
    rep_names = ["gla_norm", "gla_b_lr", "gla_head_norm", "final_norm"]
    rep_w = [gla_norm, gla_b_lr, gla_head_norm, final_norm]
    rep_shapes = [a.shape for a in rep_w]
    rep_rows = -(-sum(_rows(a.size) for a in rep_w) // 8) * 8

    def small_path(dxs0):
        small_grads.update(meta=dxs0[pad:first], ffn_norm=jnp.concatenate(d_ffn_norm, axis=0).reshape(n_units // 2, 2, D),
                           final_norm=d_final.reshape(D))
        by_owner = [_to_shards(small_grads[nm].reshape(full_shape), ax) for nm, full_shape, ax in zip(
            sh_names, [meta_f.shape, (ffn_norm.shape[0], 2, D), wlr_f.shape, pool_norm_f.shape, pool_b_f.shape, pool_scale_f.shape],
            small_axis)]
        rep_pack = _pack([small_grads[nm].reshape(s) for nm, s in zip(rep_names, rep_shapes)], rep_rows)
        send = jnp.stack([
            jnp.concatenate([_pack([g[d] for g in by_owner], sh_rows), rep_pack, loss_part], axis=0) for d in range(N_DEV)])
        total = _sum_blocks(_small_exchange(send, False, "small_reduce"), "small_sum")
        n_small = sh_rows + rep_rows

        def pack_small(sh_list, rep_list):
            return jnp.concatenate([_pack(sh_list, sh_rows), _pack(rep_list, rep_rows)], axis=0)[None]

        w_small = pack_small(small_sh, rep_w)
        m_small = pack_small([m_meta, m_ffn_norm, m_gla_w_lr, m_pool_norm, m_pool_b, m_pool_scale],
                             [m_gla_norm, m_gla_b_lr, m_gla_head_norm, m_final_norm])
        v_small = pack_small([v_meta, v_ffn_norm, v_gla_w_lr, v_pool_norm, v_pool_b, v_pool_scale],
                             [v_gla_norm, v_gla_b_lr, v_gla_head_norm, v_final_norm])
        small_out = _adamw(w_small, m_small, v_small, 0, total[:, :n_small], zero_idx, None, None, "adamw_small")
        small_res = {}
        for kind, packed in zip(("grad", "delta", "new_m", "new_v"), small_out):
            sh_vals = _unpack(packed[0, :sh_rows], sh_shapes)
            rep_vals = _unpack(packed[0, sh_rows:], rep_shapes)
            for nm, val in zip(sh_names + rep_names, sh_vals + rep_vals):
                small_res[(kind, nm)] = val
        return total[0, n_small, 0], small_res, small_out[0]

    def ffn_b_last(dY, dyh, prev):
        xs_in, h_, G, U = saved[("ffn", 0)]
        wg, wu, wd = ffn_w[0]
        dG, dU, A = _ffn_bwd_act(dyh, wd, G, U, "ffn_act0", after=prev.token)
        dwd = _ffn_bwd_wgrad(A, dyh, "ffn_wgrad_down0", after=prev.token)
        r_d = Reduce("ffn0_down", [dwd])
        dh = _ffn_bwd_dh(dG, dU, wg, wu, "ffn_dh0", after=r_d.token)
        dxs, dg, _ = _rms_bwd(dY, dh, xs_in, ffn_norm_f[0], pad, "ffn_norm_bwd0")
        d_ffn_norm[0] = dg
        small = small_path(dxs)
        prev.mid(small[2])
        r_d.mid(prev.token)
        dwg = _ffn_bwd_wgrad(dG, h_, "ffn_wgrad_gate0", after=r_d.token)
        r_g = Reduce("ffn0_gate", [dwg])
        dwu = _ffn_bwd_wgrad(dU, h_, "ffn_wgrad_up0", after=r_g.token)
        r_g.mid(dwu)
        r_u = Reduce("ffn0_up", [dwu], after=r_g.token)
        return dxs, small, (r_g, r_u, r_d)

    dxs, dyh, r3 = ffn_b(3, dxs, dyh, None)
    dxs, dyh, rp = pool_b_(dxs, r3)
    dxs, dyh, r2 = ffn_b(2, dxs, dyh, rp)
    dxs, dyh, r1 = ffn_b(1, dxs, dyh, r2)
    dxs, dyh, rg = gla_b(dxs, r1)
    dxs, (loss, small_res, _), r0 = ffn_b_last(dxs, dyh, rg)
    grad_x = dxs[first:].reshape(x.shape)
    r_last = r0[1]

    big_res = {}

    def adam_one(nm, w, m, v, entry, transposed=False):
        sums, recv = entry
        R, C = sums.shape[1:]
        w1, m1, v1 = ((t[0].T if transposed else t).reshape(1, R, C) for t in (w, m, v))
        out = _adamw(w1, m1, v1, 0, sums, q_idx, recv, None, f"adamw_{nm}", after=r_last.token)
        for kind, val in zip(("grad", "delta", "new_m", "new_v"), out):
            big_res[(kind, nm)] = val[0].T[None] if transposed else val.reshape(w.shape)
        return out[0]

    e_gla = rg.end(dxs)
    done = adam_one("gla_w_in", gla_w_in, m_gla_w_in, v_gla_w_in, e_gla[0], transposed=True)
    done = adam_one("gla_w_out", gla_w_out, m_gla_w_out, v_gla_w_out, e_gla[1])
    done = adam_one("pool_w", pool_w, m_pool_w, v_pool_w, rp.end(done)[0])
    r_last.mid(done)

    ffn_names = ["ffn_w_gate", "ffn_w_up", "ffn_w_down"]
    ffn_wmv = [tuple(t_units(t) for t in (ffn_w_gate, m_ffn_w_gate, v_ffn_w_gate)),
               tuple(t_units(t) for t in (ffn_w_up, m_ffn_w_up, v_ffn_w_up)),
               tuple(t.reshape(n_units, Fs, D) for t in (ffn_w_down, m_ffn_w_down, v_ffn_w_down))]
    ffn_prev = [[lax.empty((n_units, Fs, D), f32) for _ in range(4)] for _ in range(3)]
    order_after = r_last.token
    for u, red in ((3, r3), (2, r2), (1, r1), (0, r0)):
        entries = [r.end(done)[0] for r in red] if u == 0 else red.end(done)
        for a in range(3):
            sums, recv = entries[a]
            ffn_prev[a] = _adamw(*ffn_wmv[a], u, sums, q_idx, recv, ffn_prev[a], f"adamw_{ffn_names[a]}{u}", after=order_after)
            done = order_after = ffn_prev[a][0]
    for a in range(3):
        for kind, val in zip(("grad", "delta", "new_m", "new_v"), ffn_prev[a]):
            val = val.reshape(ffn_w_down.shape)
            big_res[(kind, ffn_names[a])] = val if a == 2 else jnp.swapaxes(val, -1, -2)

    order = ["meta", "ffn_norm", "ffn_w_gate", "ffn_w_up", "ffn_w_down", "gla_norm", "gla_w_in", "gla_w_lr", "gla_b_lr",
             "gla_head_norm", "gla_w_out", "pool_norm", "pool_w", "pool_b", "pool_scale", "final_norm"]
    res = {**small_res, **big_res}
    outs = [loss, grad_x]
    for kind in ("grad", "delta", "new_m", "new_v"):
        outs += [res[(kind, nm)] for nm in order]
    return tuple(outs)
```

```python
import functools

import jax
import jax.numpy as jnp
from jax import lax
from jax.experimental import pallas as pl
from jax.experimental.pallas import tpu as pltpu

f32 = jnp.float32
bf16 = jnp.bfloat16

N_DEV = 8
N_META = 16
GLA_HEADS = 4
GLA_CHUNK = 64
GLA_SUB = 16
GATE_RANK = 16
GATE_PAD = 128
GATE_NORM = 16.0
EPS = 1e-6
POOL_GROUPS = 4
ADAM_LR = 0.001
ADAM_B1 = 0.9
ADAM_B2 = 0.999
ADAM_EPS = 1e-08
ADAM_WD = 0.01
ADAM_STEP = 10
LANES = 128
VMEM_LIMIT_MB = 56

NN = (((1,), (0,)), ((), ()))
NT = (((1,), (1,)), ((), ()))
TN = (((0,), (0,)), ((), ()))
HI = lax.Precision.HIGHEST
MESH = pl.DeviceIdType.MESH
ANY = pl.BlockSpec(memory_space=pl.ANY)


def _cparams(sem=None, vmem_mb=None):
    kw = {}
    if sem is not None:
        kw["dimension_semantics"] = sem
    if vmem_mb is not None:
        kw["vmem_limit_bytes"] = vmem_mb * 2 ** 20
    return pltpu.CompilerParams(**kw)


def _tile(n, target, mult=16):
    best = None
    for t in range(mult, min(n, target) + 1, mult):
        if n % t == 0:
            best = t
    assert best is not None, (n, target, mult)
    return best


def _tile2(R, C, rows, mult):
    if R % mult == 0:
        return _tile(R, rows, mult), C
    return R, _tile(C, 256, LANES)


def _dot(a, b, dims=NN, precision=None):
    return lax.dot_general(a, b, dims, preferred_element_type=f32, precision=precision)


def _sigmoid(x):
    return 1.0 / (1.0 + jnp.exp(-x))


def _row_ids(tile_index, tm):
    return tile_index * tm + lax.broadcasted_iota(jnp.int32, (tm, 1), 0)


def _ordered(body, in_specs, args, after, lead=0):
    if after is None:
        return body, in_specs, args
    pos = lead + len(args)

    def body_without(*refs):
        return body(*refs[:pos], *refs[pos + 1:])

    return body_without, list(in_specs) + [ANY], list(args) + [after]


def _cast_unit(w, unit, name, after=None):
    _, R, C = w.shape
    tr, tc = _tile2(R, C, 256, 16)

    def body(w_ref, o_ref):
        o_ref[...] = w_ref[0].astype(bf16)

    body, in_specs, args = _ordered(body, [pl.BlockSpec((1, tr, tc), lambda i, j: (unit, i, j))], [w], after)
    return pl.pallas_call(
        body, name=name, grid=(R // tr, C // tc),
        in_specs=in_specs, out_specs=pl.BlockSpec((tr, tc), lambda i, j: (i, j)),
        out_shape=jax.ShapeDtypeStruct((R, C), bf16),
        compiler_params=_cparams(("parallel", "parallel")),
    )(*args)


def _rms_fwd(xs, g, out_dtype, name, after=None):
    Lp, D = xs.shape
    tm = _tile(Lp, 528)

    def body(x_ref, g_ref, h_ref):
        x = x_ref[...]
        rstd = lax.rsqrt(jnp.mean(x * x, axis=-1, keepdims=True) + EPS)
        h_ref[...] = (x * rstd * g_ref[...]).astype(out_dtype)

    in_specs = [pl.BlockSpec((tm, D), lambda i: (i, 0)), pl.BlockSpec((1, D), lambda i: (0, 0))]
    body, in_specs, args = _ordered(body, in_specs, [xs, g], after)
    return pl.pallas_call(
        body, name=name, grid=(Lp // tm,),
        in_specs=in_specs,
        out_specs=pl.BlockSpec((tm, D), lambda i: (i, 0)),
        out_shape=jax.ShapeDtypeStruct((Lp, D), out_dtype),
        compiler_params=_cparams(("parallel",)),
    )(*args)


def _rms_bwd(dY, dh, xs, g, pad, name):
    Lp, D = xs.shape
    tm = _tile(Lp, 352)

    def body(dY_ref, dh_ref, x_ref, g_ref, dxs_ref, dg_ref, half_ref):
        i = pl.program_id(0)

        @pl.when(i == 0)
        def _():
            dg_ref[...] = jnp.zeros_like(dg_ref)

        x = x_ref[...]
        rstd = lax.rsqrt(jnp.mean(x * x, axis=-1, keepdims=True) + EPS)
        xhat = x * rstd
        dh_ = dh_ref[...]
        dg_ref[...] += jnp.sum(dh_ * xhat, axis=0, keepdims=True)
        dxh = dh_ * g_ref[...]
        dx = rstd * (dxh - xhat * jnp.mean(dxh * xhat, axis=-1, keepdims=True))
        out = jnp.where(_row_ids(i, tm) >= pad, dY_ref[...] + dx, 0.0)
        dxs_ref[...] = out
        half_ref[...] = (0.5 * out).astype(bf16)

    row = pl.BlockSpec((tm, D), lambda i: (i, 0))
    vec = pl.BlockSpec((1, D), lambda i: (0, 0))
    return pl.pallas_call(
        body, name=name, grid=(Lp // tm,),
        in_specs=[row, row, row, vec], out_specs=[row, vec, row],
        out_shape=[jax.ShapeDtypeStruct((Lp, D), f32), jax.ShapeDtypeStruct((1, D), f32), jax.ShapeDtypeStruct((Lp, D), bf16)],
        compiler_params=_cparams(("arbitrary",)),
    )(dY, dh, xs, g)


def _mm(a, b, mode, out_dtype, name, tm=512, tn=512, tk=512, residual=None, after=None):
    if mode == "nn":
        (M, K), N = a.shape, b.shape[1]
    elif mode == "nt":
        (M, K), N = a.shape, b.shape[0]
    else:
        (K, M), N = a.shape, b.shape[1]
    tm = _tile(M, tm, 16 if mode != "tn" else LANES) if M > tm else M
    tn = _tile(N, tn, LANES) if N > tn else N
    tk = _tile(K, tk, LANES if mode != "tn" else 16) if K > tk else K
    nk = K // tk
    dims = {"nn": NN, "nt": NT, "tn": TN}[mode]

    def body(*refs):
        if residual is None:
            a_ref, b_ref, o_ref, acc = refs
            r_ref = None
        else:
            a_ref, b_ref, r_ref, o_ref, acc = refs
        k = pl.program_id(2)

        @pl.when(k == 0)
        def _():
            acc[...] = jnp.zeros_like(acc)

        acc[...] += _dot(a_ref[...], b_ref[...], dims)

        @pl.when(k == nk - 1)
        def _():
            r = acc[...]
            if r_ref is not None:
                r = r + r_ref[...]
            o_ref[...] = r.astype(out_dtype)

    a_spec = pl.BlockSpec((tk, tm), lambda i, j, k: (k, i)) if mode == "tn" else pl.BlockSpec((tm, tk), lambda i, j, k: (i, k))
    b_spec = pl.BlockSpec((tn, tk), lambda i, j, k: (j, k)) if mode == "nt" else pl.BlockSpec((tk, tn), lambda i, j, k: (k, j))
    o_spec = pl.BlockSpec((tm, tn), lambda i, j, k: (i, j))
    in_specs = [a_spec, b_spec] + ([o_spec] if residual is not None else [])
    args = [a, b] + ([residual] if residual is not None else [])
    body, in_specs, args = _ordered(body, in_specs, args, after)
    return pl.pallas_call(
        body, name=name, grid=(M // tm, N // tn, nk),
        in_specs=in_specs, out_specs=o_spec,
        out_shape=jax.ShapeDtypeStruct((M, N), out_dtype),
        scratch_shapes=[pltpu.VMEM((tm, tn), f32)],
        compiler_params=_cparams(("parallel", "parallel", "arbitrary"), VMEM_LIMIT_MB),
    )(*args)


def _mm_tn_full(a, b, name, tm, after=None):
    K, M = a.shape
    N = b.shape[1]
    tm = _tile(M, tm, LANES)

    def body(a_ref, b_ref, o_ref):
        o_ref[...] = _dot(a_ref[...], b_ref[...], TN).astype(bf16)

    in_specs = [pl.BlockSpec((K, tm), lambda i: (0, i)), pl.BlockSpec((K, N), lambda i: (0, 0), pipeline_mode=pl.Buffered(1))]
    body, in_specs, args = _ordered(body, in_specs, [a, b], after)
    return pl.pallas_call(
        body, name=name, grid=(M // tm,),
        in_specs=in_specs, out_specs=pl.BlockSpec((tm, N), lambda i: (i, 0)),
        out_shape=jax.ShapeDtypeStruct((M, N), bf16),
        compiler_params=_cparams(("parallel",), VMEM_LIMIT_MB),
    )(*args)


def _ffn_fwd(xs, g, wg, wu, wd, name, after=None):
    Lp, D = xs.shape
    nd, Fs, _ = wg.shape
    tm = _tile(Lp, 704)
    once = pl.Buffered(1)

    def body(x_ref, g_ref, wg_ref, wu_ref, wd_ref, out_ref, h_ref, G_ref, U_ref, hs, acc):
        j = pl.program_id(1)

        @pl.when(j == 0)
        def _():
            x = x_ref[...]
            rstd = lax.rsqrt(jnp.mean(x * x, axis=-1, keepdims=True) + EPS)
            h = (x * rstd * g_ref[...]).astype(bf16)
            hs[...] = h
            h_ref[...] = h
            acc[...] = jnp.zeros_like(acc)

        h = hs[...]
        G = _dot(h, wg_ref[0], NT)
        U = _dot(h, wu_ref[0], NT)
        G_ref[0] = G.astype(bf16)
        U_ref[0] = U.astype(bf16)
        A = (G * _sigmoid(G) * U).astype(bf16)
        acc[...] += _dot(A, wd_ref[0])

        @pl.when(j == nd - 1)
        def _():
            out_ref[...] = x_ref[...] + 0.5 * acc[...]

    row_f = pl.BlockSpec((tm, D), lambda i, j: (i, 0), pipeline_mode=once)
    act = pl.BlockSpec((1, tm, Fs), lambda i, j: (j, i, 0))
    wrow = pl.BlockSpec((1, Fs, D), lambda i, j: (j, 0, 0))
    in_specs = [row_f, pl.BlockSpec((1, D), lambda i, j: (0, 0)), wrow, wrow, wrow]
    body, in_specs, args = _ordered(body, in_specs, [xs, g, wg, wu, wd], after)
    return pl.pallas_call(
        body, name=name, grid=(Lp // tm, nd),
        in_specs=in_specs,
        out_specs=[row_f, pl.BlockSpec((tm, D), lambda i, j: (i, 0), pipeline_mode=once), act, act],
        out_shape=[jax.ShapeDtypeStruct((Lp, D), f32), jax.ShapeDtypeStruct((Lp, D), bf16),
                   jax.ShapeDtypeStruct((nd, Lp, Fs), bf16), jax.ShapeDtypeStruct((nd, Lp, Fs), bf16)],
        scratch_shapes=[pltpu.VMEM((tm, D), bf16), pltpu.VMEM((tm, D), f32)],
        compiler_params=_cparams(("parallel", "arbitrary"), VMEM_LIMIT_MB),
    )(*args)


def _ffn_bwd_act(dyh, wd, G, U, name, after=None):
    Lp, D = dyh.shape
    nd, Fs, _ = wd.shape
    tm = _tile(Lp, 704)

    def body(dyh_ref, wd_ref, G_ref, U_ref, dG_ref, dU_ref, A_ref):
        dA = _dot(dyh_ref[...], wd_ref[0], NT)
        Gf = G_ref[0].astype(f32)
        Uf = U_ref[0].astype(f32)
        s = _sigmoid(Gf)
        silu = Gf * s
        dG_ref[0] = (dA * Uf * (s * (1.0 + Gf * (1.0 - s)))).astype(bf16)
        dU_ref[0] = (dA * silu).astype(bf16)
        A_ref[0] = (silu * Uf).astype(bf16)

    act = pl.BlockSpec((1, tm, Fs), lambda j, i: (j, i, 0))
    act_s = jax.ShapeDtypeStruct((nd, Lp, Fs), bf16)
    in_specs = [pl.BlockSpec((tm, D), lambda j, i: (i, 0)), pl.BlockSpec((1, Fs, D), lambda j, i: (j, 0, 0)), act, act]
    body, in_specs, args = _ordered(body, in_specs, [dyh, wd, G, U], after)
    return pl.pallas_call(
        body, name=name, grid=(nd, Lp // tm),
        in_specs=in_specs, out_specs=[act, act, act], out_shape=[act_s, act_s, act_s],
        compiler_params=_cparams(("parallel", "parallel"), VMEM_LIMIT_MB),
    )(*args)


def _ffn_bwd_dh(dG, dU, wg, wu, name, after=None):
    nd, Lp, Fs = dG.shape
    D = wg.shape[2]
    tm = _tile(Lp, 1056)

    def body(dG_ref, dU_ref, wg_ref, wu_ref, dh_ref, acc):
        j = pl.program_id(1)

        @pl.when(j == 0)
        def _():
            acc[...] = jnp.zeros_like(acc)

        acc[...] += _dot(dG_ref[0], wg_ref[0]) + _dot(dU_ref[0], wu_ref[0])

        @pl.when(j == nd - 1)
        def _():
            dh_ref[...] = acc[...]

    act = pl.BlockSpec((1, tm, Fs), lambda i, j: (j, i, 0))
    wrow = pl.BlockSpec((1, Fs, D), lambda i, j: (j, 0, 0))
    body, in_specs, args = _ordered(body, [act, act, wrow, wrow], [dG, dU, wg, wu], after)
    return pl.pallas_call(
        body, name=name, grid=(Lp // tm, nd),
        in_specs=in_specs,
        out_specs=pl.BlockSpec((tm, D), lambda i, j: (i, 0), pipeline_mode=pl.Buffered(1)),
        out_shape=jax.ShapeDtypeStruct((Lp, D), f32),
        scratch_shapes=[pltpu.VMEM((tm, D), f32)],
        compiler_params=_cparams(("parallel", "arbitrary"), VMEM_LIMIT_MB),
    )(*args)


def _ffn_bwd_wgrad(act, rows, name, after=None):
    nd, Lp, Fs = act.shape
    D = rows.shape[1]

    def body(a_ref, r_ref, o_ref):
        o_ref[0] = _dot(a_ref[0], r_ref[...], TN).astype(bf16)

    in_specs = [pl.BlockSpec((1, Lp, Fs), lambda j: (j, 0, 0)),
                pl.BlockSpec((Lp, D), lambda j: (0, 0), pipeline_mode=pl.Buffered(1))]
    body, in_specs, args = _ordered(body, in_specs, [act, rows], after)
    return pl.pallas_call(
        body, name=name, grid=(nd,),
        in_specs=in_specs, out_specs=pl.BlockSpec((1, Fs, D), lambda j: (j, 0, 0)),
        out_shape=jax.ShapeDtypeStruct((nd, Fs, D), bf16),
        compiler_params=_cparams(("parallel",), VMEM_LIMIT_MB),
    )(*args)


def _gate_fwd(proj, wlr, blr, pad, gate_blk, name):
    Lp = proj.shape[0]
    DK = wlr.shape[1]
    tm = _tile(Lp, 528)

    def body(lr_ref, w_ref, b_ref, lg_ref):
        z = _dot(lr_ref[...].astype(bf16), w_ref[...].astype(bf16)) + b_ref[...]
        ls = jnp.minimum(z, 0.0) - jnp.log(1.0 + jnp.exp(-jnp.abs(z)))
        lg_ref[...] = jnp.where(_row_ids(pl.program_id(0), tm) >= pad, ls * (1.0 / GATE_NORM), 0.0)

    return pl.pallas_call(
        body, name=name, grid=(Lp // tm,),
        in_specs=[pl.BlockSpec((tm, GATE_PAD), lambda i: (i, gate_blk)),
                  pl.BlockSpec((GATE_PAD, DK), lambda i: (0, 0)), pl.BlockSpec((1, DK), lambda i: (0, 0))],
        out_specs=pl.BlockSpec((tm, DK), lambda i: (i, 0)),
        out_shape=jax.ShapeDtypeStruct((Lp, DK), f32),
        compiler_params=_cparams(("parallel",)),
    )(proj, wlr, blr)


def _gate_bwd(dlg, proj, wlr, blr, pad, gate_blk, name):
    Lp = proj.shape[0]
    DK = wlr.shape[1]
    tm = _tile(Lp, 528)

    def body(dlg_ref, lr_ref, w_ref, b_ref, dlr_ref, dw_ref, db_ref):
        i = pl.program_id(0)

        @pl.when(i == 0)
        def _():
            dw_ref[...] = jnp.zeros_like(dw_ref)
            db_ref[...] = jnp.zeros_like(db_ref)

        lr = lr_ref[...].astype(bf16)
        w = w_ref[...].astype(bf16)
        z = _dot(lr, w) + b_ref[...]
        dz = jnp.where(_row_ids(i, tm) >= pad, dlg_ref[...] * _sigmoid(-z) * (1.0 / GATE_NORM), 0.0)
        dzb = dz.astype(bf16)
        dlr_ref[...] = _dot(dzb, w, NT).astype(bf16)
        dw_ref[...] += _dot(lr, dzb, TN)
        db_ref[...] += jnp.sum(dz, axis=0, keepdims=True)

    return pl.pallas_call(
        body, name=name, grid=(Lp // tm,),
        in_specs=[pl.BlockSpec((tm, DK), lambda i: (i, 0)), pl.BlockSpec((tm, GATE_PAD), lambda i: (i, gate_blk)),
                  pl.BlockSpec((GATE_PAD, DK), lambda i: (0, 0)), pl.BlockSpec((1, DK), lambda i: (0, 0))],
        out_specs=[pl.BlockSpec((tm, GATE_PAD), lambda i: (i, 0)), pl.BlockSpec((GATE_PAD, DK), lambda i: (0, 0)),
                   pl.BlockSpec((1, DK), lambda i: (0, 0))],
        out_shape=[jax.ShapeDtypeStruct((Lp, GATE_PAD), bf16), jax.ShapeDtypeStruct((GATE_PAD, DK), f32),
                   jax.ShapeDtypeStruct((1, DK), f32)],
        compiler_params=_cparams(("arbitrary",)),
    )(dlg, proj, wlr, blr)


def _chunk_decay(lg):
    C = lg.shape[0]
    r = lax.broadcasted_iota(jnp.int32, (C, C), 0)
    c = lax.broadcasted_iota(jnp.int32, (C, C), 1)
    return _dot(jnp.where(r >= c, 1.0, 0.0).astype(f32), lg, NN, HI)


def _col(v):
    return jnp.transpose(jnp.broadcast_to(v, (8, v.shape[1])))[:, 0:1]


def _intra_scores(q, k, b, A_ref):
    C = q.shape[0]
    S = GLA_SUB
    A_ref[...] = jnp.zeros_like(A_ref)
    ri = lax.broadcasted_iota(jnp.int32, (S, 1), 0)
    for I in range(C // S):
        lo = S * I
        qI, bI = q[lo:lo + S], b[lo:lo + S]
        if I > 0:
            bref = b[lo - 1:lo]
            qs = qI * jnp.exp(bI - bref)
            ks = k[:lo] * jnp.exp(bref - b[:lo])
            A_ref[lo:lo + S, 0:lo] = _dot(qs, ks, NT, HI)
        for jj in range(S):
            j = lo + jj
            P = jnp.exp(jnp.minimum(bI - b[j:j + 1], 0.0))
            a = jnp.sum(qI * P * k[j:j + 1], axis=1, keepdims=True)
            A_ref[lo:lo + S, j:j + 1] = jnp.where(ri >= jj, a, 0.0)


def _intra_grads(q, k, b, dA, dq_ref, dk_ref):
    C = q.shape[0]
    S = GLA_SUB
    ri = lax.broadcasted_iota(jnp.int32, (S, 1), 0)
    for I in range(C // S):
        lo = S * I
        qI, bI = q[lo:lo + S], b[lo:lo + S]
        dqI = jnp.zeros_like(qI)
        if I > 0:
            bref = b[lo - 1:lo]
            eq = jnp.exp(bI - bref)
            ek = jnp.exp(bref - b[:lo])
            qs = qI * eq
            ks = k[:lo] * ek
            dAI = dA[lo:lo + S, 0:lo]
            dqI = dqI + _dot(dAI, ks, NN, HI) * eq
            dk_ref[0:lo, :] += _dot(dAI, qs, TN, HI) * ek
        for jj in range(S):
            j = lo + jj
            P = jnp.exp(jnp.minimum(bI - b[j:j + 1], 0.0))
            t = jnp.where(ri >= jj, dA[lo:lo + S, j:j + 1], 0.0) * P
            dqI = dqI + t * k[j:j + 1]
            dk_ref[j:j + 1, :] += jnp.sum(t * qI, axis=0, keepdims=True)
        dq_ref[lo:lo + S, :] += dqI


GLA_HEADS_PER_STEP = 2


def _gla_fwd(proj, lg, hnw, H, name):
    Lp = proj.shape[0]
    DK = lg.shape[1]
    hk = DK // H
    hv = hnw.shape[1]
    DV = hv * H
    C = GLA_CHUNK
    NC = Lp // C
    HS = GLA_HEADS_PER_STEP
    G = H // HS
    scale = float(hk) ** -0.5
    kq, kv, kr = G, (2 * DK) // (HS * hv), (2 * DK) // (HS * hv) + G

    def body(q_ref, k_ref, v_ref, r_ref, lg_ref, w_ref, o_ref, y_ref, s_ref, S_scr, A_scr):
        c = pl.program_id(1)

        @pl.when(c == 0)
        def _():
            S_scr[...] = jnp.zeros_like(S_scr)

        for hh in range(HS):
            ck, cv = slice(hh * hk, (hh + 1) * hk), slice(hh * hv, (hh + 1) * hv)
            q = q_ref[:, ck] * scale
            k = k_ref[:, ck]
            v = v_ref[:, cv]
            b = _chunk_decay(lg_ref[:, ck])
            bl = b[C - 1:C]
            S = S_scr[hh]
            s_ref[hh, 0] = S
            _intra_scores(q, k, b, A_scr.at[hh])
            vb = v.astype(bf16)
            o = _dot((q * jnp.exp(b)).astype(bf16), S.astype(bf16)) + _dot(A_scr[hh].astype(bf16), vb)
            kb = (k * jnp.exp(bl - b)).astype(bf16)
            S_scr[hh] = jnp.exp(_col(bl)) * S + _dot(kb, vb, TN)
            o_ref[:, cv] = o
            on = o * lax.rsqrt(jnp.mean(o * o, axis=-1, keepdims=True) + EPS) * w_ref[...]
            r = r_ref[:, cv]
            y_ref[:, cv] = (on * (r * _sigmoid(r))).astype(bf16)

    return pl.pallas_call(
        body, name=name, grid=(G, NC),
        in_specs=[pl.BlockSpec((C, HS * hk), lambda g, c: (c, g)),
                  pl.BlockSpec((C, HS * hk), lambda g, c: (c, kq + g)),
                  pl.BlockSpec((C, HS * hv), lambda g, c: (c, kv + g)),
                  pl.BlockSpec((C, HS * hv), lambda g, c: (c, kr + g)),
                  pl.BlockSpec((C, HS * hk), lambda g, c: (c, g)),
                  pl.BlockSpec((1, hv), lambda g, c: (0, 0))],
        out_specs=[pl.BlockSpec((C, HS * hv), lambda g, c: (c, g)), pl.BlockSpec((C, HS * hv), lambda g, c: (c, g)),
                   pl.BlockSpec((HS, 1, hk, hv), lambda g, c: (g, c, 0, 0))],
        out_shape=[jax.ShapeDtypeStruct((Lp, DV), f32), jax.ShapeDtypeStruct((Lp, DV), bf16),
                   jax.ShapeDtypeStruct((H, NC, hk, hv), f32)],
        scratch_shapes=[pltpu.VMEM((HS, hk, hv), f32), pltpu.VMEM((HS, C, C), f32)],
        compiler_params=_cparams(("parallel", "arbitrary")),
    )(proj, proj, proj, proj, lg, hnw)


def _gla_bwd(dy, proj, lg, o, states, hnw, H, pad, name, after=None):
    Lp = proj.shape[0]
    DK = lg.shape[1]
    hk = DK // H
    hv = hnw.shape[1]
    DV = hv * H
    C = GLA_CHUNK
    NC = Lp // C
    HS = GLA_HEADS_PER_STEP
    G = H // HS
    scale = float(hk) ** -0.5
    kq, kv, kr = G, (2 * DK) // (HS * hv), (2 * DK) // (HS * hv) + G

    def body(dy_ref, q_ref, k_ref, v_ref, r_ref, lg_ref, o_ref, s_ref, sn_ref, w_ref,
             dq_ref, dk_ref, dv_ref, dr_ref, dlg_ref, dw_ref, dS_scr, A_scr, dq_s, dk_s):
        g = pl.program_id(0)
        cc = pl.program_id(1)
        c = NC - 1 - cc

        @pl.when(cc == 0)
        def _():
            dS_scr[...] = jnp.zeros_like(dS_scr)

        @pl.when((cc == 0) & (g == 0))
        def _():
            dw_ref[...] = jnp.zeros_like(dw_ref)

        keep = (c * C + lax.broadcasted_iota(jnp.int32, (C, 1), 0)) >= pad
        ri = lax.broadcasted_iota(jnp.int32, (C, C), 0)
        ci = lax.broadcasted_iota(jnp.int32, (C, C), 1)
        w = w_ref[...]
        for hh in range(HS):
            ck, cv = slice(hh * hk, (hh + 1) * hk), slice(hh * hv, (hh + 1) * hv)
            o_ = o_ref[:, cv]
            rs = lax.rsqrt(jnp.mean(o_ * o_, axis=-1, keepdims=True) + EPS)
            ohat = o_ * rs
            r = r_ref[:, cv]
            sg = _sigmoid(r)
            dy_ = dy_ref[:, cv]
            d_on = dy_ * (r * sg)
            dr_ref[:, cv] = jnp.where(keep, dy_ * (ohat * w) * (sg * (1.0 + r * (1.0 - sg))), 0.0).astype(bf16)
            dw_ref[...] += jnp.sum(d_on * ohat, axis=0, keepdims=True)
            d_oh = d_on * w
            do = rs * (d_oh - ohat * jnp.mean(d_oh * ohat, axis=-1, keepdims=True))
            dob = do.astype(bf16)
            q = q_ref[:, ck] * scale
            k = k_ref[:, ck]
            vb = v_ref[:, cv].astype(bf16)
            b = _chunk_decay(lg_ref[:, ck])
            bl = b[C - 1:C]
            eb = jnp.exp(b)
            ekb = jnp.exp(bl - b)
            S = s_ref[hh, 0]
            dS = dS_scr[hh]
            dSb = dS.astype(bf16)
            _intra_scores(q, k, b, A_scr.at[hh])
            dA = jnp.where(ri >= ci, _dot(dob, vb, NT), 0.0)
            kb = (k * ekb).astype(bf16)
            qb = (q * eb).astype(bf16)
            dv = _dot(A_scr[hh].astype(bf16), dob, TN) + _dot(kb, dSb)
            dq_s[hh] = _dot(dob, S.astype(bf16), NT) * eb
            dk_s[hh] = _dot(vb, dSb, NT) * ekb
            dS_scr[hh] = _dot(qb, dob, TN) + jnp.exp(_col(bl)) * dS
            _intra_grads(q, k, b, dA, dq_s.at[hh], dk_s.at[hh])
            dq = dq_s[hh]
            dk = dk_s[hh]
            Dm = q * dq - k * dk
            after_rows = _dot(jnp.ones((8, hv), f32), sn_ref[hh, 0] * dS, NT, HI)[0:1]
            dlg = _dot(jnp.where(ri <= ci, 1.0, 0.0).astype(f32), Dm, NN, HI) + after_rows
            dlg_ref[:, ck] = jnp.where(keep, dlg, 0.0)
            dq_ref[:, ck] = jnp.where(keep, dq * scale, 0.0).astype(bf16)
            dk_ref[:, ck] = jnp.where(keep, dk, 0.0).astype(bf16)
            dv_ref[:, cv] = jnp.where(keep, dv, 0.0).astype(bf16)

    rev = lambda cc: NC - 1 - cc
    bk = lambda off: pl.BlockSpec((C, HS * hk), lambda g, cc: (rev(cc), off + g))
    bv = lambda off: pl.BlockSpec((C, HS * hv), lambda g, cc: (rev(cc), off + g))
    in_specs = [bv(0), bk(0), bk(kq), bv(kv), bv(kr), bk(0), bv(0),
                pl.BlockSpec((HS, 1, hk, hv), lambda g, cc: (g, rev(cc), 0, 0)),
                pl.BlockSpec((HS, 1, hk, hv), lambda g, cc: (g, jnp.minimum(rev(cc) + 1, NC - 1), 0, 0)),
                pl.BlockSpec((1, hv), lambda g, cc: (0, 0))]
    body, in_specs, args = _ordered(body, in_specs, [dy, proj, proj, proj, proj, lg, o, states, states, hnw], after)
    return pl.pallas_call(
        body, name=name, grid=(G, NC),
        in_specs=in_specs,
        out_specs=[bk(0), bk(0), bv(0), bv(0), bk(0), pl.BlockSpec((1, hv), lambda g, cc: (0, 0))],
        out_shape=[jax.ShapeDtypeStruct((Lp, DK), bf16), jax.ShapeDtypeStruct((Lp, DK), bf16),
                   jax.ShapeDtypeStruct((Lp, DV), bf16), jax.ShapeDtypeStruct((Lp, DV), bf16),
                   jax.ShapeDtypeStruct((Lp, DK), f32), jax.ShapeDtypeStruct((1, hv), f32)],
        scratch_shapes=[pltpu.VMEM((HS, hk, hv), f32), pltpu.VMEM((HS, C, C), f32),
                        pltpu.VMEM((HS, C, hk), f32), pltpu.VMEM((HS, C, hk), f32)],
        compiler_params=_cparams(("arbitrary", "arbitrary")),
    )(*args)


def _window_sums(x, back):
    n = x.shape[0]
    out = []
    s = x
    for w in (1, 2, 4, 8):
        s = s + pltpu.roll(s, w if back else n - w, 0)
        out.append(s)
    return out


def _pool_windows(hn, pad, n_real, name):
    Lp, D = hn.shape
    GW = D // POOL_GROUPS
    cb = min(GW, 256)
    per = GW // cb

    def body(h_ref, p_ref):
        g = pl.program_id(0) // per
        x = h_ref[...]
        s2, s4, s8, s16 = _window_sums(x, True)
        sel = jnp.where(g == 0, s2, jnp.where(g == 1, s4, jnp.where(g == 2, s8, s16)))
        win = jnp.left_shift(2, g).astype(f32)
        rows = lax.broadcasted_iota(jnp.int32, (Lp, 1), 0)
        t = (rows - pad).astype(f32)
        cnt = jnp.minimum(jnp.maximum(t, 0.0) + 1.0, win)
        p_ref[...] = jnp.where(rows >= pad, sel / cnt - x, 0.0).astype(bf16)

    return pl.pallas_call(
        body, name=name, grid=(D // cb,),
        in_specs=[pl.BlockSpec((Lp, cb), lambda i: (0, i))],
        out_specs=pl.BlockSpec((Lp, cb), lambda i: (0, i)),
        out_shape=jax.ShapeDtypeStruct((Lp, D), bf16),
        compiler_params=_cparams(("parallel",)),
    )(hn)


def _pool_windows_bwd(dp, pad, name):
    Lp, D = dp.shape
    GW = D // POOL_GROUPS
    cb = min(GW, 256)
    per = GW // cb

    def body(dp_ref, dh_ref):
        g = pl.program_id(0) // per
        rows = lax.broadcasted_iota(jnp.int32, (Lp, 1), 0)
        d = jnp.where(rows >= pad, dp_ref[...], 0.0)
        win = jnp.left_shift(2, g).astype(f32)
        t = (rows - pad).astype(f32)
        cnt = jnp.minimum(jnp.maximum(t, 0.0) + 1.0, win)
        s2, s4, s8, s16 = _window_sums(d / cnt, False)
        sel = jnp.where(g == 0, s2, jnp.where(g == 1, s4, jnp.where(g == 2, s8, s16)))
        dh_ref[...] = jnp.where(rows >= pad, sel - d, 0.0)

    return pl.pallas_call(
        body, name=name, grid=(D // cb,),
        in_specs=[pl.BlockSpec((Lp, cb), lambda i: (0, i))],
        out_specs=pl.BlockSpec((Lp, cb), lambda i: (0, i)),
        out_shape=jax.ShapeDtypeStruct((Lp, D), f32),
        compiler_params=_cparams(("parallel",)),
    )(dp)


def _pool_mix_fwd(xs, pooled, w, bias, scale, pad, name):
    Lp, D = xs.shape
    GW = D // POOL_GROUPS
    tm = _tile(Lp, 1056)

    def body(x_ref, p_ref, w_ref, b_ref, s_ref, o_ref):
        z = _dot(p_ref[...], w_ref[0]) + b_ref[...]
        keep = _row_ids(pl.program_id(1), tm) >= pad
        o_ref[...] = x_ref[...] + jnp.where(keep, z * s_ref[...], 0.0)

    blk = pl.BlockSpec((tm, GW), lambda g, i: (i, g))
    vec = pl.BlockSpec((1, GW), lambda g, i: (0, g))
    return pl.pallas_call(
        body, name=name, grid=(POOL_GROUPS, Lp // tm),
        in_specs=[blk, blk, pl.BlockSpec((1, GW, GW), lambda g, i: (g, 0, 0)), vec, vec],
        out_specs=blk, out_shape=jax.ShapeDtypeStruct((Lp, D), f32),
        compiler_params=_cparams(("parallel", "parallel")),
    )(xs, pooled, w, bias, scale)


def _pool_mix_bwd(dY, pooled, w, bias, scale, pad, name, after=None):
    Lp, D = dY.shape
    GW = D // POOL_GROUPS
    tm = _tile(Lp, 1056)
    nm = Lp // tm

    def body(dY_ref, p_ref, w_ref, b_ref, s_ref, dp_ref, dw_ref, db_ref, ds_ref, acc):
        i = pl.program_id(1)

        @pl.when(i == 0)
        def _():
            acc[...] = jnp.zeros_like(acc)
            db_ref[...] = jnp.zeros_like(db_ref)
            ds_ref[...] = jnp.zeros_like(ds_ref)

        keep = _row_ids(i, tm) >= pad
        dY_ = jnp.where(keep, dY_ref[...], 0.0)
        p = p_ref[...]
        z = _dot(p, w_ref[0]) + b_ref[...]
        ds_ref[...] += jnp.sum(dY_ * z, axis=0, keepdims=True)
        dz = dY_ * s_ref[...]
        db_ref[...] += jnp.sum(dz, axis=0, keepdims=True)
        dzb = dz.astype(bf16)
        acc[...] += _dot(p, dzb, TN)
        dp_ref[...] = _dot(dzb, w_ref[0], NT)

        @pl.when(i == nm - 1)
        def _():
            dw_ref[0] = acc[...].astype(bf16)

    blk = pl.BlockSpec((tm, GW), lambda g, i: (i, g))
    vec = pl.BlockSpec((1, GW), lambda g, i: (0, g))
    wsp = pl.BlockSpec((1, GW, GW), lambda g, i: (g, 0, 0))
    body, in_specs, args = _ordered(body, [blk, blk, wsp, vec, vec], [dY, pooled, w, bias, scale], after)
    return pl.pallas_call(
        body, name=name, grid=(POOL_GROUPS, nm),
        in_specs=in_specs, out_specs=[blk, wsp, vec, vec],
        out_shape=[jax.ShapeDtypeStruct((Lp, D), f32), jax.ShapeDtypeStruct((POOL_GROUPS, GW, GW), bf16),
                   jax.ShapeDtypeStruct((1, D), f32), jax.ShapeDtypeStruct((1, D), f32)],
        scratch_shapes=[pltpu.VMEM((GW, GW), f32)],
        compiler_params=_cparams(("parallel", "arbitrary")),
    )(*args)


def _loss_head(xs, target, g, first, name):
    Lp, D = xs.shape
    tm = GLA_CHUNK
    off = first // tm

    def body(x_ref, t_ref, g_ref, loss_ref, dxs_ref, dg_ref, half_ref):
        i = pl.program_id(0)

        @pl.when(i == 0)
        def _():
            loss_ref[...] = jnp.zeros_like(loss_ref)
            dg_ref[...] = jnp.zeros_like(dg_ref)

        @pl.when(i < off)
        def _():
            dxs_ref[...] = jnp.zeros_like(dxs_ref)
            half_ref[...] = jnp.zeros_like(half_ref)

        @pl.when(i >= off)
        def _():
            x = x_ref[...]
            rstd = lax.rsqrt(jnp.mean(x * x, axis=-1, keepdims=True) + EPS)
            xhat = x * rstd
            gg = g_ref[...]
            err = xhat * gg - t_ref[...]
            loss_ref[...] += 0.5 * jnp.sum(jnp.mean(err * err, axis=-1, keepdims=True))
            dy = err * (1.0 / D)
            dg_ref[...] += jnp.sum(dy * xhat, axis=0, keepdims=True)
            dxh = dy * gg
            out = rstd * (dxh - xhat * jnp.mean(dxh * xhat, axis=-1, keepdims=True))
            dxs_ref[...] = out
            half_ref[...] = (0.5 * out).astype(bf16)

    row = pl.BlockSpec((tm, D), lambda i: (i, 0))
    return pl.pallas_call(
        body, name=name, grid=(Lp // tm,),
        in_specs=[row, pl.BlockSpec((tm, D), lambda i: (jnp.maximum(i - off, 0), 0)), pl.BlockSpec((1, D), lambda i: (0, 0))],
        out_specs=[pl.BlockSpec((8, LANES), lambda i: (0, 0)), row, pl.BlockSpec((1, D), lambda i: (0, 0)), row],
        out_shape=[jax.ShapeDtypeStruct((8, LANES), f32), jax.ShapeDtypeStruct((Lp, D), f32),
                   jax.ShapeDtypeStruct((1, D), f32), jax.ShapeDtypeStruct((Lp, D), bf16)],
        compiler_params=_cparams(("arbitrary",)),
    )(xs, target, g)


def _adam_math(w, g, m, v):
    m2 = ADAM_B1 * m + (1.0 - ADAM_B1) * g
    v2 = ADAM_B2 * v + (1.0 - ADAM_B2) * (g * g)
    m_hat = m2 / (1.0 - ADAM_B1 ** ADAM_STEP)
    v_hat = v2 / (1.0 - ADAM_B2 ** ADAM_STEP)
    delta = -ADAM_LR * (m_hat / (jnp.sqrt(v_hat) + ADAM_EPS) + ADAM_WD * w)
    return delta, m2, v2


def _adamw(w, m, v, unit, own, own_idx, recv, prev, name, after=None):
    U, R, C = w.shape
    tr, tc = _tile2(R, C, 256, 8 if own.dtype == f32 and recv is None else 16)
    n_recv = 0 if recv is None else recv.shape[0]

    def body(idx_ref, w_ref, m_ref, v_ref, own_ref, *rest):
        rest = list(rest)
        recv_refs = [rest.pop(0) for _ in range(n_recv)]
        if prev is not None:
            rest = rest[4:]
        g_ref, d_ref, m2_ref, v2_ref = rest
        g = own_ref[0].astype(f32)
        for r_ref in recv_refs:
            g = g + r_ref[0].astype(f32)
        delta, m2, v2 = _adam_math(w_ref[0], g, m_ref[0], v_ref[0])
        g_ref[0] = g
        d_ref[0] = delta
        m2_ref[0] = m2
        v2_ref[0] = v2

    blk = pl.BlockSpec((1, tr, tc), lambda i, j, idx: (unit, i, j))
    in_specs = [blk, blk, blk, pl.BlockSpec((1, tr, tc), lambda i, j, idx: (idx[0], i, j))]
    args = [w, m, v, own]
    for p in range(n_recv):
        in_specs.append(pl.BlockSpec((1, tr, tc), lambda i, j, idx, p=p: (p, i, j)))
        args.append(recv)
    aliases = {}
    if prev is not None:
        for t in range(4):
            aliases[1 + len(args) + t] = t
        in_specs += [ANY] * 4
        args += list(prev)
    body, in_specs, args = _ordered(body, in_specs, args, after, lead=1)
    out = jax.ShapeDtypeStruct((U, R, C), f32)
    return pl.pallas_call(
        body, name=name,
        grid_spec=pltpu.PrefetchScalarGridSpec(
            num_scalar_prefetch=1, grid=(R // tr, C // tc), in_specs=in_specs, out_specs=[blk] * 4),
        out_shape=[out] * 4, input_output_aliases=aliases,
        compiler_params=_cparams(("parallel", "parallel")),
    )(own_idx, *args)


def _place():
    return lax.axis_index("x"), lax.axis_index("y"), lax.axis_index("c")


HBM = pl.BlockSpec(memory_space=pltpu.HBM)
SEM = pl.BlockSpec(memory_space=pltpu.SEMAPHORE)
VMEM_SPEC = pl.BlockSpec(memory_space=pltpu.VMEM)
EFFECT = pltpu.SideEffectType.DATAFLOW_SIDE_EFFECTING
TOKEN = jax.ShapeDtypeStruct((8, LANES), f32)


def _hbm(x):
    return pltpu.with_memory_space_constraint(x, pltpu.HBM)


def _hbm_like(xs):
    return [pltpu.HBM(x.shape, x.dtype) for x in xs]


def _slot(px, py, pc):
    return 4 * px + 2 * py + pc


def _halves(ref):
    n = ref.shape[0]
    cut = n // 2 if n < 32 else (n // 2) // 16 * 16
    return ref.at[pl.ds(0, cut)], ref.at[pl.ds(cut, n - cut)]


def _gather_start(shards, after, name):
    n = len(shards)
    me = _slot(*_place())
    bufs = [lax.dynamic_update_slice(lax.empty((N_DEV,) + s.shape, s.dtype), s[None], (me,) + (0,) * s.ndim) for s in shards]

    def body(*refs):
        ins, land = refs[:n], refs[n:2 * n]
        send, recv = refs[2 * n + 1], refs[2 * n + 2]
        token = refs[-1]
        x, y, c = _place()
        to = [(x, y, 1 - c), (1 - x, y, c), (x, 1 - y, c)]
        for a in range(n):
            for k, dev in enumerate(to):
                pltpu.make_async_remote_copy(
                    src_ref=ins[a], dst_ref=land[a].at[_slot(x, y, c)], send_sem=send.at[3 * a + k], recv_sem=recv.at[3 * a + k],
                    device_id=dev, device_id_type=MESH).start()
        token[...] = jnp.zeros_like(token)

    out = pl.pallas_call(
        body, name=name,
        in_specs=[HBM] * (2 * n) + [ANY],
        out_specs=[SEM, SEM] + [HBM] * (2 * n) + [VMEM_SPEC],
        out_shape=[pltpu.SemaphoreType.DMA((3 * n,)), pltpu.SemaphoreType.DMA((3 * n,))] + _hbm_like(shards) + _hbm_like(bufs) + [TOKEN],
        input_output_aliases={i: 2 + i for i in range(2 * n)},
        compiler_params=pltpu.CompilerParams(has_side_effects=EFFECT),
    )(*[_hbm(s) for s in shards], *[_hbm(b) for b in bufs], after)
    return dict(send1=out[0], recv1=out[1], shards=list(out[2:2 + n]), bufs=list(out[2 + n:2 + 2 * n]), token=out[-1])


def _gather_mid(h, after, name):
    n = len(h["bufs"])

    def body(*refs):
        land, recv1 = refs[:n], refs[n]
        send2, recv2 = refs[n + 2], refs[n + 3]
        token = refs[-1]
        x, y, c = _place()
        nbr = [(1 - x, y, c), (x, 1 - y, c)]
        for j, dev in enumerate(nbr):
            for a in range(n):
                blk = land[a].at[_slot(*dev)]
                pltpu.make_async_remote_copy(
                    src_ref=blk, dst_ref=blk, send_sem=send2.at[4 * a + j], recv_sem=recv1.at[3 * a + 1 + j],
                    device_id=dev, device_id_type=MESH).wait_recv()
                pltpu.make_async_remote_copy(
                    src_ref=blk, dst_ref=blk, send_sem=send2.at[4 * a + j], recv_sem=recv2.at[4 * a + j],
                    device_id=(x, y, 1 - c), device_id_type=MESH).start()
        for a in range(n):
            from_x, from_y = land[a].at[_slot(*nbr[0])], land[a].at[_slot(*nbr[1])]
            for k, (half, dev) in enumerate([(_halves(from_y)[0], nbr[0]), (_halves(from_x)[1], nbr[1])]):
                pltpu.make_async_remote_copy(
                    src_ref=half, dst_ref=half, send_sem=send2.at[4 * a + 2 + k], recv_sem=recv2.at[4 * a + 2 + k],
                    device_id=dev, device_id_type=MESH).start()
        token[...] = jnp.zeros_like(token)

    out = pl.pallas_call(
        body, name=name,
        in_specs=[HBM] * n + [SEM, ANY],
        out_specs=[SEM, SEM] + [HBM] * n + [VMEM_SPEC],
        out_shape=[pltpu.SemaphoreType.DMA((4 * n,)), pltpu.SemaphoreType.DMA((4 * n,))] + _hbm_like(h["bufs"]) + [TOKEN],
        input_output_aliases={i: 2 + i for i in range(n)},
        compiler_params=pltpu.CompilerParams(has_side_effects=EFFECT),
    )(*h["bufs"], h["recv1"], after)
    h.update(send2=out[0], recv2=out[1], bufs=list(out[2:2 + n]), token=out[-1])
    return h


def _gather_mid2(h, after, name):
    n = len(h["bufs"])

    def body(*refs):
        land, recv2 = refs[:n], refs[n]
        send3, recv3 = refs[n + 2], refs[n + 3]
        token = refs[-1]
        x, y, c = _place()
        for a in range(n):
            blk = land[a].at[_slot(1 - x, 1 - y, c)]
            for k, half in enumerate(_halves(blk)):
                pltpu.make_async_remote_copy(
                    src_ref=half, dst_ref=half, send_sem=send3.at[a], recv_sem=recv2.at[4 * a + 2 + k],
                    device_id=(x, y, 1 - c), device_id_type=MESH).wait_recv()
            pltpu.make_async_remote_copy(
                src_ref=blk, dst_ref=blk, send_sem=send3.at[a], recv_sem=recv3.at[a],
                device_id=(x, y, 1 - c), device_id_type=MESH).start()
        token[...] = jnp.zeros_like(token)

    out = pl.pallas_call(
        body, name=name,
        in_specs=[HBM] * n + [SEM, ANY],
        out_specs=[SEM, SEM] + [HBM] * n + [VMEM_SPEC],
        out_shape=[pltpu.SemaphoreType.DMA((n,)), pltpu.SemaphoreType.DMA((n,))] + _hbm_like(h["bufs"]) + [TOKEN],
        input_output_aliases={i: 2 + i for i in range(n)},
        compiler_params=pltpu.CompilerParams(has_side_effects=EFFECT),
    )(*h["bufs"], h["recv2"], after)
    h.update(send3=out[0], recv3=out[1], bufs=list(out[2:2 + n]), token=out[-1])
    return h


def _gather_end(h, after, name):
    n = len(h["bufs"])

    def body(*refs):
        ins, land = refs[:n], refs[n:2 * n]
        send1, recv1, send2, recv2, send3, recv3 = refs[2 * n:2 * n + 6]
        x, y, c = _place()
        sib = (x, y, 1 - c)
        nbr = [(1 - x, y), (x, 1 - y)]

        def wait(src, dst, ssem, rsem, send):
            cp = pltpu.make_async_remote_copy(src_ref=src, dst_ref=dst, send_sem=ssem, recv_sem=rsem, device_id=sib, device_id_type=MESH)
            cp.wait_send() if send else cp.wait_recv()

        for a in range(n):
            mine = land[a].at[_slot(x, y, c)]
            for k in range(3):
                wait(ins[a], mine, send1.at[3 * a + k], recv1.at[3 * a + k], True)
            wait(ins[a], land[a].at[_slot(x, y, 1 - c)], send1.at[3 * a], recv1.at[3 * a], False)
            for j, (px, py) in enumerate(nbr):
                sent = land[a].at[_slot(px, py, c)]
                wait(sent, sent, send2.at[4 * a + j], recv2.at[4 * a + j], True)
                wait(sent, land[a].at[_slot(px, py, 1 - c)], send2.at[4 * a + j], recv2.at[4 * a + j], False)
            halves = [_halves(land[a].at[_slot(*nbr[1], c)])[0], _halves(land[a].at[_slot(*nbr[0], c)])[1]]
            for k, half in enumerate(halves):
                wait(half, half, send2.at[4 * a + 2 + k], recv2.at[4 * a + 2 + k], True)
            diag = land[a].at[_slot(1 - x, 1 - y, c)]
            wait(diag, diag, send3.at[a], recv3.at[a], True)
            wait(diag, land[a].at[_slot(1 - x, 1 - y, 1 - c)], send3.at[a], recv3.at[a], False)

    out = pl.pallas_call(
        body, name=name,
        in_specs=[HBM] * (2 * n) + [SEM] * 6 + [ANY],
        out_specs=[HBM] * n,
        out_shape=_hbm_like(h["bufs"]),
        input_output_aliases={n + i: i for i in range(n)},
        compiler_params=pltpu.CompilerParams(has_side_effects=EFFECT),
    )(*h["shards"], *h["bufs"], h["send1"], h["recv1"], h["send2"], h["recv2"], h["send3"], h["recv3"], after)
    return list(out)


def _peer_plan(kind, x, y, c):
    if kind == "pair":
        return [(2 * q + (1 - c), q, (x, y, 1 - c)) for q in range(4)]
    chips = [(1 - x, y), (x, 1 - y), (1 - x, 1 - y)]
    return [(2 * px + py, k, (px, py, c)) for k, (px, py) in enumerate(chips)]


def _exchange_start(kind, srcs, after, name):
    n = len(srcs)
    K = 4 if kind == "pair" else 3
    lands = [_hbm(lax.empty((K,) + s.shape[1:], s.dtype)) for s in srcs]

    def body(*refs):
        ins, land = refs[:n], refs[n:2 * n]
        send, recv = refs[2 * n + 1], refs[2 * n + 2]
        token = refs[-1]
        for a in range(n):
            for k, (si, di, dev) in enumerate(_peer_plan(kind, *_place())):
                pltpu.make_async_remote_copy(
                    src_ref=ins[a].at[si], dst_ref=land[a].at[di], send_sem=send.at[K * a + k], recv_sem=recv.at[K * a + k],
                    device_id=dev, device_id_type=MESH).start()
        token[...] = jnp.zeros_like(token)

    out = pl.pallas_call(
        body, name=name,
        in_specs=[HBM] * (2 * n) + [ANY],
        out_specs=[SEM, SEM] + [HBM] * (2 * n) + [VMEM_SPEC],
        out_shape=[pltpu.SemaphoreType.DMA((K * n,)), pltpu.SemaphoreType.DMA((K * n,))] + _hbm_like(srcs) + _hbm_like(lands) + [TOKEN],
        input_output_aliases={i: 2 + i for i in range(2 * n)},
        compiler_params=pltpu.CompilerParams(has_side_effects=EFFECT),
    )(*[_hbm(s) for s in srcs], *lands, after)
    return dict(kind=kind, send=out[0], recv=out[1], srcs=list(out[2:2 + n]), lands=list(out[2 + n:2 + 2 * n]), token=out[-1])


def _exchange_wait(h, after, name):
    n = len(h["srcs"])
    kind = h["kind"]
    K = 4 if kind == "pair" else 3

    def body(*refs):
        ins, land = refs[:n], refs[n:2 * n]
        send, recv = refs[2 * n], refs[2 * n + 1]
        for a in range(n):
            for k, (si, di, dev) in enumerate(_peer_plan(kind, *_place())):
                cp = pltpu.make_async_remote_copy(
                    src_ref=ins[a].at[si], dst_ref=land[a].at[di], send_sem=send.at[K * a + k], recv_sem=recv.at[K * a + k],
                    device_id=dev, device_id_type=MESH)
                cp.wait_send()
                cp.wait_recv()

    out = pl.pallas_call(
        body, name=name,
        in_specs=[HBM] * (2 * n) + [SEM, SEM, ANY],
        out_specs=[HBM] * (2 * n),
        out_shape=_hbm_like(h["srcs"]) + _hbm_like(h["lands"]),
        input_output_aliases={i: i for i in range(2 * n)},
        compiler_params=pltpu.CompilerParams(has_side_effects=EFFECT),
    )(*h["srcs"], *h["lands"], h["send"], h["recv"], after)
    return list(out[:n]), list(out[n:])


def _pair_add(gs, gots, c_idx, name):
    n = len(gs)
    _, R, C = gs[0].shape
    tr, tc = _tile2(R, C, 512, 16)

    def body(c_ref, *refs):
        for a in range(n):
            refs[2 * n + a][0] = (refs[a][0].astype(f32) + refs[n + a][0].astype(f32)).astype(bf16)

    mine = pl.BlockSpec((1, tr, tc), lambda q, i, j, c: (2 * q + c[0], i, j))
    blk = pl.BlockSpec((1, tr, tc), lambda q, i, j, c: (q, i, j))
    return pl.pallas_call(
        body, name=name,
        grid_spec=pltpu.PrefetchScalarGridSpec(
            num_scalar_prefetch=1, grid=(4, R // tr, C // tc),
            in_specs=[mine] * n + [blk] * n, out_specs=[blk] * n),
        out_shape=[jax.ShapeDtypeStruct((4, R, C), bf16)] * n,
        compiler_params=_cparams(("parallel", "parallel", "parallel")),
    )(c_idx, *gs, *gots)


def _small_exchange(send, gather, name, after=None):
    R = send.shape[-2]

    def body(in_ref, out_ref, send_sems, recv_sems):
        x, y, c = _place()
        me = 4 * x + 2 * y + c
        out_ref[me] = in_ref[...] if gather else in_ref[me]
        cps = []
        for k in range(1, N_DEV):
            px, py, pc = x ^ ((k >> 2) & 1), y ^ ((k >> 1) & 1), c ^ (k & 1)
            src = in_ref if gather else in_ref.at[4 * px + 2 * py + pc]
            cps.append(pltpu.make_async_remote_copy(
                src_ref=src, dst_ref=out_ref.at[me],
                send_sem=send_sems.at[k - 1], recv_sem=recv_sems.at[k - 1],
                device_id=(px, py, pc), device_id_type=MESH))
        for cp in cps:
            cp.start()
        for cp in cps:
            cp.wait()

    body, in_specs, args = _ordered(body, [pl.BlockSpec(memory_space=pltpu.VMEM)], [send], after)
    return pl.pallas_call(
        body, name=name,
        in_specs=in_specs, out_specs=pl.BlockSpec(memory_space=pltpu.VMEM),
        out_shape=jax.ShapeDtypeStruct((N_DEV, R, LANES), f32),
        scratch_shapes=[pltpu.SemaphoreType.DMA((N_DEV - 1,)), pltpu.SemaphoreType.DMA((N_DEV - 1,))],
    )(*args)


def _sum_blocks(blocks, name):
    def body(in_ref, o_ref):
        s = in_ref[0]
        for d in range(1, N_DEV):
            s = s + in_ref[d]
        o_ref[0] = s

    return pl.pallas_call(body, name=name, out_shape=jax.ShapeDtypeStruct((1,) + blocks.shape[1:], f32))(blocks)


def _rows(n):
    return -(-n // LANES)


def _pack(arrs, total_rows):
    parts = []
    for a in arrs:
        flat = a.reshape(-1).astype(f32)
        parts.append(jnp.pad(flat, (0, _rows(flat.size) * LANES - flat.size)))
    flat = jnp.concatenate(parts)
    return jnp.pad(flat, (0, total_rows * LANES - flat.size)).reshape(total_rows, LANES)


def _unpack(packed, shapes):
    lead = packed.shape[:-2]
    flat = packed.reshape(lead + (-1,))
    out, pos = [], 0
    for s in shapes:
        n = 1
        for d in s:
            n *= d
        out.append(flat[..., pos:pos + n].reshape(lead + tuple(s)))
        pos += _rows(n) * LANES
    return out


def _to_shards(full, axis):
    s = full.shape
    return jnp.moveaxis(full.reshape(s[:axis] + (N_DEV, s[axis] // N_DEV) + s[axis + 1:]), axis, 0)


def _from_shards(sh, axis):
    m = jnp.moveaxis(sh, 0, axis)
    s = m.shape
    return m.reshape(s[:axis] + (s[axis] * s[axis + 1],) + s[axis + 2:])


def kernel(x, meta, ffn_norm, ffn_w_gate, ffn_w_up, ffn_w_down, gla_norm, gla_w_in, gla_w_lr, gla_b_lr, gla_head_norm, gla_w_out, pool_norm, pool_w, pool_b, pool_scale, final_norm, loss_target, m_meta, m_ffn_norm, m_ffn_w_gate, m_ffn_w_up, m_ffn_w_down, m_gla_norm, m_gla_w_in, m_gla_w_lr, m_gla_b_lr, m_gla_head_norm, m_gla_w_out, m_pool_norm, m_pool_w, m_pool_b, m_pool_scale, m_final_norm, v_meta, v_ffn_norm, v_ffn_w_gate, v_ffn_w_up, v_ffn_w_down, v_gla_norm, v_gla_w_in, v_gla_w_lr, v_gla_b_lr, v_gla_head_norm, v_gla_w_out, v_pool_norm, v_pool_w, v_pool_b, v_pool_scale, v_final_norm):
    H = GLA_HEADS
    _, SEQ, D = x.shape
    Fs = ffn_w_gate.shape[-1]
    DK, DV = D // 2, D
    hv = DV // H
    GW = D // POOL_GROUPS
    INW = 2 * DK + 2 * DV + GATE_RANK
    NPK = 2 * DK + 2 * DV + GATE_PAD
    pad = (-N_META) % GLA_CHUNK
    first = pad + N_META
    Lp = first + SEQ
    n_units = ffn_w_gate.shape[0] * ffn_w_gate.shape[1]
    assert first % GLA_CHUNK == 0 and Lp % GLA_CHUNK == 0 and pad >= POOL_GROUPS * 4

    px, py, pc = _place()
    c_idx = jnp.reshape(pc, (1,)).astype(jnp.int32)
    q_idx = jnp.reshape(2 * px + py, (1,)).astype(jnp.int32)
    zero_idx = jnp.zeros((1,), jnp.int32)

    small_sh = [meta, ffn_norm, gla_w_lr, pool_norm, pool_b, pool_scale]
    small_axis = [1, 2, 2, 1, 2, 1]
    sh_shapes = [a.shape for a in small_sh]
    sh_rows = -(-sum(_rows(a.size) for a in small_sh) // 8) * 8
    gathered = _small_exchange(_pack(small_sh, sh_rows), True, "small_gather")
    meta_f, ffn_norm_f, wlr_f, pool_norm_f, pool_b_f, pool_scale_f = [
        _from_shards(a, ax) for a, ax in zip(_unpack(gathered, sh_shapes), small_axis)]
    ffn_norm_f = ffn_norm_f.reshape(n_units, 1, D)
    wlr128 = jnp.pad(wlr_f[0], ((0, GATE_PAD - GATE_RANK), (0, 0)))

    def t_units(w):
        return jnp.swapaxes(w, -1, -2).reshape(n_units, Fs, D)

    ffn_f32 = [t_units(ffn_w_gate), t_units(ffn_w_up), ffn_w_down.reshape(n_units, Fs, D)]
    mixer_f32 = [gla_w_in[0].T[None], gla_w_out, pool_w[0].reshape(1, -1, GW)]
    gather_order = [("ffn0", ffn_f32, 0), ("mixers", mixer_f32, 0)] + [(f"ffn{u}", ffn_f32, u) for u in range(1, n_units)]
    c_lr = 2 * DK + DV
    c_r = 2 * DK + 2 * DV
    gate_blk = c_r // GATE_PAD

    def cast_shards(i, after):
        tag, arrays, u = gather_order[i]
        shards = [_cast_unit(w, u, f"cast_{tag}_{a}", after) for a, w in enumerate(arrays)]
        if tag == "mixers":
            shards[2] = shards[2].reshape(pool_w.shape[1:])
        return shards

    def pass_on(i, h, after):
        tag = gather_order[i][0]
        nxt = later_shards[i + 1] if i + 1 < len(gather_order) else None
        h = _gather_mid(h, after, f"gather_mid_{tag}")
        if nxt is not None:
            nxt = _gather_start(nxt, h["token"], f"gather_start_{gather_order[i + 1][0]}")
        return h, nxt

    def complete(i, h, after):
        tag = gather_order[i][0]
        h = _gather_mid2(h, after, f"gather_mid2_{tag}")
        return _gather_end(h, h["token"], f"gather_end_{tag}")

    xs = jnp.concatenate([jnp.zeros((pad, D), f32), meta_f, x[0]], axis=0)
    saved = {}
    ffn_w = [None] * n_units

    def ffn_f(u, xs, after=None):
        out, h, G, U = _ffn_fwd(xs, ffn_norm_f[u], *ffn_w[u], name=f"ffn_fwd{u}", after=after)
        saved[("ffn", u)] = (xs, h, G, U)
        return out

    def gla_f(xs, win_p, wout_full, after=None):
        hn = _rms_fwd(xs, gla_norm, bf16, "gla_norm_fwd", after=after)
        proj = _mm(hn, win_p, "nt", f32, "gla_proj", tm=1056, tn=896, tk=2048)
        lg = _gate_fwd(proj, wlr128, gla_b_lr, pad, gate_blk, "gla_gate_fwd")
        o, y, states = _gla_fwd(proj, lg, gla_head_norm, H, "gla_core_fwd")
        out = _mm(y, wout_full, "nn", f32, "gla_out", tm=1056, tn=512, tk=2048, residual=xs)
        saved["gla"] = (xs, hn, proj, lg, o, y, states)
        return out

    def pool_f(xs, wpool_full):
        hn = _rms_fwd(xs, pool_norm_f, f32, "pool_norm_fwd")
        pooled = _pool_windows(hn, pad, Lp - pad, "pool_windows_fwd")
        out = _pool_mix_fwd(xs, pooled, wpool_full, pool_b_f.reshape(1, D), pool_scale_f, pad, "pool_mix_fwd")
        saved["pool"] = (xs, pooled)
        return out

    depth = ffn_w_gate.shape[0]
    assert depth == 2 and n_units == 4
    h0 = _gather_start(cast_shards(0, None), gathered, "gather_start_ffn0")
    later_shards = {}
    last = h0["token"]
    for i in range(1, len(gather_order)):
        later_shards[i] = cast_shards(i, last)
        last = later_shards[i][0]
    h0, h1 = pass_on(0, h0, last)
    ffn_w[0] = complete(0, h0, h1["token"])
    h1, h2 = pass_on(1, h1, ffn_w[0][0])
    xs = ffn_f(0, xs, after=h2["token"])
    win_g, wout_g, wpool_g = complete(1, h1, xs)
    h2, h3 = pass_on(2, h2, wout_g)
    win_full = win_g.reshape(INW, D)
    win_p = jnp.concatenate([win_full[:c_lr], win_full[c_lr + GATE_RANK:], win_full[c_lr:c_lr + GATE_RANK],
                             jnp.zeros((GATE_PAD - GATE_RANK, D), bf16)], axis=0)
    wout_full = wout_g.reshape(DV, D)
    wpool_full = _from_shards(wpool_g, 1)
    xs = gla_f(xs, win_p, wout_full, after=h3["token"])
    ffn_w[1] = complete(2, h2, xs)
    h3, h4 = pass_on(3, h3, ffn_w[1][0])
    xs = ffn_f(1, xs, after=h4["token"])
    ffn_w[2] = complete(3, h3, xs)
    h4, _ = pass_on(4, h4, ffn_w[2][0])
    xs = ffn_f(2, xs, after=h4["token"])
    xs = pool_f(xs, wpool_full)
    ffn_w[3] = complete(4, h4, xs)
    xs = ffn_f(3, xs)
    loss_part, dxs, d_final, dyh = _loss_head(xs, loss_target[0], final_norm.reshape(1, D), first, "loss_head")

    class Reduce:
        def __init__(self, tag, grads, after=None):
            self.tag = tag
            self.h = _exchange_start("pair", grads, loss_part if after is None else after, f"pair_start_{tag}")
            self.token = self.h["token"]

        def mid(self, after):
            grads, got = _exchange_wait(self.h, after, f"pair_wait_{self.tag}")
            if len({g.shape for g in grads}) == 1:
                self.sums = list(_pair_add(grads, got, c_idx, f"pair_add_{self.tag}"))
            else:
                self.sums = [_pair_add([g], [r], c_idx, f"pair_add_{self.tag}{a}")[0] for a, (g, r) in enumerate(zip(grads, got))]
            self.h = _exchange_start("chips", self.sums, loss_part, f"chips_start_{self.tag}")
            self.token = self.h["token"]

        def end(self, after):
            sums, recv = _exchange_wait(self.h, after, f"chips_wait_{self.tag}")
            return list(zip(sums, recv))

    d_ffn_norm = [None] * n_units
    small_grads = {}

    def ffn_b(u, dY, dyh, prev):
        xs_in, h_, G, U = saved[("ffn", u)]
        wg, wu, wd = ffn_w[u]
        tok = None if prev is None else prev.token
        dG, dU, A = _ffn_bwd_act(dyh, wd, G, U, f"ffn_act{u}", after=tok)
        dh = _ffn_bwd_dh(dG, dU, wg, wu, f"ffn_dh{u}")
        dxs, dg, dyh_next = _rms_bwd(dY, dh, xs_in, ffn_norm_f[u], pad, f"ffn_norm_bwd{u}")
        if prev is not None:
            prev.mid(dxs)
            tok = prev.token
        dwg = _ffn_bwd_wgrad(dG, h_, f"ffn_wgrad_gate{u}", after=tok)
        dwu = _ffn_bwd_wgrad(dU, h_, f"ffn_wgrad_up{u}", after=tok)
        dwd = _ffn_bwd_wgrad(A, dyh, f"ffn_wgrad_down{u}", after=tok)
        d_ffn_norm[u] = dg
        return dxs, dyh_next, Reduce(f"ffn{u}", [dwg, dwu, dwd])

    def gla_b(dY, prev):
        xs_in, hn, proj, lg, o, y, states = saved["gla"]
        dyb = dY.astype(bf16)
        dy = _mm(dyb, wout_full, "nt", f32, "gla_out_dgrad", tm=1056, tn=512, tk=2048, after=prev.token)
        dwout = _mm_tn_full(y, dyb, "gla_out_wgrad", 1024, after=prev.token)
        prev.mid(dwout)
        dq, dk, dv, dr, dlg, dhw = _gla_bwd(dy, proj, lg, o, states, gla_head_norm, H, pad, "gla_core_bwd", after=prev.token)
        dlr, dwlr, dblr = _gate_bwd(dlg, proj, wlr128, gla_b_lr, pad, gate_blk, "gla_gate_bwd")
        dproj = jnp.concatenate([dq, dk, dv, dr, dlr], axis=1)
        dwin_p = _mm_tn_full(dproj, hn, "gla_proj_wgrad", 896)
        dhn = _mm(dproj, win_p, "nn", f32, "gla_proj_dgrad", tm=1056, tn=1024, tk=896)
        dxs, dgn, dyh_next = _rms_bwd(dY, dhn, xs_in, gla_norm, pad, "gla_norm_bwd")
        dwin = jnp.concatenate([dwin_p[:c_lr], dwin_p[c_r:c_r + GATE_RANK], dwin_p[c_lr:c_r]], axis=0)
        small_grads.update(gla_w_lr=dwlr[:GATE_RANK][None], gla_b_lr=dblr, gla_head_norm=dhw, gla_norm=dgn)
        return dxs, dyh_next, Reduce("gla", [dwin.reshape(N_DEV, INW // N_DEV, D), dwout.reshape(N_DEV, DV // N_DEV, D)])

    def pool_b_(dY, prev):
        xs_in, pooled = saved["pool"]
        dp, dw, db, ds = _pool_mix_bwd(dY, pooled, wpool_full, pool_b_f.reshape(1, D), pool_scale_f, pad, "pool_mix_bwd",
                                       after=prev.token)
        dhn = _pool_windows_bwd(dp, pad, "pool_windows_bwd")
        dxs, dgn, dyh_next = _rms_bwd(dY, dhn, xs_in, pool_norm_f, pad, "pool_norm_bwd")
        prev.mid(dxs)
        dws = _to_shards(dw, 1)
        small_grads.update(pool_b=db.reshape(1, POOL_GROUPS, GW), pool_scale=ds, pool_norm=dgn)
        return dxs, dyh_next, Reduce("pool", [dws.reshape(N_DEV, POOL_GROUPS * GW // N_DEV, GW)], after=prev.token)

    sh_names = ["meta", "ffn_norm", "gla_w_lr", "pool_norm", "pool_b", "pool_scale"]
    rep_names = ["gla_norm", "gla_b_lr", "gla_head_norm", "final_norm"]
    rep_w = [gla_norm, gla_b_lr, gla_head_norm, final_norm]
    rep_shapes = [a.shape for a in rep_w]
    rep_rows = -(-sum(_rows(a.size) for a in rep_w) // 8) * 8

    def small_path(dxs0):
        small_grads.update(meta=dxs0[pad:first], ffn_norm=jnp.concatenate(d_ffn_norm, axis=0).reshape(n_units // 2, 2, D),
                           final_norm=d_final.reshape(D))
        by_owner = [_to_shards(small_grads[nm].reshape(full_shape), ax) for nm, full_shape, ax in zip(
            sh_names, [meta_f.shape, (ffn_norm.shape[0], 2, D), wlr_f.shape, pool_norm_f.shape, pool_b_f.shape, pool_scale_f.shape],
            small_axis)]
        rep_pack = _pack([small_grads[nm].reshape(s) for nm, s in zip(rep_names, rep_shapes)], rep_rows)
        send = jnp.stack([
            jnp.concatenate([_pack([g[d] for g in by_owner], sh_rows), rep_pack, loss_part], axis=0) for d in range(N_DEV)])
        total = _sum_blocks(_small_exchange(send, False, "small_reduce"), "small_sum")
        n_small = sh_rows + rep_rows

        def pack_small(sh_list, rep_list):
            return jnp.concatenate([_pack(sh_list, sh_rows), _pack(rep_list, rep_rows)], axis=0)[None]

        w_small = pack_small(small_sh, rep_w)
        m_small = pack_small([m_meta, m_ffn_norm, m_gla_w_lr, m_pool_norm, m_pool_b, m_pool_scale],
                             [m_gla_norm, m_gla_b_lr, m_gla_head_norm, m_final_norm])
        v_small = pack_small([v_meta, v_ffn_norm, v_gla_w_lr, v_pool_norm, v_pool_b, v_pool_scale],
                             [v_gla_norm, v_gla_b_lr, v_gla_head_norm, v_final_norm])
        small_out = _adamw(w_small, m_small, v_small, 0, total[:, :n_small], zero_idx, None, None, "adamw_small")
        small_res = {}
        for kind, packed in zip(("grad", "delta", "new_m", "new_v"), small_out):
            sh_vals = _unpack(packed[0, :sh_rows], sh_shapes)
            rep_vals = _unpack(packed[0, sh_rows:], rep_shapes)
            for nm, val in zip(sh_names + rep_names, sh_vals + rep_vals):
                small_res[(kind, nm)] = val
        return total[0, n_small, 0], small_res, small_out[0]

    def ffn_b_last(dY, dyh, prev):
        xs_in, h_, G, U = saved[("ffn", 0)]
        wg, wu, wd = ffn_w[0]
        dG, dU, A = _ffn_bwd_act(dyh, wd, G, U, "ffn_act0", after=prev.token)
        dwd = _ffn_bwd_wgrad(A, dyh, "ffn_wgrad_down0", after=prev.token)
        r_d = Reduce("ffn0_down", [dwd])
        dh = _ffn_bwd_dh(dG, dU, wg, wu, "ffn_dh0", after=r_d.token)
        dxs, dg, _ = _rms_bwd(dY, dh, xs_in, ffn_norm_f[0], pad, "ffn_norm_bwd0")
        d_ffn_norm[0] = dg
        small = small_path(dxs)
        prev.mid(small[2])
        r_d.mid(prev.token)
        dwg = _ffn_bwd_wgrad(dG, h_, "ffn_wgrad_gate0", after=r_d.token)
        r_g = Reduce("ffn0_gate", [dwg])
        dwu = _ffn_bwd_wgrad(dU, h_, "ffn_wgrad_up0", after=r_g.token)
        r_g.mid(dwu)
        r_u = Reduce("ffn0_up", [dwu], after=r_g.token)
        return dxs, small, (r_g, r_u, r_d)

    dxs, dyh, r3 = ffn_b(3, dxs, dyh, None)
    dxs, dyh, rp = pool_b_(dxs, r3)
    dxs, dyh, r2 = ffn_b(2, dxs, dyh, rp)
    dxs, dyh, r1 = ffn_b(1, dxs, dyh, r2)
    dxs, dyh, rg = gla_b(dxs, r1)
    dxs, (loss, small_res, _), r0 = ffn_b_last(dxs, dyh, rg)
    grad_x = dxs[first:].reshape(x.shape)
    r_last = r0[1]

    big_res = {}

    def adam_one(nm, w, m, v, entry, transposed=False):
        sums, recv = entry
        R, C = sums.shape[1:]
        w1, m1, v1 = ((t[0].T if transposed else t).reshape(1, R, C) for t in (w, m, v))
        out = _adamw(w1, m1, v1, 0, sums, q_idx, recv, None, f"adamw_{nm}", after=r_last.token)
        for kind, val in zip(("grad", "delta", "new_m", "new_v"), out):
            big_res[(kind, nm)] = val[0].T[None] if transposed else val.reshape(w.shape)
        return out[0]

    e_gla = rg.end(dxs)
    done = adam_one("gla_w_in", gla_w_in, m_gla_w_in, v_gla_w_in, e_gla[0], transposed=True)
    done = adam_one("gla_w_out", gla_w_out, m_gla_w_out, v_gla_w_out, e_gla[1])
    done = adam_one("pool_w", pool_w, m_pool_w, v_pool_w, rp.end(done)[0])
    r_last.mid(done)

    ffn_names = ["ffn_w_gate", "ffn_w_up", "ffn_w_down"]
    ffn_wmv = [tuple(t_units(t) for t in (ffn_w_gate, m_ffn_w_gate, v_ffn_w_gate)),
               tuple(t_units(t) for t in (ffn_w_up, m_ffn_w_up, v_ffn_w_up)),
               tuple(t.reshape(n_units, Fs, D) for t in (ffn_w_down, m_ffn_w_down, v_ffn_w_down))]
    ffn_prev = [[lax.empty((n_units, Fs, D), f32) for _ in range(4)] for _ in range(3)]
    order_after = r_last.token
    for u, red in ((3, r3), (2, r2), (1, r1), (0, r0)):
        entries = [r.end(done)[0] for r in red] if u == 0 else red.end(done)
        for a in range(3):
            sums, recv = entries[a]
            ffn_prev[a] = _adamw(*ffn_wmv[a], u, sums, q_idx, recv, ffn_prev[a], f"adamw_{ffn_names[a]}{u}", after=order_after)
            done = order_after = ffn_prev[a][0]
    for a in range(3):
        for kind, val in zip(("grad", "delta", "new_m", "new_v"), ffn_prev[a]):
            val = val.reshape(ffn_w_down.shape)
            big_res[(kind, ffn_names[a])] = val if a == 2 else jnp.swapaxes(val, -1, -2)

    order = ["meta", "ffn_norm", "ffn_w_gate", "ffn_w_up", "ffn_w_down", "gla_norm", "gla_w_in", "gla_w_lr", "gla_b_lr",
             "gla_head_norm", "gla_w_out", "pool_norm", "pool_w", "pool_b", "pool_scale", "final_norm"]
    res = {**small_res, **big_res}
    outs = [loss, grad_x]
    for kind in ("grad", "delta", "new_m", "new_v"):
        outs += [res[(kind, nm)] for nm in order]
    return tuple(outs)
```

```python
import jax
import jax.numpy as jnp
from jax import lax
from jax.experimental import pallas as pl
from jax.experimental.pallas import tpu as pltpu

f32 = jnp.float32
bf16 = jnp.bfloat16

N_DEV = 8
N_META = 16
GLA_HEADS = 4
GLA_CHUNK = 64
GLA_SUB = 16
GLA_HEADS_PER_STEP = 2
GATE_RANK = 16
GATE_PAD = 128
GATE_NORM = 16.0
EPS = 1e-6
POOL_GROUPS = 4
ADAM_LR = 0.001
ADAM_B1 = 0.9
ADAM_B2 = 0.999
ADAM_EPS = 1e-08
ADAM_WD = 0.01
ADAM_STEP = 10
LANES = 128
VMEM_LIMIT_MB = 56

NN = (((1,), (0,)), ((), ()))
NT = (((1,), (1,)), ((), ()))
TN = (((0,), (0,)), ((), ()))
HI = lax.Precision.HIGHEST
MESH = pl.DeviceIdType.MESH
ANY = pl.BlockSpec(memory_space=pl.ANY)


def _cparams(sem=None, vmem_mb=None):
    kw = {}
    if sem is not None:
        kw["dimension_semantics"] = sem
    if vmem_mb is not None:
        kw["vmem_limit_bytes"] = vmem_mb * 2 ** 20
    return pltpu.CompilerParams(**kw)


def _tile(n, target, mult=16):
    best = None
    for t in range(mult, min(n, target) + 1, mult):
        if n % t == 0:
            best = t
    assert best is not None, (n, target, mult)
    return best


def _tile2(R, C, rows, mult):
    if R % mult == 0:
        return _tile(R, rows, mult), C
    return R, _tile(C, 256, LANES)


def _dot(a, b, dims=NN, precision=None):
    return lax.dot_general(a, b, dims, preferred_element_type=f32, precision=precision)


def _sigmoid(x):
    return 1.0 / (1.0 + jnp.exp(-x))


def _row_ids(tile_index, tm):
    return tile_index * tm + lax.broadcasted_iota(jnp.int32, (tm, 1), 0)


def _ordered(body, in_specs, args, after, lead=0):
    if after is None:
        return body, in_specs, args
    pos = lead + len(args)

    def body_without(*refs):
        return body(*refs[:pos], *refs[pos + 1:])

    return body_without, list(in_specs) + [ANY], list(args) + [after]


def _cast_unit(w, unit, name, after=None):
    _, R, C = w.shape
    tr, tc = _tile2(R, C, 256, 16)

    def body(w_ref, o_ref):
        o_ref[...] = w_ref[0].astype(bf16)

    body, in_specs, args = _ordered(body, [pl.BlockSpec((1, tr, tc), lambda i, j: (unit, i, j))], [w], after)
    return pl.pallas_call(
        body, name=name, grid=(R // tr, C // tc),
        in_specs=in_specs, out_specs=pl.BlockSpec((tr, tc), lambda i, j: (i, j)),
        out_shape=jax.ShapeDtypeStruct((R, C), bf16),
        compiler_params=_cparams(("parallel", "parallel")),
    )(*args)


def _rms_fwd(xs, g, out_dtype, name, after=None):
    Lp, D = xs.shape
    tm = _tile(Lp, 528)

    def body(x_ref, g_ref, h_ref):
        x = x_ref[...]
        rstd = lax.rsqrt(jnp.mean(x * x, axis=-1, keepdims=True) + EPS)
        h_ref[...] = (x * rstd * g_ref[...]).astype(out_dtype)

    in_specs = [pl.BlockSpec((tm, D), lambda i: (i, 0)), pl.BlockSpec((1, D), lambda i: (0, 0))]
    body, in_specs, args = _ordered(body, in_specs, [xs, g], after)
    return pl.pallas_call(
        body, name=name, grid=(Lp // tm,),
        in_specs=in_specs,
        out_specs=pl.BlockSpec((tm, D), lambda i: (i, 0)),
        out_shape=jax.ShapeDtypeStruct((Lp, D), out_dtype),
        compiler_params=_cparams(("parallel",)),
    )(*args)


def _rms_bwd(dY, dh, xs, g, pad, name):
    Lp, D = xs.shape
    tm = _tile(Lp, 352)

    def body(dY_ref, dh_ref, x_ref, g_ref, dxs_ref, dg_ref, half_ref):
        i = pl.program_id(0)

        @pl.when(i == 0)
        def _():
            dg_ref[...] = jnp.zeros_like(dg_ref)

        x = x_ref[...]
        rstd = lax.rsqrt(jnp.mean(x * x, axis=-1, keepdims=True) + EPS)
        xhat = x * rstd
        dh_ = dh_ref[...]
        dg_ref[...] += jnp.sum(dh_ * xhat, axis=0, keepdims=True)
        dxh = dh_ * g_ref[...]
        dx = rstd * (dxh - xhat * jnp.mean(dxh * xhat, axis=-1, keepdims=True))
        out = jnp.where(_row_ids(i, tm) >= pad, dY_ref[...] + dx, 0.0)
        dxs_ref[...] = out
        half_ref[...] = (0.5 * out).astype(bf16)

    row = pl.BlockSpec((tm, D), lambda i: (i, 0))
    vec = pl.BlockSpec((1, D), lambda i: (0, 0))
    return pl.pallas_call(
        body, name=name, grid=(Lp // tm,),
        in_specs=[row, row, row, vec], out_specs=[row, vec, row],
        out_shape=[jax.ShapeDtypeStruct((Lp, D), f32), jax.ShapeDtypeStruct((1, D), f32), jax.ShapeDtypeStruct((Lp, D), bf16)],
        compiler_params=_cparams(("arbitrary",)),
    )(dY, dh, xs, g)


def _mm(a, b, mode, out_dtype, name, tm=512, tn=512, tk=512, residual=None, after=None):
    if mode == "nn":
        (M, K), N = a.shape, b.shape[1]
    elif mode == "nt":
        (M, K), N = a.shape, b.shape[0]
    else:
        (K, M), N = a.shape, b.shape[1]
    tm = _tile(M, tm, 16 if mode != "tn" else LANES) if M > tm else M
    tn = _tile(N, tn, LANES) if N > tn else N
    tk = _tile(K, tk, LANES if mode != "tn" else 16) if K > tk else K
    nk = K // tk
    dims = {"nn": NN, "nt": NT, "tn": TN}[mode]

    def body(*refs):
        if residual is None:
            a_ref, b_ref, o_ref, acc = refs
            r_ref = None
        else:
            a_ref, b_ref, r_ref, o_ref, acc = refs
        k = pl.program_id(2)

        @pl.when(k == 0)
        def _():
            acc[...] = jnp.zeros_like(acc)

        acc[...] += _dot(a_ref[...], b_ref[...], dims)

        @pl.when(k == nk - 1)
        def _():
            r = acc[...]
            if r_ref is not None:
                r = r + r_ref[...]
            o_ref[...] = r.astype(out_dtype)

    a_spec = pl.BlockSpec((tk, tm), lambda i, j, k: (k, i)) if mode == "tn" else pl.BlockSpec((tm, tk), lambda i, j, k: (i, k))
    b_spec = pl.BlockSpec((tn, tk), lambda i, j, k: (j, k)) if mode == "nt" else pl.BlockSpec((tk, tn), lambda i, j, k: (k, j))
    o_spec = pl.BlockSpec((tm, tn), lambda i, j, k: (i, j))
    in_specs = [a_spec, b_spec] + ([o_spec] if residual is not None else [])
    args = [a, b] + ([residual] if residual is not None else [])
    body, in_specs, args = _ordered(body, in_specs, args, after)
    return pl.pallas_call(
        body, name=name, grid=(M // tm, N // tn, nk),
        in_specs=in_specs, out_specs=o_spec,
        out_shape=jax.ShapeDtypeStruct((M, N), out_dtype),
        scratch_shapes=[pltpu.VMEM((tm, tn), f32)],
        compiler_params=_cparams(("parallel", "parallel", "arbitrary"), VMEM_LIMIT_MB),
    )(*args)


def _mm_tn_full(a, b, name, tm, after=None):
    K, M = a.shape
    N = b.shape[1]
    tm = _tile(M, tm, LANES)

    def body(a_ref, b_ref, o_ref):
        o_ref[...] = _dot(a_ref[...], b_ref[...], TN).astype(bf16)

    in_specs = [pl.BlockSpec((K, tm), lambda i: (0, i)), pl.BlockSpec((K, N), lambda i: (0, 0), pipeline_mode=pl.Buffered(1))]
    body, in_specs, args = _ordered(body, in_specs, [a, b], after)
    return pl.pallas_call(
        body, name=name, grid=(M // tm,),
        in_specs=in_specs, out_specs=pl.BlockSpec((tm, N), lambda i: (i, 0)),
        out_shape=jax.ShapeDtypeStruct((M, N), bf16),
        compiler_params=_cparams(("parallel",), VMEM_LIMIT_MB),
    )(*args)


def _ffn_fwd(xs, g, wg, wu, wd, name, after=None):
    Lp, D = xs.shape
    nd, Fs, _ = wg.shape
    tm = _tile(Lp, 704)
    once = pl.Buffered(1)

    def body(x_ref, g_ref, wg_ref, wu_ref, wd_ref, out_ref, h_ref, G_ref, U_ref, hs, acc):
        j = pl.program_id(1)

        @pl.when(j == 0)
        def _():
            x = x_ref[...]
            rstd = lax.rsqrt(jnp.mean(x * x, axis=-1, keepdims=True) + EPS)
            h = (x * rstd * g_ref[...]).astype(bf16)
            hs[...] = h
            h_ref[...] = h
            acc[...] = jnp.zeros_like(acc)

        h = hs[...]
        G = _dot(h, wg_ref[0], NT)
        U = _dot(h, wu_ref[0], NT)
        G_ref[0] = G.astype(bf16)
        U_ref[0] = U.astype(bf16)
        A = (G * _sigmoid(G) * U).astype(bf16)
        acc[...] += _dot(A, wd_ref[0])

        @pl.when(j == nd - 1)
        def _():
            out_ref[...] = x_ref[...] + 0.5 * acc[...]

    row_f = pl.BlockSpec((tm, D), lambda i, j: (i, 0), pipeline_mode=once)
    act = pl.BlockSpec((1, tm, Fs), lambda i, j: (j, i, 0))
    wrow = pl.BlockSpec((1, Fs, D), lambda i, j: (j, 0, 0))
    in_specs = [row_f, pl.BlockSpec((1, D), lambda i, j: (0, 0)), wrow, wrow, wrow]
    body, in_specs, args = _ordered(body, in_specs, [xs, g, wg, wu, wd], after)
    return pl.pallas_call(
        body, name=name, grid=(Lp // tm, nd),
        in_specs=in_specs,
        out_specs=[row_f, pl.BlockSpec((tm, D), lambda i, j: (i, 0), pipeline_mode=once), act, act],
        out_shape=[jax.ShapeDtypeStruct((Lp, D), f32), jax.ShapeDtypeStruct((Lp, D), bf16),
                   jax.ShapeDtypeStruct((nd, Lp, Fs), bf16), jax.ShapeDtypeStruct((nd, Lp, Fs), bf16)],
        scratch_shapes=[pltpu.VMEM((tm, D), bf16), pltpu.VMEM((tm, D), f32)],
        compiler_params=_cparams(("parallel", "arbitrary"), VMEM_LIMIT_MB),
    )(*args)


def _ffn_bwd_act(dyh, wd, G, U, name, after=None):
    Lp, D = dyh.shape
    nd, Fs, _ = wd.shape
    tm = _tile(Lp, 704)

    def body(dyh_ref, wd_ref, G_ref, U_ref, dG_ref, dU_ref, A_ref):
        dA = _dot(dyh_ref[...], wd_ref[0], NT)
        Gf = G_ref[0].astype(f32)
        Uf = U_ref[0].astype(f32)
        s = _sigmoid(Gf)
        silu = Gf * s
        dG_ref[0] = (dA * Uf * (s * (1.0 + Gf * (1.0 - s)))).astype(bf16)
        dU_ref[0] = (dA * silu).astype(bf16)
        A_ref[0] = (silu * Uf).astype(bf16)

    act = pl.BlockSpec((1, tm, Fs), lambda j, i: (j, i, 0))
    act_s = jax.ShapeDtypeStruct((nd, Lp, Fs), bf16)
    in_specs = [pl.BlockSpec((tm, D), lambda j, i: (i, 0)), pl.BlockSpec((1, Fs, D), lambda j, i: (j, 0, 0)), act, act]
    body, in_specs, args = _ordered(body, in_specs, [dyh, wd, G, U], after)
    return pl.pallas_call(
        body, name=name, grid=(nd, Lp // tm),
        in_specs=in_specs, out_specs=[act, act, act], out_shape=[act_s, act_s, act_s],
        compiler_params=_cparams(("parallel", "parallel"), VMEM_LIMIT_MB),
    )(*args)


def _ffn_bwd_dh(dG, dU, wg, wu, name, after=None):
    nd, Lp, Fs = dG.shape
    D = wg.shape[2]
    tm = _tile(Lp, 1056)

    def body(dG_ref, dU_ref, wg_ref, wu_ref, dh_ref, acc):
        j = pl.program_id(1)

        @pl.when(j == 0)
        def _():
            acc[...] = jnp.zeros_like(acc)

        acc[...] += _dot(dG_ref[0], wg_ref[0]) + _dot(dU_ref[0], wu_ref[0])

        @pl.when(j == nd - 1)
        def _():
            dh_ref[...] = acc[...]

    act = pl.BlockSpec((1, tm, Fs), lambda i, j: (j, i, 0))
    wrow = pl.BlockSpec((1, Fs, D), lambda i, j: (j, 0, 0))
    body, in_specs, args = _ordered(body, [act, act, wrow, wrow], [dG, dU, wg, wu], after)
    return pl.pallas_call(
        body, name=name, grid=(Lp // tm, nd),
        in_specs=in_specs,
        out_specs=pl.BlockSpec((tm, D), lambda i, j: (i, 0), pipeline_mode=pl.Buffered(1)),
        out_shape=jax.ShapeDtypeStruct((Lp, D), f32),
        scratch_shapes=[pltpu.VMEM((tm, D), f32)],
        compiler_params=_cparams(("parallel", "arbitrary"), VMEM_LIMIT_MB),
    )(*args)


def _ffn_bwd_wgrad(act, rows, name, after=None):
    nd, Lp, Fs = act.shape
    D = rows.shape[1]

    def body(a_ref, r_ref, o_ref):
        o_ref[0] = _dot(a_ref[0], r_ref[...], TN).astype(bf16)

    in_specs = [pl.BlockSpec((1, Lp, Fs), lambda j: (j, 0, 0)),
                pl.BlockSpec((Lp, D), lambda j: (0, 0), pipeline_mode=pl.Buffered(1))]
    body, in_specs, args = _ordered(body, in_specs, [act, rows], after)
    return pl.pallas_call(
        body, name=name, grid=(nd,),
        in_specs=in_specs, out_specs=pl.BlockSpec((1, Fs, D), lambda j: (j, 0, 0)),
        out_shape=jax.ShapeDtypeStruct((nd, Fs, D), bf16),
        compiler_params=_cparams(("parallel",), VMEM_LIMIT_MB),
    )(*args)


def _gate_fwd(proj, wlr, blr, pad, gate_blk, name):
    Lp = proj.shape[0]
    DK = wlr.shape[1]
    tm = _tile(Lp, 528)

    def body(lr_ref, w_ref, b_ref, lg_ref):
        z = _dot(lr_ref[...].astype(bf16), w_ref[...].astype(bf16)) + b_ref[...]
        ls = jnp.minimum(z, 0.0) - jnp.log(1.0 + jnp.exp(-jnp.abs(z)))
        lg_ref[...] = jnp.where(_row_ids(pl.program_id(0), tm) >= pad, ls * (1.0 / GATE_NORM), 0.0)

    return pl.pallas_call(
        body, name=name, grid=(Lp // tm,),
        in_specs=[pl.BlockSpec((tm, GATE_PAD), lambda i: (i, gate_blk)),
                  pl.BlockSpec((GATE_PAD, DK), lambda i: (0, 0)), pl.BlockSpec((1, DK), lambda i: (0, 0))],
        out_specs=pl.BlockSpec((tm, DK), lambda i: (i, 0)),
        out_shape=jax.ShapeDtypeStruct((Lp, DK), f32),
        compiler_params=_cparams(("parallel",)),
    )(proj, wlr, blr)


def _gate_bwd(dlg, proj, wlr, blr, pad, gate_blk, name):
    Lp = proj.shape[0]
    DK = wlr.shape[1]
    tm = _tile(Lp, 528)

    def body(dlg_ref, lr_ref, w_ref, b_ref, dlr_ref, dw_ref, db_ref):
        i = pl.program_id(0)

        @pl.when(i == 0)
        def _():
            dw_ref[...] = jnp.zeros_like(dw_ref)
            db_ref[...] = jnp.zeros_like(db_ref)

        lr = lr_ref[...].astype(bf16)
        w = w_ref[...].astype(bf16)
        z = _dot(lr, w) + b_ref[...]
        dz = jnp.where(_row_ids(i, tm) >= pad, dlg_ref[...] * _sigmoid(-z) * (1.0 / GATE_NORM), 0.0)
        dzb = dz.astype(bf16)
        dlr_ref[...] = _dot(dzb, w, NT).astype(bf16)
        dw_ref[...] += _dot(lr, dzb, TN)
        db_ref[...] += jnp.sum(dz, axis=0, keepdims=True)

    return pl.pallas_call(
        body, name=name, grid=(Lp // tm,),
        in_specs=[pl.BlockSpec((tm, DK), lambda i: (i, 0)), pl.BlockSpec((tm, GATE_PAD), lambda i: (i, gate_blk)),
                  pl.BlockSpec((GATE_PAD, DK), lambda i: (0, 0)), pl.BlockSpec((1, DK), lambda i: (0, 0))],
        out_specs=[pl.BlockSpec((tm, GATE_PAD), lambda i: (i, 0)), pl.BlockSpec((GATE_PAD, DK), lambda i: (0, 0)),
                   pl.BlockSpec((1, DK), lambda i: (0, 0))],
        out_shape=[jax.ShapeDtypeStruct((Lp, GATE_PAD), bf16), jax.ShapeDtypeStruct((GATE_PAD, DK), f32),
                   jax.ShapeDtypeStruct((1, DK), f32)],
        compiler_params=_cparams(("arbitrary",)),
    )(dlg, proj, wlr, blr)


def _chunk_decay(lg):
    C = lg.shape[0]
    r = lax.broadcasted_iota(jnp.int32, (C, C), 0)
    c = lax.broadcasted_iota(jnp.int32, (C, C), 1)
    return _dot(jnp.where(r >= c, 1.0, 0.0).astype(f32), lg, NN, HI)


def _col(v):
    return jnp.transpose(jnp.broadcast_to(v, (8, v.shape[1])))[:, 0:1]


def _intra_scores(q, k, b, A_ref):
    C = q.shape[0]
    S = GLA_SUB
    A_ref[...] = jnp.zeros_like(A_ref)
    ri = lax.broadcasted_iota(jnp.int32, (S, 1), 0)
    for I in range(C // S):
        lo = S * I
        qI, bI = q[lo:lo + S], b[lo:lo + S]
        if I > 0:
            bref = b[lo - 1:lo]
            qs = qI * jnp.exp(bI - bref)
            ks = k[:lo] * jnp.exp(bref - b[:lo])
            A_ref[lo:lo + S, 0:lo] = _dot(qs, ks, NT, HI)
        for jj in range(S):
            j = lo + jj
            P = jnp.exp(jnp.minimum(bI - b[j:j + 1], 0.0))
            a = jnp.sum(qI * P * k[j:j + 1], axis=1, keepdims=True)
            A_ref[lo:lo + S, j:j + 1] = jnp.where(ri >= jj, a, 0.0)


def _intra_grads(q, k, b, dA, dq_ref, dk_ref):
    C = q.shape[0]
    S = GLA_SUB
    ri = lax.broadcasted_iota(jnp.int32, (S, 1), 0)
    for I in range(C // S):
        lo = S * I
        qI, bI = q[lo:lo + S], b[lo:lo + S]
        dqI = jnp.zeros_like(qI)
        if I > 0:
            bref = b[lo - 1:lo]
            eq = jnp.exp(bI - bref)
            ek = jnp.exp(bref - b[:lo])
            qs = qI * eq
            ks = k[:lo] * ek
            dAI = dA[lo:lo + S, 0:lo]
            dqI = dqI + _dot(dAI, ks, NN, HI) * eq
            dk_ref[0:lo, :] += _dot(dAI, qs, TN, HI) * ek
        for jj in range(S):
            j = lo + jj
            P = jnp.exp(jnp.minimum(bI - b[j:j + 1], 0.0))
            t = jnp.where(ri >= jj, dA[lo:lo + S, j:j + 1], 0.0) * P
            dqI = dqI + t * k[j:j + 1]
            dk_ref[j:j + 1, :] += jnp.sum(t * qI, axis=0, keepdims=True)
        dq_ref[lo:lo + S, :] += dqI


def _gla_fwd(proj, lg, hnw, H, name):
    Lp = proj.shape[0]
    DK = lg.shape[1]
    hk = DK // H
    hv = hnw.shape[1]
    DV = hv * H
    C = GLA_CHUNK
    NC = Lp // C
    HS = GLA_HEADS_PER_STEP
    G = H // HS
    scale = float(hk) ** -0.5
    kq, kv, kr = G, (2 * DK) // (HS * hv), (2 * DK) // (HS * hv) + G

    def body(q_ref, k_ref, v_ref, r_ref, lg_ref, w_ref, o_ref, y_ref, s_ref, S_scr, A_scr):
        c = pl.program_id(1)

        @pl.when(c == 0)
        def _():
            S_scr[...] = jnp.zeros_like(S_scr)

        for hh in range(HS):
            ck, cv = slice(hh * hk, (hh + 1) * hk), slice(hh * hv, (hh + 1) * hv)
            q = q_ref[:, ck] * scale
            k = k_ref[:, ck]
            v = v_ref[:, cv]
            b = _chunk_decay(lg_ref[:, ck])
            bl = b[C - 1:C]
            S = S_scr[hh]
            s_ref[hh, 0] = S
            _intra_scores(q, k, b, A_scr.at[hh])
            vb = v.astype(bf16)
            o = _dot((q * jnp.exp(b)).astype(bf16), S.astype(bf16)) + _dot(A_scr[hh].astype(bf16), vb)
            kb = (k * jnp.exp(bl - b)).astype(bf16)
            S_scr[hh] = jnp.exp(_col(bl)) * S + _dot(kb, vb, TN)
            o_ref[:, cv] = o
            on = o * lax.rsqrt(jnp.mean(o * o, axis=-1, keepdims=True) + EPS) * w_ref[...]
            r = r_ref[:, cv]
            y_ref[:, cv] = (on * (r * _sigmoid(r))).astype(bf16)

    return pl.pallas_call(
        body, name=name, grid=(G, NC),
        in_specs=[pl.BlockSpec((C, HS * hk), lambda g, c: (c, g)),
                  pl.BlockSpec((C, HS * hk), lambda g, c: (c, kq + g)),
                  pl.BlockSpec((C, HS * hv), lambda g, c: (c, kv + g)),
                  pl.BlockSpec((C, HS * hv), lambda g, c: (c, kr + g)),
                  pl.BlockSpec((C, HS * hk), lambda g, c: (c, g)),
                  pl.BlockSpec((1, hv), lambda g, c: (0, 0))],
        out_specs=[pl.BlockSpec((C, HS * hv), lambda g, c: (c, g)), pl.BlockSpec((C, HS * hv), lambda g, c: (c, g)),
                   pl.BlockSpec((HS, 1, hk, hv), lambda g, c: (g, c, 0, 0))],
        out_shape=[jax.ShapeDtypeStruct((Lp, DV), f32), jax.ShapeDtypeStruct((Lp, DV), bf16),
                   jax.ShapeDtypeStruct((H, NC, hk, hv), f32)],
        scratch_shapes=[pltpu.VMEM((HS, hk, hv), f32), pltpu.VMEM((HS, C, C), f32)],
        compiler_params=_cparams(("parallel", "arbitrary")),
    )(proj, proj, proj, proj, lg, hnw)


def _gla_bwd(dy, proj, lg, o, states, hnw, H, pad, name, after=None):
    Lp = proj.shape[0]
    DK = lg.shape[1]
    hk = DK // H
    hv = hnw.shape[1]
    DV = hv * H
    C = GLA_CHUNK
    NC = Lp // C
    HS = GLA_HEADS_PER_STEP
    G = H // HS
    scale = float(hk) ** -0.5
    kq, kv, kr = G, (2 * DK) // (HS * hv), (2 * DK) // (HS * hv) + G

    def body(dy_ref, q_ref, k_ref, v_ref, r_ref, lg_ref, o_ref, s_ref, sn_ref, w_ref,
             dq_ref, dk_ref, dv_ref, dr_ref, dlg_ref, dw_ref, dS_scr, A_scr, dq_s, dk_s):
        g = pl.program_id(0)
        cc = pl.program_id(1)
        c = NC - 1 - cc

        @pl.when(cc == 0)
        def _():
            dS_scr[...] = jnp.zeros_like(dS_scr)

        @pl.when((cc == 0) & (g == 0))
        def _():
            dw_ref[...] = jnp.zeros_like(dw_ref)

        keep = (c * C + lax.broadcasted_iota(jnp.int32, (C, 1), 0)) >= pad
        ri = lax.broadcasted_iota(jnp.int32, (C, C), 0)
        ci = lax.broadcasted_iota(jnp.int32, (C, C), 1)
        w = w_ref[...]
        for hh in range(HS):
            ck, cv = slice(hh * hk, (hh + 1) * hk), slice(hh * hv, (hh + 1) * hv)
            o_ = o_ref[:, cv]
            rs = lax.rsqrt(jnp.mean(o_ * o_, axis=-1, keepdims=True) + EPS)
            ohat = o_ * rs
            r = r_ref[:, cv]
            sg = _sigmoid(r)
            dy_ = dy_ref[:, cv]
            d_on = dy_ * (r * sg)
            dr_ref[:, cv] = jnp.where(keep, dy_ * (ohat * w) * (sg * (1.0 + r * (1.0 - sg))), 0.0).astype(bf16)
            dw_ref[...] += jnp.sum(d_on * ohat, axis=0, keepdims=True)
            d_oh = d_on * w
            do = rs * (d_oh - ohat * jnp.mean(d_oh * ohat, axis=-1, keepdims=True))
            dob = do.astype(bf16)
            q = q_ref[:, ck] * scale
            k = k_ref[:, ck]
            vb = v_ref[:, cv].astype(bf16)
            b = _chunk_decay(lg_ref[:, ck])
            bl = b[C - 1:C]
            eb = jnp.exp(b)
            ekb = jnp.exp(bl - b)
            S = s_ref[hh, 0]
            dS = dS_scr[hh]
            dSb = dS.astype(bf16)
            _intra_scores(q, k, b, A_scr.at[hh])
            dA = jnp.where(ri >= ci, _dot(dob, vb, NT), 0.0)
            kb = (k * ekb).astype(bf16)
            qb = (q * eb).astype(bf16)
            dv = _dot(A_scr[hh].astype(bf16), dob, TN) + _dot(kb, dSb)
            dq_s[hh] = _dot(dob, S.astype(bf16), NT) * eb
            dk_s[hh] = _dot(vb, dSb, NT) * ekb
            dS_scr[hh] = _dot(qb, dob, TN) + jnp.exp(_col(bl)) * dS
            _intra_grads(q, k, b, dA, dq_s.at[hh], dk_s.at[hh])
            dq = dq_s[hh]
            dk = dk_s[hh]
            Dm = q * dq - k * dk
            after_rows = _dot(jnp.ones((8, hv), f32), sn_ref[hh, 0] * dS, NT, HI)[0:1]
            dlg = _dot(jnp.where(ri <= ci, 1.0, 0.0).astype(f32), Dm, NN, HI) + after_rows
            dlg_ref[:, ck] = jnp.where(keep, dlg, 0.0)
            dq_ref[:, ck] = jnp.where(keep, dq * scale, 0.0).astype(bf16)
            dk_ref[:, ck] = jnp.where(keep, dk, 0.0).astype(bf16)
            dv_ref[:, cv] = jnp.where(keep, dv, 0.0).astype(bf16)

    rev = lambda cc: NC - 1 - cc
    bk = lambda off: pl.BlockSpec((C, HS * hk), lambda g, cc: (rev(cc), off + g))
    bv = lambda off: pl.BlockSpec((C, HS * hv), lambda g, cc: (rev(cc), off + g))
    in_specs = [bv(0), bk(0), bk(kq), bv(kv), bv(kr), bk(0), bv(0),
                pl.BlockSpec((HS, 1, hk, hv), lambda g, cc: (g, rev(cc), 0, 0)),
                pl.BlockSpec((HS, 1, hk, hv), lambda g, cc: (g, jnp.minimum(rev(cc) + 1, NC - 1), 0, 0)),
                pl.BlockSpec((1, hv), lambda g, cc: (0, 0))]
    body, in_specs, args = _ordered(body, in_specs, [dy, proj, proj, proj, proj, lg, o, states, states, hnw], after)
    return pl.pallas_call(
        body, name=name, grid=(G, NC),
        in_specs=in_specs,
        out_specs=[bk(0), bk(0), bv(0), bv(0), bk(0), pl.BlockSpec((1, hv), lambda g, cc: (0, 0))],
        out_shape=[jax.ShapeDtypeStruct((Lp, DK), bf16), jax.ShapeDtypeStruct((Lp, DK), bf16),
                   jax.ShapeDtypeStruct((Lp, DV), bf16), jax.ShapeDtypeStruct((Lp, DV), bf16),
                   jax.ShapeDtypeStruct((Lp, DK), f32), jax.ShapeDtypeStruct((1, hv), f32)],
        scratch_shapes=[pltpu.VMEM((HS, hk, hv), f32), pltpu.VMEM((HS, C, C), f32),
                        pltpu.VMEM((HS, C, hk), f32), pltpu.VMEM((HS, C, hk), f32)],
        compiler_params=_cparams(("arbitrary", "arbitrary")),
    )(*args)


def _window_sums(x, back):
    n = x.shape[0]
    out = []
    s = x
    for w in (1, 2, 4, 8):
        s = s + pltpu.roll(s, w if back else n - w, 0)
        out.append(s)
    return out


def _pool_windows(hn, pad, name):
    Lp, D = hn.shape
    GW = D // POOL_GROUPS
    cb = min(GW, 256)
    per = GW // cb

    def body(h_ref, p_ref):
        g = pl.program_id(0) // per
        x = h_ref[...]
        s2, s4, s8, s16 = _window_sums(x, True)
        sel = jnp.where(g == 0, s2, jnp.where(g == 1, s4, jnp.where(g == 2, s8, s16)))
        win = jnp.left_shift(2, g).astype(f32)
        rows = lax.broadcasted_iota(jnp.int32, (Lp, 1), 0)
        t = (rows - pad).astype(f32)
        cnt = jnp.minimum(jnp.maximum(t, 0.0) + 1.0, win)
        p_ref[...] = jnp.where(rows >= pad, sel / cnt - x, 0.0).astype(bf16)

    return pl.pallas_call(
        body, name=name, grid=(D // cb,),
        in_specs=[pl.BlockSpec((Lp, cb), lambda i: (0, i))],
        out_specs=pl.BlockSpec((Lp, cb), lambda i: (0, i)),
        out_shape=jax.ShapeDtypeStruct((Lp, D), bf16),
        compiler_params=_cparams(("parallel",)),
    )(hn)


def _pool_windows_bwd(dp, pad, name):
    Lp, D = dp.shape
    GW = D // POOL_GROUPS
    cb = min(GW, 256)
    per = GW // cb

    def body(dp_ref, dh_ref):
        g = pl.program_id(0) // per
        rows = lax.broadcasted_iota(jnp.int32, (Lp, 1), 0)
        d = jnp.where(rows >= pad, dp_ref[...], 0.0)
        win = jnp.left_shift(2, g).astype(f32)
        t = (rows - pad).astype(f32)
        cnt = jnp.minimum(jnp.maximum(t, 0.0) + 1.0, win)
        s2, s4, s8, s16 = _window_sums(d / cnt, False)
        sel = jnp.where(g == 0, s2, jnp.where(g == 1, s4, jnp.where(g == 2, s8, s16)))
        dh_ref[...] = jnp.where(rows >= pad, sel - d, 0.0)

    return pl.pallas_call(
        body, name=name, grid=(D // cb,),
        in_specs=[pl.BlockSpec((Lp, cb), lambda i: (0, i))],
        out_specs=pl.BlockSpec((Lp, cb), lambda i: (0, i)),
        out_shape=jax.ShapeDtypeStruct((Lp, D), f32),
        compiler_params=_cparams(("parallel",)),
    )(dp)


def _pool_mix_fwd(xs, pooled, w, bias, scale, pad, name):
    Lp, D = xs.shape
    GW = D // POOL_GROUPS
    tm = _tile(Lp, 1056)

    def body(x_ref, p_ref, w_ref, b_ref, s_ref, o_ref):
        z = _dot(p_ref[...], w_ref[0]) + b_ref[...]
        keep = _row_ids(pl.program_id(1), tm) >= pad
        o_ref[...] = x_ref[...] + jnp.where(keep, z * s_ref[...], 0.0)

    blk = pl.BlockSpec((tm, GW), lambda g, i: (i, g))
    vec = pl.BlockSpec((1, GW), lambda g, i: (0, g))
    return pl.pallas_call(
        body, name=name, grid=(POOL_GROUPS, Lp // tm),
        in_specs=[blk, blk, pl.BlockSpec((1, GW, GW), lambda g, i: (g, 0, 0)), vec, vec],
        out_specs=blk, out_shape=jax.ShapeDtypeStruct((Lp, D), f32),
        compiler_params=_cparams(("parallel", "parallel")),
    )(xs, pooled, w, bias, scale)


def _pool_mix_bwd(dY, pooled, w, bias, scale, pad, name, after=None):
    Lp, D = dY.shape
    GW = D // POOL_GROUPS
    tm = _tile(Lp, 1056)
    nm = Lp // tm

    def body(dY_ref, p_ref, w_ref, b_ref, s_ref, dp_ref, dw_ref, db_ref, ds_ref, acc):
        i = pl.program_id(1)

        @pl.when(i == 0)
        def _():
            acc[...] = jnp.zeros_like(acc)
            db_ref[...] = jnp.zeros_like(db_ref)
            ds_ref[...] = jnp.zeros_like(ds_ref)

        keep = _row_ids(i, tm) >= pad
        dY_ = jnp.where(keep, dY_ref[...], 0.0)
        p = p_ref[...]
        z = _dot(p, w_ref[0]) + b_ref[...]
        ds_ref[...] += jnp.sum(dY_ * z, axis=0, keepdims=True)
        dz = dY_ * s_ref[...]
        db_ref[...] += jnp.sum(dz, axis=0, keepdims=True)
        dzb = dz.astype(bf16)
        acc[...] += _dot(p, dzb, TN)
        dp_ref[...] = _dot(dzb, w_ref[0], NT)

        @pl.when(i == nm - 1)
        def _():
            dw_ref[0] = acc[...].astype(bf16)

    blk = pl.BlockSpec((tm, GW), lambda g, i: (i, g))
    vec = pl.BlockSpec((1, GW), lambda g, i: (0, g))
    wsp = pl.BlockSpec((1, GW, GW), lambda g, i: (g, 0, 0))
    body, in_specs, args = _ordered(body, [blk, blk, wsp, vec, vec], [dY, pooled, w, bias, scale], after)
    return pl.pallas_call(
        body, name=name, grid=(POOL_GROUPS, nm),
        in_specs=in_specs, out_specs=[blk, wsp, vec, vec],
        out_shape=[jax.ShapeDtypeStruct((Lp, D), f32), jax.ShapeDtypeStruct((POOL_GROUPS, GW, GW), bf16),
                   jax.ShapeDtypeStruct((1, D), f32), jax.ShapeDtypeStruct((1, D), f32)],
        scratch_shapes=[pltpu.VMEM((GW, GW), f32)],
        compiler_params=_cparams(("parallel", "arbitrary")),
    )(*args)


def _loss_head(xs, target, g, first, name):
    Lp, D = xs.shape
    tm = GLA_CHUNK
    off = first // tm

    def body(x_ref, t_ref, g_ref, loss_ref, dxs_ref, dg_ref, half_ref):
        i = pl.program_id(0)

        @pl.when(i == 0)
        def _():
            loss_ref[...] = jnp.zeros_like(loss_ref)
            dg_ref[...] = jnp.zeros_like(dg_ref)

        @pl.when(i < off)
        def _():
            dxs_ref[...] = jnp.zeros_like(dxs_ref)
            half_ref[...] = jnp.zeros_like(half_ref)

        @pl.when(i >= off)
        def _():
            x = x_ref[...]
            rstd = lax.rsqrt(jnp.mean(x * x, axis=-1, keepdims=True) + EPS)
            xhat = x * rstd
            gg = g_ref[...]
            err = xhat * gg - t_ref[...]
            loss_ref[...] += 0.5 * jnp.sum(jnp.mean(err * err, axis=-1, keepdims=True))
            dy = err * (1.0 / D)
            dg_ref[...] += jnp.sum(dy * xhat, axis=0, keepdims=True)
            dxh = dy * gg
            out = rstd * (dxh - xhat * jnp.mean(dxh * xhat, axis=-1, keepdims=True))
            dxs_ref[...] = out
            half_ref[...] = (0.5 * out).astype(bf16)

    row = pl.BlockSpec((tm, D), lambda i: (i, 0))
    return pl.pallas_call(
        body, name=name, grid=(Lp // tm,),
        in_specs=[row, pl.BlockSpec((tm, D), lambda i: (jnp.maximum(i - off, 0), 0)), pl.BlockSpec((1, D), lambda i: (0, 0))],
        out_specs=[pl.BlockSpec((8, LANES), lambda i: (0, 0)), row, pl.BlockSpec((1, D), lambda i: (0, 0)), row],
        out_shape=[jax.ShapeDtypeStruct((8, LANES), f32), jax.ShapeDtypeStruct((Lp, D), f32),
                   jax.ShapeDtypeStruct((1, D), f32), jax.ShapeDtypeStruct((Lp, D), bf16)],
        compiler_params=_cparams(("arbitrary",)),
    )(xs, target, g)


def _adam_math(w, g, m, v):
    m2 = ADAM_B1 * m + (1.0 - ADAM_B1) * g
    v2 = ADAM_B2 * v + (1.0 - ADAM_B2) * (g * g)
    m_hat = m2 / (1.0 - ADAM_B1 ** ADAM_STEP)
    v_hat = v2 / (1.0 - ADAM_B2 ** ADAM_STEP)
    delta = -ADAM_LR * (m_hat / (jnp.sqrt(v_hat) + ADAM_EPS) + ADAM_WD * w)
    return delta, m2, v2


def _adamw(w, m, v, unit, own, own_idx, recv, prev, name, after=None):
    U, R, C = w.shape
    tr, tc = _tile2(R, C, 256, 8 if own.dtype == f32 and recv is None else 16)
    n_recv = 0 if recv is None else recv.shape[0]

    def body(idx_ref, w_ref, m_ref, v_ref, own_ref, *rest):
        rest = list(rest)
        recv_refs = [rest.pop(0) for _ in range(n_recv)]
        if prev is not None:
            rest = rest[4:]
        g_ref, d_ref, m2_ref, v2_ref = rest
        g = own_ref[0].astype(f32)
        for r_ref in recv_refs:
            g = g + r_ref[0].astype(f32)
        delta, m2, v2 = _adam_math(w_ref[0], g, m_ref[0], v_ref[0])
        g_ref[0] = g
        d_ref[0] = delta
        m2_ref[0] = m2
        v2_ref[0] = v2

    blk = pl.BlockSpec((1, tr, tc), lambda i, j, idx: (unit, i, j))
    in_specs = [blk, blk, blk, pl.BlockSpec((1, tr, tc), lambda i, j, idx: (idx[0], i, j))]
    args = [w, m, v, own]
    for p in range(n_recv):
        in_specs.append(pl.BlockSpec((1, tr, tc), lambda i, j, idx, p=p: (p, i, j)))
        args.append(recv)
    aliases = {}
    if prev is not None:
        for t in range(4):
            aliases[1 + len(args) + t] = t
        in_specs += [ANY] * 4
        args += list(prev)
    body, in_specs, args = _ordered(body, in_specs, args, after, lead=1)
    out = jax.ShapeDtypeStruct((U, R, C), f32)
    return pl.pallas_call(
        body, name=name,
        grid_spec=pltpu.PrefetchScalarGridSpec(
            num_scalar_prefetch=1, grid=(R // tr, C // tc), in_specs=in_specs, out_specs=[blk] * 4),
        out_shape=[out] * 4, input_output_aliases=aliases,
        compiler_params=_cparams(("parallel", "parallel")),
    )(own_idx, *args)


def _place():
    return lax.axis_index("x"), lax.axis_index("y"), lax.axis_index("c")


HBM = pl.BlockSpec(memory_space=pltpu.HBM)
SEM = pl.BlockSpec(memory_space=pltpu.SEMAPHORE)
VMEM_SPEC = pl.BlockSpec(memory_space=pltpu.VMEM)
EFFECT = pltpu.SideEffectType.DATAFLOW_SIDE_EFFECTING
TOKEN = jax.ShapeDtypeStruct((8, LANES), f32)


def _hbm(x):
    return pltpu.with_memory_space_constraint(x, pltpu.HBM)


def _hbm_like(xs):
    return [pltpu.HBM(x.shape, x.dtype) for x in xs]


def _slot(px, py, pc):
    return 4 * px + 2 * py + pc


def _halves(ref):
    n = ref.shape[0]
    cut = n // 2 if n < 32 else (n // 2) // 16 * 16
    return ref.at[pl.ds(0, cut)], ref.at[pl.ds(cut, n - cut)]


def _gather_start(shards, after, name):
    n = len(shards)
    me = _slot(*_place())
    bufs = [lax.dynamic_update_slice(lax.empty((N_DEV,) + s.shape, s.dtype), s[None], (me,) + (0,) * s.ndim) for s in shards]

    def body(*refs):
        ins, land = refs[:n], refs[n:2 * n]
        send, recv = refs[2 * n + 1], refs[2 * n + 2]
        token = refs[-1]
        x, y, c = _place()
        to = [(x, y, 1 - c), (1 - x, y, c), (x, 1 - y, c)]
        for a in range(n):
            for k, dev in enumerate(to):
                pltpu.make_async_remote_copy(
                    src_ref=ins[a], dst_ref=land[a].at[_slot(x, y, c)], send_sem=send.at[3 * a + k], recv_sem=recv.at[3 * a + k],
                    device_id=dev, device_id_type=MESH).start()
        token[...] = jnp.zeros_like(token)

    out = pl.pallas_call(
        body, name=name,
        in_specs=[HBM] * (2 * n) + [ANY],
        out_specs=[SEM, SEM] + [HBM] * (2 * n) + [VMEM_SPEC],
        out_shape=[pltpu.SemaphoreType.DMA((3 * n,)), pltpu.SemaphoreType.DMA((3 * n,))] + _hbm_like(shards) + _hbm_like(bufs) + [TOKEN],
        input_output_aliases={i: 2 + i for i in range(2 * n)},
        compiler_params=pltpu.CompilerParams(has_side_effects=EFFECT),
    )(*[_hbm(s) for s in shards], *[_hbm(b) for b in bufs], after)
    return dict(send1=out[0], recv1=out[1], shards=list(out[2:2 + n]), bufs=list(out[2 + n:2 + 2 * n]), token=out[-1])


def _gather_mid(h, after, name):
    n = len(h["bufs"])

    def body(*refs):
        land, recv1 = refs[:n], refs[n]
        send2, recv2 = refs[n + 2], refs[n + 3]
        token = refs[-1]
        x, y, c = _place()
        nbr = [(1 - x, y, c), (x, 1 - y, c)]
        for j, dev in enumerate(nbr):
            for a in range(n):
                blk = land[a].at[_slot(*dev)]
                pltpu.make_async_remote_copy(
                    src_ref=blk, dst_ref=blk, send_sem=send2.at[4 * a + j], recv_sem=recv1.at[3 * a + 1 + j],
                    device_id=dev, device_id_type=MESH).wait_recv()
                pltpu.make_async_remote_copy(
                    src_ref=blk, dst_ref=blk, send_sem=send2.at[4 * a + j], recv_sem=recv2.at[4 * a + j],
                    device_id=(x, y, 1 - c), device_id_type=MESH).start()
        for a in range(n):
            from_x, from_y = land[a].at[_slot(*nbr[0])], land[a].at[_slot(*nbr[1])]
            for k, (half, dev) in enumerate([(_halves(from_y)[0], nbr[0]), (_halves(from_x)[1], nbr[1])]):
                pltpu.make_async_remote_copy(
                    src_ref=half, dst_ref=half, send_sem=send2.at[4 * a + 2 + k], recv_sem=recv2.at[4 * a + 2 + k],
                    device_id=dev, device_id_type=MESH).start()
        token[...] = jnp.zeros_like(token)

    out = pl.pallas_call(
        body, name=name,
        in_specs=[HBM] * n + [SEM, ANY],
        out_specs=[SEM, SEM] + [HBM] * n + [VMEM_SPEC],
        out_shape=[pltpu.SemaphoreType.DMA((4 * n,)), pltpu.SemaphoreType.DMA((4 * n,))] + _hbm_like(h["bufs"]) + [TOKEN],
        input_output_aliases={i: 2 + i for i in range(n)},
        compiler_params=pltpu.CompilerParams(has_side_effects=EFFECT),
    )(*h["bufs"], h["recv1"], after)
    h.update(send2=out[0], recv2=out[1], bufs=list(out[2:2 + n]), token=out[-1])
    return h


def _gather_mid2(h, after, name):
    n = len(h["bufs"])

    def body(*refs):
        land, recv2 = refs[:n], refs[n]
        send3, recv3 = refs[n + 2], refs[n + 3]
        token = refs[-1]
        x, y, c = _place()
        for a in range(n):
            blk = land[a].at[_slot(1 - x, 1 - y, c)]
            for k, half in enumerate(_halves(blk)):
                pltpu.make_async_remote_copy(
                    src_ref=half, dst_ref=half, send_sem=send3.at[a], recv_sem=recv2.at[4 * a + 2 + k],
                    device_id=(x, y, 1 - c), device_id_type=MESH).wait_recv()
            pltpu.make_async_remote_copy(
                src_ref=blk, dst_ref=blk, send_sem=send3.at[a], recv_sem=recv3.at[a],
                device_id=(x, y, 1 - c), device_id_type=MESH).start()
        token[...] = jnp.zeros_like(token)

    out = pl.pallas_call(
        body, name=name,
        in_specs=[HBM] * n + [SEM, ANY],
        out_specs=[SEM, SEM] + [HBM] * n + [VMEM_SPEC],
        out_shape=[pltpu.SemaphoreType.DMA((n,)), pltpu.SemaphoreType.DMA((n,))] + _hbm_like(h["bufs"]) + [TOKEN],
        input_output_aliases={i: 2 + i for i in range(n)},
        compiler_params=pltpu.CompilerParams(has_side_effects=EFFECT),
    )(*h["bufs"], h["recv2"], after)
    h.update(send3=out[0], recv3=out[1], bufs=list(out[2:2 + n]), token=out[-1])
    return h


def _gather_end(h, after, name):
    n = len(h["bufs"])

    def body(*refs):
        ins, land = refs[:n], refs[n:2 * n]
        send1, recv1, send2, recv2, send3, recv3 = refs[2 * n:2 * n + 6]
        x, y, c = _place()
        sib = (x, y, 1 - c)
        nbr = [(1 - x, y), (x, 1 - y)]

        def wait(src, dst, ssem, rsem, send):
            cp = pltpu.make_async_remote_copy(src_ref=src, dst_ref=dst, send_sem=ssem, recv_sem=rsem, device_id=sib, device_id_type=MESH)
            cp.wait_send() if send else cp.wait_recv()

        for a in range(n):
            mine = land[a].at[_slot(x, y, c)]
            for k in range(3):
                wait(ins[a], mine, send1.at[3 * a + k], recv1.at[3 * a + k], True)
            wait(ins[a], land[a].at[_slot(x, y, 1 - c)], send1.at[3 * a], recv1.at[3 * a], False)
            for j, (px, py) in enumerate(nbr):
                sent = land[a].at[_slot(px, py, c)]
                wait(sent, sent, send2.at[4 * a + j], recv2.at[4 * a + j], True)
                wait(sent, land[a].at[_slot(px, py, 1 - c)], send2.at[4 * a + j], recv2.at[4 * a + j], False)
            halves = [_halves(land[a].at[_slot(*nbr[1], c)])[0], _halves(land[a].at[_slot(*nbr[0], c)])[1]]
            for k, half in enumerate(halves):
                wait(half, half, send2.at[4 * a + 2 + k], recv2.at[4 * a + 2 + k], True)
            diag = land[a].at[_slot(1 - x, 1 - y, c)]
            wait(diag, diag, send3.at[a], recv3.at[a], True)
            wait(diag, land[a].at[_slot(1 - x, 1 - y, 1 - c)], send3.at[a], recv3.at[a], False)

    out = pl.pallas_call(
        body, name=name,
        in_specs=[HBM] * (2 * n) + [SEM] * 6 + [ANY],
        out_specs=[HBM] * n,
        out_shape=_hbm_like(h["bufs"]),
        input_output_aliases={n + i: i for i in range(n)},
        compiler_params=pltpu.CompilerParams(has_side_effects=EFFECT),
    )(*h["shards"], *h["bufs"], h["send1"], h["recv1"], h["send2"], h["recv2"], h["send3"], h["recv3"], after)
    return list(out)


def _peer_plan(kind, x, y, c):
    if kind == "pair":
        return [(2 * q + (1 - c), q, (x, y, 1 - c)) for q in range(4)]
    chips = [(1 - x, y), (x, 1 - y), (1 - x, 1 - y)]
    return [(2 * px + py, k, (px, py, c)) for k, (px, py) in enumerate(chips)]


def _exchange_start(kind, srcs, after, name):
    n = len(srcs)
    K = 4 if kind == "pair" else 3
    lands = [_hbm(lax.empty((K,) + s.shape[1:], s.dtype)) for s in srcs]

    def body(*refs):
        ins, land = refs[:n], refs[n:2 * n]
        send, recv = refs[2 * n + 1], refs[2 * n + 2]
        token = refs[-1]
        for a in range(n):
            for k, (si, di, dev) in enumerate(_peer_plan(kind, *_place())):
                pltpu.make_async_remote_copy(
                    src_ref=ins[a].at[si], dst_ref=land[a].at[di], send_sem=send.at[K * a + k], recv_sem=recv.at[K * a + k],
                    device_id=dev, device_id_type=MESH).start()
        token[...] = jnp.zeros_like(token)

    out = pl.pallas_call(
        body, name=name,
        in_specs=[HBM] * (2 * n) + [ANY],
        out_specs=[SEM, SEM] + [HBM] * (2 * n) + [VMEM_SPEC],
        out_shape=[pltpu.SemaphoreType.DMA((K * n,)), pltpu.SemaphoreType.DMA((K * n,))] + _hbm_like(srcs) + _hbm_like(lands) + [TOKEN],
        input_output_aliases={i: 2 + i for i in range(2 * n)},
        compiler_params=pltpu.CompilerParams(has_side_effects=EFFECT),
    )(*[_hbm(s) for s in srcs], *lands, after)
    return dict(kind=kind, send=out[0], recv=out[1], srcs=list(out[2:2 + n]), lands=list(out[2 + n:2 + 2 * n]), token=out[-1])


def _exchange_wait(h, after, name):
    n = len(h["srcs"])
    kind = h["kind"]
    K = 4 if kind == "pair" else 3

    def body(*refs):
        ins, land = refs[:n], refs[n:2 * n]
        send, recv = refs[2 * n], refs[2 * n + 1]
        for a in range(n):
            for k, (si, di, dev) in enumerate(_peer_plan(kind, *_place())):
                cp = pltpu.make_async_remote_copy(
                    src_ref=ins[a].at[si], dst_ref=land[a].at[di], send_sem=send.at[K * a + k], recv_sem=recv.at[K * a + k],
                    device_id=dev, device_id_type=MESH)
                cp.wait_send()
                cp.wait_recv()

    out = pl.pallas_call(
        body, name=name,
        in_specs=[HBM] * (2 * n) + [SEM, SEM, ANY],
        out_specs=[HBM] * (2 * n),
        out_shape=_hbm_like(h["srcs"]) + _hbm_like(h["lands"]),
        input_output_aliases={i: i for i in range(2 * n)},
        compiler_params=pltpu.CompilerParams(has_side_effects=EFFECT),
    )(*h["srcs"], *h["lands"], h["send"], h["recv"], after)
    return list(out[:n]), list(out[n:])


def _pair_add(gs, gots, c_idx, name):
    n = len(gs)
    _, R, C = gs[0].shape
    tr, tc = _tile2(R, C, 512, 16)

    def body(c_ref, *refs):
        for a in range(n):
            refs[2 * n + a][0] = (refs[a][0].astype(f32) + refs[n + a][0].astype(f32)).astype(bf16)

    mine = pl.BlockSpec((1, tr, tc), lambda q, i, j, c: (2 * q + c[0], i, j))
    blk = pl.BlockSpec((1, tr, tc), lambda q, i, j, c: (q, i, j))
    return pl.pallas_call(
        body, name=name,
        grid_spec=pltpu.PrefetchScalarGridSpec(
            num_scalar_prefetch=1, grid=(4, R // tr, C // tc),
            in_specs=[mine] * n + [blk] * n, out_specs=[blk] * n),
        out_shape=[jax.ShapeDtypeStruct((4, R, C), bf16)] * n,
        compiler_params=_cparams(("parallel", "parallel", "parallel")),
    )(c_idx, *gs, *gots)


def _small_exchange(send, gather, name, after=None):
    R = send.shape[-2]

    def body(in_ref, out_ref, send_sems, recv_sems):
        x, y, c = _place()
        me = 4 * x + 2 * y + c
        out_ref[me] = in_ref[...] if gather else in_ref[me]
        cps = []
        for k in range(1, N_DEV):
            px, py, pc = x ^ ((k >> 2) & 1), y ^ ((k >> 1) & 1), c ^ (k & 1)
            src = in_ref if gather else in_ref.at[4 * px + 2 * py + pc]
            cps.append(pltpu.make_async_remote_copy(
                src_ref=src, dst_ref=out_ref.at[me],
                send_sem=send_sems.at[k - 1], recv_sem=recv_sems.at[k - 1],
                device_id=(px, py, pc), device_id_type=MESH))
        for cp in cps:
            cp.start()
        for cp in cps:
            cp.wait()

    body, in_specs, args = _ordered(body, [pl.BlockSpec(memory_space=pltpu.VMEM)], [send], after)
    return pl.pallas_call(
        body, name=name,
        in_specs=in_specs, out_specs=pl.BlockSpec(memory_space=pltpu.VMEM),
        out_shape=jax.ShapeDtypeStruct((N_DEV, R, LANES), f32),
        scratch_shapes=[pltpu.SemaphoreType.DMA((N_DEV - 1,)), pltpu.SemaphoreType.DMA((N_DEV - 1,))],
    )(*args)


def _sum_blocks(blocks, name):
    def body(in_ref, o_ref):
        s = in_ref[0]
        for d in range(1, N_DEV):
            s = s + in_ref[d]
        o_ref[0] = s

    return pl.pallas_call(body, name=name, out_shape=jax.ShapeDtypeStruct((1,) + blocks.shape[1:], f32))(blocks)


def _rows(n):
    return -(-n // LANES)


def _pack(arrs, total_rows):
    parts = []
    for a in arrs:
        flat = a.reshape(-1).astype(f32)
        parts.append(jnp.pad(flat, (0, _rows(flat.size) * LANES - flat.size)))
    flat = jnp.concatenate(parts)
    return jnp.pad(flat, (0, total_rows * LANES - flat.size)).reshape(total_rows, LANES)


def _unpack(packed, shapes):
    lead = packed.shape[:-2]
    flat = packed.reshape(lead + (-1,))
    out, pos = [], 0
    for s in shapes:
        n = 1
        for d in s:
            n *= d
        out.append(flat[..., pos:pos + n].reshape(lead + tuple(s)))
        pos += _rows(n) * LANES
    return out


def _to_shards(full, axis):
    s = full.shape
    return jnp.moveaxis(full.reshape(s[:axis] + (N_DEV, s[axis] // N_DEV) + s[axis + 1:]), axis, 0)


def _from_shards(sh, axis):
    m = jnp.moveaxis(sh, 0, axis)
    s = m.shape
    return m.reshape(s[:axis] + (s[axis] * s[axis + 1],) + s[axis + 2:])


def kernel(x, meta, ffn_norm, ffn_w_gate, ffn_w_up, ffn_w_down, gla_norm, gla_w_in, gla_w_lr, gla_b_lr, gla_head_norm, gla_w_out, pool_norm, pool_w, pool_b, pool_scale, final_norm, loss_target, m_meta, m_ffn_norm, m_ffn_w_gate, m_ffn_w_up, m_ffn_w_down, m_gla_norm, m_gla_w_in, m_gla_w_lr, m_gla_b_lr, m_gla_head_norm, m_gla_w_out, m_pool_norm, m_pool_w, m_pool_b, m_pool_scale, m_final_norm, v_meta, v_ffn_norm, v_ffn_w_gate, v_ffn_w_up, v_ffn_w_down, v_gla_norm, v_gla_w_in, v_gla_w_lr, v_gla_b_lr, v_gla_head_norm, v_gla_w_out, v_pool_norm, v_pool_w, v_pool_b, v_pool_scale, v_final_norm):
    H = GLA_HEADS
    _, SEQ, D = x.shape
    Fs = ffn_w_gate.shape[-1]
    DK, DV = D // 2, D
    hv = DV // H
    GW = D // POOL_GROUPS
    INW = 2 * DK + 2 * DV + GATE_RANK
    NPK = 2 * DK + 2 * DV + GATE_PAD
    pad = (-N_META) % GLA_CHUNK
    first = pad + N_META
    Lp = first + SEQ
    n_units = ffn_w_gate.shape[0] * ffn_w_gate.shape[1]
    assert first % GLA_CHUNK == 0 and Lp % GLA_CHUNK == 0 and pad >= POOL_GROUPS * 4

    px, py, pc = _place()
    c_idx = jnp.reshape(pc, (1,)).astype(jnp.int32)
    q_idx = jnp.reshape(2 * px + py, (1,)).astype(jnp.int32)
    zero_idx = jnp.zeros((1,), jnp.int32)

    small_sh = [meta, ffn_norm, gla_w_lr, pool_norm, pool_b, pool_scale]
    small_axis = [1, 2, 2, 1, 2, 1]
    sh_shapes = [a.shape for a in small_sh]
    sh_rows = -(-sum(_rows(a.size) for a in small_sh) // 8) * 8
    gathered = _small_exchange(_pack(small_sh, sh_rows), True, "small_gather")
    meta_f, ffn_norm_f, wlr_f, pool_norm_f, pool_b_f, pool_scale_f = [
        _from_shards(a, ax) for a, ax in zip(_unpack(gathered, sh_shapes), small_axis)]
    ffn_norm_f = ffn_norm_f.reshape(n_units, 1, D)
    wlr128 = jnp.pad(wlr_f[0], ((0, GATE_PAD - GATE_RANK), (0, 0)))

    def t_units(w):
        return jnp.swapaxes(w, -1, -2).reshape(n_units, Fs, D)

    ffn_f32 = [t_units(ffn_w_gate), t_units(ffn_w_up), ffn_w_down.reshape(n_units, Fs, D)]
    mixer_f32 = [gla_w_in[0].T[None], gla_w_out, pool_w[0].reshape(1, -1, GW)]
    gather_order = [("ffn0", ffn_f32, 0), ("mixers", mixer_f32, 0)] + [(f"ffn{u}", ffn_f32, u) for u in range(1, n_units)]
    c_lr = 2 * DK + DV
    c_r = 2 * DK + 2 * DV
    gate_blk = c_r // GATE_PAD

    def cast_shards(i, after):
        tag, arrays, u = gather_order[i]
        shards = [_cast_unit(w, u, f"cast_{tag}_{a}", after) for a, w in enumerate(arrays)]
        if tag == "mixers":
            shards[2] = shards[2].reshape(pool_w.shape[1:])
        return shards

    def pass_on(i, h, after):
        tag = gather_order[i][0]
        nxt = later_shards[i + 1] if i + 1 < len(gather_order) else None
        h = _gather_mid(h, after, f"gather_mid_{tag}")
        if nxt is not None:
            nxt = _gather_start(nxt, h["token"], f"gather_start_{gather_order[i + 1][0]}")
        return h, nxt

    def complete(i, h, after):
        tag = gather_order[i][0]
        h = _gather_mid2(h, after, f"gather_mid2_{tag}")
        return _gather_end(h, h["token"], f"gather_end_{tag}")

    xs = jnp.concatenate([jnp.zeros((pad, D), f32), meta_f, x[0]], axis=0)
    saved = {}
    ffn_w = [None] * n_units

    def ffn_f(u, xs, after=None):
        out, h, G, U = _ffn_fwd(xs, ffn_norm_f[u], *ffn_w[u], name=f"ffn_fwd{u}", after=after)
        saved[("ffn", u)] = (xs, h, G, U)
        return out

    def gla_f(xs, win_p, wout_full, after=None):
        hn = _rms_fwd(xs, gla_norm, bf16, "gla_norm_fwd", after=after)
        proj = _mm(hn, win_p, "nt", f32, "gla_proj", tm=1056, tn=896, tk=2048)
        lg = _gate_fwd(proj, wlr128, gla_b_lr, pad, gate_blk, "gla_gate_fwd")
        o, y, states = _gla_fwd(proj, lg, gla_head_norm, H, "gla_core_fwd")
        out = _mm(y, wout_full, "nn", f32, "gla_out", tm=1056, tn=512, tk=2048, residual=xs)
        saved["gla"] = (xs, hn, proj, lg, o, y, states)
        return out

    def pool_f(xs, wpool_full):
        hn = _rms_fwd(xs, pool_norm_f, f32, "pool_norm_fwd")
        pooled = _pool_windows(hn, pad, "pool_windows_fwd")
        out = _pool_mix_fwd(xs, pooled, wpool_full, pool_b_f.reshape(1, D), pool_scale_f, pad, "pool_mix_fwd")
        saved["pool"] = (xs, pooled)
        return out

    depth = ffn_w_gate.shape[0]
    assert depth == 2 and n_units == 4
    h0 = _gather_start(cast_shards(0, None), gathered, "gather_start_ffn0")
    later_shards = {}
    last = h0["token"]
    for i in range(1, len(gather_order)):
        later_shards[i] = cast_shards(i, last)
        last = later_shards[i][0]
    h0, h1 = pass_on(0, h0, last)
    ffn_w[0] = complete(0, h0, h1["token"])
    h1, h2 = pass_on(1, h1, ffn_w[0][0])
    xs = ffn_f(0, xs, after=h2["token"])
    win_g, wout_g, wpool_g = complete(1, h1, xs)
    h2, h3 = pass_on(2, h2, wout_g)
    win_full = win_g.reshape(INW, D)
    win_p = jnp.concatenate([win_full[:c_lr], win_full[c_lr + GATE_RANK:], win_full[c_lr:c_lr + GATE_RANK],
                             jnp.zeros((GATE_PAD - GATE_RANK, D), bf16)], axis=0)
    wout_full = wout_g.reshape(DV, D)
    wpool_full = _from_shards(wpool_g, 1)
    xs = gla_f(xs, win_p, wout_full, after=h3["token"])
    ffn_w[1] = complete(2, h2, xs)
    h3, h4 = pass_on(3, h3, ffn_w[1][0])
    xs = ffn_f(1, xs, after=h4["token"])
    ffn_w[2] = complete(3, h3, xs)
    h4, _ = pass_on(4, h4, ffn_w[2][0])
    xs = ffn_f(2, xs, after=h4["token"])
    xs = pool_f(xs, wpool_full)
    ffn_w[3] = complete(4, h4, xs)
    xs = ffn_f(3, xs)
    loss_part, dxs, d_final, dyh = _loss_head(xs, loss_target[0], final_norm.reshape(1, D), first, "loss_head")

    class Reduce:
        def __init__(self, tag, grads, after=None):
            self.tag = tag
            self.h = _exchange_start("pair", grads, loss_part if after is None else after, f"pair_start_{tag}")
            self.token = self.h["token"]

        def mid(self, after):
            grads, got = _exchange_wait(self.h, after, f"pair_wait_{self.tag}")
            if len({g.shape for g in grads}) == 1:
                self.sums = list(_pair_add(grads, got, c_idx, f"pair_add_{self.tag}"))
            else:
                self.sums = [_pair_add([g], [r], c_idx, f"pair_add_{self.tag}{a}")[0] for a, (g, r) in enumerate(zip(grads, got))]
            self.h = _exchange_start("chips", self.sums, loss_part, f"chips_start_{self.tag}")
            self.token = self.h["token"]

        def end(self, after):
            sums, recv = _exchange_wait(self.h, after, f"chips_wait_{self.tag}")
            return list(zip(sums, recv))

    d_ffn_norm = [None] * n_units
    small_grads = {}

    def ffn_b(u, dY, dyh, prev):
        xs_in, h_, G, U = saved[("ffn", u)]
        wg, wu, wd = ffn_w[u]
        tok = None if prev is None else prev.token
        dG, dU, A = _ffn_bwd_act(dyh, wd, G, U, f"ffn_act{u}", after=tok)
        dh = _ffn_bwd_dh(dG, dU, wg, wu, f"ffn_dh{u}")
        dxs, dg, dyh_next = _rms_bwd(dY, dh, xs_in, ffn_norm_f[u], pad, f"ffn_norm_bwd{u}")
        if prev is not None:
            prev.mid(dxs)
            tok = prev.token
        dwg = _ffn_bwd_wgrad(dG, h_, f"ffn_wgrad_gate{u}", after=tok)
        dwu = _ffn_bwd_wgrad(dU, h_, f"ffn_wgrad_up{u}", after=tok)
        dwd = _ffn_bwd_wgrad(A, dyh, f"ffn_wgrad_down{u}", after=tok)
        d_ffn_norm[u] = dg
        return dxs, dyh_next, Reduce(f"ffn{u}", [dwg, dwu, dwd])

    def gla_b(dY, prev):
        xs_in, hn, proj, lg, o, y, states = saved["gla"]
        dyb = dY.astype(bf16)
        dy = _mm(dyb, wout_full, "nt", f32, "gla_out_dgrad", tm=1056, tn=512, tk=2048, after=prev.token)
        dwout = _mm_tn_full(y, dyb, "gla_out_wgrad", 1024, after=prev.token)
        prev.mid(dwout)
        dq, dk, dv, dr, dlg, dhw = _gla_bwd(dy, proj, lg, o, states, gla_head_norm, H, pad, "gla_core_bwd", after=prev.token)
        dlr, dwlr, dblr = _gate_bwd(dlg, proj, wlr128, gla_b_lr, pad, gate_blk, "gla_gate_bwd")
        dproj = jnp.concatenate([dq, dk, dv, dr, dlr], axis=1)
        dwin_p = _mm_tn_full(dproj, hn, "gla_proj_wgrad", 896)
        dhn = _mm(dproj, win_p, "nn", f32, "gla_proj_dgrad", tm=1056, tn=1024, tk=896)
        dxs, dgn, dyh_next = _rms_bwd(dY, dhn, xs_in, gla_norm, pad, "gla_norm_bwd")
        dwin = jnp.concatenate([dwin_p[:c_lr], dwin_p[c_r:c_r + GATE_RANK], dwin_p[c_lr:c_r]], axis=0)
        small_grads.update(gla_w_lr=dwlr[:GATE_RANK][None], gla_b_lr=dblr, gla_head_norm=dhw, gla_norm=dgn)
        return dxs, dyh_next, Reduce("gla", [dwin.reshape(N_DEV, INW // N_DEV, D), dwout.reshape(N_DEV, DV // N_DEV, D)])

    def pool_b_(dY, prev):
        xs_in, pooled = saved["pool"]
        dp, dw, db, ds = _pool_mix_bwd(dY, pooled, wpool_full, pool_b_f.reshape(1, D), pool_scale_f, pad, "pool_mix_bwd",
                                       after=prev.token)
        dhn = _pool_windows_bwd(dp, pad, "pool_windows_bwd")
        dxs, dgn, dyh_next = _rms_bwd(dY, dhn, xs_in, pool_norm_f, pad, "pool_norm_bwd")
        prev.mid(dxs)
        dws = _to_shards(dw, 1)
        small_grads.update(pool_b=db.reshape(1, POOL_GROUPS, GW), pool_scale=ds, pool_norm=dgn)
        return dxs, dyh_next, Reduce("pool", [dws.reshape(N_DEV, POOL_GROUPS * GW // N_DEV, GW)], after=prev.token)

    sh_names = ["meta", "ffn_norm", "gla_w_lr", "pool_norm", "pool_b", "pool_scale"]
    rep_names = ["gla_norm", "gla_b_lr", "gla_head_norm", "final_norm"]
    rep_w = [gla_norm, gla_b_lr, gla_head_norm, final_norm]
    rep_shapes = [a.shape for a in rep_w]
    rep_rows = -(-sum(_rows(a.size) for a in rep_w) // 8) * 8

    def small_path(dxs0):
        small_grads.update(meta=dxs0[pad:first], ffn_norm=jnp.concatenate(d_ffn_norm, axis=0).reshape(n_units // 2, 2, D),
                           final_norm=d_final.reshape(D))
        by_owner = [_to_shards(small_grads[nm].reshape(full_shape), ax) for nm, full_shape, ax in zip(
            sh_names, [meta_f.shape, (ffn_norm.shape[0], 2, D), wlr_f.shape, pool_norm_f.shape, pool_b_f.shape, pool_scale_f.shape],
            small_axis)]
        rep_pack = _pack([small_grads[nm].reshape(s) for nm, s in zip(rep_names, rep_shapes)], rep_rows)
        send = jnp.stack([
            jnp.concatenate([_pack([g[d] for g in by_owner], sh_rows), rep_pack, loss_part], axis=0) for d in range(N_DEV)])
        total = _sum_blocks(_small_exchange(send, False, "small_reduce"), "small_sum")
        n_small = sh_rows + rep_rows

        def pack_small(sh_list, rep_list):
            return jnp.concatenate([_pack(sh_list, sh_rows), _pack(rep_list, rep_rows)], axis=0)[None]

        w_small = pack_small(small_sh, rep_w)
        m_small = pack_small([m_meta, m_ffn_norm, m_gla_w_lr, m_pool_norm, m_pool_b, m_pool_scale],
                             [m_gla_norm, m_gla_b_lr, m_gla_head_norm, m_final_norm])
        v_small = pack_small([v_meta, v_ffn_norm, v_gla_w_lr, v_pool_norm, v_pool_b, v_pool_scale],
                             [v_gla_norm, v_gla_b_lr, v_gla_head_norm, v_final_norm])
        small_out = _adamw(w_small, m_small, v_small, 0, total[:, :n_small], zero_idx, None, None, "adamw_small")
        small_res = {}
        for kind, packed in zip(("grad", "delta", "new_m", "new_v"), small_out):
            sh_vals = _unpack(packed[0, :sh_rows], sh_shapes)
            rep_vals = _unpack(packed[0, sh_rows:], rep_shapes)
            for nm, val in zip(sh_names + rep_names, sh_vals + rep_vals):
                small_res[(kind, nm)] = val
        return total[0, n_small, 0], small_res, small_out[0]

    def ffn_b_last(dY, dyh, prev):
        xs_in, h_, G, U = saved[("ffn", 0)]
        wg, wu, wd = ffn_w[0]
        dG, dU, A = _ffn_bwd_act(dyh, wd, G, U, "ffn_act0", after=prev.token)
        prev.mid(dG)
        dwd = _ffn_bwd_wgrad(A, dyh, "ffn_wgrad_down0", after=prev.token)
        r_d = Reduce("ffn0_down", [dwd])
        dh = _ffn_bwd_dh(dG, dU, wg, wu, "ffn_dh0", after=r_d.token)
        dxs, dg, _ = _rms_bwd(dY, dh, xs_in, ffn_norm_f[0], pad, "ffn_norm_bwd0")
        d_ffn_norm[0] = dg
        small = small_path(dxs)
        r_d.mid(small[2])
        dwg = _ffn_bwd_wgrad(dG, h_, "ffn_wgrad_gate0", after=r_d.token)
        r_g = Reduce("ffn0_gate", [dwg])
        dwu = _ffn_bwd_wgrad(dU, h_, "ffn_wgrad_up0", after=r_g.token)
        r_g.mid(dwu)
        r_u = Reduce("ffn0_up", [dwu], after=r_g.token)
        return dxs, small, (r_g, r_u, r_d)

    dxs, dyh, r3 = ffn_b(3, dxs, dyh, None)
    dxs, dyh, rp = pool_b_(dxs, r3)
    dxs, dyh, r2 = ffn_b(2, dxs, dyh, rp)
    dxs, dyh, r1 = ffn_b(1, dxs, dyh, r2)
    dxs, dyh, rg = gla_b(dxs, r1)
    dxs, (loss, small_res, _), r0 = ffn_b_last(dxs, dyh, rg)
    grad_x = dxs[first:].reshape(x.shape)
    r_last = r0[1]

    big_res = {}

    def adam_one(nm, w, m, v, entry, transposed=False):
        sums, recv = entry
        R, C = sums.shape[1:]
        w1, m1, v1 = ((t[0].T if transposed else t).reshape(1, R, C) for t in (w, m, v))
        out = _adamw(w1, m1, v1, 0, sums, q_idx, recv, None, f"adamw_{nm}", after=r_last.token)
        for kind, val in zip(("grad", "delta", "new_m", "new_v"), out):
            big_res[(kind, nm)] = val[0].T[None] if transposed else val.reshape(w.shape)
        return out[0]

    e_gla = rg.end(dxs)
    done = adam_one("gla_w_in", gla_w_in, m_gla_w_in, v_gla_w_in, e_gla[0], transposed=True)
    done = adam_one("gla_w_out", gla_w_out, m_gla_w_out, v_gla_w_out, e_gla[1])
    done = adam_one("pool_w", pool_w, m_pool_w, v_pool_w, rp.end(done)[0])
    r_last.mid(done)

    ffn_names = ["ffn_w_gate", "ffn_w_up", "ffn_w_down"]
    ffn_wmv = [tuple(t_units(t) for t in (ffn_w_gate, m_ffn_w_gate, v_ffn_w_gate)),
               tuple(t_units(t) for t in (ffn_w_up, m_ffn_w_up, v_ffn_w_up)),
               tuple(t.reshape(n_units, Fs, D) for t in (ffn_w_down, m_ffn_w_down, v_ffn_w_down))]
    ffn_prev = [[lax.empty((n_units, Fs, D), f32) for _ in range(4)] for _ in range(3)]
    order_after = r_last.token
    for u, red in ((3, r3), (2, r2), (1, r1), (0, r0)):
        entries = [r.end(done)[0] for r in red] if u == 0 else red.end(done)
        for a in range(3):
            sums, recv = entries[a]
            ffn_prev[a] = _adamw(*ffn_wmv[a], u, sums, q_idx, recv, ffn_prev[a], f"adamw_{ffn_names[a]}{u}", after=order_after)
            done = order_after = ffn_prev[a][0]
    for a in range(3):
        for kind, val in zip(("grad", "delta", "new_m", "new_v"), ffn_prev[a]):
            val = val.reshape(ffn_w_down.shape)
            big_res[(kind, ffn_names[a])] = val if a == 2 else jnp.swapaxes(val, -1, -2)

    order = ["meta", "ffn_norm", "ffn_w_gate", "ffn_w_up", "ffn_w_down", "gla_norm", "gla_w_in", "gla_w_lr", "gla_b_lr",
             "gla_head_norm", "gla_w_out", "pool_norm", "pool_w", "pool_b", "pool_scale", "final_norm"]
    res = {**small_res, **big_res}
    outs = [loss, grad_x]
    for kind in ("grad", "delta", "new_m", "new_v"):
        outs += [res[(kind, nm)] for nm in order]
    return tuple(outs)
```

```python
import jax
import jax.numpy as jnp
from jax import lax
from jax.experimental import pallas as pl
from jax.experimental.pallas import tpu as pltpu

f32 = jnp.float32
bf16 = jnp.bfloat16

N_DEV = 8
N_META = 16
GLA_HEADS = 4
GLA_CHUNK = 64
GLA_SUB = 16
GLA_HEADS_PER_STEP = 4
GATE_RANK = 16
GATE_PAD = 128
GATE_NORM = 16.0
EPS = 1e-6
POOL_GROUPS = 4
ADAM_LR = 0.001
ADAM_B1 = 0.9
ADAM_B2 = 0.999
ADAM_EPS = 1e-08
ADAM_WD = 0.01
ADAM_STEP = 10
LANES = 128
VMEM_LIMIT_MB = 56

NN = (((1,), (0,)), ((), ()))
NT = (((1,), (1,)), ((), ()))
TN = (((0,), (0,)), ((), ()))
HI = lax.Precision.HIGHEST
MESH = pl.DeviceIdType.MESH
ANY = pl.BlockSpec(memory_space=pl.ANY)


def _cparams(sem=None, vmem_mb=None):
    kw = {}
    if sem is not None:
        kw["dimension_semantics"] = sem
    if vmem_mb is not None:
        kw["vmem_limit_bytes"] = vmem_mb * 2 ** 20
    return pltpu.CompilerParams(**kw)


def _tile(n, target, mult=16):
    best = None
    for t in range(mult, min(n, target) + 1, mult):
        if n % t == 0:
            best = t
    assert best is not None, (n, target, mult)
    return best


def _tile2(R, C, rows, mult):
    if R % mult == 0:
        return _tile(R, rows, mult), C
    return R, _tile(C, 256, LANES)


def _dot(a, b, dims=NN, precision=None):
    return lax.dot_general(a, b, dims, preferred_element_type=f32, precision=precision)


def _sigmoid(x):
    return 1.0 / (1.0 + jnp.exp(-x))


def _row_ids(tile_index, tm):
    return tile_index * tm + lax.broadcasted_iota(jnp.int32, (tm, 1), 0)


def _ordered(body, in_specs, args, after, lead=0):
    if after is None:
        return body, in_specs, args
    pos = lead + len(args)

    def body_without(*refs):
        return body(*refs[:pos], *refs[pos + 1:])

    return body_without, list(in_specs) + [ANY], list(args) + [after]


def _cast_unit(w, unit, name, after=None):
    _, R, C = w.shape
    tr, tc = _tile2(R, C, 256, 16)

    def body(w_ref, o_ref):
        o_ref[...] = w_ref[0].astype(bf16)

    body, in_specs, args = _ordered(body, [pl.BlockSpec((1, tr, tc), lambda i, j: (unit, i, j))], [w], after)
    return pl.pallas_call(
        body, name=name, grid=(R // tr, C // tc),
        in_specs=in_specs, out_specs=pl.BlockSpec((tr, tc), lambda i, j: (i, j)),
        out_shape=jax.ShapeDtypeStruct((R, C), bf16),
        compiler_params=_cparams(("parallel", "parallel")),
    )(*args)


def _rms_fwd(xs, g, out_dtype, name, after=None):
    Lp, D = xs.shape
    tm = _tile(Lp, 528)

    def body(x_ref, g_ref, h_ref):
        x = x_ref[...]
        rstd = lax.rsqrt(jnp.mean(x * x, axis=-1, keepdims=True) + EPS)
        h_ref[...] = (x * rstd * g_ref[...]).astype(out_dtype)

    in_specs = [pl.BlockSpec((tm, D), lambda i: (i, 0)), pl.BlockSpec((1, D), lambda i: (0, 0))]
    body, in_specs, args = _ordered(body, in_specs, [xs, g], after)
    return pl.pallas_call(
        body, name=name, grid=(Lp // tm,),
        in_specs=in_specs,
        out_specs=pl.BlockSpec((tm, D), lambda i: (i, 0)),
        out_shape=jax.ShapeDtypeStruct((Lp, D), out_dtype),
        compiler_params=_cparams(("parallel",)),
    )(*args)


def _rms_bwd(dY, dh, xs, g, pad, name):
    Lp, D = xs.shape
    tm = _tile(Lp, 352)

    def body(dY_ref, dh_ref, x_ref, g_ref, dxs_ref, dg_ref, half_ref):
        i = pl.program_id(0)

        @pl.when(i == 0)
        def _():
            dg_ref[...] = jnp.zeros_like(dg_ref)

        x = x_ref[...]
        rstd = lax.rsqrt(jnp.mean(x * x, axis=-1, keepdims=True) + EPS)
        xhat = x * rstd
        dh_ = dh_ref[...]
        dg_ref[...] += jnp.sum(dh_ * xhat, axis=0, keepdims=True)
        dxh = dh_ * g_ref[...]
        dx = rstd * (dxh - xhat * jnp.mean(dxh * xhat, axis=-1, keepdims=True))
        out = jnp.where(_row_ids(i, tm) >= pad, dY_ref[...] + dx, 0.0)
        dxs_ref[...] = out
        half_ref[...] = (0.5 * out).astype(bf16)

    row = pl.BlockSpec((tm, D), lambda i: (i, 0))
    vec = pl.BlockSpec((1, D), lambda i: (0, 0))
    return pl.pallas_call(
        body, name=name, grid=(Lp // tm,),
        in_specs=[row, row, row, vec], out_specs=[row, vec, row],
        out_shape=[jax.ShapeDtypeStruct((Lp, D), f32), jax.ShapeDtypeStruct((1, D), f32), jax.ShapeDtypeStruct((Lp, D), bf16)],
        compiler_params=_cparams(("arbitrary",)),
    )(dY, dh, xs, g)


def _mm(a, b, mode, out_dtype, name, tm=512, tn=512, tk=512, residual=None, after=None):
    if mode == "nn":
        (M, K), N = a.shape, b.shape[1]
    elif mode == "nt":
        (M, K), N = a.shape, b.shape[0]
    else:
        (K, M), N = a.shape, b.shape[1]
    tm = _tile(M, tm, 16 if mode != "tn" else LANES) if M > tm else M
    tn = _tile(N, tn, LANES) if N > tn else N
    tk = _tile(K, tk, LANES if mode != "tn" else 16) if K > tk else K
    nk = K // tk
    dims = {"nn": NN, "nt": NT, "tn": TN}[mode]

    def body(*refs):
        if residual is None:
            a_ref, b_ref, o_ref, acc = refs
            r_ref = None
        else:
            a_ref, b_ref, r_ref, o_ref, acc = refs
        k = pl.program_id(2)

        @pl.when(k == 0)
        def _():
            acc[...] = jnp.zeros_like(acc)

        acc[...] += _dot(a_ref[...], b_ref[...], dims)

        @pl.when(k == nk - 1)
        def _():
            r = acc[...]
            if r_ref is not None:
                r = r + r_ref[...]
            o_ref[...] = r.astype(out_dtype)

    a_spec = pl.BlockSpec((tk, tm), lambda i, j, k: (k, i)) if mode == "tn" else pl.BlockSpec((tm, tk), lambda i, j, k: (i, k))
    b_spec = pl.BlockSpec((tn, tk), lambda i, j, k: (j, k)) if mode == "nt" else pl.BlockSpec((tk, tn), lambda i, j, k: (k, j))
    o_spec = pl.BlockSpec((tm, tn), lambda i, j, k: (i, j))
    in_specs = [a_spec, b_spec] + ([o_spec] if residual is not None else [])
    args = [a, b] + ([residual] if residual is not None else [])
    body, in_specs, args = _ordered(body, in_specs, args, after)
    return pl.pallas_call(
        body, name=name, grid=(M // tm, N // tn, nk),
        in_specs=in_specs, out_specs=o_spec,
        out_shape=jax.ShapeDtypeStruct((M, N), out_dtype),
        scratch_shapes=[pltpu.VMEM((tm, tn), f32)],
        compiler_params=_cparams(("parallel", "parallel", "arbitrary"), VMEM_LIMIT_MB),
    )(*args)


def _mm_tn_full(a, b, name, tm, after=None):
    K, M = a.shape
    N = b.shape[1]
    tm = _tile(M, tm, LANES)

    def body(a_ref, b_ref, o_ref):
        o_ref[...] = _dot(a_ref[...], b_ref[...], TN).astype(bf16)

    in_specs = [pl.BlockSpec((K, tm), lambda i: (0, i)), pl.BlockSpec((K, N), lambda i: (0, 0), pipeline_mode=pl.Buffered(1))]
    body, in_specs, args = _ordered(body, in_specs, [a, b], after)
    return pl.pallas_call(
        body, name=name, grid=(M // tm,),
        in_specs=in_specs, out_specs=pl.BlockSpec((tm, N), lambda i: (i, 0)),
        out_shape=jax.ShapeDtypeStruct((M, N), bf16),
        compiler_params=_cparams(("parallel",), VMEM_LIMIT_MB),
    )(*args)


def _ffn_fwd(xs, g, wg, wu, wd, name, after=None):
    Lp, D = xs.shape
    nd, Fs, _ = wg.shape
    tm = _tile(Lp, 704)
    once = pl.Buffered(1)

    def body(x_ref, g_ref, wg_ref, wu_ref, wd_ref, out_ref, h_ref, G_ref, U_ref, hs, acc):
        j = pl.program_id(1)

        @pl.when(j == 0)
        def _():
            x = x_ref[...]
            rstd = lax.rsqrt(jnp.mean(x * x, axis=-1, keepdims=True) + EPS)
            h = (x * rstd * g_ref[...]).astype(bf16)
            hs[...] = h
            h_ref[...] = h
            acc[...] = jnp.zeros_like(acc)

        h = hs[...]
        G = _dot(h, wg_ref[0], NT)
        U = _dot(h, wu_ref[0], NT)
        G_ref[0] = G.astype(bf16)
        U_ref[0] = U.astype(bf16)
        A = (G * _sigmoid(G) * U).astype(bf16)
        acc[...] += _dot(A, wd_ref[0])

        @pl.when(j == nd - 1)
        def _():
            out_ref[...] = x_ref[...] + 0.5 * acc[...]

    row_f = pl.BlockSpec((tm, D), lambda i, j: (i, 0), pipeline_mode=once)
    act = pl.BlockSpec((1, tm, Fs), lambda i, j: (j, i, 0))
    wrow = pl.BlockSpec((1, Fs, D), lambda i, j: (j, 0, 0))
    in_specs = [row_f, pl.BlockSpec((1, D), lambda i, j: (0, 0)), wrow, wrow, wrow]
    body, in_specs, args = _ordered(body, in_specs, [xs, g, wg, wu, wd], after)
    return pl.pallas_call(
        body, name=name, grid=(Lp // tm, nd),
        in_specs=in_specs,
        out_specs=[row_f, pl.BlockSpec((tm, D), lambda i, j: (i, 0), pipeline_mode=once), act, act],
        out_shape=[jax.ShapeDtypeStruct((Lp, D), f32), jax.ShapeDtypeStruct((Lp, D), bf16),
                   jax.ShapeDtypeStruct((nd, Lp, Fs), bf16), jax.ShapeDtypeStruct((nd, Lp, Fs), bf16)],
        scratch_shapes=[pltpu.VMEM((tm, D), bf16), pltpu.VMEM((tm, D), f32)],
        compiler_params=_cparams(("parallel", "arbitrary"), VMEM_LIMIT_MB),
    )(*args)


def _ffn_bwd_act(dyh, wd, G, U, name, after=None):
    Lp, D = dyh.shape
    nd, Fs, _ = wd.shape
    tm = _tile(Lp, 704)

    def body(dyh_ref, wd_ref, G_ref, U_ref, dG_ref, dU_ref, A_ref):
        dA = _dot(dyh_ref[...], wd_ref[0], NT)
        Gf = G_ref[0].astype(f32)
        Uf = U_ref[0].astype(f32)
        s = _sigmoid(Gf)
        silu = Gf * s
        dG_ref[0] = (dA * Uf * (s * (1.0 + Gf * (1.0 - s)))).astype(bf16)
        dU_ref[0] = (dA * silu).astype(bf16)
        A_ref[0] = (silu * Uf).astype(bf16)

    act = pl.BlockSpec((1, tm, Fs), lambda j, i: (j, i, 0))
    act_s = jax.ShapeDtypeStruct((nd, Lp, Fs), bf16)
    in_specs = [pl.BlockSpec((tm, D), lambda j, i: (i, 0)), pl.BlockSpec((1, Fs, D), lambda j, i: (j, 0, 0)), act, act]
    body, in_specs, args = _ordered(body, in_specs, [dyh, wd, G, U], after)
    return pl.pallas_call(
        body, name=name, grid=(nd, Lp // tm),
        in_specs=in_specs, out_specs=[act, act, act], out_shape=[act_s, act_s, act_s],
        compiler_params=_cparams(("parallel", "parallel"), VMEM_LIMIT_MB),
    )(*args)


def _ffn_bwd_dh(dG, dU, wg, wu, name, after=None):
    nd, Lp, Fs = dG.shape
    D = wg.shape[2]
    tm = _tile(Lp, 1056)

    def body(dG_ref, dU_ref, wg_ref, wu_ref, dh_ref, acc):
        j = pl.program_id(1)

        @pl.when(j == 0)
        def _():
            acc[...] = jnp.zeros_like(acc)

        acc[...] += _dot(dG_ref[0], wg_ref[0]) + _dot(dU_ref[0], wu_ref[0])

        @pl.when(j == nd - 1)
        def _():
            dh_ref[...] = acc[...]

    act = pl.BlockSpec((1, tm, Fs), lambda i, j: (j, i, 0))
    wrow = pl.BlockSpec((1, Fs, D), lambda i, j: (j, 0, 0))
    body, in_specs, args = _ordered(body, [act, act, wrow, wrow], [dG, dU, wg, wu], after)
    return pl.pallas_call(
        body, name=name, grid=(Lp // tm, nd),
        in_specs=in_specs,
        out_specs=pl.BlockSpec((tm, D), lambda i, j: (i, 0), pipeline_mode=pl.Buffered(1)),
        out_shape=jax.ShapeDtypeStruct((Lp, D), f32),
        scratch_shapes=[pltpu.VMEM((tm, D), f32)],
        compiler_params=_cparams(("parallel", "arbitrary"), VMEM_LIMIT_MB),
    )(*args)


def _ffn_bwd_wgrad(act, rows, name, after=None):
    nd, Lp, Fs = act.shape
    D = rows.shape[1]

    def body(a_ref, r_ref, o_ref):
        o_ref[0] = _dot(a_ref[0], r_ref[...], TN).astype(bf16)

    in_specs = [pl.BlockSpec((1, Lp, Fs), lambda j: (j, 0, 0)),
                pl.BlockSpec((Lp, D), lambda j: (0, 0), pipeline_mode=pl.Buffered(1))]
    body, in_specs, args = _ordered(body, in_specs, [act, rows], after)
    return pl.pallas_call(
        body, name=name, grid=(nd,),
        in_specs=in_specs, out_specs=pl.BlockSpec((1, Fs, D), lambda j: (j, 0, 0)),
        out_shape=jax.ShapeDtypeStruct((nd, Fs, D), bf16),
        compiler_params=_cparams(("parallel",), VMEM_LIMIT_MB),
    )(*args)


def _gate_fwd(proj, wlr, blr, pad, gate_blk, name):
    Lp = proj.shape[0]
    DK = wlr.shape[1]
    tm = _tile(Lp, 528)

    def body(lr_ref, w_ref, b_ref, lg_ref):
        z = _dot(lr_ref[...].astype(bf16), w_ref[...].astype(bf16)) + b_ref[...]
        ls = jnp.minimum(z, 0.0) - jnp.log(1.0 + jnp.exp(-jnp.abs(z)))
        lg_ref[...] = jnp.where(_row_ids(pl.program_id(0), tm) >= pad, ls * (1.0 / GATE_NORM), 0.0)

    return pl.pallas_call(
        body, name=name, grid=(Lp // tm,),
        in_specs=[pl.BlockSpec((tm, GATE_PAD), lambda i: (i, gate_blk)),
                  pl.BlockSpec((GATE_PAD, DK), lambda i: (0, 0)), pl.BlockSpec((1, DK), lambda i: (0, 0))],
        out_specs=pl.BlockSpec((tm, DK), lambda i: (i, 0)),
        out_shape=jax.ShapeDtypeStruct((Lp, DK), f32),
        compiler_params=_cparams(("parallel",)),
    )(proj, wlr, blr)


def _gate_bwd(dlg, proj, wlr, blr, pad, gate_blk, name):
    Lp = proj.shape[0]
    DK = wlr.shape[1]
    tm = _tile(Lp, 528)

    def body(dlg_ref, lr_ref, w_ref, b_ref, dlr_ref, dw_ref, db_ref):
        i = pl.program_id(0)

        @pl.when(i == 0)
        def _():
            dw_ref[...] = jnp.zeros_like(dw_ref)
            db_ref[...] = jnp.zeros_like(db_ref)

        lr = lr_ref[...].astype(bf16)
        w = w_ref[...].astype(bf16)
        z = _dot(lr, w) + b_ref[...]
        dz = jnp.where(_row_ids(i, tm) >= pad, dlg_ref[...] * _sigmoid(-z) * (1.0 / GATE_NORM), 0.0)
        dzb = dz.astype(bf16)
        dlr_ref[...] = _dot(dzb, w, NT).astype(bf16)
        dw_ref[...] += _dot(lr, dzb, TN)
        db_ref[...] += jnp.sum(dz, axis=0, keepdims=True)

    return pl.pallas_call(
        body, name=name, grid=(Lp // tm,),
        in_specs=[pl.BlockSpec((tm, DK), lambda i: (i, 0)), pl.BlockSpec((tm, GATE_PAD), lambda i: (i, gate_blk)),
                  pl.BlockSpec((GATE_PAD, DK), lambda i: (0, 0)), pl.BlockSpec((1, DK), lambda i: (0, 0))],
        out_specs=[pl.BlockSpec((tm, GATE_PAD), lambda i: (i, 0)), pl.BlockSpec((GATE_PAD, DK), lambda i: (0, 0)),
                   pl.BlockSpec((1, DK), lambda i: (0, 0))],
        out_shape=[jax.ShapeDtypeStruct((Lp, GATE_PAD), bf16), jax.ShapeDtypeStruct((GATE_PAD, DK), f32),
                   jax.ShapeDtypeStruct((1, DK), f32)],
        compiler_params=_cparams(("arbitrary",)),
    )(dlg, proj, wlr, blr)


def _chunk_decay(lg):
    C = lg.shape[0]
    r = lax.broadcasted_iota(jnp.int32, (C, C), 0)
    c = lax.broadcasted_iota(jnp.int32, (C, C), 1)
    return _dot(jnp.where(r >= c, 1.0, 0.0).astype(f32), lg, NN, HI)


def _col(v):
    return jnp.transpose(jnp.broadcast_to(v, (8, v.shape[1])))[:, 0:1]


def _intra_scores(q, k, b, A_ref):
    C = q.shape[0]
    S = GLA_SUB
    A_ref[...] = jnp.zeros_like(A_ref)
    ri = lax.broadcasted_iota(jnp.int32, (S, 1), 0)
    for I in range(C // S):
        lo = S * I
        qI, bI = q[lo:lo + S], b[lo:lo + S]
        if I > 0:
            bref = b[lo - 1:lo]
            qs = qI * jnp.exp(bI - bref)
            ks = k[:lo] * jnp.exp(bref - b[:lo])
            A_ref[lo:lo + S, 0:lo] = _dot(qs, ks, NT, HI)
        for jj in range(S):
            j = lo + jj
            P = jnp.exp(jnp.minimum(bI - b[j:j + 1], 0.0))
            a = jnp.sum(qI * P * k[j:j + 1], axis=1, keepdims=True)
            A_ref[lo:lo + S, j:j + 1] = jnp.where(ri >= jj, a, 0.0)


def _intra_grads(q, k, b, dA, dq_ref, dk_ref):
    C = q.shape[0]
    S = GLA_SUB
    ri = lax.broadcasted_iota(jnp.int32, (S, 1), 0)
    for I in range(C // S):
        lo = S * I
        qI, bI = q[lo:lo + S], b[lo:lo + S]
        dqI = jnp.zeros_like(qI)
        if I > 0:
            bref = b[lo - 1:lo]
            eq = jnp.exp(bI - bref)
            ek = jnp.exp(bref - b[:lo])
            qs = qI * eq
            ks = k[:lo] * ek
            dAI = dA[lo:lo + S, 0:lo]
            dqI = dqI + _dot(dAI, ks, NN, HI) * eq
            dk_ref[0:lo, :] += _dot(dAI, qs, TN, HI) * ek
        for jj in range(S):
            j = lo + jj
            P = jnp.exp(jnp.minimum(bI - b[j:j + 1], 0.0))
            t = jnp.where(ri >= jj, dA[lo:lo + S, j:j + 1], 0.0) * P
            dqI = dqI + t * k[j:j + 1]
            dk_ref[j:j + 1, :] += jnp.sum(t * qI, axis=0, keepdims=True)
        dq_ref[lo:lo + S, :] += dqI


def _gla_fwd(proj, lg, hnw, H, name):
    Lp = proj.shape[0]
    DK = lg.shape[1]
    hk = DK // H
    hv = hnw.shape[1]
    DV = hv * H
    C = GLA_CHUNK
    NC = Lp // C
    HS = min(GLA_HEADS_PER_STEP, H)
    G = H // HS
    scale = float(hk) ** -0.5
    kq, kv, kr = G, (2 * DK) // (HS * hv), (2 * DK) // (HS * hv) + G

    def body(q_ref, k_ref, v_ref, r_ref, lg_ref, w_ref, o_ref, y_ref, s_ref, S_scr, A_scr):
        c = pl.program_id(1)

        @pl.when(c == 0)
        def _():
            S_scr[...] = jnp.zeros_like(S_scr)

        for hh in range(HS):
            ck, cv = slice(hh * hk, (hh + 1) * hk), slice(hh * hv, (hh + 1) * hv)
            q = q_ref[:, ck] * scale
            k = k_ref[:, ck]
            v = v_ref[:, cv]
            b = _chunk_decay(lg_ref[:, ck])
            bl = b[C - 1:C]
            S = S_scr[hh]
            s_ref[hh, 0] = S
            _intra_scores(q, k, b, A_scr.at[hh])
            vb = v.astype(bf16)
            o = _dot((q * jnp.exp(b)).astype(bf16), S.astype(bf16)) + _dot(A_scr[hh].astype(bf16), vb)
            kb = (k * jnp.exp(bl - b)).astype(bf16)
            S_scr[hh] = jnp.exp(_col(bl)) * S + _dot(kb, vb, TN)
            o_ref[:, cv] = o
            on = o * lax.rsqrt(jnp.mean(o * o, axis=-1, keepdims=True) + EPS) * w_ref[...]
            r = r_ref[:, cv]
            y_ref[:, cv] = (on * (r * _sigmoid(r))).astype(bf16)

    return pl.pallas_call(
        body, name=name, grid=(G, NC),
        in_specs=[pl.BlockSpec((C, HS * hk), lambda g, c: (c, g)),
                  pl.BlockSpec((C, HS * hk), lambda g, c: (c, kq + g)),
                  pl.BlockSpec((C, HS * hv), lambda g, c: (c, kv + g)),
                  pl.BlockSpec((C, HS * hv), lambda g, c: (c, kr + g)),
                  pl.BlockSpec((C, HS * hk), lambda g, c: (c, g)),
                  pl.BlockSpec((1, hv), lambda g, c: (0, 0))],
        out_specs=[pl.BlockSpec((C, HS * hv), lambda g, c: (c, g)), pl.BlockSpec((C, HS * hv), lambda g, c: (c, g)),
                   pl.BlockSpec((HS, 1, hk, hv), lambda g, c: (g, c, 0, 0))],
        out_shape=[jax.ShapeDtypeStruct((Lp, DV), f32), jax.ShapeDtypeStruct((Lp, DV), bf16),
                   jax.ShapeDtypeStruct((H, NC, hk, hv), f32)],
        scratch_shapes=[pltpu.VMEM((HS, hk, hv), f32), pltpu.VMEM((HS, C, C), f32)],
        compiler_params=_cparams(("parallel", "arbitrary")),
    )(proj, proj, proj, proj, lg, hnw)


def _gla_bwd(dy, proj, lg, o, states, hnw, H, pad, name, after=None):
    Lp = proj.shape[0]
    DK = lg.shape[1]
    hk = DK // H
    hv = hnw.shape[1]
    DV = hv * H
    C = GLA_CHUNK
    NC = Lp // C
    HS = min(GLA_HEADS_PER_STEP, H)
    G = H // HS
    scale = float(hk) ** -0.5
    kq, kv, kr = G, (2 * DK) // (HS * hv), (2 * DK) // (HS * hv) + G

    def body(dy_ref, q_ref, k_ref, v_ref, r_ref, lg_ref, o_ref, s_ref, sn_ref, w_ref,
             dq_ref, dk_ref, dv_ref, dr_ref, dlg_ref, dw_ref, dS_scr, A_scr, dq_s, dk_s):
        g = pl.program_id(0)
        cc = pl.program_id(1)
        c = NC - 1 - cc

        @pl.when(cc == 0)
        def _():
            dS_scr[...] = jnp.zeros_like(dS_scr)

        @pl.when((cc == 0) & (g == 0))
        def _():
            dw_ref[...] = jnp.zeros_like(dw_ref)

        keep = (c * C + lax.broadcasted_iota(jnp.int32, (C, 1), 0)) >= pad
        ri = lax.broadcasted_iota(jnp.int32, (C, C), 0)
        ci = lax.broadcasted_iota(jnp.int32, (C, C), 1)
        w = w_ref[...]
        for hh in range(HS):
            ck, cv = slice(hh * hk, (hh + 1) * hk), slice(hh * hv, (hh + 1) * hv)
            o_ = o_ref[:, cv]
            rs = lax.rsqrt(jnp.mean(o_ * o_, axis=-1, keepdims=True) + EPS)
            ohat = o_ * rs
            r = r_ref[:, cv]
            sg = _sigmoid(r)
            dy_ = dy_ref[:, cv]
            d_on = dy_ * (r * sg)
            dr_ref[:, cv] = jnp.where(keep, dy_ * (ohat * w) * (sg * (1.0 + r * (1.0 - sg))), 0.0).astype(bf16)
            dw_ref[...] += jnp.sum(d_on * ohat, axis=0, keepdims=True)
            d_oh = d_on * w
            do = rs * (d_oh - ohat * jnp.mean(d_oh * ohat, axis=-1, keepdims=True))
            dob = do.astype(bf16)
            q = q_ref[:, ck] * scale
            k = k_ref[:, ck]
            vb = v_ref[:, cv].astype(bf16)
            b = _chunk_decay(lg_ref[:, ck])
            bl = b[C - 1:C]
            eb = jnp.exp(b)
            ekb = jnp.exp(bl - b)
            S = s_ref[hh, 0]
            dS = dS_scr[hh]
            dSb = dS.astype(bf16)
            _intra_scores(q, k, b, A_scr.at[hh])
            dA = jnp.where(ri >= ci, _dot(dob, vb, NT), 0.0)
            kb = (k * ekb).astype(bf16)
            qb = (q * eb).astype(bf16)
            dv = _dot(A_scr[hh].astype(bf16), dob, TN) + _dot(kb, dSb)
            dq_s[hh] = _dot(dob, S.astype(bf16), NT) * eb
            dk_s[hh] = _dot(vb, dSb, NT) * ekb
            dS_scr[hh] = _dot(qb, dob, TN) + jnp.exp(_col(bl)) * dS
            _intra_grads(q, k, b, dA, dq_s.at[hh], dk_s.at[hh])
            dq = dq_s[hh]
            dk = dk_s[hh]
            Dm = q * dq - k * dk
            after_rows = _dot(jnp.ones((8, hv), f32), sn_ref[hh, 0] * dS, NT, HI)[0:1]
            dlg = _dot(jnp.where(ri <= ci, 1.0, 0.0).astype(f32), Dm, NN, HI) + after_rows
            dlg_ref[:, ck] = jnp.where(keep, dlg, 0.0)
            dq_ref[:, ck] = jnp.where(keep, dq * scale, 0.0).astype(bf16)
            dk_ref[:, ck] = jnp.where(keep, dk, 0.0).astype(bf16)
            dv_ref[:, cv] = jnp.where(keep, dv, 0.0).astype(bf16)

    rev = lambda cc: NC - 1 - cc
    bk = lambda off: pl.BlockSpec((C, HS * hk), lambda g, cc: (rev(cc), off + g))
    bv = lambda off: pl.BlockSpec((C, HS * hv), lambda g, cc: (rev(cc), off + g))
    in_specs = [bv(0), bk(0), bk(kq), bv(kv), bv(kr), bk(0), bv(0),
                pl.BlockSpec((HS, 1, hk, hv), lambda g, cc: (g, rev(cc), 0, 0)),
                pl.BlockSpec((HS, 1, hk, hv), lambda g, cc: (g, jnp.minimum(rev(cc) + 1, NC - 1), 0, 0)),
                pl.BlockSpec((1, hv), lambda g, cc: (0, 0))]
    body, in_specs, args = _ordered(body, in_specs, [dy, proj, proj, proj, proj, lg, o, states, states, hnw], after)
    return pl.pallas_call(
        body, name=name, grid=(G, NC),
        in_specs=in_specs,
        out_specs=[bk(0), bk(0), bv(0), bv(0), bk(0), pl.BlockSpec((1, hv), lambda g, cc: (0, 0))],
        out_shape=[jax.ShapeDtypeStruct((Lp, DK), bf16), jax.ShapeDtypeStruct((Lp, DK), bf16),
                   jax.ShapeDtypeStruct((Lp, DV), bf16), jax.ShapeDtypeStruct((Lp, DV), bf16),
                   jax.ShapeDtypeStruct((Lp, DK), f32), jax.ShapeDtypeStruct((1, hv), f32)],
        scratch_shapes=[pltpu.VMEM((HS, hk, hv), f32), pltpu.VMEM((HS, C, C), f32),
                        pltpu.VMEM((HS, C, hk), f32), pltpu.VMEM((HS, C, hk), f32)],
        compiler_params=_cparams(("arbitrary", "arbitrary")),
    )(*args)


def _window_sums(x, back):
    n = x.shape[0]
    out = []
    s = x
    for w in (1, 2, 4, 8):
        s = s + pltpu.roll(s, w if back else n - w, 0)
        out.append(s)
    return out


def _pool_windows(hn, pad, name):
    Lp, D = hn.shape
    GW = D // POOL_GROUPS
    cb = min(GW, 256)
    per = GW // cb

    def body(h_ref, p_ref):
        g = pl.program_id(0) // per
        x = h_ref[...]
        s2, s4, s8, s16 = _window_sums(x, True)
        sel = jnp.where(g == 0, s2, jnp.where(g == 1, s4, jnp.where(g == 2, s8, s16)))
        win = jnp.left_shift(2, g).astype(f32)
        rows = lax.broadcasted_iota(jnp.int32, (Lp, 1), 0)
        t = (rows - pad).astype(f32)
        cnt = jnp.minimum(jnp.maximum(t, 0.0) + 1.0, win)
        p_ref[...] = jnp.where(rows >= pad, sel / cnt - x, 0.0).astype(bf16)

    return pl.pallas_call(
        body, name=name, grid=(D // cb,),
        in_specs=[pl.BlockSpec((Lp, cb), lambda i: (0, i))],
        out_specs=pl.BlockSpec((Lp, cb), lambda i: (0, i)),
        out_shape=jax.ShapeDtypeStruct((Lp, D), bf16),
        compiler_params=_cparams(("parallel",)),
    )(hn)


def _pool_windows_bwd(dp, pad, name):
    Lp, D = dp.shape
    GW = D // POOL_GROUPS
    cb = min(GW, 256)
    per = GW // cb

    def body(dp_ref, dh_ref):
        g = pl.program_id(0) // per
        rows = lax.broadcasted_iota(jnp.int32, (Lp, 1), 0)
        d = jnp.where(rows >= pad, dp_ref[...], 0.0)
        win = jnp.left_shift(2, g).astype(f32)
        t = (rows - pad).astype(f32)
        cnt = jnp.minimum(jnp.maximum(t, 0.0) + 1.0, win)
        s2, s4, s8, s16 = _window_sums(d / cnt, False)
        sel = jnp.where(g == 0, s2, jnp.where(g == 1, s4, jnp.where(g == 2, s8, s16)))
        dh_ref[...] = jnp.where(rows >= pad, sel - d, 0.0)

    return pl.pallas_call(
        body, name=name, grid=(D // cb,),
        in_specs=[pl.BlockSpec((Lp, cb), lambda i: (0, i))],
        out_specs=pl.BlockSpec((Lp, cb), lambda i: (0, i)),
        out_shape=jax.ShapeDtypeStruct((Lp, D), f32),
        compiler_params=_cparams(("parallel",)),
    )(dp)


def _pool_mix_fwd(xs, pooled, w, bias, scale, pad, name):
    Lp, D = xs.shape
    GW = D // POOL_GROUPS
    tm = _tile(Lp, 1056)

    def body(x_ref, p_ref, w_ref, b_ref, s_ref, o_ref):
        z = _dot(p_ref[...], w_ref[0]) + b_ref[...]
        keep = _row_ids(pl.program_id(1), tm) >= pad
        o_ref[...] = x_ref[...] + jnp.where(keep, z * s_ref[...], 0.0)

    blk = pl.BlockSpec((tm, GW), lambda g, i: (i, g))
    vec = pl.BlockSpec((1, GW), lambda g, i: (0, g))
    return pl.pallas_call(
        body, name=name, grid=(POOL_GROUPS, Lp // tm),
        in_specs=[blk, blk, pl.BlockSpec((1, GW, GW), lambda g, i: (g, 0, 0)), vec, vec],
        out_specs=blk, out_shape=jax.ShapeDtypeStruct((Lp, D), f32),
        compiler_params=_cparams(("parallel", "parallel")),
    )(xs, pooled, w, bias, scale)


def _pool_mix_bwd(dY, pooled, w, bias, scale, pad, name, after=None):
    Lp, D = dY.shape
    GW = D // POOL_GROUPS
    tm = _tile(Lp, 1056)
    nm = Lp // tm

    def body(dY_ref, p_ref, w_ref, b_ref, s_ref, dp_ref, dw_ref, db_ref, ds_ref, acc):
        i = pl.program_id(1)

        @pl.when(i == 0)
        def _():
            acc[...] = jnp.zeros_like(acc)
            db_ref[...] = jnp.zeros_like(db_ref)
            ds_ref[...] = jnp.zeros_like(ds_ref)

        keep = _row_ids(i, tm) >= pad
        dY_ = jnp.where(keep, dY_ref[...], 0.0)
        p = p_ref[...]
        z = _dot(p, w_ref[0]) + b_ref[...]
        ds_ref[...] += jnp.sum(dY_ * z, axis=0, keepdims=True)
        dz = dY_ * s_ref[...]
        db_ref[...] += jnp.sum(dz, axis=0, keepdims=True)
        dzb = dz.astype(bf16)
        acc[...] += _dot(p, dzb, TN)
        dp_ref[...] = _dot(dzb, w_ref[0], NT)

        @pl.when(i == nm - 1)
        def _():
            dw_ref[0] = acc[...].astype(bf16)

    blk = pl.BlockSpec((tm, GW), lambda g, i: (i, g))
    vec = pl.BlockSpec((1, GW), lambda g, i: (0, g))
    wsp = pl.BlockSpec((1, GW, GW), lambda g, i: (g, 0, 0))
    body, in_specs, args = _ordered(body, [blk, blk, wsp, vec, vec], [dY, pooled, w, bias, scale], after)
    return pl.pallas_call(
        body, name=name, grid=(POOL_GROUPS, nm),
        in_specs=in_specs, out_specs=[blk, wsp, vec, vec],
        out_shape=[jax.ShapeDtypeStruct((Lp, D), f32), jax.ShapeDtypeStruct((POOL_GROUPS, GW, GW), bf16),
                   jax.ShapeDtypeStruct((1, D), f32), jax.ShapeDtypeStruct((1, D), f32)],
        scratch_shapes=[pltpu.VMEM((GW, GW), f32)],
        compiler_params=_cparams(("parallel", "arbitrary")),
    )(*args)


def _loss_head(xs, target, g, first, name):
    Lp, D = xs.shape
    tm = GLA_CHUNK
    off = first // tm

    def body(x_ref, t_ref, g_ref, loss_ref, dxs_ref, dg_ref, half_ref):
        i = pl.program_id(0)

        @pl.when(i == 0)
        def _():
            loss_ref[...] = jnp.zeros_like(loss_ref)
            dg_ref[...] = jnp.zeros_like(dg_ref)

        @pl.when(i < off)
        def _():
            dxs_ref[...] = jnp.zeros_like(dxs_ref)
            half_ref[...] = jnp.zeros_like(half_ref)

        @pl.when(i >= off)
        def _():
            x = x_ref[...]
            rstd = lax.rsqrt(jnp.mean(x * x, axis=-1, keepdims=True) + EPS)
            xhat = x * rstd
            gg = g_ref[...]
            err = xhat * gg - t_ref[...]
            loss_ref[...] += 0.5 * jnp.sum(jnp.mean(err * err, axis=-1, keepdims=True))
            dy = err * (1.0 / D)
            dg_ref[...] += jnp.sum(dy * xhat, axis=0, keepdims=True)
            dxh = dy * gg
            out = rstd * (dxh - xhat * jnp.mean(dxh * xhat, axis=-1, keepdims=True))
            dxs_ref[...] = out
            half_ref[...] = (0.5 * out).astype(bf16)

    row = pl.BlockSpec((tm, D), lambda i: (i, 0))
    return pl.pallas_call(
        body, name=name, grid=(Lp // tm,),
        in_specs=[row, pl.BlockSpec((tm, D), lambda i: (jnp.maximum(i - off, 0), 0)), pl.BlockSpec((1, D), lambda i: (0, 0))],
        out_specs=[pl.BlockSpec((8, LANES), lambda i: (0, 0)), row, pl.BlockSpec((1, D), lambda i: (0, 0)), row],
        out_shape=[jax.ShapeDtypeStruct((8, LANES), f32), jax.ShapeDtypeStruct((Lp, D), f32),
                   jax.ShapeDtypeStruct((1, D), f32), jax.ShapeDtypeStruct((Lp, D), bf16)],
        compiler_params=_cparams(("arbitrary",)),
    )(xs, target, g)


def _adam_math(w, g, m, v):
    m2 = ADAM_B1 * m + (1.0 - ADAM_B1) * g
    v2 = ADAM_B2 * v + (1.0 - ADAM_B2) * (g * g)
    m_hat = m2 / (1.0 - ADAM_B1 ** ADAM_STEP)
    v_hat = v2 / (1.0 - ADAM_B2 ** ADAM_STEP)
    delta = -ADAM_LR * (m_hat / (jnp.sqrt(v_hat) + ADAM_EPS) + ADAM_WD * w)
    return delta, m2, v2


def _adamw(w, m, v, unit, own, own_idx, recv, prev, name, after=None):
    U, R, C = w.shape
    tr, tc = _tile2(R, C, 256, 8 if own.dtype == f32 and recv is None else 16)
    n_recv = 0 if recv is None else recv.shape[0]

    def body(idx_ref, w_ref, m_ref, v_ref, own_ref, *rest):
        rest = list(rest)
        recv_refs = [rest.pop(0) for _ in range(n_recv)]
        if prev is not None:
            rest = rest[4:]
        g_ref, d_ref, m2_ref, v2_ref = rest
        g = own_ref[0].astype(f32)
        for r_ref in recv_refs:
            g = g + r_ref[0].astype(f32)
        delta, m2, v2 = _adam_math(w_ref[0], g, m_ref[0], v_ref[0])
        g_ref[0] = g
        d_ref[0] = delta
        m2_ref[0] = m2
        v2_ref[0] = v2

    blk = pl.BlockSpec((1, tr, tc), lambda i, j, idx: (unit, i, j))
    in_specs = [blk, blk, blk, pl.BlockSpec((1, tr, tc), lambda i, j, idx: (idx[0], i, j))]
    args = [w, m, v, own]
    for p in range(n_recv):
        in_specs.append(pl.BlockSpec((1, tr, tc), lambda i, j, idx, p=p: (p, i, j)))
        args.append(recv)
    aliases = {}
    if prev is not None:
        for t in range(4):
            aliases[1 + len(args) + t] = t
        in_specs += [ANY] * 4
        args += list(prev)
    body, in_specs, args = _ordered(body, in_specs, args, after, lead=1)
    out = jax.ShapeDtypeStruct((U, R, C), f32)
    return pl.pallas_call(
        body, name=name,
        grid_spec=pltpu.PrefetchScalarGridSpec(
            num_scalar_prefetch=1, grid=(R // tr, C // tc), in_specs=in_specs, out_specs=[blk] * 4),
        out_shape=[out] * 4, input_output_aliases=aliases,
        compiler_params=_cparams(("parallel", "parallel")),
    )(own_idx, *args)


def _place():
    return lax.axis_index("x"), lax.axis_index("y"), lax.axis_index("c")


HBM = pl.BlockSpec(memory_space=pltpu.HBM)
SEM = pl.BlockSpec(memory_space=pltpu.SEMAPHORE)
VMEM_SPEC = pl.BlockSpec(memory_space=pltpu.VMEM)
EFFECT = pltpu.SideEffectType.DATAFLOW_SIDE_EFFECTING
TOKEN = jax.ShapeDtypeStruct((8, LANES), f32)


def _hbm(x):
    return pltpu.with_memory_space_constraint(x, pltpu.HBM)


def _hbm_like(xs):
    return [pltpu.HBM(x.shape, x.dtype) for x in xs]


def _slot(px, py, pc):
    return 4 * px + 2 * py + pc


def _halves(ref):
    n = ref.shape[0]
    cut = n // 2 if n < 32 else (n // 2) // 16 * 16
    return ref.at[pl.ds(0, cut)], ref.at[pl.ds(cut, n - cut)]


def _gather_start(shards, after, name):
    n = len(shards)
    me = _slot(*_place())
    bufs = [lax.dynamic_update_slice(lax.empty((N_DEV,) + s.shape, s.dtype), s[None], (me,) + (0,) * s.ndim) for s in shards]

    def body(*refs):
        ins, land = refs[:n], refs[n:2 * n]
        send, recv = refs[2 * n + 1], refs[2 * n + 2]
        token = refs[-1]
        x, y, c = _place()
        to = [(x, y, 1 - c), (1 - x, y, c), (x, 1 - y, c)]
        for a in range(n):
            for k, dev in enumerate(to):
                pltpu.make_async_remote_copy(
                    src_ref=ins[a], dst_ref=land[a].at[_slot(x, y, c)], send_sem=send.at[3 * a + k], recv_sem=recv.at[3 * a + k],
                    device_id=dev, device_id_type=MESH).start()
        token[...] = jnp.zeros_like(token)

    out = pl.pallas_call(
        body, name=name,
        in_specs=[HBM] * (2 * n) + [ANY],
        out_specs=[SEM, SEM] + [HBM] * (2 * n) + [VMEM_SPEC],
        out_shape=[pltpu.SemaphoreType.DMA((3 * n,)), pltpu.SemaphoreType.DMA((3 * n,))] + _hbm_like(shards) + _hbm_like(bufs) + [TOKEN],
        input_output_aliases={i: 2 + i for i in range(2 * n)},
        compiler_params=pltpu.CompilerParams(has_side_effects=EFFECT),
    )(*[_hbm(s) for s in shards], *[_hbm(b) for b in bufs], after)
    return dict(send1=out[0], recv1=out[1], shards=list(out[2:2 + n]), bufs=list(out[2 + n:2 + 2 * n]), token=out[-1])


def _gather_mid(h, after, name):
    n = len(h["bufs"])

    def body(*refs):
        land, recv1 = refs[:n], refs[n]
        send2, recv2 = refs[n + 2], refs[n + 3]
        token = refs[-1]
        x, y, c = _place()
        nbr = [(1 - x, y, c), (x, 1 - y, c)]
        for j, dev in enumerate(nbr):
            for a in range(n):
                blk = land[a].at[_slot(*dev)]
                pltpu.make_async_remote_copy(
                    src_ref=blk, dst_ref=blk, send_sem=send2.at[4 * a + j], recv_sem=recv1.at[3 * a + 1 + j],
                    device_id=dev, device_id_type=MESH).wait_recv()
                pltpu.make_async_remote_copy(
                    src_ref=blk, dst_ref=blk, send_sem=send2.at[4 * a + j], recv_sem=recv2.at[4 * a + j],
                    device_id=(x, y, 1 - c), device_id_type=MESH).start()
        for a in range(n):
            from_x, from_y = land[a].at[_slot(*nbr[0])], land[a].at[_slot(*nbr[1])]
            for k, (half, dev) in enumerate([(_halves(from_y)[0], nbr[0]), (_halves(from_x)[1], nbr[1])]):
                pltpu.make_async_remote_copy(
                    src_ref=half, dst_ref=half, send_sem=send2.at[4 * a + 2 + k], recv_sem=recv2.at[4 * a + 2 + k],
                    device_id=dev, device_id_type=MESH).start()
        token[...] = jnp.zeros_like(token)

    out = pl.pallas_call(
        body, name=name,
        in_specs=[HBM] * n + [SEM, ANY],
        out_specs=[SEM, SEM] + [HBM] * n + [VMEM_SPEC],
        out_shape=[pltpu.SemaphoreType.DMA((4 * n,)), pltpu.SemaphoreType.DMA((4 * n,))] + _hbm_like(h["bufs"]) + [TOKEN],
        input_output_aliases={i: 2 + i for i in range(n)},
        compiler_params=pltpu.CompilerParams(has_side_effects=EFFECT),
    )(*h["bufs"], h["recv1"], after)
    h.update(send2=out[0], recv2=out[1], bufs=list(out[2:2 + n]), token=out[-1])
    return h


def _gather_mid2(h, after, name):
    n = len(h["bufs"])

    def body(*refs):
        land, recv2 = refs[:n], refs[n]
        send3, recv3 = refs[n + 2], refs[n + 3]
        token = refs[-1]
        x, y, c = _place()
        for a in range(n):
            blk = land[a].at[_slot(1 - x, 1 - y, c)]
            for k, half in enumerate(_halves(blk)):
                pltpu.make_async_remote_copy(
                    src_ref=half, dst_ref=half, send_sem=send3.at[a], recv_sem=recv2.at[4 * a + 2 + k],
                    device_id=(x, y, 1 - c), device_id_type=MESH).wait_recv()
            pltpu.make_async_remote_copy(
                src_ref=blk, dst_ref=blk, send_sem=send3.at[a], recv_sem=recv3.at[a],
                device_id=(x, y, 1 - c), device_id_type=MESH).start()
        token[...] = jnp.zeros_like(token)

    out = pl.pallas_call(
        body, name=name,
        in_specs=[HBM] * n + [SEM, ANY],
        out_specs=[SEM, SEM] + [HBM] * n + [VMEM_SPEC],
        out_shape=[pltpu.SemaphoreType.DMA((n,)), pltpu.SemaphoreType.DMA((n,))] + _hbm_like(h["bufs"]) + [TOKEN],
        input_output_aliases={i: 2 + i for i in range(n)},
        compiler_params=pltpu.CompilerParams(has_side_effects=EFFECT),
    )(*h["bufs"], h["recv2"], after)
    h.update(send3=out[0], recv3=out[1], bufs=list(out[2:2 + n]), token=out[-1])
    return h


def _gather_end(h, after, name):
    n = len(h["bufs"])

    def body(*refs):
        ins, land = refs[:n], refs[n:2 * n]
        send1, recv1, send2, recv2, send3, recv3 = refs[2 * n:2 * n + 6]
        x, y, c = _place()
        sib = (x, y, 1 - c)
        nbr = [(1 - x, y), (x, 1 - y)]

        def wait(src, dst, ssem, rsem, send):
            cp = pltpu.make_async_remote_copy(src_ref=src, dst_ref=dst, send_sem=ssem, recv_sem=rsem, device_id=sib, device_id_type=MESH)
            cp.wait_send() if send else cp.wait_recv()

        for a in range(n):
            mine = land[a].at[_slot(x, y, c)]
            for k in range(3):
                wait(ins[a], mine, send1.at[3 * a + k], recv1.at[3 * a + k], True)
            wait(ins[a], land[a].at[_slot(x, y, 1 - c)], send1.at[3 * a], recv1.at[3 * a], False)
            for j, (px, py) in enumerate(nbr):
                sent = land[a].at[_slot(px, py, c)]
                wait(sent, sent, send2.at[4 * a + j], recv2.at[4 * a + j], True)
                wait(sent, land[a].at[_slot(px, py, 1 - c)], send2.at[4 * a + j], recv2.at[4 * a + j], False)
            halves = [_halves(land[a].at[_slot(*nbr[1], c)])[0], _halves(land[a].at[_slot(*nbr[0], c)])[1]]
            for k, half in enumerate(halves):
                wait(half, half, send2.at[4 * a + 2 + k], recv2.at[4 * a + 2 + k], True)
            diag = land[a].at[_slot(1 - x, 1 - y, c)]
            wait(diag, diag, send3.at[a], recv3.at[a], True)
            wait(diag, land[a].at[_slot(1 - x, 1 - y, 1 - c)], send3.at[a], recv3.at[a], False)

    out = pl.pallas_call(
        body, name=name,
        in_specs=[HBM] * (2 * n) + [SEM] * 6 + [ANY],
        out_specs=[HBM] * n,
        out_shape=_hbm_like(h["bufs"]),
        input_output_aliases={n + i: i for i in range(n)},
        compiler_params=pltpu.CompilerParams(has_side_effects=EFFECT),
    )(*h["shards"], *h["bufs"], h["send1"], h["recv1"], h["send2"], h["recv2"], h["send3"], h["recv3"], after)
    return list(out)


def _peer_plan(kind, x, y, c):
    if kind == "pair":
        return [(2 * q + (1 - c), q, (x, y, 1 - c)) for q in range(4)]
    chips = [(1 - x, y), (x, 1 - y), (1 - x, 1 - y)]
    return [(2 * px + py, k, (px, py, c)) for k, (px, py) in enumerate(chips)]


def _exchange_start(kind, srcs, after, name):
    n = len(srcs)
    K = 4 if kind == "pair" else 3
    lands = [_hbm(lax.empty((K,) + s.shape[1:], s.dtype)) for s in srcs]

    def body(*refs):
        ins, land = refs[:n], refs[n:2 * n]
        send, recv = refs[2 * n + 1], refs[2 * n + 2]
        token = refs[-1]
        for a in range(n):
            for k, (si, di, dev) in enumerate(_peer_plan(kind, *_place())):
                pltpu.make_async_remote_copy(
                    src_ref=ins[a].at[si], dst_ref=land[a].at[di], send_sem=send.at[K * a + k], recv_sem=recv.at[K * a + k],
                    device_id=dev, device_id_type=MESH).start()
        token[...] = jnp.zeros_like(token)

    out = pl.pallas_call(
        body, name=name,
        in_specs=[HBM] * (2 * n) + [ANY],
        out_specs=[SEM, SEM] + [HBM] * (2 * n) + [VMEM_SPEC],
        out_shape=[pltpu.SemaphoreType.DMA((K * n,)), pltpu.SemaphoreType.DMA((K * n,))] + _hbm_like(srcs) + _hbm_like(lands) + [TOKEN],
        input_output_aliases={i: 2 + i for i in range(2 * n)},
        compiler_params=pltpu.CompilerParams(has_side_effects=EFFECT),
    )(*[_hbm(s) for s in srcs], *lands, after)
    return dict(kind=kind, send=out[0], recv=out[1], srcs=list(out[2:2 + n]), lands=list(out[2 + n:2 + 2 * n]), token=out[-1])


def _exchange_wait(h, after, name):
    n = len(h["srcs"])
    kind = h["kind"]
    K = 4 if kind == "pair" else 3

    def body(*refs):
        ins, land = refs[:n], refs[n:2 * n]
        send, recv = refs[2 * n], refs[2 * n + 1]
        for a in range(n):
            for k, (si, di, dev) in enumerate(_peer_plan(kind, *_place())):
                cp = pltpu.make_async_remote_copy(
                    src_ref=ins[a].at[si], dst_ref=land[a].at[di], send_sem=send.at[K * a + k], recv_sem=recv.at[K * a + k],
                    device_id=dev, device_id_type=MESH)
                cp.wait_send()
                cp.wait_recv()

    out = pl.pallas_call(
        body, name=name,
        in_specs=[HBM] * (2 * n) + [SEM, SEM, ANY],
        out_specs=[HBM] * (2 * n),
        out_shape=_hbm_like(h["srcs"]) + _hbm_like(h["lands"]),
        input_output_aliases={i: i for i in range(2 * n)},
        compiler_params=pltpu.CompilerParams(has_side_effects=EFFECT),
    )(*h["srcs"], *h["lands"], h["send"], h["recv"], after)
    return list(out[:n]), list(out[n:])


def _pair_add(gs, gots, c_idx, name):
    n = len(gs)
    _, R, C = gs[0].shape
    tr, tc = _tile2(R, C, 512, 16)

    def body(c_ref, *refs):
        for a in range(n):
            refs[2 * n + a][0] = (refs[a][0].astype(f32) + refs[n + a][0].astype(f32)).astype(bf16)

    mine = pl.BlockSpec((1, tr, tc), lambda q, i, j, c: (2 * q + c[0], i, j))
    blk = pl.BlockSpec((1, tr, tc), lambda q, i, j, c: (q, i, j))
    return pl.pallas_call(
        body, name=name,
        grid_spec=pltpu.PrefetchScalarGridSpec(
            num_scalar_prefetch=1, grid=(4, R // tr, C // tc),
            in_specs=[mine] * n + [blk] * n, out_specs=[blk] * n),
        out_shape=[jax.ShapeDtypeStruct((4, R, C), bf16)] * n,
        compiler_params=_cparams(("parallel", "parallel", "parallel")),
    )(c_idx, *gs, *gots)


def _small_exchange(send, gather, name, after=None):
    R = send.shape[-2]

    def body(in_ref, out_ref, send_sems, recv_sems):
        x, y, c = _place()
        me = 4 * x + 2 * y + c
        out_ref[me] = in_ref[...] if gather else in_ref[me]
        cps = []
        for k in range(1, N_DEV):
            px, py, pc = x ^ ((k >> 2) & 1), y ^ ((k >> 1) & 1), c ^ (k & 1)
            src = in_ref if gather else in_ref.at[4 * px + 2 * py + pc]
            cps.append(pltpu.make_async_remote_copy(
                src_ref=src, dst_ref=out_ref.at[me],
                send_sem=send_sems.at[k - 1], recv_sem=recv_sems.at[k - 1],
                device_id=(px, py, pc), device_id_type=MESH))
        for cp in cps:
            cp.start()
        for cp in cps:
            cp.wait()

    body, in_specs, args = _ordered(body, [pl.BlockSpec(memory_space=pltpu.VMEM)], [send], after)
    return pl.pallas_call(
        body, name=name,
        in_specs=in_specs, out_specs=pl.BlockSpec(memory_space=pltpu.VMEM),
        out_shape=jax.ShapeDtypeStruct((N_DEV, R, LANES), f32),
        scratch_shapes=[pltpu.SemaphoreType.DMA((N_DEV - 1,)), pltpu.SemaphoreType.DMA((N_DEV - 1,))],
    )(*args)


def _sum_blocks(blocks, name):
    def body(in_ref, o_ref):
        s = in_ref[0]
        for d in range(1, N_DEV):
            s = s + in_ref[d]
        o_ref[0] = s

    return pl.pallas_call(body, name=name, out_shape=jax.ShapeDtypeStruct((1,) + blocks.shape[1:], f32))(blocks)


def _rows(n):
    return -(-n // LANES)


def _pack(arrs, total_rows):
    parts = []
    for a in arrs:
        flat = a.reshape(-1).astype(f32)
        parts.append(jnp.pad(flat, (0, _rows(flat.size) * LANES - flat.size)))
    flat = jnp.concatenate(parts)
    return jnp.pad(flat, (0, total_rows * LANES - flat.size)).reshape(total_rows, LANES)


def _unpack(packed, shapes):
    lead = packed.shape[:-2]
    flat = packed.reshape(lead + (-1,))
    out, pos = [], 0
    for s in shapes:
        n = 1
        for d in s:
            n *= d
        out.append(flat[..., pos:pos + n].reshape(lead + tuple(s)))
        pos += _rows(n) * LANES
    return out


def _to_shards(full, axis):
    s = full.shape
    return jnp.moveaxis(full.reshape(s[:axis] + (N_DEV, s[axis] // N_DEV) + s[axis + 1:]), axis, 0)


def _from_shards(sh, axis):
    m = jnp.moveaxis(sh, 0, axis)
    s = m.shape
    return m.reshape(s[:axis] + (s[axis] * s[axis + 1],) + s[axis + 2:])


def kernel(x, meta, ffn_norm, ffn_w_gate, ffn_w_up, ffn_w_down, gla_norm, gla_w_in, gla_w_lr, gla_b_lr, gla_head_norm, gla_w_out, pool_norm, pool_w, pool_b, pool_scale, final_norm, loss_target, m_meta, m_ffn_norm, m_ffn_w_gate, m_ffn_w_up, m_ffn_w_down, m_gla_norm, m_gla_w_in, m_gla_w_lr, m_gla_b_lr, m_gla_head_norm, m_gla_w_out, m_pool_norm, m_pool_w, m_pool_b, m_pool_scale, m_final_norm, v_meta, v_ffn_norm, v_ffn_w_gate, v_ffn_w_up, v_ffn_w_down, v_gla_norm, v_gla_w_in, v_gla_w_lr, v_gla_b_lr, v_gla_head_norm, v_gla_w_out, v_pool_norm, v_pool_w, v_pool_b, v_pool_scale, v_final_norm):
    H = GLA_HEADS
    _, SEQ, D = x.shape
    Fs = ffn_w_gate.shape[-1]
    DK, DV = D // 2, D
    hv = DV // H
    GW = D // POOL_GROUPS
    INW = 2 * DK + 2 * DV + GATE_RANK
    NPK = 2 * DK + 2 * DV + GATE_PAD
    pad = (-N_META) % GLA_CHUNK
    first = pad + N_META
    Lp = first + SEQ
    n_units = ffn_w_gate.shape[0] * ffn_w_gate.shape[1]
    assert first % GLA_CHUNK == 0 and Lp % GLA_CHUNK == 0 and pad >= POOL_GROUPS * 4

    px, py, pc = _place()
    c_idx = jnp.reshape(pc, (1,)).astype(jnp.int32)
    q_idx = jnp.reshape(2 * px + py, (1,)).astype(jnp.int32)
    zero_idx = jnp.zeros((1,), jnp.int32)

    small_sh = [meta, ffn_norm, gla_w_lr, pool_norm, pool_b, pool_scale]
    small_axis = [1, 2, 2, 1, 2, 1]
    sh_shapes = [a.shape for a in small_sh]
    sh_rows = -(-sum(_rows(a.size) for a in small_sh) // 8) * 8
    gathered = _small_exchange(_pack(small_sh, sh_rows), True, "small_gather")
    meta_f, ffn_norm_f, wlr_f, pool_norm_f, pool_b_f, pool_scale_f = [
        _from_shards(a, ax) for a, ax in zip(_unpack(gathered, sh_shapes), small_axis)]
    ffn_norm_f = ffn_norm_f.reshape(n_units, 1, D)
    wlr128 = jnp.pad(wlr_f[0], ((0, GATE_PAD - GATE_RANK), (0, 0)))

    def t_units(w):
        return jnp.swapaxes(w, -1, -2).reshape(n_units, Fs, D)

    ffn_f32 = [t_units(ffn_w_gate), t_units(ffn_w_up), ffn_w_down.reshape(n_units, Fs, D)]
    mixer_f32 = [gla_w_in[0].T[None], gla_w_out, pool_w[0].reshape(1, -1, GW)]
    gather_order = [("ffn0", ffn_f32, 0), ("mixers", mixer_f32, 0)] + [(f"ffn{u}", ffn_f32, u) for u in range(1, n_units)]
    c_lr = 2 * DK + DV
    c_r = 2 * DK + 2 * DV
    gate_blk = c_r // GATE_PAD

    def cast_shards(i, after):
        tag, arrays, u = gather_order[i]
        shards = [_cast_unit(w, u, f"cast_{tag}_{a}", after) for a, w in enumerate(arrays)]
        if tag == "mixers":
            shards[2] = shards[2].reshape(pool_w.shape[1:])
        return shards

    def pass_on(i, h, after):
        tag = gather_order[i][0]
        nxt = later_shards[i + 1] if i + 1 < len(gather_order) else None
        h = _gather_mid(h, after, f"gather_mid_{tag}")
        if nxt is not None:
            nxt = _gather_start(nxt, h["token"], f"gather_start_{gather_order[i + 1][0]}")
        return h, nxt

    def complete(i, h, after):
        tag = gather_order[i][0]
        h = _gather_mid2(h, after, f"gather_mid2_{tag}")
        return _gather_end(h, h["token"], f"gather_end_{tag}")

    xs = jnp.concatenate([jnp.zeros((pad, D), f32), meta_f, x[0]], axis=0)
    saved = {}
    ffn_w = [None] * n_units

    def ffn_f(u, xs, after=None):
        out, h, G, U = _ffn_fwd(xs, ffn_norm_f[u], *ffn_w[u], name=f"ffn_fwd{u}", after=after)
        saved[("ffn", u)] = (xs, h, G, U)
        return out

    def gla_f(xs, win_p, wout_full, after=None):
        hn = _rms_fwd(xs, gla_norm, bf16, "gla_norm_fwd", after=after)
        proj = _mm(hn, win_p, "nt", f32, "gla_proj", tm=1056, tn=896, tk=2048)
        lg = _gate_fwd(proj, wlr128, gla_b_lr, pad, gate_blk, "gla_gate_fwd")
        o, y, states = _gla_fwd(proj, lg, gla_head_norm, H, "gla_core_fwd")
        out = _mm(y, wout_full, "nn", f32, "gla_out", tm=1056, tn=512, tk=2048, residual=xs)
        saved["gla"] = (xs, hn, proj, lg, o, y, states)
        return out

    def pool_f(xs, wpool_full):
        hn = _rms_fwd(xs, pool_norm_f, f32, "pool_norm_fwd")
        pooled = _pool_windows(hn, pad, "pool_windows_fwd")
        out = _pool_mix_fwd(xs, pooled, wpool_full, pool_b_f.reshape(1, D), pool_scale_f, pad, "pool_mix_fwd")
        saved["pool"] = (xs, pooled)
        return out

    depth = ffn_w_gate.shape[0]
    assert depth == 2 and n_units == 4
    h0 = _gather_start(cast_shards(0, None), gathered, "gather_start_ffn0")
    later_shards = {}
    last = h0["token"]
    for i in range(1, len(gather_order)):
        later_shards[i] = cast_shards(i, last)
        last = later_shards[i][0]
    h0, h1 = pass_on(0, h0, last)
    ffn_w[0] = complete(0, h0, h1["token"])
    h1, h2 = pass_on(1, h1, ffn_w[0][0])
    xs = ffn_f(0, xs, after=h2["token"])
    win_g, wout_g, wpool_g = complete(1, h1, xs)
    h2, h3 = pass_on(2, h2, wout_g)
    win_full = win_g.reshape(INW, D)
    win_p = jnp.concatenate([win_full[:c_lr], win_full[c_lr + GATE_RANK:], win_full[c_lr:c_lr + GATE_RANK],
                             jnp.zeros((GATE_PAD - GATE_RANK, D), bf16)], axis=0)
    wout_full = wout_g.reshape(DV, D)
    wpool_full = _from_shards(wpool_g, 1)
    xs = gla_f(xs, win_p, wout_full, after=h3["token"])
    ffn_w[1] = complete(2, h2, xs)
    h3, h4 = pass_on(3, h3, ffn_w[1][0])
    xs = ffn_f(1, xs, after=h4["token"])
    ffn_w[2] = complete(3, h3, xs)
    h4, _ = pass_on(4, h4, ffn_w[2][0])
    xs = ffn_f(2, xs, after=h4["token"])
    xs = pool_f(xs, wpool_full)
    ffn_w[3] = complete(4, h4, xs)
    xs = ffn_f(3, xs)
    loss_part, dxs, d_final, dyh = _loss_head(xs, loss_target[0], final_norm.reshape(1, D), first, "loss_head")

    class Reduce:
        def __init__(self, tag, grads, after=None):
            self.tag = tag
            self.h = _exchange_start("pair", grads, loss_part if after is None else after, f"pair_start_{tag}")
            self.token = self.h["token"]

        def mid(self, after):
            grads, got = _exchange_wait(self.h, after, f"pair_wait_{self.tag}")
            if len({g.shape for g in grads}) == 1:
                self.sums = list(_pair_add(grads, got, c_idx, f"pair_add_{self.tag}"))
            else:
                self.sums = [_pair_add([g], [r], c_idx, f"pair_add_{self.tag}{a}")[0] for a, (g, r) in enumerate(zip(grads, got))]
            self.h = _exchange_start("chips", self.sums, loss_part, f"chips_start_{self.tag}")
            self.token = self.h["token"]

        def end(self, after):
            sums, recv = _exchange_wait(self.h, after, f"chips_wait_{self.tag}")
            return list(zip(sums, recv))

    d_ffn_norm = [None] * n_units
    small_grads = {}

    def ffn_b(u, dY, dyh, prev):
        xs_in, h_, G, U = saved[("ffn", u)]
        wg, wu, wd = ffn_w[u]
        tok = None if prev is None else prev.token
        dG, dU, A = _ffn_bwd_act(dyh, wd, G, U, f"ffn_act{u}", after=tok)
        dh = _ffn_bwd_dh(dG, dU, wg, wu, f"ffn_dh{u}")
        dxs, dg, dyh_next = _rms_bwd(dY, dh, xs_in, ffn_norm_f[u], pad, f"ffn_norm_bwd{u}")
        if prev is not None:
            prev.mid(dxs)
            tok = prev.token
        dwg = _ffn_bwd_wgrad(dG, h_, f"ffn_wgrad_gate{u}", after=tok)
        dwu = _ffn_bwd_wgrad(dU, h_, f"ffn_wgrad_up{u}", after=tok)
        dwd = _ffn_bwd_wgrad(A, dyh, f"ffn_wgrad_down{u}", after=tok)
        d_ffn_norm[u] = dg
        return dxs, dyh_next, Reduce(f"ffn{u}", [dwg, dwu, dwd])

    def gla_b(dY, prev):
        xs_in, hn, proj, lg, o, y, states = saved["gla"]
        dyb = dY.astype(bf16)
        dy = _mm(dyb, wout_full, "nt", f32, "gla_out_dgrad", tm=1056, tn=512, tk=2048, after=prev.token)
        dwout = _mm_tn_full(y, dyb, "gla_out_wgrad", 1024, after=prev.token)
        prev.mid(dwout)
        dq, dk, dv, dr, dlg, dhw = _gla_bwd(dy, proj, lg, o, states, gla_head_norm, H, pad, "gla_core_bwd", after=prev.token)
        dlr, dwlr, dblr = _gate_bwd(dlg, proj, wlr128, gla_b_lr, pad, gate_blk, "gla_gate_bwd")
        dproj = jnp.concatenate([dq, dk, dv, dr, dlr], axis=1)
        dwin_p = _mm_tn_full(dproj, hn, "gla_proj_wgrad", 896)
        dhn = _mm(dproj, win_p, "nn", f32, "gla_proj_dgrad", tm=1056, tn=1024, tk=896)
        dxs, dgn, dyh_next = _rms_bwd(dY, dhn, xs_in, gla_norm, pad, "gla_norm_bwd")
        dwin = jnp.concatenate([dwin_p[:c_lr], dwin_p[c_r:c_r + GATE_RANK], dwin_p[c_lr:c_r]], axis=0)
        small_grads.update(gla_w_lr=dwlr[:GATE_RANK][None], gla_b_lr=dblr, gla_head_norm=dhw, gla_norm=dgn)
        return dxs, dyh_next, Reduce("gla", [dwin.reshape(N_DEV, INW // N_DEV, D), dwout.reshape(N_DEV, DV // N_DEV, D)])

    def pool_b_(dY, prev):
        xs_in, pooled = saved["pool"]
        dp, dw, db, ds = _pool_mix_bwd(dY, pooled, wpool_full, pool_b_f.reshape(1, D), pool_scale_f, pad, "pool_mix_bwd",
                                       after=prev.token)
        dhn = _pool_windows_bwd(dp, pad, "pool_windows_bwd")
        dxs, dgn, dyh_next = _rms_bwd(dY, dhn, xs_in, pool_norm_f, pad, "pool_norm_bwd")
        prev.mid(dxs)
        dws = _to_shards(dw, 1)
        small_grads.update(pool_b=db.reshape(1, POOL_GROUPS, GW), pool_scale=ds, pool_norm=dgn)
        return dxs, dyh_next, Reduce("pool", [dws.reshape(N_DEV, POOL_GROUPS * GW // N_DEV, GW)], after=prev.token)

    sh_names = ["meta", "ffn_norm", "gla_w_lr", "pool_norm", "pool_b", "pool_scale"]
    rep_names = ["gla_norm", "gla_b_lr", "gla_head_norm", "final_norm"]
    rep_w = [gla_norm, gla_b_lr, gla_head_norm, final_norm]
    rep_shapes = [a.shape for a in rep_w]
    rep_rows = -(-sum(_rows(a.size) for a in rep_w) // 8) * 8

    def small_path(dxs0):
        small_grads.update(meta=dxs0[pad:first], ffn_norm=jnp.concatenate(d_ffn_norm, axis=0).reshape(n_units // 2, 2, D),
                           final_norm=d_final.reshape(D))
        by_owner = [_to_shards(small_grads[nm].reshape(full_shape), ax) for nm, full_shape, ax in zip(
            sh_names, [meta_f.shape, (ffn_norm.shape[0], 2, D), wlr_f.shape, pool_norm_f.shape, pool_b_f.shape, pool_scale_f.shape],
            small_axis)]
        rep_pack = _pack([small_grads[nm].reshape(s) for nm, s in zip(rep_names, rep_shapes)], rep_rows)
        send = jnp.stack([
            jnp.concatenate([_pack([g[d] for g in by_owner], sh_rows), rep_pack, loss_part], axis=0) for d in range(N_DEV)])
        total = _sum_blocks(_small_exchange(send, False, "small_reduce"), "small_sum")
        n_small = sh_rows + rep_rows

        def pack_small(sh_list, rep_list):
            return jnp.concatenate([_pack(sh_list, sh_rows), _pack(rep_list, rep_rows)], axis=0)[None]

        w_small = pack_small(small_sh, rep_w)
        m_small = pack_small([m_meta, m_ffn_norm, m_gla_w_lr, m_pool_norm, m_pool_b, m_pool_scale],
                             [m_gla_norm, m_gla_b_lr, m_gla_head_norm, m_final_norm])
        v_small = pack_small([v_meta, v_ffn_norm, v_gla_w_lr, v_pool_norm, v_pool_b, v_pool_scale],
                             [v_gla_norm, v_gla_b_lr, v_gla_head_norm, v_final_norm])
        small_out = _adamw(w_small, m_small, v_small, 0, total[:, :n_small], zero_idx, None, None, "adamw_small")
        small_res = {}
        for kind, packed in zip(("grad", "delta", "new_m", "new_v"), small_out):
            sh_vals = _unpack(packed[0, :sh_rows], sh_shapes)
            rep_vals = _unpack(packed[0, sh_rows:], rep_shapes)
            for nm, val in zip(sh_names + rep_names, sh_vals + rep_vals):
                small_res[(kind, nm)] = val
        return total[0, n_small, 0], small_res, small_out[0]

    def ffn_b_last(dY, dyh, prev):
        xs_in, h_, G, U = saved[("ffn", 0)]
        wg, wu, wd = ffn_w[0]
        dG, dU, A = _ffn_bwd_act(dyh, wd, G, U, "ffn_act0", after=prev.token)
        prev.mid(dG)
        dwd = _ffn_bwd_wgrad(A, dyh, "ffn_wgrad_down0", after=prev.token)
        r_d = Reduce("ffn0_down", [dwd])
        dh = _ffn_bwd_dh(dG, dU, wg, wu, "ffn_dh0", after=r_d.token)
        dxs, dg, _ = _rms_bwd(dY, dh, xs_in, ffn_norm_f[0], pad, "ffn_norm_bwd0")
        d_ffn_norm[0] = dg
        small = small_path(dxs)
        r_d.mid(small[2])
        dwg = _ffn_bwd_wgrad(dG, h_, "ffn_wgrad_gate0", after=r_d.token)
        r_g = Reduce("ffn0_gate", [dwg])
        dwu = _ffn_bwd_wgrad(dU, h_, "ffn_wgrad_up0", after=r_g.token)
        r_g.mid(dwu)
        r_u = Reduce("ffn0_up", [dwu], after=r_g.token)
        return dxs, small, (r_g, r_u, r_d)

    dxs, dyh, r3 = ffn_b(3, dxs, dyh, None)
    dxs, dyh, rp = pool_b_(dxs, r3)
    dxs, dyh, r2 = ffn_b(2, dxs, dyh, rp)
    dxs, dyh, r1 = ffn_b(1, dxs, dyh, r2)
    dxs, dyh, rg = gla_b(dxs, r1)
    dxs, (loss, small_res, _), r0 = ffn_b_last(dxs, dyh, rg)
    grad_x = dxs[first:].reshape(x.shape)
    r_last = r0[1]

    big_res = {}

    def adam_one(nm, w, m, v, entry, transposed=False):
        sums, recv = entry
        R, C = sums.shape[1:]
        w1, m1, v1 = ((t[0].T if transposed else t).reshape(1, R, C) for t in (w, m, v))
        out = _adamw(w1, m1, v1, 0, sums, q_idx, recv, None, f"adamw_{nm}", after=r_last.token)
        for kind, val in zip(("grad", "delta", "new_m", "new_v"), out):
            big_res[(kind, nm)] = val[0].T[None] if transposed else val.reshape(w.shape)
        return out[0]

    e_gla = rg.end(dxs)
    done = adam_one("gla_w_in", gla_w_in, m_gla_w_in, v_gla_w_in, e_gla[0], transposed=True)
    done = adam_one("gla_w_out", gla_w_out, m_gla_w_out, v_gla_w_out, e_gla[1])
    done = adam_one("pool_w", pool_w, m_pool_w, v_pool_w, rp.end(done)[0])
    r_last.mid(done)

    ffn_names = ["ffn_w_gate", "ffn_w_up", "ffn_w_down"]
    ffn_wmv = [tuple(t_units(t) for t in (ffn_w_gate, m_ffn_w_gate, v_ffn_w_gate)),
               tuple(t_units(t) for t in (ffn_w_up, m_ffn_w_up, v_ffn_w_up)),
               tuple(t.reshape(n_units, Fs, D) for t in (ffn_w_down, m_ffn_w_down, v_ffn_w_down))]
    ffn_prev = [[lax.empty((n_units, Fs, D), f32) for _ in range(4)] for _ in range(3)]
    order_after = r_last.token
    for u, red in ((3, r3), (2, r2), (1, r1), (0, r0)):
        entries = [r.end(done)[0] for r in red] if u == 0 else red.end(done)
        for a in range(3):
            sums, recv = entries[a]
            ffn_prev[a] = _adamw(*ffn_wmv[a], u, sums, q_idx, recv, ffn_prev[a], f"adamw_{ffn_names[a]}{u}", after=order_after)
            done = order_after = ffn_prev[a][0]
    for a in range(3):
        for kind, val in zip(("grad", "delta", "new_m", "new_v"), ffn_prev[a]):
            val = val.reshape(ffn_w_down.shape)
            big_res[(kind, ffn_names[a])] = val if a == 2 else jnp.swapaxes(val, -1, -2)

    order = ["meta", "ffn_norm", "ffn_w_gate", "ffn_w_up", "ffn_w_down", "gla_norm", "gla_w_in", "gla_w_lr", "gla_b_lr",
             "gla_head_norm", "gla_w_out", "pool_norm", "pool_w", "pool_b", "pool_scale", "final_norm"]
    res = {**small_res, **big_res}
    outs = [loss, grad_x]
    for kind in ("grad", "delta", "new_m", "new_v"):
        outs += [res[(kind, nm)] for nm in order]
    return tuple(outs)
```

```python
import jax
import jax.numpy as jnp
from jax import lax
from jax.experimental import pallas as pl
from jax.experimental.pallas import tpu as pltpu

f32 = jnp.float32
bf16 = jnp.bfloat16

N_DEV = 8
N_META = 16
GLA_HEADS = 4
GLA_CHUNK = 64
GLA_SUB = 16
GLA_HEADS_PER_STEP = 4
GATE_RANK = 16
GATE_PAD = 128
GATE_NORM = 16.0
EPS = 1e-6
POOL_GROUPS = 4
ADAM_LR = 0.001
ADAM_B1 = 0.9
ADAM_B2 = 0.999
ADAM_EPS = 1e-08
ADAM_WD = 0.01
ADAM_STEP = 10
LANES = 128
VMEM_LIMIT_MB = 56

NN = (((1,), (0,)), ((), ()))
NT = (((1,), (1,)), ((), ()))
TN = (((0,), (0,)), ((), ()))
HI = lax.Precision.HIGHEST
MESH = pl.DeviceIdType.MESH
ANY = pl.BlockSpec(memory_space=pl.ANY)


def _cparams(sem=None, vmem_mb=None):
    kw = {}
    if sem is not None:
        kw["dimension_semantics"] = sem
    if vmem_mb is not None:
        kw["vmem_limit_bytes"] = vmem_mb * 2 ** 20
    return pltpu.CompilerParams(**kw)


def _tile(n, target, mult=16):
    best = None
    for t in range(mult, min(n, target) + 1, mult):
        if n % t == 0:
            best = t
    assert best is not None, (n, target, mult)
    return best


def _tile2(R, C, rows, mult):
    if R % mult == 0:
        return _tile(R, rows, mult), C
    return R, _tile(C, 256, LANES)


def _dot(a, b, dims=NN, precision=None):
    return lax.dot_general(a, b, dims, preferred_element_type=f32, precision=precision)


def _sigmoid(x):
    return 1.0 / (1.0 + jnp.exp(-x))


def _row_ids(tile_index, tm):
    return tile_index * tm + lax.broadcasted_iota(jnp.int32, (tm, 1), 0)


def _ordered(body, in_specs, args, after, lead=0):
    if after is None:
        return body, in_specs, args
    pos = lead + len(args)

    def body_without(*refs):
        return body(*refs[:pos], *refs[pos + 1:])

    return body_without, list(in_specs) + [ANY], list(args) + [after]


def _cast_unit(w, unit, name, after=None):
    _, R, C = w.shape
    tr, tc = _tile2(R, C, 256, 16)

    def body(w_ref, o_ref):
        o_ref[...] = w_ref[0].astype(bf16)

    body, in_specs, args = _ordered(body, [pl.BlockSpec((1, tr, tc), lambda i, j: (unit, i, j))], [w], after)
    return pl.pallas_call(
        body, name=name, grid=(R // tr, C // tc),
        in_specs=in_specs, out_specs=pl.BlockSpec((tr, tc), lambda i, j: (i, j)),
        out_shape=jax.ShapeDtypeStruct((R, C), bf16),
        compiler_params=_cparams(("parallel", "parallel")),
    )(*args)


def _rms_fwd(xs, g, out_dtype, name, after=None):
    Lp, D = xs.shape
    tm = _tile(Lp, 528)

    def body(x_ref, g_ref, h_ref):
        x = x_ref[...]
        rstd = lax.rsqrt(jnp.mean(x * x, axis=-1, keepdims=True) + EPS)
        h_ref[...] = (x * rstd * g_ref[...]).astype(out_dtype)

    in_specs = [pl.BlockSpec((tm, D), lambda i: (i, 0)), pl.BlockSpec((1, D), lambda i: (0, 0))]
    body, in_specs, args = _ordered(body, in_specs, [xs, g], after)
    return pl.pallas_call(
        body, name=name, grid=(Lp // tm,),
        in_specs=in_specs,
        out_specs=pl.BlockSpec((tm, D), lambda i: (i, 0)),
        out_shape=jax.ShapeDtypeStruct((Lp, D), out_dtype),
        compiler_params=_cparams(("parallel",)),
    )(*args)


def _rms_bwd(dY, dh, xs, g, pad, name):
    Lp, D = xs.shape
    tm = _tile(Lp, 352)

    def body(dY_ref, dh_ref, x_ref, g_ref, dxs_ref, dg_ref, half_ref):
        i = pl.program_id(0)

        @pl.when(i == 0)
        def _():
            dg_ref[...] = jnp.zeros_like(dg_ref)

        x = x_ref[...]
        rstd = lax.rsqrt(jnp.mean(x * x, axis=-1, keepdims=True) + EPS)
        xhat = x * rstd
        dh_ = dh_ref[...]
        dg_ref[...] += jnp.sum(dh_ * xhat, axis=0, keepdims=True)
        dxh = dh_ * g_ref[...]
        dx = rstd * (dxh - xhat * jnp.mean(dxh * xhat, axis=-1, keepdims=True))
        out = jnp.where(_row_ids(i, tm) >= pad, dY_ref[...] + dx, 0.0)
        dxs_ref[...] = out
        half_ref[...] = (0.5 * out).astype(bf16)

    row = pl.BlockSpec((tm, D), lambda i: (i, 0))
    vec = pl.BlockSpec((1, D), lambda i: (0, 0))
    return pl.pallas_call(
        body, name=name, grid=(Lp // tm,),
        in_specs=[row, row, row, vec], out_specs=[row, vec, row],
        out_shape=[jax.ShapeDtypeStruct((Lp, D), f32), jax.ShapeDtypeStruct((1, D), f32), jax.ShapeDtypeStruct((Lp, D), bf16)],
        compiler_params=_cparams(("arbitrary",)),
    )(dY, dh, xs, g)


def _mm(a, b, mode, out_dtype, name, tm=512, tn=512, tk=512, residual=None, after=None):
    if mode == "nn":
        (M, K), N = a.shape, b.shape[1]
    elif mode == "nt":
        (M, K), N = a.shape, b.shape[0]
    else:
        (K, M), N = a.shape, b.shape[1]
    tm = _tile(M, tm, 16 if mode != "tn" else LANES) if M > tm else M
    tn = _tile(N, tn, LANES) if N > tn else N
    tk = _tile(K, tk, LANES if mode != "tn" else 16) if K > tk else K
    nk = K // tk
    dims = {"nn": NN, "nt": NT, "tn": TN}[mode]

    def body(*refs):
        if residual is None:
            a_ref, b_ref, o_ref, acc = refs
            r_ref = None
        else:
            a_ref, b_ref, r_ref, o_ref, acc = refs
        k = pl.program_id(2)

        @pl.when(k == 0)
        def _():
            acc[...] = jnp.zeros_like(acc)

        acc[...] += _dot(a_ref[...], b_ref[...], dims)

        @pl.when(k == nk - 1)
        def _():
            r = acc[...]
            if r_ref is not None:
                r = r + r_ref[...]
            o_ref[...] = r.astype(out_dtype)

    a_spec = pl.BlockSpec((tk, tm), lambda i, j, k: (k, i)) if mode == "tn" else pl.BlockSpec((tm, tk), lambda i, j, k: (i, k))
    b_spec = pl.BlockSpec((tn, tk), lambda i, j, k: (j, k)) if mode == "nt" else pl.BlockSpec((tk, tn), lambda i, j, k: (k, j))
    o_spec = pl.BlockSpec((tm, tn), lambda i, j, k: (i, j))
    in_specs = [a_spec, b_spec] + ([o_spec] if residual is not None else [])
    args = [a, b] + ([residual] if residual is not None else [])
    body, in_specs, args = _ordered(body, in_specs, args, after)
    return pl.pallas_call(
        body, name=name, grid=(M // tm, N // tn, nk),
        in_specs=in_specs, out_specs=o_spec,
        out_shape=jax.ShapeDtypeStruct((M, N), out_dtype),
        scratch_shapes=[pltpu.VMEM((tm, tn), f32)],
        compiler_params=_cparams(("parallel", "parallel", "arbitrary"), VMEM_LIMIT_MB),
    )(*args)


def _mm_tn_full(a, b, name, tm, after=None):
    K, M = a.shape
    N = b.shape[1]
    tm = _tile(M, tm, LANES)

    def body(a_ref, b_ref, o_ref):
        o_ref[...] = _dot(a_ref[...], b_ref[...], TN).astype(bf16)

    in_specs = [pl.BlockSpec((K, tm), lambda i: (0, i)), pl.BlockSpec((K, N), lambda i: (0, 0), pipeline_mode=pl.Buffered(1))]
    body, in_specs, args = _ordered(body, in_specs, [a, b], after)
    return pl.pallas_call(
        body, name=name, grid=(M // tm,),
        in_specs=in_specs, out_specs=pl.BlockSpec((tm, N), lambda i: (i, 0)),
        out_shape=jax.ShapeDtypeStruct((M, N), bf16),
        compiler_params=_cparams(("parallel",), VMEM_LIMIT_MB),
    )(*args)


def _ffn_fwd(xs, g, wg, wu, wd, name, after=None):
    Lp, D = xs.shape
    nd, Fs, _ = wg.shape
    tm = _tile(Lp, 704)
    once = pl.Buffered(1)

    def body(x_ref, g_ref, wg_ref, wu_ref, wd_ref, out_ref, h_ref, G_ref, U_ref, hs, acc):
        j = pl.program_id(1)

        @pl.when(j == 0)
        def _():
            x = x_ref[...]
            rstd = lax.rsqrt(jnp.mean(x * x, axis=-1, keepdims=True) + EPS)
            h = (x * rstd * g_ref[...]).astype(bf16)
            hs[...] = h
            h_ref[...] = h
            acc[...] = jnp.zeros_like(acc)

        h = hs[...]
        G = _dot(h, wg_ref[0], NT)
        U = _dot(h, wu_ref[0], NT)
        G_ref[0] = G.astype(bf16)
        U_ref[0] = U.astype(bf16)
        A = (G * _sigmoid(G) * U).astype(bf16)
        acc[...] += _dot(A, wd_ref[0])

        @pl.when(j == nd - 1)
        def _():
            out_ref[...] = x_ref[...] + 0.5 * acc[...]

    row_f = pl.BlockSpec((tm, D), lambda i, j: (i, 0), pipeline_mode=once)
    act = pl.BlockSpec((1, tm, Fs), lambda i, j: (j, i, 0))
    wrow = pl.BlockSpec((1, Fs, D), lambda i, j: (j, 0, 0))
    in_specs = [row_f, pl.BlockSpec((1, D), lambda i, j: (0, 0)), wrow, wrow, wrow]
    body, in_specs, args = _ordered(body, in_specs, [xs, g, wg, wu, wd], after)
    return pl.pallas_call(
        body, name=name, grid=(Lp // tm, nd),
        in_specs=in_specs,
        out_specs=[row_f, pl.BlockSpec((tm, D), lambda i, j: (i, 0), pipeline_mode=once), act, act],
        out_shape=[jax.ShapeDtypeStruct((Lp, D), f32), jax.ShapeDtypeStruct((Lp, D), bf16),
                   jax.ShapeDtypeStruct((nd, Lp, Fs), bf16), jax.ShapeDtypeStruct((nd, Lp, Fs), bf16)],
        scratch_shapes=[pltpu.VMEM((tm, D), bf16), pltpu.VMEM((tm, D), f32)],
        compiler_params=_cparams(("parallel", "arbitrary"), VMEM_LIMIT_MB),
    )(*args)


def _ffn_bwd_act(dyh, wd, G, U, name, after=None):
    Lp, D = dyh.shape
    nd, Fs, _ = wd.shape
    tm = _tile(Lp, 704)

    def body(dyh_ref, wd_ref, G_ref, U_ref, dG_ref, dU_ref, A_ref):
        dA = _dot(dyh_ref[...], wd_ref[0], NT)
        Gf = G_ref[0].astype(f32)
        Uf = U_ref[0].astype(f32)
        s = _sigmoid(Gf)
        silu = Gf * s
        dG_ref[0] = (dA * Uf * (s * (1.0 + Gf * (1.0 - s)))).astype(bf16)
        dU_ref[0] = (dA * silu).astype(bf16)
        A_ref[0] = (silu * Uf).astype(bf16)

    act = pl.BlockSpec((1, tm, Fs), lambda j, i: (j, i, 0))
    act_s = jax.ShapeDtypeStruct((nd, Lp, Fs), bf16)
    in_specs = [pl.BlockSpec((tm, D), lambda j, i: (i, 0)), pl.BlockSpec((1, Fs, D), lambda j, i: (j, 0, 0)), act, act]
    body, in_specs, args = _ordered(body, in_specs, [dyh, wd, G, U], after)
    return pl.pallas_call(
        body, name=name, grid=(nd, Lp // tm),
        in_specs=in_specs, out_specs=[act, act, act], out_shape=[act_s, act_s, act_s],
        compiler_params=_cparams(("parallel", "parallel"), VMEM_LIMIT_MB),
    )(*args)


def _ffn_bwd_dh(dG, dU, wg, wu, name, after=None):
    nd, Lp, Fs = dG.shape
    D = wg.shape[2]
    tm = _tile(Lp, 1056)

    def body(dG_ref, dU_ref, wg_ref, wu_ref, dh_ref, acc):
        j = pl.program_id(1)

        @pl.when(j == 0)
        def _():
            acc[...] = jnp.zeros_like(acc)

        acc[...] += _dot(dG_ref[0], wg_ref[0]) + _dot(dU_ref[0], wu_ref[0])

        @pl.when(j == nd - 1)
        def _():
            dh_ref[...] = acc[...]

    act = pl.BlockSpec((1, tm, Fs), lambda i, j: (j, i, 0))
    wrow = pl.BlockSpec((1, Fs, D), lambda i, j: (j, 0, 0))
    body, in_specs, args = _ordered(body, [act, act, wrow, wrow], [dG, dU, wg, wu], after)
    return pl.pallas_call(
        body, name=name, grid=(Lp // tm, nd),
        in_specs=in_specs,
        out_specs=pl.BlockSpec((tm, D), lambda i, j: (i, 0), pipeline_mode=pl.Buffered(1)),
        out_shape=jax.ShapeDtypeStruct((Lp, D), f32),
        scratch_shapes=[pltpu.VMEM((tm, D), f32)],
        compiler_params=_cparams(("parallel", "arbitrary"), VMEM_LIMIT_MB),
    )(*args)


def _ffn_bwd_wgrad(act, rows, name, after=None):
    nd, Lp, Fs = act.shape
    D = rows.shape[1]

    def body(a_ref, r_ref, o_ref):
        o_ref[0] = _dot(a_ref[0], r_ref[...], TN).astype(bf16)

    in_specs = [pl.BlockSpec((1, Lp, Fs), lambda j: (j, 0, 0)),
                pl.BlockSpec((Lp, D), lambda j: (0, 0), pipeline_mode=pl.Buffered(1))]
    body, in_specs, args = _ordered(body, in_specs, [act, rows], after)
    return pl.pallas_call(
        body, name=name, grid=(nd,),
        in_specs=in_specs, out_specs=pl.BlockSpec((1, Fs, D), lambda j: (j, 0, 0)),
        out_shape=jax.ShapeDtypeStruct((nd, Fs, D), bf16),
        compiler_params=_cparams(("parallel",), VMEM_LIMIT_MB),
    )(*args)


def _gate_fwd(proj, wlr, blr, pad, gate_blk, name):
    Lp = proj.shape[0]
    DK = wlr.shape[1]
    tm = _tile(Lp, 528)

    def body(lr_ref, w_ref, b_ref, lg_ref):
        z = _dot(lr_ref[...].astype(bf16), w_ref[...].astype(bf16)) + b_ref[...]
        ls = jnp.minimum(z, 0.0) - jnp.log(1.0 + jnp.exp(-jnp.abs(z)))
        lg_ref[...] = jnp.where(_row_ids(pl.program_id(0), tm) >= pad, ls * (1.0 / GATE_NORM), 0.0)

    return pl.pallas_call(
        body, name=name, grid=(Lp // tm,),
        in_specs=[pl.BlockSpec((tm, GATE_PAD), lambda i: (i, gate_blk)),
                  pl.BlockSpec((GATE_PAD, DK), lambda i: (0, 0)), pl.BlockSpec((1, DK), lambda i: (0, 0))],
        out_specs=pl.BlockSpec((tm, DK), lambda i: (i, 0)),
        out_shape=jax.ShapeDtypeStruct((Lp, DK), f32),
        compiler_params=_cparams(("parallel",)),
    )(proj, wlr, blr)


def _gate_bwd(dlg, proj, wlr, blr, pad, gate_blk, name):
    Lp = proj.shape[0]
    DK = wlr.shape[1]
    tm = _tile(Lp, 528)

    def body(dlg_ref, lr_ref, w_ref, b_ref, dlr_ref, dw_ref, db_ref):
        i = pl.program_id(0)

        @pl.when(i == 0)
        def _():
            dw_ref[...] = jnp.zeros_like(dw_ref)
            db_ref[...] = jnp.zeros_like(db_ref)

        lr = lr_ref[...].astype(bf16)
        w = w_ref[...].astype(bf16)
        z = _dot(lr, w) + b_ref[...]
        dz = jnp.where(_row_ids(i, tm) >= pad, dlg_ref[...] * _sigmoid(-z) * (1.0 / GATE_NORM), 0.0)
        dzb = dz.astype(bf16)
        dlr_ref[...] = _dot(dzb, w, NT).astype(bf16)
        dw_ref[...] += _dot(lr, dzb, TN)
        db_ref[...] += jnp.sum(dz, axis=0, keepdims=True)

    return pl.pallas_call(
        body, name=name, grid=(Lp // tm,),
        in_specs=[pl.BlockSpec((tm, DK), lambda i: (i, 0)), pl.BlockSpec((tm, GATE_PAD), lambda i: (i, gate_blk)),
                  pl.BlockSpec((GATE_PAD, DK), lambda i: (0, 0)), pl.BlockSpec((1, DK), lambda i: (0, 0))],
        out_specs=[pl.BlockSpec((tm, GATE_PAD), lambda i: (i, 0)), pl.BlockSpec((GATE_PAD, DK), lambda i: (0, 0)),
                   pl.BlockSpec((1, DK), lambda i: (0, 0))],
        out_shape=[jax.ShapeDtypeStruct((Lp, GATE_PAD), bf16), jax.ShapeDtypeStruct((GATE_PAD, DK), f32),
                   jax.ShapeDtypeStruct((1, DK), f32)],
        compiler_params=_cparams(("arbitrary",)),
    )(dlg, proj, wlr, blr)


def _chunk_decay(lg):
    C = lg.shape[0]
    r = lax.broadcasted_iota(jnp.int32, (C, C), 0)
    c = lax.broadcasted_iota(jnp.int32, (C, C), 1)
    return _dot(jnp.where(r >= c, 1.0, 0.0).astype(f32), lg, NN, HI)


def _col(v):
    return jnp.transpose(jnp.broadcast_to(v, (8, v.shape[1])))[:, 0:1]


def _intra_scores(q, k, b, A_ref):
    C = q.shape[0]
    S = GLA_SUB
    A_ref[...] = jnp.zeros_like(A_ref)
    ri = lax.broadcasted_iota(jnp.int32, (S, 1), 0)
    for I in range(C // S):
        lo = S * I
        qI, bI = q[lo:lo + S], b[lo:lo + S]
        if I > 0:
            bref = b[lo - 1:lo]
            qs = qI * jnp.exp(bI - bref)
            ks = k[:lo] * jnp.exp(bref - b[:lo])
            A_ref[lo:lo + S, 0:lo] = _dot(qs, ks, NT, HI)
        for jj in range(S):
            j = lo + jj
            P = jnp.exp(jnp.minimum(bI - b[j:j + 1], 0.0))
            a = jnp.sum(qI * P * k[j:j + 1], axis=1, keepdims=True)
            A_ref[lo:lo + S, j:j + 1] = jnp.where(ri >= jj, a, 0.0)


def _intra_grads(q, k, b, dA, dq_ref, dk_ref):
    C = q.shape[0]
    S = GLA_SUB
    ri = lax.broadcasted_iota(jnp.int32, (S, 1), 0)
    for I in range(C // S):
        lo = S * I
        qI, bI = q[lo:lo + S], b[lo:lo + S]
        dqI = jnp.zeros_like(qI)
        if I > 0:
            bref = b[lo - 1:lo]
            eq = jnp.exp(bI - bref)
            ek = jnp.exp(bref - b[:lo])
            qs = qI * eq
            ks = k[:lo] * ek
            dAI = dA[lo:lo + S, 0:lo]
            dqI = dqI + _dot(dAI, ks, NN, HI) * eq
            dk_ref[0:lo, :] += _dot(dAI, qs, TN, HI) * ek
        for jj in range(S):
            j = lo + jj
            P = jnp.exp(jnp.minimum(bI - b[j:j + 1], 0.0))
            t = jnp.where(ri >= jj, dA[lo:lo + S, j:j + 1], 0.0) * P
            dqI = dqI + t * k[j:j + 1]
            dk_ref[j:j + 1, :] += jnp.sum(t * qI, axis=0, keepdims=True)
        dq_ref[lo:lo + S, :] += dqI


def _gla_fwd(proj, lg, hnw, H, name):
    Lp = proj.shape[0]
    DK = lg.shape[1]
    hk = DK // H
    hv = hnw.shape[1]
    DV = hv * H
    C = GLA_CHUNK
    NC = Lp // C
    HS = min(GLA_HEADS_PER_STEP, H)
    G = H // HS
    scale = float(hk) ** -0.5
    kq, kv, kr = G, (2 * DK) // (HS * hv), (2 * DK) // (HS * hv) + G

    def body(q_ref, k_ref, v_ref, r_ref, lg_ref, w_ref, o_ref, y_ref, s_ref, S_scr, A_scr):
        c = pl.program_id(1)

        @pl.when(c == 0)
        def _():
            S_scr[...] = jnp.zeros_like(S_scr)

        for hh in range(HS):
            ck, cv = slice(hh * hk, (hh + 1) * hk), slice(hh * hv, (hh + 1) * hv)
            q = q_ref[:, ck] * scale
            k = k_ref[:, ck]
            v = v_ref[:, cv]
            b = _chunk_decay(lg_ref[:, ck])
            bl = b[C - 1:C]
            S = S_scr[hh]
            s_ref[hh, 0] = S
            _intra_scores(q, k, b, A_scr.at[hh])
            vb = v.astype(bf16)
            o = _dot((q * jnp.exp(b)).astype(bf16), S.astype(bf16)) + _dot(A_scr[hh].astype(bf16), vb)
            kb = (k * jnp.exp(bl - b)).astype(bf16)
            S_scr[hh] = jnp.exp(_col(bl)) * S + _dot(kb, vb, TN)
            o_ref[:, cv] = o
            on = o * lax.rsqrt(jnp.mean(o * o, axis=-1, keepdims=True) + EPS) * w_ref[...]
            r = r_ref[:, cv]
            y_ref[:, cv] = (on * (r * _sigmoid(r))).astype(bf16)

    return pl.pallas_call(
        body, name=name, grid=(G, NC),
        in_specs=[pl.BlockSpec((C, HS * hk), lambda g, c: (c, g)),
                  pl.BlockSpec((C, HS * hk), lambda g, c: (c, kq + g)),
                  pl.BlockSpec((C, HS * hv), lambda g, c: (c, kv + g)),
                  pl.BlockSpec((C, HS * hv), lambda g, c: (c, kr + g)),
                  pl.BlockSpec((C, HS * hk), lambda g, c: (c, g)),
                  pl.BlockSpec((1, hv), lambda g, c: (0, 0))],
        out_specs=[pl.BlockSpec((C, HS * hv), lambda g, c: (c, g)), pl.BlockSpec((C, HS * hv), lambda g, c: (c, g)),
                   pl.BlockSpec((HS, 1, hk, hv), lambda g, c: (g, c, 0, 0))],
        out_shape=[jax.ShapeDtypeStruct((Lp, DV), f32), jax.ShapeDtypeStruct((Lp, DV), bf16),
                   jax.ShapeDtypeStruct((H, NC, hk, hv), f32)],
        scratch_shapes=[pltpu.VMEM((HS, hk, hv), f32), pltpu.VMEM((HS, C, C), f32)],
        compiler_params=_cparams(("parallel", "arbitrary")),
    )(proj, proj, proj, proj, lg, hnw)


def _gla_bwd(dy, proj, lg, o, states, hnw, H, pad, name, after=None):
    Lp = proj.shape[0]
    DK = lg.shape[1]
    hk = DK // H
    hv = hnw.shape[1]
    DV = hv * H
    C = GLA_CHUNK
    NC = Lp // C
    HS = min(GLA_HEADS_PER_STEP, H)
    G = H // HS
    scale = float(hk) ** -0.5
    kq, kv, kr = G, (2 * DK) // (HS * hv), (2 * DK) // (HS * hv) + G

    def body(dy_ref, q_ref, k_ref, v_ref, r_ref, lg_ref, o_ref, s_ref, sn_ref, w_ref,
             dq_ref, dk_ref, dv_ref, dr_ref, dlg_ref, dw_ref, dS_scr, A_scr, dq_s, dk_s):
        g = pl.program_id(0)
        cc = pl.program_id(1)
        c = NC - 1 - cc

        @pl.when(cc == 0)
        def _():
            dS_scr[...] = jnp.zeros_like(dS_scr)

        @pl.when((cc == 0) & (g == 0))
        def _():
            dw_ref[...] = jnp.zeros_like(dw_ref)

        keep = (c * C + lax.broadcasted_iota(jnp.int32, (C, 1), 0)) >= pad
        ri = lax.broadcasted_iota(jnp.int32, (C, C), 0)
        ci = lax.broadcasted_iota(jnp.int32, (C, C), 1)
        w = w_ref[...]
        for hh in range(HS):
            ck, cv = slice(hh * hk, (hh + 1) * hk), slice(hh * hv, (hh + 1) * hv)
            o_ = o_ref[:, cv]
            rs = lax.rsqrt(jnp.mean(o_ * o_, axis=-1, keepdims=True) + EPS)
            ohat = o_ * rs
            r = r_ref[:, cv]
            sg = _sigmoid(r)
            dy_ = dy_ref[:, cv]
            d_on = dy_ * (r * sg)
            dr_ref[:, cv] = jnp.where(keep, dy_ * (ohat * w) * (sg * (1.0 + r * (1.0 - sg))), 0.0).astype(bf16)
            dw_ref[...] += jnp.sum(d_on * ohat, axis=0, keepdims=True)
            d_oh = d_on * w
            do = rs * (d_oh - ohat * jnp.mean(d_oh * ohat, axis=-1, keepdims=True))
            dob = do.astype(bf16)
            q = q_ref[:, ck] * scale
            k = k_ref[:, ck]
            vb = v_ref[:, cv].astype(bf16)
            b = _chunk_decay(lg_ref[:, ck])
            bl = b[C - 1:C]
            eb = jnp.exp(b)
            ekb = jnp.exp(bl - b)
            S = s_ref[hh, 0]
            dS = dS_scr[hh]
            dSb = dS.astype(bf16)
            _intra_scores(q, k, b, A_scr.at[hh])
            dA = jnp.where(ri >= ci, _dot(dob, vb, NT), 0.0)
            kb = (k * ekb).astype(bf16)
            qb = (q * eb).astype(bf16)
            dv = _dot(A_scr[hh].astype(bf16), dob, TN) + _dot(kb, dSb)
            dq_s[hh] = _dot(dob, S.astype(bf16), NT) * eb
            dk_s[hh] = _dot(vb, dSb, NT) * ekb
            dS_scr[hh] = _dot(qb, dob, TN) + jnp.exp(_col(bl)) * dS
            _intra_grads(q, k, b, dA, dq_s.at[hh], dk_s.at[hh])
            dq = dq_s[hh]
            dk = dk_s[hh]
            Dm = q * dq - k * dk
            after_rows = _dot(jnp.ones((8, hv), f32), sn_ref[hh, 0] * dS, NT, HI)[0:1]
            dlg = _dot(jnp.where(ri <= ci, 1.0, 0.0).astype(f32), Dm, NN, HI) + after_rows
            dlg_ref[:, ck] = jnp.where(keep, dlg, 0.0)
            dq_ref[:, ck] = jnp.where(keep, dq * scale, 0.0).astype(bf16)
            dk_ref[:, ck] = jnp.where(keep, dk, 0.0).astype(bf16)
            dv_ref[:, cv] = jnp.where(keep, dv, 0.0).astype(bf16)

    rev = lambda cc: NC - 1 - cc
    bk = lambda off: pl.BlockSpec((C, HS * hk), lambda g, cc: (rev(cc), off + g))
    bv = lambda off: pl.BlockSpec((C, HS * hv), lambda g, cc: (rev(cc), off + g))
    in_specs = [bv(0), bk(0), bk(kq), bv(kv), bv(kr), bk(0), bv(0),
                pl.BlockSpec((HS, 1, hk, hv), lambda g, cc: (g, rev(cc), 0, 0)),
                pl.BlockSpec((HS, 1, hk, hv), lambda g, cc: (g, jnp.minimum(rev(cc) + 1, NC - 1), 0, 0)),
                pl.BlockSpec((1, hv), lambda g, cc: (0, 0))]
    body, in_specs, args = _ordered(body, in_specs, [dy, proj, proj, proj, proj, lg, o, states, states, hnw], after)
    return pl.pallas_call(
        body, name=name, grid=(G, NC),
        in_specs=in_specs,
        out_specs=[bk(0), bk(0), bv(0), bv(0), bk(0), pl.BlockSpec((1, hv), lambda g, cc: (0, 0))],
        out_shape=[jax.ShapeDtypeStruct((Lp, DK), bf16), jax.ShapeDtypeStruct((Lp, DK), bf16),
                   jax.ShapeDtypeStruct((Lp, DV), bf16), jax.ShapeDtypeStruct((Lp, DV), bf16),
                   jax.ShapeDtypeStruct((Lp, DK), f32), jax.ShapeDtypeStruct((1, hv), f32)],
        scratch_shapes=[pltpu.VMEM((HS, hk, hv), f32), pltpu.VMEM((HS, C, C), f32),
                        pltpu.VMEM((HS, C, hk), f32), pltpu.VMEM((HS, C, hk), f32)],
        compiler_params=_cparams(("arbitrary", "arbitrary")),
    )(*args)


def _window_sums(x, back):
    n = x.shape[0]
    out = []
    s = x
    for w in (1, 2, 4, 8):
        s = s + pltpu.roll(s, w if back else n - w, 0)
        out.append(s)
    return out


def _pool_windows(hn, pad, name):
    Lp, D = hn.shape
    GW = D // POOL_GROUPS
    cb = min(GW, 256)
    per = GW // cb

    def body(h_ref, p_ref):
        g = pl.program_id(0) // per
        x = h_ref[...]
        s2, s4, s8, s16 = _window_sums(x, True)
        sel = jnp.where(g == 0, s2, jnp.where(g == 1, s4, jnp.where(g == 2, s8, s16)))
        win = jnp.left_shift(2, g).astype(f32)
        rows = lax.broadcasted_iota(jnp.int32, (Lp, 1), 0)
        t = (rows - pad).astype(f32)
        cnt = jnp.minimum(jnp.maximum(t, 0.0) + 1.0, win)
        p_ref[...] = jnp.where(rows >= pad, sel / cnt - x, 0.0).astype(bf16)

    return pl.pallas_call(
        body, name=name, grid=(D // cb,),
        in_specs=[pl.BlockSpec((Lp, cb), lambda i: (0, i))],
        out_specs=pl.BlockSpec((Lp, cb), lambda i: (0, i)),
        out_shape=jax.ShapeDtypeStruct((Lp, D), bf16),
        compiler_params=_cparams(("parallel",)),
    )(hn)


def _pool_windows_bwd(dp, pad, name):
    Lp, D = dp.shape
    GW = D // POOL_GROUPS
    cb = min(GW, 256)
    per = GW // cb

    def body(dp_ref, dh_ref):
        g = pl.program_id(0) // per
        rows = lax.broadcasted_iota(jnp.int32, (Lp, 1), 0)
        d = jnp.where(rows >= pad, dp_ref[...], 0.0)
        win = jnp.left_shift(2, g).astype(f32)
        t = (rows - pad).astype(f32)
        cnt = jnp.minimum(jnp.maximum(t, 0.0) + 1.0, win)
        s2, s4, s8, s16 = _window_sums(d / cnt, False)
        sel = jnp.where(g == 0, s2, jnp.where(g == 1, s4, jnp.where(g == 2, s8, s16)))
        dh_ref[...] = jnp.where(rows >= pad, sel - d, 0.0)

    return pl.pallas_call(
        body, name=name, grid=(D // cb,),
        in_specs=[pl.BlockSpec((Lp, cb), lambda i: (0, i))],
        out_specs=pl.BlockSpec((Lp, cb), lambda i: (0, i)),
        out_shape=jax.ShapeDtypeStruct((Lp, D), f32),
        compiler_params=_cparams(("parallel",)),
    )(dp)


def _pool_mix_fwd(xs, pooled, w, bias, scale, pad, name):
    Lp, D = xs.shape
    GW = D // POOL_GROUPS
    tm = _tile(Lp, 1056)

    def body(x_ref, p_ref, w_ref, b_ref, s_ref, o_ref):
        z = _dot(p_ref[...], w_ref[0]) + b_ref[...]
        keep = _row_ids(pl.program_id(1), tm) >= pad
        o_ref[...] = x_ref[...] + jnp.where(keep, z * s_ref[...], 0.0)

    blk = pl.BlockSpec((tm, GW), lambda g, i: (i, g))
    vec = pl.BlockSpec((1, GW), lambda g, i: (0, g))
    return pl.pallas_call(
        body, name=name, grid=(POOL_GROUPS, Lp // tm),
        in_specs=[blk, blk, pl.BlockSpec((1, GW, GW), lambda g, i: (g, 0, 0)), vec, vec],
        out_specs=blk, out_shape=jax.ShapeDtypeStruct((Lp, D), f32),
        compiler_params=_cparams(("parallel", "parallel")),
    )(xs, pooled, w, bias, scale)


def _pool_mix_bwd(dY, pooled, w, bias, scale, pad, name, after=None):
    Lp, D = dY.shape
    GW = D // POOL_GROUPS
    tm = _tile(Lp, 1056)
    nm = Lp // tm

    def body(dY_ref, p_ref, w_ref, b_ref, s_ref, dp_ref, dw_ref, db_ref, ds_ref, acc):
        i = pl.program_id(1)

        @pl.when(i == 0)
        def _():
            acc[...] = jnp.zeros_like(acc)
            db_ref[...] = jnp.zeros_like(db_ref)
            ds_ref[...] = jnp.zeros_like(ds_ref)

        keep = _row_ids(i, tm) >= pad
        dY_ = jnp.where(keep, dY_ref[...], 0.0)
        p = p_ref[...]
        z = _dot(p, w_ref[0]) + b_ref[...]
        ds_ref[...] += jnp.sum(dY_ * z, axis=0, keepdims=True)
        dz = dY_ * s_ref[...]
        db_ref[...] += jnp.sum(dz, axis=0, keepdims=True)
        dzb = dz.astype(bf16)
        acc[...] += _dot(p, dzb, TN)
        dp_ref[...] = _dot(dzb, w_ref[0], NT)

        @pl.when(i == nm - 1)
        def _():
            dw_ref[0] = acc[...].astype(bf16)

    blk = pl.BlockSpec((tm, GW), lambda g, i: (i, g))
    vec = pl.BlockSpec((1, GW), lambda g, i: (0, g))
    wsp = pl.BlockSpec((1, GW, GW), lambda g, i: (g, 0, 0))
    body, in_specs, args = _ordered(body, [blk, blk, wsp, vec, vec], [dY, pooled, w, bias, scale], after)
    return pl.pallas_call(
        body, name=name, grid=(POOL_GROUPS, nm),
        in_specs=in_specs, out_specs=[blk, wsp, vec, vec],
        out_shape=[jax.ShapeDtypeStruct((Lp, D), f32), jax.ShapeDtypeStruct((POOL_GROUPS, GW, GW), bf16),
                   jax.ShapeDtypeStruct((1, D), f32), jax.ShapeDtypeStruct((1, D), f32)],
        scratch_shapes=[pltpu.VMEM((GW, GW), f32)],
        compiler_params=_cparams(("parallel", "arbitrary")),
    )(*args)


def _loss_head(xs, target, g, first, name):
    Lp, D = xs.shape
    tm = GLA_CHUNK
    off = first // tm

    def body(x_ref, t_ref, g_ref, loss_ref, dxs_ref, dg_ref, half_ref):
        i = pl.program_id(0)

        @pl.when(i == 0)
        def _():
            loss_ref[...] = jnp.zeros_like(loss_ref)
            dg_ref[...] = jnp.zeros_like(dg_ref)

        @pl.when(i < off)
        def _():
            dxs_ref[...] = jnp.zeros_like(dxs_ref)
            half_ref[...] = jnp.zeros_like(half_ref)

        @pl.when(i >= off)
        def _():
            x = x_ref[...]
            rstd = lax.rsqrt(jnp.mean(x * x, axis=-1, keepdims=True) + EPS)
            xhat = x * rstd
            gg = g_ref[...]
            err = xhat * gg - t_ref[...]
            loss_ref[...] += 0.5 * jnp.sum(jnp.mean(err * err, axis=-1, keepdims=True))
            dy = err * (1.0 / D)
            dg_ref[...] += jnp.sum(dy * xhat, axis=0, keepdims=True)
            dxh = dy * gg
            out = rstd * (dxh - xhat * jnp.mean(dxh * xhat, axis=-1, keepdims=True))
            dxs_ref[...] = out
            half_ref[...] = (0.5 * out).astype(bf16)

    row = pl.BlockSpec((tm, D), lambda i: (i, 0))
    return pl.pallas_call(
        body, name=name, grid=(Lp // tm,),
        in_specs=[row, pl.BlockSpec((tm, D), lambda i: (jnp.maximum(i - off, 0), 0)), pl.BlockSpec((1, D), lambda i: (0, 0))],
        out_specs=[pl.BlockSpec((8, LANES), lambda i: (0, 0)), row, pl.BlockSpec((1, D), lambda i: (0, 0)), row],
        out_shape=[jax.ShapeDtypeStruct((8, LANES), f32), jax.ShapeDtypeStruct((Lp, D), f32),
                   jax.ShapeDtypeStruct((1, D), f32), jax.ShapeDtypeStruct((Lp, D), bf16)],
        compiler_params=_cparams(("arbitrary",)),
    )(xs, target, g)


def _adam_math(w, g, m, v):
    m2 = ADAM_B1 * m + (1.0 - ADAM_B1) * g
    v2 = ADAM_B2 * v + (1.0 - ADAM_B2) * (g * g)
    m_hat = m2 / (1.0 - ADAM_B1 ** ADAM_STEP)
    v_hat = v2 / (1.0 - ADAM_B2 ** ADAM_STEP)
    delta = -ADAM_LR * (m_hat / (jnp.sqrt(v_hat) + ADAM_EPS) + ADAM_WD * w)
    return delta, m2, v2


def _adamw(w, m, v, unit, own, own_idx, recv, prev, name, after=None):
    U, R, C = w.shape
    tr, tc = _tile2(R, C, 256, 8 if own.dtype == f32 and recv is None else 16)
    n_recv = 0 if recv is None else recv.shape[0]

    def body(idx_ref, w_ref, m_ref, v_ref, own_ref, *rest):
        rest = list(rest)
        recv_refs = [rest.pop(0) for _ in range(n_recv)]
        if prev is not None:
            rest = rest[4:]
        g_ref, d_ref, m2_ref, v2_ref = rest
        g = own_ref[0].astype(f32)
        for r_ref in recv_refs:
            g = g + r_ref[0].astype(f32)
        delta, m2, v2 = _adam_math(w_ref[0], g, m_ref[0], v_ref[0])
        g_ref[0] = g
        d_ref[0] = delta
        m2_ref[0] = m2
        v2_ref[0] = v2

    blk = pl.BlockSpec((1, tr, tc), lambda i, j, idx: (unit, i, j))
    in_specs = [blk, blk, blk, pl.BlockSpec((1, tr, tc), lambda i, j, idx: (idx[0], i, j))]
    args = [w, m, v, own]
    for p in range(n_recv):
        in_specs.append(pl.BlockSpec((1, tr, tc), lambda i, j, idx, p=p: (p, i, j)))
        args.append(recv)
    aliases = {}
    if prev is not None:
        for t in range(4):
            aliases[1 + len(args) + t] = t
        in_specs += [ANY] * 4
        args += list(prev)
    body, in_specs, args = _ordered(body, in_specs, args, after, lead=1)
    out = jax.ShapeDtypeStruct((U, R, C), f32)
    return pl.pallas_call(
        body, name=name,
        grid_spec=pltpu.PrefetchScalarGridSpec(
            num_scalar_prefetch=1, grid=(R // tr, C // tc), in_specs=in_specs, out_specs=[blk] * 4),
        out_shape=[out] * 4, input_output_aliases=aliases,
        compiler_params=_cparams(("parallel", "parallel")),
    )(own_idx, *args)


def _place():
    return lax.axis_index("x"), lax.axis_index("y"), lax.axis_index("c")


HBM = pl.BlockSpec(memory_space=pltpu.HBM)
SEM = pl.BlockSpec(memory_space=pltpu.SEMAPHORE)
VMEM_SPEC = pl.BlockSpec(memory_space=pltpu.VMEM)
EFFECT = pltpu.SideEffectType.DATAFLOW_SIDE_EFFECTING
TOKEN = jax.ShapeDtypeStruct((8, LANES), f32)


def _hbm(x):
    return pltpu.with_memory_space_constraint(x, pltpu.HBM)


def _hbm_like(xs):
    return [pltpu.HBM(x.shape, x.dtype) for x in xs]


def _slot(px, py, pc):
    return 4 * px + 2 * py + pc


def _halves(ref):
    n = ref.shape[0]
    cut = n // 2 if n < 32 else (n // 2) // 16 * 16
    return ref.at[pl.ds(0, cut)], ref.at[pl.ds(cut, n - cut)]


def _gather_start(shards, after, name):
    n = len(shards)
    me = _slot(*_place())
    bufs = [lax.dynamic_update_slice(lax.empty((N_DEV,) + s.shape, s.dtype), s[None], (me,) + (0,) * s.ndim) for s in shards]

    def body(*refs):
        ins, land = refs[:n], refs[n:2 * n]
        send, recv = refs[2 * n + 1], refs[2 * n + 2]
        token = refs[-1]
        x, y, c = _place()
        to = [(x, y, 1 - c), (1 - x, y, c), (x, 1 - y, c)]
        for a in range(n):
            for k, dev in enumerate(to):
                pltpu.make_async_remote_copy(
                    src_ref=ins[a], dst_ref=land[a].at[_slot(x, y, c)], send_sem=send.at[3 * a + k], recv_sem=recv.at[3 * a + k],
                    device_id=dev, device_id_type=MESH).start()
        token[...] = jnp.zeros_like(token)

    out = pl.pallas_call(
        body, name=name,
        in_specs=[HBM] * (2 * n) + [ANY],
        out_specs=[SEM, SEM] + [HBM] * (2 * n) + [VMEM_SPEC],
        out_shape=[pltpu.SemaphoreType.DMA((3 * n,)), pltpu.SemaphoreType.DMA((3 * n,))] + _hbm_like(shards) + _hbm_like(bufs) + [TOKEN],
        input_output_aliases={i: 2 + i for i in range(2 * n)},
        compiler_params=pltpu.CompilerParams(has_side_effects=EFFECT),
    )(*[_hbm(s) for s in shards], *[_hbm(b) for b in bufs], after)
    return dict(send1=out[0], recv1=out[1], shards=list(out[2:2 + n]), bufs=list(out[2 + n:2 + 2 * n]), token=out[-1])


def _gather_mid(h, after, name):
    n = len(h["bufs"])

    def body(*refs):
        land, recv1 = refs[:n], refs[n]
        send2, recv2 = refs[n + 2], refs[n + 3]
        token = refs[-1]
        x, y, c = _place()
        nbr = [(1 - x, y, c), (x, 1 - y, c)]
        for j, dev in enumerate(nbr):
            for a in range(n):
                blk = land[a].at[_slot(*dev)]
                pltpu.make_async_remote_copy(
                    src_ref=blk, dst_ref=blk, send_sem=send2.at[4 * a + j], recv_sem=recv1.at[3 * a + 1 + j],
                    device_id=dev, device_id_type=MESH).wait_recv()
                pltpu.make_async_remote_copy(
                    src_ref=blk, dst_ref=blk, send_sem=send2.at[4 * a + j], recv_sem=recv2.at[4 * a + j],
                    device_id=(x, y, 1 - c), device_id_type=MESH).start()
        for a in range(n):
            from_x, from_y = land[a].at[_slot(*nbr[0])], land[a].at[_slot(*nbr[1])]
            for k, (half, dev) in enumerate([(_halves(from_y)[0], nbr[0]), (_halves(from_x)[1], nbr[1])]):
                pltpu.make_async_remote_copy(
                    src_ref=half, dst_ref=half, send_sem=send2.at[4 * a + 2 + k], recv_sem=recv2.at[4 * a + 2 + k],
                    device_id=dev, device_id_type=MESH).start()
        token[...] = jnp.zeros_like(token)

    out = pl.pallas_call(
        body, name=name,
        in_specs=[HBM] * n + [SEM, ANY],
        out_specs=[SEM, SEM] + [HBM] * n + [VMEM_SPEC],
        out_shape=[pltpu.SemaphoreType.DMA((4 * n,)), pltpu.SemaphoreType.DMA((4 * n,))] + _hbm_like(h["bufs"]) + [TOKEN],
        input_output_aliases={i: 2 + i for i in range(n)},
        compiler_params=pltpu.CompilerParams(has_side_effects=EFFECT),
    )(*h["bufs"], h["recv1"], after)
    h.update(send2=out[0], recv2=out[1], bufs=list(out[2:2 + n]), token=out[-1])
    return h


def _gather_mid2(h, after, name):
    n = len(h["bufs"])

    def body(*refs):
        land, recv2 = refs[:n], refs[n]
        send3, recv3 = refs[n + 2], refs[n + 3]
        token = refs[-1]
        x, y, c = _place()
        for a in range(n):
            blk = land[a].at[_slot(1 - x, 1 - y, c)]
            for k, half in enumerate(_halves(blk)):
                pltpu.make_async_remote_copy(
                    src_ref=half, dst_ref=half, send_sem=send3.at[a], recv_sem=recv2.at[4 * a + 2 + k],
                    device_id=(x, y, 1 - c), device_id_type=MESH).wait_recv()
            pltpu.make_async_remote_copy(
                src_ref=blk, dst_ref=blk, send_sem=send3.at[a], recv_sem=recv3.at[a],
                device_id=(x, y, 1 - c), device_id_type=MESH).start()
        token[...] = jnp.zeros_like(token)

    out = pl.pallas_call(
        body, name=name,
        in_specs=[HBM] * n + [SEM, ANY],
        out_specs=[SEM, SEM] + [HBM] * n + [VMEM_SPEC],
        out_shape=[pltpu.SemaphoreType.DMA((n,)), pltpu.SemaphoreType.DMA((n,))] + _hbm_like(h["bufs"]) + [TOKEN],
        input_output_aliases={i: 2 + i for i in range(n)},
        compiler_params=pltpu.CompilerParams(has_side_effects=EFFECT),
    )(*h["bufs"], h["recv2"], after)
    h.update(send3=out[0], recv3=out[1], bufs=list(out[2:2 + n]), token=out[-1])
    return h


def _gather_end(h, after, name):
    n = len(h["bufs"])

    def body(*refs):
        ins, land = refs[:n], refs[n:2 * n]
        send1, recv1, send2, recv2, send3, recv3 = refs[2 * n:2 * n + 6]
        x, y, c = _place()
        sib = (x, y, 1 - c)
        nbr = [(1 - x, y), (x, 1 - y)]

        def wait(src, dst, ssem, rsem, send):
            cp = pltpu.make_async_remote_copy(src_ref=src, dst_ref=dst, send_sem=ssem, recv_sem=rsem, device_id=sib, device_id_type=MESH)
            cp.wait_send() if send else cp.wait_recv()

        for a in range(n):
            mine = land[a].at[_slot(x, y, c)]
            for k in range(3):
                wait(ins[a], mine, send1.at[3 * a + k], recv1.at[3 * a + k], True)
            wait(ins[a], land[a].at[_slot(x, y, 1 - c)], send1.at[3 * a], recv1.at[3 * a], False)
            for j, (px, py) in enumerate(nbr):
                sent = land[a].at[_slot(px, py, c)]
                wait(sent, sent, send2.at[4 * a + j], recv2.at[4 * a + j], True)
                wait(sent, land[a].at[_slot(px, py, 1 - c)], send2.at[4 * a + j], recv2.at[4 * a + j], False)
            halves = [_halves(land[a].at[_slot(*nbr[1], c)])[0], _halves(land[a].at[_slot(*nbr[0], c)])[1]]
            for k, half in enumerate(halves):
                wait(half, half, send2.at[4 * a + 2 + k], recv2.at[4 * a + 2 + k], True)
            diag = land[a].at[_slot(1 - x, 1 - y, c)]
            wait(diag, diag, send3.at[a], recv3.at[a], True)
            wait(diag, land[a].at[_slot(1 - x, 1 - y, 1 - c)], send3.at[a], recv3.at[a], False)

    out = pl.pallas_call(
        body, name=name,
        in_specs=[HBM] * (2 * n) + [SEM] * 6 + [ANY],
        out_specs=[HBM] * n,
        out_shape=_hbm_like(h["bufs"]),
        input_output_aliases={n + i: i for i in range(n)},
        compiler_params=pltpu.CompilerParams(has_side_effects=EFFECT),
    )(*h["shards"], *h["bufs"], h["send1"], h["recv1"], h["send2"], h["recv2"], h["send3"], h["recv3"], after)
    return list(out)


def _peer_plan(kind, x, y, c):
    if kind == "pair":
        return [(2 * q + (1 - c), q, (x, y, 1 - c)) for q in range(4)]
    chips = [(1 - x, y), (x, 1 - y), (1 - x, 1 - y)]
    return [(2 * px + py, k, (px, py, c)) for k, (px, py) in enumerate(chips)]


def _exchange_start(kind, srcs, after, name):
    n = len(srcs)
    K = 4 if kind == "pair" else 3
    lands = [_hbm(lax.empty((K,) + s.shape[1:], s.dtype)) for s in srcs]

    def body(*refs):
        ins, land = refs[:n], refs[n:2 * n]
        send, recv = refs[2 * n + 1], refs[2 * n + 2]
        token = refs[-1]
        for a in range(n):
            for k, (si, di, dev) in enumerate(_peer_plan(kind, *_place())):
                pltpu.make_async_remote_copy(
                    src_ref=ins[a].at[si], dst_ref=land[a].at[di], send_sem=send.at[K * a + k], recv_sem=recv.at[K * a + k],
                    device_id=dev, device_id_type=MESH).start()
        token[...] = jnp.zeros_like(token)

    out = pl.pallas_call(
        body, name=name,
        in_specs=[HBM] * (2 * n) + [ANY],
        out_specs=[SEM, SEM] + [HBM] * (2 * n) + [VMEM_SPEC],
        out_shape=[pltpu.SemaphoreType.DMA((K * n,)), pltpu.SemaphoreType.DMA((K * n,))] + _hbm_like(srcs) + _hbm_like(lands) + [TOKEN],
        input_output_aliases={i: 2 + i for i in range(2 * n)},
        compiler_params=pltpu.CompilerParams(has_side_effects=EFFECT),
    )(*[_hbm(s) for s in srcs], *lands, after)
    return dict(kind=kind, send=out[0], recv=out[1], srcs=list(out[2:2 + n]), lands=list(out[2 + n:2 + 2 * n]), token=out[-1])


def _exchange_wait(h, after, name):
    n = len(h["srcs"])
    kind = h["kind"]
    K = 4 if kind == "pair" else 3

    def body(*refs):
        ins, land = refs[:n], refs[n:2 * n]
        send, recv = refs[2 * n], refs[2 * n + 1]
        for a in range(n):
            for k, (si, di, dev) in enumerate(_peer_plan(kind, *_place())):
                cp = pltpu.make_async_remote_copy(
                    src_ref=ins[a].at[si], dst_ref=land[a].at[di], send_sem=send.at[K * a + k], recv_sem=recv.at[K * a + k],
                    device_id=dev, device_id_type=MESH)
                cp.wait_send()
                cp.wait_recv()

    out = pl.pallas_call(
        body, name=name,
        in_specs=[HBM] * (2 * n) + [SEM, SEM, ANY],
        out_specs=[HBM] * (2 * n),
        out_shape=_hbm_like(h["srcs"]) + _hbm_like(h["lands"]),
        input_output_aliases={i: i for i in range(2 * n)},
        compiler_params=pltpu.CompilerParams(has_side_effects=EFFECT),
    )(*h["srcs"], *h["lands"], h["send"], h["recv"], after)
    return list(out[:n]), list(out[n:])


def _pair_add(gs, gots, c_idx, name):
    n = len(gs)
    _, R, C = gs[0].shape
    tr, tc = _tile2(R, C, 512, 16)

    def body(c_ref, *refs):
        for a in range(n):
            refs[2 * n + a][0] = (refs[a][0].astype(f32) + refs[n + a][0].astype(f32)).astype(bf16)

    mine = pl.BlockSpec((1, tr, tc), lambda q, i, j, c: (2 * q + c[0], i, j))
    blk = pl.BlockSpec((1, tr, tc), lambda q, i, j, c: (q, i, j))
    return pl.pallas_call(
        body, name=name,
        grid_spec=pltpu.PrefetchScalarGridSpec(
            num_scalar_prefetch=1, grid=(4, R // tr, C // tc),
            in_specs=[mine] * n + [blk] * n, out_specs=[blk] * n),
        out_shape=[jax.ShapeDtypeStruct((4, R, C), bf16)] * n,
        compiler_params=_cparams(("parallel", "parallel", "parallel")),
    )(c_idx, *gs, *gots)


def _small_exchange(send, gather, name, after=None):
    R = send.shape[-2]

    def body(in_ref, out_ref, send_sems, recv_sems):
        x, y, c = _place()
        me = 4 * x + 2 * y + c
        out_ref[me] = in_ref[...] if gather else in_ref[me]
        cps = []
        for k in range(1, N_DEV):
            px, py, pc = x ^ ((k >> 2) & 1), y ^ ((k >> 1) & 1), c ^ (k & 1)
            src = in_ref if gather else in_ref.at[4 * px + 2 * py + pc]
            cps.append(pltpu.make_async_remote_copy(
                src_ref=src, dst_ref=out_ref.at[me],
                send_sem=send_sems.at[k - 1], recv_sem=recv_sems.at[k - 1],
                device_id=(px, py, pc), device_id_type=MESH))
        for cp in cps:
            cp.start()
        for cp in cps:
            cp.wait()

    body, in_specs, args = _ordered(body, [pl.BlockSpec(memory_space=pltpu.VMEM)], [send], after)
    return pl.pallas_call(
        body, name=name,
        in_specs=in_specs, out_specs=pl.BlockSpec(memory_space=pltpu.VMEM),
        out_shape=jax.ShapeDtypeStruct((N_DEV, R, LANES), f32),
        scratch_shapes=[pltpu.SemaphoreType.DMA((N_DEV - 1,)), pltpu.SemaphoreType.DMA((N_DEV - 1,))],
    )(*args)


def _sum_blocks(blocks, name):
    def body(in_ref, o_ref):
        s = in_ref[0]
        for d in range(1, N_DEV):
            s = s + in_ref[d]
        o_ref[0] = s

    return pl.pallas_call(body, name=name, out_shape=jax.ShapeDtypeStruct((1,) + blocks.shape[1:], f32))(blocks)


def _rows(n):
    return -(-n // LANES)


def _pack(arrs, total_rows, lead=0):
    head = arrs[0].shape[:lead]
    parts = []
    for a in arrs:
        flat = a.reshape(head + (-1,)).astype(f32)
        n = flat.shape[-1]
        parts.append(jnp.pad(flat, [(0, 0)] * lead + [(0, _rows(n) * LANES - n)]))
    flat = jnp.concatenate(parts, axis=-1)
    flat = jnp.pad(flat, [(0, 0)] * lead + [(0, total_rows * LANES - flat.shape[-1])])
    return flat.reshape(head + (total_rows, LANES))


def _unpack(packed, shapes):
    lead = packed.shape[:-2]
    flat = packed.reshape(lead + (-1,))
    out, pos = [], 0
    for s in shapes:
        n = 1
        for d in s:
            n *= d
        out.append(flat[..., pos:pos + n].reshape(lead + tuple(s)))
        pos += _rows(n) * LANES
    return out


def _to_shards(full, axis):
    s = full.shape
    return jnp.moveaxis(full.reshape(s[:axis] + (N_DEV, s[axis] // N_DEV) + s[axis + 1:]), axis, 0)


def _from_shards(sh, axis):
    m = jnp.moveaxis(sh, 0, axis)
    s = m.shape
    return m.reshape(s[:axis] + (s[axis] * s[axis + 1],) + s[axis + 2:])


def kernel(x, meta, ffn_norm, ffn_w_gate, ffn_w_up, ffn_w_down, gla_norm, gla_w_in, gla_w_lr, gla_b_lr, gla_head_norm, gla_w_out, pool_norm, pool_w, pool_b, pool_scale, final_norm, loss_target, m_meta, m_ffn_norm, m_ffn_w_gate, m_ffn_w_up, m_ffn_w_down, m_gla_norm, m_gla_w_in, m_gla_w_lr, m_gla_b_lr, m_gla_head_norm, m_gla_w_out, m_pool_norm, m_pool_w, m_pool_b, m_pool_scale, m_final_norm, v_meta, v_ffn_norm, v_ffn_w_gate, v_ffn_w_up, v_ffn_w_down, v_gla_norm, v_gla_w_in, v_gla_w_lr, v_gla_b_lr, v_gla_head_norm, v_gla_w_out, v_pool_norm, v_pool_w, v_pool_b, v_pool_scale, v_final_norm):
    H = GLA_HEADS
    _, SEQ, D = x.shape
    Fs = ffn_w_gate.shape[-1]
    DK, DV = D // 2, D
    hv = DV // H
    GW = D // POOL_GROUPS
    INW = 2 * DK + 2 * DV + GATE_RANK
    NPK = 2 * DK + 2 * DV + GATE_PAD
    pad = (-N_META) % GLA_CHUNK
    first = pad + N_META
    Lp = first + SEQ
    n_units = ffn_w_gate.shape[0] * ffn_w_gate.shape[1]
    assert first % GLA_CHUNK == 0 and Lp % GLA_CHUNK == 0 and pad >= POOL_GROUPS * 4

    px, py, pc = _place()
    c_idx = jnp.reshape(pc, (1,)).astype(jnp.int32)
    q_idx = jnp.reshape(2 * px + py, (1,)).astype(jnp.int32)
    zero_idx = jnp.zeros((1,), jnp.int32)

    small_sh = [meta, ffn_norm, gla_w_lr, pool_norm, pool_b, pool_scale]
    small_axis = [1, 2, 2, 1, 2, 1]
    sh_shapes = [a.shape for a in small_sh]
    sh_rows = -(-sum(_rows(a.size) for a in small_sh) // 8) * 8
    gathered = _small_exchange(_pack(small_sh, sh_rows), True, "small_gather")
    meta_f, ffn_norm_f, wlr_f, pool_norm_f, pool_b_f, pool_scale_f = [
        _from_shards(a, ax) for a, ax in zip(_unpack(gathered, sh_shapes), small_axis)]
    ffn_norm_f = ffn_norm_f.reshape(n_units, 1, D)
    wlr128 = jnp.pad(wlr_f[0], ((0, GATE_PAD - GATE_RANK), (0, 0)))

    def t_units(w):
        return jnp.swapaxes(w, -1, -2).reshape(n_units, Fs, D)

    ffn_f32 = [t_units(ffn_w_gate), t_units(ffn_w_up), ffn_w_down.reshape(n_units, Fs, D)]
    mixer_f32 = [gla_w_in[0].T[None], gla_w_out, pool_w[0].reshape(1, -1, GW)]
    gather_order = [("ffn0", ffn_f32, 0), ("mixers", mixer_f32, 0)] + [(f"ffn{u}", ffn_f32, u) for u in range(1, n_units)]
    c_lr = 2 * DK + DV
    c_r = 2 * DK + 2 * DV
    gate_blk = c_r // GATE_PAD

    def cast_shards(i, after):
        tag, arrays, u = gather_order[i]
        shards = [_cast_unit(w, u, f"cast_{tag}_{a}", after) for a, w in enumerate(arrays)]
        if tag == "mixers":
            shards[2] = shards[2].reshape(pool_w.shape[1:])
        return shards

    def pass_on(i, h, after):
        tag = gather_order[i][0]
        nxt = later_shards[i + 1] if i + 1 < len(gather_order) else None
        h = _gather_mid(h, after, f"gather_mid_{tag}")
        if nxt is not None:
            nxt = _gather_start(nxt, h["token"], f"gather_start_{gather_order[i + 1][0]}")
        return h, nxt

    def complete(i, h, after):
        tag = gather_order[i][0]
        h = _gather_mid2(h, after, f"gather_mid2_{tag}")
        return _gather_end(h, h["token"], f"gather_end_{tag}")

    xs = jnp.concatenate([jnp.zeros((pad, D), f32), meta_f, x[0]], axis=0)
    saved = {}
    ffn_w = [None] * n_units

    def ffn_f(u, xs, after=None):
        out, h, G, U = _ffn_fwd(xs, ffn_norm_f[u], *ffn_w[u], name=f"ffn_fwd{u}", after=after)
        saved[("ffn", u)] = (xs, h, G, U)
        return out

    def gla_f(xs, win_p, wout_full, after=None):
        hn = _rms_fwd(xs, gla_norm, bf16, "gla_norm_fwd", after=after)
        proj = _mm(hn, win_p, "nt", f32, "gla_proj", tm=1056, tn=896, tk=2048)
        lg = _gate_fwd(proj, wlr128, gla_b_lr, pad, gate_blk, "gla_gate_fwd")
        o, y, states = _gla_fwd(proj, lg, gla_head_norm, H, "gla_core_fwd")
        out = _mm(y, wout_full, "nn", f32, "gla_out", tm=1056, tn=512, tk=2048, residual=xs)
        saved["gla"] = (xs, hn, proj, lg, o, y, states)
        return out

    def pool_f(xs, wpool_full):
        hn = _rms_fwd(xs, pool_norm_f, f32, "pool_norm_fwd")
        pooled = _pool_windows(hn, pad, "pool_windows_fwd")
        out = _pool_mix_fwd(xs, pooled, wpool_full, pool_b_f.reshape(1, D), pool_scale_f, pad, "pool_mix_fwd")
        saved["pool"] = (xs, pooled)
        return out

    depth = ffn_w_gate.shape[0]
    assert depth == 2 and n_units == 4
    h0 = _gather_start(cast_shards(0, None), gathered, "gather_start_ffn0")
    later_shards = {}
    last = h0["token"]
    for i in range(1, len(gather_order)):
        later_shards[i] = cast_shards(i, last)
        last = later_shards[i][0]
    h0, h1 = pass_on(0, h0, last)
    ffn_w[0] = complete(0, h0, h1["token"])
    h1, h2 = pass_on(1, h1, ffn_w[0][0])
    xs = ffn_f(0, xs, after=h2["token"])
    win_g, wout_g, wpool_g = complete(1, h1, xs)
    h2, h3 = pass_on(2, h2, wout_g)
    win_full = win_g.reshape(INW, D)
    win_p = jnp.concatenate([win_full[:c_lr], win_full[c_lr + GATE_RANK:], win_full[c_lr:c_lr + GATE_RANK],
                             jnp.zeros((GATE_PAD - GATE_RANK, D), bf16)], axis=0)
    wout_full = wout_g.reshape(DV, D)
    wpool_full = _from_shards(wpool_g, 1)
    xs = gla_f(xs, win_p, wout_full, after=h3["token"])
    ffn_w[1] = complete(2, h2, xs)
    h3, h4 = pass_on(3, h3, ffn_w[1][0])
    xs = ffn_f(1, xs, after=h4["token"])
    ffn_w[2] = complete(3, h3, xs)
    h4, _ = pass_on(4, h4, ffn_w[2][0])
    xs = ffn_f(2, xs, after=h4["token"])
    xs = pool_f(xs, wpool_full)
    ffn_w[3] = complete(4, h4, xs)
    xs = ffn_f(3, xs)
    loss_part, dxs, d_final, dyh = _loss_head(xs, loss_target[0], final_norm.reshape(1, D), first, "loss_head")

    class Reduce:
        def __init__(self, tag, grads, after=None):
            self.tag = tag
            self.h = _exchange_start("pair", grads, loss_part if after is None else after, f"pair_start_{tag}")
            self.token = self.h["token"]

        def mid(self, after):
            grads, got = _exchange_wait(self.h, after, f"pair_wait_{self.tag}")
            if len({g.shape for g in grads}) == 1:
                self.sums = list(_pair_add(grads, got, c_idx, f"pair_add_{self.tag}"))
            else:
                self.sums = [_pair_add([g], [r], c_idx, f"pair_add_{self.tag}{a}")[0] for a, (g, r) in enumerate(zip(grads, got))]
            self.h = _exchange_start("chips", self.sums, loss_part, f"chips_start_{self.tag}")
            self.token = self.h["token"]

        def end(self, after):
            sums, recv = _exchange_wait(self.h, after, f"chips_wait_{self.tag}")
            return list(zip(sums, recv))

    d_ffn_norm = [None] * n_units
    small_grads = {}

    def ffn_b(u, dY, dyh, prev):
        xs_in, h_, G, U = saved[("ffn", u)]
        wg, wu, wd = ffn_w[u]
        tok = None if prev is None else prev.token
        dG, dU, A = _ffn_bwd_act(dyh, wd, G, U, f"ffn_act{u}", after=tok)
        dh = _ffn_bwd_dh(dG, dU, wg, wu, f"ffn_dh{u}")
        dxs, dg, dyh_next = _rms_bwd(dY, dh, xs_in, ffn_norm_f[u], pad, f"ffn_norm_bwd{u}")
        if prev is not None:
            prev.mid(dxs)
            tok = prev.token
        dwg = _ffn_bwd_wgrad(dG, h_, f"ffn_wgrad_gate{u}", after=tok)
        dwu = _ffn_bwd_wgrad(dU, h_, f"ffn_wgrad_up{u}", after=tok)
        dwd = _ffn_bwd_wgrad(A, dyh, f"ffn_wgrad_down{u}", after=tok)
        d_ffn_norm[u] = dg
        return dxs, dyh_next, Reduce(f"ffn{u}", [dwg, dwu, dwd])

    def gla_b(dY, prev):
        xs_in, hn, proj, lg, o, y, states = saved["gla"]
        dyb = dY.astype(bf16)
        dy = _mm(dyb, wout_full, "nt", f32, "gla_out_dgrad", tm=1056, tn=512, tk=2048, after=prev.token)
        dwout = _mm_tn_full(y, dyb, "gla_out_wgrad", 1024, after=prev.token)
        prev.mid(dwout)
        dq, dk, dv, dr, dlg, dhw = _gla_bwd(dy, proj, lg, o, states, gla_head_norm, H, pad, "gla_core_bwd", after=prev.token)
        dlr, dwlr, dblr = _gate_bwd(dlg, proj, wlr128, gla_b_lr, pad, gate_blk, "gla_gate_bwd")
        dproj = jnp.concatenate([dq, dk, dv, dr, dlr], axis=1)
        dwin_p = _mm_tn_full(dproj, hn, "gla_proj_wgrad", 896)
        dhn = _mm(dproj, win_p, "nn", f32, "gla_proj_dgrad", tm=1056, tn=1024, tk=896)
        dxs, dgn, dyh_next = _rms_bwd(dY, dhn, xs_in, gla_norm, pad, "gla_norm_bwd")
        dwin = jnp.concatenate([dwin_p[:c_lr], dwin_p[c_r:c_r + GATE_RANK], dwin_p[c_lr:c_r]], axis=0)
        small_grads.update(gla_w_lr=dwlr[:GATE_RANK][None], gla_b_lr=dblr, gla_head_norm=dhw, gla_norm=dgn)
        return dxs, dyh_next, Reduce("gla", [dwin.reshape(N_DEV, INW // N_DEV, D), dwout.reshape(N_DEV, DV // N_DEV, D)])

    def pool_b_(dY, prev):
        xs_in, pooled = saved["pool"]
        dp, dw, db, ds = _pool_mix_bwd(dY, pooled, wpool_full, pool_b_f.reshape(1, D), pool_scale_f, pad, "pool_mix_bwd",
                                       after=prev.token)
        dhn = _pool_windows_bwd(dp, pad, "pool_windows_bwd")
        dxs, dgn, dyh_next = _rms_bwd(dY, dhn, xs_in, pool_norm_f, pad, "pool_norm_bwd")
        prev.mid(dxs)
        dws = _to_shards(dw, 1)
        small_grads.update(pool_b=db.reshape(1, POOL_GROUPS, GW), pool_scale=ds, pool_norm=dgn)
        return dxs, dyh_next, Reduce("pool", [dws.reshape(N_DEV, POOL_GROUPS * GW // N_DEV, GW)], after=prev.token)

    sh_names = ["meta", "ffn_norm", "gla_w_lr", "pool_norm", "pool_b", "pool_scale"]
    rep_names = ["gla_norm", "gla_b_lr", "gla_head_norm", "final_norm"]
    rep_w = [gla_norm, gla_b_lr, gla_head_norm, final_norm]
    rep_shapes = [a.shape for a in rep_w]
    rep_rows = -(-sum(_rows(a.size) for a in rep_w) // 8) * 8

    def small_path(dxs0):
        small_grads.update(meta=dxs0[pad:first], ffn_norm=jnp.concatenate(d_ffn_norm, axis=0).reshape(n_units // 2, 2, D),
                           final_norm=d_final.reshape(D))
        by_owner = [_to_shards(small_grads[nm].reshape(full_shape), ax) for nm, full_shape, ax in zip(
            sh_names, [meta_f.shape, (ffn_norm.shape[0], 2, D), wlr_f.shape, pool_norm_f.shape, pool_b_f.shape, pool_scale_f.shape],
            small_axis)]
        rep_pack = _pack([small_grads[nm].reshape(s) for nm, s in zip(rep_names, rep_shapes)], rep_rows)
        tail_rows = jnp.concatenate([rep_pack, loss_part], axis=0)
        send = jnp.concatenate([_pack(by_owner, sh_rows, lead=1), jnp.broadcast_to(tail_rows, (N_DEV,) + tail_rows.shape)], axis=1)
        total = _sum_blocks(_small_exchange(send, False, "small_reduce"), "small_sum")
        n_small = sh_rows + rep_rows

        def pack_small(sh_list, rep_list):
            return jnp.concatenate([_pack(sh_list, sh_rows), _pack(rep_list, rep_rows)], axis=0)[None]

        w_small = pack_small(small_sh, rep_w)
        m_small = pack_small([m_meta, m_ffn_norm, m_gla_w_lr, m_pool_norm, m_pool_b, m_pool_scale],
                             [m_gla_norm, m_gla_b_lr, m_gla_head_norm, m_final_norm])
        v_small = pack_small([v_meta, v_ffn_norm, v_gla_w_lr, v_pool_norm, v_pool_b, v_pool_scale],
                             [v_gla_norm, v_gla_b_lr, v_gla_head_norm, v_final_norm])
        small_out = _adamw(w_small, m_small, v_small, 0, total[:, :n_small], zero_idx, None, None, "adamw_small")
        small_res = {}
        for kind, packed in zip(("grad", "delta", "new_m", "new_v"), small_out):
            sh_vals = _unpack(packed[0, :sh_rows], sh_shapes)
            rep_vals = _unpack(packed[0, sh_rows:], rep_shapes)
            for nm, val in zip(sh_names + rep_names, sh_vals + rep_vals):
                small_res[(kind, nm)] = val
        return total[0, n_small, 0], small_res, small_out[0]

    def ffn_b_last(dY, dyh, prev):
        xs_in, h_, G, U = saved[("ffn", 0)]
        wg, wu, wd = ffn_w[0]
        dG, dU, A = _ffn_bwd_act(dyh, wd, G, U, "ffn_act0", after=prev.token)
        prev.mid(dG)
        dwd = _ffn_bwd_wgrad(A, dyh, "ffn_wgrad_down0", after=prev.token)
        r_d = Reduce("ffn0_down", [dwd])
        dh = _ffn_bwd_dh(dG, dU, wg, wu, "ffn_dh0", after=r_d.token)
        dxs, dg, _ = _rms_bwd(dY, dh, xs_in, ffn_norm_f[0], pad, "ffn_norm_bwd0")
        d_ffn_norm[0] = dg
        small = small_path(dxs)
        r_d.mid(small[2])
        dwg = _ffn_bwd_wgrad(dG, h_, "ffn_wgrad_gate0", after=r_d.token)
        r_g = Reduce("ffn0_gate", [dwg])
        dwu = _ffn_bwd_wgrad(dU, h_, "ffn_wgrad_up0", after=r_g.token)
        r_g.mid(dwu)
        r_u = Reduce("ffn0_up", [dwu], after=r_g.token)
        return dxs, small, (r_g, r_u, r_d)

    dxs, dyh, r3 = ffn_b(3, dxs, dyh, None)
    dxs, dyh, rp = pool_b_(dxs, r3)
    dxs, dyh, r2 = ffn_b(2, dxs, dyh, rp)
    dxs, dyh, r1 = ffn_b(1, dxs, dyh, r2)
    dxs, dyh, rg = gla_b(dxs, r1)
    dxs, (loss, small_res, _), r0 = ffn_b_last(dxs, dyh, rg)
    grad_x = dxs[first:].reshape(x.shape)
    r_last = r0[1]

    big_res = {}

    def adam_one(nm, w, m, v, entry, transposed=False):
        sums, recv = entry
        R, C = sums.shape[1:]
        w1, m1, v1 = ((t[0].T if transposed else t).reshape(1, R, C) for t in (w, m, v))
        out = _adamw(w1, m1, v1, 0, sums, q_idx, recv, None, f"adamw_{nm}", after=r_last.token)
        for kind, val in zip(("grad", "delta", "new_m", "new_v"), out):
            big_res[(kind, nm)] = val[0].T[None] if transposed else val.reshape(w.shape)
        return out[0]

    e_gla = rg.end(dxs)
    done = adam_one("gla_w_in", gla_w_in, m_gla_w_in, v_gla_w_in, e_gla[0], transposed=True)
    done = adam_one("gla_w_out", gla_w_out, m_gla_w_out, v_gla_w_out, e_gla[1])
    done = adam_one("pool_w", pool_w, m_pool_w, v_pool_w, rp.end(done)[0])
    r_last.mid(done)

    ffn_names = ["ffn_w_gate", "ffn_w_up", "ffn_w_down"]
    ffn_wmv = [tuple(t_units(t) for t in (ffn_w_gate, m_ffn_w_gate, v_ffn_w_gate)),
               tuple(t_units(t) for t in (ffn_w_up, m_ffn_w_up, v_ffn_w_up)),
               tuple(t.reshape(n_units, Fs, D) for t in (ffn_w_down, m_ffn_w_down, v_ffn_w_down))]
    ffn_prev = [[lax.empty((n_units, Fs, D), f32) for _ in range(4)] for _ in range(3)]
    order_after = r_last.token
    for u, red in ((3, r3), (2, r2), (1, r1), (0, r0)):
        entries = [r.end(done)[0] for r in red] if u == 0 else red.end(done)
        for a in range(3):
            sums, recv = entries[a]
            ffn_prev[a] = _adamw(*ffn_wmv[a], u, sums, q_idx, recv, ffn_prev[a], f"adamw_{ffn_names[a]}{u}", after=order_after)
            done = order_after = ffn_prev[a][0]
    for a in range(3):
        for kind, val in zip(("grad", "delta", "new_m", "new_v"), ffn_prev[a]):
            val = val.reshape(ffn_w_down.shape)
            big_res[(kind, ffn_names[a])] = val if a == 2 else jnp.swapaxes(val, -1, -2)

    order = ["meta", "ffn_norm", "ffn_w_gate", "ffn_w_up", "ffn_w_down", "gla_norm", "gla_w_in", "gla_w_lr", "gla_b_lr",
             "gla_head_norm", "gla_w_out", "pool_norm", "pool_w", "pool_b", "pool_scale", "final_norm"]
    res = {**small_res, **big_res}
    outs = [loss, grad_x]
    for kind in ("grad", "delta", "new_m", "new_v"):
        outs += [res[(kind, nm)] for nm in order]
    return tuple(outs)
```

```python
import jax
import jax.numpy as jnp
from jax import lax
from jax.experimental import pallas as pl
from jax.experimental.pallas import tpu as pltpu

f32 = jnp.float32
bf16 = jnp.bfloat16

N_DEV = 8
N_META = 16
GLA_HEADS = 4
GLA_CHUNK = 64
GLA_SUB = 16
GLA_HEADS_PER_STEP = 4
GATE_RANK = 16
GATE_PAD = 128
GATE_NORM = 16.0
EPS = 1e-6
POOL_GROUPS = 4
ADAM_LR = 0.001
ADAM_B1 = 0.9
ADAM_B2 = 0.999
ADAM_EPS = 1e-08
ADAM_WD = 0.01
ADAM_STEP = 10
LANES = 128
VMEM_LIMIT_MB = 56

NN = (((1,), (0,)), ((), ()))
NT = (((1,), (1,)), ((), ()))
TN = (((0,), (0,)), ((), ()))
HI = lax.Precision.HIGHEST
MESH = pl.DeviceIdType.MESH
ANY = pl.BlockSpec(memory_space=pl.ANY)


def _cparams(sem=None, vmem_mb=None):
    kw = {}
    if sem is not None:
        kw["dimension_semantics"] = sem
    if vmem_mb is not None:
        kw["vmem_limit_bytes"] = vmem_mb * 2 ** 20
    return pltpu.CompilerParams(**kw)


def _tile(n, target, mult=16):
    best = None
    for t in range(mult, min(n, target) + 1, mult):
        if n % t == 0:
            best = t
    assert best is not None, (n, target, mult)
    return best


def _tile2(R, C, rows, mult):
    if R % mult == 0:
        return _tile(R, rows, mult), C
    return R, _tile(C, 256, LANES)


def _dot(a, b, dims=NN, precision=None):
    return lax.dot_general(a, b, dims, preferred_element_type=f32, precision=precision)


def _sigmoid(x):
    return 1.0 / (1.0 + jnp.exp(-x))


def _row_ids(tile_index, tm):
    return tile_index * tm + lax.broadcasted_iota(jnp.int32, (tm, 1), 0)


def _ordered(body, in_specs, args, after, lead=0):
    if after is None:
        return body, in_specs, args
    pos = lead + len(args)

    def body_without(*refs):
        return body(*refs[:pos], *refs[pos + 1:])

    return body_without, list(in_specs) + [ANY], list(args) + [after]


def _cast_unit(w, unit, name, after=None):
    _, R, C = w.shape
    tr, tc = _tile2(R, C, 256, 16)

    def body(w_ref, o_ref):
        o_ref[...] = w_ref[0].astype(bf16)

    body, in_specs, args = _ordered(body, [pl.BlockSpec((1, tr, tc), lambda i, j: (unit, i, j))], [w], after)
    return pl.pallas_call(
        body, name=name, grid=(R // tr, C // tc),
        in_specs=in_specs, out_specs=pl.BlockSpec((tr, tc), lambda i, j: (i, j)),
        out_shape=jax.ShapeDtypeStruct((R, C), bf16),
        compiler_params=_cparams(("parallel", "parallel")),
    )(*args)


def _rms_fwd(xs, g, out_dtype, name, after=None):
    Lp, D = xs.shape
    tm = _tile(Lp, 528)

    def body(x_ref, g_ref, h_ref):
        x = x_ref[...]
        rstd = lax.rsqrt(jnp.mean(x * x, axis=-1, keepdims=True) + EPS)
        h_ref[...] = (x * rstd * g_ref[...]).astype(out_dtype)

    in_specs = [pl.BlockSpec((tm, D), lambda i: (i, 0)), pl.BlockSpec((1, D), lambda i: (0, 0))]
    body, in_specs, args = _ordered(body, in_specs, [xs, g], after)
    return pl.pallas_call(
        body, name=name, grid=(Lp // tm,),
        in_specs=in_specs,
        out_specs=pl.BlockSpec((tm, D), lambda i: (i, 0)),
        out_shape=jax.ShapeDtypeStruct((Lp, D), out_dtype),
        compiler_params=_cparams(("parallel",)),
    )(*args)


def _rms_bwd(dY, dh, xs, g, pad, name):
    Lp, D = xs.shape
    tm = _tile(Lp, 352)

    def body(dY_ref, dh_ref, x_ref, g_ref, dxs_ref, dg_ref, half_ref):
        i = pl.program_id(0)

        @pl.when(i == 0)
        def _():
            dg_ref[...] = jnp.zeros_like(dg_ref)

        x = x_ref[...]
        rstd = lax.rsqrt(jnp.mean(x * x, axis=-1, keepdims=True) + EPS)
        xhat = x * rstd
        dh_ = dh_ref[...]
        dg_ref[...] += jnp.sum(dh_ * xhat, axis=0, keepdims=True)
        dxh = dh_ * g_ref[...]
        dx = rstd * (dxh - xhat * jnp.mean(dxh * xhat, axis=-1, keepdims=True))
        out = jnp.where(_row_ids(i, tm) >= pad, dY_ref[...] + dx, 0.0)
        dxs_ref[...] = out
        half_ref[...] = (0.5 * out).astype(bf16)

    row = pl.BlockSpec((tm, D), lambda i: (i, 0))
    vec = pl.BlockSpec((1, D), lambda i: (0, 0))
    return pl.pallas_call(
        body, name=name, grid=(Lp // tm,),
        in_specs=[row, row, row, vec], out_specs=[row, vec, row],
        out_shape=[jax.ShapeDtypeStruct((Lp, D), f32), jax.ShapeDtypeStruct((1, D), f32), jax.ShapeDtypeStruct((Lp, D), bf16)],
        compiler_params=_cparams(("arbitrary",)),
    )(dY, dh, xs, g)


def _mm(a, b, mode, out_dtype, name, tm=512, tn=512, tk=512, residual=None, after=None):
    if mode == "nn":
        (M, K), N = a.shape, b.shape[1]
    elif mode == "nt":
        (M, K), N = a.shape, b.shape[0]
    else:
        (K, M), N = a.shape, b.shape[1]
    tm = _tile(M, tm, 16 if mode != "tn" else LANES) if M > tm else M
    tn = _tile(N, tn, LANES) if N > tn else N
    tk = _tile(K, tk, LANES if mode != "tn" else 16) if K > tk else K
    nk = K // tk
    dims = {"nn": NN, "nt": NT, "tn": TN}[mode]

    def body(*refs):
        if residual is None:
            a_ref, b_ref, o_ref, acc = refs
            r_ref = None
        else:
            a_ref, b_ref, r_ref, o_ref, acc = refs
        k = pl.program_id(2)

        @pl.when(k == 0)
        def _():
            acc[...] = jnp.zeros_like(acc)

        acc[...] += _dot(a_ref[...], b_ref[...], dims)

        @pl.when(k == nk - 1)
        def _():
            r = acc[...]
            if r_ref is not None:
                r = r + r_ref[...]
            o_ref[...] = r.astype(out_dtype)

    a_spec = pl.BlockSpec((tk, tm), lambda i, j, k: (k, i)) if mode == "tn" else pl.BlockSpec((tm, tk), lambda i, j, k: (i, k))
    b_spec = pl.BlockSpec((tn, tk), lambda i, j, k: (j, k)) if mode == "nt" else pl.BlockSpec((tk, tn), lambda i, j, k: (k, j))
    o_spec = pl.BlockSpec((tm, tn), lambda i, j, k: (i, j))
    in_specs = [a_spec, b_spec] + ([o_spec] if residual is not None else [])
    args = [a, b] + ([residual] if residual is not None else [])
    body, in_specs, args = _ordered(body, in_specs, args, after)
    return pl.pallas_call(
        body, name=name, grid=(M // tm, N // tn, nk),
        in_specs=in_specs, out_specs=o_spec,
        out_shape=jax.ShapeDtypeStruct((M, N), out_dtype),
        scratch_shapes=[pltpu.VMEM((tm, tn), f32)],
        compiler_params=_cparams(("parallel", "parallel", "arbitrary"), VMEM_LIMIT_MB),
    )(*args)


def _mm_tn_full(a, b, name, tm, after=None):
    K, M = a.shape
    N = b.shape[1]
    tm = _tile(M, tm, LANES)

    def body(a_ref, b_ref, o_ref):
        o_ref[...] = _dot(a_ref[...], b_ref[...], TN).astype(bf16)

    in_specs = [pl.BlockSpec((K, tm), lambda i: (0, i)), pl.BlockSpec((K, N), lambda i: (0, 0), pipeline_mode=pl.Buffered(1))]
    body, in_specs, args = _ordered(body, in_specs, [a, b], after)
    return pl.pallas_call(
        body, name=name, grid=(M // tm,),
        in_specs=in_specs, out_specs=pl.BlockSpec((tm, N), lambda i: (i, 0)),
        out_shape=jax.ShapeDtypeStruct((M, N), bf16),
        compiler_params=_cparams(("parallel",), VMEM_LIMIT_MB),
    )(*args)


def _ffn_fwd(xs, g, wg, wu, wd, name, after=None):
    Lp, D = xs.shape
    nd, Fs, _ = wg.shape
    tm = _tile(Lp, 704)
    once = pl.Buffered(1)

    def body(x_ref, g_ref, wg_ref, wu_ref, wd_ref, out_ref, h_ref, G_ref, U_ref, hs, acc):
        j = pl.program_id(1)

        @pl.when(j == 0)
        def _():
            x = x_ref[...]
            rstd = lax.rsqrt(jnp.mean(x * x, axis=-1, keepdims=True) + EPS)
            h = (x * rstd * g_ref[...]).astype(bf16)
            hs[...] = h
            h_ref[...] = h
            acc[...] = jnp.zeros_like(acc)

        h = hs[...]
        G = _dot(h, wg_ref[0], NT)
        U = _dot(h, wu_ref[0], NT)
        G_ref[0] = G.astype(bf16)
        U_ref[0] = U.astype(bf16)
        A = (G * _sigmoid(G) * U).astype(bf16)
        acc[...] += _dot(A, wd_ref[0])

        @pl.when(j == nd - 1)
        def _():
            out_ref[...] = x_ref[...] + 0.5 * acc[...]

    row_f = pl.BlockSpec((tm, D), lambda i, j: (i, 0), pipeline_mode=once)
    act = pl.BlockSpec((1, tm, Fs), lambda i, j: (j, i, 0))
    wrow = pl.BlockSpec((1, Fs, D), lambda i, j: (j, 0, 0))
    in_specs = [row_f, pl.BlockSpec((1, D), lambda i, j: (0, 0)), wrow, wrow, wrow]
    body, in_specs, args = _ordered(body, in_specs, [xs, g, wg, wu, wd], after)
    return pl.pallas_call(
        body, name=name, grid=(Lp // tm, nd),
        in_specs=in_specs,
        out_specs=[row_f, pl.BlockSpec((tm, D), lambda i, j: (i, 0), pipeline_mode=once), act, act],
        out_shape=[jax.ShapeDtypeStruct((Lp, D), f32), jax.ShapeDtypeStruct((Lp, D), bf16),
                   jax.ShapeDtypeStruct((nd, Lp, Fs), bf16), jax.ShapeDtypeStruct((nd, Lp, Fs), bf16)],
        scratch_shapes=[pltpu.VMEM((tm, D), bf16), pltpu.VMEM((tm, D), f32)],
        compiler_params=_cparams(("parallel", "arbitrary"), VMEM_LIMIT_MB),
    )(*args)


def _ffn_bwd_act(dyh, wd, G, U, name, after=None):
    Lp, D = dyh.shape
    nd, Fs, _ = wd.shape
    tm = _tile(Lp, 704)

    def body(dyh_ref, wd_ref, G_ref, U_ref, dG_ref, dU_ref, A_ref):
        dA = _dot(dyh_ref[...], wd_ref[0], NT)
        Gf = G_ref[0].astype(f32)
        Uf = U_ref[0].astype(f32)
        s = _sigmoid(Gf)
        silu = Gf * s
        dG_ref[0] = (dA * Uf * (s * (1.0 + Gf * (1.0 - s)))).astype(bf16)
        dU_ref[0] = (dA * silu).astype(bf16)
        A_ref[0] = (silu * Uf).astype(bf16)

    act = pl.BlockSpec((1, tm, Fs), lambda j, i: (j, i, 0))
    act_s = jax.ShapeDtypeStruct((nd, Lp, Fs), bf16)
    in_specs = [pl.BlockSpec((tm, D), lambda j, i: (i, 0)), pl.BlockSpec((1, Fs, D), lambda j, i: (j, 0, 0)), act, act]
    body, in_specs, args = _ordered(body, in_specs, [dyh, wd, G, U], after)
    return pl.pallas_call(
        body, name=name, grid=(nd, Lp // tm),
        in_specs=in_specs, out_specs=[act, act, act], out_shape=[act_s, act_s, act_s],
        compiler_params=_cparams(("parallel", "parallel"), VMEM_LIMIT_MB),
    )(*args)


def _ffn_bwd_dh(dG, dU, wg, wu, name, after=None):
    nd, Lp, Fs = dG.shape
    D = wg.shape[2]
    tm = _tile(Lp, 1056)

    def body(dG_ref, dU_ref, wg_ref, wu_ref, dh_ref, acc):
        j = pl.program_id(1)

        @pl.when(j == 0)
        def _():
            acc[...] = jnp.zeros_like(acc)

        acc[...] += _dot(dG_ref[0], wg_ref[0]) + _dot(dU_ref[0], wu_ref[0])

        @pl.when(j == nd - 1)
        def _():
            dh_ref[...] = acc[...]

    act = pl.BlockSpec((1, tm, Fs), lambda i, j: (j, i, 0))
    wrow = pl.BlockSpec((1, Fs, D), lambda i, j: (j, 0, 0))
    body, in_specs, args = _ordered(body, [act, act, wrow, wrow], [dG, dU, wg, wu], after)
    return pl.pallas_call(
        body, name=name, grid=(Lp // tm, nd),
        in_specs=in_specs,
        out_specs=pl.BlockSpec((tm, D), lambda i, j: (i, 0), pipeline_mode=pl.Buffered(1)),
        out_shape=jax.ShapeDtypeStruct((Lp, D), f32),
        scratch_shapes=[pltpu.VMEM((tm, D), f32)],
        compiler_params=_cparams(("parallel", "arbitrary"), VMEM_LIMIT_MB),
    )(*args)


def _ffn_bwd_wgrad(act, rows, name, after=None):
    nd, Lp, Fs = act.shape
    D = rows.shape[1]

    def body(a_ref, r_ref, o_ref):
        o_ref[0] = _dot(a_ref[0], r_ref[...], TN).astype(bf16)

    in_specs = [pl.BlockSpec((1, Lp, Fs), lambda j: (j, 0, 0)),
                pl.BlockSpec((Lp, D), lambda j: (0, 0), pipeline_mode=pl.Buffered(1))]
    body, in_specs, args = _ordered(body, in_specs, [act, rows], after)
    return pl.pallas_call(
        body, name=name, grid=(nd,),
        in_specs=in_specs, out_specs=pl.BlockSpec((1, Fs, D), lambda j: (j, 0, 0)),
        out_shape=jax.ShapeDtypeStruct((nd, Fs, D), bf16),
        compiler_params=_cparams(("parallel",), VMEM_LIMIT_MB),
    )(*args)


def _gate_fwd(proj, wlr, blr, pad, gate_blk, name):
    Lp = proj.shape[0]
    DK = wlr.shape[1]
    tm = _tile(Lp, 528)

    def body(lr_ref, w_ref, b_ref, lg_ref):
        z = _dot(lr_ref[...].astype(bf16), w_ref[...].astype(bf16)) + b_ref[...]
        ls = jnp.minimum(z, 0.0) - jnp.log(1.0 + jnp.exp(-jnp.abs(z)))
        lg_ref[...] = jnp.where(_row_ids(pl.program_id(0), tm) >= pad, ls * (1.0 / GATE_NORM), 0.0)

    return pl.pallas_call(
        body, name=name, grid=(Lp // tm,),
        in_specs=[pl.BlockSpec((tm, GATE_PAD), lambda i: (i, gate_blk)),
                  pl.BlockSpec((GATE_PAD, DK), lambda i: (0, 0)), pl.BlockSpec((1, DK), lambda i: (0, 0))],
        out_specs=pl.BlockSpec((tm, DK), lambda i: (i, 0)),
        out_shape=jax.ShapeDtypeStruct((Lp, DK), f32),
        compiler_params=_cparams(("parallel",)),
    )(proj, wlr, blr)


def _gate_bwd(dlg, proj, wlr, blr, pad, gate_blk, name):
    Lp = proj.shape[0]
    DK = wlr.shape[1]
    tm = _tile(Lp, 528)

    def body(dlg_ref, lr_ref, w_ref, b_ref, dlr_ref, dw_ref, db_ref):
        i = pl.program_id(0)

        @pl.when(i == 0)
        def _():
            dw_ref[...] = jnp.zeros_like(dw_ref)
            db_ref[...] = jnp.zeros_like(db_ref)

        lr = lr_ref[...].astype(bf16)
        w = w_ref[...].astype(bf16)
        z = _dot(lr, w) + b_ref[...]
        dz = jnp.where(_row_ids(i, tm) >= pad, dlg_ref[...] * _sigmoid(-z) * (1.0 / GATE_NORM), 0.0)
        dzb = dz.astype(bf16)
        dlr_ref[...] = _dot(dzb, w, NT).astype(bf16)
        dw_ref[...] += _dot(lr, dzb, TN)
        db_ref[...] += jnp.sum(dz, axis=0, keepdims=True)

    return pl.pallas_call(
        body, name=name, grid=(Lp // tm,),
        in_specs=[pl.BlockSpec((tm, DK), lambda i: (i, 0)), pl.BlockSpec((tm, GATE_PAD), lambda i: (i, gate_blk)),
                  pl.BlockSpec((GATE_PAD, DK), lambda i: (0, 0)), pl.BlockSpec((1, DK), lambda i: (0, 0))],
        out_specs=[pl.BlockSpec((tm, GATE_PAD), lambda i: (i, 0)), pl.BlockSpec((GATE_PAD, DK), lambda i: (0, 0)),
                   pl.BlockSpec((1, DK), lambda i: (0, 0))],
        out_shape=[jax.ShapeDtypeStruct((Lp, GATE_PAD), bf16), jax.ShapeDtypeStruct((GATE_PAD, DK), f32),
                   jax.ShapeDtypeStruct((1, DK), f32)],
        compiler_params=_cparams(("arbitrary",)),
    )(dlg, proj, wlr, blr)


def _chunk_decay(lg):
    C = lg.shape[0]
    r = lax.broadcasted_iota(jnp.int32, (C, C), 0)
    c = lax.broadcasted_iota(jnp.int32, (C, C), 1)
    return _dot(jnp.where(r >= c, 1.0, 0.0).astype(f32), lg, NN, HI)


def _col(v):
    return jnp.transpose(jnp.broadcast_to(v, (8, v.shape[1])))[:, 0:1]


def _intra_scores(q, k, b, A_ref):
    C = q.shape[0]
    S = GLA_SUB
    A_ref[...] = jnp.zeros_like(A_ref)
    ri = lax.broadcasted_iota(jnp.int32, (S, 1), 0)
    for I in range(C // S):
        lo = S * I
        qI, bI = q[lo:lo + S], b[lo:lo + S]
        if I > 0:
            bref = b[lo - 1:lo]
            qs = qI * jnp.exp(bI - bref)
            ks = k[:lo] * jnp.exp(bref - b[:lo])
            A_ref[lo:lo + S, 0:lo] = _dot(qs, ks, NT, HI)
        for jj in range(S):
            j = lo + jj
            P = jnp.exp(jnp.minimum(bI - b[j:j + 1], 0.0))
            a = jnp.sum(qI * P * k[j:j + 1], axis=1, keepdims=True)
            A_ref[lo:lo + S, j:j + 1] = jnp.where(ri >= jj, a, 0.0)


def _intra_grads(q, k, b, dA, dq_ref, dk_ref):
    C = q.shape[0]
    S = GLA_SUB
    ri = lax.broadcasted_iota(jnp.int32, (S, 1), 0)
    for I in range(C // S):
        lo = S * I
        qI, bI = q[lo:lo + S], b[lo:lo + S]
        dqI = jnp.zeros_like(qI)
        if I > 0:
            bref = b[lo - 1:lo]
            eq = jnp.exp(bI - bref)
            ek = jnp.exp(bref - b[:lo])
            qs = qI * eq
            ks = k[:lo] * ek
            dAI = dA[lo:lo + S, 0:lo]
            dqI = dqI + _dot(dAI, ks, NN, HI) * eq
            dk_ref[0:lo, :] += _dot(dAI, qs, TN, HI) * ek
        for jj in range(S):
            j = lo + jj
            P = jnp.exp(jnp.minimum(bI - b[j:j + 1], 0.0))
            t = jnp.where(ri >= jj, dA[lo:lo + S, j:j + 1], 0.0) * P
            dqI = dqI + t * k[j:j + 1]
            dk_ref[j:j + 1, :] += jnp.sum(t * qI, axis=0, keepdims=True)
        dq_ref[lo:lo + S, :] += dqI


def _gla_fwd(proj, lg, hnw, H, name):
    Lp = proj.shape[0]
    DK = lg.shape[1]
    hk = DK // H
    hv = hnw.shape[1]
    DV = hv * H
    C = GLA_CHUNK
    NC = Lp // C
    HS = min(GLA_HEADS_PER_STEP, H)
    G = H // HS
    scale = float(hk) ** -0.5
    kq, kv, kr = G, (2 * DK) // (HS * hv), (2 * DK) // (HS * hv) + G

    def body(q_ref, k_ref, v_ref, r_ref, lg_ref, w_ref, o_ref, y_ref, s_ref, S_scr, A_scr):
        c = pl.program_id(1)

        @pl.when(c == 0)
        def _():
            S_scr[...] = jnp.zeros_like(S_scr)

        for hh in range(HS):
            ck, cv = slice(hh * hk, (hh + 1) * hk), slice(hh * hv, (hh + 1) * hv)
            q = q_ref[:, ck] * scale
            k = k_ref[:, ck]
            v = v_ref[:, cv]
            b = _chunk_decay(lg_ref[:, ck])
            bl = b[C - 1:C]
            S = S_scr[hh]
            s_ref[hh, 0] = S
            _intra_scores(q, k, b, A_scr.at[hh])
            vb = v.astype(bf16)
            o = _dot((q * jnp.exp(b)).astype(bf16), S.astype(bf16)) + _dot(A_scr[hh].astype(bf16), vb)
            kb = (k * jnp.exp(bl - b)).astype(bf16)
            S_scr[hh] = jnp.exp(_col(bl)) * S + _dot(kb, vb, TN)
            o_ref[:, cv] = o
            on = o * lax.rsqrt(jnp.mean(o * o, axis=-1, keepdims=True) + EPS) * w_ref[...]
            r = r_ref[:, cv]
            y_ref[:, cv] = (on * (r * _sigmoid(r))).astype(bf16)

    return pl.pallas_call(
        body, name=name, grid=(G, NC),
        in_specs=[pl.BlockSpec((C, HS * hk), lambda g, c: (c, g)),
                  pl.BlockSpec((C, HS * hk), lambda g, c: (c, kq + g)),
                  pl.BlockSpec((C, HS * hv), lambda g, c: (c, kv + g)),
                  pl.BlockSpec((C, HS * hv), lambda g, c: (c, kr + g)),
                  pl.BlockSpec((C, HS * hk), lambda g, c: (c, g)),
                  pl.BlockSpec((1, hv), lambda g, c: (0, 0))],
        out_specs=[pl.BlockSpec((C, HS * hv), lambda g, c: (c, g)), pl.BlockSpec((C, HS * hv), lambda g, c: (c, g)),
                   pl.BlockSpec((HS, 1, hk, hv), lambda g, c: (g, c, 0, 0))],
        out_shape=[jax.ShapeDtypeStruct((Lp, DV), f32), jax.ShapeDtypeStruct((Lp, DV), bf16),
                   jax.ShapeDtypeStruct((H, NC, hk, hv), f32)],
        scratch_shapes=[pltpu.VMEM((HS, hk, hv), f32), pltpu.VMEM((HS, C, C), f32)],
        compiler_params=_cparams(("parallel", "arbitrary")),
    )(proj, proj, proj, proj, lg, hnw)


def _gla_bwd(dy, proj, lg, o, states, hnw, H, pad, name, after=None):
    Lp = proj.shape[0]
    DK = lg.shape[1]
    hk = DK // H
    hv = hnw.shape[1]
    DV = hv * H
    C = GLA_CHUNK
    NC = Lp // C
    HS = min(GLA_HEADS_PER_STEP, H)
    G = H // HS
    scale = float(hk) ** -0.5
    kq, kv, kr = G, (2 * DK) // (HS * hv), (2 * DK) // (HS * hv) + G

    def body(dy_ref, q_ref, k_ref, v_ref, r_ref, lg_ref, o_ref, s_ref, sn_ref, w_ref,
             dq_ref, dk_ref, dv_ref, dr_ref, dlg_ref, dw_ref, dS_scr, A_scr, dq_s, dk_s):
        g = pl.program_id(0)
        cc = pl.program_id(1)
        c = NC - 1 - cc

        @pl.when(cc == 0)
        def _():
            dS_scr[...] = jnp.zeros_like(dS_scr)

        @pl.when((cc == 0) & (g == 0))
        def _():
            dw_ref[...] = jnp.zeros_like(dw_ref)

        keep = (c * C + lax.broadcasted_iota(jnp.int32, (C, 1), 0)) >= pad
        ri = lax.broadcasted_iota(jnp.int32, (C, C), 0)
        ci = lax.broadcasted_iota(jnp.int32, (C, C), 1)
        w = w_ref[...]
        for hh in range(HS):
            ck, cv = slice(hh * hk, (hh + 1) * hk), slice(hh * hv, (hh + 1) * hv)
            o_ = o_ref[:, cv]
            rs = lax.rsqrt(jnp.mean(o_ * o_, axis=-1, keepdims=True) + EPS)
            ohat = o_ * rs
            r = r_ref[:, cv]
            sg = _sigmoid(r)
            dy_ = dy_ref[:, cv]
            d_on = dy_ * (r * sg)
            dr_ref[:, cv] = jnp.where(keep, dy_ * (ohat * w) * (sg * (1.0 + r * (1.0 - sg))), 0.0).astype(bf16)
            dw_ref[...] += jnp.sum(d_on * ohat, axis=0, keepdims=True)
            d_oh = d_on * w
            do = rs * (d_oh - ohat * jnp.mean(d_oh * ohat, axis=-1, keepdims=True))
            dob = do.astype(bf16)
            q = q_ref[:, ck] * scale
            k = k_ref[:, ck]
            vb = v_ref[:, cv].astype(bf16)
            b = _chunk_decay(lg_ref[:, ck])
            bl = b[C - 1:C]
            eb = jnp.exp(b)
            ekb = jnp.exp(bl - b)
            S = s_ref[hh, 0]
            dS = dS_scr[hh]
            dSb = dS.astype(bf16)
            _intra_scores(q, k, b, A_scr.at[hh])
            dA = jnp.where(ri >= ci, _dot(dob, vb, NT), 0.0)
            kb = (k * ekb).astype(bf16)
            qb = (q * eb).astype(bf16)
            dv = _dot(A_scr[hh].astype(bf16), dob, TN) + _dot(kb, dSb)
            dq_s[hh] = _dot(dob, S.astype(bf16), NT) * eb
            dk_s[hh] = _dot(vb, dSb, NT) * ekb
            dS_scr[hh] = _dot(qb, dob, TN) + jnp.exp(_col(bl)) * dS
            _intra_grads(q, k, b, dA, dq_s.at[hh], dk_s.at[hh])
            dq = dq_s[hh]
            dk = dk_s[hh]
            Dm = q * dq - k * dk
            after_rows = _dot(jnp.ones((8, hv), f32), sn_ref[hh, 0] * dS, NT, HI)[0:1]
            dlg = _dot(jnp.where(ri <= ci, 1.0, 0.0).astype(f32), Dm, NN, HI) + after_rows
            dlg_ref[:, ck] = jnp.where(keep, dlg, 0.0)
            dq_ref[:, ck] = jnp.where(keep, dq * scale, 0.0).astype(bf16)
            dk_ref[:, ck] = jnp.where(keep, dk, 0.0).astype(bf16)
            dv_ref[:, cv] = jnp.where(keep, dv, 0.0).astype(bf16)

    rev = lambda cc: NC - 1 - cc
    bk = lambda off: pl.BlockSpec((C, HS * hk), lambda g, cc: (rev(cc), off + g))
    bv = lambda off: pl.BlockSpec((C, HS * hv), lambda g, cc: (rev(cc), off + g))
    in_specs = [bv(0), bk(0), bk(kq), bv(kv), bv(kr), bk(0), bv(0),
                pl.BlockSpec((HS, 1, hk, hv), lambda g, cc: (g, rev(cc), 0, 0)),
                pl.BlockSpec((HS, 1, hk, hv), lambda g, cc: (g, jnp.minimum(rev(cc) + 1, NC - 1), 0, 0)),
                pl.BlockSpec((1, hv), lambda g, cc: (0, 0))]
    body, in_specs, args = _ordered(body, in_specs, [dy, proj, proj, proj, proj, lg, o, states, states, hnw], after)
    return pl.pallas_call(
        body, name=name, grid=(G, NC),
        in_specs=in_specs,
        out_specs=[bk(0), bk(0), bv(0), bv(0), bk(0), pl.BlockSpec((1, hv), lambda g, cc: (0, 0))],
        out_shape=[jax.ShapeDtypeStruct((Lp, DK), bf16), jax.ShapeDtypeStruct((Lp, DK), bf16),
                   jax.ShapeDtypeStruct((Lp, DV), bf16), jax.ShapeDtypeStruct((Lp, DV), bf16),
                   jax.ShapeDtypeStruct((Lp, DK), f32), jax.ShapeDtypeStruct((1, hv), f32)],
        scratch_shapes=[pltpu.VMEM((HS, hk, hv), f32), pltpu.VMEM((HS, C, C), f32),
                        pltpu.VMEM((HS, C, hk), f32), pltpu.VMEM((HS, C, hk), f32)],
        compiler_params=_cparams(("arbitrary", "arbitrary")),
    )(*args)


def _window_sums(x, back):
    n = x.shape[0]
    out = []
    s = x
    for w in (1, 2, 4, 8):
        s = s + pltpu.roll(s, w if back else n - w, 0)
        out.append(s)
    return out


def _pool_windows(hn, pad, name):
    Lp, D = hn.shape
    GW = D // POOL_GROUPS
    cb = min(GW, 256)
    per = GW // cb

    def body(h_ref, p_ref):
        g = pl.program_id(0) // per
        x = h_ref[...]
        s2, s4, s8, s16 = _window_sums(x, True)
        sel = jnp.where(g == 0, s2, jnp.where(g == 1, s4, jnp.where(g == 2, s8, s16)))
        win = jnp.left_shift(2, g).astype(f32)
        rows = lax.broadcasted_iota(jnp.int32, (Lp, 1), 0)
        t = (rows - pad).astype(f32)
        cnt = jnp.minimum(jnp.maximum(t, 0.0) + 1.0, win)
        p_ref[...] = jnp.where(rows >= pad, sel / cnt - x, 0.0).astype(bf16)

    return pl.pallas_call(
        body, name=name, grid=(D // cb,),
        in_specs=[pl.BlockSpec((Lp, cb), lambda i: (0, i))],
        out_specs=pl.BlockSpec((Lp, cb), lambda i: (0, i)),
        out_shape=jax.ShapeDtypeStruct((Lp, D), bf16),
        compiler_params=_cparams(("parallel",)),
    )(hn)


def _pool_windows_bwd(dp, pad, name):
    Lp, D = dp.shape
    GW = D // POOL_GROUPS
    cb = min(GW, 256)
    per = GW // cb

    def body(dp_ref, dh_ref):
        g = pl.program_id(0) // per
        rows = lax.broadcasted_iota(jnp.int32, (Lp, 1), 0)
        d = jnp.where(rows >= pad, dp_ref[...], 0.0)
        win = jnp.left_shift(2, g).astype(f32)
        t = (rows - pad).astype(f32)
        cnt = jnp.minimum(jnp.maximum(t, 0.0) + 1.0, win)
        s2, s4, s8, s16 = _window_sums(d / cnt, False)
        sel = jnp.where(g == 0, s2, jnp.where(g == 1, s4, jnp.where(g == 2, s8, s16)))
        dh_ref[...] = jnp.where(rows >= pad, sel - d, 0.0)

    return pl.pallas_call(
        body, name=name, grid=(D // cb,),
        in_specs=[pl.BlockSpec((Lp, cb), lambda i: (0, i))],
        out_specs=pl.BlockSpec((Lp, cb), lambda i: (0, i)),
        out_shape=jax.ShapeDtypeStruct((Lp, D), f32),
        compiler_params=_cparams(("parallel",)),
    )(dp)


def _pool_mix_fwd(xs, pooled, w, bias, scale, pad, name):
    Lp, D = xs.shape
    GW = D // POOL_GROUPS
    tm = _tile(Lp, 1056)

    def body(x_ref, p_ref, w_ref, b_ref, s_ref, o_ref):
        z = _dot(p_ref[...], w_ref[0]) + b_ref[...]
        keep = _row_ids(pl.program_id(1), tm) >= pad
        o_ref[...] = x_ref[...] + jnp.where(keep, z * s_ref[...], 0.0)

    blk = pl.BlockSpec((tm, GW), lambda g, i: (i, g))
    vec = pl.BlockSpec((1, GW), lambda g, i: (0, g))
    return pl.pallas_call(
        body, name=name, grid=(POOL_GROUPS, Lp // tm),
        in_specs=[blk, blk, pl.BlockSpec((1, GW, GW), lambda g, i: (g, 0, 0)), vec, vec],
        out_specs=blk, out_shape=jax.ShapeDtypeStruct((Lp, D), f32),
        compiler_params=_cparams(("parallel", "parallel")),
    )(xs, pooled, w, bias, scale)


def _pool_mix_bwd(dY, pooled, w, bias, scale, pad, name, after=None):
    Lp, D = dY.shape
    GW = D // POOL_GROUPS
    tm = _tile(Lp, 1056)
    nm = Lp // tm

    def body(dY_ref, p_ref, w_ref, b_ref, s_ref, dp_ref, dw_ref, db_ref, ds_ref, acc):
        i = pl.program_id(1)

        @pl.when(i == 0)
        def _():
            acc[...] = jnp.zeros_like(acc)
            db_ref[...] = jnp.zeros_like(db_ref)
            ds_ref[...] = jnp.zeros_like(ds_ref)

        keep = _row_ids(i, tm) >= pad
        dY_ = jnp.where(keep, dY_ref[...], 0.0)
        p = p_ref[...]
        z = _dot(p, w_ref[0]) + b_ref[...]
        ds_ref[...] += jnp.sum(dY_ * z, axis=0, keepdims=True)
        dz = dY_ * s_ref[...]
        db_ref[...] += jnp.sum(dz, axis=0, keepdims=True)
        dzb = dz.astype(bf16)
        acc[...] += _dot(p, dzb, TN)
        dp_ref[...] = _dot(dzb, w_ref[0], NT)

        @pl.when(i == nm - 1)
        def _():
            dw_ref[0] = acc[...].astype(bf16)

    blk = pl.BlockSpec((tm, GW), lambda g, i: (i, g))
    vec = pl.BlockSpec((1, GW), lambda g, i: (0, g))
    wsp = pl.BlockSpec((1, GW, GW), lambda g, i: (g, 0, 0))
    body, in_specs, args = _ordered(body, [blk, blk, wsp, vec, vec], [dY, pooled, w, bias, scale], after)
    return pl.pallas_call(
        body, name=name, grid=(POOL_GROUPS, nm),
        in_specs=in_specs, out_specs=[blk, wsp, vec, vec],
        out_shape=[jax.ShapeDtypeStruct((Lp, D), f32), jax.ShapeDtypeStruct((POOL_GROUPS, GW, GW), bf16),
                   jax.ShapeDtypeStruct((1, D), f32), jax.ShapeDtypeStruct((1, D), f32)],
        scratch_shapes=[pltpu.VMEM((GW, GW), f32)],
        compiler_params=_cparams(("parallel", "arbitrary")),
    )(*args)


def _loss_head(xs, target, g, first, name):
    Lp, D = xs.shape
    tm = GLA_CHUNK
    off = first // tm

    def body(x_ref, t_ref, g_ref, loss_ref, dxs_ref, dg_ref, half_ref):
        i = pl.program_id(0)

        @pl.when(i == 0)
        def _():
            loss_ref[...] = jnp.zeros_like(loss_ref)
            dg_ref[...] = jnp.zeros_like(dg_ref)

        @pl.when(i < off)
        def _():
            dxs_ref[...] = jnp.zeros_like(dxs_ref)
            half_ref[...] = jnp.zeros_like(half_ref)

        @pl.when(i >= off)
        def _():
            x = x_ref[...]
            rstd = lax.rsqrt(jnp.mean(x * x, axis=-1, keepdims=True) + EPS)
            xhat = x * rstd
            gg = g_ref[...]
            err = xhat * gg - t_ref[...]
            loss_ref[...] += 0.5 * jnp.sum(jnp.mean(err * err, axis=-1, keepdims=True))
            dy = err * (1.0 / D)
            dg_ref[...] += jnp.sum(dy * xhat, axis=0, keepdims=True)
            dxh = dy * gg
            out = rstd * (dxh - xhat * jnp.mean(dxh * xhat, axis=-1, keepdims=True))
            dxs_ref[...] = out
            half_ref[...] = (0.5 * out).astype(bf16)

    row = pl.BlockSpec((tm, D), lambda i: (i, 0))
    return pl.pallas_call(
        body, name=name, grid=(Lp // tm,),
        in_specs=[row, pl.BlockSpec((tm, D), lambda i: (jnp.maximum(i - off, 0), 0)), pl.BlockSpec((1, D), lambda i: (0, 0))],
        out_specs=[pl.BlockSpec((8, LANES), lambda i: (0, 0)), row, pl.BlockSpec((1, D), lambda i: (0, 0)), row],
        out_shape=[jax.ShapeDtypeStruct((8, LANES), f32), jax.ShapeDtypeStruct((Lp, D), f32),
                   jax.ShapeDtypeStruct((1, D), f32), jax.ShapeDtypeStruct((Lp, D), bf16)],
        compiler_params=_cparams(("arbitrary",)),
    )(xs, target, g)


def _adam_math(w, g, m, v):
    m2 = ADAM_B1 * m + (1.0 - ADAM_B1) * g
    v2 = ADAM_B2 * v + (1.0 - ADAM_B2) * (g * g)
    m_hat = m2 / (1.0 - ADAM_B1 ** ADAM_STEP)
    v_hat = v2 / (1.0 - ADAM_B2 ** ADAM_STEP)
    delta = -ADAM_LR * (m_hat / (jnp.sqrt(v_hat) + ADAM_EPS) + ADAM_WD * w)
    return delta, m2, v2


def _adamw(w, m, v, unit, own, own_idx, recv, prev, name, after=None):
    U, R, C = w.shape
    tr, tc = _tile2(R, C, 64, 8 if own.dtype == f32 and recv is None else 16)
    n_recv = 0 if recv is None else recv.shape[0]

    def body(idx_ref, w_ref, m_ref, v_ref, own_ref, *rest):
        rest = list(rest)
        recv_refs = [rest.pop(0) for _ in range(n_recv)]
        if prev is not None:
            rest = rest[4:]
        g_ref, d_ref, m2_ref, v2_ref = rest
        g = own_ref[0].astype(f32)
        for r_ref in recv_refs:
            g = g + r_ref[0].astype(f32)
        delta, m2, v2 = _adam_math(w_ref[0], g, m_ref[0], v_ref[0])
        g_ref[0] = g
        d_ref[0] = delta
        m2_ref[0] = m2
        v2_ref[0] = v2

    blk = pl.BlockSpec((1, tr, tc), lambda i, j, idx: (unit, i, j))
    in_specs = [blk, blk, blk, pl.BlockSpec((1, tr, tc), lambda i, j, idx: (idx[0], i, j))]
    args = [w, m, v, own]
    for p in range(n_recv):
        in_specs.append(pl.BlockSpec((1, tr, tc), lambda i, j, idx, p=p: (p, i, j)))
        args.append(recv)
    aliases = {}
    if prev is not None:
        for t in range(4):
            aliases[1 + len(args) + t] = t
        in_specs += [ANY] * 4
        args += list(prev)
    body, in_specs, args = _ordered(body, in_specs, args, after, lead=1)
    out = jax.ShapeDtypeStruct((U, R, C), f32)
    return pl.pallas_call(
        body, name=name,
        grid_spec=pltpu.PrefetchScalarGridSpec(
            num_scalar_prefetch=1, grid=(R // tr, C // tc), in_specs=in_specs, out_specs=[blk] * 4),
        out_shape=[out] * 4, input_output_aliases=aliases,
        compiler_params=_cparams(("parallel", "parallel")),
    )(own_idx, *args)


def _place():
    return lax.axis_index("x"), lax.axis_index("y"), lax.axis_index("c")


HBM = pl.BlockSpec(memory_space=pltpu.HBM)
SEM = pl.BlockSpec(memory_space=pltpu.SEMAPHORE)
VMEM_SPEC = pl.BlockSpec(memory_space=pltpu.VMEM)
EFFECT = pltpu.SideEffectType.DATAFLOW_SIDE_EFFECTING
TOKEN = jax.ShapeDtypeStruct((8, LANES), f32)


def _hbm(x):
    return pltpu.with_memory_space_constraint(x, pltpu.HBM)


def _hbm_like(xs):
    return [pltpu.HBM(x.shape, x.dtype) for x in xs]


def _slot(px, py, pc):
    return 4 * px + 2 * py + pc


def _halves(ref):
    n = ref.shape[0]
    cut = n // 2 if n < 32 else (n // 2) // 16 * 16
    return ref.at[pl.ds(0, cut)], ref.at[pl.ds(cut, n - cut)]


def _gather_start(shards, after, name):
    n = len(shards)
    me = _slot(*_place())
    bufs = [lax.dynamic_update_slice(lax.empty((N_DEV,) + s.shape, s.dtype), s[None], (me,) + (0,) * s.ndim) for s in shards]

    def body(*refs):
        ins, land = refs[:n], refs[n:2 * n]
        send, recv = refs[2 * n + 1], refs[2 * n + 2]
        token = refs[-1]
        x, y, c = _place()
        to = [(x, y, 1 - c), (1 - x, y, c), (x, 1 - y, c)]
        for a in range(n):
            for k, dev in enumerate(to):
                pltpu.make_async_remote_copy(
                    src_ref=ins[a], dst_ref=land[a].at[_slot(x, y, c)], send_sem=send.at[3 * a + k], recv_sem=recv.at[3 * a + k],
                    device_id=dev, device_id_type=MESH).start()
        token[...] = jnp.zeros_like(token)

    out = pl.pallas_call(
        body, name=name,
        in_specs=[HBM] * (2 * n) + [ANY],
        out_specs=[SEM, SEM] + [HBM] * (2 * n) + [VMEM_SPEC],
        out_shape=[pltpu.SemaphoreType.DMA((3 * n,)), pltpu.SemaphoreType.DMA((3 * n,))] + _hbm_like(shards) + _hbm_like(bufs) + [TOKEN],
        input_output_aliases={i: 2 + i for i in range(2 * n)},
        compiler_params=pltpu.CompilerParams(has_side_effects=EFFECT),
    )(*[_hbm(s) for s in shards], *[_hbm(b) for b in bufs], after)
    return dict(send1=out[0], recv1=out[1], shards=list(out[2:2 + n]), bufs=list(out[2 + n:2 + 2 * n]), token=out[-1])


def _gather_mid(h, after, name):
    n = len(h["bufs"])

    def body(*refs):
        land, recv1 = refs[:n], refs[n]
        send2, recv2 = refs[n + 2], refs[n + 3]
        token = refs[-1]
        x, y, c = _place()
        nbr = [(1 - x, y, c), (x, 1 - y, c)]
        for j, dev in enumerate(nbr):
            for a in range(n):
                blk = land[a].at[_slot(*dev)]
                pltpu.make_async_remote_copy(
                    src_ref=blk, dst_ref=blk, send_sem=send2.at[4 * a + j], recv_sem=recv1.at[3 * a + 1 + j],
                    device_id=dev, device_id_type=MESH).wait_recv()
                pltpu.make_async_remote_copy(
                    src_ref=blk, dst_ref=blk, send_sem=send2.at[4 * a + j], recv_sem=recv2.at[4 * a + j],
                    device_id=(x, y, 1 - c), device_id_type=MESH).start()
        for a in range(n):
            from_x, from_y = land[a].at[_slot(*nbr[0])], land[a].at[_slot(*nbr[1])]
            for k, (half, dev) in enumerate([(_halves(from_y)[0], nbr[0]), (_halves(from_x)[1], nbr[1])]):
                pltpu.make_async_remote_copy(
                    src_ref=half, dst_ref=half, send_sem=send2.at[4 * a + 2 + k], recv_sem=recv2.at[4 * a + 2 + k],
                    device_id=dev, device_id_type=MESH).start()
        token[...] = jnp.zeros_like(token)

    out = pl.pallas_call(
        body, name=name,
        in_specs=[HBM] * n + [SEM, ANY],
        out_specs=[SEM, SEM] + [HBM] * n + [VMEM_SPEC],
        out_shape=[pltpu.SemaphoreType.DMA((4 * n,)), pltpu.SemaphoreType.DMA((4 * n,))] + _hbm_like(h["bufs"]) + [TOKEN],
        input_output_aliases={i: 2 + i for i in range(n)},
        compiler_params=pltpu.CompilerParams(has_side_effects=EFFECT),
    )(*h["bufs"], h["recv1"], after)
    h.update(send2=out[0], recv2=out[1], bufs=list(out[2:2 + n]), token=out[-1])
    return h


def _gather_mid2(h, after, name):
    n = len(h["bufs"])

    def body(*refs):
        land, recv2 = refs[:n], refs[n]
        send3, recv3 = refs[n + 2], refs[n + 3]
        token = refs[-1]
        x, y, c = _place()
        for a in range(n):
            blk = land[a].at[_slot(1 - x, 1 - y, c)]
            for k, half in enumerate(_halves(blk)):
                pltpu.make_async_remote_copy(
                    src_ref=half, dst_ref=half, send_sem=send3.at[a], recv_sem=recv2.at[4 * a + 2 + k],
                    device_id=(x, y, 1 - c), device_id_type=MESH).wait_recv()
            pltpu.make_async_remote_copy(
                src_ref=blk, dst_ref=blk, send_sem=send3.at[a], recv_sem=recv3.at[a],
                device_id=(x, y, 1 - c), device_id_type=MESH).start()
        token[...] = jnp.zeros_like(token)

    out = pl.pallas_call(
        body, name=name,
        in_specs=[HBM] * n + [SEM, ANY],
        out_specs=[SEM, SEM] + [HBM] * n + [VMEM_SPEC],
        out_shape=[pltpu.SemaphoreType.DMA((n,)), pltpu.SemaphoreType.DMA((n,))] + _hbm_like(h["bufs"]) + [TOKEN],
        input_output_aliases={i: 2 + i for i in range(n)},
        compiler_params=pltpu.CompilerParams(has_side_effects=EFFECT),
    )(*h["bufs"], h["recv2"], after)
    h.update(send3=out[0], recv3=out[1], bufs=list(out[2:2 + n]), token=out[-1])
    return h


def _gather_end(h, after, name):
    n = len(h["bufs"])

    def body(*refs):
        ins, land = refs[:n], refs[n:2 * n]
        send1, recv1, send2, recv2, send3, recv3 = refs[2 * n:2 * n + 6]
        x, y, c = _place()
        sib = (x, y, 1 - c)
        nbr = [(1 - x, y), (x, 1 - y)]

        def wait(src, dst, ssem, rsem, send):
            cp = pltpu.make_async_remote_copy(src_ref=src, dst_ref=dst, send_sem=ssem, recv_sem=rsem, device_id=sib, device_id_type=MESH)
            cp.wait_send() if send else cp.wait_recv()

        for a in range(n):
            mine = land[a].at[_slot(x, y, c)]
            for k in range(3):
                wait(ins[a], mine, send1.at[3 * a + k], recv1.at[3 * a + k], True)
            wait(ins[a], land[a].at[_slot(x, y, 1 - c)], send1.at[3 * a], recv1.at[3 * a], False)
            for j, (px, py) in enumerate(nbr):
                sent = land[a].at[_slot(px, py, c)]
                wait(sent, sent, send2.at[4 * a + j], recv2.at[4 * a + j], True)
                wait(sent, land[a].at[_slot(px, py, 1 - c)], send2.at[4 * a + j], recv2.at[4 * a + j], False)
            halves = [_halves(land[a].at[_slot(*nbr[1], c)])[0], _halves(land[a].at[_slot(*nbr[0], c)])[1]]
            for k, half in enumerate(halves):
                wait(half, half, send2.at[4 * a + 2 + k], recv2.at[4 * a + 2 + k], True)
            diag = land[a].at[_slot(1 - x, 1 - y, c)]
            wait(diag, diag, send3.at[a], recv3.at[a], True)
            wait(diag, land[a].at[_slot(1 - x, 1 - y, 1 - c)], send3.at[a], recv3.at[a], False)

    out = pl.pallas_call(
        body, name=name,
        in_specs=[HBM] * (2 * n) + [SEM] * 6 + [ANY],
        out_specs=[HBM] * n,
        out_shape=_hbm_like(h["bufs"]),
        input_output_aliases={n + i: i for i in range(n)},
        compiler_params=pltpu.CompilerParams(has_side_effects=EFFECT),
    )(*h["shards"], *h["bufs"], h["send1"], h["recv1"], h["send2"], h["recv2"], h["send3"], h["recv3"], after)
    return list(out)


def _peer_plan(kind, x, y, c):
    if kind == "pair":
        return [(2 * q + (1 - c), q, (x, y, 1 - c)) for q in range(4)]
    chips = [(1 - x, y), (x, 1 - y), (1 - x, 1 - y)]
    return [(2 * px + py, k, (px, py, c)) for k, (px, py) in enumerate(chips)]


def _exchange_start(kind, srcs, after, name):
    n = len(srcs)
    K = 4 if kind == "pair" else 3
    lands = [_hbm(lax.empty((K,) + s.shape[1:], s.dtype)) for s in srcs]

    def body(*refs):
        ins, land = refs[:n], refs[n:2 * n]
        send, recv = refs[2 * n + 1], refs[2 * n + 2]
        token = refs[-1]
        for a in range(n):
            for k, (si, di, dev) in enumerate(_peer_plan(kind, *_place())):
                pltpu.make_async_remote_copy(
                    src_ref=ins[a].at[si], dst_ref=land[a].at[di], send_sem=send.at[K * a + k], recv_sem=recv.at[K * a + k],
                    device_id=dev, device_id_type=MESH).start()
        token[...] = jnp.zeros_like(token)

    out = pl.pallas_call(
        body, name=name,
        in_specs=[HBM] * (2 * n) + [ANY],
        out_specs=[SEM, SEM] + [HBM] * (2 * n) + [VMEM_SPEC],
        out_shape=[pltpu.SemaphoreType.DMA((K * n,)), pltpu.SemaphoreType.DMA((K * n,))] + _hbm_like(srcs) + _hbm_like(lands) + [TOKEN],
        input_output_aliases={i: 2 + i for i in range(2 * n)},
        compiler_params=pltpu.CompilerParams(has_side_effects=EFFECT),
    )(*[_hbm(s) for s in srcs], *lands, after)
    return dict(kind=kind, send=out[0], recv=out[1], srcs=list(out[2:2 + n]), lands=list(out[2 + n:2 + 2 * n]), token=out[-1])


def _exchange_wait(h, after, name):
    n = len(h["srcs"])
    kind = h["kind"]
    K = 4 if kind == "pair" else 3

    def body(*refs):
        ins, land = refs[:n], refs[n:2 * n]
        send, recv = refs[2 * n], refs[2 * n + 1]
        for a in range(n):
            for k, (si, di, dev) in enumerate(_peer_plan(kind, *_place())):
                cp = pltpu.make_async_remote_copy(
                    src_ref=ins[a].at[si], dst_ref=land[a].at[di], send_sem=send.at[K * a + k], recv_sem=recv.at[K * a + k],
                    device_id=dev, device_id_type=MESH)
                cp.wait_send()
                cp.wait_recv()

    out = pl.pallas_call(
        body, name=name,
        in_specs=[HBM] * (2 * n) + [SEM, SEM, ANY],
        out_specs=[HBM] * (2 * n),
        out_shape=_hbm_like(h["srcs"]) + _hbm_like(h["lands"]),
        input_output_aliases={i: i for i in range(2 * n)},
        compiler_params=pltpu.CompilerParams(has_side_effects=EFFECT),
    )(*h["srcs"], *h["lands"], h["send"], h["recv"], after)
    return list(out[:n]), list(out[n:])


def _pair_add(gs, gots, c_idx, name):
    n = len(gs)
    _, R, C = gs[0].shape
    tr, tc = _tile2(R, C, 512, 16)

    def body(c_ref, *refs):
        for a in range(n):
            refs[2 * n + a][0] = (refs[a][0].astype(f32) + refs[n + a][0].astype(f32)).astype(bf16)

    mine = pl.BlockSpec((1, tr, tc), lambda q, i, j, c: (2 * q + c[0], i, j))
    blk = pl.BlockSpec((1, tr, tc), lambda q, i, j, c: (q, i, j))
    return pl.pallas_call(
        body, name=name,
        grid_spec=pltpu.PrefetchScalarGridSpec(
            num_scalar_prefetch=1, grid=(4, R // tr, C // tc),
            in_specs=[mine] * n + [blk] * n, out_specs=[blk] * n),
        out_shape=[jax.ShapeDtypeStruct((4, R, C), bf16)] * n,
        compiler_params=_cparams(("parallel", "parallel", "parallel")),
    )(c_idx, *gs, *gots)


def _small_exchange(send, gather, name, after=None):
    R = send.shape[-2]

    def body(in_ref, out_ref, send_sems, recv_sems):
        x, y, c = _place()
        me = 4 * x + 2 * y + c
        out_ref[me] = in_ref[...] if gather else in_ref[me]
        cps = []
        for k in range(1, N_DEV):
            px, py, pc = x ^ ((k >> 2) & 1), y ^ ((k >> 1) & 1), c ^ (k & 1)
            src = in_ref if gather else in_ref.at[4 * px + 2 * py + pc]
            cps.append(pltpu.make_async_remote_copy(
                src_ref=src, dst_ref=out_ref.at[me],
                send_sem=send_sems.at[k - 1], recv_sem=recv_sems.at[k - 1],
                device_id=(px, py, pc), device_id_type=MESH))
        for cp in cps:
            cp.start()
        for cp in cps:
            cp.wait()

    body, in_specs, args = _ordered(body, [pl.BlockSpec(memory_space=pltpu.VMEM)], [send], after)
    return pl.pallas_call(
        body, name=name,
        in_specs=in_specs, out_specs=pl.BlockSpec(memory_space=pltpu.VMEM),
        out_shape=jax.ShapeDtypeStruct((N_DEV, R, LANES), f32),
        scratch_shapes=[pltpu.SemaphoreType.DMA((N_DEV - 1,)), pltpu.SemaphoreType.DMA((N_DEV - 1,))],
    )(*args)


def _sum_blocks(blocks, name):
    def body(in_ref, o_ref):
        s = in_ref[0]
        for d in range(1, N_DEV):
            s = s + in_ref[d]
        o_ref[0] = s

    return pl.pallas_call(body, name=name, out_shape=jax.ShapeDtypeStruct((1,) + blocks.shape[1:], f32))(blocks)


def _rows(n):
    return -(-n // LANES)


def _pack(arrs, total_rows, lead=0):
    head = arrs[0].shape[:lead]
    parts = []
    for a in arrs:
        flat = a.reshape(head + (-1,)).astype(f32)
        n = flat.shape[-1]
        parts.append(jnp.pad(flat, [(0, 0)] * lead + [(0, _rows(n) * LANES - n)]))
    flat = jnp.concatenate(parts, axis=-1)
    flat = jnp.pad(flat, [(0, 0)] * lead + [(0, total_rows * LANES - flat.shape[-1])])
    return flat.reshape(head + (total_rows, LANES))


def _unpack(packed, shapes):
    lead = packed.shape[:-2]
    flat = packed.reshape(lead + (-1,))
    out, pos = [], 0
    for s in shapes:
        n = 1
        for d in s:
            n *= d
        out.append(flat[..., pos:pos + n].reshape(lead + tuple(s)))
        pos += _rows(n) * LANES
    return out


def _to_shards(full, axis):
    s = full.shape
    return jnp.moveaxis(full.reshape(s[:axis] + (N_DEV, s[axis] // N_DEV) + s[axis + 1:]), axis, 0)


def _from_shards(sh, axis):
    m = jnp.moveaxis(sh, 0, axis)
    s = m.shape
    return m.reshape(s[:axis] + (s[axis] * s[axis + 1],) + s[axis + 2:])


def kernel(x, meta, ffn_norm, ffn_w_gate, ffn_w_up, ffn_w_down, gla_norm, gla_w_in, gla_w_lr, gla_b_lr, gla_head_norm, gla_w_out, pool_norm, pool_w, pool_b, pool_scale, final_norm, loss_target, m_meta, m_ffn_norm, m_ffn_w_gate, m_ffn_w_up, m_ffn_w_down, m_gla_norm, m_gla_w_in, m_gla_w_lr, m_gla_b_lr, m_gla_head_norm, m_gla_w_out, m_pool_norm, m_pool_w, m_pool_b, m_pool_scale, m_final_norm, v_meta, v_ffn_norm, v_ffn_w_gate, v_ffn_w_up, v_ffn_w_down, v_gla_norm, v_gla_w_in, v_gla_w_lr, v_gla_b_lr, v_gla_head_norm, v_gla_w_out, v_pool_norm, v_pool_w, v_pool_b, v_pool_scale, v_final_norm):
    H = GLA_HEADS
    _, SEQ, D = x.shape
    Fs = ffn_w_gate.shape[-1]
    DK, DV = D // 2, D
    hv = DV // H
    GW = D // POOL_GROUPS
    INW = 2 * DK + 2 * DV + GATE_RANK
    NPK = 2 * DK + 2 * DV + GATE_PAD
    pad = (-N_META) % GLA_CHUNK
    first = pad + N_META
    Lp = first + SEQ
    n_units = ffn_w_gate.shape[0] * ffn_w_gate.shape[1]
    assert first % GLA_CHUNK == 0 and Lp % GLA_CHUNK == 0 and pad >= POOL_GROUPS * 4

    px, py, pc = _place()
    c_idx = jnp.reshape(pc, (1,)).astype(jnp.int32)
    q_idx = jnp.reshape(2 * px + py, (1,)).astype(jnp.int32)
    zero_idx = jnp.zeros((1,), jnp.int32)

    small_sh = [meta, ffn_norm, gla_w_lr, pool_norm, pool_b, pool_scale]
    small_axis = [1, 2, 2, 1, 2, 1]
    sh_shapes = [a.shape for a in small_sh]
    sh_rows = -(-sum(_rows(a.size) for a in small_sh) // 8) * 8
    gathered = _small_exchange(_pack(small_sh, sh_rows), True, "small_gather")
    meta_f, ffn_norm_f, wlr_f, pool_norm_f, pool_b_f, pool_scale_f = [
        _from_shards(a, ax) for a, ax in zip(_unpack(gathered, sh_shapes), small_axis)]
    ffn_norm_f = ffn_norm_f.reshape(n_units, 1, D)
    wlr128 = jnp.pad(wlr_f[0], ((0, GATE_PAD - GATE_RANK), (0, 0)))

    def t_units(w):
        return jnp.swapaxes(w, -1, -2).reshape(n_units, Fs, D)

    ffn_f32 = [t_units(ffn_w_gate), t_units(ffn_w_up), ffn_w_down.reshape(n_units, Fs, D)]
    mixer_f32 = [gla_w_in[0].T[None], gla_w_out, pool_w[0].reshape(1, -1, GW)]
    gather_order = [("ffn0", ffn_f32, 0), ("mixers", mixer_f32, 0)] + [(f"ffn{u}", ffn_f32, u) for u in range(1, n_units)]
    c_lr = 2 * DK + DV
    c_r = 2 * DK + 2 * DV
    gate_blk = c_r // GATE_PAD

    def cast_shards(i, after):
        tag, arrays, u = gather_order[i]
        shards = [_cast_unit(w, u, f"cast_{tag}_{a}", after) for a, w in enumerate(arrays)]
        if tag == "mixers":
            shards[2] = shards[2].reshape(pool_w.shape[1:])
        return shards

    def pass_on(i, h, after):
        tag = gather_order[i][0]
        nxt = later_shards[i + 1] if i + 1 < len(gather_order) else None
        h = _gather_mid(h, after, f"gather_mid_{tag}")
        if nxt is not None:
            nxt = _gather_start(nxt, h["token"], f"gather_start_{gather_order[i + 1][0]}")
        return h, nxt

    def complete(i, h, after):
        tag = gather_order[i][0]
        h = _gather_mid2(h, after, f"gather_mid2_{tag}")
        return _gather_end(h, h["token"], f"gather_end_{tag}")

    xs = jnp.concatenate([jnp.zeros((pad, D), f32), meta_f, x[0]], axis=0)
    saved = {}
    ffn_w = [None] * n_units

    def ffn_f(u, xs, after=None):
        out, h, G, U = _ffn_fwd(xs, ffn_norm_f[u], *ffn_w[u], name=f"ffn_fwd{u}", after=after)
        saved[("ffn", u)] = (xs, h, G, U)
        return out

    def gla_f(xs, win_p, wout_full, after=None):
        hn = _rms_fwd(xs, gla_norm, bf16, "gla_norm_fwd", after=after)
        proj = _mm(hn, win_p, "nt", f32, "gla_proj", tm=1056, tn=896, tk=2048)
        lg = _gate_fwd(proj, wlr128, gla_b_lr, pad, gate_blk, "gla_gate_fwd")
        o, y, states = _gla_fwd(proj, lg, gla_head_norm, H, "gla_core_fwd")
        out = _mm(y, wout_full, "nn", f32, "gla_out", tm=1056, tn=512, tk=2048, residual=xs)
        saved["gla"] = (xs, hn, proj, lg, o, y, states)
        return out

    def pool_f(xs, wpool_full):
        hn = _rms_fwd(xs, pool_norm_f, f32, "pool_norm_fwd")
        pooled = _pool_windows(hn, pad, "pool_windows_fwd")
        out = _pool_mix_fwd(xs, pooled, wpool_full, pool_b_f.reshape(1, D), pool_scale_f, pad, "pool_mix_fwd")
        saved["pool"] = (xs, pooled)
        return out

    depth = ffn_w_gate.shape[0]
    assert depth == 2 and n_units == 4
    h0 = _gather_start(cast_shards(0, None), gathered, "gather_start_ffn0")
    later_shards = {}
    last = h0["token"]
    for i in range(1, len(gather_order)):
        later_shards[i] = cast_shards(i, last)
        last = later_shards[i][0]
    h0, h1 = pass_on(0, h0, last)
    ffn_w[0] = complete(0, h0, h1["token"])
    h1, h2 = pass_on(1, h1, ffn_w[0][0])
    xs = ffn_f(0, xs, after=h2["token"])
    win_g, wout_g, wpool_g = complete(1, h1, xs)
    h2, h3 = pass_on(2, h2, wout_g)
    win_full = win_g.reshape(INW, D)
    win_p = jnp.concatenate([win_full[:c_lr], win_full[c_lr + GATE_RANK:], win_full[c_lr:c_lr + GATE_RANK],
                             jnp.zeros((GATE_PAD - GATE_RANK, D), bf16)], axis=0)
    wout_full = wout_g.reshape(DV, D)
    wpool_full = _from_shards(wpool_g, 1)
    xs = gla_f(xs, win_p, wout_full, after=h3["token"])
    ffn_w[1] = complete(2, h2, xs)
    h3, h4 = pass_on(3, h3, ffn_w[1][0])
    xs = ffn_f(1, xs, after=h4["token"])
    ffn_w[2] = complete(3, h3, xs)
    h4, _ = pass_on(4, h4, ffn_w[2][0])
    xs = ffn_f(2, xs, after=h4["token"])
    xs = pool_f(xs, wpool_full)
    ffn_w[3] = complete(4, h4, xs)
    xs = ffn_f(3, xs)
    loss_part, dxs, d_final, dyh = _loss_head(xs, loss_target[0], final_norm.reshape(1, D), first, "loss_head")

    class Reduce:
        def __init__(self, tag, grads, after=None):
            self.tag = tag
            self.h = _exchange_start("pair", grads, loss_part if after is None else after, f"pair_start_{tag}")
            self.token = self.h["token"]

        def mid(self, after):
            grads, got = _exchange_wait(self.h, after, f"pair_wait_{self.tag}")
            if len({g.shape for g in grads}) == 1:
                self.sums = list(_pair_add(grads, got, c_idx, f"pair_add_{self.tag}"))
            else:
                self.sums = [_pair_add([g], [r], c_idx, f"pair_add_{self.tag}{a}")[0] for a, (g, r) in enumerate(zip(grads, got))]
            self.h = _exchange_start("chips", self.sums, loss_part, f"chips_start_{self.tag}")
            self.token = self.h["token"]

        def end(self, after):
            sums, recv = _exchange_wait(self.h, after, f"chips_wait_{self.tag}")
            return list(zip(sums, recv))

    d_ffn_norm = [None] * n_units
    small_grads = {}

    def ffn_b(u, dY, dyh, prev):
        xs_in, h_, G, U = saved[("ffn", u)]
        wg, wu, wd = ffn_w[u]
        tok = None if prev is None else prev.token
        dG, dU, A = _ffn_bwd_act(dyh, wd, G, U, f"ffn_act{u}", after=tok)
        dh = _ffn_bwd_dh(dG, dU, wg, wu, f"ffn_dh{u}")
        dxs, dg, dyh_next = _rms_bwd(dY, dh, xs_in, ffn_norm_f[u], pad, f"ffn_norm_bwd{u}")
        if prev is not None:
            prev.mid(dxs)
            tok = prev.token
        dwg = _ffn_bwd_wgrad(dG, h_, f"ffn_wgrad_gate{u}", after=tok)
        dwu = _ffn_bwd_wgrad(dU, h_, f"ffn_wgrad_up{u}", after=tok)
        dwd = _ffn_bwd_wgrad(A, dyh, f"ffn_wgrad_down{u}", after=tok)
        d_ffn_norm[u] = dg
        return dxs, dyh_next, Reduce(f"ffn{u}", [dwg, dwu, dwd])

    def gla_b(dY, prev):
        xs_in, hn, proj, lg, o, y, states = saved["gla"]
        dyb = dY.astype(bf16)
        dy = _mm(dyb, wout_full, "nt", f32, "gla_out_dgrad", tm=1056, tn=512, tk=2048, after=prev.token)
        dwout = _mm_tn_full(y, dyb, "gla_out_wgrad", 1024, after=prev.token)
        prev.mid(dwout)
        dq, dk, dv, dr, dlg, dhw = _gla_bwd(dy, proj, lg, o, states, gla_head_norm, H, pad, "gla_core_bwd", after=prev.token)
        dlr, dwlr, dblr = _gate_bwd(dlg, proj, wlr128, gla_b_lr, pad, gate_blk, "gla_gate_bwd")
        dproj = jnp.concatenate([dq, dk, dv, dr, dlr], axis=1)
        dwin_p = _mm_tn_full(dproj, hn, "gla_proj_wgrad", 896)
        dhn = _mm(dproj, win_p, "nn", f32, "gla_proj_dgrad", tm=1056, tn=1024, tk=896)
        dxs, dgn, dyh_next = _rms_bwd(dY, dhn, xs_in, gla_norm, pad, "gla_norm_bwd")
        dwin = jnp.concatenate([dwin_p[:c_lr], dwin_p[c_r:c_r + GATE_RANK], dwin_p[c_lr:c_r]], axis=0)
        small_grads.update(gla_w_lr=dwlr[:GATE_RANK][None], gla_b_lr=dblr, gla_head_norm=dhw, gla_norm=dgn)
        return dxs, dyh_next, Reduce("gla", [dwin.reshape(N_DEV, INW // N_DEV, D), dwout.reshape(N_DEV, DV // N_DEV, D)])

    def pool_b_(dY, prev):
        xs_in, pooled = saved["pool"]
        dp, dw, db, ds = _pool_mix_bwd(dY, pooled, wpool_full, pool_b_f.reshape(1, D), pool_scale_f, pad, "pool_mix_bwd",
                                       after=prev.token)
        dhn = _pool_windows_bwd(dp, pad, "pool_windows_bwd")
        dxs, dgn, dyh_next = _rms_bwd(dY, dhn, xs_in, pool_norm_f, pad, "pool_norm_bwd")
        prev.mid(dxs)
        dws = _to_shards(dw, 1)
        small_grads.update(pool_b=db.reshape(1, POOL_GROUPS, GW), pool_scale=ds, pool_norm=dgn)
        return dxs, dyh_next, Reduce("pool", [dws.reshape(N_DEV, POOL_GROUPS * GW // N_DEV, GW)], after=prev.token)

    sh_names = ["meta", "ffn_norm", "gla_w_lr", "pool_norm", "pool_b", "pool_scale"]
    rep_names = ["gla_norm", "gla_b_lr", "gla_head_norm", "final_norm"]
    rep_w = [gla_norm, gla_b_lr, gla_head_norm, final_norm]
    rep_shapes = [a.shape for a in rep_w]
    rep_rows = -(-sum(_rows(a.size) for a in rep_w) // 8) * 8

    def small_path(dxs0):
        small_grads.update(meta=dxs0[pad:first], ffn_norm=jnp.concatenate(d_ffn_norm, axis=0).reshape(n_units // 2, 2, D),
                           final_norm=d_final.reshape(D))
        by_owner = [_to_shards(small_grads[nm].reshape(full_shape), ax) for nm, full_shape, ax in zip(
            sh_names, [meta_f.shape, (ffn_norm.shape[0], 2, D), wlr_f.shape, pool_norm_f.shape, pool_b_f.shape, pool_scale_f.shape],
            small_axis)]
        rep_pack = _pack([small_grads[nm].reshape(s) for nm, s in zip(rep_names, rep_shapes)], rep_rows)
        tail_rows = jnp.concatenate([rep_pack, loss_part], axis=0)
        send = jnp.concatenate([_pack(by_owner, sh_rows, lead=1), jnp.broadcast_to(tail_rows, (N_DEV,) + tail_rows.shape)], axis=1)
        total = _sum_blocks(_small_exchange(send, False, "small_reduce"), "small_sum")
        n_small = sh_rows + rep_rows

        def pack_small(sh_list, rep_list):
            return jnp.concatenate([_pack(sh_list, sh_rows), _pack(rep_list, rep_rows)], axis=0)[None]

        w_small = pack_small(small_sh, rep_w)
        m_small = pack_small([m_meta, m_ffn_norm, m_gla_w_lr, m_pool_norm, m_pool_b, m_pool_scale],
                             [m_gla_norm, m_gla_b_lr, m_gla_head_norm, m_final_norm])
        v_small = pack_small([v_meta, v_ffn_norm, v_gla_w_lr, v_pool_norm, v_pool_b, v_pool_scale],
                             [v_gla_norm, v_gla_b_lr, v_gla_head_norm, v_final_norm])
        small_out = _adamw(w_small, m_small, v_small, 0, total[:, :n_small], zero_idx, None, None, "adamw_small")
        small_res = {}
        for kind, packed in zip(("grad", "delta", "new_m", "new_v"), small_out):
            sh_vals = _unpack(packed[0, :sh_rows], sh_shapes)
            rep_vals = _unpack(packed[0, sh_rows:], rep_shapes)
            for nm, val in zip(sh_names + rep_names, sh_vals + rep_vals):
                small_res[(kind, nm)] = val
        return total[0, n_small, 0], small_res, small_out[0]

    def ffn_b_last(dY, dyh, prev):
        xs_in, h_, G, U = saved[("ffn", 0)]
        wg, wu, wd = ffn_w[0]
        dG, dU, A = _ffn_bwd_act(dyh, wd, G, U, "ffn_act0", after=prev.token)
        prev.mid(dG)
        dwd = _ffn_bwd_wgrad(A, dyh, "ffn_wgrad_down0", after=prev.token)
        r_d = Reduce("ffn0_down", [dwd])
        dh = _ffn_bwd_dh(dG, dU, wg, wu, "ffn_dh0", after=r_d.token)
        dxs, dg, _ = _rms_bwd(dY, dh, xs_in, ffn_norm_f[0], pad, "ffn_norm_bwd0")
        d_ffn_norm[0] = dg
        small = small_path(dxs)
        r_d.mid(small[2])
        dwg = _ffn_bwd_wgrad(dG, h_, "ffn_wgrad_gate0", after=r_d.token)
        r_g = Reduce("ffn0_gate", [dwg])
        dwu = _ffn_bwd_wgrad(dU, h_, "ffn_wgrad_up0", after=r_g.token)
        r_g.mid(dwu)
        r_u = Reduce("ffn0_up", [dwu], after=r_g.token)
        return dxs, small, (r_g, r_u, r_d)

    dxs, dyh, r3 = ffn_b(3, dxs, dyh, None)
    dxs, dyh, rp = pool_b_(dxs, r3)
    dxs, dyh, r2 = ffn_b(2, dxs, dyh, rp)
    dxs, dyh, r1 = ffn_b(1, dxs, dyh, r2)
    dxs, dyh, rg = gla_b(dxs, r1)
    dxs, (loss, small_res, _), r0 = ffn_b_last(dxs, dyh, rg)
    grad_x = dxs[first:].reshape(x.shape)
    r_last = r0[1]

    big_res = {}

    def adam_one(nm, w, m, v, entry, transposed=False):
        sums, recv = entry
        R, C = sums.shape[1:]
        w1, m1, v1 = ((t[0].T if transposed else t).reshape(1, R, C) for t in (w, m, v))
        out = _adamw(w1, m1, v1, 0, sums, q_idx, recv, None, f"adamw_{nm}", after=r_last.token)
        for kind, val in zip(("grad", "delta", "new_m", "new_v"), out):
            big_res[(kind, nm)] = val[0].T[None] if transposed else val.reshape(w.shape)
        return out[0]

    e_gla = rg.end(dxs)
    done = adam_one("gla_w_in", gla_w_in, m_gla_w_in, v_gla_w_in, e_gla[0], transposed=True)
    done = adam_one("gla_w_out", gla_w_out, m_gla_w_out, v_gla_w_out, e_gla[1])
    done = adam_one("pool_w", pool_w, m_pool_w, v_pool_w, rp.end(done)[0])
    r_last.mid(done)

    ffn_names = ["ffn_w_gate", "ffn_w_up", "ffn_w_down"]
    ffn_wmv = [tuple(t_units(t) for t in (ffn_w_gate, m_ffn_w_gate, v_ffn_w_gate)),
               tuple(t_units(t) for t in (ffn_w_up, m_ffn_w_up, v_ffn_w_up)),
               tuple(t.reshape(n_units, Fs, D) for t in (ffn_w_down, m_ffn_w_down, v_ffn_w_down))]
    ffn_prev = [[lax.empty((n_units, Fs, D), f32) for _ in range(4)] for _ in range(3)]
    order_after = r_last.token
    for u, red in ((3, r3), (2, r2), (1, r1), (0, r0)):
        entries = [r.end(done)[0] for r in red] if u == 0 else red.end(done)
        for a in range(3):
            sums, recv = entries[a]
            ffn_prev[a] = _adamw(*ffn_wmv[a], u, sums, q_idx, recv, ffn_prev[a], f"adamw_{ffn_names[a]}{u}", after=order_after)
            done = order_after = ffn_prev[a][0]
    for a in range(3):
        for kind, val in zip(("grad", "delta", "new_m", "new_v"), ffn_prev[a]):
            val = val.reshape(ffn_w_down.shape)
            big_res[(kind, ffn_names[a])] = val if a == 2 else jnp.swapaxes(val, -1, -2)

    order = ["meta", "ffn_norm", "ffn_w_gate", "ffn_w_up", "ffn_w_down", "gla_norm", "gla_w_in", "gla_w_lr", "gla_b_lr",
             "gla_head_norm", "gla_w_out", "pool_norm", "pool_w", "pool_b", "pool_scale", "final_norm"]
    res = {**small_res, **big_res}
    outs = [loss, grad_x]
    for kind in ("grad", "delta", "new_m", "new_v"):
        outs += [res[(kind, nm)] for nm in order]
    return tuple(outs)
```

```python
import jax
import jax.numpy as jnp
from jax import lax
from jax.experimental import pallas as pl
from jax.experimental.pallas import tpu as pltpu

f32 = jnp.float32
bf16 = jnp.bfloat16

N_DEV = 8
N_META = 16
GLA_HEADS = 4
GLA_CHUNK = 64
GLA_SUB = 16
GLA_HEADS_PER_STEP = 4
GATE_RANK = 16
GATE_PAD = 128
GATE_NORM = 16.0
EPS = 1e-6
POOL_GROUPS = 4
ADAM_LR = 0.001
ADAM_B1 = 0.9
ADAM_B2 = 0.999
ADAM_EPS = 1e-08
ADAM_WD = 0.01
ADAM_STEP = 10
LANES = 128
VMEM_LIMIT_MB = 56

NN = (((1,), (0,)), ((), ()))
NT = (((1,), (1,)), ((), ()))
TN = (((0,), (0,)), ((), ()))
HI = lax.Precision.HIGHEST
MESH = pl.DeviceIdType.MESH
ANY = pl.BlockSpec(memory_space=pl.ANY)


def _cparams(sem=None, vmem_mb=None):
    kw = {}
    if sem is not None:
        kw["dimension_semantics"] = sem
    if vmem_mb is not None:
        kw["vmem_limit_bytes"] = vmem_mb * 2 ** 20
    return pltpu.CompilerParams(**kw)


def _tile(n, target, mult=16):
    best = None
    for t in range(mult, min(n, target) + 1, mult):
        if n % t == 0:
            best = t
    assert best is not None, (n, target, mult)
    return best


def _tile2(R, C, rows, mult):
    if R % mult == 0:
        return _tile(R, rows, mult), C
    return R, _tile(C, 256, LANES)


def _dot(a, b, dims=NN, precision=None):
    return lax.dot_general(a, b, dims, preferred_element_type=f32, precision=precision)


def _sigmoid(x):
    return 1.0 / (1.0 + jnp.exp(-x))


def _row_ids(tile_index, tm):
    return tile_index * tm + lax.broadcasted_iota(jnp.int32, (tm, 1), 0)


def _ordered(body, in_specs, args, after, lead=0):
    if after is None:
        return body, in_specs, args
    pos = lead + len(args)

    def body_without(*refs):
        return body(*refs[:pos], *refs[pos + 1:])

    return body_without, list(in_specs) + [ANY], list(args) + [after]


def _cast_unit(w, unit, name, after=None):
    _, R, C = w.shape
    tr, tc = _tile2(R, C, 256, 16)

    def body(w_ref, o_ref):
        o_ref[...] = w_ref[0].astype(bf16)

    body, in_specs, args = _ordered(body, [pl.BlockSpec((1, tr, tc), lambda i, j: (unit, i, j))], [w], after)
    return pl.pallas_call(
        body, name=name, grid=(R // tr, C // tc),
        in_specs=in_specs, out_specs=pl.BlockSpec((tr, tc), lambda i, j: (i, j)),
        out_shape=jax.ShapeDtypeStruct((R, C), bf16),
        compiler_params=_cparams(("parallel", "parallel")),
    )(*args)


def _rms_fwd(xs, g, out_dtype, name, after=None):
    Lp, D = xs.shape
    tm = _tile(Lp, 528)

    def body(x_ref, g_ref, h_ref):
        x = x_ref[...]
        rstd = lax.rsqrt(jnp.mean(x * x, axis=-1, keepdims=True) + EPS)
        h_ref[...] = (x * rstd * g_ref[...]).astype(out_dtype)

    in_specs = [pl.BlockSpec((tm, D), lambda i: (i, 0)), pl.BlockSpec((1, D), lambda i: (0, 0))]
    body, in_specs, args = _ordered(body, in_specs, [xs, g], after)
    return pl.pallas_call(
        body, name=name, grid=(Lp // tm,),
        in_specs=in_specs,
        out_specs=pl.BlockSpec((tm, D), lambda i: (i, 0)),
        out_shape=jax.ShapeDtypeStruct((Lp, D), out_dtype),
        compiler_params=_cparams(("parallel",)),
    )(*args)


def _rms_bwd(dY, dh, xs, g, pad, name):
    Lp, D = xs.shape
    tm = _tile(Lp, 352)

    def body(dY_ref, dh_ref, x_ref, g_ref, dxs_ref, dg_ref, half_ref):
        i = pl.program_id(0)

        @pl.when(i == 0)
        def _():
            dg_ref[...] = jnp.zeros_like(dg_ref)

        x = x_ref[...]
        rstd = lax.rsqrt(jnp.mean(x * x, axis=-1, keepdims=True) + EPS)
        xhat = x * rstd
        dh_ = dh_ref[...]
        dg_ref[...] += jnp.sum(dh_ * xhat, axis=0, keepdims=True)
        dxh = dh_ * g_ref[...]
        dx = rstd * (dxh - xhat * jnp.mean(dxh * xhat, axis=-1, keepdims=True))
        out = jnp.where(_row_ids(i, tm) >= pad, dY_ref[...] + dx, 0.0)
        dxs_ref[...] = out
        half_ref[...] = (0.5 * out).astype(bf16)

    row = pl.BlockSpec((tm, D), lambda i: (i, 0))
    vec = pl.BlockSpec((1, D), lambda i: (0, 0))
    return pl.pallas_call(
        body, name=name, grid=(Lp // tm,),
        in_specs=[row, row, row, vec], out_specs=[row, vec, row],
        out_shape=[jax.ShapeDtypeStruct((Lp, D), f32), jax.ShapeDtypeStruct((1, D), f32), jax.ShapeDtypeStruct((Lp, D), bf16)],
        compiler_params=_cparams(("arbitrary",)),
    )(dY, dh, xs, g)


def _mm(a, b, mode, out_dtype, name, tm=512, tn=512, tk=512, residual=None, after=None):
    if mode == "nn":
        (M, K), N = a.shape, b.shape[1]
    elif mode == "nt":
        (M, K), N = a.shape, b.shape[0]
    else:
        (K, M), N = a.shape, b.shape[1]
    tm = _tile(M, tm, 16 if mode != "tn" else LANES) if M > tm else M
    tn = _tile(N, tn, LANES) if N > tn else N
    tk = _tile(K, tk, LANES if mode != "tn" else 16) if K > tk else K
    nk = K // tk
    dims = {"nn": NN, "nt": NT, "tn": TN}[mode]

    def body(*refs):
        if residual is None:
            a_ref, b_ref, o_ref, acc = refs
            r_ref = None
        else:
            a_ref, b_ref, r_ref, o_ref, acc = refs
        k = pl.program_id(2)

        @pl.when(k == 0)
        def _():
            acc[...] = jnp.zeros_like(acc)

        acc[...] += _dot(a_ref[...], b_ref[...], dims)

        @pl.when(k == nk - 1)
        def _():
            r = acc[...]
            if r_ref is not None:
                r = r + r_ref[...]
            o_ref[...] = r.astype(out_dtype)

    a_spec = pl.BlockSpec((tk, tm), lambda i, j, k: (k, i)) if mode == "tn" else pl.BlockSpec((tm, tk), lambda i, j, k: (i, k))
    b_spec = pl.BlockSpec((tn, tk), lambda i, j, k: (j, k)) if mode == "nt" else pl.BlockSpec((tk, tn), lambda i, j, k: (k, j))
    o_spec = pl.BlockSpec((tm, tn), lambda i, j, k: (i, j))
    in_specs = [a_spec, b_spec] + ([o_spec] if residual is not None else [])
    args = [a, b] + ([residual] if residual is not None else [])
    body, in_specs, args = _ordered(body, in_specs, args, after)
    return pl.pallas_call(
        body, name=name, grid=(M // tm, N // tn, nk),
        in_specs=in_specs, out_specs=o_spec,
        out_shape=jax.ShapeDtypeStruct((M, N), out_dtype),
        scratch_shapes=[pltpu.VMEM((tm, tn), f32)],
        compiler_params=_cparams(("parallel", "parallel", "arbitrary"), VMEM_LIMIT_MB),
    )(*args)


def _mm_tn_full(a, b, name, tm, after=None):
    K, M = a.shape
    N = b.shape[1]
    tm = _tile(M, tm, LANES)

    def body(a_ref, b_ref, o_ref):
        o_ref[...] = _dot(a_ref[...], b_ref[...], TN).astype(bf16)

    in_specs = [pl.BlockSpec((K, tm), lambda i: (0, i)), pl.BlockSpec((K, N), lambda i: (0, 0), pipeline_mode=pl.Buffered(1))]
    body, in_specs, args = _ordered(body, in_specs, [a, b], after)
    return pl.pallas_call(
        body, name=name, grid=(M // tm,),
        in_specs=in_specs, out_specs=pl.BlockSpec((tm, N), lambda i: (i, 0)),
        out_shape=jax.ShapeDtypeStruct((M, N), bf16),
        compiler_params=_cparams(("parallel",), VMEM_LIMIT_MB),
    )(*args)


def _ffn_fwd(xs, g, wg, wu, wd, name, after=None):
    Lp, D = xs.shape
    nd, Fs, _ = wg.shape
    tm = _tile(Lp, 704)
    once = pl.Buffered(1)

    def body(x_ref, g_ref, wg_ref, wu_ref, wd_ref, out_ref, h_ref, G_ref, U_ref, hs, acc):
        j = pl.program_id(1)

        @pl.when(j == 0)
        def _():
            x = x_ref[...]
            rstd = lax.rsqrt(jnp.mean(x * x, axis=-1, keepdims=True) + EPS)
            h = (x * rstd * g_ref[...]).astype(bf16)
            hs[...] = h
            h_ref[...] = h
            acc[...] = jnp.zeros_like(acc)

        h = hs[...]
        G = _dot(h, wg_ref[0], NT)
        U = _dot(h, wu_ref[0], NT)
        G_ref[0] = G.astype(bf16)
        U_ref[0] = U.astype(bf16)
        A = (G * _sigmoid(G) * U).astype(bf16)
        acc[...] += _dot(A, wd_ref[0])

        @pl.when(j == nd - 1)
        def _():
            out_ref[...] = x_ref[...] + 0.5 * acc[...]

    row_f = pl.BlockSpec((tm, D), lambda i, j: (i, 0), pipeline_mode=once)
    act = pl.BlockSpec((1, tm, Fs), lambda i, j: (j, i, 0))
    wrow = pl.BlockSpec((1, Fs, D), lambda i, j: (j, 0, 0))
    in_specs = [row_f, pl.BlockSpec((1, D), lambda i, j: (0, 0)), wrow, wrow, wrow]
    body, in_specs, args = _ordered(body, in_specs, [xs, g, wg, wu, wd], after)
    return pl.pallas_call(
        body, name=name, grid=(Lp // tm, nd),
        in_specs=in_specs,
        out_specs=[row_f, pl.BlockSpec((tm, D), lambda i, j: (i, 0), pipeline_mode=once), act, act],
        out_shape=[jax.ShapeDtypeStruct((Lp, D), f32), jax.ShapeDtypeStruct((Lp, D), bf16),
                   jax.ShapeDtypeStruct((nd, Lp, Fs), bf16), jax.ShapeDtypeStruct((nd, Lp, Fs), bf16)],
        scratch_shapes=[pltpu.VMEM((tm, D), bf16), pltpu.VMEM((tm, D), f32)],
        compiler_params=_cparams(("parallel", "arbitrary"), VMEM_LIMIT_MB),
    )(*args)


def _ffn_bwd_act(dyh, wd, G, U, name, after=None):
    Lp, D = dyh.shape
    nd, Fs, _ = wd.shape
    tm = _tile(Lp, 704)

    def body(dyh_ref, wd_ref, G_ref, U_ref, dG_ref, dU_ref, A_ref):
        dA = _dot(dyh_ref[...], wd_ref[0], NT)
        Gf = G_ref[0].astype(f32)
        Uf = U_ref[0].astype(f32)
        s = _sigmoid(Gf)
        silu = Gf * s
        dG_ref[0] = (dA * Uf * (s * (1.0 + Gf * (1.0 - s)))).astype(bf16)
        dU_ref[0] = (dA * silu).astype(bf16)
        A_ref[0] = (silu * Uf).astype(bf16)

    act = pl.BlockSpec((1, tm, Fs), lambda j, i: (j, i, 0))
    act_s = jax.ShapeDtypeStruct((nd, Lp, Fs), bf16)
    in_specs = [pl.BlockSpec((tm, D), lambda j, i: (i, 0)), pl.BlockSpec((1, Fs, D), lambda j, i: (j, 0, 0)), act, act]
    body, in_specs, args = _ordered(body, in_specs, [dyh, wd, G, U], after)
    return pl.pallas_call(
        body, name=name, grid=(nd, Lp // tm),
        in_specs=in_specs, out_specs=[act, act, act], out_shape=[act_s, act_s, act_s],
        compiler_params=_cparams(("parallel", "parallel"), VMEM_LIMIT_MB),
    )(*args)


def _ffn_bwd_dh(dG, dU, wg, wu, name, after=None):
    nd, Lp, Fs = dG.shape
    D = wg.shape[2]
    tm = _tile(Lp, 1056)

    def body(dG_ref, dU_ref, wg_ref, wu_ref, dh_ref, acc):
        j = pl.program_id(1)

        @pl.when(j == 0)
        def _():
            acc[...] = jnp.zeros_like(acc)

        acc[...] += _dot(dG_ref[0], wg_ref[0]) + _dot(dU_ref[0], wu_ref[0])

        @pl.when(j == nd - 1)
        def _():
            dh_ref[...] = acc[...]

    act = pl.BlockSpec((1, tm, Fs), lambda i, j: (j, i, 0))
    wrow = pl.BlockSpec((1, Fs, D), lambda i, j: (j, 0, 0))
    body, in_specs, args = _ordered(body, [act, act, wrow, wrow], [dG, dU, wg, wu], after)
    return pl.pallas_call(
        body, name=name, grid=(Lp // tm, nd),
        in_specs=in_specs,
        out_specs=pl.BlockSpec((tm, D), lambda i, j: (i, 0), pipeline_mode=pl.Buffered(1)),
        out_shape=jax.ShapeDtypeStruct((Lp, D), f32),
        scratch_shapes=[pltpu.VMEM((tm, D), f32)],
        compiler_params=_cparams(("parallel", "arbitrary"), VMEM_LIMIT_MB),
    )(*args)


def _ffn_bwd_dh_norm(dG, dU, wg, wu, dY, xs, g, pad, name, after=None):
    nd, Lp, Fs = dG.shape
    D = wg.shape[2]
    tm = _tile(Lp, 528)
    once = pl.Buffered(1)

    def body(dG_ref, dU_ref, wg_ref, wu_ref, dY_ref, x_ref, g_ref, dxs_ref, dg_ref, half_ref, acc):
        i = pl.program_id(0)
        j = pl.program_id(1)

        @pl.when(j == 0)
        def _():
            acc[...] = jnp.zeros_like(acc)

        @pl.when((i == 0) & (j == 0))
        def _():
            dg_ref[...] = jnp.zeros_like(dg_ref)

        acc[...] += _dot(dG_ref[0], wg_ref[0]) + _dot(dU_ref[0], wu_ref[0])

        @pl.when(j == nd - 1)
        def _():
            x = x_ref[...]
            rstd = lax.rsqrt(jnp.mean(x * x, axis=-1, keepdims=True) + EPS)
            xhat = x * rstd
            dh = acc[...]
            dg_ref[...] += jnp.sum(dh * xhat, axis=0, keepdims=True)
            dxh = dh * g_ref[...]
            dx = rstd * (dxh - xhat * jnp.mean(dxh * xhat, axis=-1, keepdims=True))
            out = jnp.where(_row_ids(i, tm) >= pad, dY_ref[...] + dx, 0.0)
            dxs_ref[...] = out
            half_ref[...] = (0.5 * out).astype(bf16)

    act = pl.BlockSpec((1, tm, Fs), lambda i, j: (j, i, 0))
    wrow = pl.BlockSpec((1, Fs, D), lambda i, j: (j, 0, 0))
    row = pl.BlockSpec((tm, D), lambda i, j: (i, 0), pipeline_mode=once)
    vec = pl.BlockSpec((1, D), lambda i, j: (0, 0))
    body, in_specs, args = _ordered(body, [act, act, wrow, wrow, row, row, vec], [dG, dU, wg, wu, dY, xs, g], after)
    return pl.pallas_call(
        body, name=name, grid=(Lp // tm, nd),
        in_specs=in_specs,
        out_specs=[row, vec, pl.BlockSpec((tm, D), lambda i, j: (i, 0), pipeline_mode=once)],
        out_shape=[jax.ShapeDtypeStruct((Lp, D), f32), jax.ShapeDtypeStruct((1, D), f32), jax.ShapeDtypeStruct((Lp, D), bf16)],
        scratch_shapes=[pltpu.VMEM((tm, D), f32)],
        compiler_params=_cparams(("arbitrary", "arbitrary"), VMEM_LIMIT_MB),
    )(*args)


def _ffn_bwd_wgrad(act, rows, name, after=None):
    nd, Lp, Fs = act.shape
    D = rows.shape[1]

    def body(a_ref, r_ref, o_ref):
        o_ref[0] = _dot(a_ref[0], r_ref[...], TN).astype(bf16)

    in_specs = [pl.BlockSpec((1, Lp, Fs), lambda j: (j, 0, 0)),
                pl.BlockSpec((Lp, D), lambda j: (0, 0), pipeline_mode=pl.Buffered(1))]
    body, in_specs, args = _ordered(body, in_specs, [act, rows], after)
    return pl.pallas_call(
        body, name=name, grid=(nd,),
        in_specs=in_specs, out_specs=pl.BlockSpec((1, Fs, D), lambda j: (j, 0, 0)),
        out_shape=jax.ShapeDtypeStruct((nd, Fs, D), bf16),
        compiler_params=_cparams(("parallel",), VMEM_LIMIT_MB),
    )(*args)


def _gate_fwd(proj, wlr, blr, pad, gate_blk, name):
    Lp = proj.shape[0]
    DK = wlr.shape[1]
    tm = _tile(Lp, 528)

    def body(lr_ref, w_ref, b_ref, lg_ref):
        z = _dot(lr_ref[...].astype(bf16), w_ref[...].astype(bf16)) + b_ref[...]
        ls = jnp.minimum(z, 0.0) - jnp.log(1.0 + jnp.exp(-jnp.abs(z)))
        lg_ref[...] = jnp.where(_row_ids(pl.program_id(0), tm) >= pad, ls * (1.0 / GATE_NORM), 0.0)

    return pl.pallas_call(
        body, name=name, grid=(Lp // tm,),
        in_specs=[pl.BlockSpec((tm, GATE_PAD), lambda i: (i, gate_blk)),
                  pl.BlockSpec((GATE_PAD, DK), lambda i: (0, 0)), pl.BlockSpec((1, DK), lambda i: (0, 0))],
        out_specs=pl.BlockSpec((tm, DK), lambda i: (i, 0)),
        out_shape=jax.ShapeDtypeStruct((Lp, DK), f32),
        compiler_params=_cparams(("parallel",)),
    )(proj, wlr, blr)


def _gate_bwd(dlg, proj, wlr, blr, pad, gate_blk, name):
    Lp = proj.shape[0]
    DK = wlr.shape[1]
    tm = _tile(Lp, 528)

    def body(dlg_ref, lr_ref, w_ref, b_ref, dlr_ref, dw_ref, db_ref):
        i = pl.program_id(0)

        @pl.when(i == 0)
        def _():
            dw_ref[...] = jnp.zeros_like(dw_ref)
            db_ref[...] = jnp.zeros_like(db_ref)

        lr = lr_ref[...].astype(bf16)
        w = w_ref[...].astype(bf16)
        z = _dot(lr, w) + b_ref[...]
        dz = jnp.where(_row_ids(i, tm) >= pad, dlg_ref[...] * _sigmoid(-z) * (1.0 / GATE_NORM), 0.0)
        dzb = dz.astype(bf16)
        dlr_ref[...] = _dot(dzb, w, NT).astype(bf16)
        dw_ref[...] += _dot(lr, dzb, TN)
        db_ref[...] += jnp.sum(dz, axis=0, keepdims=True)

    return pl.pallas_call(
        body, name=name, grid=(Lp // tm,),
        in_specs=[pl.BlockSpec((tm, DK), lambda i: (i, 0)), pl.BlockSpec((tm, GATE_PAD), lambda i: (i, gate_blk)),
                  pl.BlockSpec((GATE_PAD, DK), lambda i: (0, 0)), pl.BlockSpec((1, DK), lambda i: (0, 0))],
        out_specs=[pl.BlockSpec((tm, GATE_PAD), lambda i: (i, 0)), pl.BlockSpec((GATE_PAD, DK), lambda i: (0, 0)),
                   pl.BlockSpec((1, DK), lambda i: (0, 0))],
        out_shape=[jax.ShapeDtypeStruct((Lp, GATE_PAD), bf16), jax.ShapeDtypeStruct((GATE_PAD, DK), f32),
                   jax.ShapeDtypeStruct((1, DK), f32)],
        compiler_params=_cparams(("arbitrary",)),
    )(dlg, proj, wlr, blr)


def _chunk_decay(lg):
    C = lg.shape[0]
    r = lax.broadcasted_iota(jnp.int32, (C, C), 0)
    c = lax.broadcasted_iota(jnp.int32, (C, C), 1)
    return _dot(jnp.where(r >= c, 1.0, 0.0).astype(f32), lg, NN, HI)


def _col(v):
    return jnp.transpose(jnp.broadcast_to(v, (8, v.shape[1])))[:, 0:1]


def _intra_scores(q, k, b, A_ref):
    C = q.shape[0]
    S = GLA_SUB
    A_ref[...] = jnp.zeros_like(A_ref)
    ri = lax.broadcasted_iota(jnp.int32, (S, 1), 0)
    for I in range(C // S):
        lo = S * I
        qI, bI = q[lo:lo + S], b[lo:lo + S]
        if I > 0:
            bref = b[lo - 1:lo]
            qs = qI * jnp.exp(bI - bref)
            ks = k[:lo] * jnp.exp(bref - b[:lo])
            A_ref[lo:lo + S, 0:lo] = _dot(qs, ks, NT, HI)
        for jj in range(S):
            j = lo + jj
            P = jnp.exp(jnp.minimum(bI - b[j:j + 1], 0.0))
            a = jnp.sum(qI * P * k[j:j + 1], axis=1, keepdims=True)
            A_ref[lo:lo + S, j:j + 1] = jnp.where(ri >= jj, a, 0.0)


def _intra_grads(q, k, b, dA, dq_ref, dk_ref):
    C = q.shape[0]
    S = GLA_SUB
    ri = lax.broadcasted_iota(jnp.int32, (S, 1), 0)
    for I in range(C // S):
        lo = S * I
        qI, bI = q[lo:lo + S], b[lo:lo + S]
        dqI = jnp.zeros_like(qI)
        if I > 0:
            bref = b[lo - 1:lo]
            eq = jnp.exp(bI - bref)
            ek = jnp.exp(bref - b[:lo])
            qs = qI * eq
            ks = k[:lo] * ek
            dAI = dA[lo:lo + S, 0:lo]
            dqI = dqI + _dot(dAI, ks, NN, HI) * eq
            dk_ref[0:lo, :] += _dot(dAI, qs, TN, HI) * ek
        for jj in range(S):
            j = lo + jj
            P = jnp.exp(jnp.minimum(bI - b[j:j + 1], 0.0))
            t = jnp.where(ri >= jj, dA[lo:lo + S, j:j + 1], 0.0) * P
            dqI = dqI + t * k[j:j + 1]
            dk_ref[j:j + 1, :] += jnp.sum(t * qI, axis=0, keepdims=True)
        dq_ref[lo:lo + S, :] += dqI


def _gla_fwd(proj, lg, hnw, H, name):
    Lp = proj.shape[0]
    DK = lg.shape[1]
    hk = DK // H
    hv = hnw.shape[1]
    DV = hv * H
    C = GLA_CHUNK
    NC = Lp // C
    HS = min(GLA_HEADS_PER_STEP, H)
    G = H // HS
    scale = float(hk) ** -0.5
    kq, kv, kr = G, (2 * DK) // (HS * hv), (2 * DK) // (HS * hv) + G

    def body(q_ref, k_ref, v_ref, r_ref, lg_ref, w_ref, o_ref, y_ref, s_ref, S_scr, A_scr):
        c = pl.program_id(1)

        @pl.when(c == 0)
        def _():
            S_scr[...] = jnp.zeros_like(S_scr)

        for hh in range(HS):
            ck, cv = slice(hh * hk, (hh + 1) * hk), slice(hh * hv, (hh + 1) * hv)
            q = q_ref[:, ck] * scale
            k = k_ref[:, ck]
            v = v_ref[:, cv]
            b = _chunk_decay(lg_ref[:, ck])
            bl = b[C - 1:C]
            S = S_scr[hh]
            s_ref[hh, 0] = S
            _intra_scores(q, k, b, A_scr.at[hh])
            vb = v.astype(bf16)
            o = _dot((q * jnp.exp(b)).astype(bf16), S.astype(bf16)) + _dot(A_scr[hh].astype(bf16), vb)
            kb = (k * jnp.exp(bl - b)).astype(bf16)
            S_scr[hh] = jnp.exp(_col(bl)) * S + _dot(kb, vb, TN)
            o_ref[:, cv] = o
            on = o * lax.rsqrt(jnp.mean(o * o, axis=-1, keepdims=True) + EPS) * w_ref[...]
            r = r_ref[:, cv]
            y_ref[:, cv] = (on * (r * _sigmoid(r))).astype(bf16)

    return pl.pallas_call(
        body, name=name, grid=(G, NC),
        in_specs=[pl.BlockSpec((C, HS * hk), lambda g, c: (c, g)),
                  pl.BlockSpec((C, HS * hk), lambda g, c: (c, kq + g)),
                  pl.BlockSpec((C, HS * hv), lambda g, c: (c, kv + g)),
                  pl.BlockSpec((C, HS * hv), lambda g, c: (c, kr + g)),
                  pl.BlockSpec((C, HS * hk), lambda g, c: (c, g)),
                  pl.BlockSpec((1, hv), lambda g, c: (0, 0))],
        out_specs=[pl.BlockSpec((C, HS * hv), lambda g, c: (c, g)), pl.BlockSpec((C, HS * hv), lambda g, c: (c, g)),
                   pl.BlockSpec((HS, 1, hk, hv), lambda g, c: (g, c, 0, 0))],
        out_shape=[jax.ShapeDtypeStruct((Lp, DV), f32), jax.ShapeDtypeStruct((Lp, DV), bf16),
                   jax.ShapeDtypeStruct((H, NC, hk, hv), f32)],
        scratch_shapes=[pltpu.VMEM((HS, hk, hv), f32), pltpu.VMEM((HS, C, C), f32)],
        compiler_params=_cparams(("parallel", "arbitrary")),
    )(proj, proj, proj, proj, lg, hnw)


def _gla_bwd(dy, proj, lg, o, states, hnw, H, pad, name, after=None):
    Lp = proj.shape[0]
    DK = lg.shape[1]
    hk = DK // H
    hv = hnw.shape[1]
    DV = hv * H
    C = GLA_CHUNK
    NC = Lp // C
    HS = min(GLA_HEADS_PER_STEP, H)
    G = H // HS
    scale = float(hk) ** -0.5
    kq, kv, kr = G, (2 * DK) // (HS * hv), (2 * DK) // (HS * hv) + G

    def body(dy_ref, q_ref, k_ref, v_ref, r_ref, lg_ref, o_ref, s_ref, sn_ref, w_ref,
             dq_ref, dk_ref, dv_ref, dr_ref, dlg_ref, dw_ref, dS_scr, A_scr, dq_s, dk_s):
        g = pl.program_id(0)
        cc = pl.program_id(1)
        c = NC - 1 - cc

        @pl.when(cc == 0)
        def _():
            dS_scr[...] = jnp.zeros_like(dS_scr)

        @pl.when((cc == 0) & (g == 0))
        def _():
            dw_ref[...] = jnp.zeros_like(dw_ref)

        keep = (c * C + lax.broadcasted_iota(jnp.int32, (C, 1), 0)) >= pad
        ri = lax.broadcasted_iota(jnp.int32, (C, C), 0)
        ci = lax.broadcasted_iota(jnp.int32, (C, C), 1)
        w = w_ref[...]
        for hh in range(HS):
            ck, cv = slice(hh * hk, (hh + 1) * hk), slice(hh * hv, (hh + 1) * hv)
            o_ = o_ref[:, cv]
            rs = lax.rsqrt(jnp.mean(o_ * o_, axis=-1, keepdims=True) + EPS)
            ohat = o_ * rs
            r = r_ref[:, cv]
            sg = _sigmoid(r)
            dy_ = dy_ref[:, cv]
            d_on = dy_ * (r * sg)
            dr_ref[:, cv] = jnp.where(keep, dy_ * (ohat * w) * (sg * (1.0 + r * (1.0 - sg))), 0.0).astype(bf16)
            dw_ref[...] += jnp.sum(d_on * ohat, axis=0, keepdims=True)
            d_oh = d_on * w
            do = rs * (d_oh - ohat * jnp.mean(d_oh * ohat, axis=-1, keepdims=True))
            dob = do.astype(bf16)
            q = q_ref[:, ck] * scale
            k = k_ref[:, ck]
            vb = v_ref[:, cv].astype(bf16)
            b = _chunk_decay(lg_ref[:, ck])
            bl = b[C - 1:C]
            eb = jnp.exp(b)
            ekb = jnp.exp(bl - b)
            S = s_ref[hh, 0]
            dS = dS_scr[hh]
            dSb = dS.astype(bf16)
            _intra_scores(q, k, b, A_scr.at[hh])
            dA = jnp.where(ri >= ci, _dot(dob, vb, NT), 0.0)
            kb = (k * ekb).astype(bf16)
            qb = (q * eb).astype(bf16)
            dv = _dot(A_scr[hh].astype(bf16), dob, TN) + _dot(kb, dSb)
            dq_s[hh] = _dot(dob, S.astype(bf16), NT) * eb
            dk_s[hh] = _dot(vb, dSb, NT) * ekb
            dS_scr[hh] = _dot(qb, dob, TN) + jnp.exp(_col(bl)) * dS
            _intra_grads(q, k, b, dA, dq_s.at[hh], dk_s.at[hh])
            dq = dq_s[hh]
            dk = dk_s[hh]
            Dm = q * dq - k * dk
            after_rows = _dot(jnp.ones((8, hv), f32), sn_ref[hh, 0] * dS, NT, HI)[0:1]
            dlg = _dot(jnp.where(ri <= ci, 1.0, 0.0).astype(f32), Dm, NN, HI) + after_rows
            dlg_ref[:, ck] = jnp.where(keep, dlg, 0.0)
            dq_ref[:, ck] = jnp.where(keep, dq * scale, 0.0).astype(bf16)
            dk_ref[:, ck] = jnp.where(keep, dk, 0.0).astype(bf16)
            dv_ref[:, cv] = jnp.where(keep, dv, 0.0).astype(bf16)

    rev = lambda cc: NC - 1 - cc
    bk = lambda off: pl.BlockSpec((C, HS * hk), lambda g, cc: (rev(cc), off + g))
    bv = lambda off: pl.BlockSpec((C, HS * hv), lambda g, cc: (rev(cc), off + g))
    in_specs = [bv(0), bk(0), bk(kq), bv(kv), bv(kr), bk(0), bv(0),
                pl.BlockSpec((HS, 1, hk, hv), lambda g, cc: (g, rev(cc), 0, 0)),
                pl.BlockSpec((HS, 1, hk, hv), lambda g, cc: (g, jnp.minimum(rev(cc) + 1, NC - 1), 0, 0)),
                pl.BlockSpec((1, hv), lambda g, cc: (0, 0))]
    body, in_specs, args = _ordered(body, in_specs, [dy, proj, proj, proj, proj, lg, o, states, states, hnw], after)
    return pl.pallas_call(
        body, name=name, grid=(G, NC),
        in_specs=in_specs,
        out_specs=[bk(0), bk(0), bv(0), bv(0), bk(0), pl.BlockSpec((1, hv), lambda g, cc: (0, 0))],
        out_shape=[jax.ShapeDtypeStruct((Lp, DK), bf16), jax.ShapeDtypeStruct((Lp, DK), bf16),
                   jax.ShapeDtypeStruct((Lp, DV), bf16), jax.ShapeDtypeStruct((Lp, DV), bf16),
                   jax.ShapeDtypeStruct((Lp, DK), f32), jax.ShapeDtypeStruct((1, hv), f32)],
        scratch_shapes=[pltpu.VMEM((HS, hk, hv), f32), pltpu.VMEM((HS, C, C), f32),
                        pltpu.VMEM((HS, C, hk), f32), pltpu.VMEM((HS, C, hk), f32)],
        compiler_params=_cparams(("arbitrary", "arbitrary")),
    )(*args)


def _window_sums(x, back):
    n = x.shape[0]
    out = []
    s = x
    for w in (1, 2, 4, 8):
        s = s + pltpu.roll(s, w if back else n - w, 0)
        out.append(s)
    return out


def _pool_windows(hn, pad, name):
    Lp, D = hn.shape
    GW = D // POOL_GROUPS
    cb = min(GW, 256)
    per = GW // cb

    def body(h_ref, p_ref):
        g = pl.program_id(0) // per
        x = h_ref[...]
        s2, s4, s8, s16 = _window_sums(x, True)
        sel = jnp.where(g == 0, s2, jnp.where(g == 1, s4, jnp.where(g == 2, s8, s16)))
        win = jnp.left_shift(2, g).astype(f32)
        rows = lax.broadcasted_iota(jnp.int32, (Lp, 1), 0)
        t = (rows - pad).astype(f32)
        cnt = jnp.minimum(jnp.maximum(t, 0.0) + 1.0, win)
        p_ref[...] = jnp.where(rows >= pad, sel / cnt - x, 0.0).astype(bf16)

    return pl.pallas_call(
        body, name=name, grid=(D // cb,),
        in_specs=[pl.BlockSpec((Lp, cb), lambda i: (0, i))],
        out_specs=pl.BlockSpec((Lp, cb), lambda i: (0, i)),
        out_shape=jax.ShapeDtypeStruct((Lp, D), bf16),
        compiler_params=_cparams(("parallel",)),
    )(hn)


def _pool_windows_bwd(dp, pad, name):
    Lp, D = dp.shape
    GW = D // POOL_GROUPS
    cb = min(GW, 256)
    per = GW // cb

    def body(dp_ref, dh_ref):
        g = pl.program_id(0) // per
        rows = lax.broadcasted_iota(jnp.int32, (Lp, 1), 0)
        d = jnp.where(rows >= pad, dp_ref[...], 0.0)
        win = jnp.left_shift(2, g).astype(f32)
        t = (rows - pad).astype(f32)
        cnt = jnp.minimum(jnp.maximum(t, 0.0) + 1.0, win)
        s2, s4, s8, s16 = _window_sums(d / cnt, False)
        sel = jnp.where(g == 0, s2, jnp.where(g == 1, s4, jnp.where(g == 2, s8, s16)))
        dh_ref[...] = jnp.where(rows >= pad, sel - d, 0.0)

    return pl.pallas_call(
        body, name=name, grid=(D // cb,),
        in_specs=[pl.BlockSpec((Lp, cb), lambda i: (0, i))],
        out_specs=pl.BlockSpec((Lp, cb), lambda i: (0, i)),
        out_shape=jax.ShapeDtypeStruct((Lp, D), f32),
        compiler_params=_cparams(("parallel",)),
    )(dp)


def _pool_mix_fwd(xs, pooled, w, bias, scale, pad, name):
    Lp, D = xs.shape
    GW = D // POOL_GROUPS
    tm = _tile(Lp, 1056)

    def body(x_ref, p_ref, w_ref, b_ref, s_ref, o_ref):
        z = _dot(p_ref[...], w_ref[0]) + b_ref[...]
        keep = _row_ids(pl.program_id(1), tm) >= pad
        o_ref[...] = x_ref[...] + jnp.where(keep, z * s_ref[...], 0.0)

    blk = pl.BlockSpec((tm, GW), lambda g, i: (i, g))
    vec = pl.BlockSpec((1, GW), lambda g, i: (0, g))
    return pl.pallas_call(
        body, name=name, grid=(POOL_GROUPS, Lp // tm),
        in_specs=[blk, blk, pl.BlockSpec((1, GW, GW), lambda g, i: (g, 0, 0)), vec, vec],
        out_specs=blk, out_shape=jax.ShapeDtypeStruct((Lp, D), f32),
        compiler_params=_cparams(("parallel", "parallel")),
    )(xs, pooled, w, bias, scale)


def _pool_mix_bwd(dY, pooled, w, bias, scale, pad, name, after=None):
    Lp, D = dY.shape
    GW = D // POOL_GROUPS
    tm = _tile(Lp, 1056)
    nm = Lp // tm

    def body(dY_ref, p_ref, w_ref, b_ref, s_ref, dp_ref, dw_ref, db_ref, ds_ref, acc):
        i = pl.program_id(1)

        @pl.when(i == 0)
        def _():
            acc[...] = jnp.zeros_like(acc)
            db_ref[...] = jnp.zeros_like(db_ref)
            ds_ref[...] = jnp.zeros_like(ds_ref)

        keep = _row_ids(i, tm) >= pad
        dY_ = jnp.where(keep, dY_ref[...], 0.0)
        p = p_ref[...]
        z = _dot(p, w_ref[0]) + b_ref[...]
        ds_ref[...] += jnp.sum(dY_ * z, axis=0, keepdims=True)
        dz = dY_ * s_ref[...]
        db_ref[...] += jnp.sum(dz, axis=0, keepdims=True)
        dzb = dz.astype(bf16)
        acc[...] += _dot(p, dzb, TN)
        dp_ref[...] = _dot(dzb, w_ref[0], NT)

        @pl.when(i == nm - 1)
        def _():
            dw_ref[0] = acc[...].astype(bf16)

    blk = pl.BlockSpec((tm, GW), lambda g, i: (i, g))
    vec = pl.BlockSpec((1, GW), lambda g, i: (0, g))
    wsp = pl.BlockSpec((1, GW, GW), lambda g, i: (g, 0, 0))
    body, in_specs, args = _ordered(body, [blk, blk, wsp, vec, vec], [dY, pooled, w, bias, scale], after)
    return pl.pallas_call(
        body, name=name, grid=(POOL_GROUPS, nm),
        in_specs=in_specs, out_specs=[blk, wsp, vec, vec],
        out_shape=[jax.ShapeDtypeStruct((Lp, D), f32), jax.ShapeDtypeStruct((POOL_GROUPS, GW, GW), bf16),
                   jax.ShapeDtypeStruct((1, D), f32), jax.ShapeDtypeStruct((1, D), f32)],
        scratch_shapes=[pltpu.VMEM((GW, GW), f32)],
        compiler_params=_cparams(("parallel", "arbitrary")),
    )(*args)


def _loss_head(xs, target, g, first, name):
    Lp, D = xs.shape
    tm = GLA_CHUNK
    off = first // tm

    def body(x_ref, t_ref, g_ref, loss_ref, dxs_ref, dg_ref, half_ref):
        i = pl.program_id(0)

        @pl.when(i == 0)
        def _():
            loss_ref[...] = jnp.zeros_like(loss_ref)
            dg_ref[...] = jnp.zeros_like(dg_ref)

        @pl.when(i < off)
        def _():
            dxs_ref[...] = jnp.zeros_like(dxs_ref)
            half_ref[...] = jnp.zeros_like(half_ref)

        @pl.when(i >= off)
        def _():
            x = x_ref[...]
            rstd = lax.rsqrt(jnp.mean(x * x, axis=-1, keepdims=True) + EPS)
            xhat = x * rstd
            gg = g_ref[...]
            err = xhat * gg - t_ref[...]
            loss_ref[...] += 0.5 * jnp.sum(jnp.mean(err * err, axis=-1, keepdims=True))
            dy = err * (1.0 / D)
            dg_ref[...] += jnp.sum(dy * xhat, axis=0, keepdims=True)
            dxh = dy * gg
            out = rstd * (dxh - xhat * jnp.mean(dxh * xhat, axis=-1, keepdims=True))
            dxs_ref[...] = out
            half_ref[...] = (0.5 * out).astype(bf16)

    row = pl.BlockSpec((tm, D), lambda i: (i, 0))
    return pl.pallas_call(
        body, name=name, grid=(Lp // tm,),
        in_specs=[row, pl.BlockSpec((tm, D), lambda i: (jnp.maximum(i - off, 0), 0)), pl.BlockSpec((1, D), lambda i: (0, 0))],
        out_specs=[pl.BlockSpec((8, LANES), lambda i: (0, 0)), row, pl.BlockSpec((1, D), lambda i: (0, 0)), row],
        out_shape=[jax.ShapeDtypeStruct((8, LANES), f32), jax.ShapeDtypeStruct((Lp, D), f32),
                   jax.ShapeDtypeStruct((1, D), f32), jax.ShapeDtypeStruct((Lp, D), bf16)],
        compiler_params=_cparams(("arbitrary",)),
    )(xs, target, g)


def _adam_math(w, g, m, v):
    m2 = ADAM_B1 * m + (1.0 - ADAM_B1) * g
    v2 = ADAM_B2 * v + (1.0 - ADAM_B2) * (g * g)
    m_hat = m2 / (1.0 - ADAM_B1 ** ADAM_STEP)
    v_hat = v2 / (1.0 - ADAM_B2 ** ADAM_STEP)
    delta = -ADAM_LR * (m_hat / (jnp.sqrt(v_hat) + ADAM_EPS) + ADAM_WD * w)
    return delta, m2, v2


def _adamw(w, m, v, unit, own, own_idx, recv, prev, name, after=None):
    U, R, C = w.shape
    tr, tc = _tile2(R, C, 64, 8 if own.dtype == f32 and recv is None else 16)
    n_recv = 0 if recv is None else recv.shape[0]

    def body(idx_ref, w_ref, m_ref, v_ref, own_ref, *rest):
        rest = list(rest)
        recv_refs = [rest.pop(0) for _ in range(n_recv)]
        if prev is not None:
            rest = rest[4:]
        g_ref, d_ref, m2_ref, v2_ref = rest
        g = own_ref[0].astype(f32)
        for r_ref in recv_refs:
            g = g + r_ref[0].astype(f32)
        delta, m2, v2 = _adam_math(w_ref[0], g, m_ref[0], v_ref[0])
        g_ref[0] = g
        d_ref[0] = delta
        m2_ref[0] = m2
        v2_ref[0] = v2

    blk = pl.BlockSpec((1, tr, tc), lambda i, j, idx: (unit, i, j))
    in_specs = [blk, blk, blk, pl.BlockSpec((1, tr, tc), lambda i, j, idx: (idx[0], i, j))]
    args = [w, m, v, own]
    for p in range(n_recv):
        in_specs.append(pl.BlockSpec((1, tr, tc), lambda i, j, idx, p=p: (p, i, j)))
        args.append(recv)
    aliases = {}
    if prev is not None:
        for t in range(4):
            aliases[1 + len(args) + t] = t
        in_specs += [ANY] * 4
        args += list(prev)
    body, in_specs, args = _ordered(body, in_specs, args, after, lead=1)
    out = jax.ShapeDtypeStruct((U, R, C), f32)
    return pl.pallas_call(
        body, name=name,
        grid_spec=pltpu.PrefetchScalarGridSpec(
            num_scalar_prefetch=1, grid=(R // tr, C // tc), in_specs=in_specs, out_specs=[blk] * 4),
        out_shape=[out] * 4, input_output_aliases=aliases,
        compiler_params=_cparams(("parallel", "parallel")),
    )(own_idx, *args)


def _place():
    return lax.axis_index("x"), lax.axis_index("y"), lax.axis_index("c")


HBM = pl.BlockSpec(memory_space=pltpu.HBM)
SEM = pl.BlockSpec(memory_space=pltpu.SEMAPHORE)
VMEM_SPEC = pl.BlockSpec(memory_space=pltpu.VMEM)
EFFECT = pltpu.SideEffectType.DATAFLOW_SIDE_EFFECTING
TOKEN = jax.ShapeDtypeStruct((8, LANES), f32)


def _hbm(x):
    return pltpu.with_memory_space_constraint(x, pltpu.HBM)


def _hbm_like(xs):
    return [pltpu.HBM(x.shape, x.dtype) for x in xs]


def _slot(px, py, pc):
    return 4 * px + 2 * py + pc


def _halves(ref):
    n = ref.shape[0]
    cut = n // 2 if n < 32 else (n // 2) // 16 * 16
    return ref.at[pl.ds(0, cut)], ref.at[pl.ds(cut, n - cut)]


def _gather_start(shards, after, name):
    n = len(shards)
    me = _slot(*_place())
    bufs = [lax.dynamic_update_slice(lax.empty((N_DEV,) + s.shape, s.dtype), s[None], (me,) + (0,) * s.ndim) for s in shards]

    def body(*refs):
        ins, land = refs[:n], refs[n:2 * n]
        send, recv = refs[2 * n + 1], refs[2 * n + 2]
        token = refs[-1]
        x, y, c = _place()
        to = [(x, y, 1 - c), (1 - x, y, c), (x, 1 - y, c)]
        for a in range(n):
            for k, dev in enumerate(to):
                pltpu.make_async_remote_copy(
                    src_ref=ins[a], dst_ref=land[a].at[_slot(x, y, c)], send_sem=send.at[3 * a + k], recv_sem=recv.at[3 * a + k],
                    device_id=dev, device_id_type=MESH).start()
        token[...] = jnp.zeros_like(token)

    out = pl.pallas_call(
        body, name=name,
        in_specs=[HBM] * (2 * n) + [ANY],
        out_specs=[SEM, SEM] + [HBM] * (2 * n) + [VMEM_SPEC],
        out_shape=[pltpu.SemaphoreType.DMA((3 * n,)), pltpu.SemaphoreType.DMA((3 * n,))] + _hbm_like(shards) + _hbm_like(bufs) + [TOKEN],
        input_output_aliases={i: 2 + i for i in range(2 * n)},
        compiler_params=pltpu.CompilerParams(has_side_effects=EFFECT),
    )(*[_hbm(s) for s in shards], *[_hbm(b) for b in bufs], after)
    return dict(send1=out[0], recv1=out[1], shards=list(out[2:2 + n]), bufs=list(out[2 + n:2 + 2 * n]), token=out[-1])


def _gather_mid(h, after, name):
    n = len(h["bufs"])

    def body(*refs):
        land, recv1 = refs[:n], refs[n]
        send2, recv2 = refs[n + 2], refs[n + 3]
        token = refs[-1]
        x, y, c = _place()
        nbr = [(1 - x, y, c), (x, 1 - y, c)]
        for j, dev in enumerate(nbr):
            for a in range(n):
                blk = land[a].at[_slot(*dev)]
                pltpu.make_async_remote_copy(
                    src_ref=blk, dst_ref=blk, send_sem=send2.at[4 * a + j], recv_sem=recv1.at[3 * a + 1 + j],
                    device_id=dev, device_id_type=MESH).wait_recv()
                pltpu.make_async_remote_copy(
                    src_ref=blk, dst_ref=blk, send_sem=send2.at[4 * a + j], recv_sem=recv2.at[4 * a + j],
                    device_id=(x, y, 1 - c), device_id_type=MESH).start()
        for a in range(n):
            from_x, from_y = land[a].at[_slot(*nbr[0])], land[a].at[_slot(*nbr[1])]
            for k, (half, dev) in enumerate([(_halves(from_y)[0], nbr[0]), (_halves(from_x)[1], nbr[1])]):
                pltpu.make_async_remote_copy(
                    src_ref=half, dst_ref=half, send_sem=send2.at[4 * a + 2 + k], recv_sem=recv2.at[4 * a + 2 + k],
                    device_id=dev, device_id_type=MESH).start()
        token[...] = jnp.zeros_like(token)

    out = pl.pallas_call(
        body, name=name,
        in_specs=[HBM] * n + [SEM, ANY],
        out_specs=[SEM, SEM] + [HBM] * n + [VMEM_SPEC],
        out_shape=[pltpu.SemaphoreType.DMA((4 * n,)), pltpu.SemaphoreType.DMA((4 * n,))] + _hbm_like(h["bufs"]) + [TOKEN],
        input_output_aliases={i: 2 + i for i in range(n)},
        compiler_params=pltpu.CompilerParams(has_side_effects=EFFECT),
    )(*h["bufs"], h["recv1"], after)
    h.update(send2=out[0], recv2=out[1], bufs=list(out[2:2 + n]), token=out[-1])
    return h


def _gather_mid2(h, after, name):
    n = len(h["bufs"])

    def body(*refs):
        land, recv2 = refs[:n], refs[n]
        send3, recv3 = refs[n + 2], refs[n + 3]
        token = refs[-1]
        x, y, c = _place()
        for a in range(n):
            blk = land[a].at[_slot(1 - x, 1 - y, c)]
            for k, half in enumerate(_halves(blk)):
                pltpu.make_async_remote_copy(
                    src_ref=half, dst_ref=half, send_sem=send3.at[a], recv_sem=recv2.at[4 * a + 2 + k],
                    device_id=(x, y, 1 - c), device_id_type=MESH).wait_recv()
            pltpu.make_async_remote_copy(
                src_ref=blk, dst_ref=blk, send_sem=send3.at[a], recv_sem=recv3.at[a],
                device_id=(x, y, 1 - c), device_id_type=MESH).start()
        token[...] = jnp.zeros_like(token)

    out = pl.pallas_call(
        body, name=name,
        in_specs=[HBM] * n + [SEM, ANY],
        out_specs=[SEM, SEM] + [HBM] * n + [VMEM_SPEC],
        out_shape=[pltpu.SemaphoreType.DMA((n,)), pltpu.SemaphoreType.DMA((n,))] + _hbm_like(h["bufs"]) + [TOKEN],
        input_output_aliases={i: 2 + i for i in range(n)},
        compiler_params=pltpu.CompilerParams(has_side_effects=EFFECT),
    )(*h["bufs"], h["recv2"], after)
    h.update(send3=out[0], recv3=out[1], bufs=list(out[2:2 + n]), token=out[-1])
    return h


def _gather_end(h, after, name):
    n = len(h["bufs"])

    def body(*refs):
        ins, land = refs[:n], refs[n:2 * n]
        send1, recv1, send2, recv2, send3, recv3 = refs[2 * n:2 * n + 6]
        x, y, c = _place()
        sib = (x, y, 1 - c)
        nbr = [(1 - x, y), (x, 1 - y)]

        def wait(src, dst, ssem, rsem, send):
            cp = pltpu.make_async_remote_copy(src_ref=src, dst_ref=dst, send_sem=ssem, recv_sem=rsem, device_id=sib, device_id_type=MESH)
            cp.wait_send() if send else cp.wait_recv()

        for a in range(n):
            mine = land[a].at[_slot(x, y, c)]
            for k in range(3):
                wait(ins[a], mine, send1.at[3 * a + k], recv1.at[3 * a + k], True)
            wait(ins[a], land[a].at[_slot(x, y, 1 - c)], send1.at[3 * a], recv1.at[3 * a], False)
            for j, (px, py) in enumerate(nbr):
                sent = land[a].at[_slot(px, py, c)]
                wait(sent, sent, send2.at[4 * a + j], recv2.at[4 * a + j], True)
                wait(sent, land[a].at[_slot(px, py, 1 - c)], send2.at[4 * a + j], recv2.at[4 * a + j], False)
            halves = [_halves(land[a].at[_slot(*nbr[1], c)])[0], _halves(land[a].at[_slot(*nbr[0], c)])[1]]
            for k, half in enumerate(halves):
                wait(half, half, send2.at[4 * a + 2 + k], recv2.at[4 * a + 2 + k], True)
            diag = land[a].at[_slot(1 - x, 1 - y, c)]
            wait(diag, diag, send3.at[a], recv3.at[a], True)
            wait(diag, land[a].at[_slot(1 - x, 1 - y, 1 - c)], send3.at[a], recv3.at[a], False)

    out = pl.pallas_call(
        body, name=name,
        in_specs=[HBM] * (2 * n) + [SEM] * 6 + [ANY],
        out_specs=[HBM] * n,
        out_shape=_hbm_like(h["bufs"]),
        input_output_aliases={n + i: i for i in range(n)},
        compiler_params=pltpu.CompilerParams(has_side_effects=EFFECT),
    )(*h["shards"], *h["bufs"], h["send1"], h["recv1"], h["send2"], h["recv2"], h["send3"], h["recv3"], after)
    return list(out)


def _peer_plan(kind, x, y, c):
    if kind == "pair":
        return [(2 * q + (1 - c), q, (x, y, 1 - c)) for q in range(4)]
    chips = [(1 - x, y), (x, 1 - y), (1 - x, 1 - y)]
    return [(2 * px + py, k, (px, py, c)) for k, (px, py) in enumerate(chips)]


def _exchange_start(kind, srcs, after, name):
    n = len(srcs)
    K = 4 if kind == "pair" else 3
    lands = [_hbm(lax.empty((K,) + s.shape[1:], s.dtype)) for s in srcs]

    def body(*refs):
        ins, land = refs[:n], refs[n:2 * n]
        send, recv = refs[2 * n + 1], refs[2 * n + 2]
        token = refs[-1]
        for a in range(n):
            for k, (si, di, dev) in enumerate(_peer_plan(kind, *_place())):
                pltpu.make_async_remote_copy(
                    src_ref=ins[a].at[si], dst_ref=land[a].at[di], send_sem=send.at[K * a + k], recv_sem=recv.at[K * a + k],
                    device_id=dev, device_id_type=MESH).start()
        token[...] = jnp.zeros_like(token)

    out = pl.pallas_call(
        body, name=name,
        in_specs=[HBM] * (2 * n) + [ANY],
        out_specs=[SEM, SEM] + [HBM] * (2 * n) + [VMEM_SPEC],
        out_shape=[pltpu.SemaphoreType.DMA((K * n,)), pltpu.SemaphoreType.DMA((K * n,))] + _hbm_like(srcs) + _hbm_like(lands) + [TOKEN],
        input_output_aliases={i: 2 + i for i in range(2 * n)},
        compiler_params=pltpu.CompilerParams(has_side_effects=EFFECT),
    )(*[_hbm(s) for s in srcs], *lands, after)
    return dict(kind=kind, send=out[0], recv=out[1], srcs=list(out[2:2 + n]), lands=list(out[2 + n:2 + 2 * n]), token=out[-1])


def _exchange_wait(h, after, name):
    n = len(h["srcs"])
    kind = h["kind"]
    K = 4 if kind == "pair" else 3

    def body(*refs):
        ins, land = refs[:n], refs[n:2 * n]
        send, recv = refs[2 * n], refs[2 * n + 1]
        for a in range(n):
            for k, (si, di, dev) in enumerate(_peer_plan(kind, *_place())):
                cp = pltpu.make_async_remote_copy(
                    src_ref=ins[a].at[si], dst_ref=land[a].at[di], send_sem=send.at[K * a + k], recv_sem=recv.at[K * a + k],
                    device_id=dev, device_id_type=MESH)
                cp.wait_send()
                cp.wait_recv()

    out = pl.pallas_call(
        body, name=name,
        in_specs=[HBM] * (2 * n) + [SEM, SEM, ANY],
        out_specs=[HBM] * (2 * n),
        out_shape=_hbm_like(h["srcs"]) + _hbm_like(h["lands"]),
        input_output_aliases={i: i for i in range(2 * n)},
        compiler_params=pltpu.CompilerParams(has_side_effects=EFFECT),
    )(*h["srcs"], *h["lands"], h["send"], h["recv"], after)
    return list(out[:n]), list(out[n:])


def _pair_add(gs, gots, c_idx, name):
    n = len(gs)
    _, R, C = gs[0].shape
    tr, tc = _tile2(R, C, 512, 16)

    def body(c_ref, *refs):
        for a in range(n):
            refs[2 * n + a][0] = (refs[a][0].astype(f32) + refs[n + a][0].astype(f32)).astype(bf16)

    mine = pl.BlockSpec((1, tr, tc), lambda q, i, j, c: (2 * q + c[0], i, j))
    blk = pl.BlockSpec((1, tr, tc), lambda q, i, j, c: (q, i, j))
    return pl.pallas_call(
        body, name=name,
        grid_spec=pltpu.PrefetchScalarGridSpec(
            num_scalar_prefetch=1, grid=(4, R // tr, C // tc),
            in_specs=[mine] * n + [blk] * n, out_specs=[blk] * n),
        out_shape=[jax.ShapeDtypeStruct((4, R, C), bf16)] * n,
        compiler_params=_cparams(("parallel", "parallel", "parallel")),
    )(c_idx, *gs, *gots)


def _small_exchange(send, gather, name, after=None):
    R = send.shape[-2]

    def body(in_ref, out_ref, send_sems, recv_sems):
        x, y, c = _place()
        me = 4 * x + 2 * y + c
        out_ref[me] = in_ref[...] if gather else in_ref[me]
        cps = []
        for k in range(1, N_DEV):
            px, py, pc = x ^ ((k >> 2) & 1), y ^ ((k >> 1) & 1), c ^ (k & 1)
            src = in_ref if gather else in_ref.at[4 * px + 2 * py + pc]
            cps.append(pltpu.make_async_remote_copy(
                src_ref=src, dst_ref=out_ref.at[me],
                send_sem=send_sems.at[k - 1], recv_sem=recv_sems.at[k - 1],
                device_id=(px, py, pc), device_id_type=MESH))
        for cp in cps:
            cp.start()
        for cp in cps:
            cp.wait()

    body, in_specs, args = _ordered(body, [pl.BlockSpec(memory_space=pltpu.VMEM)], [send], after)
    return pl.pallas_call(
        body, name=name,
        in_specs=in_specs, out_specs=pl.BlockSpec(memory_space=pltpu.VMEM),
        out_shape=jax.ShapeDtypeStruct((N_DEV, R, LANES), f32),
        scratch_shapes=[pltpu.SemaphoreType.DMA((N_DEV - 1,)), pltpu.SemaphoreType.DMA((N_DEV - 1,))],
    )(*args)


def _sum_blocks(blocks, name):
    def body(in_ref, o_ref):
        s = in_ref[0]
        for d in range(1, N_DEV):
            s = s + in_ref[d]
        o_ref[0] = s

    return pl.pallas_call(body, name=name, out_shape=jax.ShapeDtypeStruct((1,) + blocks.shape[1:], f32))(blocks)


def _rows(n):
    return -(-n // LANES)


def _pack(arrs, total_rows, lead=0):
    head = arrs[0].shape[:lead]
    parts = []
    for a in arrs:
        flat = a.reshape(head + (-1,)).astype(f32)
        n = flat.shape[-1]
        parts.append(jnp.pad(flat, [(0, 0)] * lead + [(0, _rows(n) * LANES - n)]))
    flat = jnp.concatenate(parts, axis=-1)
    flat = jnp.pad(flat, [(0, 0)] * lead + [(0, total_rows * LANES - flat.shape[-1])])
    return flat.reshape(head + (total_rows, LANES))


def _unpack(packed, shapes):
    lead = packed.shape[:-2]
    flat = packed.reshape(lead + (-1,))
    out, pos = [], 0
    for s in shapes:
        n = 1
        for d in s:
            n *= d
        out.append(flat[..., pos:pos + n].reshape(lead + tuple(s)))
        pos += _rows(n) * LANES
    return out


def _to_shards(full, axis):
    s = full.shape
    return jnp.moveaxis(full.reshape(s[:axis] + (N_DEV, s[axis] // N_DEV) + s[axis + 1:]), axis, 0)


def _from_shards(sh, axis):
    m = jnp.moveaxis(sh, 0, axis)
    s = m.shape
    return m.reshape(s[:axis] + (s[axis] * s[axis + 1],) + s[axis + 2:])


def kernel(x, meta, ffn_norm, ffn_w_gate, ffn_w_up, ffn_w_down, gla_norm, gla_w_in, gla_w_lr, gla_b_lr, gla_head_norm, gla_w_out, pool_norm, pool_w, pool_b, pool_scale, final_norm, loss_target, m_meta, m_ffn_norm, m_ffn_w_gate, m_ffn_w_up, m_ffn_w_down, m_gla_norm, m_gla_w_in, m_gla_w_lr, m_gla_b_lr, m_gla_head_norm, m_gla_w_out, m_pool_norm, m_pool_w, m_pool_b, m_pool_scale, m_final_norm, v_meta, v_ffn_norm, v_ffn_w_gate, v_ffn_w_up, v_ffn_w_down, v_gla_norm, v_gla_w_in, v_gla_w_lr, v_gla_b_lr, v_gla_head_norm, v_gla_w_out, v_pool_norm, v_pool_w, v_pool_b, v_pool_scale, v_final_norm):
    H = GLA_HEADS
    _, SEQ, D = x.shape
    Fs = ffn_w_gate.shape[-1]
    DK, DV = D // 2, D
    hv = DV // H
    GW = D // POOL_GROUPS
    INW = 2 * DK + 2 * DV + GATE_RANK
    NPK = 2 * DK + 2 * DV + GATE_PAD
    pad = (-N_META) % GLA_CHUNK
    first = pad + N_META
    Lp = first + SEQ
    n_units = ffn_w_gate.shape[0] * ffn_w_gate.shape[1]
    assert first % GLA_CHUNK == 0 and Lp % GLA_CHUNK == 0 and pad >= POOL_GROUPS * 4

    px, py, pc = _place()
    c_idx = jnp.reshape(pc, (1,)).astype(jnp.int32)
    q_idx = jnp.reshape(2 * px + py, (1,)).astype(jnp.int32)
    zero_idx = jnp.zeros((1,), jnp.int32)

    small_sh = [meta, ffn_norm, gla_w_lr, pool_norm, pool_b, pool_scale]
    small_axis = [1, 2, 2, 1, 2, 1]
    sh_shapes = [a.shape for a in small_sh]
    sh_rows = -(-sum(_rows(a.size) for a in small_sh) // 8) * 8
    gathered = _small_exchange(_pack(small_sh, sh_rows), True, "small_gather")
    meta_f, ffn_norm_f, wlr_f, pool_norm_f, pool_b_f, pool_scale_f = [
        _from_shards(a, ax) for a, ax in zip(_unpack(gathered, sh_shapes), small_axis)]
    ffn_norm_f = ffn_norm_f.reshape(n_units, 1, D)
    wlr128 = jnp.pad(wlr_f[0], ((0, GATE_PAD - GATE_RANK), (0, 0)))

    def t_units(w):
        return jnp.swapaxes(w, -1, -2).reshape(n_units, Fs, D)

    ffn_f32 = [t_units(ffn_w_gate), t_units(ffn_w_up), ffn_w_down.reshape(n_units, Fs, D)]
    mixer_f32 = [gla_w_in[0].T[None], gla_w_out, pool_w[0].reshape(1, -1, GW)]
    gather_order = [("ffn0", ffn_f32, 0), ("mixers", mixer_f32, 0)] + [(f"ffn{u}", ffn_f32, u) for u in range(1, n_units)]
    c_lr = 2 * DK + DV
    c_r = 2 * DK + 2 * DV
    gate_blk = c_r // GATE_PAD

    def cast_shards(i, after):
        tag, arrays, u = gather_order[i]
        shards = [_cast_unit(w, u, f"cast_{tag}_{a}", after) for a, w in enumerate(arrays)]
        if tag == "mixers":
            shards[2] = shards[2].reshape(pool_w.shape[1:])
        return shards

    def pass_on(i, h, after):
        tag = gather_order[i][0]
        nxt = later_shards[i + 1] if i + 1 < len(gather_order) else None
        h = _gather_mid(h, after, f"gather_mid_{tag}")
        if nxt is not None:
            nxt = _gather_start(nxt, h["token"], f"gather_start_{gather_order[i + 1][0]}")
        return h, nxt

    def complete(i, h, after):
        tag = gather_order[i][0]
        h = _gather_mid2(h, after, f"gather_mid2_{tag}")
        return _gather_end(h, h["token"], f"gather_end_{tag}")

    xs = jnp.concatenate([jnp.zeros((pad, D), f32), meta_f, x[0]], axis=0)
    saved = {}
    ffn_w = [None] * n_units

    def ffn_f(u, xs, after=None):
        out, h, G, U = _ffn_fwd(xs, ffn_norm_f[u], *ffn_w[u], name=f"ffn_fwd{u}", after=after)
        saved[("ffn", u)] = (xs, h, G, U)
        return out

    def gla_f(xs, win_p, wout_full, after=None):
        hn = _rms_fwd(xs, gla_norm, bf16, "gla_norm_fwd", after=after)
        proj = _mm(hn, win_p, "nt", f32, "gla_proj", tm=1056, tn=896, tk=2048)
        lg = _gate_fwd(proj, wlr128, gla_b_lr, pad, gate_blk, "gla_gate_fwd")
        o, y, states = _gla_fwd(proj, lg, gla_head_norm, H, "gla_core_fwd")
        out = _mm(y, wout_full, "nn", f32, "gla_out", tm=1056, tn=512, tk=2048, residual=xs)
        saved["gla"] = (xs, hn, proj, lg, o, y, states)
        return out

    def pool_f(xs, wpool_full):
        hn = _rms_fwd(xs, pool_norm_f, f32, "pool_norm_fwd")
        pooled = _pool_windows(hn, pad, "pool_windows_fwd")
        out = _pool_mix_fwd(xs, pooled, wpool_full, pool_b_f.reshape(1, D), pool_scale_f, pad, "pool_mix_fwd")
        saved["pool"] = (xs, pooled)
        return out

    depth = ffn_w_gate.shape[0]
    assert depth == 2 and n_units == 4
    h0 = _gather_start(cast_shards(0, None), gathered, "gather_start_ffn0")
    later_shards = {}
    last = h0["token"]
    for i in range(1, len(gather_order)):
        later_shards[i] = cast_shards(i, last)
        last = later_shards[i][0]
    h0, h1 = pass_on(0, h0, last)
    ffn_w[0] = complete(0, h0, h1["token"])
    h1, h2 = pass_on(1, h1, ffn_w[0][0])
    xs = ffn_f(0, xs, after=h2["token"])
    win_g, wout_g, wpool_g = complete(1, h1, xs)
    h2, h3 = pass_on(2, h2, wout_g)
    win_full = win_g.reshape(INW, D)
    win_p = jnp.concatenate([win_full[:c_lr], win_full[c_lr + GATE_RANK:], win_full[c_lr:c_lr + GATE_RANK],
                             jnp.zeros((GATE_PAD - GATE_RANK, D), bf16)], axis=0)
    wout_full = wout_g.reshape(DV, D)
    wpool_full = _from_shards(wpool_g, 1)
    xs = gla_f(xs, win_p, wout_full, after=h3["token"])
    ffn_w[1] = complete(2, h2, xs)
    h3, h4 = pass_on(3, h3, ffn_w[1][0])
    xs = ffn_f(1, xs, after=h4["token"])
    ffn_w[2] = complete(3, h3, xs)
    h4, _ = pass_on(4, h4, ffn_w[2][0])
    xs = ffn_f(2, xs, after=h4["token"])
    xs = pool_f(xs, wpool_full)
    ffn_w[3] = complete(4, h4, xs)
    xs = ffn_f(3, xs)
    loss_part, dxs, d_final, dyh = _loss_head(xs, loss_target[0], final_norm.reshape(1, D), first, "loss_head")

    class Reduce:
        def __init__(self, tag, grads, after=None):
            self.tag = tag
            self.h = _exchange_start("pair", grads, loss_part if after is None else after, f"pair_start_{tag}")
            self.token = self.h["token"]

        def mid(self, after):
            grads, got = _exchange_wait(self.h, after, f"pair_wait_{self.tag}")
            if len({g.shape for g in grads}) == 1:
                self.sums = list(_pair_add(grads, got, c_idx, f"pair_add_{self.tag}"))
            else:
                self.sums = [_pair_add([g], [r], c_idx, f"pair_add_{self.tag}{a}")[0] for a, (g, r) in enumerate(zip(grads, got))]
            self.h = _exchange_start("chips", self.sums, loss_part, f"chips_start_{self.tag}")
            self.token = self.h["token"]

        def end(self, after):
            sums, recv = _exchange_wait(self.h, after, f"chips_wait_{self.tag}")
            return list(zip(sums, recv))

    d_ffn_norm = [None] * n_units
    small_grads = {}

    def ffn_b(u, dY, dyh, prev):
        xs_in, h_, G, U = saved[("ffn", u)]
        wg, wu, wd = ffn_w[u]
        tok = None if prev is None else prev.token
        dG, dU, A = _ffn_bwd_act(dyh, wd, G, U, f"ffn_act{u}", after=tok)
        dxs, dg, dyh_next = _ffn_bwd_dh_norm(dG, dU, wg, wu, dY, xs_in, ffn_norm_f[u], pad, f"ffn_dh{u}")
        if prev is not None:
            prev.mid(dxs)
            tok = prev.token
        dwg = _ffn_bwd_wgrad(dG, h_, f"ffn_wgrad_gate{u}", after=tok)
        dwu = _ffn_bwd_wgrad(dU, h_, f"ffn_wgrad_up{u}", after=tok)
        dwd = _ffn_bwd_wgrad(A, dyh, f"ffn_wgrad_down{u}", after=tok)
        d_ffn_norm[u] = dg
        return dxs, dyh_next, Reduce(f"ffn{u}", [dwg, dwu, dwd])

    def gla_b(dY, prev):
        xs_in, hn, proj, lg, o, y, states = saved["gla"]
        dyb = dY.astype(bf16)
        dy = _mm(dyb, wout_full, "nt", f32, "gla_out_dgrad", tm=1056, tn=512, tk=2048, after=prev.token)
        dwout = _mm_tn_full(y, dyb, "gla_out_wgrad", 1024, after=prev.token)
        prev.mid(dwout)
        dq, dk, dv, dr, dlg, dhw = _gla_bwd(dy, proj, lg, o, states, gla_head_norm, H, pad, "gla_core_bwd", after=prev.token)
        dlr, dwlr, dblr = _gate_bwd(dlg, proj, wlr128, gla_b_lr, pad, gate_blk, "gla_gate_bwd")
        dproj = jnp.concatenate([dq, dk, dv, dr, dlr], axis=1)
        dwin_p = _mm_tn_full(dproj, hn, "gla_proj_wgrad", 896)
        dhn = _mm(dproj, win_p, "nn", f32, "gla_proj_dgrad", tm=1056, tn=1024, tk=896)
        dxs, dgn, dyh_next = _rms_bwd(dY, dhn, xs_in, gla_norm, pad, "gla_norm_bwd")
        dwin = jnp.concatenate([dwin_p[:c_lr], dwin_p[c_r:c_r + GATE_RANK], dwin_p[c_lr:c_r]], axis=0)
        small_grads.update(gla_w_lr=dwlr[:GATE_RANK][None], gla_b_lr=dblr, gla_head_norm=dhw, gla_norm=dgn)
        return dxs, dyh_next, Reduce("gla", [dwin.reshape(N_DEV, INW // N_DEV, D), dwout.reshape(N_DEV, DV // N_DEV, D)])

    def pool_b_(dY, prev):
        xs_in, pooled = saved["pool"]
        dp, dw, db, ds = _pool_mix_bwd(dY, pooled, wpool_full, pool_b_f.reshape(1, D), pool_scale_f, pad, "pool_mix_bwd",
                                       after=prev.token)
        dhn = _pool_windows_bwd(dp, pad, "pool_windows_bwd")
        dxs, dgn, dyh_next = _rms_bwd(dY, dhn, xs_in, pool_norm_f, pad, "pool_norm_bwd")
        prev.mid(dxs)
        dws = _to_shards(dw, 1)
        small_grads.update(pool_b=db.reshape(1, POOL_GROUPS, GW), pool_scale=ds, pool_norm=dgn)
        return dxs, dyh_next, Reduce("pool", [dws.reshape(N_DEV, POOL_GROUPS * GW // N_DEV, GW)], after=prev.token)

    sh_names = ["meta", "ffn_norm", "gla_w_lr", "pool_norm", "pool_b", "pool_scale"]
    rep_names = ["gla_norm", "gla_b_lr", "gla_head_norm", "final_norm"]
    rep_w = [gla_norm, gla_b_lr, gla_head_norm, final_norm]
    rep_shapes = [a.shape for a in rep_w]
    rep_rows = -(-sum(_rows(a.size) for a in rep_w) // 8) * 8

    def small_path(dxs0):
        small_grads.update(meta=dxs0[pad:first], ffn_norm=jnp.concatenate(d_ffn_norm, axis=0).reshape(n_units // 2, 2, D),
                           final_norm=d_final.reshape(D))
        by_owner = [_to_shards(small_grads[nm].reshape(full_shape), ax) for nm, full_shape, ax in zip(
            sh_names, [meta_f.shape, (ffn_norm.shape[0], 2, D), wlr_f.shape, pool_norm_f.shape, pool_b_f.shape, pool_scale_f.shape],
            small_axis)]
        rep_pack = _pack([small_grads[nm].reshape(s) for nm, s in zip(rep_names, rep_shapes)], rep_rows)
        tail_rows = jnp.concatenate([rep_pack, loss_part], axis=0)
        send = jnp.concatenate([_pack(by_owner, sh_rows, lead=1), jnp.broadcast_to(tail_rows, (N_DEV,) + tail_rows.shape)], axis=1)
        total = _sum_blocks(_small_exchange(send, False, "small_reduce"), "small_sum")
        n_small = sh_rows + rep_rows

        def pack_small(sh_list, rep_list):
            return jnp.concatenate([_pack(sh_list, sh_rows), _pack(rep_list, rep_rows)], axis=0)[None]

        w_small = pack_small(small_sh, rep_w)
        m_small = pack_small([m_meta, m_ffn_norm, m_gla_w_lr, m_pool_norm, m_pool_b, m_pool_scale],
                             [m_gla_norm, m_gla_b_lr, m_gla_head_norm, m_final_norm])
        v_small = pack_small([v_meta, v_ffn_norm, v_gla_w_lr, v_pool_norm, v_pool_b, v_pool_scale],
                             [v_gla_norm, v_gla_b_lr, v_gla_head_norm, v_final_norm])
        small_out = _adamw(w_small, m_small, v_small, 0, total[:, :n_small], zero_idx, None, None, "adamw_small")
        small_res = {}
        for kind, packed in zip(("grad", "delta", "new_m", "new_v"), small_out):
            sh_vals = _unpack(packed[0, :sh_rows], sh_shapes)
            rep_vals = _unpack(packed[0, sh_rows:], rep_shapes)
            for nm, val in zip(sh_names + rep_names, sh_vals + rep_vals):
                small_res[(kind, nm)] = val
        return total[0, n_small, 0], small_res, small_out[0]

    def ffn_b_last(dY, dyh, prev):
        xs_in, h_, G, U = saved[("ffn", 0)]
        wg, wu, wd = ffn_w[0]
        dG, dU, A = _ffn_bwd_act(dyh, wd, G, U, "ffn_act0", after=prev.token)
        prev.mid(dG)
        dwd = _ffn_bwd_wgrad(A, dyh, "ffn_wgrad_down0", after=prev.token)
        r_d = Reduce("ffn0_down", [dwd])
        dxs, dg, _ = _ffn_bwd_dh_norm(dG, dU, wg, wu, dY, xs_in, ffn_norm_f[0], pad, "ffn_dh0", after=r_d.token)
        d_ffn_norm[0] = dg
        small = small_path(dxs)
        r_d.mid(small[2])
        dwg = _ffn_bwd_wgrad(dG, h_, "ffn_wgrad_gate0", after=r_d.token)
        r_g = Reduce("ffn0_gate", [dwg])
        dwu = _ffn_bwd_wgrad(dU, h_, "ffn_wgrad_up0", after=r_g.token)
        r_g.mid(dwu)
        r_u = Reduce("ffn0_up", [dwu], after=r_g.token)
        return dxs, small, (r_g, r_u, r_d)

    dxs, dyh, r3 = ffn_b(3, dxs, dyh, None)
    dxs, dyh, rp = pool_b_(dxs, r3)
    dxs, dyh, r2 = ffn_b(2, dxs, dyh, rp)
    dxs, dyh, r1 = ffn_b(1, dxs, dyh, r2)
    dxs, dyh, rg = gla_b(dxs, r1)
    dxs, (loss, small_res, _), r0 = ffn_b_last(dxs, dyh, rg)
    grad_x = dxs[first:].reshape(x.shape)
    r_last = r0[1]

    big_res = {}

    def adam_one(nm, w, m, v, entry, transposed=False):
        sums, recv = entry
        R, C = sums.shape[1:]
        w1, m1, v1 = ((t[0].T if transposed else t).reshape(1, R, C) for t in (w, m, v))
        out = _adamw(w1, m1, v1, 0, sums, q_idx, recv, None, f"adamw_{nm}", after=r_last.token)
        for kind, val in zip(("grad", "delta", "new_m", "new_v"), out):
            big_res[(kind, nm)] = val[0].T[None] if transposed else val.reshape(w.shape)
        return out[0]

    e_gla = rg.end(dxs)
    done = adam_one("gla_w_in", gla_w_in, m_gla_w_in, v_gla_w_in, e_gla[0], transposed=True)
    done = adam_one("gla_w_out", gla_w_out, m_gla_w_out, v_gla_w_out, e_gla[1])
    done = adam_one("pool_w", pool_w, m_pool_w, v_pool_w, rp.end(done)[0])
    r_last.mid(done)

    ffn_names = ["ffn_w_gate", "ffn_w_up", "ffn_w_down"]
    ffn_wmv = [tuple(t_units(t) for t in (ffn_w_gate, m_ffn_w_gate, v_ffn_w_gate)),
               tuple(t_units(t) for t in (ffn_w_up, m_ffn_w_up, v_ffn_w_up)),
               tuple(t.reshape(n_units, Fs, D) for t in (ffn_w_down, m_ffn_w_down, v_ffn_w_down))]
    ffn_prev = [[lax.empty((n_units, Fs, D), f32) for _ in range(4)] for _ in range(3)]
    order_after = r_last.token
    for u, red in ((3, r3), (2, r2), (1, r1), (0, r0)):
        entries = [r.end(done)[0] for r in red] if u == 0 else red.end(done)
        for a in range(3):
            sums, recv = entries[a]
            ffn_prev[a] = _adamw(*ffn_wmv[a], u, sums, q_idx, recv, ffn_prev[a], f"adamw_{ffn_names[a]}{u}", after=order_after)
            done = order_after = ffn_prev[a][0]
    for a in range(3):
        for kind, val in zip(("grad", "delta", "new_m", "new_v"), ffn_prev[a]):
            val = val.reshape(ffn_w_down.shape)
            big_res[(kind, ffn_names[a])] = val if a == 2 else jnp.swapaxes(val, -1, -2)

    order = ["meta", "ffn_norm", "ffn_w_gate", "ffn_w_up", "ffn_w_down", "gla_norm", "gla_w_in", "gla_w_lr", "gla_b_lr",
             "gla_head_norm", "gla_w_out", "pool_norm", "pool_w", "pool_b", "pool_scale", "final_norm"]
    res = {**small_res, **big_res}
    outs = [loss, grad_x]
    for kind in ("grad", "delta", "new_m", "new_v"):
        outs += [res[(kind, nm)] for nm in order]
    return tuple(outs)
```
